```python
import math
import jax
import jax.numpy as jnp
from jax import lax
import numpy as np

D_MODEL = 1024
BATCH = 32
SEQ = 256
DEPTH = 4
DEC_BATCH = 2
DEC_SEQ = 1024
PAST_LEN = 256

GRID_W = 64
N_MIXERS = 4
N_HYENA = (DEPTH + 3) // 4
N_GLA = (DEPTH + 2) // 4
N_FNET = (DEPTH + 1) // 4
N_POOL = DEPTH // 4

HY_ORDER = 2
HY_BANDS = 8
HY_EMB = 1 + 2 * HY_BANDS
HY_FILTER_HIDDEN = 64
HY_FAST_DECAY = 0.3
HY_SLOW_DECAY = 1.5
HY_DECAY_TARGET = 1e-2

GLA_HEADS = 4
GLA_DK = D_MODEL // 2 // GLA_HEADS
GLA_DV = D_MODEL // GLA_HEADS
GLA_GATE_RANK = 16
GLA_GATE_NORMALIZER = 16.0
GLA_CHUNK = 64

FNET_GROUPS = 4
POOL_WINDOWS = (2, 4, 8, 16)
POOL_GROUP = D_MODEL // len(POOL_WINDOWS)

MOE_GROUPS = 4
MOE_PER_GROUP = 4
MOE_EXPERTS = MOE_GROUPS * MOE_PER_GROUP
MOE_HIDDEN = D_MODEL // 2
NORM_EPS = 1e-6

kernel_name = 'hybrid_diffusion_trunk_step'


def rmsnorm(x, g):
    xf = x.astype(jnp.float32)
    y = xf * lax.rsqrt(jnp.mean(xf * xf, axis=-1, keepdims=True) + NORM_EPS)
    return (y * g.astype(jnp.float32)).astype(x.dtype)


def short_conv3(u, w, b):
    up = jnp.pad(u, ((0, 0), (1, 1), (0, 0)))
    return up[:, :-2] * w[0] + up[:, 1:-1] * w[1] + up[:, 2:] * w[2] + b


def hyena_filters(L, w1, b1, freq, w2, b2, w3):
    pos = jnp.arange(L, dtype=jnp.float32)
    bands = jnp.linspace(1e-4, HY_BANDS - 1, HY_BANDS, dtype=jnp.float32)
    ang = (2.0 * math.pi * pos / L)[:, None] * bands[None, :]
    z = jnp.concatenate([(pos / L)[:, None], jnp.cos(ang), -jnp.sin(ang)], axis=-1)
    f = jnp.sin(freq * (z @ w1 + b1))
    f = jnp.sin(freq * (f @ w2 + b2))
    f = (f @ w3).astype(jnp.float32).reshape(L, 2, HY_ORDER, D_MODEL)
    t_lin = jnp.linspace(0.0, 1.0, L, dtype=jnp.float32)[:, None]
    max_decay = math.log(HY_DECAY_TARGET) / HY_FAST_DECAY
    min_decay = math.log(HY_DECAY_TARGET) / HY_SLOW_DECAY
    deltas = jnp.linspace(min_decay, max_decay, D_MODEL, dtype=jnp.float32)
    window = jnp.exp(-t_lin * jnp.abs(deltas)[None, :])
    f = f * window[:, None, None, :]
    return f[:, 0], f[:, 1]


def two_sided_filter(kf, kb):
    return jnp.concatenate([kf, jnp.zeros_like(kf[:1]), kb[:0:-1]], axis=0)


def fft_long_conv(z, kf_hat, skip):
    L = z.shape[1]
    zh = jnp.fft.rfft(z, n=2 * L, axis=1)
    y = jnp.fft.irfft(zh * kf_hat[None], n=2 * L, axis=1)[:, :L]
    return y + z * skip


def hyena_mixer(h, w_in, b_in, conv_w, conv_b, f_w1, f_b1, f_freq, f_w2, f_b2, f_w3, skip, w_out, b_out):
    L = h.shape[1]
    u = short_conv3(h @ w_in + b_in, conv_w, conv_b).astype(jnp.float32)
    v, x1, x2 = jnp.split(u, 3, axis=-1)
    kf, kb = hyena_filters(L, f_w1, f_b1, f_freq, f_w2, f_b2, f_w3)
    k_hat = jnp.fft.rfft(two_sided_filter(kf, kb), axis=0)
    z = x1 * fft_long_conv(v, k_hat[:, 0], skip[0])
    z = x2 * fft_long_conv(z, k_hat[:, 1], skip[1])
    return z.astype(h.dtype) @ w_out + b_out


def gla_chunked(q, k, v, gk, s0):
    B, L, H, DK = q.shape
    DV = v.shape[-1]
    n = L // GLA_CHUNK

    def chunks(a):
        return a.reshape(B, n, GLA_CHUNK, H, a.shape[-1]).transpose(0, 3, 1, 2, 4)

    q, k, v, gk = chunks(q), chunks(k), chunks(v), chunks(gk)
    b = jnp.cumsum(gk, axis=3)
    b_last = b[:, :, :, -1:]
    qe = q * jnp.exp(b)
    ke = k * jnp.exp(-b)
    kd = k * jnp.exp(b_last - b)
    lower = jnp.tril(jnp.ones((GLA_CHUNK, GLA_CHUNK), dtype=bool))
    scores = jnp.where(lower, jnp.einsum('bhncd,bhnsd->bhncs', qe, ke), 0.0)
    o_intra = jnp.einsum('bhncs,bhnse->bhnce', scores, v)
    upd = jnp.einsum('bhncd,bhnce->nbhde', kd, v)
    dec = jnp.exp(b_last[:, :, :, 0]).transpose(2, 0, 1, 3)

    def step(s, inp):
        d, u = inp
        return d[..., None] * s + u, s

    s_final, s_prev = lax.scan(step, s0.astype(jnp.float32), (dec, upd))
    o_inter = jnp.einsum('bhncd,nbhde->bhnce', qe, s_prev)
    o = (o_intra + o_inter).transpose(0, 2, 3, 1, 4).reshape(B, L, H, DV)
    return o, s_final


def gla_mixer(h, s0, w_q, w_k, w_v, w_g, w_gk1, w_gk2, b_gk, norm_g, w_o):
    B, L, _ = h.shape
    f32 = jnp.float32
    q = (h @ w_q).astype(f32).reshape(B, L, GLA_HEADS, GLA_DK) * GLA_DK ** -0.5
    k = (h @ w_k).astype(f32).reshape(B, L, GLA_HEADS, GLA_DK)
    v = (h @ w_v).astype(f32).reshape(B, L, GLA_HEADS, GLA_DV)
    g = jax.nn.silu((h @ w_g).astype(f32)).reshape(B, L, GLA_HEADS, GLA_DV)
    gk_f = jax.nn.log_sigmoid(((h @ w_gk1[0]) @ w_gk2[0] + b_gk[0]).astype(f32)) / GLA_GATE_NORMALIZER
    gk_b = jax.nn.log_sigmoid(((h @ w_gk1[1]) @ w_gk2[1] + b_gk[1]).astype(f32)) / GLA_GATE_NORMALIZER
    gk_f = gk_f.reshape(B, L, GLA_HEADS, GLA_DK)
    gk_b = gk_b.reshape(B, L, GLA_HEADS, GLA_DK)
    o_f, s_f = gla_chunked(q, k, v, gk_f, s0[:, 0])
    o_b, s_b = gla_chunked(q[:, ::-1], k[:, ::-1], v[:, ::-1], gk_b[:, ::-1], s0[:, 1])
    o = o_f + o_b[:, ::-1]
    o = rmsnorm(o, norm_g) * g
    y = o.reshape(B, L, GLA_HEADS * GLA_DV).astype(h.dtype) @ w_o
    return y, jnp.stack([s_f, s_b], axis=1)


def fnet_mixer(h, w_out, b_out):
    B, L, D = h.shape
    hg = h.astype(jnp.float32).reshape(B, L, FNET_GROUPS, D // FNET_GROUPS)
    mixed = jnp.fft.fftn(hg, axes=(1, 3), norm='ortho').real
    return mixed.reshape(B, L, D).astype(h.dtype) @ w_out + b_out


def window_mean(x, axis, k):
    L = x.shape[axis]
    lo = k // 2
    hi = k - lo - 1
    pad = [(0, 0)] * x.ndim
    pad[axis] = (1, 0)
    cs = jnp.pad(jnp.cumsum(x, axis=axis), pad)
    t = jnp.arange(L)
    end = jnp.minimum(t + hi + 1, L)
    start = jnp.maximum(t - lo, 0)
    s = jnp.take(cs, end, axis=axis) - jnp.take(cs, start, axis=axis)
    shape = [1] * x.ndim
    shape[axis] = L
    return s / (end - start).astype(x.dtype).reshape(shape)


def pool_mixer(h, grid_rows, w, b, scale):
    B, L, D = h.shape
    hf = h.astype(jnp.float32)
    means = []
    for gi, k in enumerate(POOL_WINDOWS):
        hg = hf[..., gi * POOL_GROUP:(gi + 1) * POOL_GROUP]
        if grid_rows is None:
            m = window_mean(hg, 1, k)
        else:
            g2 = hg.reshape(B, grid_rows, GRID_W, POOL_GROUP)
            m = window_mean(window_mean(g2, 1, k), 2, k).reshape(B, L, POOL_GROUP)
        means.append(m)
    d = jnp.stack(means, axis=2) - hf.reshape(B, L, len(POOL_WINDOWS), POOL_GROUP)
    y = jnp.einsum('blgc,gce->blge', d, w).reshape(B, L, D) + b
    return (y * scale).astype(h.dtype)


def hier_moe(h, w_rg, w_re, w1, w3, w2):
    B, L, D = h.shape
    xt = h.reshape(B * L, D)
    pg = jax.nn.softmax((xt @ w_rg).astype(jnp.float32), axis=-1)
    p_grp, g_idx = lax.top_k(pg, 1)
    le = (xt @ w_re).astype(jnp.float32).reshape(-1, MOE_GROUPS, MOE_PER_GROUP)
    le = jnp.take_along_axis(le, g_idx[:, :, None], axis=1)[:, 0]
    pe_top, e_idx = lax.top_k(jax.nn.softmax(le, axis=-1), 2)
    pe_top = pe_top / jnp.sum(pe_top, axis=-1, keepdims=True)
    expert = g_idx * MOE_PER_GROUP + e_idx
    weight = p_grp * pe_top
    combine = jnp.sum(jax.nn.one_hot(expert, MOE_EXPERTS, dtype=jnp.float32) * weight[..., None], axis=1)
    hid = jax.nn.silu(jnp.einsum('nd,xdf->nxf', xt, w1)) * jnp.einsum('nd,xdf->nxf', xt, w3)
    y = jnp.einsum('nxf,xfd->nd', hid * combine[..., None].astype(hid.dtype), w2)
    return y.reshape(B, L, D)


def trunk(x, cond, grid_rows, gla_s0, prm):
    states = []
    for i in range(DEPTH):
        mod = (jax.nn.silu(cond) @ prm['w_ada'][i] + prm['b_ada'][i])[:, None, :]
        sh1, sc1, g1, sh2, sc2, g2 = jnp.split(mod, 6, axis=-1)
        h = rmsnorm(x, prm['norm_g'][i, 0]) * (1.0 + sc1) + sh1
        kind, j = i % N_MIXERS, i // N_MIXERS
        if kind == 0:
            y = hyena_mixer(h, prm['hy_w_in'][j], prm['hy_b_in'][j], prm['hy_conv_w'][j], prm['hy_conv_b'][j],
                            prm['hy_f_w1'][j], prm['hy_f_b1'][j], prm['hy_f_freq'][j], prm['hy_f_w2'][j],
                            prm['hy_f_b2'][j], prm['hy_f_w3'][j], prm['hy_skip'][j], prm['hy_w_out'][j],
                            prm['hy_b_out'][j])
        elif kind == 1:
            if gla_s0 is None:
                s0 = jnp.zeros((x.shape[0], 2, GLA_HEADS, GLA_DK, GLA_DV), jnp.float32)
            else:
                s0 = gla_s0[:, j]
            y, s_fin = gla_mixer(h, s0, prm['gla_w_q'][j], prm['gla_w_k'][j], prm['gla_w_v'][j], prm['gla_w_g'][j],
                                 prm['gla_w_gk1'][j], prm['gla_w_gk2'][j], prm['gla_b_gk'][j],
                                 prm['gla_norm_g'][j], prm['gla_w_o'][j])
            if gla_s0 is None:
                states.append(s_fin)
        elif kind == 2:
            y = fnet_mixer(h, prm['fn_w_out'][j], prm['fn_b_out'][j])
        else:
            y = pool_mixer(h, grid_rows, prm['pool_w'][j], prm['pool_b'][j], prm['pool_scale'][j])
        x = x + g1 * y
        h = rmsnorm(x, prm['norm_g'][i, 1]) * (1.0 + sc2) + sh2
        x = x + g2 * hier_moe(h, prm['moe_w_rg'][i], prm['moe_w_re'][i], prm['moe_w1'][i],
                              prm['moe_w3'][i], prm['moe_w2'][i])
    return rmsnorm(x, prm['final_g']), states


def setup_inputs(seed: int = 0) -> dict:
    key = jax.random.key(seed)
    ks = iter(jax.random.split(key, 64))

    def nrm(shape, scale):
        return jax.random.normal(next(ks), shape, jnp.float32) * scale

    D = D_MODEL
    return {
        'x_prompt': nrm((BATCH, SEQ, D), 1.0),
        'x_sample': nrm((DEC_BATCH, DEC_SEQ, D), 1.0),
        'state_gla': nrm((DEC_BATCH, N_GLA, 2, GLA_HEADS, GLA_DK, GLA_DV), 1.0),
        'c': nrm((DEC_BATCH, D), 1.0),
        'c_ctx': nrm((D,), 0.5),
        'w_ada': nrm((DEPTH, D, 6 * D), 0.5 * D ** -0.5),
        'b_ada': nrm((DEPTH, 6 * D), 0.02),
        'norm_g': 1.0 + nrm((DEPTH, 2, D), 0.02),
        'hy_w_in': nrm((N_HYENA, D, 3 * D), D ** -0.5),
        'hy_b_in': nrm((N_HYENA, 3 * D), 0.02),
        'hy_conv_w': nrm((N_HYENA, 3, 3 * D), 0.5),
        'hy_conv_b': nrm((N_HYENA, 3 * D), 0.02),
        'hy_f_w1': nrm((N_HYENA, HY_EMB, HY_FILTER_HIDDEN), HY_EMB ** -0.5),
        'hy_f_b1': nrm((N_HYENA, HY_FILTER_HIDDEN), 0.02),
        'hy_f_freq': 1.0 + nrm((N_HYENA, HY_FILTER_HIDDEN), 0.02),
        'hy_f_w2': nrm((N_HYENA, HY_FILTER_HIDDEN, HY_FILTER_HIDDEN), HY_FILTER_HIDDEN ** -0.5),
        'hy_f_b2': nrm((N_HYENA, HY_FILTER_HIDDEN), 0.02),
        'hy_f_w3': nrm((N_HYENA, HY_FILTER_HIDDEN, 2 * HY_ORDER * D), 0.005),
        'hy_skip': nrm((N_HYENA, HY_ORDER, D), 0.5),
        'hy_w_out': nrm((N_HYENA, D, D), D ** -0.5),
        'hy_b_out': nrm((N_HYENA, D), 0.02),
        'gla_w_q': nrm((N_GLA, D, GLA_HEADS * GLA_DK), D ** -0.5),
        'gla_w_k': nrm((N_GLA, D, GLA_HEADS * GLA_DK), D ** -0.5),
        'gla_w_v': nrm((N_GLA, D, GLA_HEADS * GLA_DV), D ** -0.5),
        'gla_w_g': nrm((N_GLA, D, GLA_HEADS * GLA_DV), D ** -0.5),
        'gla_w_gk1': nrm((N_GLA, 2, D, GLA_GATE_RANK), D ** -0.5),
        'gla_w_gk2': nrm((N_GLA, 2, GLA_GATE_RANK, GLA_HEADS * GLA_DK), GLA_GATE_RANK ** -0.5),
        'gla_b_gk': nrm((N_GLA, 2, GLA_HEADS * GLA_DK), 0.02),
        'gla_norm_g': 1.0 + nrm((N_GLA, GLA_DV), 0.02),
        'gla_w_o': nrm((N_GLA, GLA_HEADS * GLA_DV, D), (GLA_HEADS * GLA_DV) ** -0.5),
        'fn_w_out': nrm((N_FNET, D, D), D ** -0.5),
        'fn_b_out': nrm((N_FNET, D), 0.02),
        'pool_w': nrm((N_POOL, len(POOL_WINDOWS), POOL_GROUP, POOL_GROUP), POOL_GROUP ** -0.5),
        'pool_b': nrm((N_POOL, D), 0.02),
        'pool_scale': 0.5 + nrm((N_POOL, D), 0.05),
        'moe_w_rg': nrm((DEPTH, D, MOE_GROUPS), D ** -0.5),
        'moe_w_re': nrm((DEPTH, D, MOE_EXPERTS), D ** -0.5),
        'moe_w1': nrm((DEPTH, MOE_EXPERTS, D, MOE_HIDDEN), D ** -0.5),
        'moe_w3': nrm((DEPTH, MOE_EXPERTS, D, MOE_HIDDEN), D ** -0.5),
        'moe_w2': nrm((DEPTH, MOE_EXPERTS, MOE_HIDDEN, D), MOE_HIDDEN ** -0.5),
        'final_g': 1.0 + nrm((D,), 0.02),
    }


def reference(x_prompt, x_sample, state_gla, c, c_ctx, w_ada, b_ada, norm_g,
              hy_w_in, hy_b_in, hy_conv_w, hy_conv_b, hy_f_w1, hy_f_b1, hy_f_freq, hy_f_w2, hy_f_b2,
              hy_f_w3, hy_skip, hy_w_out, hy_b_out,
              gla_w_q, gla_w_k, gla_w_v, gla_w_g, gla_w_gk1, gla_w_gk2, gla_b_gk, gla_norm_g, gla_w_o,
              fn_w_out, fn_b_out, pool_w, pool_b, pool_scale,
              moe_w_rg, moe_w_re, moe_w1, moe_w3, moe_w2, final_g):
    prm = {
        'w_ada': w_ada, 'b_ada': b_ada, 'norm_g': norm_g,
        'hy_w_in': hy_w_in, 'hy_b_in': hy_b_in, 'hy_conv_w': hy_conv_w, 'hy_conv_b': hy_conv_b,
        'hy_f_w1': hy_f_w1, 'hy_f_b1': hy_f_b1, 'hy_f_freq': hy_f_freq, 'hy_f_w2': hy_f_w2,
        'hy_f_b2': hy_f_b2, 'hy_f_w3': hy_f_w3, 'hy_skip': hy_skip, 'hy_w_out': hy_w_out, 'hy_b_out': hy_b_out,
        'gla_w_q': gla_w_q, 'gla_w_k': gla_w_k, 'gla_w_v': gla_w_v, 'gla_w_g': gla_w_g,
        'gla_w_gk1': gla_w_gk1, 'gla_w_gk2': gla_w_gk2, 'gla_b_gk': gla_b_gk, 'gla_norm_g': gla_norm_g,
        'gla_w_o': gla_w_o, 'fn_w_out': fn_w_out, 'fn_b_out': fn_b_out,
        'pool_w': pool_w, 'pool_b': pool_b, 'pool_scale': pool_scale,
        'moe_w_rg': moe_w_rg, 'moe_w_re': moe_w_re, 'moe_w1': moe_w1, 'moe_w3': moe_w3, 'moe_w2': moe_w2,
        'final_g': final_g,
    }
    y_prompt, ctx_states = trunk(x_prompt, c_ctx[None, :], None, None, prm)
    new_state_gla = jnp.stack(ctx_states, axis=1)
    rows = x_sample.shape[1] // GRID_W
    y_sample, _ = trunk(x_sample, c, rows, state_gla, prm)
    return (y_prompt, y_sample, new_state_gla)
```

```python
import functools
import math

import jax
import jax.numpy as jnp
import numpy as np
from jax import lax
from jax.experimental import pallas as pl
from jax.experimental.pallas import tpu as pltpu

F32 = jnp.float32
BF16 = jnp.bfloat16

D = 1024
CTX_B, CTX_L = 32, 256
LAT_B, LAT_L = 2, 1024
N_CTX = CTX_B * CTX_L
N_LAT = LAT_B * LAT_L
N_TOK = N_CTX + N_LAT
DEPTH = 4
GRID_W = 64
EPS = 1e-6

HY_BANDS = 8
HY_EMB = 1 + 2 * HY_BANDS
HY_EMB_PAD = 32
HY_HID = 64
HY_FAST_DECAY = 0.3
HY_SLOW_DECAY = 1.5
HY_DECAY_TARGET = 1e-2

GLA_H = 4
GLA_DK = 128
GLA_DV = 256
GLA_RANK = 16
GLA_NORMALIZER = 16.0
GLA_CHUNK = 64

FNET_GROUPS = 4
FNET_C = D // FNET_GROUPS
POOL_WINDOWS = (2, 4, 8, 16)
POOL_G = D // len(POOL_WINDOWS)

MOE_GROUPS = 4
MOE_PER_GROUP = 4
MOE_E = MOE_GROUPS * MOE_PER_GROUP
MOE_HID = D // 2

MOD_ROWS = 8
TM = 512
TM_BIG = 1024
VMEM_LIMIT = 56 * 1024 * 1024


def _cparams(n_axes):
    return pltpu.CompilerParams(dimension_semantics=("arbitrary",) * n_axes, vmem_limit_bytes=VMEM_LIMIT)


def _norm_mod(x, g, sc, sh):
    ms = jnp.mean(x * x, axis=-1, keepdims=True)
    return (x * lax.rsqrt(ms + EPS) * g) * (1.0 + sc) + sh


def _split(a):
    hi = a.astype(BF16)
    lo = (a - hi.astype(F32)).astype(BF16)
    return hi, lo


def _dot(a, b):
    return jnp.dot(a, b, preferred_element_type=F32)


def _dot_precise(a, b):
    a_hi, a_lo = _split(a)
    b_hi, b_lo = _split(b)
    return _dot(a_hi, b_hi) + (_dot(a_hi, b_lo) + _dot(a_lo, b_hi))


def _silu(x):
    return x * (1.0 / (1.0 + jnp.exp(-x)))


def _log_sigmoid(x):
    return jnp.minimum(x, 0.0) - jnp.log(1.0 + jnp.exp(-jnp.abs(x)))


def _mod_spec(layer, chunk, row_fn):
    base = layer * MOD_ROWS * 6 + chunk

    def index_map(*ids):
        return (base + row_fn(*ids) * 6, 0, 0)

    return pl.BlockSpec((None, 1, D), index_map)


def _row_tm(t, *_):
    return jnp.where(t < N_CTX // TM, 0, 1 + (t - N_CTX // TM) // (LAT_L // TM))


def _row_big(t, *_):
    return jnp.where(t < N_CTX // TM_BIG, 0, 1 + (t - N_CTX // TM_BIG) // (LAT_L // TM_BIG))


def _vec_spec(n):
    return pl.BlockSpec((1, n), lambda *ids: (0, 0))


def _full_spec(shape):
    nd = len(shape)
    return pl.BlockSpec(shape, lambda *ids: (0,) * nd)


def _ada_kernel(cond_ref, w_ref, b_ref, o_ref):
    s = _silu(cond_ref[...]).astype(BF16)
    o_ref[...] = _dot(s, w_ref[...].astype(BF16)) + b_ref[...]


def _ada_table(cond, w_ada, b_ada):
    tn = 1536
    return pl.pallas_call(
        _ada_kernel,
        grid=(DEPTH, 6 * D // tn),
        in_specs=[
            pl.BlockSpec((MOD_ROWS, D), lambda i, j: (0, 0)),
            pl.BlockSpec((None, D, tn), lambda i, j: (i, 0, j)),
            pl.BlockSpec((None, 1, tn), lambda i, j: (i, 0, j)),
        ],
        out_specs=pl.BlockSpec((None, MOD_ROWS, tn), lambda i, j: (i, 0, j)),
        out_shape=jax.ShapeDtypeStruct((DEPTH, MOD_ROWS, 6 * D), F32),
        compiler_params=_cparams(2),
        name="ada_table",
    )(cond, w_ada, b_ada.reshape(DEPTH, 1, 6 * D))


def _outproj_kernel(z_ref, w_ref, b_ref, gate_ref, x_ref, o_ref):
    y = _dot(z_ref[...], w_ref[...]) + b_ref[...]
    o_ref[...] = x_ref[...] + gate_ref[...] * y


def _outproj(x, z, w_bf16, bias, mod, layer, row_off, n_rows):
    k = z.shape[1]
    off = row_off // TM
    first_lat = N_CTX // TM

    def row_fn(t):
        g = t + off
        return jnp.where(g < first_lat, 0, 1 + (g - first_lat) // (LAT_L // TM))

    return pl.pallas_call(
        _outproj_kernel,
        grid=(n_rows // TM,),
        in_specs=[
            pl.BlockSpec((TM, k), lambda t: (t, 0)),
            _full_spec((k, D)),
            _vec_spec(D),
            _mod_spec(layer, 2, row_fn),
            pl.BlockSpec((TM, D), lambda t: (t + off, 0)),
        ],
        out_specs=pl.BlockSpec((TM, D), lambda t: (t + off, 0)),
        out_shape=jax.ShapeDtypeStruct((N_TOK, D), F32),
        input_output_aliases={4: 0},
        compiler_params=_cparams(1),
        name="outproj_residual",
    )(z, w_bf16, bias.reshape(1, D), mod, x)


def _dft_mats(L):
    n2 = 2 * L
    k = np.arange(L)[:, None].astype(np.float64)
    n = np.arange(n2)[None, :].astype(np.float64)
    ang = 2.0 * np.pi * k * n / n2
    full = np.concatenate([np.cos(ang), -np.sin(ang)], axis=0)
    full[L, :] = np.cos(np.pi * np.arange(n2))
    fwd = full[:, :L]
    bwd = np.zeros((n2, L))
    bwd[:, 1:] = full[:, n2 - np.arange(1, L)]
    t = np.arange(L)[:, None].astype(np.float64)
    kk = np.arange(L)[None, :].astype(np.float64)
    ang_i = 2.0 * np.pi * t * kk / n2
    inv_re = np.cos(ang_i) / L
    inv_re[:, 0] = 1.0 / n2
    inv_im = -np.sin(ang_i) / L
    inv_im[:, 0] = np.cos(np.pi * np.arange(L)) / n2
    inv = np.concatenate([inv_re, inv_im], axis=1)
    return tuple(jnp.asarray(m, F32).astype(BF16) for m in (fwd, np.concatenate([fwd, bwd], axis=1), inv))


def _hyena_pos_emb(L):
    pos = np.arange(L, dtype=np.float64)
    bands = np.linspace(1e-4, HY_BANDS - 1, HY_BANDS)
    ang = (2.0 * np.pi * pos / L)[:, None] * bands[None, :]
    z = np.concatenate([(pos / L)[:, None], np.cos(ang), -np.sin(ang)], axis=-1)
    zp = np.zeros((L, HY_EMB_PAD))
    zp[:, :HY_EMB] = z
    return jnp.asarray(zp, F32)


def _hyena_filter_kernel(z_ref, w1_ref, b1_ref, fr_ref, w2_ref, b2_ref, w3f_ref, w3b_ref, ff_ref, o_ref, *, L, tn):
    j = pl.program_id(1)
    fr = fr_ref[...]
    f = jnp.sin(fr * (_dot_precise(z_ref[...], w1_ref[...]) + b1_ref[...]))
    f = jnp.sin(fr * (_dot_precise(f, w2_ref[...]) + b2_ref[...]))
    t_lin = lax.broadcasted_iota(jnp.int32, (L, tn), 0).astype(F32) / float(L - 1)
    ch = (lax.broadcasted_iota(jnp.int32, (L, tn), 1) + j * tn).astype(F32)
    max_decay = math.log(HY_DECAY_TARGET) / HY_FAST_DECAY
    min_decay = math.log(HY_DECAY_TARGET) / HY_SLOW_DECAY
    deltas = min_decay + ch * ((max_decay - min_decay) / float(D - 1))
    window = jnp.exp(-t_lin * jnp.abs(deltas))
    kf = _dot_precise(f, w3f_ref[...]) * window
    kb = _dot_precise(f, w3b_ref[...]) * window
    taps = jnp.concatenate([kf, kb], axis=0).astype(BF16)
    o_ref[...] = _dot(ff_ref[...], taps)


def _hyena_filters(L, ff, f_w1, f_b1, f_freq, f_w2, f_b2, f_w3):
    tn = 512
    nj = D // tn
    w1p = jnp.zeros((HY_EMB_PAD, HY_HID), F32).at[:HY_EMB].set(f_w1)
    kern = functools.partial(_hyena_filter_kernel, L=L, tn=tn)
    return pl.pallas_call(
        kern,
        grid=(2, nj),
        in_specs=[
            _full_spec((L, HY_EMB_PAD)),
            _full_spec((HY_EMB_PAD, HY_HID)),
            _vec_spec(HY_HID),
            _vec_spec(HY_HID),
            _full_spec((HY_HID, HY_HID)),
            _vec_spec(HY_HID),
            pl.BlockSpec((HY_HID, tn), lambda o, j: (0, o * nj + j)),
            pl.BlockSpec((HY_HID, tn), lambda o, j: (0, (2 + o) * nj + j)),
            _full_spec((2 * L, 2 * L)),
        ],
        out_specs=pl.BlockSpec((None, 2 * L, tn), lambda o, j: (o, 0, j)),
        out_shape=jax.ShapeDtypeStruct((2, 2 * L, D), F32),
        compiler_params=_cparams(2),
        name=f"hyena_filters_L{L}",
    )(_hyena_pos_emb(L), w1p, f_b1.reshape(1, -1), f_freq.reshape(1, -1), f_w2, f_b2.reshape(1, -1),
      f_w3, f_w3, ff)


def _hyena_in_kernel(x_ref, g_ref, sc_ref, sh_ref, w_ref, b_ref, cw_ref, cb_ref, o_ref, h_scr):
    t = pl.program_id(0)

    @pl.when(pl.program_id(1) == 0)
    def _():
        h_scr[...] = _norm_mod(x_ref[...], g_ref[...], sc_ref[...], sh_ref[...]).astype(BF16)

    u = _dot(h_scr[...], w_ref[...]) + b_ref[...]
    seq = jnp.where(t < N_CTX // TM_BIG, CTX_L, LAT_L)
    pos = lax.broadcasted_iota(jnp.int32, u.shape, 0) & (seq - 1)
    prev = jnp.where(pos == 0, 0.0, pltpu.roll(u, 1, 0))
    nxt = jnp.where(pos == seq - 1, 0.0, pltpu.roll(u, TM_BIG - 1, 0))
    cw = cw_ref[...]
    o_ref[...] = prev * cw[0:1] + u * cw[1:2] + nxt * cw[2:3] + cb_ref[...]


def _hyena_in(x, mod, layer, norm_g, w_in_bf16, b_in, conv_w, conv_b):
    return pl.pallas_call(
        _hyena_in_kernel,
        grid=(N_TOK // TM_BIG, 3),
        in_specs=[
            pl.BlockSpec((TM_BIG, D), lambda t, p: (t, 0)),
            _vec_spec(D),
            _mod_spec(layer, 1, _row_big),
            _mod_spec(layer, 0, _row_big),
            pl.BlockSpec((D, D), lambda t, p: (0, p)),
            pl.BlockSpec((1, D), lambda t, p: (0, p)),
            pl.BlockSpec((3, D), lambda t, p: (0, p)),
            pl.BlockSpec((1, D), lambda t, p: (0, p)),
        ],
        out_specs=pl.BlockSpec((TM_BIG, D), lambda t, p: (t, p)),
        out_shape=jax.ShapeDtypeStruct((N_TOK, 3 * D), F32),
        scratch_shapes=[pltpu.VMEM((TM_BIG, D), BF16)],
        compiler_params=_cparams(2),
        name="hyena_in",
    )(x, norm_g.reshape(1, D), mod, mod, w_in_bf16, b_in.reshape(1, -1), conv_w, conv_b.reshape(1, -1))


def _hyena_conv_kernel(v_ref, x1_ref, x2_ref, kh_ref, skip_ref, fwd_ref, inv_ref, o_ref, *, L):
    fwd = fwd_ref[...]
    inv = inv_ref[...]
    row0 = lax.broadcasted_iota(jnp.int32, (L, v_ref.shape[1]), 0) == 0

    def long_conv(z, order):
        zh = _dot(fwd, z.astype(BF16))
        zr, zi = zh[:L], zh[L:]
        kr, ki = kh_ref[order, :L, :], kh_ref[order, L:, :]
        pr = jnp.where(row0, zr * kr, zr * kr - zi * ki)
        pi = jnp.where(row0, zi * ki, zr * ki + zi * kr)
        prod = jnp.concatenate([pr, pi], axis=0).astype(BF16)
        return _dot(inv, prod) + z * skip_ref[order:order + 1, :]

    z = x1_ref[...] * long_conv(v_ref[...], 0)
    z = x2_ref[...] * long_conv(z, 1)
    o_ref[...] = z.astype(BF16)


def _hyena_conv(u, khat, skip, fwd, inv, L, n_batch, row_off, tn):
    nj = D // tn
    rb = row_off // L
    kern = functools.partial(_hyena_conv_kernel, L=L)
    return pl.pallas_call(
        kern,
        grid=(nj, n_batch),
        in_specs=[
            pl.BlockSpec((L, tn), lambda j, b: (rb + b, j)),
            pl.BlockSpec((L, tn), lambda j, b: (rb + b, nj + j)),
            pl.BlockSpec((L, tn), lambda j, b: (rb + b, 2 * nj + j)),
            pl.BlockSpec((2, 2 * L, tn), lambda j, b: (0, 0, j)),
            pl.BlockSpec((2, tn), lambda j, b: (0, j)),
            _full_spec((2 * L, L)),
            _full_spec((L, 2 * L)),
        ],
        out_specs=pl.BlockSpec((L, tn), lambda j, b: (b, j)),
        out_shape=jax.ShapeDtypeStruct((n_batch * L, D), BF16),
        compiler_params=_cparams(2),
        name=f"hyena_conv_L{L}",
    )(u, u, u, khat, skip, fwd, inv)


GLA_PROJ = 2 * GLA_H * GLA_DK + 2 * GLA_H * GLA_DV
GLA_COLS = GLA_PROJ + 2 * GLA_H * GLA_DK
GK1_PAD = 128


def _gla_proj_kernel(x_ref, g_ref, sc_ref, sh_ref, w_ref, wg1_ref, wg2_ref, bg_ref, o_ref):
    h = _norm_mod(x_ref[...], g_ref[...], sc_ref[...], sh_ref[...]).astype(BF16)
    p = _dot(h, w_ref[...])
    nq = GLA_H * GLA_DK
    o_ref[:, 0:nq] = p[:, 0:nq] * (GLA_DK ** -0.5)
    o_ref[:, nq:nq + nq + GLA_H * GLA_DV] = p[:, nq:nq + nq + GLA_H * GLA_DV]
    o_ref[:, 2 * nq + GLA_H * GLA_DV:GLA_PROJ] = _silu(p[:, 2 * nq + GLA_H * GLA_DV:GLA_PROJ])
    low = _dot(h, wg1_ref[...]).astype(BF16)
    gk = _dot(low, wg2_ref[...]) + bg_ref[...]
    o_ref[:, GLA_PROJ:GLA_COLS] = _log_sigmoid(gk) / GLA_NORMALIZER


def _gla_proj(x, mod, layer, norm_g, w_cat, wg1, wg2, bg):
    return pl.pallas_call(
        _gla_proj_kernel,
        grid=(N_TOK // TM,),
        in_specs=[
            pl.BlockSpec((TM, D), lambda t: (t, 0)),
            _vec_spec(D),
            _mod_spec(layer, 1, _row_tm),
            _mod_spec(layer, 0, _row_tm),
            _full_spec((D, GLA_PROJ)),
            _full_spec((D, GK1_PAD)),
            _full_spec((GK1_PAD, 2 * GLA_H * GLA_DK)),
            _vec_spec(2 * GLA_H * GLA_DK),
        ],
        out_specs=pl.BlockSpec((TM, GLA_COLS), lambda t: (t, 0)),
        out_shape=jax.ShapeDtypeStruct((N_TOK, GLA_COLS), F32),
        compiler_params=_cparams(1),
        name="gla_proj",
    )(x, norm_g.reshape(1, D), mod, mod, w_cat, wg1, wg2, bg)


def _gla_core_kernel(*refs, L, has_s0):
    if has_s0:
        q_ref, k_ref, v_ref, g_ref, gkf_ref, gkb_ref, tri_ref, ng_ref, s0_ref, o_ref, sf_ref, acc = refs
    else:
        q_ref, k_ref, v_ref, g_ref, gkf_ref, gkb_ref, tri_ref, ng_ref, o_ref, sf_ref, acc = refs
        s0_ref = None
    C = GLA_CHUNK
    n = L // C
    ones = jnp.ones((C, GLA_DK), BF16)
    ri = lax.broadcasted_iota(jnp.int32, (C, C), 0)
    ci = lax.broadcasted_iota(jnp.int32, (C, C), 1)
    tn_dims = (((0,), (0,)), ((), ()))

    for direction, gk_ref in enumerate((gkf_ref, gkb_ref)):
        tri = tri_ref[direction]
        keep = (ci <= ri) if direction == 0 else (ci >= ri)
        last = C - 1 if direction == 0 else 0
        s = s0_ref[direction] if has_s0 else jnp.zeros((GLA_DK, GLA_DV), F32)
        order = range(n) if direction == 0 else range(n - 1, -1, -1)
        for c in order:
            rows = slice(c * C, (c + 1) * C)
            gk_hi, gk_lo = _split(gk_ref[rows, :])
            b = _dot(tri, gk_hi) + _dot(tri, gk_lo)
            b_last = b[last:last + 1, :]
            q = q_ref[rows, :]
            k = k_ref[rows, :]
            v = v_ref[rows, :].astype(BF16)
            qe = (q * jnp.exp(b)).astype(BF16)
            ke = (k * jnp.exp(-b)).astype(BF16)
            kd = (k * jnp.exp(b_last - b)).astype(BF16)
            scores = lax.dot_general(qe, ke, (((1,), (1,)), ((), ())), preferred_element_type=F32)
            scores = jnp.where(keep, scores, 0.0).astype(BF16)
            o = _dot(scores, v) + _dot(qe, s.astype(BF16))
            if direction == 0:
                acc[rows, :] = o
            else:
                acc[rows, :] = acc[rows, :] + o
            tot = (lax.dot_general(gk_hi, ones, tn_dims, preferred_element_type=F32)
                   + lax.dot_general(gk_lo, ones, tn_dims, preferred_element_type=F32))
            dec = jnp.exp(tot)
            dec = jnp.concatenate([dec, dec], axis=1)
            s = dec * s + lax.dot_general(kd, v, tn_dims, preferred_element_type=F32)
        sf_ref[direction] = s

    o = acc[...]
    o = o * lax.rsqrt(jnp.mean(o * o, axis=-1, keepdims=True) + EPS) * ng_ref[...]
    o_ref[...] = (o * g_ref[...]).astype(BF16)


def _gla_core(proj, tri, norm_g, s0, L, n_batch, row_off):
    rb = row_off // L
    H = GLA_H
    has_s0 = s0 is not None
    kern = functools.partial(_gla_core_kernel, L=L, has_s0=has_s0)
    kb, vb = GLA_DK, GLA_DV
    in_specs = [
        pl.BlockSpec((L, kb), lambda b, h: (rb + b, h)),
        pl.BlockSpec((L, kb), lambda b, h: (rb + b, H + h)),
        pl.BlockSpec((L, vb), lambda b, h: (rb + b, (2 * H * kb) // vb + h)),
        pl.BlockSpec((L, vb), lambda b, h: (rb + b, (2 * H * kb) // vb + H + h)),
        pl.BlockSpec((L, kb), lambda b, h: (rb + b, GLA_PROJ // kb + h)),
        pl.BlockSpec((L, kb), lambda b, h: (rb + b, GLA_PROJ // kb + H + h)),
        _full_spec((2, GLA_CHUNK, GLA_CHUNK)),
        _vec_spec(vb),
    ]
    args = [proj] * 6 + [tri, norm_g.reshape(1, vb)]
    if has_s0:
        in_specs.append(pl.BlockSpec((None, 2, None, kb, vb), lambda b, h: (b, 0, h, 0, 0)))
        args.append(s0)
    return pl.pallas_call(
        kern,
        grid=(n_batch, H),
        in_specs=in_specs,
        out_specs=[
            pl.BlockSpec((L, vb), lambda b, h: (b, h)),
            pl.BlockSpec((None, 2, None, kb, vb), lambda b, h: (b, 0, h, 0, 0)),
        ],
        out_shape=[
            jax.ShapeDtypeStruct((n_batch * L, H * vb), BF16),
            jax.ShapeDtypeStruct((n_batch, 2, H, kb, vb), F32),
        ],
        scratch_shapes=[pltpu.VMEM((L, vb), F32)],
        compiler_params=_cparams(2),
        name=f"gla_core_L{L}",
    )(*args)


def _fnet_mats(L):
    c = np.arange(FNET_C)
    ang_c = 2.0 * np.pi * np.outer(c, c) / FNET_C
    chan = np.concatenate([np.cos(ang_c), np.sin(ang_c)], axis=1) / math.sqrt(FNET_C)
    t = np.arange(L)
    ang_l = 2.0 * np.pi * np.outer(t, t) / L
    seq = np.concatenate([np.cos(ang_l), -np.sin(ang_l)], axis=1) / math.sqrt(L)
    return jnp.asarray(chan, F32).astype(BF16), jnp.asarray(seq, F32).astype(BF16)


def _fnet_kernel(x_ref, g_ref, sc_ref, sh_ref, gate_ref, chan_ref, seq_ref, w_ref, b_ref, o_ref):
    x = x_ref[...]
    h = _norm_mod(x, g_ref[...], sc_ref[...], sh_ref[...]).astype(BF16)
    chan = chan_ref[...]
    cos_parts, sin_parts = [], []
    for gi in range(FNET_GROUPS):
        cs = _dot(h[:, gi * FNET_C:(gi + 1) * FNET_C], chan)
        cos_parts.append(cs[:, :FNET_C])
        sin_parts.append(cs[:, FNET_C:])
    stacked = jnp.concatenate([jnp.concatenate(cos_parts, axis=1), jnp.concatenate(sin_parts, axis=1)], axis=0)
    mixed = _dot(seq_ref[...], stacked.astype(BF16))
    y = _dot(mixed.astype(BF16), w_ref[...]) + b_ref[...]
    o_ref[...] = x + gate_ref[...] * y


def _fnet(x, mod, layer, norm_g, chan, seq, w_bf16, bias, L, n_batch, row_off):
    rb = row_off // L
    lat = row_off > 0

    def row_fn(b):
        return 1 + b if lat else 0

    return pl.pallas_call(
        _fnet_kernel,
        grid=(n_batch,),
        in_specs=[
            pl.BlockSpec((L, D), lambda b: (rb + b, 0)),
            _vec_spec(D),
            _mod_spec(layer, 1, row_fn),
            _mod_spec(layer, 0, row_fn),
            _mod_spec(layer, 2, row_fn),
            _full_spec((FNET_C, 2 * FNET_C)),
            _full_spec((L, 2 * L)),
            _full_spec((D, D)),
            _vec_spec(D),
        ],
        out_specs=pl.BlockSpec((L, D), lambda b: (rb + b, 0)),
        out_shape=jax.ShapeDtypeStruct((N_TOK, D), F32),
        input_output_aliases={0: 0},
        compiler_params=_cparams(1),
        name=f"fnet_L{L}",
    )(x, norm_g.reshape(1, D), mod, mod, mod, chan, seq, w_bf16, bias.reshape(1, D))


def _window_bounds(n, k):
    t = np.arange(n)
    lo, hi = k // 2, k - k // 2 - 1
    return np.maximum(t - lo, 0), np.minimum(t + hi + 1, n)


def _pool_mats(L, grid_rows):
    mats, inv = [], []
    for k in POOL_WINDOWS:
        if grid_rows is None:
            s, e = _window_bounds(L, k)
            idx = np.arange(L)[None, :]
            m = ((idx >= s[:, None]) & (idx < e[:, None])).astype(np.float64)
            cnt = (e - s).astype(np.float64)
        else:
            sr, er = _window_bounds(grid_rows, k)
            sc, ec = _window_bounds(GRID_W, k)
            ir = np.arange(grid_rows)[None, :]
            ic = np.arange(GRID_W)[None, :]
            mr = ((ir >= sr[:, None]) & (ir < er[:, None])).astype(np.float64)
            mc = ((ic >= sc[:, None]) & (ic < ec[:, None])).astype(np.float64)
            m = np.kron(mr, mc)
            cnt = np.kron((er - sr).astype(np.float64), (ec - sc).astype(np.float64))
        mats.append(m)
        inv.append(1.0 / cnt)
    return jnp.asarray(np.stack(mats), BF16), jnp.asarray(np.stack(inv)[:, :, None], F32)


def _pool_kernel(x_ref, g_ref, sc_ref, sh_ref, gate_ref, m_ref, ic_ref, w_ref, b_ref, ps_ref, o_ref):
    x = x_ref[...]
    h = _norm_mod(x, g_ref[...], sc_ref[...], sh_ref[...])
    outs = []
    for gi in range(len(POOL_WINDOWS)):
        hg = h[:, gi * POOL_G:(gi + 1) * POOL_G]
        hi, lo = _split(hg)
        m = m_ref[gi]
        mean = (_dot(m, hi) + _dot(m, lo)) * ic_ref[gi]
        outs.append(_dot((mean - hg).astype(BF16), w_ref[gi]))
    y = (jnp.concatenate(outs, axis=1) + b_ref[...]) * ps_ref[...]
    o_ref[...] = x + gate_ref[...] * y


def _pool(x, mod, layer, norm_g, mats, inv_cnt, w_bf16, bias, scale, L, n_batch, row_off):
    rb = row_off // L
    lat = row_off > 0
    G = len(POOL_WINDOWS)

    def row_fn(b):
        return 1 + b if lat else 0

    return pl.pallas_call(
        _pool_kernel,
        grid=(n_batch,),
        in_specs=[
            pl.BlockSpec((L, D), lambda b: (rb + b, 0)),
            _vec_spec(D),
            _mod_spec(layer, 1, row_fn),
            _mod_spec(layer, 0, row_fn),
            _mod_spec(layer, 2, row_fn),
            _full_spec((G, L, L)),
            _full_spec((G, L, 1)),
            _full_spec((G, POOL_G, POOL_G)),
            _vec_spec(D),
            _vec_spec(D),
        ],
        out_specs=pl.BlockSpec((L, D), lambda b: (rb + b, 0)),
        out_shape=jax.ShapeDtypeStruct((N_TOK, D), F32),
        input_output_aliases={0: 0},
        compiler_params=_cparams(1),
        name=f"pool_L{L}",
    )(x, norm_g.reshape(1, D), mod, mod, mod, mats, inv_cnt, w_bf16, bias.reshape(1, D), scale.reshape(1, D))


ROUTER_PAD = 128


def _moe_route_kernel(x_ref, g_ref, sc_ref, sh_ref, wr_ref, h_ref, comb_ref):
    h = _norm_mod(x_ref[...], g_ref[...], sc_ref[...], sh_ref[...])
    h_ref[...] = h.astype(BF16)
    logits = _dot_precise(h, wr_ref[...])
    lane = lax.broadcasted_iota(jnp.int32, logits.shape, 1)
    neg = jnp.float32(-jnp.inf)
    big = jnp.int32(ROUTER_PAD)
    is_grp = (lane >= MOE_E) & (lane < MOE_E + MOE_GROUPS)
    gl = jnp.where(is_grp, logits, neg)
    g_max = jnp.max(gl, axis=-1, keepdims=True)
    g_idx = jnp.min(jnp.where(gl == g_max, lane, big), axis=-1, keepdims=True) - MOE_E
    p_grp = 1.0 / jnp.sum(jnp.where(is_grp, jnp.exp(gl - g_max), 0.0), axis=-1, keepdims=True)
    in_grp = (lane < MOE_E) & ((lane >> 2) == g_idx)
    el = jnp.where(in_grp, logits, neg)
    m1 = jnp.max(el, axis=-1, keepdims=True)
    i1 = jnp.min(jnp.where(el == m1, lane, big), axis=-1, keepdims=True)
    z = jnp.sum(jnp.where(in_grp, jnp.exp(el - m1), 0.0), axis=-1, keepdims=True)
    el2 = jnp.where(lane == i1, neg, el)
    m2 = jnp.max(el2, axis=-1, keepdims=True)
    i2 = jnp.min(jnp.where(el2 == m2, lane, big), axis=-1, keepdims=True)
    p1 = 1.0 / z
    p2 = jnp.exp(m2 - m1) / z
    tot = p1 + p2
    comb = jnp.where(lane == i1, p_grp * (p1 / tot), 0.0) + jnp.where(lane == i2, p_grp * (p2 / tot), 0.0)
    comb_ref[...] = comb


def _moe_route(x, mod, layer, norm_g, w_router):
    return pl.pallas_call(
        _moe_route_kernel,
        grid=(N_TOK // TM,),
        in_specs=[
            pl.BlockSpec((TM, D), lambda t: (t, 0)),
            _vec_spec(D),
            _mod_spec(layer, 4, _row_tm),
            _mod_spec(layer, 3, _row_tm),
            _full_spec((D, ROUTER_PAD)),
        ],
        out_specs=[
            pl.BlockSpec((TM, D), lambda t: (t, 0)),
            pl.BlockSpec((TM, ROUTER_PAD), lambda t: (t, 0)),
        ],
        out_shape=[
            jax.ShapeDtypeStruct((N_TOK, D), BF16),
            jax.ShapeDtypeStruct((N_TOK, ROUTER_PAD), F32),
        ],
        compiler_params=_cparams(1),
        name="moe_route",
    )(x, norm_g.reshape(1, D), mod, mod, w_router)


def _moe_dense_kernel(h_ref, comb_ref, w1_ref, w3_ref, w2_ref, gate_ref, x_ref, o_ref, acc):
    e = pl.program_id(1)

    @pl.when(e == 0)
    def _():
        acc[...] = jnp.zeros_like(acc)

    h = h_ref[...]
    a = _dot(h, w1_ref[...].astype(BF16))
    b = _dot(h, w3_ref[...].astype(BF16))
    comb = comb_ref[...]
    lane = lax.broadcasted_iota(jnp.int32, comb.shape, 1)
    cw = jnp.sum(jnp.where(lane == e, comb, 0.0), axis=-1, keepdims=True)
    hid = (_silu(a) * b * cw).astype(BF16)
    acc[...] += _dot(hid, w2_ref[...].astype(BF16))

    @pl.when(e == MOE_E - 1)
    def _():
        o_ref[...] = x_ref[...] + gate_ref[...] * acc[...]


def _moe_dense(x, h, comb, mod, layer, w1, w3, w2):
    tm = TM_BIG
    return pl.pallas_call(
        _moe_dense_kernel,
        grid=(N_TOK // tm, MOE_E),
        in_specs=[
            pl.BlockSpec((tm, D), lambda t, e: (t, 0)),
            pl.BlockSpec((tm, ROUTER_PAD), lambda t, e: (t, 0)),
            pl.BlockSpec((None, None, D, MOE_HID), lambda t, e: (layer, e, 0, 0)),
            pl.BlockSpec((None, None, D, MOE_HID), lambda t, e: (layer, e, 0, 0)),
            pl.BlockSpec((None, None, MOE_HID, D), lambda t, e: (layer, e, 0, 0)),
            _mod_spec(layer, 5, _row_big),
            pl.BlockSpec((tm, D), lambda t, e: (t, 0)),
        ],
        out_specs=pl.BlockSpec((tm, D), lambda t, e: (t, 0)),
        out_shape=jax.ShapeDtypeStruct((N_TOK, D), F32),
        scratch_shapes=[pltpu.VMEM((tm, D), F32)],
        input_output_aliases={6: 0},
        compiler_params=_cparams(2),
        name="moe_dense",
    )(h, comb, w1, w3, w2, mod, x)


def _final_norm_kernel(x_ref, g_ref, o_ref):
    x = x_ref[...]
    o_ref[...] = x * lax.rsqrt(jnp.mean(x * x, axis=-1, keepdims=True) + EPS) * g_ref[...]


def _final_norm(x, final_g, row_off, n_rows):
    off = row_off // TM
    return pl.pallas_call(
        _final_norm_kernel,
        grid=(n_rows // TM,),
        in_specs=[pl.BlockSpec((TM, D), lambda t: (t + off, 0)), _vec_spec(D)],
        out_specs=pl.BlockSpec((TM, D), lambda t: (t, 0)),
        out_shape=jax.ShapeDtypeStruct((n_rows, D), F32),
        compiler_params=_cparams(1),
        name="final_norm",
    )(x, final_g.reshape(1, D))


def kernel(x_prompt, x_sample, state_gla, c, c_ctx, w_ada, b_ada, norm_g, hy_w_in, hy_b_in, hy_conv_w, hy_conv_b, hy_f_w1, hy_f_b1, hy_f_freq, hy_f_w2, hy_f_b2, hy_f_w3, hy_skip, hy_w_out, hy_b_out, gla_w_q, gla_w_k, gla_w_v, gla_w_g, gla_w_gk1, gla_w_gk2, gla_b_gk, gla_norm_g, gla_w_o, fn_w_out, fn_b_out, pool_w, pool_b, pool_scale, moe_w_rg, moe_w_re, moe_w1, moe_w3, moe_w2, final_g):
    groups = ((CTX_L, CTX_B, 0, None), (LAT_L, LAT_B, N_CTX, LAT_L // GRID_W))

    x = jnp.concatenate([x_prompt.reshape(N_CTX, D), x_sample.reshape(N_LAT, D)], axis=0)
    cond = jnp.zeros((MOD_ROWS, D), F32).at[0].set(c_ctx).at[1:1 + LAT_B].set(c)
    mod = _ada_table(cond, w_ada, b_ada).reshape(DEPTH * MOD_ROWS * 6, 1, D)

    new_states = []
    for i in range(DEPTH):
        kind, j = i % 4, i // 4
        if kind == 0:
            u = _hyena_in(x, mod, i, norm_g[i, 0], hy_w_in[j].astype(BF16), hy_b_in[j], hy_conv_w[j], hy_conv_b[j])
            w_out = hy_w_out[j].astype(BF16)
            for L, nb, off, _ in groups:
                fwd, ff, inv = _dft_mats(L)
                khat = _hyena_filters(L, ff, hy_f_w1[j], hy_f_b1[j], hy_f_freq[j], hy_f_w2[j], hy_f_b2[j],
                                      hy_f_w3[j])
                z = _hyena_conv(u, khat, hy_skip[j], fwd, inv, L, nb, off, D if L == CTX_L else 512)
                x = _outproj(x, z, w_out, hy_b_out[j], mod, i, off, nb * L)
        elif kind == 1:
            w_cat = jnp.concatenate([gla_w_q[j], gla_w_k[j], gla_w_v[j], gla_w_g[j]], axis=1).astype(BF16)
            nk = GLA_H * GLA_DK
            wg1 = jnp.zeros((D, GK1_PAD), F32).at[:, :GLA_RANK].set(gla_w_gk1[j, 0])
            wg1 = wg1.at[:, GLA_RANK:2 * GLA_RANK].set(gla_w_gk1[j, 1]).astype(BF16)
            wg2 = jnp.zeros((GK1_PAD, 2 * nk), F32).at[:GLA_RANK, :nk].set(gla_w_gk2[j, 0])
            wg2 = wg2.at[GLA_RANK:2 * GLA_RANK, nk:].set(gla_w_gk2[j, 1]).astype(BF16)
            proj = _gla_proj(x, mod, i, norm_g[i, 0], w_cat, wg1, wg2, gla_b_gk[j].reshape(1, 2 * nk))
            lower = np.tril(np.ones((GLA_CHUNK, GLA_CHUNK)))
            tri = jnp.asarray(np.stack([lower, lower.T]), BF16)
            w_o = gla_w_o[j].astype(BF16)
            for L, nb, off, grid_rows in groups:
                s0 = None if grid_rows is None else state_gla[:, j]
                o, s_fin = _gla_core(proj, tri, gla_norm_g[j], s0, L, nb, off)
                if grid_rows is None:
                    new_states.append(s_fin)
                x = _outproj(x, o, w_o, jnp.zeros((D,), F32), mod, i, off, nb * L)
        elif kind == 2:
            w_out = fn_w_out[j].astype(BF16)
            for L, nb, off, _ in groups:
                chan, seq = _fnet_mats(L)
                x = _fnet(x, mod, i, norm_g[i, 0], chan, seq, w_out, fn_b_out[j], L, nb, off)
        else:
            w_pool = pool_w[j].astype(BF16)
            for L, nb, off, grid_rows in groups:
                mats, inv_cnt = _pool_mats(L, grid_rows)
                x = _pool(x, mod, i, norm_g[i, 0], mats, inv_cnt, w_pool, pool_b[j], pool_scale[j], L, nb, off)

        w_router = jnp.zeros((D, ROUTER_PAD), F32).at[:, :MOE_E].set(moe_w_re[i])
        w_router = w_router.at[:, MOE_E:MOE_E + MOE_GROUPS].set(moe_w_rg[i])
        h, comb = _moe_route(x, mod, i, norm_g[i, 1], w_router)
        x = _moe_dense(x, h, comb, mod, i, moe_w1, moe_w3, moe_w2)

    y_prompt = _final_norm(x, final_g, 0, N_CTX).reshape(CTX_B, CTX_L, D)
    y_sample = _final_norm(x, final_g, N_CTX, N_LAT).reshape(LAT_B, LAT_L, D)
    new_state_gla = jnp.stack(new_states, axis=1)
    return (y_prompt, y_sample, new_state_gla)
```

```python
import functools
import math

import jax
import jax.numpy as jnp
import numpy as np
from jax import lax
from jax.experimental import pallas as pl
from jax.experimental.pallas import tpu as pltpu

F32 = jnp.float32
BF16 = jnp.bfloat16

D = 1024
CTX_B, CTX_L = 32, 256
LAT_B, LAT_L = 2, 1024
N_CTX = CTX_B * CTX_L
N_LAT = LAT_B * LAT_L
N_TOK = N_CTX + N_LAT
DEPTH = 4
GRID_W = 64
EPS = 1e-6

HY_BANDS = 8
HY_EMB = 1 + 2 * HY_BANDS
HY_EMB_PAD = 32
HY_HID = 64
HY_FAST_DECAY = 0.3
HY_SLOW_DECAY = 1.5
HY_DECAY_TARGET = 1e-2

GLA_H = 4
GLA_DK = 128
GLA_DV = 256
GLA_RANK = 16
GLA_NORMALIZER = 16.0
GLA_CHUNK = 64

FNET_GROUPS = 4
FNET_C = D // FNET_GROUPS
POOL_WINDOWS = (2, 4, 8, 16)
POOL_G = D // len(POOL_WINDOWS)

MOE_GROUPS = 4
MOE_PER_GROUP = 4
MOE_E = MOE_GROUPS * MOE_PER_GROUP
MOE_HID = D // 2

MOD_ROWS = 8
TM = 512
TM_BIG = 1024
VMEM_LIMIT = 56 * 1024 * 1024


def _cparams(n_axes):
    return pltpu.CompilerParams(dimension_semantics=("arbitrary",) * n_axes, vmem_limit_bytes=VMEM_LIMIT)


def _norm_mod(x, g, sc, sh):
    ms = jnp.mean(x * x, axis=-1, keepdims=True)
    return (x * lax.rsqrt(ms + EPS) * g) * (1.0 + sc) + sh


def _split(a):
    hi = a.astype(BF16)
    lo = (a - hi.astype(F32)).astype(BF16)
    return hi, lo


def _dot(a, b):
    return jnp.dot(a, b, preferred_element_type=F32)


def _dot_precise(a, b):
    a_hi, a_lo = _split(a)
    b_hi, b_lo = _split(b)
    return _dot(a_hi, b_hi) + (_dot(a_hi, b_lo) + _dot(a_lo, b_hi))


def _silu(x):
    return x * (1.0 / (1.0 + jnp.exp(-x)))


def _log_sigmoid(x):
    return jnp.minimum(x, 0.0) - jnp.log(1.0 + jnp.exp(-jnp.abs(x)))


def _mod_spec(layer, chunk, row_fn):
    base = layer * MOD_ROWS * 6 + chunk

    def index_map(*ids):
        return (base + row_fn(*ids) * 6, 0, 0)

    return pl.BlockSpec((None, 1, D), index_map)


def _row_tm(t, *_):
    return jnp.where(t < N_CTX // TM, 0, 1 + (t - N_CTX // TM) // (LAT_L // TM))


def _row_big(t, *_):
    return jnp.where(t < N_CTX // TM_BIG, 0, 1 + (t - N_CTX // TM_BIG) // (LAT_L // TM_BIG))


def _vec_spec(n):
    return pl.BlockSpec((1, n), lambda *ids: (0, 0))


def _full_spec(shape):
    nd = len(shape)
    return pl.BlockSpec(shape, lambda *ids: (0,) * nd)


def _ada_kernel(cond_ref, w_ref, b_ref, o_ref):
    s = _silu(cond_ref[...]).astype(BF16)
    o_ref[...] = _dot(s, w_ref[...].astype(BF16)) + b_ref[...]


def _ada_table(cond, w_ada, b_ada):
    tn = 1536
    return pl.pallas_call(
        _ada_kernel,
        grid=(DEPTH, 6 * D // tn),
        in_specs=[
            pl.BlockSpec((MOD_ROWS, D), lambda i, j: (0, 0)),
            pl.BlockSpec((None, D, tn), lambda i, j: (i, 0, j)),
            pl.BlockSpec((None, 1, tn), lambda i, j: (i, 0, j)),
        ],
        out_specs=pl.BlockSpec((None, MOD_ROWS, tn), lambda i, j: (i, 0, j)),
        out_shape=jax.ShapeDtypeStruct((DEPTH, MOD_ROWS, 6 * D), F32),
        compiler_params=_cparams(2),
        name="ada_table",
    )(cond, w_ada, b_ada.reshape(DEPTH, 1, 6 * D))


def _outproj_kernel(z_ref, w_ref, b_ref, gate_ref, x_ref, o_ref):
    y = _dot(z_ref[...], w_ref[...]) + b_ref[...]
    o_ref[...] = x_ref[...] + gate_ref[...] * y


def _outproj(x, z, w_bf16, bias, mod, layer, row_off, n_rows):
    k = z.shape[1]
    off = row_off // TM
    first_lat = N_CTX // TM

    def row_fn(t):
        g = t + off
        return jnp.where(g < first_lat, 0, 1 + (g - first_lat) // (LAT_L // TM))

    return pl.pallas_call(
        _outproj_kernel,
        grid=(n_rows // TM,),
        in_specs=[
            pl.BlockSpec((TM, k), lambda t: (t, 0)),
            _full_spec((k, D)),
            _vec_spec(D),
            _mod_spec(layer, 2, row_fn),
            pl.BlockSpec((TM, D), lambda t: (t + off, 0)),
        ],
        out_specs=pl.BlockSpec((TM, D), lambda t: (t + off, 0)),
        out_shape=jax.ShapeDtypeStruct((N_TOK, D), F32),
        input_output_aliases={4: 0},
        compiler_params=_cparams(1),
        name="outproj_residual",
    )(z, w_bf16, bias.reshape(1, D), mod, x)


def _dft_mats(L):
    n2 = 2 * L
    k = np.arange(L)[:, None].astype(np.float64)
    n = np.arange(n2)[None, :].astype(np.float64)
    ang = 2.0 * np.pi * k * n / n2
    full = np.concatenate([np.cos(ang), -np.sin(ang)], axis=0)
    full[L, :] = np.cos(np.pi * np.arange(n2))
    fwd = full[:, :L]
    bwd = np.zeros((n2, L))
    bwd[:, 1:] = full[:, n2 - np.arange(1, L)]
    t = np.arange(L)[:, None].astype(np.float64)
    kk = np.arange(L)[None, :].astype(np.float64)
    ang_i = 2.0 * np.pi * t * kk / n2
    inv_re = np.cos(ang_i) / L
    inv_re[:, 0] = 1.0 / n2
    inv_im = -np.sin(ang_i) / L
    inv_im[:, 0] = np.cos(np.pi * np.arange(L)) / n2
    inv = np.concatenate([inv_re, inv_im], axis=1)
    return tuple(jnp.asarray(m, F32).astype(BF16) for m in (fwd, np.concatenate([fwd, bwd], axis=1), inv))


def _hyena_pos_emb(L):
    pos = np.arange(L, dtype=np.float64)
    bands = np.linspace(1e-4, HY_BANDS - 1, HY_BANDS)
    ang = (2.0 * np.pi * pos / L)[:, None] * bands[None, :]
    z = np.concatenate([(pos / L)[:, None], np.cos(ang), -np.sin(ang)], axis=-1)
    zp = np.zeros((L, HY_EMB_PAD))
    zp[:, :HY_EMB] = z
    return jnp.asarray(zp, F32)


def _hyena_filter_kernel(z_ref, w1_ref, b1_ref, fr_ref, w2_ref, b2_ref, w3f_ref, w3b_ref, ff_ref, o_ref, *, L, tn):
    j = pl.program_id(1)
    fr = fr_ref[...]
    f = jnp.sin(fr * (_dot_precise(z_ref[...], w1_ref[...]) + b1_ref[...]))
    f = jnp.sin(fr * (_dot_precise(f, w2_ref[...]) + b2_ref[...]))
    t_lin = lax.broadcasted_iota(jnp.int32, (L, tn), 0).astype(F32) / float(L - 1)
    ch = (lax.broadcasted_iota(jnp.int32, (L, tn), 1) + j * tn).astype(F32)
    max_decay = math.log(HY_DECAY_TARGET) / HY_FAST_DECAY
    min_decay = math.log(HY_DECAY_TARGET) / HY_SLOW_DECAY
    deltas = min_decay + ch * ((max_decay - min_decay) / float(D - 1))
    window = jnp.exp(-t_lin * jnp.abs(deltas))
    kf = _dot_precise(f, w3f_ref[...]) * window
    kb = _dot_precise(f, w3b_ref[...]) * window
    taps = jnp.concatenate([kf, kb], axis=0).astype(BF16)
    o_ref[...] = _dot(ff_ref[...], taps)


def _hyena_filters(L, ff, f_w1, f_b1, f_freq, f_w2, f_b2, f_w3):
    tn = 512
    nj = D // tn
    w1p = jnp.zeros((HY_EMB_PAD, HY_HID), F32).at[:HY_EMB].set(f_w1)
    kern = functools.partial(_hyena_filter_kernel, L=L, tn=tn)
    return pl.pallas_call(
        kern,
        grid=(2, nj),
        in_specs=[
            _full_spec((L, HY_EMB_PAD)),
            _full_spec((HY_EMB_PAD, HY_HID)),
            _vec_spec(HY_HID),
            _vec_spec(HY_HID),
            _full_spec((HY_HID, HY_HID)),
            _vec_spec(HY_HID),
            pl.BlockSpec((HY_HID, tn), lambda o, j: (0, o * nj + j)),
            pl.BlockSpec((HY_HID, tn), lambda o, j: (0, (2 + o) * nj + j)),
            _full_spec((2 * L, 2 * L)),
        ],
        out_specs=pl.BlockSpec((None, 2 * L, tn), lambda o, j: (o, 0, j)),
        out_shape=jax.ShapeDtypeStruct((2, 2 * L, D), F32),
        compiler_params=_cparams(2),
        name=f"hyena_filters_L{L}",
    )(_hyena_pos_emb(L), w1p, f_b1.reshape(1, -1), f_freq.reshape(1, -1), f_w2, f_b2.reshape(1, -1),
      f_w3, f_w3, ff)


def _hyena_in_kernel(x_ref, g_ref, sc_ref, sh_ref, w_ref, b_ref, cw_ref, cb_ref, o_ref, h_scr):
    t = pl.program_id(0)

    @pl.when(pl.program_id(1) == 0)
    def _():
        h_scr[...] = _norm_mod(x_ref[...], g_ref[...], sc_ref[...], sh_ref[...]).astype(BF16)

    u = _dot(h_scr[...], w_ref[...]) + b_ref[...]
    seq = jnp.where(t < N_CTX // TM_BIG, CTX_L, LAT_L)
    pos = lax.broadcasted_iota(jnp.int32, u.shape, 0) & (seq - 1)
    prev = jnp.where(pos == 0, 0.0, pltpu.roll(u, 1, 0))
    nxt = jnp.where(pos == seq - 1, 0.0, pltpu.roll(u, TM_BIG - 1, 0))
    cw = cw_ref[...]
    o_ref[...] = prev * cw[0:1] + u * cw[1:2] + nxt * cw[2:3] + cb_ref[...]


def _hyena_in(x, mod, layer, norm_g, w_in_bf16, b_in, conv_w, conv_b):
    return pl.pallas_call(
        _hyena_in_kernel,
        grid=(N_TOK // TM_BIG, 3),
        in_specs=[
            pl.BlockSpec((TM_BIG, D), lambda t, p: (t, 0)),
            _vec_spec(D),
            _mod_spec(layer, 1, _row_big),
            _mod_spec(layer, 0, _row_big),
            pl.BlockSpec((D, D), lambda t, p: (0, p)),
            pl.BlockSpec((1, D), lambda t, p: (0, p)),
            pl.BlockSpec((3, D), lambda t, p: (0, p)),
            pl.BlockSpec((1, D), lambda t, p: (0, p)),
        ],
        out_specs=pl.BlockSpec((TM_BIG, D), lambda t, p: (t, p)),
        out_shape=jax.ShapeDtypeStruct((N_TOK, 3 * D), F32),
        scratch_shapes=[pltpu.VMEM((TM_BIG, D), BF16)],
        compiler_params=_cparams(2),
        name="hyena_in",
    )(x, norm_g.reshape(1, D), mod, mod, w_in_bf16, b_in.reshape(1, -1), conv_w, conv_b.reshape(1, -1))


def _hyena_conv_kernel(v_ref, x1_ref, x2_ref, kh_ref, skip_ref, fwd_ref, inv_ref, o_ref, *, L):
    fwd = fwd_ref[...]
    inv = inv_ref[...]
    row0 = lax.broadcasted_iota(jnp.int32, (L, v_ref.shape[1]), 0) == 0

    def long_conv(z, order):
        zh = _dot(fwd, z.astype(BF16))
        zr, zi = zh[:L], zh[L:]
        kr, ki = kh_ref[order, :L, :], kh_ref[order, L:, :]
        pr = jnp.where(row0, zr * kr, zr * kr - zi * ki)
        pi = jnp.where(row0, zi * ki, zr * ki + zi * kr)
        prod = jnp.concatenate([pr, pi], axis=0).astype(BF16)
        return _dot(inv, prod) + z * skip_ref[order:order + 1, :]

    z = x1_ref[...] * long_conv(v_ref[...], 0)
    z = x2_ref[...] * long_conv(z, 1)
    o_ref[...] = z.astype(BF16)


def _hyena_conv(u, khat, skip, fwd, inv, L, n_batch, row_off, tn):
    nj = D // tn
    rb = row_off // L
    kern = functools.partial(_hyena_conv_kernel, L=L)
    return pl.pallas_call(
        kern,
        grid=(nj, n_batch),
        in_specs=[
            pl.BlockSpec((L, tn), lambda j, b: (rb + b, j)),
            pl.BlockSpec((L, tn), lambda j, b: (rb + b, nj + j)),
            pl.BlockSpec((L, tn), lambda j, b: (rb + b, 2 * nj + j)),
            pl.BlockSpec((2, 2 * L, tn), lambda j, b: (0, 0, j)),
            pl.BlockSpec((2, tn), lambda j, b: (0, j)),
            _full_spec((2 * L, L)),
            _full_spec((L, 2 * L)),
        ],
        out_specs=pl.BlockSpec((L, tn), lambda j, b: (b, j)),
        out_shape=jax.ShapeDtypeStruct((n_batch * L, D), BF16),
        compiler_params=_cparams(2),
        name=f"hyena_conv_L{L}",
    )(u, u, u, khat, skip, fwd, inv)


GLA_PROJ = 2 * GLA_H * GLA_DK + 2 * GLA_H * GLA_DV
GLA_COLS = GLA_PROJ + 2 * GLA_H * GLA_DK
GK1_PAD = 128


def _gla_proj_kernel(x_ref, g_ref, sc_ref, sh_ref, w_ref, wg1_ref, wg2_ref, bg_ref, o_ref):
    h = _norm_mod(x_ref[...], g_ref[...], sc_ref[...], sh_ref[...]).astype(BF16)
    p = _dot(h, w_ref[...])
    nq = GLA_H * GLA_DK
    o_ref[:, 0:nq] = p[:, 0:nq] * (GLA_DK ** -0.5)
    o_ref[:, nq:nq + nq + GLA_H * GLA_DV] = p[:, nq:nq + nq + GLA_H * GLA_DV]
    o_ref[:, 2 * nq + GLA_H * GLA_DV:GLA_PROJ] = _silu(p[:, 2 * nq + GLA_H * GLA_DV:GLA_PROJ])
    low = _dot(h, wg1_ref[...]).astype(BF16)
    gk = _dot(low, wg2_ref[...]) + bg_ref[...]
    o_ref[:, GLA_PROJ:GLA_COLS] = _log_sigmoid(gk) / GLA_NORMALIZER


def _gla_proj(x, mod, layer, norm_g, w_cat, wg1, wg2, bg):
    return pl.pallas_call(
        _gla_proj_kernel,
        grid=(N_TOK // TM,),
        in_specs=[
            pl.BlockSpec((TM, D), lambda t: (t, 0)),
            _vec_spec(D),
            _mod_spec(layer, 1, _row_tm),
            _mod_spec(layer, 0, _row_tm),
            _full_spec((D, GLA_PROJ)),
            _full_spec((D, GK1_PAD)),
            _full_spec((GK1_PAD, 2 * GLA_H * GLA_DK)),
            _vec_spec(2 * GLA_H * GLA_DK),
        ],
        out_specs=pl.BlockSpec((TM, GLA_COLS), lambda t: (t, 0)),
        out_shape=jax.ShapeDtypeStruct((N_TOK, GLA_COLS), F32),
        compiler_params=_cparams(1),
        name="gla_proj",
    )(x, norm_g.reshape(1, D), mod, mod, w_cat, wg1, wg2, bg)


def _gla_core_kernel(*refs, L, has_s0):
    if has_s0:
        q_ref, k_ref, v_ref, g_ref, gkf_ref, gkb_ref, tri_ref, ng_ref, s0_ref, o_ref, sf_ref, acc = refs
    else:
        q_ref, k_ref, v_ref, g_ref, gkf_ref, gkb_ref, tri_ref, ng_ref, o_ref, sf_ref, acc = refs
        s0_ref = None
    C = GLA_CHUNK
    n = L // C
    ones = jnp.ones((C, GLA_DK), BF16)
    ri = lax.broadcasted_iota(jnp.int32, (C, C), 0)
    ci = lax.broadcasted_iota(jnp.int32, (C, C), 1)
    tn_dims = (((0,), (0,)), ((), ()))

    for direction, gk_ref in enumerate((gkf_ref, gkb_ref)):
        tri = tri_ref[direction]
        keep = (ci <= ri) if direction == 0 else (ci >= ri)
        last = C - 1 if direction == 0 else 0
        s = s0_ref[direction] if has_s0 else jnp.zeros((GLA_DK, GLA_DV), F32)
        order = range(n) if direction == 0 else range(n - 1, -1, -1)
        for c in order:
            rows = slice(c * C, (c + 1) * C)
            gk_hi, gk_lo = _split(gk_ref[rows, :])
            b = _dot(tri, gk_hi) + _dot(tri, gk_lo)
            b_last = b[last:last + 1, :]
            q = q_ref[rows, :]
            k = k_ref[rows, :]
            v = v_ref[rows, :].astype(BF16)
            qe = (q * jnp.exp(b)).astype(BF16)
            ke = (k * jnp.exp(-b)).astype(BF16)
            kd = (k * jnp.exp(b_last - b)).astype(BF16)
            scores = lax.dot_general(qe, ke, (((1,), (1,)), ((), ())), preferred_element_type=F32)
            scores = jnp.where(keep, scores, 0.0).astype(BF16)
            o = _dot(scores, v) + _dot(qe, s.astype(BF16))
            if direction == 0:
                acc[rows, :] = o
            else:
                acc[rows, :] = acc[rows, :] + o
            tot = (lax.dot_general(gk_hi, ones, tn_dims, preferred_element_type=F32)
                   + lax.dot_general(gk_lo, ones, tn_dims, preferred_element_type=F32))
            dec = jnp.exp(tot)
            dec = jnp.concatenate([dec, dec], axis=1)
            s = dec * s + lax.dot_general(kd, v, tn_dims, preferred_element_type=F32)
        sf_ref[direction] = s

    o = acc[...]
    o = o * lax.rsqrt(jnp.mean(o * o, axis=-1, keepdims=True) + EPS) * ng_ref[...]
    o_ref[...] = (o * g_ref[...]).astype(BF16)


def _gla_core(proj, tri, norm_g, s0, L, n_batch, row_off):
    rb = row_off // L
    H = GLA_H
    has_s0 = s0 is not None
    kern = functools.partial(_gla_core_kernel, L=L, has_s0=has_s0)
    kb, vb = GLA_DK, GLA_DV
    in_specs = [
        pl.BlockSpec((L, kb), lambda b, h: (rb + b, h)),
        pl.BlockSpec((L, kb), lambda b, h: (rb + b, H + h)),
        pl.BlockSpec((L, vb), lambda b, h: (rb + b, (2 * H * kb) // vb + h)),
        pl.BlockSpec((L, vb), lambda b, h: (rb + b, (2 * H * kb) // vb + H + h)),
        pl.BlockSpec((L, kb), lambda b, h: (rb + b, GLA_PROJ // kb + h)),
        pl.BlockSpec((L, kb), lambda b, h: (rb + b, GLA_PROJ // kb + H + h)),
        _full_spec((2, GLA_CHUNK, GLA_CHUNK)),
        _vec_spec(vb),
    ]
    args = [proj] * 6 + [tri, norm_g.reshape(1, vb)]
    if has_s0:
        in_specs.append(pl.BlockSpec((None, 2, None, kb, vb), lambda b, h: (b, 0, h, 0, 0)))
        args.append(s0)
    return pl.pallas_call(
        kern,
        grid=(n_batch, H),
        in_specs=in_specs,
        out_specs=[
            pl.BlockSpec((L, vb), lambda b, h: (b, h)),
            pl.BlockSpec((None, 2, None, kb, vb), lambda b, h: (b, 0, h, 0, 0)),
        ],
        out_shape=[
            jax.ShapeDtypeStruct((n_batch * L, H * vb), BF16),
            jax.ShapeDtypeStruct((n_batch, 2, H, kb, vb), F32),
        ],
        scratch_shapes=[pltpu.VMEM((L, vb), F32)],
        compiler_params=_cparams(2),
        name=f"gla_core_L{L}",
    )(*args)


def _fnet_mats(L):
    c = np.arange(FNET_C)
    ang_c = 2.0 * np.pi * np.outer(c, c) / FNET_C
    chan = np.concatenate([np.cos(ang_c), np.sin(ang_c)], axis=1) / math.sqrt(FNET_C)
    t = np.arange(L)
    ang_l = 2.0 * np.pi * np.outer(t, t) / L
    seq = np.concatenate([np.cos(ang_l), -np.sin(ang_l)], axis=1) / math.sqrt(L)
    return jnp.asarray(chan, F32).astype(BF16), jnp.asarray(seq, F32).astype(BF16)


def _fnet_kernel(x_ref, g_ref, sc_ref, sh_ref, gate_ref, chan_ref, seq_ref, w_ref, b_ref, o_ref):
    x = x_ref[...]
    h = _norm_mod(x, g_ref[...], sc_ref[...], sh_ref[...]).astype(BF16)
    chan = chan_ref[...]
    cos_parts, sin_parts = [], []
    for gi in range(FNET_GROUPS):
        cs = _dot(h[:, gi * FNET_C:(gi + 1) * FNET_C], chan)
        cos_parts.append(cs[:, :FNET_C])
        sin_parts.append(cs[:, FNET_C:])
    stacked = jnp.concatenate([jnp.concatenate(cos_parts, axis=1), jnp.concatenate(sin_parts, axis=1)], axis=0)
    mixed = _dot(seq_ref[...], stacked.astype(BF16))
    y = _dot(mixed.astype(BF16), w_ref[...]) + b_ref[...]
    o_ref[...] = x + gate_ref[...] * y


def _fnet(x, mod, layer, norm_g, chan, seq, w_bf16, bias, L, n_batch, row_off):
    rb = row_off // L
    lat = row_off > 0

    def row_fn(b):
        return 1 + b if lat else 0

    return pl.pallas_call(
        _fnet_kernel,
        grid=(n_batch,),
        in_specs=[
            pl.BlockSpec((L, D), lambda b: (rb + b, 0)),
            _vec_spec(D),
            _mod_spec(layer, 1, row_fn),
            _mod_spec(layer, 0, row_fn),
            _mod_spec(layer, 2, row_fn),
            _full_spec((FNET_C, 2 * FNET_C)),
            _full_spec((L, 2 * L)),
            _full_spec((D, D)),
            _vec_spec(D),
        ],
        out_specs=pl.BlockSpec((L, D), lambda b: (rb + b, 0)),
        out_shape=jax.ShapeDtypeStruct((N_TOK, D), F32),
        input_output_aliases={0: 0},
        compiler_params=_cparams(1),
        name=f"fnet_L{L}",
    )(x, norm_g.reshape(1, D), mod, mod, mod, chan, seq, w_bf16, bias.reshape(1, D))


def _window_bounds(n, k):
    t = np.arange(n)
    lo, hi = k // 2, k - k // 2 - 1
    return np.maximum(t - lo, 0), np.minimum(t + hi + 1, n)


def _pool_mats(L, grid_rows):
    mats, inv = [], []
    for k in POOL_WINDOWS:
        if grid_rows is None:
            s, e = _window_bounds(L, k)
            idx = np.arange(L)[None, :]
            m = ((idx >= s[:, None]) & (idx < e[:, None])).astype(np.float64)
            cnt = (e - s).astype(np.float64)
        else:
            sr, er = _window_bounds(grid_rows, k)
            sc, ec = _window_bounds(GRID_W, k)
            ir = np.arange(grid_rows)[None, :]
            ic = np.arange(GRID_W)[None, :]
            mr = ((ir >= sr[:, None]) & (ir < er[:, None])).astype(np.float64)
            mc = ((ic >= sc[:, None]) & (ic < ec[:, None])).astype(np.float64)
            m = np.kron(mr, mc)
            cnt = np.kron((er - sr).astype(np.float64), (ec - sc).astype(np.float64))
        mats.append(m)
        inv.append(1.0 / cnt)
    return jnp.asarray(np.stack(mats), BF16), jnp.asarray(np.stack(inv)[:, :, None], F32)


def _pool_kernel(x_ref, g_ref, sc_ref, sh_ref, gate_ref, m_ref, ic_ref, w_ref, b_ref, ps_ref, o_ref):
    x = x_ref[...]
    h = _norm_mod(x, g_ref[...], sc_ref[...], sh_ref[...])
    outs = []
    for gi in range(len(POOL_WINDOWS)):
        hg = h[:, gi * POOL_G:(gi + 1) * POOL_G]
        hi, lo = _split(hg)
        m = m_ref[gi]
        mean = (_dot(m, hi) + _dot(m, lo)) * ic_ref[gi]
        outs.append(_dot((mean - hg).astype(BF16), w_ref[gi]))
    y = (jnp.concatenate(outs, axis=1) + b_ref[...]) * ps_ref[...]
    o_ref[...] = x + gate_ref[...] * y


def _pool(x, mod, layer, norm_g, mats, inv_cnt, w_bf16, bias, scale, L, n_batch, row_off):
    rb = row_off // L
    lat = row_off > 0
    G = len(POOL_WINDOWS)

    def row_fn(b):
        return 1 + b if lat else 0

    return pl.pallas_call(
        _pool_kernel,
        grid=(n_batch,),
        in_specs=[
            pl.BlockSpec((L, D), lambda b: (rb + b, 0)),
            _vec_spec(D),
            _mod_spec(layer, 1, row_fn),
            _mod_spec(layer, 0, row_fn),
            _mod_spec(layer, 2, row_fn),
            _full_spec((G, L, L)),
            _full_spec((G, L, 1)),
            _full_spec((G, POOL_G, POOL_G)),
            _vec_spec(D),
            _vec_spec(D),
        ],
        out_specs=pl.BlockSpec((L, D), lambda b: (rb + b, 0)),
        out_shape=jax.ShapeDtypeStruct((N_TOK, D), F32),
        input_output_aliases={0: 0},
        compiler_params=_cparams(1),
        name=f"pool_L{L}",
    )(x, norm_g.reshape(1, D), mod, mod, mod, mats, inv_cnt, w_bf16, bias.reshape(1, D), scale.reshape(1, D))


ROUTER_PAD = 128
LANES = 128
ROW_PITCH = 16
Y_PITCH = D // LANES
MOE_TILE = 512
MOE_MAX_TILES = N_TOK // MOE_TILE + MOE_GROUPS
MOE_ROWS = MOE_MAX_TILES * MOE_TILE
MOE_Y_ROWS = N_TOK + 2 * MOE_TILE


def _moe_route_kernel(x_ref, g_ref, sc_ref, sh_ref, wr_ref, tri_ref, h3_ref, route_ref, cnt_ref, carry):
    t = pl.program_id(0)

    @pl.when(t == 0)
    def _():
        carry[...] = jnp.zeros_like(carry)

    h = _norm_mod(x_ref[...], g_ref[...], sc_ref[...], sh_ref[...])
    logits = _dot_precise(h, wr_ref[...])
    lane = lax.broadcasted_iota(jnp.int32, logits.shape, 1)
    neg = jnp.float32(-jnp.inf)
    big = jnp.int32(ROUTER_PAD)
    is_grp = (lane >= MOE_E) & (lane < MOE_E + MOE_GROUPS)
    gl = jnp.where(is_grp, logits, neg)
    g_max = jnp.max(gl, axis=-1, keepdims=True)
    g_idx = jnp.min(jnp.where(gl == g_max, lane, big), axis=-1, keepdims=True) - MOE_E
    p_grp = 1.0 / jnp.sum(jnp.where(is_grp, jnp.exp(gl - g_max), 0.0), axis=-1, keepdims=True)
    in_grp = (lane < MOE_E) & ((lane >> 2) == g_idx)
    el = jnp.where(in_grp, logits, neg)
    m1 = jnp.max(el, axis=-1, keepdims=True)
    i1 = jnp.min(jnp.where(el == m1, lane, big), axis=-1, keepdims=True)
    z = jnp.sum(jnp.where(in_grp, jnp.exp(el - m1), 0.0), axis=-1, keepdims=True)
    el2 = jnp.where(lane == i1, neg, el)
    m2 = jnp.max(el2, axis=-1, keepdims=True)
    i2 = jnp.min(jnp.where(el2 == m2, lane, big), axis=-1, keepdims=True)
    p1 = 1.0 / z
    p2 = jnp.exp(m2 - m1) / z
    tot = p1 + p2
    eid = lane + MOE_PER_GROUP * g_idx
    in4 = lane < MOE_PER_GROUP
    cw4 = (jnp.where(in4 & (eid == i1), p_grp * (p1 / tot), 0.0)
           + jnp.where(in4 & (eid == i2), p_grp * (p2 / tot), 0.0))
    member = jnp.where(lane == g_idx, 1.0, 0.0).astype(BF16)
    before = _dot(tri_ref[...], member) + carry[...]
    rank = jnp.sum(jnp.where(lane == g_idx, before, 0.0), axis=-1, keepdims=True)
    carry[...] = carry[...] + jnp.sum(member.astype(F32), axis=0, keepdims=True)
    cnt_ref[...] = carry[...].astype(jnp.int32)
    route_ref[...] = jnp.where(lane == 0, g_idx, jnp.where(lane == 1, rank.astype(jnp.int32), 0))
    h3_ref[...] = jnp.zeros_like(h3_ref)
    for s in range(Y_PITCH):
        h3_ref[pl.ds(s, TM, stride=ROW_PITCH), :] = h[:, s * LANES:(s + 1) * LANES]
    h3_ref[pl.ds(Y_PITCH, TM, stride=ROW_PITCH), :] = cw4


def _moe_route(x, mod, layer, norm_g, w_router, tri):
    return pl.pallas_call(
        _moe_route_kernel,
        grid=(N_TOK // TM,),
        in_specs=[
            pl.BlockSpec((TM, D), lambda t: (t, 0)),
            _vec_spec(D),
            _mod_spec(layer, 4, _row_tm),
            _mod_spec(layer, 3, _row_tm),
            _full_spec((D, ROUTER_PAD)),
            _full_spec((TM, TM)),
        ],
        out_specs=[
            pl.BlockSpec((TM * ROW_PITCH, LANES), lambda t: (t, 0)),
            pl.BlockSpec((TM, LANES), lambda t: (t, 0)),
            pl.BlockSpec((1, LANES), lambda t: (0, 0)),
        ],
        out_shape=[
            jax.ShapeDtypeStruct((N_TOK * ROW_PITCH, LANES), F32),
            jax.ShapeDtypeStruct((N_TOK, LANES), jnp.int32),
            jax.ShapeDtypeStruct((1, LANES), jnp.int32),
        ],
        scratch_shapes=[pltpu.VMEM((1, LANES), F32)],
        compiler_params=_cparams(1),
        name="moe_route",
    )(x, norm_g.reshape(1, D), mod, mod, w_router, tri)


def _moe_invert_kernel(pos_ref, src_ref):
    def clear(i, carry):
        src_ref[i] = jnp.int32(N_TOK)
        return carry

    lax.fori_loop(0, MOE_ROWS, clear, 0, unroll=8)

    def place(n, carry):
        src_ref[pos_ref[n]] = n
        return carry

    lax.fori_loop(0, N_TOK, place, 0, unroll=8)


def _moe_invert(pos):
    return pl.pallas_call(
        _moe_invert_kernel,
        in_specs=[pl.BlockSpec(memory_space=pltpu.SMEM)],
        out_specs=pl.BlockSpec(memory_space=pltpu.SMEM),
        out_shape=jax.ShapeDtypeStruct((MOE_ROWS,), jnp.int32),
        name="moe_invert",
    )(pos)


def _moe_expert_kernel(src_ref, grp_ref, nact_ref, h3_hbm, w1_ref, w3_ref, w2_ref, y3_hbm,
                       xb0, xb1, yb0, yb1, xmat, cw, acc, gsem, ssem):
    t = pl.program_id(0)
    k = pl.program_id(1)
    n_active = nact_ref[0]
    active = t < n_active
    xbufs, ybufs = (xb0, xb1), (yb0, yb1)
    T = MOE_TILE

    def gather_row(tile, slot, r):
        tok = jnp.minimum(src_ref[tile * T + r], N_TOK - 1)
        return pltpu.make_async_copy(
            h3_hbm.at[pl.ds(pl.multiple_of(tok * ROW_PITCH, ROW_PITCH), ROW_PITCH), :],
            xbufs[slot].at[pl.ds(pl.multiple_of(r * ROW_PITCH, ROW_PITCH), ROW_PITCH), :], gsem.at[slot])

    def scatter_row(tile, slot, r):
        tok = src_ref[tile * T + r]
        dst = jnp.where(tok < N_TOK, tok, N_TOK + slot * T + r)
        return pltpu.make_async_copy(
            ybufs[slot].at[pl.ds(pl.multiple_of(r * Y_PITCH, Y_PITCH), Y_PITCH), :],
            y3_hbm.at[pl.ds(pl.multiple_of(dst * Y_PITCH, Y_PITCH), Y_PITCH), :], ssem.at[slot])

    def start_all(make, tile, slot):
        def body(r, carry):
            make(tile, slot, r).start()
            return carry

        lax.fori_loop(0, T, body, 0, unroll=8)

    def wait_gather(slot):
        pltpu.make_async_copy(h3_hbm.at[pl.ds(0, T * ROW_PITCH), :], xbufs[slot], gsem.at[slot]).wait()

    def wait_scatter(slot):
        pltpu.make_async_copy(ybufs[slot], y3_hbm.at[pl.ds(0, T * Y_PITCH), :], ssem.at[slot]).wait()

    for slot in (0, 1):
        @pl.when(active & (k == 0) & (t % 2 == slot))
        def _(slot=slot):
            if slot == 0:
                @pl.when(t == 0)
                def _():
                    start_all(gather_row, 0, 0)
                    yb0[...] = jnp.zeros_like(yb0)
                    dumps = [pltpu.make_async_copy(
                        yb0, y3_hbm.at[pl.ds((N_TOK + s * T) * Y_PITCH, T * Y_PITCH), :], ssem.at[0]) for s in (0, 1)]
                    for cp in dumps:
                        cp.start()
                    for cp in dumps:
                        cp.wait()

            wait_gather(slot)

            @pl.when(t + 1 < n_active)
            def _():
                start_all(gather_row, t + 1, 1 - slot)

            for s in range(Y_PITCH):
                xmat[:, s * LANES:(s + 1) * LANES] = xbufs[slot][pl.ds(s, T, stride=ROW_PITCH), :].astype(BF16)
            cw[...] = xbufs[slot][pl.ds(Y_PITCH, T, stride=ROW_PITCH), :]

    @pl.when(active)
    def _():
        x = xmat[...]
        a = _dot(x, w1_ref[...].astype(BF16))
        b = _dot(x, w3_ref[...].astype(BF16))
        lane = lax.broadcasted_iota(jnp.int32, (T, LANES), 1)
        cwk = jnp.sum(jnp.where(lane == k, cw[...], 0.0), axis=-1, keepdims=True)
        hid = (_silu(a) * b * cwk).astype(BF16)
        contrib = _dot(hid, w2_ref[...].astype(BF16))

        @pl.when(k == 0)
        def _():
            acc[...] = contrib

        @pl.when(k > 0)
        def _():
            acc[...] = acc[...] + contrib

    for slot in (0, 1):
        @pl.when(active & (k == MOE_PER_GROUP - 1) & (t % 2 == slot))
        def _(slot=slot):
            @pl.when(t >= 2)
            def _():
                wait_scatter(slot)

            for s in range(Y_PITCH):
                ybufs[slot][pl.ds(s, T, stride=Y_PITCH), :] = acc[:, s * LANES:(s + 1) * LANES]
            start_all(scatter_row, t, slot)

            @pl.when(t == n_active - 1)
            def _():
                wait_scatter(slot)

                @pl.when(t >= 1)
                def _():
                    wait_scatter(1 - slot)


def _moe_experts(h3, src, tile_group, n_active, layer, w1, w3, w2):
    T = MOE_TILE

    def w_index(t, k, src_ref, grp_ref, nact_ref):
        last = nact_ref[0] - 1
        e = jnp.where(t <= last, grp_ref[t] * MOE_PER_GROUP + k, grp_ref[last] * MOE_PER_GROUP + MOE_PER_GROUP - 1)
        return (layer, e, 0, 0)

    grid_spec = pltpu.PrefetchScalarGridSpec(
        num_scalar_prefetch=3,
        grid=(MOE_MAX_TILES, MOE_PER_GROUP),
        in_specs=[
            pl.BlockSpec(memory_space=pl.ANY),
            pl.BlockSpec((None, None, D, MOE_HID), w_index),
            pl.BlockSpec((None, None, D, MOE_HID), w_index),
            pl.BlockSpec((None, None, MOE_HID, D), w_index),
        ],
        out_specs=pl.BlockSpec(memory_space=pl.ANY),
        scratch_shapes=[
            pltpu.VMEM((T * ROW_PITCH, LANES), F32),
            pltpu.VMEM((T * ROW_PITCH, LANES), F32),
            pltpu.VMEM((T * Y_PITCH, LANES), F32),
            pltpu.VMEM((T * Y_PITCH, LANES), F32),
            pltpu.VMEM((T, D), BF16),
            pltpu.VMEM((T, LANES), F32),
            pltpu.VMEM((T, D), F32),
            pltpu.SemaphoreType.DMA((2,)),
            pltpu.SemaphoreType.DMA((2,)),
        ],
    )
    return pl.pallas_call(
        _moe_expert_kernel,
        grid_spec=grid_spec,
        out_shape=jax.ShapeDtypeStruct((MOE_Y_ROWS * Y_PITCH, LANES), F32),
        compiler_params=_cparams(2),
        name="moe_experts",
    )(src, tile_group, n_active, h3, w1, w3, w2)


def _moe_combine_kernel(y3_ref, gate_ref, x_ref, o_ref):
    y = jnp.concatenate([y3_ref[pl.ds(s, TM, stride=Y_PITCH), :] for s in range(Y_PITCH)], axis=1)
    o_ref[...] = x_ref[...] + gate_ref[...] * y


def _moe_combine(x, y3, mod, layer):
    return pl.pallas_call(
        _moe_combine_kernel,
        grid=(N_TOK // TM,),
        in_specs=[
            pl.BlockSpec((TM * Y_PITCH, LANES), lambda t: (t, 0)),
            _mod_spec(layer, 5, _row_tm),
            pl.BlockSpec((TM, D), lambda t: (t, 0)),
        ],
        out_specs=pl.BlockSpec((TM, D), lambda t: (t, 0)),
        out_shape=jax.ShapeDtypeStruct((N_TOK, D), F32),
        input_output_aliases={2: 0},
        compiler_params=_cparams(1),
        name="moe_combine",
    )(y3, mod, x)


def _moe(x, mod, layer, norm_g, w_rg, w_re, w1, w3, w2, tri):
    w_router = jnp.zeros((D, ROUTER_PAD), F32).at[:, :MOE_E].set(w_re).at[:, MOE_E:MOE_E + MOE_GROUPS].set(w_rg)
    h3, route, counts = _moe_route(x, mod, layer, norm_g, w_router, tri)
    cnt = counts[0, :MOE_GROUPS]
    ntile = (cnt + MOE_TILE - 1) // MOE_TILE
    tile_end = jnp.cumsum(ntile)
    seg_start = (tile_end - ntile) * MOE_TILE
    g_idx, rank = route[:, 0], route[:, 1]
    pos = jnp.sum(jnp.where(g_idx[:, None] == jnp.arange(MOE_GROUPS)[None, :], seg_start[None, :], 0), axis=1) + rank
    tiles = jnp.arange(MOE_MAX_TILES, dtype=jnp.int32)
    tile_group = jnp.minimum(jnp.sum(tiles[:, None] >= tile_end[None, :], axis=1), MOE_GROUPS - 1).astype(jnp.int32)
    n_active = tile_end[-1:].astype(jnp.int32)
    src = _moe_invert(pos.astype(jnp.int32))
    y3 = _moe_experts(h3, src, tile_group, n_active, layer, w1, w3, w2)
    return _moe_combine(x, y3, mod, layer)


def _final_norm_kernel(x_ref, g_ref, o_ref):
    x = x_ref[...]
    o_ref[...] = x * lax.rsqrt(jnp.mean(x * x, axis=-1, keepdims=True) + EPS) * g_ref[...]


def _final_norm(x, final_g, row_off, n_rows):
    off = row_off // TM
    return pl.pallas_call(
        _final_norm_kernel,
        grid=(n_rows // TM,),
        in_specs=[pl.BlockSpec((TM, D), lambda t: (t + off, 0)), _vec_spec(D)],
        out_specs=pl.BlockSpec((TM, D), lambda t: (t, 0)),
        out_shape=jax.ShapeDtypeStruct((n_rows, D), F32),
        compiler_params=_cparams(1),
        name="final_norm",
    )(x, final_g.reshape(1, D))


def kernel(x_prompt, x_sample, state_gla, c, c_ctx, w_ada, b_ada, norm_g, hy_w_in, hy_b_in, hy_conv_w, hy_conv_b, hy_f_w1, hy_f_b1, hy_f_freq, hy_f_w2, hy_f_b2, hy_f_w3, hy_skip, hy_w_out, hy_b_out, gla_w_q, gla_w_k, gla_w_v, gla_w_g, gla_w_gk1, gla_w_gk2, gla_b_gk, gla_norm_g, gla_w_o, fn_w_out, fn_b_out, pool_w, pool_b, pool_scale, moe_w_rg, moe_w_re, moe_w1, moe_w3, moe_w2, final_g):
    groups = ((CTX_L, CTX_B, 0, None), (LAT_L, LAT_B, N_CTX, LAT_L // GRID_W))

    x = jnp.concatenate([x_prompt.reshape(N_CTX, D), x_sample.reshape(N_LAT, D)], axis=0)
    cond = jnp.zeros((MOD_ROWS, D), F32).at[0].set(c_ctx).at[1:1 + LAT_B].set(c)
    mod = _ada_table(cond, w_ada, b_ada).reshape(DEPTH * MOD_ROWS * 6, 1, D)

    tri_tm = jnp.asarray(np.tril(np.ones((TM, TM)), -1), BF16)

    new_states = []
    for i in range(DEPTH):
        kind, j = i % 4, i // 4
        if kind == 0:
            u = _hyena_in(x, mod, i, norm_g[i, 0], hy_w_in[j].astype(BF16), hy_b_in[j], hy_conv_w[j], hy_conv_b[j])
            w_out = hy_w_out[j].astype(BF16)
            for L, nb, off, _ in groups:
                fwd, ff, inv = _dft_mats(L)
                khat = _hyena_filters(L, ff, hy_f_w1[j], hy_f_b1[j], hy_f_freq[j], hy_f_w2[j], hy_f_b2[j],
                                      hy_f_w3[j])
                z = _hyena_conv(u, khat, hy_skip[j], fwd, inv, L, nb, off, D if L == CTX_L else 512)
                x = _outproj(x, z, w_out, hy_b_out[j], mod, i, off, nb * L)
        elif kind == 1:
            w_cat = jnp.concatenate([gla_w_q[j], gla_w_k[j], gla_w_v[j], gla_w_g[j]], axis=1).astype(BF16)
            nk = GLA_H * GLA_DK
            wg1 = jnp.zeros((D, GK1_PAD), F32).at[:, :GLA_RANK].set(gla_w_gk1[j, 0])
            wg1 = wg1.at[:, GLA_RANK:2 * GLA_RANK].set(gla_w_gk1[j, 1]).astype(BF16)
            wg2 = jnp.zeros((GK1_PAD, 2 * nk), F32).at[:GLA_RANK, :nk].set(gla_w_gk2[j, 0])
            wg2 = wg2.at[GLA_RANK:2 * GLA_RANK, nk:].set(gla_w_gk2[j, 1]).astype(BF16)
            proj = _gla_proj(x, mod, i, norm_g[i, 0], w_cat, wg1, wg2, gla_b_gk[j].reshape(1, 2 * nk))
            lower = np.tril(np.ones((GLA_CHUNK, GLA_CHUNK)))
            tri = jnp.asarray(np.stack([lower, lower.T]), BF16)
            w_o = gla_w_o[j].astype(BF16)
            for L, nb, off, grid_rows in groups:
                s0 = None if grid_rows is None else state_gla[:, j]
                o, s_fin = _gla_core(proj, tri, gla_norm_g[j], s0, L, nb, off)
                if grid_rows is None:
                    new_states.append(s_fin)
                x = _outproj(x, o, w_o, jnp.zeros((D,), F32), mod, i, off, nb * L)
        elif kind == 2:
            w_out = fn_w_out[j].astype(BF16)
            for L, nb, off, _ in groups:
                chan, seq = _fnet_mats(L)
                x = _fnet(x, mod, i, norm_g[i, 0], chan, seq, w_out, fn_b_out[j], L, nb, off)
        else:
            w_pool = pool_w[j].astype(BF16)
            for L, nb, off, grid_rows in groups:
                mats, inv_cnt = _pool_mats(L, grid_rows)
                x = _pool(x, mod, i, norm_g[i, 0], mats, inv_cnt, w_pool, pool_b[j], pool_scale[j], L, nb, off)

        x = _moe(x, mod, i, norm_g[i, 1], moe_w_rg[i], moe_w_re[i], moe_w1, moe_w3, moe_w2, tri_tm)

    y_prompt = _final_norm(x, final_g, 0, N_CTX).reshape(CTX_B, CTX_L, D)
    y_sample = _final_norm(x, final_g, N_CTX, N_LAT).reshape(LAT_B, LAT_L, D)
    new_state_gla = jnp.stack(new_states, axis=1)
    return (y_prompt, y_sample, new_state_gla)
```

```python
import functools
import math

import jax
import jax.numpy as jnp
import numpy as np
from jax import lax
from jax.experimental import pallas as pl
from jax.experimental.pallas import tpu as pltpu

F32 = jnp.float32
BF16 = jnp.bfloat16

D = 1024
CTX_B, CTX_L = 32, 256
LAT_B, LAT_L = 2, 1024
N_CTX = CTX_B * CTX_L
N_LAT = LAT_B * LAT_L
N_TOK = N_CTX + N_LAT
DEPTH = 4
GRID_W = 64
EPS = 1e-6

HY_BANDS = 8
HY_EMB = 1 + 2 * HY_BANDS
HY_EMB_PAD = 32
HY_HID = 64
HY_FAST_DECAY = 0.3
HY_SLOW_DECAY = 1.5
HY_DECAY_TARGET = 1e-2

GLA_H = 4
GLA_DK = 128
GLA_DV = 256
GLA_RANK = 16
GLA_NORMALIZER = 16.0
GLA_CHUNK = 64

FNET_GROUPS = 4
FNET_C = D // FNET_GROUPS
POOL_WINDOWS = (2, 4, 8, 16)
POOL_G = D // len(POOL_WINDOWS)

MOE_GROUPS = 4
MOE_PER_GROUP = 4
MOE_E = MOE_GROUPS * MOE_PER_GROUP
MOE_HID = D // 2

MOD_ROWS = 8
TM = 512
TM_BIG = 1024
VMEM_LIMIT = 56 * 1024 * 1024


def _cparams(n_axes):
    return pltpu.CompilerParams(dimension_semantics=("arbitrary",) * n_axes, vmem_limit_bytes=VMEM_LIMIT)


def _norm_mod(x, g, sc, sh):
    ms = jnp.mean(x * x, axis=-1, keepdims=True)
    return (x * lax.rsqrt(ms + EPS) * g) * (1.0 + sc) + sh


def _split(a):
    hi = a.astype(BF16)
    lo = (a - hi.astype(F32)).astype(BF16)
    return hi, lo


def _dot(a, b):
    return jnp.dot(a, b, preferred_element_type=F32)


def _dot_precise(a, b):
    a_hi, a_lo = _split(a)
    b_hi, b_lo = _split(b)
    return _dot(a_hi, b_hi) + (_dot(a_hi, b_lo) + _dot(a_lo, b_hi))


def _silu(x):
    return x * (1.0 / (1.0 + jnp.exp(-x)))


def _log_sigmoid(x):
    return jnp.minimum(x, 0.0) - jnp.log(1.0 + jnp.exp(-jnp.abs(x)))


def _mod_spec(layer, chunk, row_fn):
    base = layer * MOD_ROWS * 6 + chunk

    def index_map(*ids):
        return (base + row_fn(*ids) * 6, 0, 0)

    return pl.BlockSpec((None, 1, D), index_map)


def _row_tm(t, *_):
    return jnp.where(t < N_CTX // TM, 0, 1 + (t - N_CTX // TM) // (LAT_L // TM))


def _row_big(t, *_):
    return jnp.where(t < N_CTX // TM_BIG, 0, 1 + (t - N_CTX // TM_BIG) // (LAT_L // TM_BIG))


def _vec_spec(n):
    return pl.BlockSpec((1, n), lambda *ids: (0, 0))


def _full_spec(shape):
    nd = len(shape)
    return pl.BlockSpec(shape, lambda *ids: (0,) * nd)


def _ada_kernel(cond_ref, w_ref, b_ref, o_ref):
    s = _silu(cond_ref[...]).astype(BF16)
    o_ref[...] = _dot(s, w_ref[...].astype(BF16)) + b_ref[...]


def _ada_table(cond, w_ada, b_ada):
    tn = 1536
    return pl.pallas_call(
        _ada_kernel,
        grid=(DEPTH, 6 * D // tn),
        in_specs=[
            pl.BlockSpec((MOD_ROWS, D), lambda i, j: (0, 0)),
            pl.BlockSpec((None, D, tn), lambda i, j: (i, 0, j)),
            pl.BlockSpec((None, 1, tn), lambda i, j: (i, 0, j)),
        ],
        out_specs=pl.BlockSpec((None, MOD_ROWS, tn), lambda i, j: (i, 0, j)),
        out_shape=jax.ShapeDtypeStruct((DEPTH, MOD_ROWS, 6 * D), F32),
        compiler_params=_cparams(2),
        name="ada_table",
    )(cond, w_ada, b_ada.reshape(DEPTH, 1, 6 * D))


def _outproj_kernel(z_ref, w_ref, b_ref, gate_ref, x_ref, o_ref):
    y = _dot(z_ref[...], w_ref[...]) + b_ref[...]
    o_ref[...] = x_ref[...] + gate_ref[...] * y


def _outproj(x, z, w_bf16, bias, mod, layer, row_off, n_rows):
    k = z.shape[1]
    off = row_off // TM
    first_lat = N_CTX // TM

    def row_fn(t):
        g = t + off
        return jnp.where(g < first_lat, 0, 1 + (g - first_lat) // (LAT_L // TM))

    return pl.pallas_call(
        _outproj_kernel,
        grid=(n_rows // TM,),
        in_specs=[
            pl.BlockSpec((TM, k), lambda t: (t, 0)),
            _full_spec((k, D)),
            _vec_spec(D),
            _mod_spec(layer, 2, row_fn),
            pl.BlockSpec((TM, D), lambda t: (t + off, 0)),
        ],
        out_specs=pl.BlockSpec((TM, D), lambda t: (t + off, 0)),
        out_shape=jax.ShapeDtypeStruct((N_TOK, D), F32),
        input_output_aliases={4: 0},
        compiler_params=_cparams(1),
        name="outproj_residual",
    )(z, w_bf16, bias.reshape(1, D), mod, x)


def _dft_mats(L):
    n2 = 2 * L
    k = np.arange(L)[:, None].astype(np.float64)
    n = np.arange(n2)[None, :].astype(np.float64)
    ang = 2.0 * np.pi * k * n / n2
    full = np.concatenate([np.cos(ang), -np.sin(ang)], axis=0)
    full[L, :] = np.cos(np.pi * np.arange(n2))
    fwd = full[:, :L]
    bwd = np.zeros((n2, L))
    bwd[:, 1:] = full[:, n2 - np.arange(1, L)]
    t = np.arange(L)[:, None].astype(np.float64)
    kk = np.arange(L)[None, :].astype(np.float64)
    ang_i = 2.0 * np.pi * t * kk / n2
    inv_re = np.cos(ang_i) / L
    inv_re[:, 0] = 1.0 / n2
    inv_im = -np.sin(ang_i) / L
    inv_im[:, 0] = np.cos(np.pi * np.arange(L)) / n2
    inv = np.concatenate([inv_re, inv_im], axis=1)
    return tuple(jnp.asarray(m, F32).astype(BF16) for m in (fwd, np.concatenate([fwd, bwd], axis=1), inv))


def _hyena_pos_emb(L):
    pos = np.arange(L, dtype=np.float64)
    bands = np.linspace(1e-4, HY_BANDS - 1, HY_BANDS)
    ang = (2.0 * np.pi * pos / L)[:, None] * bands[None, :]
    z = np.concatenate([(pos / L)[:, None], np.cos(ang), -np.sin(ang)], axis=-1)
    zp = np.zeros((L, HY_EMB_PAD))
    zp[:, :HY_EMB] = z
    return jnp.asarray(zp, F32)


def _hyena_filter_kernel(z_ref, w1_ref, b1_ref, fr_ref, w2_ref, b2_ref, w3f_ref, w3b_ref, ff_ref, o_ref, *, L, tn):
    j = pl.program_id(1)
    fr = fr_ref[...]
    f = jnp.sin(fr * (_dot_precise(z_ref[...], w1_ref[...]) + b1_ref[...]))
    f = jnp.sin(fr * (_dot_precise(f, w2_ref[...]) + b2_ref[...]))
    t_lin = lax.broadcasted_iota(jnp.int32, (L, tn), 0).astype(F32) / float(L - 1)
    ch = (lax.broadcasted_iota(jnp.int32, (L, tn), 1) + j * tn).astype(F32)
    max_decay = math.log(HY_DECAY_TARGET) / HY_FAST_DECAY
    min_decay = math.log(HY_DECAY_TARGET) / HY_SLOW_DECAY
    deltas = min_decay + ch * ((max_decay - min_decay) / float(D - 1))
    window = jnp.exp(-t_lin * jnp.abs(deltas))
    kf = _dot_precise(f, w3f_ref[...]) * window
    kb = _dot_precise(f, w3b_ref[...]) * window
    taps = jnp.concatenate([kf, kb], axis=0).astype(BF16)
    o_ref[...] = _dot(ff_ref[...], taps)


def _hyena_filters(L, ff, f_w1, f_b1, f_freq, f_w2, f_b2, f_w3):
    tn = 512
    nj = D // tn
    w1p = jnp.zeros((HY_EMB_PAD, HY_HID), F32).at[:HY_EMB].set(f_w1)
    kern = functools.partial(_hyena_filter_kernel, L=L, tn=tn)
    return pl.pallas_call(
        kern,
        grid=(2, nj),
        in_specs=[
            _full_spec((L, HY_EMB_PAD)),
            _full_spec((HY_EMB_PAD, HY_HID)),
            _vec_spec(HY_HID),
            _vec_spec(HY_HID),
            _full_spec((HY_HID, HY_HID)),
            _vec_spec(HY_HID),
            pl.BlockSpec((HY_HID, tn), lambda o, j: (0, o * nj + j)),
            pl.BlockSpec((HY_HID, tn), lambda o, j: (0, (2 + o) * nj + j)),
            _full_spec((2 * L, 2 * L)),
        ],
        out_specs=pl.BlockSpec((None, 2 * L, tn), lambda o, j: (o, 0, j)),
        out_shape=jax.ShapeDtypeStruct((2, 2 * L, D), F32),
        compiler_params=_cparams(2),
        name=f"hyena_filters_L{L}",
    )(_hyena_pos_emb(L), w1p, f_b1.reshape(1, -1), f_freq.reshape(1, -1), f_w2, f_b2.reshape(1, -1),
      f_w3, f_w3, ff)


def _hyena_in_kernel(x_ref, g_ref, sc_ref, sh_ref, w_ref, b_ref, cw_ref, cb_ref, o_ref, h_scr):
    t = pl.program_id(0)

    @pl.when(pl.program_id(1) == 0)
    def _():
        h_scr[...] = _norm_mod(x_ref[...], g_ref[...], sc_ref[...], sh_ref[...]).astype(BF16)

    u = _dot(h_scr[...], w_ref[...]) + b_ref[...]
    seq = jnp.where(t < N_CTX // TM_BIG, CTX_L, LAT_L)
    pos = lax.broadcasted_iota(jnp.int32, u.shape, 0) & (seq - 1)
    prev = jnp.where(pos == 0, 0.0, pltpu.roll(u, 1, 0))
    nxt = jnp.where(pos == seq - 1, 0.0, pltpu.roll(u, TM_BIG - 1, 0))
    cw = cw_ref[...]
    o_ref[...] = prev * cw[0:1] + u * cw[1:2] + nxt * cw[2:3] + cb_ref[...]


def _hyena_in(x, mod, layer, norm_g, w_in_bf16, b_in, conv_w, conv_b):
    return pl.pallas_call(
        _hyena_in_kernel,
        grid=(N_TOK // TM_BIG, 3),
        in_specs=[
            pl.BlockSpec((TM_BIG, D), lambda t, p: (t, 0)),
            _vec_spec(D),
            _mod_spec(layer, 1, _row_big),
            _mod_spec(layer, 0, _row_big),
            pl.BlockSpec((D, D), lambda t, p: (0, p)),
            pl.BlockSpec((1, D), lambda t, p: (0, p)),
            pl.BlockSpec((3, D), lambda t, p: (0, p)),
            pl.BlockSpec((1, D), lambda t, p: (0, p)),
        ],
        out_specs=pl.BlockSpec((TM_BIG, D), lambda t, p: (t, p)),
        out_shape=jax.ShapeDtypeStruct((N_TOK, 3 * D), F32),
        scratch_shapes=[pltpu.VMEM((TM_BIG, D), BF16)],
        compiler_params=_cparams(2),
        name="hyena_in",
    )(x, norm_g.reshape(1, D), mod, mod, w_in_bf16, b_in.reshape(1, -1), conv_w, conv_b.reshape(1, -1))


def _hyena_conv_kernel(v_ref, x1_ref, x2_ref, kh_ref, skip_ref, fwd_ref, inv_ref, o_ref, *, L):
    fwd = fwd_ref[...]
    inv = inv_ref[...]
    row0 = lax.broadcasted_iota(jnp.int32, (L, v_ref.shape[1]), 0) == 0

    def long_conv(z, order):
        zh = _dot(fwd, z.astype(BF16))
        zr, zi = zh[:L], zh[L:]
        kr, ki = kh_ref[order, :L, :], kh_ref[order, L:, :]
        pr = jnp.where(row0, zr * kr, zr * kr - zi * ki)
        pi = jnp.where(row0, zi * ki, zr * ki + zi * kr)
        prod = jnp.concatenate([pr, pi], axis=0).astype(BF16)
        return _dot(inv, prod) + z * skip_ref[order:order + 1, :]

    z = x1_ref[...] * long_conv(v_ref[...], 0)
    z = x2_ref[...] * long_conv(z, 1)
    o_ref[...] = z.astype(BF16)


def _hyena_conv(u, khat, skip, fwd, inv, L, n_batch, row_off, tn):
    nj = D // tn
    rb = row_off // L
    kern = functools.partial(_hyena_conv_kernel, L=L)
    return pl.pallas_call(
        kern,
        grid=(nj, n_batch),
        in_specs=[
            pl.BlockSpec((L, tn), lambda j, b: (rb + b, j)),
            pl.BlockSpec((L, tn), lambda j, b: (rb + b, nj + j)),
            pl.BlockSpec((L, tn), lambda j, b: (rb + b, 2 * nj + j)),
            pl.BlockSpec((2, 2 * L, tn), lambda j, b: (0, 0, j)),
            pl.BlockSpec((2, tn), lambda j, b: (0, j)),
            _full_spec((2 * L, L)),
            _full_spec((L, 2 * L)),
        ],
        out_specs=pl.BlockSpec((L, tn), lambda j, b: (b, j)),
        out_shape=jax.ShapeDtypeStruct((n_batch * L, D), BF16),
        compiler_params=_cparams(2),
        name=f"hyena_conv_L{L}",
    )(u, u, u, khat, skip, fwd, inv)


GLA_PROJ = 2 * GLA_H * GLA_DK + 2 * GLA_H * GLA_DV
GLA_COLS = GLA_PROJ + 2 * GLA_H * GLA_DK
GK1_PAD = 128


def _gla_proj_kernel(x_ref, g_ref, sc_ref, sh_ref, w_ref, wg1_ref, wg2_ref, bg_ref, o_ref):
    h = _norm_mod(x_ref[...], g_ref[...], sc_ref[...], sh_ref[...]).astype(BF16)
    p = _dot(h, w_ref[...])
    nq = GLA_H * GLA_DK
    o_ref[:, 0:nq] = p[:, 0:nq] * (GLA_DK ** -0.5)
    o_ref[:, nq:nq + nq + GLA_H * GLA_DV] = p[:, nq:nq + nq + GLA_H * GLA_DV]
    o_ref[:, 2 * nq + GLA_H * GLA_DV:GLA_PROJ] = _silu(p[:, 2 * nq + GLA_H * GLA_DV:GLA_PROJ])
    low = _dot(h, wg1_ref[...]).astype(BF16)
    gk = _dot(low, wg2_ref[...]) + bg_ref[...]
    o_ref[:, GLA_PROJ:GLA_COLS] = _log_sigmoid(gk) / GLA_NORMALIZER


def _gla_proj(x, mod, layer, norm_g, w_cat, wg1, wg2, bg):
    return pl.pallas_call(
        _gla_proj_kernel,
        grid=(N_TOK // TM,),
        in_specs=[
            pl.BlockSpec((TM, D), lambda t: (t, 0)),
            _vec_spec(D),
            _mod_spec(layer, 1, _row_tm),
            _mod_spec(layer, 0, _row_tm),
            _full_spec((D, GLA_PROJ)),
            _full_spec((D, GK1_PAD)),
            _full_spec((GK1_PAD, 2 * GLA_H * GLA_DK)),
            _vec_spec(2 * GLA_H * GLA_DK),
        ],
        out_specs=pl.BlockSpec((TM, GLA_COLS), lambda t: (t, 0)),
        out_shape=jax.ShapeDtypeStruct((N_TOK, GLA_COLS), F32),
        compiler_params=_cparams(1),
        name="gla_proj",
    )(x, norm_g.reshape(1, D), mod, mod, w_cat, wg1, wg2, bg)


def _gla_core_kernel(*refs, L, has_s0, hps):
    if has_s0:
        q_ref, k_ref, v_ref, g_ref, gkf_ref, gkb_ref, tri_ref, ng_ref, s0_ref, o_ref, sf_ref, acc = refs
    else:
        q_ref, k_ref, v_ref, g_ref, gkf_ref, gkb_ref, tri_ref, ng_ref, o_ref, sf_ref, acc = refs
        s0_ref = None
    C = GLA_CHUNK
    n = L // C
    ones = jnp.ones((C, GLA_DK), BF16)
    ri = lax.broadcasted_iota(jnp.int32, (C, C), 0)
    ci = lax.broadcasted_iota(jnp.int32, (C, C), 1)
    tn_dims = (((0,), (0,)), ((), ()))

    for hh in range(hps):
        kc = slice(hh * GLA_DK, (hh + 1) * GLA_DK)
        vc = slice(hh * GLA_DV, (hh + 1) * GLA_DV)
        for direction, gk_ref in enumerate((gkf_ref, gkb_ref)):
            tri = tri_ref[direction]
            keep = (ci <= ri) if direction == 0 else (ci >= ri)
            last = C - 1 if direction == 0 else 0
            s = s0_ref[direction, hh] if has_s0 else jnp.zeros((GLA_DK, GLA_DV), F32)
            order = range(n) if direction == 0 else range(n - 1, -1, -1)
            for c in order:
                rows = slice(c * C, (c + 1) * C)
                gk_hi, gk_lo = _split(gk_ref[rows, kc])
                b = _dot(tri, gk_hi) + _dot(tri, gk_lo)
                b_last = b[last:last + 1, :]
                q = q_ref[rows, kc]
                k = k_ref[rows, kc]
                v = v_ref[rows, vc].astype(BF16)
                qe = (q * jnp.exp(b)).astype(BF16)
                ke = (k * jnp.exp(-b)).astype(BF16)
                kd = (k * jnp.exp(b_last - b)).astype(BF16)
                scores = lax.dot_general(qe, ke, (((1,), (1,)), ((), ())), preferred_element_type=F32)
                scores = jnp.where(keep, scores, 0.0).astype(BF16)
                o = _dot(scores, v) + _dot(qe, s.astype(BF16))
                if direction == 0:
                    acc[rows, vc] = o
                else:
                    acc[rows, vc] = acc[rows, vc] + o
                tot = (lax.dot_general(gk_hi, ones, tn_dims, preferred_element_type=F32)
                       + lax.dot_general(gk_lo, ones, tn_dims, preferred_element_type=F32))
                dec = jnp.exp(tot)
                dec = jnp.concatenate([dec, dec], axis=1)
                s = dec * s + lax.dot_general(kd, v, tn_dims, preferred_element_type=F32)
            sf_ref[direction, hh] = s

        o = acc[:, vc]
        o = o * lax.rsqrt(jnp.mean(o * o, axis=-1, keepdims=True) + EPS) * ng_ref[...]
        o_ref[:, vc] = (o * g_ref[:, vc]).astype(BF16)


def _gla_core(proj, tri, norm_g, s0, L, n_batch, row_off, hps):
    rb = row_off // L
    H = GLA_H
    nh = H // hps
    has_s0 = s0 is not None
    kern = functools.partial(_gla_core_kernel, L=L, has_s0=has_s0, hps=hps)
    kb, vb = GLA_DK * hps, GLA_DV * hps
    in_specs = [
        pl.BlockSpec((L, kb), lambda b, h: (rb + b, h)),
        pl.BlockSpec((L, kb), lambda b, h: (rb + b, nh + h)),
        pl.BlockSpec((L, vb), lambda b, h: (rb + b, (2 * H * GLA_DK) // vb + h)),
        pl.BlockSpec((L, vb), lambda b, h: (rb + b, (2 * H * GLA_DK) // vb + nh + h)),
        pl.BlockSpec((L, kb), lambda b, h: (rb + b, GLA_PROJ // kb + h)),
        pl.BlockSpec((L, kb), lambda b, h: (rb + b, GLA_PROJ // kb + nh + h)),
        _full_spec((2, GLA_CHUNK, GLA_CHUNK)),
        _vec_spec(GLA_DV),
    ]
    args = [proj] * 6 + [tri, norm_g.reshape(1, GLA_DV)]
    state_spec = pl.BlockSpec((None, 2, hps, GLA_DK, GLA_DV), lambda b, h: (b, 0, h, 0, 0))
    if has_s0:
        in_specs.append(state_spec)
        args.append(s0)
    return pl.pallas_call(
        kern,
        grid=(n_batch, nh),
        in_specs=in_specs,
        out_specs=[pl.BlockSpec((L, vb), lambda b, h: (b, h)), state_spec],
        out_shape=[
            jax.ShapeDtypeStruct((n_batch * L, H * GLA_DV), BF16),
            jax.ShapeDtypeStruct((n_batch, 2, H, GLA_DK, GLA_DV), F32),
        ],
        scratch_shapes=[pltpu.VMEM((L, vb), F32)],
        compiler_params=_cparams(2),
        name=f"gla_core_L{L}",
    )(*args)


def _fnet_mats(L):
    c = np.arange(FNET_C)
    ang_c = 2.0 * np.pi * np.outer(c, c) / FNET_C
    chan = np.concatenate([np.cos(ang_c), np.sin(ang_c)], axis=1) / math.sqrt(FNET_C)
    t = np.arange(L)
    ang_l = 2.0 * np.pi * np.outer(t, t) / L
    seq = np.concatenate([np.cos(ang_l), -np.sin(ang_l)], axis=1) / math.sqrt(L)
    return jnp.asarray(chan, F32).astype(BF16), jnp.asarray(seq, F32).astype(BF16)


def _fnet_kernel(x_ref, g_ref, sc_ref, sh_ref, gate_ref, chan_ref, seq_ref, w_ref, b_ref, o_ref):
    x = x_ref[...]
    h = _norm_mod(x, g_ref[...], sc_ref[...], sh_ref[...]).astype(BF16)
    chan = chan_ref[...]
    cos_parts, sin_parts = [], []
    for gi in range(FNET_GROUPS):
        cs = _dot(h[:, gi * FNET_C:(gi + 1) * FNET_C], chan)
        cos_parts.append(cs[:, :FNET_C])
        sin_parts.append(cs[:, FNET_C:])
    stacked = jnp.concatenate([jnp.concatenate(cos_parts, axis=1), jnp.concatenate(sin_parts, axis=1)], axis=0)
    mixed = _dot(seq_ref[...], stacked.astype(BF16))
    y = _dot(mixed.astype(BF16), w_ref[...]) + b_ref[...]
    o_ref[...] = x + gate_ref[...] * y


def _fnet(x, mod, layer, norm_g, chan, seq, w_bf16, bias, L, n_batch, row_off):
    rb = row_off // L
    lat = row_off > 0

    def row_fn(b):
        return 1 + b if lat else 0

    return pl.pallas_call(
        _fnet_kernel,
        grid=(n_batch,),
        in_specs=[
            pl.BlockSpec((L, D), lambda b: (rb + b, 0)),
            _vec_spec(D),
            _mod_spec(layer, 1, row_fn),
            _mod_spec(layer, 0, row_fn),
            _mod_spec(layer, 2, row_fn),
            _full_spec((FNET_C, 2 * FNET_C)),
            _full_spec((L, 2 * L)),
            _full_spec((D, D)),
            _vec_spec(D),
        ],
        out_specs=pl.BlockSpec((L, D), lambda b: (rb + b, 0)),
        out_shape=jax.ShapeDtypeStruct((N_TOK, D), F32),
        input_output_aliases={0: 0},
        compiler_params=_cparams(1),
        name=f"fnet_L{L}",
    )(x, norm_g.reshape(1, D), mod, mod, mod, chan, seq, w_bf16, bias.reshape(1, D))


def _window_bounds(n, k):
    t = np.arange(n)
    lo, hi = k // 2, k - k // 2 - 1
    return np.maximum(t - lo, 0), np.minimum(t + hi + 1, n)


def _pool_mats(L, grid_rows):
    mats, inv = [], []
    for k in POOL_WINDOWS:
        if grid_rows is None:
            s, e = _window_bounds(L, k)
            idx = np.arange(L)[None, :]
            m = ((idx >= s[:, None]) & (idx < e[:, None])).astype(np.float64)
            cnt = (e - s).astype(np.float64)
        else:
            sr, er = _window_bounds(grid_rows, k)
            sc, ec = _window_bounds(GRID_W, k)
            ir = np.arange(grid_rows)[None, :]
            ic = np.arange(GRID_W)[None, :]
            mr = ((ir >= sr[:, None]) & (ir < er[:, None])).astype(np.float64)
            mc = ((ic >= sc[:, None]) & (ic < ec[:, None])).astype(np.float64)
            m = np.kron(mr, mc)
            cnt = np.kron((er - sr).astype(np.float64), (ec - sc).astype(np.float64))
        mats.append(m)
        inv.append(1.0 / cnt)
    return jnp.asarray(np.stack(mats), BF16), jnp.asarray(np.stack(inv)[:, :, None], F32)


def _pool_kernel(x_ref, g_ref, sc_ref, sh_ref, gate_ref, m_ref, ic_ref, w_ref, b_ref, ps_ref, o_ref):
    x = x_ref[...]
    h = _norm_mod(x, g_ref[...], sc_ref[...], sh_ref[...])
    outs = []
    for gi in range(len(POOL_WINDOWS)):
        hg = h[:, gi * POOL_G:(gi + 1) * POOL_G]
        hi, lo = _split(hg)
        m = m_ref[gi]
        mean = (_dot(m, hi) + _dot(m, lo)) * ic_ref[gi]
        outs.append(_dot((mean - hg).astype(BF16), w_ref[gi]))
    y = (jnp.concatenate(outs, axis=1) + b_ref[...]) * ps_ref[...]
    o_ref[...] = x + gate_ref[...] * y


def _pool(x, mod, layer, norm_g, mats, inv_cnt, w_bf16, bias, scale, L, n_batch, row_off):
    rb = row_off // L
    lat = row_off > 0
    G = len(POOL_WINDOWS)

    def row_fn(b):
        return 1 + b if lat else 0

    return pl.pallas_call(
        _pool_kernel,
        grid=(n_batch,),
        in_specs=[
            pl.BlockSpec((L, D), lambda b: (rb + b, 0)),
            _vec_spec(D),
            _mod_spec(layer, 1, row_fn),
            _mod_spec(layer, 0, row_fn),
            _mod_spec(layer, 2, row_fn),
            _full_spec((G, L, L)),
            _full_spec((G, L, 1)),
            _full_spec((G, POOL_G, POOL_G)),
            _vec_spec(D),
            _vec_spec(D),
        ],
        out_specs=pl.BlockSpec((L, D), lambda b: (rb + b, 0)),
        out_shape=jax.ShapeDtypeStruct((N_TOK, D), F32),
        input_output_aliases={0: 0},
        compiler_params=_cparams(1),
        name=f"pool_L{L}",
    )(x, norm_g.reshape(1, D), mod, mod, mod, mats, inv_cnt, w_bf16, bias.reshape(1, D), scale.reshape(1, D))


ROUTER_PAD = 128
LANES = 128
D_EXT = D + LANES
MOE_TILE = 1024
MOE_MAX_TILES = N_TOK // MOE_TILE + MOE_GROUPS
MOE_ROWS = MOE_MAX_TILES * MOE_TILE
MOE_Y_ROWS = N_TOK + 2 * MOE_TILE
MOE_DMA_PER_STEP = MOE_TILE // MOE_PER_GROUP


def _moe_route_kernel(x_ref, g_ref, sc_ref, sh_ref, wr_ref, tri_ref, h3_ref, route_ref, cnt_ref, carry):
    t = pl.program_id(0)

    @pl.when(t == 0)
    def _():
        carry[...] = jnp.zeros_like(carry)

    h = _norm_mod(x_ref[...], g_ref[...], sc_ref[...], sh_ref[...])
    logits = _dot_precise(h, wr_ref[...])
    lane = lax.broadcasted_iota(jnp.int32, logits.shape, 1)
    neg = jnp.float32(-jnp.inf)
    big = jnp.int32(ROUTER_PAD)
    is_grp = (lane >= MOE_E) & (lane < MOE_E + MOE_GROUPS)
    gl = jnp.where(is_grp, logits, neg)
    g_max = jnp.max(gl, axis=-1, keepdims=True)
    g_idx = jnp.min(jnp.where(gl == g_max, lane, big), axis=-1, keepdims=True) - MOE_E
    p_grp = 1.0 / jnp.sum(jnp.where(is_grp, jnp.exp(gl - g_max), 0.0), axis=-1, keepdims=True)
    in_grp = (lane < MOE_E) & ((lane >> 2) == g_idx)
    el = jnp.where(in_grp, logits, neg)
    m1 = jnp.max(el, axis=-1, keepdims=True)
    i1 = jnp.min(jnp.where(el == m1, lane, big), axis=-1, keepdims=True)
    z = jnp.sum(jnp.where(in_grp, jnp.exp(el - m1), 0.0), axis=-1, keepdims=True)
    el2 = jnp.where(lane == i1, neg, el)
    m2 = jnp.max(el2, axis=-1, keepdims=True)
    i2 = jnp.min(jnp.where(el2 == m2, lane, big), axis=-1, keepdims=True)
    p1 = 1.0 / z
    p2 = jnp.exp(m2 - m1) / z
    tot = p1 + p2
    eid = lane + MOE_PER_GROUP * g_idx
    in4 = lane < MOE_PER_GROUP
    cw4 = (jnp.where(in4 & (eid == i1), p_grp * (p1 / tot), 0.0)
           + jnp.where(in4 & (eid == i2), p_grp * (p2 / tot), 0.0))
    member = jnp.where(lane == g_idx, 1.0, 0.0).astype(BF16)
    before = _dot(tri_ref[...], member) + carry[...]
    rank = jnp.sum(jnp.where(lane == g_idx, before, 0.0), axis=-1, keepdims=True)
    carry[...] = carry[...] + jnp.sum(member.astype(F32), axis=0, keepdims=True)
    cnt_ref[...] = carry[...].astype(jnp.int32)
    route_ref[...] = jnp.where(lane == 0, g_idx, jnp.where(lane == 1, rank.astype(jnp.int32), 0))
    h3_ref[:, :D] = h
    h3_ref[:, D:] = cw4


def _moe_route(x, mod, layer, norm_g, w_router, tri):
    return pl.pallas_call(
        _moe_route_kernel,
        grid=(N_TOK // TM,),
        in_specs=[
            pl.BlockSpec((TM, D), lambda t: (t, 0)),
            _vec_spec(D),
            _mod_spec(layer, 4, _row_tm),
            _mod_spec(layer, 3, _row_tm),
            _full_spec((D, ROUTER_PAD)),
            _full_spec((TM, TM)),
        ],
        out_specs=[
            pl.BlockSpec((TM, D_EXT), lambda t: (t, 0)),
            pl.BlockSpec((TM, LANES), lambda t: (t, 0)),
            pl.BlockSpec((1, LANES), lambda t: (0, 0)),
        ],
        out_shape=[
            jax.ShapeDtypeStruct((N_TOK, D_EXT), F32),
            jax.ShapeDtypeStruct((N_TOK, LANES), jnp.int32),
            jax.ShapeDtypeStruct((1, LANES), jnp.int32),
        ],
        scratch_shapes=[pltpu.VMEM((1, LANES), F32)],
        compiler_params=_cparams(1),
        name="moe_route",
    )(x, norm_g.reshape(1, D), mod, mod, w_router, tri)


def _moe_invert_kernel(pos_ref, pad_lo_ref, pad_hi_ref, src_ref):
    def mark(i, carry):
        src_ref[i] = jnp.int32(N_TOK)
        return carry

    for g in range(MOE_GROUPS + 1):
        lax.fori_loop(pad_lo_ref[g], pad_hi_ref[g], mark, 0)

    def place(n, carry):
        src_ref[pos_ref[n]] = n
        return carry

    lax.fori_loop(0, N_TOK, place, 0, unroll=8)


def _moe_invert(pos, pad_lo, pad_hi):
    smem = pl.BlockSpec(memory_space=pltpu.SMEM)
    return pl.pallas_call(
        _moe_invert_kernel,
        in_specs=[smem, smem, smem],
        out_specs=smem,
        out_shape=jax.ShapeDtypeStruct((MOE_ROWS,), jnp.int32),
        name="moe_invert",
    )(pos, pad_lo, pad_hi)


def _moe_expert_kernel(src_ref, grp_ref, nact_ref, h_hbm, w1_ref, w3_ref, w2_ref, y_hbm,
                       xb0, xb1, ab0, ab1, gsem, ssem):
    t = pl.program_id(0)
    k = pl.program_id(1)
    n_active = nact_ref[0]
    last = n_active - 1
    T = MOE_TILE
    xbufs, accs = (xb0, xb1), (ab0, ab1)

    def gather_row(tile, slot, r):
        tok = jnp.minimum(src_ref[tile * T + r], N_TOK - 1)
        return pltpu.make_async_copy(h_hbm.at[pl.ds(tok, 1), :], xbufs[slot].at[pl.ds(r, 1), :], gsem.at[slot])

    def scatter_row(tile, slot, r, real):
        tok = src_ref[tile * T + r]
        dst = jnp.where(real & (tok < N_TOK), tok, N_TOK + slot * T + r)
        return pltpu.make_async_copy(accs[slot].at[pl.ds(r, 1), :], y_hbm.at[pl.ds(dst, 1), :], ssem.at[slot])

    def wait_gather(slot):
        pltpu.make_async_copy(h_hbm.at[pl.ds(0, T), :], xbufs[slot], gsem.at[slot]).wait()

    def wait_scatter(slot):
        pltpu.make_async_copy(accs[slot], y_hbm.at[pl.ds(0, T), :], ssem.at[slot]).wait()

    def step(slot):
        other = 1 - slot
        xb, acc = xbufs[slot], accs[slot]

        @pl.when(k == 0)
        def _():
            if slot == 0:
                @pl.when(t == 0)
                def _():
                    def first(r, carry):
                        gather_row(0, 0, r).start()
                        return carry

                    lax.fori_loop(0, T, first, 0, unroll=8)
                    ab0[...] = jnp.zeros_like(ab0)
                    ab1[...] = jnp.zeros_like(ab1)
                    dumps = [pltpu.make_async_copy(ab1, y_hbm.at[pl.ds(N_TOK + s * T, T), :], ssem.at[1])
                             for s in (0, 1)]
                    for cp in dumps:
                        cp.start()
                    for cp in dumps:
                        cp.wait()

            wait_gather(slot)

            @pl.when(t >= 1)
            def _():
                wait_scatter(slot)

        nxt = jnp.minimum(t + 1, last)
        prev = jnp.maximum(t - 1, 0)
        for r in range(MOE_DMA_PER_STEP):
            row = k * MOE_DMA_PER_STEP + r
            gather_row(nxt, other, row).start()
            scatter_row(prev, other, row, t >= 1).start()

        x = xb[:, :D].astype(BF16)
        a = _dot(x, w1_ref[...].astype(BF16))
        b = _dot(x, w3_ref[...].astype(BF16))
        lane = lax.broadcasted_iota(jnp.int32, (T, LANES), 1)
        cwk = jnp.sum(jnp.where(lane == k, xb[:, D:], 0.0), axis=-1, keepdims=True)
        hid = (_silu(a) * b * cwk).astype(BF16)
        acc[...] = jnp.where(k > 0, acc[...], 0.0) + _dot(hid, w2_ref[...].astype(BF16))

        @pl.when((k == MOE_PER_GROUP - 1) & (t == last))
        def _():
            wait_gather(other)
            wait_scatter(other)

            def final(r, carry):
                scatter_row(t, slot, r, True).start()
                return carry

            lax.fori_loop(0, T, final, 0, unroll=8)
            wait_scatter(slot)

    for slot in (0, 1):
        pl.when((t < n_active) & (t % 2 == slot))(functools.partial(step, slot))


def _moe_experts(h_ext, src, tile_group, n_active, layer, w1, w3, w2):
    T = MOE_TILE

    def w_index(t, k, src_ref, grp_ref, nact_ref):
        last = nact_ref[0] - 1
        e = jnp.where(t <= last, grp_ref[t] * MOE_PER_GROUP + k, grp_ref[last] * MOE_PER_GROUP + MOE_PER_GROUP - 1)
        return (layer, e, 0, 0)

    grid_spec = pltpu.PrefetchScalarGridSpec(
        num_scalar_prefetch=3,
        grid=(MOE_MAX_TILES, MOE_PER_GROUP),
        in_specs=[
            pl.BlockSpec(memory_space=pl.ANY),
            pl.BlockSpec((None, None, D, MOE_HID), w_index),
            pl.BlockSpec((None, None, D, MOE_HID), w_index),
            pl.BlockSpec((None, None, MOE_HID, D), w_index),
        ],
        out_specs=pl.BlockSpec(memory_space=pl.ANY),
        scratch_shapes=[
            pltpu.VMEM((T, D_EXT), F32),
            pltpu.VMEM((T, D_EXT), F32),
            pltpu.VMEM((T, D), F32),
            pltpu.VMEM((T, D), F32),
            pltpu.SemaphoreType.DMA((2,)),
            pltpu.SemaphoreType.DMA((2,)),
        ],
    )
    return pl.pallas_call(
        _moe_expert_kernel,
        grid_spec=grid_spec,
        out_shape=jax.ShapeDtypeStruct((MOE_Y_ROWS, D), F32),
        compiler_params=_cparams(2),
        name="moe_experts",
    )(src, tile_group, n_active, h_ext, w1, w3, w2)


def _moe_combine_kernel(y_ref, gate_ref, x_ref, o_ref):
    o_ref[...] = x_ref[...] + gate_ref[...] * y_ref[...]


def _moe_combine(x, y3, mod, layer):
    return pl.pallas_call(
        _moe_combine_kernel,
        grid=(N_TOK // TM,),
        in_specs=[
            pl.BlockSpec((TM, D), lambda t: (t, 0)),
            _mod_spec(layer, 5, _row_tm),
            pl.BlockSpec((TM, D), lambda t: (t, 0)),
        ],
        out_specs=pl.BlockSpec((TM, D), lambda t: (t, 0)),
        out_shape=jax.ShapeDtypeStruct((N_TOK, D), F32),
        input_output_aliases={2: 0},
        compiler_params=_cparams(1),
        name="moe_combine",
    )(y3, mod, x)


def _moe(x, mod, layer, norm_g, w_rg, w_re, w1, w3, w2, tri):
    w_router = jnp.zeros((D, ROUTER_PAD), F32).at[:, :MOE_E].set(w_re).at[:, MOE_E:MOE_E + MOE_GROUPS].set(w_rg)
    h3, route, counts = _moe_route(x, mod, layer, norm_g, w_router, tri)
    cnt = counts[0, :MOE_GROUPS]
    ntile = (cnt + MOE_TILE - 1) // MOE_TILE
    tile_end = jnp.cumsum(ntile)
    seg_start = (tile_end - ntile) * MOE_TILE
    g_idx, rank = route[:, 0], route[:, 1]
    pos = jnp.sum(jnp.where(g_idx[:, None] == jnp.arange(MOE_GROUPS)[None, :], seg_start[None, :], 0), axis=1) + rank
    tiles = jnp.arange(MOE_MAX_TILES, dtype=jnp.int32)
    tile_group = jnp.minimum(jnp.sum(tiles[:, None] >= tile_end[None, :], axis=1), MOE_GROUPS - 1).astype(jnp.int32)
    n_active = tile_end[-1:].astype(jnp.int32)
    seg_end = tile_end * MOE_TILE
    pad_lo = jnp.concatenate([seg_start + cnt, seg_end[-1:]]).astype(jnp.int32)
    pad_hi = jnp.concatenate([seg_end, jnp.full((1,), MOE_ROWS)]).astype(jnp.int32)
    src = _moe_invert(pos.astype(jnp.int32), pad_lo, pad_hi)
    y3 = _moe_experts(h3, src, tile_group, n_active, layer, w1, w3, w2)
    return _moe_combine(x, y3, mod, layer)


def _final_norm_kernel(x_ref, g_ref, o_ref):
    x = x_ref[...]
    o_ref[...] = x * lax.rsqrt(jnp.mean(x * x, axis=-1, keepdims=True) + EPS) * g_ref[...]


def _final_norm(x, final_g, row_off, n_rows):
    off = row_off // TM
    return pl.pallas_call(
        _final_norm_kernel,
        grid=(n_rows // TM,),
        in_specs=[pl.BlockSpec((TM, D), lambda t: (t + off, 0)), _vec_spec(D)],
        out_specs=pl.BlockSpec((TM, D), lambda t: (t, 0)),
        out_shape=jax.ShapeDtypeStruct((n_rows, D), F32),
        compiler_params=_cparams(1),
        name="final_norm",
    )(x, final_g.reshape(1, D))


def kernel(x_prompt, x_sample, state_gla, c, c_ctx, w_ada, b_ada, norm_g, hy_w_in, hy_b_in, hy_conv_w, hy_conv_b, hy_f_w1, hy_f_b1, hy_f_freq, hy_f_w2, hy_f_b2, hy_f_w3, hy_skip, hy_w_out, hy_b_out, gla_w_q, gla_w_k, gla_w_v, gla_w_g, gla_w_gk1, gla_w_gk2, gla_b_gk, gla_norm_g, gla_w_o, fn_w_out, fn_b_out, pool_w, pool_b, pool_scale, moe_w_rg, moe_w_re, moe_w1, moe_w3, moe_w2, final_g):
    groups = ((CTX_L, CTX_B, 0, None), (LAT_L, LAT_B, N_CTX, LAT_L // GRID_W))

    x = jnp.concatenate([x_prompt.reshape(N_CTX, D), x_sample.reshape(N_LAT, D)], axis=0)
    cond = jnp.zeros((MOD_ROWS, D), F32).at[0].set(c_ctx).at[1:1 + LAT_B].set(c)
    mod = _ada_table(cond, w_ada, b_ada).reshape(DEPTH * MOD_ROWS * 6, 1, D)

    tri_tm = jnp.asarray(np.tril(np.ones((TM, TM)), -1), BF16)

    new_states = []
    for i in range(DEPTH):
        kind, j = i % 4, i // 4
        if kind == 0:
            u = _hyena_in(x, mod, i, norm_g[i, 0], hy_w_in[j].astype(BF16), hy_b_in[j], hy_conv_w[j], hy_conv_b[j])
            w_out = hy_w_out[j].astype(BF16)
            for L, nb, off, _ in groups:
                fwd, ff, inv = _dft_mats(L)
                khat = _hyena_filters(L, ff, hy_f_w1[j], hy_f_b1[j], hy_f_freq[j], hy_f_w2[j], hy_f_b2[j],
                                      hy_f_w3[j])
                z = _hyena_conv(u, khat, hy_skip[j], fwd, inv, L, nb, off, D if L == CTX_L else 512)
                x = _outproj(x, z, w_out, hy_b_out[j], mod, i, off, nb * L)
        elif kind == 1:
            w_cat = jnp.concatenate([gla_w_q[j], gla_w_k[j], gla_w_v[j], gla_w_g[j]], axis=1).astype(BF16)
            nk = GLA_H * GLA_DK
            wg1 = jnp.zeros((D, GK1_PAD), F32).at[:, :GLA_RANK].set(gla_w_gk1[j, 0])
            wg1 = wg1.at[:, GLA_RANK:2 * GLA_RANK].set(gla_w_gk1[j, 1]).astype(BF16)
            wg2 = jnp.zeros((GK1_PAD, 2 * nk), F32).at[:GLA_RANK, :nk].set(gla_w_gk2[j, 0])
            wg2 = wg2.at[GLA_RANK:2 * GLA_RANK, nk:].set(gla_w_gk2[j, 1]).astype(BF16)
            proj = _gla_proj(x, mod, i, norm_g[i, 0], w_cat, wg1, wg2, gla_b_gk[j].reshape(1, 2 * nk))
            lower = np.tril(np.ones((GLA_CHUNK, GLA_CHUNK)))
            tri = jnp.asarray(np.stack([lower, lower.T]), BF16)
            w_o = gla_w_o[j].astype(BF16)
            for L, nb, off, grid_rows in groups:
                s0 = None if grid_rows is None else state_gla[:, j]
                o, s_fin = _gla_core(proj, tri, gla_norm_g[j], s0, L, nb, off, GLA_H if L == CTX_L else 1)
                if grid_rows is None:
                    new_states.append(s_fin)
                x = _outproj(x, o, w_o, jnp.zeros((D,), F32), mod, i, off, nb * L)
        elif kind == 2:
            w_out = fn_w_out[j].astype(BF16)
            for L, nb, off, _ in groups:
                chan, seq = _fnet_mats(L)
                x = _fnet(x, mod, i, norm_g[i, 0], chan, seq, w_out, fn_b_out[j], L, nb, off)
        else:
            w_pool = pool_w[j].astype(BF16)
            for L, nb, off, grid_rows in groups:
                mats, inv_cnt = _pool_mats(L, grid_rows)
                x = _pool(x, mod, i, norm_g[i, 0], mats, inv_cnt, w_pool, pool_b[j], pool_scale[j], L, nb, off)

        x = _moe(x, mod, i, norm_g[i, 1], moe_w_rg[i], moe_w_re[i], moe_w1, moe_w3, moe_w2, tri_tm)

    y_prompt = _final_norm(x, final_g, 0, N_CTX).reshape(CTX_B, CTX_L, D)
    y_sample = _final_norm(x, final_g, N_CTX, N_LAT).reshape(LAT_B, LAT_L, D)
    new_state_gla = jnp.stack(new_states, axis=1)
    return (y_prompt, y_sample, new_state_gla)
```

```python
import functools
import math

import jax
import jax.numpy as jnp
import numpy as np
from jax import lax
from jax.experimental import pallas as pl
from jax.experimental.pallas import tpu as pltpu

F32 = jnp.float32
BF16 = jnp.bfloat16

D = 1024
CTX_B, CTX_L = 32, 256
LAT_B, LAT_L = 2, 1024
N_CTX = CTX_B * CTX_L
N_LAT = LAT_B * LAT_L
N_TOK = N_CTX + N_LAT
DEPTH = 4
GRID_W = 64
EPS = 1e-6

HY_BANDS = 8
HY_EMB = 1 + 2 * HY_BANDS
HY_EMB_PAD = 32
HY_HID = 64
HY_FAST_DECAY = 0.3
HY_SLOW_DECAY = 1.5
HY_DECAY_TARGET = 1e-2

GLA_H = 4
GLA_DK = 128
GLA_DV = 256
GLA_RANK = 16
GLA_NORMALIZER = 16.0
GLA_CHUNK = 64

FNET_GROUPS = 4
FNET_C = D // FNET_GROUPS
POOL_WINDOWS = (2, 4, 8, 16)
POOL_G = D // len(POOL_WINDOWS)

MOE_GROUPS = 4
MOE_PER_GROUP = 4
MOE_E = MOE_GROUPS * MOE_PER_GROUP
MOE_HID = D // 2

MOD_ROWS = 8
TM = 512
TM_BIG = 1024
VMEM_LIMIT = 56 * 1024 * 1024


def _cparams(n_axes):
    return pltpu.CompilerParams(dimension_semantics=("arbitrary",) * n_axes, vmem_limit_bytes=VMEM_LIMIT)


def _norm_mod(x, g, sc, sh):
    ms = jnp.mean(x * x, axis=-1, keepdims=True)
    return (x * lax.rsqrt(ms + EPS) * g) * (1.0 + sc) + sh


def _split(a):
    hi = a.astype(BF16)
    lo = (a - hi.astype(F32)).astype(BF16)
    return hi, lo


def _dot(a, b):
    return jnp.dot(a, b, preferred_element_type=F32)


def _dot_precise(a, b):
    a_hi, a_lo = _split(a)
    b_hi, b_lo = _split(b)
    return _dot(a_hi, b_hi) + (_dot(a_hi, b_lo) + _dot(a_lo, b_hi))


def _silu(x):
    return x * (1.0 / (1.0 + jnp.exp(-x)))


def _log_sigmoid(x):
    return jnp.minimum(x, 0.0) - jnp.log(1.0 + jnp.exp(-jnp.abs(x)))


def _mod_spec(layer, chunk, row_fn):
    base = layer * MOD_ROWS * 6 + chunk

    def index_map(*ids):
        return (base + row_fn(*ids) * 6, 0, 0)

    return pl.BlockSpec((None, 1, D), index_map)


def _row_tm(t, *_):
    return jnp.where(t < N_CTX // TM, 0, 1 + (t - N_CTX // TM) // (LAT_L // TM))


def _row_big(t, *_):
    return jnp.where(t < N_CTX // TM_BIG, 0, 1 + (t - N_CTX // TM_BIG) // (LAT_L // TM_BIG))


def _vec_spec(n):
    return pl.BlockSpec((1, n), lambda *ids: (0, 0))


def _full_spec(shape):
    nd = len(shape)
    return pl.BlockSpec(shape, lambda *ids: (0,) * nd)


def _ada_kernel(cond_ref, w_ref, b_ref, o_ref):
    s = _silu(cond_ref[...]).astype(BF16)
    o_ref[...] = _dot(s, w_ref[...].astype(BF16)) + b_ref[...]


def _ada_table(cond, w_ada, b_ada):
    tn = 1536
    return pl.pallas_call(
        _ada_kernel,
        grid=(DEPTH, 6 * D // tn),
        in_specs=[
            pl.BlockSpec((MOD_ROWS, D), lambda i, j: (0, 0)),
            pl.BlockSpec((None, D, tn), lambda i, j: (i, 0, j)),
            pl.BlockSpec((None, 1, tn), lambda i, j: (i, 0, j)),
        ],
        out_specs=pl.BlockSpec((None, MOD_ROWS, tn), lambda i, j: (i, 0, j)),
        out_shape=jax.ShapeDtypeStruct((DEPTH, MOD_ROWS, 6 * D), F32),
        compiler_params=_cparams(2),
        name="ada_table",
    )(cond, w_ada, b_ada.reshape(DEPTH, 1, 6 * D))


def _outproj_kernel(z_ref, w_ref, b_ref, gate_ref, x_ref, o_ref):
    y = _dot(z_ref[...], w_ref[...]) + b_ref[...]
    o_ref[...] = x_ref[...] + gate_ref[...] * y


def _outproj(x, z, w_bf16, bias, mod, layer, row_off, n_rows):
    k = z.shape[1]
    off = row_off // TM
    first_lat = N_CTX // TM

    def row_fn(t):
        g = t + off
        return jnp.where(g < first_lat, 0, 1 + (g - first_lat) // (LAT_L // TM))

    return pl.pallas_call(
        _outproj_kernel,
        grid=(n_rows // TM,),
        in_specs=[
            pl.BlockSpec((TM, k), lambda t: (t, 0)),
            _full_spec((k, D)),
            _vec_spec(D),
            _mod_spec(layer, 2, row_fn),
            pl.BlockSpec((TM, D), lambda t: (t + off, 0)),
        ],
        out_specs=pl.BlockSpec((TM, D), lambda t: (t + off, 0)),
        out_shape=jax.ShapeDtypeStruct((N_TOK, D), F32),
        input_output_aliases={4: 0},
        compiler_params=_cparams(1),
        name="outproj_residual",
    )(z, w_bf16, bias.reshape(1, D), mod, x)


def _dft_mats(L):
    n2 = 2 * L
    k = np.arange(L)[:, None].astype(np.float64)
    n = np.arange(n2)[None, :].astype(np.float64)
    ang = 2.0 * np.pi * k * n / n2
    full = np.concatenate([np.cos(ang), -np.sin(ang)], axis=0)
    full[L, :] = np.cos(np.pi * np.arange(n2))
    fwd = full[:, :L]
    bwd = np.zeros((n2, L))
    bwd[:, 1:] = full[:, n2 - np.arange(1, L)]
    t = np.arange(L)[:, None].astype(np.float64)
    kk = np.arange(L)[None, :].astype(np.float64)
    ang_i = 2.0 * np.pi * t * kk / n2
    inv_re = np.cos(ang_i) / L
    inv_re[:, 0] = 1.0 / n2
    inv_im = -np.sin(ang_i) / L
    inv_im[:, 0] = np.cos(np.pi * np.arange(L)) / n2
    inv = np.concatenate([inv_re, inv_im], axis=1)
    return tuple(jnp.asarray(m, F32).astype(BF16) for m in (fwd, np.concatenate([fwd, bwd], axis=1), inv))


def _hyena_pos_emb(L):
    pos = np.arange(L, dtype=np.float64)
    bands = np.linspace(1e-4, HY_BANDS - 1, HY_BANDS)
    ang = (2.0 * np.pi * pos / L)[:, None] * bands[None, :]
    z = np.concatenate([(pos / L)[:, None], np.cos(ang), -np.sin(ang)], axis=-1)
    zp = np.zeros((L, HY_EMB_PAD))
    zp[:, :HY_EMB] = z
    return jnp.asarray(zp, F32)


def _hyena_filter_kernel(z_ref, w1_ref, b1_ref, fr_ref, w2_ref, b2_ref, w3f_ref, w3b_ref, ff_ref, o_ref, *, L, tn):
    j = pl.program_id(1)
    fr = fr_ref[...]
    f = jnp.sin(fr * (_dot_precise(z_ref[...], w1_ref[...]) + b1_ref[...]))
    f = jnp.sin(fr * (_dot_precise(f, w2_ref[...]) + b2_ref[...]))
    t_lin = lax.broadcasted_iota(jnp.int32, (L, tn), 0).astype(F32) / float(L - 1)
    ch = (lax.broadcasted_iota(jnp.int32, (L, tn), 1) + j * tn).astype(F32)
    max_decay = math.log(HY_DECAY_TARGET) / HY_FAST_DECAY
    min_decay = math.log(HY_DECAY_TARGET) / HY_SLOW_DECAY
    deltas = min_decay + ch * ((max_decay - min_decay) / float(D - 1))
    window = jnp.exp(-t_lin * jnp.abs(deltas))
    kf = _dot_precise(f, w3f_ref[...]) * window
    kb = _dot_precise(f, w3b_ref[...]) * window
    taps = jnp.concatenate([kf, kb], axis=0).astype(BF16)
    o_ref[...] = _dot(ff_ref[...], taps)


def _hyena_filters(L, ff, f_w1, f_b1, f_freq, f_w2, f_b2, f_w3):
    tn = 512
    nj = D // tn
    w1p = jnp.zeros((HY_EMB_PAD, HY_HID), F32).at[:HY_EMB].set(f_w1)
    kern = functools.partial(_hyena_filter_kernel, L=L, tn=tn)
    return pl.pallas_call(
        kern,
        grid=(2, nj),
        in_specs=[
            _full_spec((L, HY_EMB_PAD)),
            _full_spec((HY_EMB_PAD, HY_HID)),
            _vec_spec(HY_HID),
            _vec_spec(HY_HID),
            _full_spec((HY_HID, HY_HID)),
            _vec_spec(HY_HID),
            pl.BlockSpec((HY_HID, tn), lambda o, j: (0, o * nj + j)),
            pl.BlockSpec((HY_HID, tn), lambda o, j: (0, (2 + o) * nj + j)),
            _full_spec((2 * L, 2 * L)),
        ],
        out_specs=pl.BlockSpec((None, 2 * L, tn), lambda o, j: (o, 0, j)),
        out_shape=jax.ShapeDtypeStruct((2, 2 * L, D), F32),
        compiler_params=_cparams(2),
        name=f"hyena_filters_L{L}",
    )(_hyena_pos_emb(L), w1p, f_b1.reshape(1, -1), f_freq.reshape(1, -1), f_w2, f_b2.reshape(1, -1),
      f_w3, f_w3, ff)


def _hyena_in_kernel(x_ref, g_ref, sc_ref, sh_ref, w_ref, b_ref, cw_ref, cb_ref, o_ref, h_scr):
    t = pl.program_id(0)

    @pl.when(pl.program_id(1) == 0)
    def _():
        h_scr[...] = _norm_mod(x_ref[...], g_ref[...], sc_ref[...], sh_ref[...]).astype(BF16)

    u = _dot(h_scr[...], w_ref[...]) + b_ref[...]
    seq = jnp.where(t < N_CTX // TM_BIG, CTX_L, LAT_L)
    pos = lax.broadcasted_iota(jnp.int32, u.shape, 0) & (seq - 1)
    prev = jnp.where(pos == 0, 0.0, pltpu.roll(u, 1, 0))
    nxt = jnp.where(pos == seq - 1, 0.0, pltpu.roll(u, TM_BIG - 1, 0))
    cw = cw_ref[...]
    o_ref[...] = prev * cw[0:1] + u * cw[1:2] + nxt * cw[2:3] + cb_ref[...]


def _hyena_in(x, mod, layer, norm_g, w_in_bf16, b_in, conv_w, conv_b):
    return pl.pallas_call(
        _hyena_in_kernel,
        grid=(N_TOK // TM_BIG, 3),
        in_specs=[
            pl.BlockSpec((TM_BIG, D), lambda t, p: (t, 0)),
            _vec_spec(D),
            _mod_spec(layer, 1, _row_big),
            _mod_spec(layer, 0, _row_big),
            pl.BlockSpec((D, D), lambda t, p: (0, p)),
            pl.BlockSpec((1, D), lambda t, p: (0, p)),
            pl.BlockSpec((3, D), lambda t, p: (0, p)),
            pl.BlockSpec((1, D), lambda t, p: (0, p)),
        ],
        out_specs=pl.BlockSpec((TM_BIG, D), lambda t, p: (t, p)),
        out_shape=jax.ShapeDtypeStruct((N_TOK, 3 * D), F32),
        scratch_shapes=[pltpu.VMEM((TM_BIG, D), BF16)],
        compiler_params=_cparams(2),
        name="hyena_in",
    )(x, norm_g.reshape(1, D), mod, mod, w_in_bf16, b_in.reshape(1, -1), conv_w, conv_b.reshape(1, -1))


def _hyena_conv_kernel(v_ref, x1_ref, x2_ref, kh_ref, skip_ref, fwd_ref, inv_ref, o_ref, *, L):
    fwd = fwd_ref[...]
    inv = inv_ref[...]
    row0 = lax.broadcasted_iota(jnp.int32, (L, v_ref.shape[1]), 0) == 0

    def long_conv(z, order):
        zh = _dot(fwd, z.astype(BF16))
        zr, zi = zh[:L], zh[L:]
        kr, ki = kh_ref[order, :L, :], kh_ref[order, L:, :]
        pr = jnp.where(row0, zr * kr, zr * kr - zi * ki)
        pi = jnp.where(row0, zi * ki, zr * ki + zi * kr)
        prod = jnp.concatenate([pr, pi], axis=0).astype(BF16)
        return _dot(inv, prod) + z * skip_ref[order:order + 1, :]

    z = x1_ref[...] * long_conv(v_ref[...], 0)
    z = x2_ref[...] * long_conv(z, 1)
    o_ref[...] = z.astype(BF16)


def _hyena_conv(u, khat, skip, fwd, inv, L, n_batch, row_off, tn):
    nj = D // tn
    rb = row_off // L
    kern = functools.partial(_hyena_conv_kernel, L=L)
    return pl.pallas_call(
        kern,
        grid=(nj, n_batch),
        in_specs=[
            pl.BlockSpec((L, tn), lambda j, b: (rb + b, j)),
            pl.BlockSpec((L, tn), lambda j, b: (rb + b, nj + j)),
            pl.BlockSpec((L, tn), lambda j, b: (rb + b, 2 * nj + j)),
            pl.BlockSpec((2, 2 * L, tn), lambda j, b: (0, 0, j)),
            pl.BlockSpec((2, tn), lambda j, b: (0, j)),
            _full_spec((2 * L, L)),
            _full_spec((L, 2 * L)),
        ],
        out_specs=pl.BlockSpec((L, tn), lambda j, b: (b, j)),
        out_shape=jax.ShapeDtypeStruct((n_batch * L, D), BF16),
        compiler_params=_cparams(2),
        name=f"hyena_conv_L{L}",
    )(u, u, u, khat, skip, fwd, inv)


GLA_PROJ = 2 * GLA_H * GLA_DK + 2 * GLA_H * GLA_DV
GLA_COLS = GLA_PROJ + 2 * GLA_H * GLA_DK
GK1_PAD = 128


def _gla_proj_kernel(x_ref, g_ref, sc_ref, sh_ref, w_ref, wg1_ref, wg2_ref, bg_ref, o_ref):
    h = _norm_mod(x_ref[...], g_ref[...], sc_ref[...], sh_ref[...]).astype(BF16)
    p = _dot(h, w_ref[...])
    nq = GLA_H * GLA_DK
    o_ref[:, 0:nq] = p[:, 0:nq] * (GLA_DK ** -0.5)
    o_ref[:, nq:nq + nq + GLA_H * GLA_DV] = p[:, nq:nq + nq + GLA_H * GLA_DV]
    o_ref[:, 2 * nq + GLA_H * GLA_DV:GLA_PROJ] = _silu(p[:, 2 * nq + GLA_H * GLA_DV:GLA_PROJ])
    low = _dot(h, wg1_ref[...]).astype(BF16)
    gk = _dot(low, wg2_ref[...]) + bg_ref[...]
    o_ref[:, GLA_PROJ:GLA_COLS] = _log_sigmoid(gk) / GLA_NORMALIZER


def _gla_proj(x, mod, layer, norm_g, w_cat, wg1, wg2, bg):
    return pl.pallas_call(
        _gla_proj_kernel,
        grid=(N_TOK // TM,),
        in_specs=[
            pl.BlockSpec((TM, D), lambda t: (t, 0)),
            _vec_spec(D),
            _mod_spec(layer, 1, _row_tm),
            _mod_spec(layer, 0, _row_tm),
            _full_spec((D, GLA_PROJ)),
            _full_spec((D, GK1_PAD)),
            _full_spec((GK1_PAD, 2 * GLA_H * GLA_DK)),
            _vec_spec(2 * GLA_H * GLA_DK),
        ],
        out_specs=pl.BlockSpec((TM, GLA_COLS), lambda t: (t, 0)),
        out_shape=jax.ShapeDtypeStruct((N_TOK, GLA_COLS), F32),
        compiler_params=_cparams(1),
        name="gla_proj",
    )(x, norm_g.reshape(1, D), mod, mod, w_cat, wg1, wg2, bg)


def _gla_core_kernel(*refs, L, has_s0, hps):
    if has_s0:
        q_ref, k_ref, v_ref, g_ref, gkf_ref, gkb_ref, tri_ref, ng_ref, s0_ref, o_ref, sf_ref, acc = refs
    else:
        q_ref, k_ref, v_ref, g_ref, gkf_ref, gkb_ref, tri_ref, ng_ref, o_ref, sf_ref, acc = refs
        s0_ref = None
    C = GLA_CHUNK
    n = L // C
    ones = jnp.ones((C, GLA_DK), BF16)
    ri = lax.broadcasted_iota(jnp.int32, (C, C), 0)
    ci = lax.broadcasted_iota(jnp.int32, (C, C), 1)
    tn_dims = (((0,), (0,)), ((), ()))

    for hh in range(hps):
        kc = slice(hh * GLA_DK, (hh + 1) * GLA_DK)
        vc = slice(hh * GLA_DV, (hh + 1) * GLA_DV)
        for direction, gk_ref in enumerate((gkf_ref, gkb_ref)):
            tri = tri_ref[direction]
            keep = (ci <= ri) if direction == 0 else (ci >= ri)
            last = C - 1 if direction == 0 else 0
            s = s0_ref[direction, hh] if has_s0 else jnp.zeros((GLA_DK, GLA_DV), F32)
            order = range(n) if direction == 0 else range(n - 1, -1, -1)
            for c in order:
                rows = slice(c * C, (c + 1) * C)
                gk_hi, gk_lo = _split(gk_ref[rows, kc])
                b = _dot(tri, gk_hi) + _dot(tri, gk_lo)
                b_last = b[last:last + 1, :]
                q = q_ref[rows, kc]
                k = k_ref[rows, kc]
                v = v_ref[rows, vc].astype(BF16)
                qe = (q * jnp.exp(b)).astype(BF16)
                ke = (k * jnp.exp(-b)).astype(BF16)
                kd = (k * jnp.exp(b_last - b)).astype(BF16)
                scores = lax.dot_general(qe, ke, (((1,), (1,)), ((), ())), preferred_element_type=F32)
                scores = jnp.where(keep, scores, 0.0).astype(BF16)
                o = _dot(scores, v) + _dot(qe, s.astype(BF16))
                if direction == 0:
                    acc[rows, vc] = o
                else:
                    acc[rows, vc] = acc[rows, vc] + o
                tot = (lax.dot_general(gk_hi, ones, tn_dims, preferred_element_type=F32)
                       + lax.dot_general(gk_lo, ones, tn_dims, preferred_element_type=F32))
                dec = jnp.exp(tot)
                dec = jnp.concatenate([dec, dec], axis=1)
                s = dec * s + lax.dot_general(kd, v, tn_dims, preferred_element_type=F32)
            sf_ref[direction, hh] = s

        o = acc[:, vc]
        o = o * lax.rsqrt(jnp.mean(o * o, axis=-1, keepdims=True) + EPS) * ng_ref[...]
        o_ref[:, vc] = (o * g_ref[:, vc]).astype(BF16)


def _gla_core(proj, tri, norm_g, s0, L, n_batch, row_off, hps):
    rb = row_off // L
    H = GLA_H
    nh = H // hps
    has_s0 = s0 is not None
    kern = functools.partial(_gla_core_kernel, L=L, has_s0=has_s0, hps=hps)
    kb, vb = GLA_DK * hps, GLA_DV * hps
    in_specs = [
        pl.BlockSpec((L, kb), lambda b, h: (rb + b, h)),
        pl.BlockSpec((L, kb), lambda b, h: (rb + b, nh + h)),
        pl.BlockSpec((L, vb), lambda b, h: (rb + b, (2 * H * GLA_DK) // vb + h)),
        pl.BlockSpec((L, vb), lambda b, h: (rb + b, (2 * H * GLA_DK) // vb + nh + h)),
        pl.BlockSpec((L, kb), lambda b, h: (rb + b, GLA_PROJ // kb + h)),
        pl.BlockSpec((L, kb), lambda b, h: (rb + b, GLA_PROJ // kb + nh + h)),
        _full_spec((2, GLA_CHUNK, GLA_CHUNK)),
        _vec_spec(GLA_DV),
    ]
    args = [proj] * 6 + [tri, norm_g.reshape(1, GLA_DV)]
    state_spec = pl.BlockSpec((None, 2, hps, GLA_DK, GLA_DV), lambda b, h: (b, 0, h, 0, 0))
    if has_s0:
        in_specs.append(state_spec)
        args.append(s0)
    return pl.pallas_call(
        kern,
        grid=(n_batch, nh),
        in_specs=in_specs,
        out_specs=[pl.BlockSpec((L, vb), lambda b, h: (b, h)), state_spec],
        out_shape=[
            jax.ShapeDtypeStruct((n_batch * L, H * GLA_DV), BF16),
            jax.ShapeDtypeStruct((n_batch, 2, H, GLA_DK, GLA_DV), F32),
        ],
        scratch_shapes=[pltpu.VMEM((L, vb), F32)],
        compiler_params=_cparams(2),
        name=f"gla_core_L{L}",
    )(*args)


def _fnet_mats(L):
    c = np.arange(FNET_C)
    ang_c = 2.0 * np.pi * np.outer(c, c) / FNET_C
    chan = np.concatenate([np.cos(ang_c), np.sin(ang_c)], axis=1) / math.sqrt(FNET_C)
    t = np.arange(L)
    ang_l = 2.0 * np.pi * np.outer(t, t) / L
    seq = np.concatenate([np.cos(ang_l), -np.sin(ang_l)], axis=1) / math.sqrt(L)
    return jnp.asarray(chan, F32).astype(BF16), jnp.asarray(seq, F32).astype(BF16)


def _fnet_kernel(x_ref, g_ref, sc_ref, sh_ref, gate_ref, chan_ref, seq_ref, w_ref, b_ref, o_ref):
    x = x_ref[...]
    h = _norm_mod(x, g_ref[...], sc_ref[...], sh_ref[...]).astype(BF16)
    chan = chan_ref[...]
    cos_parts, sin_parts = [], []
    for gi in range(FNET_GROUPS):
        cs = _dot(h[:, gi * FNET_C:(gi + 1) * FNET_C], chan)
        cos_parts.append(cs[:, :FNET_C])
        sin_parts.append(cs[:, FNET_C:])
    stacked = jnp.concatenate([jnp.concatenate(cos_parts, axis=1), jnp.concatenate(sin_parts, axis=1)], axis=0)
    mixed = _dot(seq_ref[...], stacked.astype(BF16))
    y = _dot(mixed.astype(BF16), w_ref[...]) + b_ref[...]
    o_ref[...] = x + gate_ref[...] * y


def _fnet(x, mod, layer, norm_g, chan, seq, w_bf16, bias, L, n_batch, row_off):
    rb = row_off // L
    lat = row_off > 0

    def row_fn(b):
        return 1 + b if lat else 0

    return pl.pallas_call(
        _fnet_kernel,
        grid=(n_batch,),
        in_specs=[
            pl.BlockSpec((L, D), lambda b: (rb + b, 0)),
            _vec_spec(D),
            _mod_spec(layer, 1, row_fn),
            _mod_spec(layer, 0, row_fn),
            _mod_spec(layer, 2, row_fn),
            _full_spec((FNET_C, 2 * FNET_C)),
            _full_spec((L, 2 * L)),
            _full_spec((D, D)),
            _vec_spec(D),
        ],
        out_specs=pl.BlockSpec((L, D), lambda b: (rb + b, 0)),
        out_shape=jax.ShapeDtypeStruct((N_TOK, D), F32),
        input_output_aliases={0: 0},
        compiler_params=_cparams(1),
        name=f"fnet_L{L}",
    )(x, norm_g.reshape(1, D), mod, mod, mod, chan, seq, w_bf16, bias.reshape(1, D))


def _window_bounds(n, k):
    t = np.arange(n)
    lo, hi = k // 2, k - k // 2 - 1
    return np.maximum(t - lo, 0), np.minimum(t + hi + 1, n)


def _pool_mats(L, grid_rows):
    mats, inv = [], []
    for k in POOL_WINDOWS:
        if grid_rows is None:
            s, e = _window_bounds(L, k)
            idx = np.arange(L)[None, :]
            m = ((idx >= s[:, None]) & (idx < e[:, None])).astype(np.float64)
            cnt = (e - s).astype(np.float64)
        else:
            sr, er = _window_bounds(grid_rows, k)
            sc, ec = _window_bounds(GRID_W, k)
            ir = np.arange(grid_rows)[None, :]
            ic = np.arange(GRID_W)[None, :]
            mr = ((ir >= sr[:, None]) & (ir < er[:, None])).astype(np.float64)
            mc = ((ic >= sc[:, None]) & (ic < ec[:, None])).astype(np.float64)
            m = np.kron(mr, mc)
            cnt = np.kron((er - sr).astype(np.float64), (ec - sc).astype(np.float64))
        mats.append(m)
        inv.append(1.0 / cnt)
    return jnp.asarray(np.stack(mats), BF16), jnp.asarray(np.stack(inv)[:, :, None], F32)


def _pool_kernel(x_ref, g_ref, sc_ref, sh_ref, gate_ref, m_ref, ic_ref, w_ref, b_ref, ps_ref, o_ref):
    x = x_ref[...]
    h = _norm_mod(x, g_ref[...], sc_ref[...], sh_ref[...])
    outs = []
    for gi in range(len(POOL_WINDOWS)):
        hg = h[:, gi * POOL_G:(gi + 1) * POOL_G]
        hi, lo = _split(hg)
        m = m_ref[gi]
        mean = (_dot(m, hi) + _dot(m, lo)) * ic_ref[gi]
        outs.append(_dot((mean - hg).astype(BF16), w_ref[gi]))
    y = (jnp.concatenate(outs, axis=1) + b_ref[...]) * ps_ref[...]
    o_ref[...] = x + gate_ref[...] * y


def _pool(x, mod, layer, norm_g, mats, inv_cnt, w_bf16, bias, scale, L, n_batch, row_off):
    rb = row_off // L
    lat = row_off > 0
    G = len(POOL_WINDOWS)

    def row_fn(b):
        return 1 + b if lat else 0

    return pl.pallas_call(
        _pool_kernel,
        grid=(n_batch,),
        in_specs=[
            pl.BlockSpec((L, D), lambda b: (rb + b, 0)),
            _vec_spec(D),
            _mod_spec(layer, 1, row_fn),
            _mod_spec(layer, 0, row_fn),
            _mod_spec(layer, 2, row_fn),
            _full_spec((G, L, L)),
            _full_spec((G, L, 1)),
            _full_spec((G, POOL_G, POOL_G)),
            _vec_spec(D),
            _vec_spec(D),
        ],
        out_specs=pl.BlockSpec((L, D), lambda b: (rb + b, 0)),
        out_shape=jax.ShapeDtypeStruct((N_TOK, D), F32),
        input_output_aliases={0: 0},
        compiler_params=_cparams(1),
        name=f"pool_L{L}",
    )(x, norm_g.reshape(1, D), mod, mod, mod, mats, inv_cnt, w_bf16, bias.reshape(1, D), scale.reshape(1, D))


ROUTER_PAD = 128
LANES = 128
D_EXT = D + LANES
MOE_TILE = 1024
MOE_MAX_TILES = N_TOK // MOE_TILE + MOE_GROUPS
MOE_ROWS = MOE_MAX_TILES * MOE_TILE
MOE_Y_ROWS = N_TOK + 2 * MOE_TILE
MOE_DMA_PER_STEP = MOE_TILE // MOE_PER_GROUP


def _moe_route_kernel(x_ref, g_ref, sc_ref, sh_ref, wr_ref, tri_ref, h3_ref, route_ref, cnt_ref, carry):
    t = pl.program_id(0)

    @pl.when(t == 0)
    def _():
        carry[...] = jnp.zeros_like(carry)

    h = _norm_mod(x_ref[...], g_ref[...], sc_ref[...], sh_ref[...])
    logits = _dot_precise(h, wr_ref[...])
    lane = lax.broadcasted_iota(jnp.int32, logits.shape, 1)
    neg = jnp.float32(-jnp.inf)
    big = jnp.int32(ROUTER_PAD)
    is_grp = (lane >= MOE_E) & (lane < MOE_E + MOE_GROUPS)
    gl = jnp.where(is_grp, logits, neg)
    g_max = jnp.max(gl, axis=-1, keepdims=True)
    g_idx = jnp.min(jnp.where(gl == g_max, lane, big), axis=-1, keepdims=True) - MOE_E
    p_grp = 1.0 / jnp.sum(jnp.where(is_grp, jnp.exp(gl - g_max), 0.0), axis=-1, keepdims=True)
    in_grp = (lane < MOE_E) & ((lane >> 2) == g_idx)
    el = jnp.where(in_grp, logits, neg)
    m1 = jnp.max(el, axis=-1, keepdims=True)
    i1 = jnp.min(jnp.where(el == m1, lane, big), axis=-1, keepdims=True)
    z = jnp.sum(jnp.where(in_grp, jnp.exp(el - m1), 0.0), axis=-1, keepdims=True)
    el2 = jnp.where(lane == i1, neg, el)
    m2 = jnp.max(el2, axis=-1, keepdims=True)
    i2 = jnp.min(jnp.where(el2 == m2, lane, big), axis=-1, keepdims=True)
    p1 = 1.0 / z
    p2 = jnp.exp(m2 - m1) / z
    tot = p1 + p2
    eid = lane + MOE_PER_GROUP * g_idx
    in4 = lane < MOE_PER_GROUP
    cw4 = (jnp.where(in4 & (eid == i1), p_grp * (p1 / tot), 0.0)
           + jnp.where(in4 & (eid == i2), p_grp * (p2 / tot), 0.0))
    member = jnp.where(lane == g_idx, 1.0, 0.0).astype(BF16)
    before = _dot(tri_ref[...], member) + carry[...]
    rank = jnp.sum(jnp.where(lane == g_idx, before, 0.0), axis=-1, keepdims=True)
    carry[...] = carry[...] + jnp.sum(member.astype(F32), axis=0, keepdims=True)
    cnt_ref[...] = carry[...].astype(jnp.int32)
    route_ref[...] = jnp.where(lane == 0, g_idx, jnp.where(lane == 1, rank.astype(jnp.int32), 0))
    h3_ref[:, :D] = h
    h3_ref[:, D:] = cw4


def _moe_route(x, mod, layer, norm_g, w_router, tri):
    return pl.pallas_call(
        _moe_route_kernel,
        grid=(N_TOK // TM,),
        in_specs=[
            pl.BlockSpec((TM, D), lambda t: (t, 0)),
            _vec_spec(D),
            _mod_spec(layer, 4, _row_tm),
            _mod_spec(layer, 3, _row_tm),
            _full_spec((D, ROUTER_PAD)),
            _full_spec((TM, TM)),
        ],
        out_specs=[
            pl.BlockSpec((TM, D_EXT), lambda t: (t, 0)),
            pl.BlockSpec((TM, LANES), lambda t: (t, 0)),
            pl.BlockSpec((1, LANES), lambda t: (0, 0)),
        ],
        out_shape=[
            jax.ShapeDtypeStruct((N_TOK, D_EXT), F32),
            jax.ShapeDtypeStruct((N_TOK, LANES), jnp.int32),
            jax.ShapeDtypeStruct((1, LANES), jnp.int32),
        ],
        scratch_shapes=[pltpu.VMEM((1, LANES), F32)],
        compiler_params=_cparams(1),
        name="moe_route",
    )(x, norm_g.reshape(1, D), mod, mod, w_router, tri)


def _moe_invert_kernel(pos_ref, src_ref):
    def mark(i, carry):
        src_ref[i] = jnp.int32(N_TOK)
        return carry

    lax.fori_loop(0, MOE_ROWS, mark, 0, unroll=16)

    def place(n, carry):
        src_ref[pos_ref[n]] = n
        return carry

    lax.fori_loop(0, N_TOK, place, 0, unroll=16)


def _moe_invert(pos):
    smem = pl.BlockSpec(memory_space=pltpu.SMEM)
    return pl.pallas_call(
        _moe_invert_kernel,
        in_specs=[smem],
        out_specs=smem,
        out_shape=jax.ShapeDtypeStruct((MOE_ROWS,), jnp.int32),
        name="moe_invert",
    )(pos)


def _moe_expert_kernel(src_ref, grp_ref, nact_ref, h_hbm, w1_ref, w3_ref, w2_ref, y_hbm,
                       xb0, xb1, ab0, ab1, gsem, ssem):
    t = pl.program_id(0)
    k = pl.program_id(1)
    n_active = nact_ref[0]
    last = n_active - 1
    T = MOE_TILE
    xbufs, accs = (xb0, xb1), (ab0, ab1)

    def gather_row(tile, slot, r):
        tok = jnp.minimum(src_ref[tile * T + r], N_TOK - 1)
        return pltpu.make_async_copy(h_hbm.at[pl.ds(tok, 1), :], xbufs[slot].at[pl.ds(r, 1), :], gsem.at[slot])

    def scatter_row(tile, slot, r, real):
        tok = src_ref[tile * T + r]
        dst = jnp.where(real & (tok < N_TOK), tok, N_TOK + slot * T + r)
        return pltpu.make_async_copy(accs[slot].at[pl.ds(r, 1), :], y_hbm.at[pl.ds(dst, 1), :], ssem.at[slot])

    def wait_gather(slot):
        pltpu.make_async_copy(h_hbm.at[pl.ds(0, T), :], xbufs[slot], gsem.at[slot]).wait()

    def wait_scatter(slot):
        pltpu.make_async_copy(accs[slot], y_hbm.at[pl.ds(0, T), :], ssem.at[slot]).wait()

    def step(slot):
        other = 1 - slot
        xb, acc = xbufs[slot], accs[slot]

        @pl.when(k == 0)
        def _():
            if slot == 0:
                @pl.when(t == 0)
                def _():
                    def first(r, carry):
                        gather_row(0, 0, r).start()
                        return carry

                    lax.fori_loop(0, T, first, 0, unroll=8)
                    ab0[...] = jnp.zeros_like(ab0)
                    ab1[...] = jnp.zeros_like(ab1)
                    dumps = [pltpu.make_async_copy(ab1, y_hbm.at[pl.ds(N_TOK + s * T, T), :], ssem.at[1])
                             for s in (0, 1)]
                    for cp in dumps:
                        cp.start()
                    for cp in dumps:
                        cp.wait()

            wait_gather(slot)

            @pl.when(t >= 1)
            def _():
                wait_scatter(slot)

        nxt = jnp.minimum(t + 1, last)
        prev = jnp.maximum(t - 1, 0)
        for r in range(MOE_DMA_PER_STEP):
            row = k * MOE_DMA_PER_STEP + r
            gather_row(nxt, other, row).start(priority=r % 2)
            scatter_row(prev, other, row, t >= 1).start(priority=(r + 1) % 2)

        x = xb[:, :D].astype(BF16)
        a = _dot(x, w1_ref[...].astype(BF16))
        b = _dot(x, w3_ref[...].astype(BF16))
        lane = lax.broadcasted_iota(jnp.int32, (T, LANES), 1)
        cwk = jnp.sum(jnp.where(lane == k, xb[:, D:], 0.0), axis=-1, keepdims=True)
        hid = (_silu(a) * b * cwk).astype(BF16)
        acc[...] = jnp.where(k > 0, acc[...], 0.0) + _dot(hid, w2_ref[...].astype(BF16))

        @pl.when((k == MOE_PER_GROUP - 1) & (t == last))
        def _():
            wait_gather(other)
            wait_scatter(other)

            def final(r, carry):
                scatter_row(t, slot, r, True).start()
                return carry

            lax.fori_loop(0, T, final, 0, unroll=8)
            wait_scatter(slot)

    for slot in (0, 1):
        pl.when((t < n_active) & (t % 2 == slot))(functools.partial(step, slot))


def _moe_experts(h_ext, src, tile_group, n_active, layer, w1, w3, w2):
    T = MOE_TILE

    def w_index(t, k, src_ref, grp_ref, nact_ref):
        last = nact_ref[0] - 1
        e = jnp.where(t <= last, grp_ref[t] * MOE_PER_GROUP + k, grp_ref[last] * MOE_PER_GROUP + MOE_PER_GROUP - 1)
        return (layer, e, 0, 0)

    grid_spec = pltpu.PrefetchScalarGridSpec(
        num_scalar_prefetch=3,
        grid=(MOE_MAX_TILES, MOE_PER_GROUP),
        in_specs=[
            pl.BlockSpec(memory_space=pl.ANY),
            pl.BlockSpec((None, None, D, MOE_HID), w_index),
            pl.BlockSpec((None, None, D, MOE_HID), w_index),
            pl.BlockSpec((None, None, MOE_HID, D), w_index),
        ],
        out_specs=pl.BlockSpec(memory_space=pl.ANY),
        scratch_shapes=[
            pltpu.VMEM((T, D_EXT), F32),
            pltpu.VMEM((T, D_EXT), F32),
            pltpu.VMEM((T, D), F32),
            pltpu.VMEM((T, D), F32),
            pltpu.SemaphoreType.DMA((2,)),
            pltpu.SemaphoreType.DMA((2,)),
        ],
    )
    return pl.pallas_call(
        _moe_expert_kernel,
        grid_spec=grid_spec,
        out_shape=jax.ShapeDtypeStruct((MOE_Y_ROWS, D), F32),
        compiler_params=_cparams(2),
        name="moe_experts",
    )(src, tile_group, n_active, h_ext, w1, w3, w2)


def _moe_combine_kernel(y_ref, gate_ref, x_ref, o_ref):
    o_ref[...] = x_ref[...] + gate_ref[...] * y_ref[...]


def _moe_combine(x, y3, mod, layer):
    return pl.pallas_call(
        _moe_combine_kernel,
        grid=(N_TOK // TM,),
        in_specs=[
            pl.BlockSpec((TM, D), lambda t: (t, 0)),
            _mod_spec(layer, 5, _row_tm),
            pl.BlockSpec((TM, D), lambda t: (t, 0)),
        ],
        out_specs=pl.BlockSpec((TM, D), lambda t: (t, 0)),
        out_shape=jax.ShapeDtypeStruct((N_TOK, D), F32),
        input_output_aliases={2: 0},
        compiler_params=_cparams(1),
        name="moe_combine",
    )(y3, mod, x)


def _moe(x, mod, layer, norm_g, w_rg, w_re, w1, w3, w2, tri):
    w_router = jnp.zeros((D, ROUTER_PAD), F32).at[:, :MOE_E].set(w_re).at[:, MOE_E:MOE_E + MOE_GROUPS].set(w_rg)
    h3, route, counts = _moe_route(x, mod, layer, norm_g, w_router, tri)
    cnt = counts[0, :MOE_GROUPS]
    ntile = (cnt + MOE_TILE - 1) // MOE_TILE
    tile_end = jnp.cumsum(ntile)
    seg_start = (tile_end - ntile) * MOE_TILE
    g_idx, rank = route[:, 0], route[:, 1]
    pos = jnp.sum(jnp.where(g_idx[:, None] == jnp.arange(MOE_GROUPS)[None, :], seg_start[None, :], 0), axis=1) + rank
    tiles = jnp.arange(MOE_MAX_TILES, dtype=jnp.int32)
    tile_group = jnp.minimum(jnp.sum(tiles[:, None] >= tile_end[None, :], axis=1), MOE_GROUPS - 1).astype(jnp.int32)
    n_active = tile_end[-1:].astype(jnp.int32)
    src = _moe_invert(pos.astype(jnp.int32))
    y3 = _moe_experts(h3, src, tile_group, n_active, layer, w1, w3, w2)
    return _moe_combine(x, y3, mod, layer)


def _final_norm_kernel(x_ref, g_ref, o_ref):
    x = x_ref[...]
    o_ref[...] = x * lax.rsqrt(jnp.mean(x * x, axis=-1, keepdims=True) + EPS) * g_ref[...]


def _final_norm(x, final_g, row_off, n_rows):
    off = row_off // TM
    return pl.pallas_call(
        _final_norm_kernel,
        grid=(n_rows // TM,),
        in_specs=[pl.BlockSpec((TM, D), lambda t: (t + off, 0)), _vec_spec(D)],
        out_specs=pl.BlockSpec((TM, D), lambda t: (t, 0)),
        out_shape=jax.ShapeDtypeStruct((n_rows, D), F32),
        compiler_params=_cparams(1),
        name="final_norm",
    )(x, final_g.reshape(1, D))


def kernel(x_prompt, x_sample, state_gla, c, c_ctx, w_ada, b_ada, norm_g, hy_w_in, hy_b_in, hy_conv_w, hy_conv_b, hy_f_w1, hy_f_b1, hy_f_freq, hy_f_w2, hy_f_b2, hy_f_w3, hy_skip, hy_w_out, hy_b_out, gla_w_q, gla_w_k, gla_w_v, gla_w_g, gla_w_gk1, gla_w_gk2, gla_b_gk, gla_norm_g, gla_w_o, fn_w_out, fn_b_out, pool_w, pool_b, pool_scale, moe_w_rg, moe_w_re, moe_w1, moe_w3, moe_w2, final_g):
    groups = ((CTX_L, CTX_B, 0, None), (LAT_L, LAT_B, N_CTX, LAT_L // GRID_W))

    x = jnp.concatenate([x_prompt.reshape(N_CTX, D), x_sample.reshape(N_LAT, D)], axis=0)
    cond = jnp.zeros((MOD_ROWS, D), F32).at[0].set(c_ctx).at[1:1 + LAT_B].set(c)
    mod = _ada_table(cond, w_ada, b_ada).reshape(DEPTH * MOD_ROWS * 6, 1, D)

    tri_tm = jnp.asarray(np.tril(np.ones((TM, TM)), -1), BF16)

    new_states = []
    for i in range(DEPTH):
        kind, j = i % 4, i // 4
        if kind == 0:
            u = _hyena_in(x, mod, i, norm_g[i, 0], hy_w_in[j].astype(BF16), hy_b_in[j], hy_conv_w[j], hy_conv_b[j])
            w_out = hy_w_out[j].astype(BF16)
            for L, nb, off, _ in groups:
                fwd, ff, inv = _dft_mats(L)
                khat = _hyena_filters(L, ff, hy_f_w1[j], hy_f_b1[j], hy_f_freq[j], hy_f_w2[j], hy_f_b2[j],
                                      hy_f_w3[j])
                z = _hyena_conv(u, khat, hy_skip[j], fwd, inv, L, nb, off, D if L == CTX_L else 512)
                x = _outproj(x, z, w_out, hy_b_out[j], mod, i, off, nb * L)
        elif kind == 1:
            w_cat = jnp.concatenate([gla_w_q[j], gla_w_k[j], gla_w_v[j], gla_w_g[j]], axis=1).astype(BF16)
            nk = GLA_H * GLA_DK
            wg1 = jnp.zeros((D, GK1_PAD), F32).at[:, :GLA_RANK].set(gla_w_gk1[j, 0])
            wg1 = wg1.at[:, GLA_RANK:2 * GLA_RANK].set(gla_w_gk1[j, 1]).astype(BF16)
            wg2 = jnp.zeros((GK1_PAD, 2 * nk), F32).at[:GLA_RANK, :nk].set(gla_w_gk2[j, 0])
            wg2 = wg2.at[GLA_RANK:2 * GLA_RANK, nk:].set(gla_w_gk2[j, 1]).astype(BF16)
            proj = _gla_proj(x, mod, i, norm_g[i, 0], w_cat, wg1, wg2, gla_b_gk[j].reshape(1, 2 * nk))
            lower = np.tril(np.ones((GLA_CHUNK, GLA_CHUNK)))
            tri = jnp.asarray(np.stack([lower, lower.T]), BF16)
            w_o = gla_w_o[j].astype(BF16)
            for L, nb, off, grid_rows in groups:
                s0 = None if grid_rows is None else state_gla[:, j]
                o, s_fin = _gla_core(proj, tri, gla_norm_g[j], s0, L, nb, off, GLA_H if L == CTX_L else 1)
                if grid_rows is None:
                    new_states.append(s_fin)
                x = _outproj(x, o, w_o, jnp.zeros((D,), F32), mod, i, off, nb * L)
        elif kind == 2:
            w_out = fn_w_out[j].astype(BF16)
            for L, nb, off, _ in groups:
                chan, seq = _fnet_mats(L)
                x = _fnet(x, mod, i, norm_g[i, 0], chan, seq, w_out, fn_b_out[j], L, nb, off)
        else:
            w_pool = pool_w[j].astype(BF16)
            for L, nb, off, grid_rows in groups:
                mats, inv_cnt = _pool_mats(L, grid_rows)
                x = _pool(x, mod, i, norm_g[i, 0], mats, inv_cnt, w_pool, pool_b[j], pool_scale[j], L, nb, off)

        x = _moe(x, mod, i, norm_g[i, 1], moe_w_rg[i], moe_w_re[i], moe_w1, moe_w3, moe_w2, tri_tm)

    y_prompt = _final_norm(x, final_g, 0, N_CTX).reshape(CTX_B, CTX_L, D)
    y_sample = _final_norm(x, final_g, N_CTX, N_LAT).reshape(LAT_B, LAT_L, D)
    new_state_gla = jnp.stack(new_states, axis=1)
    return (y_prompt, y_sample, new_state_gla)
```

```python
import functools
import math

import jax
import jax.numpy as jnp
import numpy as np
from jax import lax
from jax.experimental import pallas as pl
from jax.experimental.pallas import tpu as pltpu

F32 = jnp.float32
BF16 = jnp.bfloat16

D = 1024
CTX_B, CTX_L = 32, 256
LAT_B, LAT_L = 2, 1024
N_CTX = CTX_B * CTX_L
N_LAT = LAT_B * LAT_L
N_TOK = N_CTX + N_LAT
DEPTH = 4
GRID_W = 64
EPS = 1e-6

HY_BANDS = 8
HY_EMB = 1 + 2 * HY_BANDS
HY_EMB_PAD = 32
HY_HID = 64
HY_FAST_DECAY = 0.3
HY_SLOW_DECAY = 1.5
HY_DECAY_TARGET = 1e-2

GLA_H = 4
GLA_DK = 128
GLA_DV = 256
GLA_RANK = 16
GLA_NORMALIZER = 16.0
GLA_CHUNK = 64

FNET_GROUPS = 4
FNET_C = D // FNET_GROUPS
POOL_WINDOWS = (2, 4, 8, 16)
POOL_G = D // len(POOL_WINDOWS)

MOE_GROUPS = 4
MOE_PER_GROUP = 4
MOE_E = MOE_GROUPS * MOE_PER_GROUP
MOE_HID = D // 2

MOD_ROWS = 8
TM = 512
TM_BIG = 1024
VMEM_LIMIT = 56 * 1024 * 1024


def _cparams(n_axes):
    return pltpu.CompilerParams(dimension_semantics=("arbitrary",) * n_axes, vmem_limit_bytes=VMEM_LIMIT)


def _norm_mod(x, g, sc, sh):
    ms = jnp.mean(x * x, axis=-1, keepdims=True)
    return (x * lax.rsqrt(ms + EPS) * g) * (1.0 + sc) + sh


def _split(a):
    hi = a.astype(BF16)
    lo = (a - hi.astype(F32)).astype(BF16)
    return hi, lo


def _dot(a, b):
    return jnp.dot(a, b, preferred_element_type=F32)


def _dot_precise(a, b):
    a_hi, a_lo = _split(a)
    b_hi, b_lo = _split(b)
    return _dot(a_hi, b_hi) + (_dot(a_hi, b_lo) + _dot(a_lo, b_hi))


def _silu(x):
    return x * (1.0 / (1.0 + jnp.exp(-x)))


def _log_sigmoid(x):
    return jnp.minimum(x, 0.0) - jnp.log(1.0 + jnp.exp(-jnp.abs(x)))


def _mod_spec(layer, chunk, row_fn):
    base = layer * MOD_ROWS * 6 + chunk

    def index_map(*ids):
        return (base + row_fn(*ids) * 6, 0, 0)

    return pl.BlockSpec((None, 1, D), index_map)


def _row_tm(t, *_):
    return jnp.where(t < N_CTX // TM, 0, 1 + (t - N_CTX // TM) // (LAT_L // TM))


def _row_big(t, *_):
    return jnp.where(t < N_CTX // TM_BIG, 0, 1 + (t - N_CTX // TM_BIG) // (LAT_L // TM_BIG))


def _vec_spec(n):
    return pl.BlockSpec((1, n), lambda *ids: (0, 0))


def _full_spec(shape):
    nd = len(shape)
    return pl.BlockSpec(shape, lambda *ids: (0,) * nd)


def _ada_kernel(cond_ref, w_ref, b_ref, o_ref):
    s = _silu(cond_ref[...]).astype(BF16)
    o_ref[...] = _dot(s, w_ref[...].astype(BF16)) + b_ref[...]


def _ada_table(cond, w_ada, b_ada):
    tn = 1536
    return pl.pallas_call(
        _ada_kernel,
        grid=(DEPTH, 6 * D // tn),
        in_specs=[
            pl.BlockSpec((MOD_ROWS, D), lambda i, j: (0, 0)),
            pl.BlockSpec((None, D, tn), lambda i, j: (i, 0, j)),
            pl.BlockSpec((None, 1, tn), lambda i, j: (i, 0, j)),
        ],
        out_specs=pl.BlockSpec((None, MOD_ROWS, tn), lambda i, j: (i, 0, j)),
        out_shape=jax.ShapeDtypeStruct((DEPTH, MOD_ROWS, 6 * D), F32),
        compiler_params=_cparams(2),
        name="ada_table",
    )(cond, w_ada, b_ada.reshape(DEPTH, 1, 6 * D))


def _outproj_kernel(z_ref, w_ref, b_ref, gate_ref, x_ref, o_ref):
    y = _dot(z_ref[...], w_ref[...]) + b_ref[...]
    o_ref[...] = x_ref[...] + gate_ref[...] * y


def _outproj(x, z, w_bf16, bias, mod, layer, row_off, n_rows):
    k = z.shape[1]
    off = row_off // TM
    first_lat = N_CTX // TM

    def row_fn(t):
        g = t + off
        return jnp.where(g < first_lat, 0, 1 + (g - first_lat) // (LAT_L // TM))

    return pl.pallas_call(
        _outproj_kernel,
        grid=(n_rows // TM,),
        in_specs=[
            pl.BlockSpec((TM, k), lambda t: (t, 0)),
            _full_spec((k, D)),
            _vec_spec(D),
            _mod_spec(layer, 2, row_fn),
            pl.BlockSpec((TM, D), lambda t: (t + off, 0)),
        ],
        out_specs=pl.BlockSpec((TM, D), lambda t: (t + off, 0)),
        out_shape=jax.ShapeDtypeStruct((N_TOK, D), F32),
        input_output_aliases={4: 0},
        compiler_params=_cparams(1),
        name="outproj_residual",
    )(z, w_bf16, bias.reshape(1, D), mod, x)


def _dft_mats(L):
    n2 = 2 * L
    k = np.arange(L)[:, None].astype(np.float64)
    n = np.arange(n2)[None, :].astype(np.float64)
    ang = 2.0 * np.pi * k * n / n2
    full = np.concatenate([np.cos(ang), -np.sin(ang)], axis=0)
    full[L, :] = np.cos(np.pi * np.arange(n2))
    fwd = full[:, :L]
    bwd = np.zeros((n2, L))
    bwd[:, 1:] = full[:, n2 - np.arange(1, L)]
    t = np.arange(L)[:, None].astype(np.float64)
    kk = np.arange(L)[None, :].astype(np.float64)
    ang_i = 2.0 * np.pi * t * kk / n2
    inv_re = np.cos(ang_i) / L
    inv_re[:, 0] = 1.0 / n2
    inv_im = -np.sin(ang_i) / L
    inv_im[:, 0] = np.cos(np.pi * np.arange(L)) / n2
    inv = np.concatenate([inv_re, inv_im], axis=1)
    return tuple(jnp.asarray(m, F32).astype(BF16) for m in (fwd, np.concatenate([fwd, bwd], axis=1), inv))


def _hyena_pos_emb(L):
    pos = np.arange(L, dtype=np.float64)
    bands = np.linspace(1e-4, HY_BANDS - 1, HY_BANDS)
    ang = (2.0 * np.pi * pos / L)[:, None] * bands[None, :]
    z = np.concatenate([(pos / L)[:, None], np.cos(ang), -np.sin(ang)], axis=-1)
    zp = np.zeros((L, HY_EMB_PAD))
    zp[:, :HY_EMB] = z
    return jnp.asarray(zp, F32)


def _hyena_filter_kernel(z_ref, w1_ref, b1_ref, fr_ref, w2_ref, b2_ref, w3f_ref, w3b_ref, ff_ref, o_ref, *, L, tn):
    j = pl.program_id(1)
    fr = fr_ref[...]
    f = jnp.sin(fr * (_dot_precise(z_ref[...], w1_ref[...]) + b1_ref[...]))
    f = jnp.sin(fr * (_dot_precise(f, w2_ref[...]) + b2_ref[...]))
    t_lin = lax.broadcasted_iota(jnp.int32, (L, tn), 0).astype(F32) / float(L - 1)
    ch = (lax.broadcasted_iota(jnp.int32, (L, tn), 1) + j * tn).astype(F32)
    max_decay = math.log(HY_DECAY_TARGET) / HY_FAST_DECAY
    min_decay = math.log(HY_DECAY_TARGET) / HY_SLOW_DECAY
    deltas = min_decay + ch * ((max_decay - min_decay) / float(D - 1))
    window = jnp.exp(-t_lin * jnp.abs(deltas))
    kf = _dot_precise(f, w3f_ref[...]) * window
    kb = _dot_precise(f, w3b_ref[...]) * window
    taps = jnp.concatenate([kf, kb], axis=0).astype(BF16)
    o_ref[...] = _dot(ff_ref[...], taps)


def _hyena_filters(L, ff, f_w1, f_b1, f_freq, f_w2, f_b2, f_w3):
    tn = 512
    nj = D // tn
    w1p = jnp.zeros((HY_EMB_PAD, HY_HID), F32).at[:HY_EMB].set(f_w1)
    kern = functools.partial(_hyena_filter_kernel, L=L, tn=tn)
    return pl.pallas_call(
        kern,
        grid=(2, nj),
        in_specs=[
            _full_spec((L, HY_EMB_PAD)),
            _full_spec((HY_EMB_PAD, HY_HID)),
            _vec_spec(HY_HID),
            _vec_spec(HY_HID),
            _full_spec((HY_HID, HY_HID)),
            _vec_spec(HY_HID),
            pl.BlockSpec((HY_HID, tn), lambda o, j: (0, o * nj + j)),
            pl.BlockSpec((HY_HID, tn), lambda o, j: (0, (2 + o) * nj + j)),
            _full_spec((2 * L, 2 * L)),
        ],
        out_specs=pl.BlockSpec((None, 2 * L, tn), lambda o, j: (o, 0, j)),
        out_shape=jax.ShapeDtypeStruct((2, 2 * L, D), F32),
        compiler_params=_cparams(2),
        name=f"hyena_filters_L{L}",
    )(_hyena_pos_emb(L), w1p, f_b1.reshape(1, -1), f_freq.reshape(1, -1), f_w2, f_b2.reshape(1, -1),
      f_w3, f_w3, ff)


def _hyena_in_kernel(x_ref, g_ref, sc_ref, sh_ref, w_ref, b_ref, cw_ref, cb_ref, o_ref, h_scr):
    t = pl.program_id(0)

    @pl.when(pl.program_id(1) == 0)
    def _():
        h_scr[...] = _norm_mod(x_ref[...], g_ref[...], sc_ref[...], sh_ref[...]).astype(BF16)

    u = _dot(h_scr[...], w_ref[...]) + b_ref[...]
    seq = jnp.where(t < N_CTX // TM_BIG, CTX_L, LAT_L)
    pos = lax.broadcasted_iota(jnp.int32, u.shape, 0) & (seq - 1)
    prev = jnp.where(pos == 0, 0.0, pltpu.roll(u, 1, 0))
    nxt = jnp.where(pos == seq - 1, 0.0, pltpu.roll(u, TM_BIG - 1, 0))
    cw = cw_ref[...]
    o_ref[...] = prev * cw[0:1] + u * cw[1:2] + nxt * cw[2:3] + cb_ref[...]


def _hyena_in(x, mod, layer, norm_g, w_in_bf16, b_in, conv_w, conv_b):
    return pl.pallas_call(
        _hyena_in_kernel,
        grid=(N_TOK // TM_BIG, 3),
        in_specs=[
            pl.BlockSpec((TM_BIG, D), lambda t, p: (t, 0)),
            _vec_spec(D),
            _mod_spec(layer, 1, _row_big),
            _mod_spec(layer, 0, _row_big),
            pl.BlockSpec((D, D), lambda t, p: (0, p)),
            pl.BlockSpec((1, D), lambda t, p: (0, p)),
            pl.BlockSpec((3, D), lambda t, p: (0, p)),
            pl.BlockSpec((1, D), lambda t, p: (0, p)),
        ],
        out_specs=pl.BlockSpec((TM_BIG, D), lambda t, p: (t, p)),
        out_shape=jax.ShapeDtypeStruct((N_TOK, 3 * D), F32),
        scratch_shapes=[pltpu.VMEM((TM_BIG, D), BF16)],
        compiler_params=_cparams(2),
        name="hyena_in",
    )(x, norm_g.reshape(1, D), mod, mod, w_in_bf16, b_in.reshape(1, -1), conv_w, conv_b.reshape(1, -1))


def _hyena_conv_kernel(v_ref, x1_ref, x2_ref, kh_ref, skip_ref, fwd_ref, inv_ref, o_ref, *, L):
    fwd = fwd_ref[...]
    inv = inv_ref[...]
    row0 = lax.broadcasted_iota(jnp.int32, (L, v_ref.shape[1]), 0) == 0

    def long_conv(z, order):
        zh = _dot(fwd, z.astype(BF16))
        zr, zi = zh[:L], zh[L:]
        kr, ki = kh_ref[order, :L, :], kh_ref[order, L:, :]
        pr = jnp.where(row0, zr * kr, zr * kr - zi * ki)
        pi = jnp.where(row0, zi * ki, zr * ki + zi * kr)
        prod = jnp.concatenate([pr, pi], axis=0).astype(BF16)
        return _dot(inv, prod) + z * skip_ref[order:order + 1, :]

    z = x1_ref[...] * long_conv(v_ref[...], 0)
    z = x2_ref[...] * long_conv(z, 1)
    o_ref[...] = z.astype(BF16)


def _hyena_conv(u, khat, skip, fwd, inv, L, n_batch, row_off, tn):
    nj = D // tn
    rb = row_off // L
    kern = functools.partial(_hyena_conv_kernel, L=L)
    return pl.pallas_call(
        kern,
        grid=(nj, n_batch),
        in_specs=[
            pl.BlockSpec((L, tn), lambda j, b: (rb + b, j)),
            pl.BlockSpec((L, tn), lambda j, b: (rb + b, nj + j)),
            pl.BlockSpec((L, tn), lambda j, b: (rb + b, 2 * nj + j)),
            pl.BlockSpec((2, 2 * L, tn), lambda j, b: (0, 0, j)),
            pl.BlockSpec((2, tn), lambda j, b: (0, j)),
            _full_spec((2 * L, L)),
            _full_spec((L, 2 * L)),
        ],
        out_specs=pl.BlockSpec((L, tn), lambda j, b: (b, j)),
        out_shape=jax.ShapeDtypeStruct((n_batch * L, D), BF16),
        compiler_params=_cparams(2),
        name=f"hyena_conv_L{L}",
    )(u, u, u, khat, skip, fwd, inv)


GLA_PROJ = 2 * GLA_H * GLA_DK + 2 * GLA_H * GLA_DV
GLA_COLS = GLA_PROJ + 2 * GLA_H * GLA_DK
GK1_PAD = 128


def _gla_proj_kernel(x_ref, g_ref, sc_ref, sh_ref, w_ref, wg1_ref, wg2_ref, bg_ref, o_ref):
    h = _norm_mod(x_ref[...], g_ref[...], sc_ref[...], sh_ref[...]).astype(BF16)
    p = _dot(h, w_ref[...])
    nq = GLA_H * GLA_DK
    o_ref[:, 0:nq] = p[:, 0:nq] * (GLA_DK ** -0.5)
    o_ref[:, nq:nq + nq + GLA_H * GLA_DV] = p[:, nq:nq + nq + GLA_H * GLA_DV]
    o_ref[:, 2 * nq + GLA_H * GLA_DV:GLA_PROJ] = _silu(p[:, 2 * nq + GLA_H * GLA_DV:GLA_PROJ])
    low = _dot(h, wg1_ref[...]).astype(BF16)
    gk = _dot(low, wg2_ref[...]) + bg_ref[...]
    o_ref[:, GLA_PROJ:GLA_COLS] = _log_sigmoid(gk) / GLA_NORMALIZER


def _gla_proj(x, mod, layer, norm_g, w_cat, wg1, wg2, bg):
    return pl.pallas_call(
        _gla_proj_kernel,
        grid=(N_TOK // TM,),
        in_specs=[
            pl.BlockSpec((TM, D), lambda t: (t, 0)),
            _vec_spec(D),
            _mod_spec(layer, 1, _row_tm),
            _mod_spec(layer, 0, _row_tm),
            _full_spec((D, GLA_PROJ)),
            _full_spec((D, GK1_PAD)),
            _full_spec((GK1_PAD, 2 * GLA_H * GLA_DK)),
            _vec_spec(2 * GLA_H * GLA_DK),
        ],
        out_specs=pl.BlockSpec((TM, GLA_COLS), lambda t: (t, 0)),
        out_shape=jax.ShapeDtypeStruct((N_TOK, GLA_COLS), F32),
        compiler_params=_cparams(1),
        name="gla_proj",
    )(x, norm_g.reshape(1, D), mod, mod, w_cat, wg1, wg2, bg)


def _gla_core_kernel(*refs, L, has_s0, hps):
    if has_s0:
        q_ref, k_ref, v_ref, g_ref, gkf_ref, gkb_ref, tri_ref, ng_ref, s0_ref, o_ref, sf_ref, acc = refs
    else:
        q_ref, k_ref, v_ref, g_ref, gkf_ref, gkb_ref, tri_ref, ng_ref, o_ref, sf_ref, acc = refs
        s0_ref = None
    C = GLA_CHUNK
    n = L // C
    ri = lax.broadcasted_iota(jnp.int32, (C, C), 0)
    ci = lax.broadcasted_iota(jnp.int32, (C, C), 1)
    nt_dims = (((1,), (1,)), ((), ()))
    tn_dims = (((0,), (0,)), ((), ()))

    for hh in range(hps):
        kc = slice(hh * GLA_DK, (hh + 1) * GLA_DK)
        vc = slice(hh * GLA_DV, (hh + 1) * GLA_DV)
        for direction, gk_ref in enumerate((gkf_ref, gkb_ref)):
            keep = (ci <= ri) if direction == 0 else (ci >= ri)
            last = C - 1 if direction == 0 else 0
            gk_hi, gk_lo = _split(gk_ref[:, kc])
            b_all = _dot(tri_ref[direction], jnp.concatenate([gk_hi, gk_lo], axis=1))
            b_all = b_all[:, :GLA_DK] + b_all[:, GLA_DK:]
            st = s0_ref[direction, hh].T if has_s0 else jnp.zeros((GLA_DV, GLA_DK), F32)
            order = range(n) if direction == 0 else range(n - 1, -1, -1)
            for c in order:
                rows = slice(c * C, (c + 1) * C)
                b = b_all[rows]
                b_last = b[last:last + 1, :]
                q = q_ref[rows, kc]
                k = k_ref[rows, kc]
                v = v_ref[rows, vc].astype(BF16)
                qe = (q * jnp.exp(b)).astype(BF16)
                ke = (k * jnp.exp(-b)).astype(BF16)
                kd = (k * jnp.exp(b_last - b)).astype(BF16)
                scores = lax.dot_general(qe, ke, nt_dims, preferred_element_type=F32)
                scores = jnp.where(keep, scores, 0.0).astype(BF16)
                o = _dot(scores, v) + lax.dot_general(qe, st.astype(BF16), nt_dims, preferred_element_type=F32)
                if direction == 0:
                    acc[rows, vc] = o
                else:
                    acc[rows, vc] = acc[rows, vc] + o
                st = jnp.exp(b_last) * st + lax.dot_general(v, kd, tn_dims, preferred_element_type=F32)
            sf_ref[direction, hh] = st.T

        o = acc[:, vc]
        o = o * lax.rsqrt(jnp.mean(o * o, axis=-1, keepdims=True) + EPS) * ng_ref[...]
        o_ref[:, vc] = (o * g_ref[:, vc]).astype(BF16)


def _gla_core(proj, tri, norm_g, s0, L, n_batch, row_off, hps):
    rb = row_off // L
    H = GLA_H
    nh = H // hps
    has_s0 = s0 is not None
    kern = functools.partial(_gla_core_kernel, L=L, has_s0=has_s0, hps=hps)
    kb, vb = GLA_DK * hps, GLA_DV * hps
    in_specs = [
        pl.BlockSpec((L, kb), lambda b, h: (rb + b, h)),
        pl.BlockSpec((L, kb), lambda b, h: (rb + b, nh + h)),
        pl.BlockSpec((L, vb), lambda b, h: (rb + b, (2 * H * GLA_DK) // vb + h)),
        pl.BlockSpec((L, vb), lambda b, h: (rb + b, (2 * H * GLA_DK) // vb + nh + h)),
        pl.BlockSpec((L, kb), lambda b, h: (rb + b, GLA_PROJ // kb + h)),
        pl.BlockSpec((L, kb), lambda b, h: (rb + b, GLA_PROJ // kb + nh + h)),
        _full_spec((2, L, L)),
        _vec_spec(GLA_DV),
    ]
    args = [proj] * 6 + [tri, norm_g.reshape(1, GLA_DV)]
    state_spec = pl.BlockSpec((None, 2, hps, GLA_DK, GLA_DV), lambda b, h: (b, 0, h, 0, 0))
    if has_s0:
        in_specs.append(state_spec)
        args.append(s0)
    return pl.pallas_call(
        kern,
        grid=(n_batch, nh),
        in_specs=in_specs,
        out_specs=[pl.BlockSpec((L, vb), lambda b, h: (b, h)), state_spec],
        out_shape=[
            jax.ShapeDtypeStruct((n_batch * L, H * GLA_DV), BF16),
            jax.ShapeDtypeStruct((n_batch, 2, H, GLA_DK, GLA_DV), F32),
        ],
        scratch_shapes=[pltpu.VMEM((L, vb), F32)],
        compiler_params=_cparams(2),
        name=f"gla_core_L{L}",
    )(*args)


def _fnet_mats(L):
    c = np.arange(FNET_C)
    ang_c = 2.0 * np.pi * np.outer(c, c) / FNET_C
    chan = np.concatenate([np.cos(ang_c), np.sin(ang_c)], axis=1) / math.sqrt(FNET_C)
    t = np.arange(L)
    ang_l = 2.0 * np.pi * np.outer(t, t) / L
    seq = np.concatenate([np.cos(ang_l), -np.sin(ang_l)], axis=1) / math.sqrt(L)
    return jnp.asarray(chan, F32).astype(BF16), jnp.asarray(seq, F32).astype(BF16)


def _fnet_kernel(x_ref, g_ref, sc_ref, sh_ref, gate_ref, chan_ref, seq_ref, w_ref, b_ref, o_ref):
    x = x_ref[...]
    h = _norm_mod(x, g_ref[...], sc_ref[...], sh_ref[...]).astype(BF16)
    chan = chan_ref[...]
    cos_parts, sin_parts = [], []
    for gi in range(FNET_GROUPS):
        cs = _dot(h[:, gi * FNET_C:(gi + 1) * FNET_C], chan)
        cos_parts.append(cs[:, :FNET_C])
        sin_parts.append(cs[:, FNET_C:])
    stacked = jnp.concatenate([jnp.concatenate(cos_parts, axis=1), jnp.concatenate(sin_parts, axis=1)], axis=0)
    mixed = _dot(seq_ref[...], stacked.astype(BF16))
    y = _dot(mixed.astype(BF16), w_ref[...]) + b_ref[...]
    o_ref[...] = x + gate_ref[...] * y


def _fnet(x, mod, layer, norm_g, chan, seq, w_bf16, bias, L, n_batch, row_off):
    rb = row_off // L
    lat = row_off > 0

    def row_fn(b):
        return 1 + b if lat else 0

    return pl.pallas_call(
        _fnet_kernel,
        grid=(n_batch,),
        in_specs=[
            pl.BlockSpec((L, D), lambda b: (rb + b, 0)),
            _vec_spec(D),
            _mod_spec(layer, 1, row_fn),
            _mod_spec(layer, 0, row_fn),
            _mod_spec(layer, 2, row_fn),
            _full_spec((FNET_C, 2 * FNET_C)),
            _full_spec((L, 2 * L)),
            _full_spec((D, D)),
            _vec_spec(D),
        ],
        out_specs=pl.BlockSpec((L, D), lambda b: (rb + b, 0)),
        out_shape=jax.ShapeDtypeStruct((N_TOK, D), F32),
        input_output_aliases={0: 0},
        compiler_params=_cparams(1),
        name=f"fnet_L{L}",
    )(x, norm_g.reshape(1, D), mod, mod, mod, chan, seq, w_bf16, bias.reshape(1, D))


def _window_bounds(n, k):
    t = np.arange(n)
    lo, hi = k // 2, k - k // 2 - 1
    return np.maximum(t - lo, 0), np.minimum(t + hi + 1, n)


def _pool_mats(L, grid_rows):
    mats, inv = [], []
    for k in POOL_WINDOWS:
        if grid_rows is None:
            s, e = _window_bounds(L, k)
            idx = np.arange(L)[None, :]
            m = ((idx >= s[:, None]) & (idx < e[:, None])).astype(np.float64)
            cnt = (e - s).astype(np.float64)
        else:
            sr, er = _window_bounds(grid_rows, k)
            sc, ec = _window_bounds(GRID_W, k)
            ir = np.arange(grid_rows)[None, :]
            ic = np.arange(GRID_W)[None, :]
            mr = ((ir >= sr[:, None]) & (ir < er[:, None])).astype(np.float64)
            mc = ((ic >= sc[:, None]) & (ic < ec[:, None])).astype(np.float64)
            m = np.kron(mr, mc)
            cnt = np.kron((er - sr).astype(np.float64), (ec - sc).astype(np.float64))
        mats.append(m)
        inv.append(1.0 / cnt)
    return jnp.asarray(np.stack(mats), BF16), jnp.asarray(np.stack(inv)[:, :, None], F32)


def _pool_kernel(x_ref, g_ref, sc_ref, sh_ref, gate_ref, m_ref, ic_ref, w_ref, b_ref, ps_ref, o_ref):
    x = x_ref[...]
    h = _norm_mod(x, g_ref[...], sc_ref[...], sh_ref[...])
    outs = []
    for gi in range(len(POOL_WINDOWS)):
        hg = h[:, gi * POOL_G:(gi + 1) * POOL_G]
        hi, lo = _split(hg)
        m = m_ref[gi]
        mean = (_dot(m, hi) + _dot(m, lo)) * ic_ref[gi]
        outs.append(_dot((mean - hg).astype(BF16), w_ref[gi]))
    y = (jnp.concatenate(outs, axis=1) + b_ref[...]) * ps_ref[...]
    o_ref[...] = x + gate_ref[...] * y


def _pool(x, mod, layer, norm_g, mats, inv_cnt, w_bf16, bias, scale, L, n_batch, row_off):
    rb = row_off // L
    lat = row_off > 0
    G = len(POOL_WINDOWS)

    def row_fn(b):
        return 1 + b if lat else 0

    return pl.pallas_call(
        _pool_kernel,
        grid=(n_batch,),
        in_specs=[
            pl.BlockSpec((L, D), lambda b: (rb + b, 0)),
            _vec_spec(D),
            _mod_spec(layer, 1, row_fn),
            _mod_spec(layer, 0, row_fn),
            _mod_spec(layer, 2, row_fn),
            _full_spec((G, L, L)),
            _full_spec((G, L, 1)),
            _full_spec((G, POOL_G, POOL_G)),
            _vec_spec(D),
            _vec_spec(D),
        ],
        out_specs=pl.BlockSpec((L, D), lambda b: (rb + b, 0)),
        out_shape=jax.ShapeDtypeStruct((N_TOK, D), F32),
        input_output_aliases={0: 0},
        compiler_params=_cparams(1),
        name=f"pool_L{L}",
    )(x, norm_g.reshape(1, D), mod, mod, mod, mats, inv_cnt, w_bf16, bias.reshape(1, D), scale.reshape(1, D))


ROUTER_PAD = 128
LANES = 128
D_EXT = D + LANES
MOE_TILE = 1024
MOE_MAX_TILES = N_TOK // MOE_TILE + MOE_GROUPS
MOE_ROWS = MOE_MAX_TILES * MOE_TILE
MOE_Y_ROWS = N_TOK + 2 * MOE_TILE
MOE_DMA_PER_STEP = MOE_TILE // MOE_PER_GROUP


ROUTE_ROWS = 8


def _moe_route_kernel(x_ref, g_ref, sc_ref, sh_ref, wr_ref, tri_ref, h3_ref, route_ref, cnt_ref, carry):
    t = pl.program_id(0)

    @pl.when(t == 0)
    def _():
        carry[...] = jnp.zeros_like(carry)

    h = _norm_mod(x_ref[...], g_ref[...], sc_ref[...], sh_ref[...])
    w_hi, w_lo = _split(wr_ref[...])
    h_hi, h_lo = _split(h)
    nt = (((1,), (1,)), ((), ()))
    logits = (lax.dot_general(w_hi, h_hi, nt, preferred_element_type=F32)
              + (lax.dot_general(w_hi, h_lo, nt, preferred_element_type=F32)
                 + lax.dot_general(w_lo, h_hi, nt, preferred_element_type=F32)))
    neg = jnp.float32(-jnp.inf)
    r8 = lax.broadcasted_iota(jnp.int32, (ROUTE_ROWS, TM), 0)
    r16 = lax.broadcasted_iota(jnp.int32, (MOE_E, TM), 0)
    gl = jnp.where(r8 < MOE_GROUPS, logits[MOE_E:MOE_E + ROUTE_ROWS], neg)
    g_max = jnp.max(gl, axis=0, keepdims=True)
    g_idx = jnp.min(jnp.where(gl == g_max, r8, ROUTE_ROWS), axis=0, keepdims=True)
    p_grp = 1.0 / jnp.sum(jnp.exp(gl - g_max), axis=0, keepdims=True)
    in_grp = (r16 >> 2) == g_idx
    el = jnp.where(in_grp, logits[:MOE_E], neg)
    m1 = jnp.max(el, axis=0, keepdims=True)
    i1 = jnp.min(jnp.where(el == m1, r16, MOE_E), axis=0, keepdims=True)
    z = jnp.sum(jnp.exp(el - m1), axis=0, keepdims=True)
    el2 = jnp.where(r16 == i1, neg, el)
    m2 = jnp.max(el2, axis=0, keepdims=True)
    i2 = jnp.min(jnp.where(el2 == m2, r16, MOE_E), axis=0, keepdims=True)
    p1 = 1.0 / z
    p2 = jnp.exp(m2 - m1) / z
    tot = p1 + p2
    eid = r8 + MOE_PER_GROUP * g_idx
    in4 = r8 < MOE_PER_GROUP
    cw4 = (jnp.where(in4 & (eid == i1), p_grp * (p1 / tot), 0.0)
           + jnp.where(in4 & (eid == i2), p_grp * (p2 / tot), 0.0))
    member = jnp.where(r8 == g_idx, 1.0, 0.0)
    before = _dot(member.astype(BF16), tri_ref[...]) + carry[:, 0:1]
    rank = jnp.sum(jnp.where(r8 == g_idx, before, 0.0), axis=0, keepdims=True)
    carry[...] = carry[...] + jnp.sum(member, axis=1, keepdims=True)
    cnt_ref[...] = carry[...].astype(jnp.int32)
    route_ref[...] = jnp.where(r8 == 0, g_idx, jnp.where(r8 == 1, rank.astype(jnp.int32), 0))
    h3_ref[:, :D] = h
    cw_rows = jnp.concatenate([cw4, jnp.zeros((LANES - ROUTE_ROWS, TM), F32)], axis=0)
    h3_ref[:, D:] = cw_rows.T


def _moe_route(x, mod, layer, norm_g, w_router_t, tri):
    return pl.pallas_call(
        _moe_route_kernel,
        grid=(N_TOK // TM,),
        in_specs=[
            pl.BlockSpec((TM, D), lambda t: (t, 0)),
            _vec_spec(D),
            _mod_spec(layer, 4, _row_tm),
            _mod_spec(layer, 3, _row_tm),
            _full_spec((ROUTER_PAD, D)),
            _full_spec((TM, TM)),
        ],
        out_specs=[
            pl.BlockSpec((TM, D_EXT), lambda t: (t, 0)),
            pl.BlockSpec((ROUTE_ROWS, TM), lambda t: (0, t)),
            pl.BlockSpec((ROUTE_ROWS, LANES), lambda t: (0, 0)),
        ],
        out_shape=[
            jax.ShapeDtypeStruct((N_TOK, D_EXT), F32),
            jax.ShapeDtypeStruct((ROUTE_ROWS, N_TOK), jnp.int32),
            jax.ShapeDtypeStruct((ROUTE_ROWS, LANES), jnp.int32),
        ],
        scratch_shapes=[pltpu.VMEM((ROUTE_ROWS, LANES), F32)],
        compiler_params=_cparams(1),
        name="moe_route",
    )(x, norm_g.reshape(1, D), mod, mod, w_router_t, tri)


def _moe_invert_kernel(pos_ref, src_ref):
    def mark(i, carry):
        src_ref[i] = jnp.int32(N_TOK)
        return carry

    lax.fori_loop(0, MOE_ROWS, mark, 0, unroll=16)

    def place(n, carry):
        src_ref[pos_ref[n]] = n
        return carry

    lax.fori_loop(0, N_TOK, place, 0, unroll=16)


def _moe_invert(pos):
    smem = pl.BlockSpec(memory_space=pltpu.SMEM)
    return pl.pallas_call(
        _moe_invert_kernel,
        in_specs=[smem],
        out_specs=smem,
        out_shape=jax.ShapeDtypeStruct((MOE_ROWS,), jnp.int32),
        name="moe_invert",
    )(pos)


def _moe_expert_kernel(src_ref, grp_ref, nact_ref, h_hbm, w1_ref, w3_ref, w2_ref, y_hbm,
                       xb0, xb1, ab0, ab1, gsem, ssem):
    t = pl.program_id(0)
    k = pl.program_id(1)
    n_active = nact_ref[0]
    last = n_active - 1
    T = MOE_TILE
    xbufs, accs = (xb0, xb1), (ab0, ab1)

    def gather_row(tile, slot, r):
        tok = jnp.minimum(src_ref[tile * T + r], N_TOK - 1)
        return pltpu.make_async_copy(h_hbm.at[pl.ds(tok, 1), :], xbufs[slot].at[pl.ds(r, 1), :], gsem.at[slot])

    def scatter_row(tile, slot, r, real):
        tok = src_ref[tile * T + r]
        dst = jnp.where(real & (tok < N_TOK), tok, N_TOK + slot * T + r)
        return pltpu.make_async_copy(accs[slot].at[pl.ds(r, 1), :], y_hbm.at[pl.ds(dst, 1), :], ssem.at[slot])

    def wait_gather(slot):
        pltpu.make_async_copy(h_hbm.at[pl.ds(0, T), :], xbufs[slot], gsem.at[slot]).wait()

    def wait_scatter(slot):
        pltpu.make_async_copy(accs[slot], y_hbm.at[pl.ds(0, T), :], ssem.at[slot]).wait()

    def step(slot):
        other = 1 - slot
        xb, acc = xbufs[slot], accs[slot]

        @pl.when(k == 0)
        def _():
            if slot == 0:
                @pl.when(t == 0)
                def _():
                    def first(r, carry):
                        gather_row(0, 0, r).start()
                        return carry

                    lax.fori_loop(0, T, first, 0, unroll=8)
                    ab0[...] = jnp.zeros_like(ab0)
                    ab1[...] = jnp.zeros_like(ab1)
                    dumps = [pltpu.make_async_copy(ab1, y_hbm.at[pl.ds(N_TOK + s * T, T), :], ssem.at[1])
                             for s in (0, 1)]
                    for cp in dumps:
                        cp.start()
                    for cp in dumps:
                        cp.wait()

            wait_gather(slot)

            @pl.when(t >= 1)
            def _():
                wait_scatter(slot)

        nxt = jnp.minimum(t + 1, last)
        prev = jnp.maximum(t - 1, 0)
        for r in range(MOE_DMA_PER_STEP):
            row = k * MOE_DMA_PER_STEP + r
            gather_row(nxt, other, row).start(priority=r % 2)
            scatter_row(prev, other, row, t >= 1).start(priority=(r + 1) % 2)

        x = xb[:, :D].astype(BF16)
        a = _dot(x, w1_ref[...].astype(BF16))
        b = _dot(x, w3_ref[...].astype(BF16))
        lane = lax.broadcasted_iota(jnp.int32, (T, LANES), 1)
        cwk = jnp.sum(jnp.where(lane == k, xb[:, D:], 0.0), axis=-1, keepdims=True)
        hid = (_silu(a) * b * cwk).astype(BF16)
        acc[...] = jnp.where(k > 0, acc[...], 0.0) + _dot(hid, w2_ref[...].astype(BF16))

        @pl.when((k == MOE_PER_GROUP - 1) & (t == last))
        def _():
            wait_gather(other)
            wait_scatter(other)

            def final(r, carry):
                scatter_row(t, slot, r, True).start()
                return carry

            lax.fori_loop(0, T, final, 0, unroll=8)
            wait_scatter(slot)

    for slot in (0, 1):
        pl.when((t < n_active) & (t % 2 == slot))(functools.partial(step, slot))


def _moe_experts(h_ext, src, tile_group, n_active, layer, w1, w3, w2):
    T = MOE_TILE

    def w_index(t, k, src_ref, grp_ref, nact_ref):
        last = nact_ref[0] - 1
        e = jnp.where(t <= last, grp_ref[t] * MOE_PER_GROUP + k, grp_ref[last] * MOE_PER_GROUP + MOE_PER_GROUP - 1)
        return (layer, e, 0, 0)

    grid_spec = pltpu.PrefetchScalarGridSpec(
        num_scalar_prefetch=3,
        grid=(MOE_MAX_TILES, MOE_PER_GROUP),
        in_specs=[
            pl.BlockSpec(memory_space=pl.ANY),
            pl.BlockSpec((None, None, D, MOE_HID), w_index),
            pl.BlockSpec((None, None, D, MOE_HID), w_index),
            pl.BlockSpec((None, None, MOE_HID, D), w_index),
        ],
        out_specs=pl.BlockSpec(memory_space=pl.ANY),
        scratch_shapes=[
            pltpu.VMEM((T, D_EXT), F32),
            pltpu.VMEM((T, D_EXT), F32),
            pltpu.VMEM((T, D), F32),
            pltpu.VMEM((T, D), F32),
            pltpu.SemaphoreType.DMA((2,)),
            pltpu.SemaphoreType.DMA((2,)),
        ],
    )
    return pl.pallas_call(
        _moe_expert_kernel,
        grid_spec=grid_spec,
        out_shape=jax.ShapeDtypeStruct((MOE_Y_ROWS, D), F32),
        compiler_params=_cparams(2),
        name="moe_experts",
    )(src, tile_group, n_active, h_ext, w1, w3, w2)


def _moe_combine_kernel(y_ref, gate_ref, x_ref, o_ref):
    o_ref[...] = x_ref[...] + gate_ref[...] * y_ref[...]


def _moe_combine(x, y3, mod, layer):
    return pl.pallas_call(
        _moe_combine_kernel,
        grid=(N_TOK // TM,),
        in_specs=[
            pl.BlockSpec((TM, D), lambda t: (t, 0)),
            _mod_spec(layer, 5, _row_tm),
            pl.BlockSpec((TM, D), lambda t: (t, 0)),
        ],
        out_specs=pl.BlockSpec((TM, D), lambda t: (t, 0)),
        out_shape=jax.ShapeDtypeStruct((N_TOK, D), F32),
        input_output_aliases={2: 0},
        compiler_params=_cparams(1),
        name="moe_combine",
    )(y3, mod, x)


def _moe(x, mod, layer, norm_g, w_rg, w_re, w1, w3, w2, tri, final_g=None):
    w_router_t = jnp.zeros((ROUTER_PAD, D), F32).at[:MOE_E].set(w_re.T).at[MOE_E:MOE_E + MOE_GROUPS].set(w_rg.T)
    h3, route, counts = _moe_route(x, mod, layer, norm_g, w_router_t, tri)
    cnt = counts[:MOE_GROUPS, 0]
    ntile = (cnt + MOE_TILE - 1) // MOE_TILE
    tile_end = jnp.cumsum(ntile)
    seg_start = (tile_end - ntile) * MOE_TILE
    g_idx, rank = route[0], route[1]
    pos = jnp.sum(jnp.where(g_idx[None, :] == jnp.arange(MOE_GROUPS)[:, None], seg_start[:, None], 0), axis=0) + rank
    tiles = jnp.arange(MOE_MAX_TILES, dtype=jnp.int32)
    tile_group = jnp.minimum(jnp.sum(tiles[:, None] >= tile_end[None, :], axis=1), MOE_GROUPS - 1).astype(jnp.int32)
    n_active = tile_end[-1:].astype(jnp.int32)
    src = _moe_invert(pos.astype(jnp.int32))
    y3 = _moe_experts(h3, src, tile_group, n_active, layer, w1, w3, w2)
    if final_g is None:
        return _moe_combine(x, y3, mod, layer)
    return tuple(_moe_combine_norm(x, y3, mod, layer, final_g, off, nb * L) for L, nb, off in
                 ((CTX_L, CTX_B, 0), (LAT_L, LAT_B, N_CTX)))


def _combine_norm_kernel(y_ref, gate_ref, x_ref, g_ref, o_ref):
    x = x_ref[...] + gate_ref[...] * y_ref[...]
    o_ref[...] = x * lax.rsqrt(jnp.mean(x * x, axis=-1, keepdims=True) + EPS) * g_ref[...]


def _moe_combine_norm(x, y3, mod, layer, final_g, row_off, n_rows):
    off = row_off // TM
    first_lat = N_CTX // TM

    def row_fn(t):
        g = t + off
        return jnp.where(g < first_lat, 0, 1 + (g - first_lat) // (LAT_L // TM))

    return pl.pallas_call(
        _combine_norm_kernel,
        grid=(n_rows // TM,),
        in_specs=[
            pl.BlockSpec((TM, D), lambda t: (t + off, 0)),
            _mod_spec(layer, 5, row_fn),
            pl.BlockSpec((TM, D), lambda t: (t + off, 0)),
            _vec_spec(D),
        ],
        out_specs=pl.BlockSpec((TM, D), lambda t: (t, 0)),
        out_shape=jax.ShapeDtypeStruct((n_rows, D), F32),
        compiler_params=_cparams(1),
        name="combine_final_norm",
    )(y3, mod, x, final_g.reshape(1, D))


def kernel(x_prompt, x_sample, state_gla, c, c_ctx, w_ada, b_ada, norm_g, hy_w_in, hy_b_in, hy_conv_w, hy_conv_b, hy_f_w1, hy_f_b1, hy_f_freq, hy_f_w2, hy_f_b2, hy_f_w3, hy_skip, hy_w_out, hy_b_out, gla_w_q, gla_w_k, gla_w_v, gla_w_g, gla_w_gk1, gla_w_gk2, gla_b_gk, gla_norm_g, gla_w_o, fn_w_out, fn_b_out, pool_w, pool_b, pool_scale, moe_w_rg, moe_w_re, moe_w1, moe_w3, moe_w2, final_g):
    groups = ((CTX_L, CTX_B, 0, None), (LAT_L, LAT_B, N_CTX, LAT_L // GRID_W))

    x = jnp.concatenate([x_prompt.reshape(N_CTX, D), x_sample.reshape(N_LAT, D)], axis=0)
    cond = jnp.zeros((MOD_ROWS, D), F32).at[0].set(c_ctx).at[1:1 + LAT_B].set(c)
    mod = _ada_table(cond, w_ada, b_ada).reshape(DEPTH * MOD_ROWS * 6, 1, D)

    tri_tm = jnp.asarray(np.triu(np.ones((TM, TM)), 1), BF16)

    new_states = []
    for i in range(DEPTH):
        kind, j = i % 4, i // 4
        if kind == 0:
            u = _hyena_in(x, mod, i, norm_g[i, 0], hy_w_in[j].astype(BF16), hy_b_in[j], hy_conv_w[j], hy_conv_b[j])
            w_out = hy_w_out[j].astype(BF16)
            for L, nb, off, _ in groups:
                fwd, ff, inv = _dft_mats(L)
                khat = _hyena_filters(L, ff, hy_f_w1[j], hy_f_b1[j], hy_f_freq[j], hy_f_w2[j], hy_f_b2[j],
                                      hy_f_w3[j])
                z = _hyena_conv(u, khat, hy_skip[j], fwd, inv, L, nb, off, D if L == CTX_L else 512)
                x = _outproj(x, z, w_out, hy_b_out[j], mod, i, off, nb * L)
        elif kind == 1:
            w_cat = jnp.concatenate([gla_w_q[j], gla_w_k[j], gla_w_v[j], gla_w_g[j]], axis=1).astype(BF16)
            nk = GLA_H * GLA_DK
            wg1 = jnp.zeros((D, GK1_PAD), F32).at[:, :GLA_RANK].set(gla_w_gk1[j, 0])
            wg1 = wg1.at[:, GLA_RANK:2 * GLA_RANK].set(gla_w_gk1[j, 1]).astype(BF16)
            wg2 = jnp.zeros((GK1_PAD, 2 * nk), F32).at[:GLA_RANK, :nk].set(gla_w_gk2[j, 0])
            wg2 = wg2.at[GLA_RANK:2 * GLA_RANK, nk:].set(gla_w_gk2[j, 1]).astype(BF16)
            proj = _gla_proj(x, mod, i, norm_g[i, 0], w_cat, wg1, wg2, gla_b_gk[j].reshape(1, 2 * nk))
            lower = np.tril(np.ones((GLA_CHUNK, GLA_CHUNK)))
            w_o = gla_w_o[j].astype(BF16)
            for L, nb, off, grid_rows in groups:
                eye = np.eye(L // GLA_CHUNK)
                tri = jnp.asarray(np.stack([np.kron(eye, lower), np.kron(eye, lower.T)]), BF16)
                s0 = None if grid_rows is None else state_gla[:, j]
                o, s_fin = _gla_core(proj, tri, gla_norm_g[j], s0, L, nb, off, GLA_H if L == CTX_L else 1)
                if grid_rows is None:
                    new_states.append(s_fin)
                x = _outproj(x, o, w_o, jnp.zeros((D,), F32), mod, i, off, nb * L)
        elif kind == 2:
            w_out = fn_w_out[j].astype(BF16)
            for L, nb, off, _ in groups:
                chan, seq = _fnet_mats(L)
                x = _fnet(x, mod, i, norm_g[i, 0], chan, seq, w_out, fn_b_out[j], L, nb, off)
        else:
            w_pool = pool_w[j].astype(BF16)
            for L, nb, off, grid_rows in groups:
                mats, inv_cnt = _pool_mats(L, grid_rows)
                x = _pool(x, mod, i, norm_g[i, 0], mats, inv_cnt, w_pool, pool_b[j], pool_scale[j], L, nb, off)

        x = _moe(x, mod, i, norm_g[i, 1], moe_w_rg[i], moe_w_re[i], moe_w1, moe_w3, moe_w2, tri_tm,
                 final_g if i == DEPTH - 1 else None)

    y_prompt, y_sample = x
    new_state_gla = jnp.stack(new_states, axis=1)
    return (y_prompt.reshape(CTX_B, CTX_L, D), y_sample.reshape(LAT_B, LAT_L, D), new_state_gla)
```

```python
import functools
import math

import jax
import jax.numpy as jnp
import numpy as np
from jax import lax
from jax.experimental import pallas as pl
from jax.experimental.pallas import tpu as pltpu

F32 = jnp.float32
BF16 = jnp.bfloat16

D = 1024
CTX_B, CTX_L = 32, 256
LAT_B, LAT_L = 2, 1024
N_CTX = CTX_B * CTX_L
N_LAT = LAT_B * LAT_L
N_TOK = N_CTX + N_LAT
DEPTH = 4
GRID_W = 64
EPS = 1e-6

HY_BANDS = 8
HY_EMB = 1 + 2 * HY_BANDS
HY_EMB_PAD = 32
HY_HID = 64
HY_FAST_DECAY = 0.3
HY_SLOW_DECAY = 1.5
HY_DECAY_TARGET = 1e-2

GLA_H = 4
GLA_DK = 128
GLA_DV = 256
GLA_RANK = 16
GLA_NORMALIZER = 16.0
GLA_CHUNK = 64

FNET_GROUPS = 4
FNET_C = D // FNET_GROUPS
POOL_WINDOWS = (2, 4, 8, 16)
POOL_G = D // len(POOL_WINDOWS)

MOE_GROUPS = 4
MOE_PER_GROUP = 4
MOE_E = MOE_GROUPS * MOE_PER_GROUP
MOE_HID = D // 2

MOD_ROWS = 8
TM = 512
TM_BIG = 1024
VMEM_LIMIT = 56 * 1024 * 1024


def _cparams(n_axes):
    return pltpu.CompilerParams(dimension_semantics=("arbitrary",) * n_axes, vmem_limit_bytes=VMEM_LIMIT)


def _norm_mod(x, g, sc, sh):
    ms = jnp.mean(x * x, axis=-1, keepdims=True)
    return (x * lax.rsqrt(ms + EPS) * g) * (1.0 + sc) + sh


def _split(a):
    hi = a.astype(BF16)
    lo = (a - hi.astype(F32)).astype(BF16)
    return hi, lo


def _dot(a, b):
    return jnp.dot(a, b, preferred_element_type=F32)


def _dot_precise(a, b):
    a_hi, a_lo = _split(a)
    b_hi, b_lo = _split(b)
    return _dot(a_hi, b_hi) + (_dot(a_hi, b_lo) + _dot(a_lo, b_hi))


def _silu(x):
    return x * (1.0 / (1.0 + jnp.exp(-x)))


def _log_sigmoid(x):
    return jnp.minimum(x, 0.0) - jnp.log(1.0 + jnp.exp(-jnp.abs(x)))


def _mod_spec(layer, chunk, row_fn):
    base = layer * MOD_ROWS * 6 + chunk

    def index_map(*ids):
        return (base + row_fn(*ids) * 6, 0, 0)

    return pl.BlockSpec((None, 1, D), index_map)


def _row_tm(t, *_):
    return jnp.where(t < N_CTX // TM, 0, 1 + (t - N_CTX // TM) // (LAT_L // TM))


def _row_big(t, *_):
    return jnp.where(t < N_CTX // TM_BIG, 0, 1 + (t - N_CTX // TM_BIG) // (LAT_L // TM_BIG))


def _vec_spec(n):
    return pl.BlockSpec((1, n), lambda *ids: (0, 0))


def _full_spec(shape):
    nd = len(shape)
    return pl.BlockSpec(shape, lambda *ids: (0,) * nd)


def _ada_kernel(cond_ref, w_ref, b_ref, o_ref):
    s = _silu(cond_ref[...]).astype(BF16)
    o_ref[...] = _dot(s, w_ref[...].astype(BF16)) + b_ref[...]


def _ada_table(cond, w_ada, b_ada):
    tn = 1536
    return pl.pallas_call(
        _ada_kernel,
        grid=(DEPTH, 6 * D // tn),
        in_specs=[
            pl.BlockSpec((MOD_ROWS, D), lambda i, j: (0, 0)),
            pl.BlockSpec((None, D, tn), lambda i, j: (i, 0, j)),
            pl.BlockSpec((None, 1, tn), lambda i, j: (i, 0, j)),
        ],
        out_specs=pl.BlockSpec((None, MOD_ROWS, tn), lambda i, j: (i, 0, j)),
        out_shape=jax.ShapeDtypeStruct((DEPTH, MOD_ROWS, 6 * D), F32),
        compiler_params=_cparams(2),
        name="ada_table",
    )(cond, w_ada, b_ada.reshape(DEPTH, 1, 6 * D))


def _pending_specs(pend, mod, block_rows, row_index, row_fn):
    if pend is None:
        return [], []
    y, prev_layer = pend
    return [pl.BlockSpec((block_rows, D), row_index), _mod_spec(prev_layer, 5, row_fn)], [y, mod]


def _read_x(x_ref, pending_refs):
    x = x_ref[...]
    if pending_refs:
        y_ref, gate_ref = pending_refs
        x = x + gate_ref[...] * y_ref[...]
    return x


def _outproj_kernel(z_ref, w_ref, b_ref, gate_ref, x_ref, *rest):
    *pending, o_ref = rest
    y = _dot(z_ref[...], w_ref[...]) + b_ref[...]
    o_ref[...] = _read_x(x_ref, pending) + gate_ref[...] * y


def _outproj(x, z, w_bf16, bias, mod, layer, row_off, n_rows, pend=None):
    k = z.shape[1]
    off = row_off // TM
    first_lat = N_CTX // TM

    def row_fn(t):
        g = t + off
        return jnp.where(g < first_lat, 0, 1 + (g - first_lat) // (LAT_L // TM))

    p_specs, p_args = _pending_specs(pend, mod, TM, lambda t: (t + off, 0), row_fn)
    return pl.pallas_call(
        _outproj_kernel,
        grid=(n_rows // TM,),
        in_specs=[
            pl.BlockSpec((TM, k), lambda t: (t, 0)),
            _full_spec((k, D)),
            _vec_spec(D),
            _mod_spec(layer, 2, row_fn),
            pl.BlockSpec((TM, D), lambda t: (t + off, 0)),
        ] + p_specs,
        out_specs=pl.BlockSpec((TM, D), lambda t: (t + off, 0)),
        out_shape=jax.ShapeDtypeStruct((N_TOK, D), F32),
        input_output_aliases={4: 0},
        compiler_params=_cparams(1),
        name="outproj_residual",
    )(z, w_bf16, bias.reshape(1, D), mod, x, *p_args)


def _dft_mats(L):
    n2 = 2 * L
    k = np.arange(L)[:, None].astype(np.float64)
    n = np.arange(n2)[None, :].astype(np.float64)
    ang = 2.0 * np.pi * k * n / n2
    full = np.concatenate([np.cos(ang), -np.sin(ang)], axis=0)
    full[L, :] = np.cos(np.pi * np.arange(n2))
    fwd = full[:, :L]
    bwd = np.zeros((n2, L))
    bwd[:, 1:] = full[:, n2 - np.arange(1, L)]
    t = np.arange(L)[:, None].astype(np.float64)
    kk = np.arange(L)[None, :].astype(np.float64)
    ang_i = 2.0 * np.pi * t * kk / n2
    inv_re = np.cos(ang_i) / L
    inv_re[:, 0] = 1.0 / n2
    inv_im = -np.sin(ang_i) / L
    inv_im[:, 0] = np.cos(np.pi * np.arange(L)) / n2
    inv = np.concatenate([inv_re, inv_im], axis=1)
    return tuple(jnp.asarray(m, F32).astype(BF16) for m in (fwd, np.concatenate([fwd, bwd], axis=1), inv))


def _hyena_pos_emb(L):
    pos = np.arange(L, dtype=np.float64)
    bands = np.linspace(1e-4, HY_BANDS - 1, HY_BANDS)
    ang = (2.0 * np.pi * pos / L)[:, None] * bands[None, :]
    z = np.concatenate([(pos / L)[:, None], np.cos(ang), -np.sin(ang)], axis=-1)
    zp = np.zeros((L, HY_EMB_PAD))
    zp[:, :HY_EMB] = z
    return jnp.asarray(zp, F32)


def _hyena_filter_kernel(z_ref, w1_ref, b1_ref, fr_ref, w2_ref, b2_ref, w3f_ref, w3b_ref, ff_ref, o_ref, *, L, tn):
    j = pl.program_id(1)
    fr = fr_ref[...]
    f = jnp.sin(fr * (_dot_precise(z_ref[...], w1_ref[...]) + b1_ref[...]))
    f = jnp.sin(fr * (_dot_precise(f, w2_ref[...]) + b2_ref[...]))
    t_lin = lax.broadcasted_iota(jnp.int32, (L, tn), 0).astype(F32) / float(L - 1)
    ch = (lax.broadcasted_iota(jnp.int32, (L, tn), 1) + j * tn).astype(F32)
    max_decay = math.log(HY_DECAY_TARGET) / HY_FAST_DECAY
    min_decay = math.log(HY_DECAY_TARGET) / HY_SLOW_DECAY
    deltas = min_decay + ch * ((max_decay - min_decay) / float(D - 1))
    window = jnp.exp(-t_lin * jnp.abs(deltas))
    kf = _dot_precise(f, w3f_ref[...]) * window
    kb = _dot_precise(f, w3b_ref[...]) * window
    taps = jnp.concatenate([kf, kb], axis=0).astype(BF16)
    o_ref[...] = _dot(ff_ref[...], taps)


def _hyena_filters(L, ff, f_w1, f_b1, f_freq, f_w2, f_b2, f_w3):
    tn = 512
    nj = D // tn
    w1p = jnp.zeros((HY_EMB_PAD, HY_HID), F32).at[:HY_EMB].set(f_w1)
    kern = functools.partial(_hyena_filter_kernel, L=L, tn=tn)
    return pl.pallas_call(
        kern,
        grid=(2, nj),
        in_specs=[
            _full_spec((L, HY_EMB_PAD)),
            _full_spec((HY_EMB_PAD, HY_HID)),
            _vec_spec(HY_HID),
            _vec_spec(HY_HID),
            _full_spec((HY_HID, HY_HID)),
            _vec_spec(HY_HID),
            pl.BlockSpec((HY_HID, tn), lambda o, j: (0, o * nj + j)),
            pl.BlockSpec((HY_HID, tn), lambda o, j: (0, (2 + o) * nj + j)),
            _full_spec((2 * L, 2 * L)),
        ],
        out_specs=pl.BlockSpec((None, 2 * L, tn), lambda o, j: (o, 0, j)),
        out_shape=jax.ShapeDtypeStruct((2, 2 * L, D), F32),
        compiler_params=_cparams(2),
        name=f"hyena_filters_L{L}",
    )(_hyena_pos_emb(L), w1p, f_b1.reshape(1, -1), f_freq.reshape(1, -1), f_w2, f_b2.reshape(1, -1),
      f_w3, f_w3, ff)


def _hyena_in_kernel(x_ref, g_ref, sc_ref, sh_ref, w_ref, b_ref, cw_ref, cb_ref, o_ref, h_scr):
    t = pl.program_id(0)

    @pl.when(pl.program_id(1) == 0)
    def _():
        h_scr[...] = _norm_mod(x_ref[...], g_ref[...], sc_ref[...], sh_ref[...]).astype(BF16)

    u = _dot(h_scr[...], w_ref[...]) + b_ref[...]
    seq = jnp.where(t < N_CTX // TM_BIG, CTX_L, LAT_L)
    pos = lax.broadcasted_iota(jnp.int32, u.shape, 0) & (seq - 1)
    prev = jnp.where(pos == 0, 0.0, pltpu.roll(u, 1, 0))
    nxt = jnp.where(pos == seq - 1, 0.0, pltpu.roll(u, TM_BIG - 1, 0))
    cw = cw_ref[...]
    o_ref[...] = prev * cw[0:1] + u * cw[1:2] + nxt * cw[2:3] + cb_ref[...]


def _hyena_in(x, mod, layer, norm_g, w_in_bf16, b_in, conv_w, conv_b):
    return pl.pallas_call(
        _hyena_in_kernel,
        grid=(N_TOK // TM_BIG, 3),
        in_specs=[
            pl.BlockSpec((TM_BIG, D), lambda t, p: (t, 0)),
            _vec_spec(D),
            _mod_spec(layer, 1, _row_big),
            _mod_spec(layer, 0, _row_big),
            pl.BlockSpec((D, D), lambda t, p: (0, p)),
            pl.BlockSpec((1, D), lambda t, p: (0, p)),
            pl.BlockSpec((3, D), lambda t, p: (0, p)),
            pl.BlockSpec((1, D), lambda t, p: (0, p)),
        ],
        out_specs=pl.BlockSpec((TM_BIG, D), lambda t, p: (t, p)),
        out_shape=jax.ShapeDtypeStruct((N_TOK, 3 * D), F32),
        scratch_shapes=[pltpu.VMEM((TM_BIG, D), BF16)],
        compiler_params=_cparams(2),
        name="hyena_in",
    )(x, norm_g.reshape(1, D), mod, mod, w_in_bf16, b_in.reshape(1, -1), conv_w, conv_b.reshape(1, -1))


def _hyena_conv_kernel(v_ref, x1_ref, x2_ref, kh_ref, skip_ref, fwd_ref, inv_ref, o_ref, *, L):
    fwd = fwd_ref[...]
    inv = inv_ref[...]
    row0 = lax.broadcasted_iota(jnp.int32, (L, v_ref.shape[1]), 0) == 0

    def long_conv(z, order):
        zh = _dot(fwd, z.astype(BF16))
        zr, zi = zh[:L], zh[L:]
        kr, ki = kh_ref[order, :L, :], kh_ref[order, L:, :]
        pr = jnp.where(row0, zr * kr, zr * kr - zi * ki)
        pi = jnp.where(row0, zi * ki, zr * ki + zi * kr)
        prod = jnp.concatenate([pr, pi], axis=0).astype(BF16)
        return _dot(inv, prod) + z * skip_ref[order:order + 1, :]

    z = x1_ref[...] * long_conv(v_ref[...], 0)
    z = x2_ref[...] * long_conv(z, 1)
    o_ref[...] = z.astype(BF16)


def _hyena_conv(u, khat, skip, fwd, inv, L, n_batch, row_off, tn):
    nj = D // tn
    rb = row_off // L
    kern = functools.partial(_hyena_conv_kernel, L=L)
    return pl.pallas_call(
        kern,
        grid=(nj, n_batch),
        in_specs=[
            pl.BlockSpec((L, tn), lambda j, b: (rb + b, j)),
            pl.BlockSpec((L, tn), lambda j, b: (rb + b, nj + j)),
            pl.BlockSpec((L, tn), lambda j, b: (rb + b, 2 * nj + j)),
            pl.BlockSpec((2, 2 * L, tn), lambda j, b: (0, 0, j)),
            pl.BlockSpec((2, tn), lambda j, b: (0, j)),
            _full_spec((2 * L, L)),
            _full_spec((L, 2 * L)),
        ],
        out_specs=pl.BlockSpec((L, tn), lambda j, b: (b, j)),
        out_shape=jax.ShapeDtypeStruct((n_batch * L, D), BF16),
        compiler_params=_cparams(2),
        name=f"hyena_conv_L{L}",
    )(u, u, u, khat, skip, fwd, inv)


GLA_PROJ = 2 * GLA_H * GLA_DK + 2 * GLA_H * GLA_DV
GLA_COLS = GLA_PROJ + 2 * GLA_H * GLA_DK
GK1_PAD = 128


def _gla_proj_kernel(x_ref, g_ref, sc_ref, sh_ref, w_ref, wg1_ref, wg2_ref, bg_ref, *rest):
    *pending, o_ref = rest
    h = _norm_mod(_read_x(x_ref, pending), g_ref[...], sc_ref[...], sh_ref[...]).astype(BF16)
    p = _dot(h, w_ref[...])
    nq = GLA_H * GLA_DK
    o_ref[:, 0:nq] = p[:, 0:nq] * (GLA_DK ** -0.5)
    o_ref[:, nq:nq + nq + GLA_H * GLA_DV] = p[:, nq:nq + nq + GLA_H * GLA_DV]
    o_ref[:, 2 * nq + GLA_H * GLA_DV:GLA_PROJ] = _silu(p[:, 2 * nq + GLA_H * GLA_DV:GLA_PROJ])
    low = _dot(h, wg1_ref[...]).astype(BF16)
    gk = _dot(low, wg2_ref[...]) + bg_ref[...]
    o_ref[:, GLA_PROJ:GLA_COLS] = _log_sigmoid(gk) / GLA_NORMALIZER


def _gla_proj(x, mod, layer, norm_g, w_cat, wg1, wg2, bg, pend=None):
    p_specs, p_args = _pending_specs(pend, mod, TM, lambda t: (t, 0), _row_tm)
    return pl.pallas_call(
        _gla_proj_kernel,
        grid=(N_TOK // TM,),
        in_specs=[
            pl.BlockSpec((TM, D), lambda t: (t, 0)),
            _vec_spec(D),
            _mod_spec(layer, 1, _row_tm),
            _mod_spec(layer, 0, _row_tm),
            _full_spec((D, GLA_PROJ)),
            _full_spec((D, GK1_PAD)),
            _full_spec((GK1_PAD, 2 * GLA_H * GLA_DK)),
            _vec_spec(2 * GLA_H * GLA_DK),
        ] + p_specs,
        out_specs=pl.BlockSpec((TM, GLA_COLS), lambda t: (t, 0)),
        out_shape=jax.ShapeDtypeStruct((N_TOK, GLA_COLS), F32),
        compiler_params=_cparams(1),
        name="gla_proj",
    )(x, norm_g.reshape(1, D), mod, mod, w_cat, wg1, wg2, bg, *p_args)


def _gla_core_kernel(*refs, L, has_s0, hps):
    if has_s0:
        q_ref, k_ref, v_ref, g_ref, gkf_ref, gkb_ref, tri_ref, ng_ref, s0_ref, o_ref, sf_ref, acc = refs
    else:
        q_ref, k_ref, v_ref, g_ref, gkf_ref, gkb_ref, tri_ref, ng_ref, o_ref, sf_ref, acc = refs
        s0_ref = None
    C = GLA_CHUNK
    n = L // C
    ri = lax.broadcasted_iota(jnp.int32, (C, C), 0)
    ci = lax.broadcasted_iota(jnp.int32, (C, C), 1)
    nt_dims = (((1,), (1,)), ((), ()))
    tn_dims = (((0,), (0,)), ((), ()))

    for hh in range(hps):
        kc = slice(hh * GLA_DK, (hh + 1) * GLA_DK)
        vc = slice(hh * GLA_DV, (hh + 1) * GLA_DV)
        for direction, gk_ref in enumerate((gkf_ref, gkb_ref)):
            keep = (ci <= ri) if direction == 0 else (ci >= ri)
            last = C - 1 if direction == 0 else 0
            gk_hi, gk_lo = _split(gk_ref[:, kc])
            b_all = _dot(tri_ref[direction], jnp.concatenate([gk_hi, gk_lo], axis=1))
            b_all = b_all[:, :GLA_DK] + b_all[:, GLA_DK:]
            st = s0_ref[direction, hh].T if has_s0 else jnp.zeros((GLA_DV, GLA_DK), F32)
            order = range(n) if direction == 0 else range(n - 1, -1, -1)
            for c in order:
                rows = slice(c * C, (c + 1) * C)
                b = b_all[rows]
                b_last = b[last:last + 1, :]
                q = q_ref[rows, kc]
                k = k_ref[rows, kc]
                v = v_ref[rows, vc].astype(BF16)
                qe = (q * jnp.exp(b)).astype(BF16)
                ke = (k * jnp.exp(-b)).astype(BF16)
                kd = (k * jnp.exp(b_last - b)).astype(BF16)
                scores = lax.dot_general(qe, ke, nt_dims, preferred_element_type=F32)
                scores = jnp.where(keep, scores, 0.0).astype(BF16)
                o = _dot(scores, v) + lax.dot_general(qe, st.astype(BF16), nt_dims, preferred_element_type=F32)
                if direction == 0:
                    acc[rows, vc] = o
                else:
                    acc[rows, vc] = acc[rows, vc] + o
                st = jnp.exp(b_last) * st + lax.dot_general(v, kd, tn_dims, preferred_element_type=F32)
            sf_ref[direction, hh] = st.T

        o = acc[:, vc]
        o = o * lax.rsqrt(jnp.mean(o * o, axis=-1, keepdims=True) + EPS) * ng_ref[...]
        o_ref[:, vc] = (o * g_ref[:, vc]).astype(BF16)


def _gla_core(proj, tri, norm_g, s0, L, n_batch, row_off, hps):
    rb = row_off // L
    H = GLA_H
    nh = H // hps
    has_s0 = s0 is not None
    kern = functools.partial(_gla_core_kernel, L=L, has_s0=has_s0, hps=hps)
    kb, vb = GLA_DK * hps, GLA_DV * hps
    in_specs = [
        pl.BlockSpec((L, kb), lambda b, h: (rb + b, h)),
        pl.BlockSpec((L, kb), lambda b, h: (rb + b, nh + h)),
        pl.BlockSpec((L, vb), lambda b, h: (rb + b, (2 * H * GLA_DK) // vb + h)),
        pl.BlockSpec((L, vb), lambda b, h: (rb + b, (2 * H * GLA_DK) // vb + nh + h)),
        pl.BlockSpec((L, kb), lambda b, h: (rb + b, GLA_PROJ // kb + h)),
        pl.BlockSpec((L, kb), lambda b, h: (rb + b, GLA_PROJ // kb + nh + h)),
        _full_spec((2, L, L)),
        _vec_spec(GLA_DV),
    ]
    args = [proj] * 6 + [tri, norm_g.reshape(1, GLA_DV)]
    state_spec = pl.BlockSpec((None, 2, hps, GLA_DK, GLA_DV), lambda b, h: (b, 0, h, 0, 0))
    if has_s0:
        in_specs.append(state_spec)
        args.append(s0)
    return pl.pallas_call(
        kern,
        grid=(n_batch, nh),
        in_specs=in_specs,
        out_specs=[pl.BlockSpec((L, vb), lambda b, h: (b, h)), state_spec],
        out_shape=[
            jax.ShapeDtypeStruct((n_batch * L, H * GLA_DV), BF16),
            jax.ShapeDtypeStruct((n_batch, 2, H, GLA_DK, GLA_DV), F32),
        ],
        scratch_shapes=[pltpu.VMEM((L, vb), F32)],
        compiler_params=_cparams(2),
        name=f"gla_core_L{L}",
    )(*args)


def _fnet_mats(L):
    c = np.arange(FNET_C)
    ang_c = 2.0 * np.pi * np.outer(c, c) / FNET_C
    chan = np.concatenate([np.cos(ang_c), np.sin(ang_c)], axis=1) / math.sqrt(FNET_C)
    t = np.arange(L)
    ang_l = 2.0 * np.pi * np.outer(t, t) / L
    seq = np.concatenate([np.cos(ang_l), -np.sin(ang_l)], axis=1) / math.sqrt(L)
    return jnp.asarray(chan, F32).astype(BF16), jnp.asarray(seq, F32).astype(BF16)


def _fnet_kernel(x_ref, g_ref, sc_ref, sh_ref, gate_ref, chan_ref, seq_ref, w_ref, b_ref, *rest):
    *pending, o_ref = rest
    x = _read_x(x_ref, pending)
    h = _norm_mod(x, g_ref[...], sc_ref[...], sh_ref[...]).astype(BF16)
    chan = chan_ref[...]
    cos_parts, sin_parts = [], []
    for gi in range(FNET_GROUPS):
        cs = _dot(h[:, gi * FNET_C:(gi + 1) * FNET_C], chan)
        cos_parts.append(cs[:, :FNET_C])
        sin_parts.append(cs[:, FNET_C:])
    stacked = jnp.concatenate([jnp.concatenate(cos_parts, axis=1), jnp.concatenate(sin_parts, axis=1)], axis=0)
    mixed = _dot(seq_ref[...], stacked.astype(BF16))
    y = _dot(mixed.astype(BF16), w_ref[...]) + b_ref[...]
    o_ref[...] = x + gate_ref[...] * y


def _fnet(x, mod, layer, norm_g, chan, seq, w_bf16, bias, L, n_batch, row_off, pend=None):
    rb = row_off // L
    lat = row_off > 0

    def row_fn(b):
        return 1 + b if lat else 0

    p_specs, p_args = _pending_specs(pend, mod, L, lambda b: (rb + b, 0), row_fn)
    return pl.pallas_call(
        _fnet_kernel,
        grid=(n_batch,),
        in_specs=[
            pl.BlockSpec((L, D), lambda b: (rb + b, 0)),
            _vec_spec(D),
            _mod_spec(layer, 1, row_fn),
            _mod_spec(layer, 0, row_fn),
            _mod_spec(layer, 2, row_fn),
            _full_spec((FNET_C, 2 * FNET_C)),
            _full_spec((L, 2 * L)),
            _full_spec((D, D)),
            _vec_spec(D),
        ] + p_specs,
        out_specs=pl.BlockSpec((L, D), lambda b: (rb + b, 0)),
        out_shape=jax.ShapeDtypeStruct((N_TOK, D), F32),
        input_output_aliases={0: 0},
        compiler_params=_cparams(1),
        name=f"fnet_L{L}",
    )(x, norm_g.reshape(1, D), mod, mod, mod, chan, seq, w_bf16, bias.reshape(1, D), *p_args)


def _window_bounds(n, k):
    t = np.arange(n)
    lo, hi = k // 2, k - k // 2 - 1
    return np.maximum(t - lo, 0), np.minimum(t + hi + 1, n)


def _pool_mats(L, grid_rows):
    mats, inv = [], []
    for k in POOL_WINDOWS:
        if grid_rows is None:
            s, e = _window_bounds(L, k)
            idx = np.arange(L)[None, :]
            m = ((idx >= s[:, None]) & (idx < e[:, None])).astype(np.float64)
            cnt = (e - s).astype(np.float64)
        else:
            sr, er = _window_bounds(grid_rows, k)
            sc, ec = _window_bounds(GRID_W, k)
            ir = np.arange(grid_rows)[None, :]
            ic = np.arange(GRID_W)[None, :]
            mr = ((ir >= sr[:, None]) & (ir < er[:, None])).astype(np.float64)
            mc = ((ic >= sc[:, None]) & (ic < ec[:, None])).astype(np.float64)
            m = np.kron(mr, mc)
            cnt = np.kron((er - sr).astype(np.float64), (ec - sc).astype(np.float64))
        mats.append(m)
        inv.append(1.0 / cnt)
    return jnp.asarray(np.stack(mats), BF16), jnp.asarray(np.stack(inv)[:, :, None], F32)


def _pool_kernel(x_ref, g_ref, sc_ref, sh_ref, gate_ref, m_ref, ic_ref, w_ref, b_ref, ps_ref, *rest):
    *pending, o_ref = rest
    x = _read_x(x_ref, pending)
    h = _norm_mod(x, g_ref[...], sc_ref[...], sh_ref[...])
    outs = []
    for gi in range(len(POOL_WINDOWS)):
        hg = h[:, gi * POOL_G:(gi + 1) * POOL_G]
        hi, lo = _split(hg)
        m = m_ref[gi]
        mean = (_dot(m, hi) + _dot(m, lo)) * ic_ref[gi]
        outs.append(_dot((mean - hg).astype(BF16), w_ref[gi]))
    y = (jnp.concatenate(outs, axis=1) + b_ref[...]) * ps_ref[...]
    o_ref[...] = x + gate_ref[...] * y


def _pool(x, mod, layer, norm_g, mats, inv_cnt, w_bf16, bias, scale, L, n_batch, row_off, pend=None):
    rb = row_off // L
    lat = row_off > 0
    G = len(POOL_WINDOWS)

    def row_fn(b):
        return 1 + b if lat else 0

    p_specs, p_args = _pending_specs(pend, mod, L, lambda b: (rb + b, 0), row_fn)
    return pl.pallas_call(
        _pool_kernel,
        grid=(n_batch,),
        in_specs=[
            pl.BlockSpec((L, D), lambda b: (rb + b, 0)),
            _vec_spec(D),
            _mod_spec(layer, 1, row_fn),
            _mod_spec(layer, 0, row_fn),
            _mod_spec(layer, 2, row_fn),
            _full_spec((G, L, L)),
            _full_spec((G, L, 1)),
            _full_spec((G, POOL_G, POOL_G)),
            _vec_spec(D),
            _vec_spec(D),
        ] + p_specs,
        out_specs=pl.BlockSpec((L, D), lambda b: (rb + b, 0)),
        out_shape=jax.ShapeDtypeStruct((N_TOK, D), F32),
        input_output_aliases={0: 0},
        compiler_params=_cparams(1),
        name=f"pool_L{L}",
    )(x, norm_g.reshape(1, D), mod, mod, mod, mats, inv_cnt, w_bf16, bias.reshape(1, D), scale.reshape(1, D),
      *p_args)


ROUTER_PAD = 128
LANES = 128
D_EXT = D + LANES
MOE_TILE = 1024
MOE_MAX_TILES = N_TOK // MOE_TILE + MOE_GROUPS
MOE_ROWS = MOE_MAX_TILES * MOE_TILE
MOE_DMA_CHUNK = 64
MOE_Y_ROWS = N_TOK + 2 * MOE_DMA_CHUNK


ROUTE_ROWS = 8


def _moe_route_kernel(x_ref, g_ref, sc_ref, sh_ref, wr_ref, tri_ref, h3_ref, route_ref, cnt_ref, carry):
    t = pl.program_id(0)

    @pl.when(t == 0)
    def _():
        carry[...] = jnp.zeros_like(carry)

    h = _norm_mod(x_ref[...], g_ref[...], sc_ref[...], sh_ref[...])
    w_hi, w_lo = _split(wr_ref[...])
    h_hi, h_lo = _split(h)
    nt = (((1,), (1,)), ((), ()))
    logits = (lax.dot_general(w_hi, h_hi, nt, preferred_element_type=F32)
              + (lax.dot_general(w_hi, h_lo, nt, preferred_element_type=F32)
                 + lax.dot_general(w_lo, h_hi, nt, preferred_element_type=F32)))
    neg = jnp.float32(-jnp.inf)
    r8 = lax.broadcasted_iota(jnp.int32, (ROUTE_ROWS, TM), 0)
    r16 = lax.broadcasted_iota(jnp.int32, (MOE_E, TM), 0)
    gl = jnp.where(r8 < MOE_GROUPS, logits[MOE_E:MOE_E + ROUTE_ROWS], neg)
    g_max = jnp.max(gl, axis=0, keepdims=True)
    g_idx = jnp.min(jnp.where(gl == g_max, r8, ROUTE_ROWS), axis=0, keepdims=True)
    p_grp = 1.0 / jnp.sum(jnp.exp(gl - g_max), axis=0, keepdims=True)
    in_grp = (r16 >> 2) == g_idx
    el = jnp.where(in_grp, logits[:MOE_E], neg)
    m1 = jnp.max(el, axis=0, keepdims=True)
    i1 = jnp.min(jnp.where(el == m1, r16, MOE_E), axis=0, keepdims=True)
    z = jnp.sum(jnp.exp(el - m1), axis=0, keepdims=True)
    el2 = jnp.where(r16 == i1, neg, el)
    m2 = jnp.max(el2, axis=0, keepdims=True)
    i2 = jnp.min(jnp.where(el2 == m2, r16, MOE_E), axis=0, keepdims=True)
    p1 = 1.0 / z
    p2 = jnp.exp(m2 - m1) / z
    tot = p1 + p2
    eid = r8 + MOE_PER_GROUP * g_idx
    in4 = r8 < MOE_PER_GROUP
    cw4 = (jnp.where(in4 & (eid == i1), p_grp * (p1 / tot), 0.0)
           + jnp.where(in4 & (eid == i2), p_grp * (p2 / tot), 0.0))
    member = jnp.where(r8 == g_idx, 1.0, 0.0)
    before = _dot(member.astype(BF16), tri_ref[...]) + carry[:, 0:1]
    rank = jnp.sum(jnp.where(r8 == g_idx, before, 0.0), axis=0, keepdims=True)
    carry[...] = carry[...] + jnp.sum(member, axis=1, keepdims=True)
    cnt_ref[...] = carry[...].astype(jnp.int32)
    route_ref[...] = jnp.where(r8 == 0, g_idx, jnp.where(r8 == 1, rank.astype(jnp.int32), 0))
    h3_ref[:, :D] = h
    cw_rows = jnp.concatenate([cw4, jnp.zeros((LANES - ROUTE_ROWS, TM), F32)], axis=0)
    h3_ref[:, D:] = cw_rows.T


def _moe_route(x, mod, layer, norm_g, w_router_t, tri):
    return pl.pallas_call(
        _moe_route_kernel,
        grid=(N_TOK // TM,),
        in_specs=[
            pl.BlockSpec((TM, D), lambda t: (t, 0)),
            _vec_spec(D),
            _mod_spec(layer, 4, _row_tm),
            _mod_spec(layer, 3, _row_tm),
            _full_spec((ROUTER_PAD, D)),
            _full_spec((TM, TM)),
        ],
        out_specs=[
            pl.BlockSpec((TM, D_EXT), lambda t: (t, 0)),
            pl.BlockSpec((ROUTE_ROWS, TM), lambda t: (0, t)),
            pl.BlockSpec((ROUTE_ROWS, LANES), lambda t: (0, 0)),
        ],
        out_shape=[
            jax.ShapeDtypeStruct((N_TOK, D_EXT), F32),
            jax.ShapeDtypeStruct((ROUTE_ROWS, N_TOK), jnp.int32),
            jax.ShapeDtypeStruct((ROUTE_ROWS, LANES), jnp.int32),
        ],
        scratch_shapes=[pltpu.VMEM((ROUTE_ROWS, LANES), F32)],
        compiler_params=_cparams(1),
        name="moe_route",
    )(x, norm_g.reshape(1, D), mod, mod, w_router_t, tri)


def _moe_invert_kernel(pos_ref, src_ref):
    def mark(i, carry):
        src_ref[i] = jnp.int32(N_TOK)
        return carry

    lax.fori_loop(0, MOE_ROWS, mark, 0, unroll=16)

    def place(n, carry):
        src_ref[pos_ref[n]] = n
        return carry

    lax.fori_loop(0, N_TOK, place, 0, unroll=16)


def _moe_invert(pos):
    smem = pl.BlockSpec(memory_space=pltpu.SMEM)
    return pl.pallas_call(
        _moe_invert_kernel,
        in_specs=[smem],
        out_specs=smem,
        out_shape=jax.ShapeDtypeStruct((MOE_ROWS,), jnp.int32),
        name="moe_invert",
    )(pos)


def _moe_expert_kernel(src_ref, grp_ref, nact_ref, nchunk_ref, h_hbm, w1_ref, w3_ref, w2_ref, y_hbm,
                       xb0, xb1, ab0, ab1, gsem, ssem):
    t = pl.program_id(0)
    k = pl.program_id(1)
    n_active = nact_ref[0]
    last = n_active - 1
    T = MOE_TILE
    CH = MOE_DMA_CHUNK
    xbufs, accs = (xb0, xb1), (ab0, ab1)

    def gather_row(tile, slot, r):
        tok = jnp.minimum(src_ref[tile * T + r], N_TOK - 1)
        return pltpu.make_async_copy(h_hbm.at[pl.ds(tok, 1), :], xbufs[slot].at[pl.ds(r, 1), :], gsem.at[slot])

    def scatter_row(tile, slot, r):
        tok = src_ref[tile * T + r]
        dst = jnp.where(tok < N_TOK, tok, N_TOK + slot * CH + (r & (CH - 1)))
        return pltpu.make_async_copy(accs[slot].at[pl.ds(r, 1), :], y_hbm.at[pl.ds(dst, 1), :], ssem.at[slot])

    def start_rows(make, tile, slot):
        def chunk(c, carry):
            def row(r, carry2):
                make(tile, slot, c * CH + r).start()
                return carry2

            return lax.fori_loop(0, CH, row, carry, unroll=8)

        lax.fori_loop(0, nchunk_ref[tile], chunk, 0)

    def wait_rows(tile, slot, gather):
        def chunk(c, carry):
            if gather:
                pltpu.make_async_copy(h_hbm.at[pl.ds(0, CH), :], xbufs[slot].at[pl.ds(0, CH), :], gsem.at[slot]).wait()
            else:
                pltpu.make_async_copy(accs[slot].at[pl.ds(0, CH), :], y_hbm.at[pl.ds(0, CH), :], ssem.at[slot]).wait()
            return carry

        lax.fori_loop(0, nchunk_ref[tile], chunk, 0)

    def step(slot):
        other = 1 - slot
        xb, acc = xbufs[slot], accs[slot]

        @pl.when(k == 0)
        def _():
            if slot == 0:
                @pl.when(t == 0)
                def _():
                    xb0[...] = jnp.zeros_like(xb0)
                    xb1[...] = jnp.zeros_like(xb1)
                    ab0[...] = jnp.zeros_like(ab0)
                    ab1[...] = jnp.zeros_like(ab1)
                    dump = pltpu.make_async_copy(ab0.at[pl.ds(0, 2 * CH), :], y_hbm.at[pl.ds(N_TOK, 2 * CH), :],
                                                 ssem.at[0])
                    dump.start()
                    dump.wait()
                    start_rows(gather_row, 0, 0)

            wait_rows(t, slot, True)

            @pl.when(t < last)
            def _():
                start_rows(gather_row, t + 1, other)

            @pl.when(t >= 2)
            def _():
                wait_rows(t - 2, slot, False)

        x = xb[:, :D].astype(BF16)
        a = _dot(x, w1_ref[...].astype(BF16))
        b = _dot(x, w3_ref[...].astype(BF16))
        lane = lax.broadcasted_iota(jnp.int32, (T, LANES), 1)
        cwk = jnp.sum(jnp.where(lane == k, xb[:, D:], 0.0), axis=-1, keepdims=True)
        hid = (_silu(a) * b * cwk).astype(BF16)
        acc[...] = jnp.where(k > 0, acc[...], 0.0) + _dot(hid, w2_ref[...].astype(BF16))

        @pl.when(k == MOE_PER_GROUP - 1)
        def _():
            start_rows(scatter_row, t, slot)

            @pl.when(t == last)
            def _():
                wait_rows(t, slot, False)

                @pl.when(t >= 1)
                def _():
                    wait_rows(t - 1, other, False)

    for slot in (0, 1):
        pl.when((t < n_active) & (t % 2 == slot))(functools.partial(step, slot))


def _moe_experts(h_ext, src, tile_group, n_active, n_chunk, layer, w1, w3, w2):
    T = MOE_TILE

    def w_index(t, k, src_ref, grp_ref, nact_ref, nchunk_ref):
        last = nact_ref[0] - 1
        e = jnp.where(t <= last, grp_ref[t] * MOE_PER_GROUP + k, grp_ref[last] * MOE_PER_GROUP + MOE_PER_GROUP - 1)
        return (layer, e, 0, 0)

    grid_spec = pltpu.PrefetchScalarGridSpec(
        num_scalar_prefetch=4,
        grid=(MOE_MAX_TILES, MOE_PER_GROUP),
        in_specs=[
            pl.BlockSpec(memory_space=pl.ANY),
            pl.BlockSpec((None, None, D, MOE_HID), w_index),
            pl.BlockSpec((None, None, D, MOE_HID), w_index),
            pl.BlockSpec((None, None, MOE_HID, D), w_index),
        ],
        out_specs=pl.BlockSpec(memory_space=pl.ANY),
        scratch_shapes=[
            pltpu.VMEM((T, D_EXT), F32),
            pltpu.VMEM((T, D_EXT), F32),
            pltpu.VMEM((T, D), F32),
            pltpu.VMEM((T, D), F32),
            pltpu.SemaphoreType.DMA((2,)),
            pltpu.SemaphoreType.DMA((2,)),
        ],
    )
    return pl.pallas_call(
        _moe_expert_kernel,
        grid_spec=grid_spec,
        out_shape=jax.ShapeDtypeStruct((MOE_Y_ROWS, D), F32),
        compiler_params=_cparams(2),
        name="moe_experts",
    )(src, tile_group, n_active, n_chunk, h_ext, w1, w3, w2)


def _moe_combine_kernel(y_ref, gate_ref, x_ref, o_ref):
    o_ref[...] = x_ref[...] + gate_ref[...] * y_ref[...]


def _moe_combine(x, y3, mod, layer):
    return pl.pallas_call(
        _moe_combine_kernel,
        grid=(N_TOK // TM,),
        in_specs=[
            pl.BlockSpec((TM, D), lambda t: (t, 0)),
            _mod_spec(layer, 5, _row_tm),
            pl.BlockSpec((TM, D), lambda t: (t, 0)),
        ],
        out_specs=pl.BlockSpec((TM, D), lambda t: (t, 0)),
        out_shape=jax.ShapeDtypeStruct((N_TOK, D), F32),
        input_output_aliases={2: 0},
        compiler_params=_cparams(1),
        name="moe_combine",
    )(y3, mod, x)


def _moe(x, mod, layer, norm_g, w_rg, w_re, w1, w3, w2, tri, final_g=None):
    w_router_t = jnp.zeros((ROUTER_PAD, D), F32).at[:MOE_E].set(w_re.T).at[MOE_E:MOE_E + MOE_GROUPS].set(w_rg.T)
    h3, route, counts = _moe_route(x, mod, layer, norm_g, w_router_t, tri)
    cnt = counts[:MOE_GROUPS, 0]
    ntile = (cnt + MOE_TILE - 1) // MOE_TILE
    tile_end = jnp.cumsum(ntile)
    seg_start = (tile_end - ntile) * MOE_TILE
    g_idx, rank = route[0], route[1]
    pos = jnp.sum(jnp.where(g_idx[None, :] == jnp.arange(MOE_GROUPS)[:, None], seg_start[:, None], 0), axis=0) + rank
    tiles = jnp.arange(MOE_MAX_TILES, dtype=jnp.int32)
    tile_group = jnp.minimum(jnp.sum(tiles[:, None] >= tile_end[None, :], axis=1), MOE_GROUPS - 1).astype(jnp.int32)
    n_active = tile_end[-1:].astype(jnp.int32)
    src = _moe_invert(pos.astype(jnp.int32))
    first_tile = (tile_end - ntile)[tile_group]
    real_rows = jnp.clip(cnt[tile_group] - (tiles - first_tile) * MOE_TILE, 0, MOE_TILE)
    n_chunk = ((real_rows + MOE_DMA_CHUNK - 1) // MOE_DMA_CHUNK).astype(jnp.int32)
    y3 = _moe_experts(h3, src, tile_group, n_active, n_chunk, layer, w1, w3, w2)
    if final_g is None:
        return y3
    return tuple(_moe_combine_norm(x, y3, mod, layer, final_g, off, nb * L) for L, nb, off in
                 ((CTX_L, CTX_B, 0), (LAT_L, LAT_B, N_CTX)))


def _combine_norm_kernel(y_ref, gate_ref, x_ref, g_ref, o_ref):
    x = x_ref[...] + gate_ref[...] * y_ref[...]
    o_ref[...] = x * lax.rsqrt(jnp.mean(x * x, axis=-1, keepdims=True) + EPS) * g_ref[...]


def _moe_combine_norm(x, y3, mod, layer, final_g, row_off, n_rows):
    off = row_off // TM
    first_lat = N_CTX // TM

    def row_fn(t):
        g = t + off
        return jnp.where(g < first_lat, 0, 1 + (g - first_lat) // (LAT_L // TM))

    return pl.pallas_call(
        _combine_norm_kernel,
        grid=(n_rows // TM,),
        in_specs=[
            pl.BlockSpec((TM, D), lambda t: (t + off, 0)),
            _mod_spec(layer, 5, row_fn),
            pl.BlockSpec((TM, D), lambda t: (t + off, 0)),
            _vec_spec(D),
        ],
        out_specs=pl.BlockSpec((TM, D), lambda t: (t, 0)),
        out_shape=jax.ShapeDtypeStruct((n_rows, D), F32),
        compiler_params=_cparams(1),
        name="combine_final_norm",
    )(y3, mod, x, final_g.reshape(1, D))


def kernel(x_prompt, x_sample, state_gla, c, c_ctx, w_ada, b_ada, norm_g, hy_w_in, hy_b_in, hy_conv_w, hy_conv_b, hy_f_w1, hy_f_b1, hy_f_freq, hy_f_w2, hy_f_b2, hy_f_w3, hy_skip, hy_w_out, hy_b_out, gla_w_q, gla_w_k, gla_w_v, gla_w_g, gla_w_gk1, gla_w_gk2, gla_b_gk, gla_norm_g, gla_w_o, fn_w_out, fn_b_out, pool_w, pool_b, pool_scale, moe_w_rg, moe_w_re, moe_w1, moe_w3, moe_w2, final_g):
    groups = ((CTX_L, CTX_B, 0, None), (LAT_L, LAT_B, N_CTX, LAT_L // GRID_W))

    x = jnp.concatenate([x_prompt.reshape(N_CTX, D), x_sample.reshape(N_LAT, D)], axis=0)
    cond = jnp.zeros((MOD_ROWS, D), F32).at[0].set(c_ctx).at[1:1 + LAT_B].set(c)
    mod = _ada_table(cond, w_ada, b_ada).reshape(DEPTH * MOD_ROWS * 6, 1, D)

    tri_tm = jnp.asarray(np.triu(np.ones((TM, TM)), 1), BF16)

    new_states = []
    pend = None
    for i in range(DEPTH):
        kind, j = i % 4, i // 4
        if kind == 0:
            if pend is not None:
                x = _moe_combine(x, pend[0], mod, pend[1])
            u = _hyena_in(x, mod, i, norm_g[i, 0], hy_w_in[j].astype(BF16), hy_b_in[j], hy_conv_w[j], hy_conv_b[j])
            w_out = hy_w_out[j].astype(BF16)
            for L, nb, off, _ in groups:
                fwd, ff, inv = _dft_mats(L)
                khat = _hyena_filters(L, ff, hy_f_w1[j], hy_f_b1[j], hy_f_freq[j], hy_f_w2[j], hy_f_b2[j],
                                      hy_f_w3[j])
                z = _hyena_conv(u, khat, hy_skip[j], fwd, inv, L, nb, off, D if L == CTX_L else 512)
                x = _outproj(x, z, w_out, hy_b_out[j], mod, i, off, nb * L)
        elif kind == 1:
            w_cat = jnp.concatenate([gla_w_q[j], gla_w_k[j], gla_w_v[j], gla_w_g[j]], axis=1).astype(BF16)
            nk = GLA_H * GLA_DK
            wg1 = jnp.zeros((D, GK1_PAD), F32).at[:, :GLA_RANK].set(gla_w_gk1[j, 0])
            wg1 = wg1.at[:, GLA_RANK:2 * GLA_RANK].set(gla_w_gk1[j, 1]).astype(BF16)
            wg2 = jnp.zeros((GK1_PAD, 2 * nk), F32).at[:GLA_RANK, :nk].set(gla_w_gk2[j, 0])
            wg2 = wg2.at[GLA_RANK:2 * GLA_RANK, nk:].set(gla_w_gk2[j, 1]).astype(BF16)
            proj = _gla_proj(x, mod, i, norm_g[i, 0], w_cat, wg1, wg2, gla_b_gk[j].reshape(1, 2 * nk), pend)
            lower = np.tril(np.ones((GLA_CHUNK, GLA_CHUNK)))
            w_o = gla_w_o[j].astype(BF16)
            for L, nb, off, grid_rows in groups:
                eye = np.eye(L // GLA_CHUNK)
                tri = jnp.asarray(np.stack([np.kron(eye, lower), np.kron(eye, lower.T)]), BF16)
                s0 = None if grid_rows is None else state_gla[:, j]
                o, s_fin = _gla_core(proj, tri, gla_norm_g[j], s0, L, nb, off, GLA_H if L == CTX_L else 1)
                if grid_rows is None:
                    new_states.append(s_fin)
                x = _outproj(x, o, w_o, jnp.zeros((D,), F32), mod, i, off, nb * L, pend)
        elif kind == 2:
            w_out = fn_w_out[j].astype(BF16)
            for L, nb, off, _ in groups:
                chan, seq = _fnet_mats(L)
                x = _fnet(x, mod, i, norm_g[i, 0], chan, seq, w_out, fn_b_out[j], L, nb, off, pend)
        else:
            w_pool = pool_w[j].astype(BF16)
            for L, nb, off, grid_rows in groups:
                mats, inv_cnt = _pool_mats(L, grid_rows)
                x = _pool(x, mod, i, norm_g[i, 0], mats, inv_cnt, w_pool, pool_b[j], pool_scale[j], L, nb, off,
                          pend)

        out = _moe(x, mod, i, norm_g[i, 1], moe_w_rg[i], moe_w_re[i], moe_w1, moe_w3, moe_w2, tri_tm,
                   final_g if i == DEPTH - 1 else None)
        pend = (out, i)

    y_prompt, y_sample = out
    new_state_gla = jnp.stack(new_states, axis=1)
    return (y_prompt.reshape(CTX_B, CTX_L, D), y_sample.reshape(LAT_B, LAT_L, D), new_state_gla)
```

```python
import functools
import math

import jax
import jax.numpy as jnp
import numpy as np
from jax import lax
from jax.experimental import pallas as pl
from jax.experimental.pallas import tpu as pltpu

F32 = jnp.float32
BF16 = jnp.bfloat16

D = 1024
CTX_B, CTX_L = 32, 256
LAT_B, LAT_L = 2, 1024
N_CTX = CTX_B * CTX_L
N_LAT = LAT_B * LAT_L
N_TOK = N_CTX + N_LAT
DEPTH = 4
GRID_W = 64
EPS = 1e-6

HY_BANDS = 8
HY_EMB = 1 + 2 * HY_BANDS
HY_EMB_PAD = 32
HY_HID = 64
HY_FAST_DECAY = 0.3
HY_SLOW_DECAY = 1.5
HY_DECAY_TARGET = 1e-2

GLA_H = 4
GLA_DK = 128
GLA_DV = 256
GLA_RANK = 16
GLA_NORMALIZER = 16.0
GLA_CHUNK = 64

FNET_GROUPS = 4
FNET_C = D // FNET_GROUPS
POOL_WINDOWS = (2, 4, 8, 16)
POOL_G = D // len(POOL_WINDOWS)

MOE_GROUPS = 4
MOE_PER_GROUP = 4
MOE_E = MOE_GROUPS * MOE_PER_GROUP
MOE_HID = D // 2

MOD_ROWS = 8
TM = 512
TM_BIG = 1024
VMEM_LIMIT = 56 * 1024 * 1024


def _cparams(n_axes):
    return pltpu.CompilerParams(dimension_semantics=("arbitrary",) * n_axes, vmem_limit_bytes=VMEM_LIMIT)


def _norm_mod(x, g, sc, sh):
    ms = jnp.mean(x * x, axis=-1, keepdims=True)
    return (x * lax.rsqrt(ms + EPS) * g) * (1.0 + sc) + sh


def _split(a):
    hi = a.astype(BF16)
    lo = (a - hi.astype(F32)).astype(BF16)
    return hi, lo


def _dot(a, b):
    return jnp.dot(a, b, preferred_element_type=F32)


def _dot_precise(a, b):
    a_hi, a_lo = _split(a)
    b_hi, b_lo = _split(b)
    return _dot(a_hi, b_hi) + (_dot(a_hi, b_lo) + _dot(a_lo, b_hi))


def _silu(x):
    return x * (1.0 / (1.0 + jnp.exp(-x)))


def _log_sigmoid(x):
    return jnp.minimum(x, 0.0) - jnp.log(1.0 + jnp.exp(-jnp.abs(x)))


def _mod_spec(layer, chunk, row_fn):
    base = layer * MOD_ROWS * 6 + chunk

    def index_map(*ids):
        return (base + row_fn(*ids) * 6, 0, 0)

    return pl.BlockSpec((None, 1, D), index_map)


def _row_tm(t, *_):
    return jnp.where(t < N_CTX // TM, 0, 1 + (t - N_CTX // TM) // (LAT_L // TM))


def _row_big(t, *_):
    return jnp.where(t < N_CTX // TM_BIG, 0, 1 + (t - N_CTX // TM_BIG) // (LAT_L // TM_BIG))


def _vec_spec(n):
    return pl.BlockSpec((1, n), lambda *ids: (0, 0))


def _full_spec(shape):
    nd = len(shape)
    return pl.BlockSpec(shape, lambda *ids: (0,) * nd)


def _ada_kernel(cond_ref, w_ref, b_ref, o_ref):
    s = _silu(cond_ref[...]).astype(BF16)
    o_ref[...] = _dot(s, w_ref[...].astype(BF16)) + b_ref[...]


def _ada_table(cond, w_ada, b_ada):
    tn = 1536
    return pl.pallas_call(
        _ada_kernel,
        grid=(DEPTH, 6 * D // tn),
        in_specs=[
            pl.BlockSpec((MOD_ROWS, D), lambda i, j: (0, 0)),
            pl.BlockSpec((None, D, tn), lambda i, j: (i, 0, j)),
            pl.BlockSpec((None, 1, tn), lambda i, j: (i, 0, j)),
        ],
        out_specs=pl.BlockSpec((None, MOD_ROWS, tn), lambda i, j: (i, 0, j)),
        out_shape=jax.ShapeDtypeStruct((DEPTH, MOD_ROWS, 6 * D), F32),
        compiler_params=_cparams(2),
        name="ada_table",
    )(cond, w_ada, b_ada.reshape(DEPTH, 1, 6 * D))


def _pending_specs(pend, mod, block_rows, row_index, row_fn):
    if pend is None:
        return [], []
    y, prev_layer = pend
    return [pl.BlockSpec((block_rows, D), row_index), _mod_spec(prev_layer, 5, row_fn)], [y, mod]


def _read_x(x_ref, pending_refs):
    x = x_ref[...]
    if pending_refs:
        y_ref, gate_ref = pending_refs
        x = x + gate_ref[...] * y_ref[...]
    return x


def _outproj_kernel(z_ref, w_ref, b_ref, gate_ref, x_ref, *rest):
    *pending, o_ref = rest
    y = _dot(z_ref[...], w_ref[...]) + b_ref[...]
    o_ref[...] = _read_x(x_ref, pending) + gate_ref[...] * y


def _outproj(x, z, w_bf16, bias, mod, layer, row_off, n_rows, pend=None):
    k = z.shape[1]
    off = row_off // TM
    first_lat = N_CTX // TM

    def row_fn(t):
        g = t + off
        return jnp.where(g < first_lat, 0, 1 + (g - first_lat) // (LAT_L // TM))

    p_specs, p_args = _pending_specs(pend, mod, TM, lambda t: (t + off, 0), row_fn)
    return pl.pallas_call(
        _outproj_kernel,
        grid=(n_rows // TM,),
        in_specs=[
            pl.BlockSpec((TM, k), lambda t: (t, 0)),
            _full_spec((k, D)),
            _vec_spec(D),
            _mod_spec(layer, 2, row_fn),
            pl.BlockSpec((TM, D), lambda t: (t + off, 0)),
        ] + p_specs,
        out_specs=pl.BlockSpec((TM, D), lambda t: (t + off, 0)),
        out_shape=jax.ShapeDtypeStruct((N_TOK, D), F32),
        input_output_aliases={4: 0},
        compiler_params=_cparams(1),
        name="outproj_residual",
    )(z, w_bf16, bias.reshape(1, D), mod, x, *p_args)


def _dft_mats(L):
    n2 = 2 * L
    k = np.arange(L)[:, None].astype(np.float64)
    n = np.arange(n2)[None, :].astype(np.float64)
    ang = 2.0 * np.pi * k * n / n2
    full = np.concatenate([np.cos(ang), -np.sin(ang)], axis=0)
    full[L, :] = np.cos(np.pi * np.arange(n2))
    fwd = full[:, :L]
    bwd = np.zeros((n2, L))
    bwd[:, 1:] = full[:, n2 - np.arange(1, L)]
    t = np.arange(L)[:, None].astype(np.float64)
    kk = np.arange(L)[None, :].astype(np.float64)
    ang_i = 2.0 * np.pi * t * kk / n2
    inv_re = np.cos(ang_i) / L
    inv_re[:, 0] = 1.0 / n2
    inv_im = -np.sin(ang_i) / L
    inv_im[:, 0] = np.cos(np.pi * np.arange(L)) / n2
    inv = np.concatenate([inv_re, inv_im], axis=1)
    return tuple(jnp.asarray(m, F32).astype(BF16) for m in (fwd, np.concatenate([fwd, bwd], axis=1), inv))


def _hyena_pos_emb(L):
    pos = np.arange(L, dtype=np.float64)
    bands = np.linspace(1e-4, HY_BANDS - 1, HY_BANDS)
    ang = (2.0 * np.pi * pos / L)[:, None] * bands[None, :]
    z = np.concatenate([(pos / L)[:, None], np.cos(ang), -np.sin(ang)], axis=-1)
    zp = np.zeros((L, HY_EMB_PAD))
    zp[:, :HY_EMB] = z
    return jnp.asarray(zp, F32)


def _hyena_filter_kernel(z_ref, w1_ref, b1_ref, fr_ref, w2_ref, b2_ref, w3f_ref, w3b_ref, ff_ref, o_ref, *, L, tn):
    j = pl.program_id(1)
    fr = fr_ref[...]
    f = jnp.sin(fr * (_dot_precise(z_ref[...], w1_ref[...]) + b1_ref[...]))
    f = jnp.sin(fr * (_dot_precise(f, w2_ref[...]) + b2_ref[...]))
    t_lin = lax.broadcasted_iota(jnp.int32, (L, tn), 0).astype(F32) / float(L - 1)
    ch = (lax.broadcasted_iota(jnp.int32, (L, tn), 1) + j * tn).astype(F32)
    max_decay = math.log(HY_DECAY_TARGET) / HY_FAST_DECAY
    min_decay = math.log(HY_DECAY_TARGET) / HY_SLOW_DECAY
    deltas = min_decay + ch * ((max_decay - min_decay) / float(D - 1))
    window = jnp.exp(-t_lin * jnp.abs(deltas))
    kf = _dot_precise(f, w3f_ref[...]) * window
    kb = _dot_precise(f, w3b_ref[...]) * window
    taps = jnp.concatenate([kf, kb], axis=0).astype(BF16)
    o_ref[...] = _dot(ff_ref[...], taps)


def _hyena_filters(L, ff, f_w1, f_b1, f_freq, f_w2, f_b2, f_w3):
    tn = 512
    nj = D // tn
    w1p = jnp.zeros((HY_EMB_PAD, HY_HID), F32).at[:HY_EMB].set(f_w1)
    kern = functools.partial(_hyena_filter_kernel, L=L, tn=tn)
    return pl.pallas_call(
        kern,
        grid=(2, nj),
        in_specs=[
            _full_spec((L, HY_EMB_PAD)),
            _full_spec((HY_EMB_PAD, HY_HID)),
            _vec_spec(HY_HID),
            _vec_spec(HY_HID),
            _full_spec((HY_HID, HY_HID)),
            _vec_spec(HY_HID),
            pl.BlockSpec((HY_HID, tn), lambda o, j: (0, o * nj + j)),
            pl.BlockSpec((HY_HID, tn), lambda o, j: (0, (2 + o) * nj + j)),
            _full_spec((2 * L, 2 * L)),
        ],
        out_specs=pl.BlockSpec((None, 2 * L, tn), lambda o, j: (o, 0, j)),
        out_shape=jax.ShapeDtypeStruct((2, 2 * L, D), F32),
        compiler_params=_cparams(2),
        name=f"hyena_filters_L{L}",
    )(_hyena_pos_emb(L), w1p, f_b1.reshape(1, -1), f_freq.reshape(1, -1), f_w2, f_b2.reshape(1, -1),
      f_w3, f_w3, ff)


def _hyena_in_kernel(x_ref, g_ref, sc_ref, sh_ref, w_ref, b_ref, cw_ref, cb_ref, o_ref, h_scr):
    t = pl.program_id(0)

    @pl.when(pl.program_id(1) == 0)
    def _():
        h_scr[...] = _norm_mod(x_ref[...], g_ref[...], sc_ref[...], sh_ref[...]).astype(BF16)

    u = _dot(h_scr[...], w_ref[...]) + b_ref[...]
    seq = jnp.where(t < N_CTX // TM_BIG, CTX_L, LAT_L)
    pos = lax.broadcasted_iota(jnp.int32, u.shape, 0) & (seq - 1)
    prev = jnp.where(pos == 0, 0.0, pltpu.roll(u, 1, 0))
    nxt = jnp.where(pos == seq - 1, 0.0, pltpu.roll(u, TM_BIG - 1, 0))
    cw = cw_ref[...]
    o_ref[...] = prev * cw[0:1] + u * cw[1:2] + nxt * cw[2:3] + cb_ref[...]


def _hyena_in(x, mod, layer, norm_g, w_in_bf16, b_in, conv_w, conv_b):
    return pl.pallas_call(
        _hyena_in_kernel,
        grid=(N_TOK // TM_BIG, 3),
        in_specs=[
            pl.BlockSpec((TM_BIG, D), lambda t, p: (t, 0)),
            _vec_spec(D),
            _mod_spec(layer, 1, _row_big),
            _mod_spec(layer, 0, _row_big),
            pl.BlockSpec((D, D), lambda t, p: (0, p)),
            pl.BlockSpec((1, D), lambda t, p: (0, p)),
            pl.BlockSpec((3, D), lambda t, p: (0, p)),
            pl.BlockSpec((1, D), lambda t, p: (0, p)),
        ],
        out_specs=pl.BlockSpec((TM_BIG, D), lambda t, p: (t, p)),
        out_shape=jax.ShapeDtypeStruct((N_TOK, 3 * D), F32),
        scratch_shapes=[pltpu.VMEM((TM_BIG, D), BF16)],
        compiler_params=_cparams(2),
        name="hyena_in",
    )(x, norm_g.reshape(1, D), mod, mod, w_in_bf16, b_in.reshape(1, -1), conv_w, conv_b.reshape(1, -1))


def _hyena_conv_kernel(v_ref, x1_ref, x2_ref, kh_ref, skip_ref, fwd_ref, inv_ref, o_ref, *, L):
    fwd = fwd_ref[...]
    inv = inv_ref[...]
    row0 = lax.broadcasted_iota(jnp.int32, (L, v_ref.shape[1]), 0) == 0

    def long_conv(z, order):
        zh = _dot(fwd, z.astype(BF16))
        zr, zi = zh[:L], zh[L:]
        kr, ki = kh_ref[order, :L, :], kh_ref[order, L:, :]
        pr = jnp.where(row0, zr * kr, zr * kr - zi * ki)
        pi = jnp.where(row0, zi * ki, zr * ki + zi * kr)
        prod = jnp.concatenate([pr, pi], axis=0).astype(BF16)
        return _dot(inv, prod) + z * skip_ref[order:order + 1, :]

    z = x1_ref[...] * long_conv(v_ref[...], 0)
    z = x2_ref[...] * long_conv(z, 1)
    o_ref[...] = z.astype(BF16)


def _hyena_conv(u, khat, skip, fwd, inv, L, n_batch, row_off, tn):
    nj = D // tn
    rb = row_off // L
    kern = functools.partial(_hyena_conv_kernel, L=L)
    return pl.pallas_call(
        kern,
        grid=(nj, n_batch),
        in_specs=[
            pl.BlockSpec((L, tn), lambda j, b: (rb + b, j)),
            pl.BlockSpec((L, tn), lambda j, b: (rb + b, nj + j)),
            pl.BlockSpec((L, tn), lambda j, b: (rb + b, 2 * nj + j)),
            pl.BlockSpec((2, 2 * L, tn), lambda j, b: (0, 0, j)),
            pl.BlockSpec((2, tn), lambda j, b: (0, j)),
            _full_spec((2 * L, L)),
            _full_spec((L, 2 * L)),
        ],
        out_specs=pl.BlockSpec((L, tn), lambda j, b: (b, j)),
        out_shape=jax.ShapeDtypeStruct((n_batch * L, D), BF16),
        compiler_params=_cparams(2),
        name=f"hyena_conv_L{L}",
    )(u, u, u, khat, skip, fwd, inv)


GLA_PROJ = 2 * GLA_H * GLA_DK + 2 * GLA_H * GLA_DV
GLA_COLS = GLA_PROJ + 2 * GLA_H * GLA_DK
GK1_PAD = 128


def _gla_proj_kernel(x_ref, g_ref, sc_ref, sh_ref, w_ref, wg1_ref, wg2_ref, bg_ref, *rest):
    *pending, o_ref = rest
    h = _norm_mod(_read_x(x_ref, pending), g_ref[...], sc_ref[...], sh_ref[...]).astype(BF16)
    p = _dot(h, w_ref[...])
    nq = GLA_H * GLA_DK
    o_ref[:, 0:nq] = p[:, 0:nq] * (GLA_DK ** -0.5)
    o_ref[:, nq:nq + nq + GLA_H * GLA_DV] = p[:, nq:nq + nq + GLA_H * GLA_DV]
    o_ref[:, 2 * nq + GLA_H * GLA_DV:GLA_PROJ] = _silu(p[:, 2 * nq + GLA_H * GLA_DV:GLA_PROJ])
    low = _dot(h, wg1_ref[...]).astype(BF16)
    gk = _dot(low, wg2_ref[...]) + bg_ref[...]
    o_ref[:, GLA_PROJ:GLA_COLS] = _log_sigmoid(gk) / GLA_NORMALIZER


def _gla_proj(x, mod, layer, norm_g, w_cat, wg1, wg2, bg, pend=None):
    p_specs, p_args = _pending_specs(pend, mod, TM, lambda t: (t, 0), _row_tm)
    return pl.pallas_call(
        _gla_proj_kernel,
        grid=(N_TOK // TM,),
        in_specs=[
            pl.BlockSpec((TM, D), lambda t: (t, 0)),
            _vec_spec(D),
            _mod_spec(layer, 1, _row_tm),
            _mod_spec(layer, 0, _row_tm),
            _full_spec((D, GLA_PROJ)),
            _full_spec((D, GK1_PAD)),
            _full_spec((GK1_PAD, 2 * GLA_H * GLA_DK)),
            _vec_spec(2 * GLA_H * GLA_DK),
        ] + p_specs,
        out_specs=pl.BlockSpec((TM, GLA_COLS), lambda t: (t, 0)),
        out_shape=jax.ShapeDtypeStruct((N_TOK, GLA_COLS), F32),
        compiler_params=_cparams(1),
        name="gla_proj",
    )(x, norm_g.reshape(1, D), mod, mod, w_cat, wg1, wg2, bg, *p_args)


def _gla_core_kernel(*refs, L, has_s0, hps):
    if has_s0:
        q_ref, k_ref, v_ref, g_ref, gkf_ref, gkb_ref, tri_ref, ng_ref, s0_ref, o_ref, sf_ref, acc = refs
    else:
        q_ref, k_ref, v_ref, g_ref, gkf_ref, gkb_ref, tri_ref, ng_ref, o_ref, sf_ref, acc = refs
        s0_ref = None
    C = GLA_CHUNK
    n = L // C
    ri = lax.broadcasted_iota(jnp.int32, (C, C), 0)
    ci = lax.broadcasted_iota(jnp.int32, (C, C), 1)
    nt_dims = (((1,), (1,)), ((), ()))
    tn_dims = (((0,), (0,)), ((), ()))

    for hh in range(hps):
        kc = slice(hh * GLA_DK, (hh + 1) * GLA_DK)
        vc = slice(hh * GLA_DV, (hh + 1) * GLA_DV)
        for direction, gk_ref in enumerate((gkf_ref, gkb_ref)):
            keep = (ci <= ri) if direction == 0 else (ci >= ri)
            last = C - 1 if direction == 0 else 0
            gk_hi, gk_lo = _split(gk_ref[:, kc])
            b_all = _dot(tri_ref[direction], jnp.concatenate([gk_hi, gk_lo], axis=1))
            b_all = b_all[:, :GLA_DK] + b_all[:, GLA_DK:]
            st = s0_ref[direction, hh].T if has_s0 else jnp.zeros((GLA_DV, GLA_DK), F32)
            order = range(n) if direction == 0 else range(n - 1, -1, -1)
            for c in order:
                rows = slice(c * C, (c + 1) * C)
                b = b_all[rows]
                b_last = b[last:last + 1, :]
                q = q_ref[rows, kc]
                k = k_ref[rows, kc]
                v = v_ref[rows, vc].astype(BF16)
                qe = (q * jnp.exp(b)).astype(BF16)
                ke = (k * jnp.exp(-b)).astype(BF16)
                kd = (k * jnp.exp(b_last - b)).astype(BF16)
                scores = lax.dot_general(qe, ke, nt_dims, preferred_element_type=F32)
                scores = jnp.where(keep, scores, 0.0).astype(BF16)
                o = _dot(scores, v) + lax.dot_general(qe, st.astype(BF16), nt_dims, preferred_element_type=F32)
                if direction == 0:
                    acc[rows, vc] = o
                else:
                    acc[rows, vc] = acc[rows, vc] + o
                st = jnp.exp(b_last) * st + lax.dot_general(v, kd, tn_dims, preferred_element_type=F32)
            sf_ref[direction, hh] = st.T

        o = acc[:, vc]
        o = o * lax.rsqrt(jnp.mean(o * o, axis=-1, keepdims=True) + EPS) * ng_ref[...]
        o_ref[:, vc] = (o * g_ref[:, vc]).astype(BF16)


def _gla_core(proj, tri, norm_g, s0, L, n_batch, row_off, hps):
    rb = row_off // L
    H = GLA_H
    nh = H // hps
    has_s0 = s0 is not None
    kern = functools.partial(_gla_core_kernel, L=L, has_s0=has_s0, hps=hps)
    kb, vb = GLA_DK * hps, GLA_DV * hps
    in_specs = [
        pl.BlockSpec((L, kb), lambda b, h: (rb + b, h)),
        pl.BlockSpec((L, kb), lambda b, h: (rb + b, nh + h)),
        pl.BlockSpec((L, vb), lambda b, h: (rb + b, (2 * H * GLA_DK) // vb + h)),
        pl.BlockSpec((L, vb), lambda b, h: (rb + b, (2 * H * GLA_DK) // vb + nh + h)),
        pl.BlockSpec((L, kb), lambda b, h: (rb + b, GLA_PROJ // kb + h)),
        pl.BlockSpec((L, kb), lambda b, h: (rb + b, GLA_PROJ // kb + nh + h)),
        _full_spec((2, L, L)),
        _vec_spec(GLA_DV),
    ]
    args = [proj] * 6 + [tri, norm_g.reshape(1, GLA_DV)]
    state_spec = pl.BlockSpec((None, 2, hps, GLA_DK, GLA_DV), lambda b, h: (b, 0, h, 0, 0))
    if has_s0:
        in_specs.append(state_spec)
        args.append(s0)
    return pl.pallas_call(
        kern,
        grid=(n_batch, nh),
        in_specs=in_specs,
        out_specs=[pl.BlockSpec((L, vb), lambda b, h: (b, h)), state_spec],
        out_shape=[
            jax.ShapeDtypeStruct((n_batch * L, H * GLA_DV), BF16),
            jax.ShapeDtypeStruct((n_batch, 2, H, GLA_DK, GLA_DV), F32),
        ],
        scratch_shapes=[pltpu.VMEM((L, vb), F32)],
        compiler_params=_cparams(2),
        name=f"gla_core_L{L}",
    )(*args)


def _fnet_mats(L):
    c = np.arange(FNET_C)
    ang_c = 2.0 * np.pi * np.outer(c, c) / FNET_C
    chan = np.concatenate([np.cos(ang_c), np.sin(ang_c)], axis=1) / math.sqrt(FNET_C)
    t = np.arange(L)
    ang_l = 2.0 * np.pi * np.outer(t, t) / L
    seq = np.concatenate([np.cos(ang_l), -np.sin(ang_l)], axis=1) / math.sqrt(L)
    return jnp.asarray(chan, F32).astype(BF16), jnp.asarray(seq, F32).astype(BF16)


def _fnet_kernel(x_ref, g_ref, sc_ref, sh_ref, gate_ref, chan_ref, seq_ref, w_ref, b_ref, *rest):
    *pending, o_ref = rest
    x = _read_x(x_ref, pending)
    h = _norm_mod(x, g_ref[...], sc_ref[...], sh_ref[...]).astype(BF16)
    chan = chan_ref[...]
    cos_parts, sin_parts = [], []
    for gi in range(FNET_GROUPS):
        cs = _dot(h[:, gi * FNET_C:(gi + 1) * FNET_C], chan)
        cos_parts.append(cs[:, :FNET_C])
        sin_parts.append(cs[:, FNET_C:])
    stacked = jnp.concatenate([jnp.concatenate(cos_parts, axis=1), jnp.concatenate(sin_parts, axis=1)], axis=0)
    mixed = _dot(seq_ref[...], stacked.astype(BF16))
    y = _dot(mixed.astype(BF16), w_ref[...]) + b_ref[...]
    o_ref[...] = x + gate_ref[...] * y


def _fnet(x, mod, layer, norm_g, chan, seq, w_bf16, bias, L, n_batch, row_off, pend=None):
    rb = row_off // L
    lat = row_off > 0

    def row_fn(b):
        return 1 + b if lat else 0

    p_specs, p_args = _pending_specs(pend, mod, L, lambda b: (rb + b, 0), row_fn)
    return pl.pallas_call(
        _fnet_kernel,
        grid=(n_batch,),
        in_specs=[
            pl.BlockSpec((L, D), lambda b: (rb + b, 0)),
            _vec_spec(D),
            _mod_spec(layer, 1, row_fn),
            _mod_spec(layer, 0, row_fn),
            _mod_spec(layer, 2, row_fn),
            _full_spec((FNET_C, 2 * FNET_C)),
            _full_spec((L, 2 * L)),
            _full_spec((D, D)),
            _vec_spec(D),
        ] + p_specs,
        out_specs=pl.BlockSpec((L, D), lambda b: (rb + b, 0)),
        out_shape=jax.ShapeDtypeStruct((N_TOK, D), F32),
        input_output_aliases={0: 0},
        compiler_params=_cparams(1),
        name=f"fnet_L{L}",
    )(x, norm_g.reshape(1, D), mod, mod, mod, chan, seq, w_bf16, bias.reshape(1, D), *p_args)


def _window_bounds(n, k):
    t = np.arange(n)
    lo, hi = k // 2, k - k // 2 - 1
    return np.maximum(t - lo, 0), np.minimum(t + hi + 1, n)


def _pool_mats(L, grid_rows):
    mats, inv = [], []
    for k in POOL_WINDOWS:
        if grid_rows is None:
            s, e = _window_bounds(L, k)
            idx = np.arange(L)[None, :]
            m = ((idx >= s[:, None]) & (idx < e[:, None])).astype(np.float64)
            cnt = (e - s).astype(np.float64)
        else:
            sr, er = _window_bounds(grid_rows, k)
            sc, ec = _window_bounds(GRID_W, k)
            ir = np.arange(grid_rows)[None, :]
            ic = np.arange(GRID_W)[None, :]
            mr = ((ir >= sr[:, None]) & (ir < er[:, None])).astype(np.float64)
            mc = ((ic >= sc[:, None]) & (ic < ec[:, None])).astype(np.float64)
            m = np.kron(mr, mc)
            cnt = np.kron((er - sr).astype(np.float64), (ec - sc).astype(np.float64))
        mats.append(m)
        inv.append(1.0 / cnt)
    return jnp.asarray(np.stack(mats), BF16), jnp.asarray(np.stack(inv)[:, :, None], F32)


def _pool_kernel(x_ref, g_ref, sc_ref, sh_ref, gate_ref, m_ref, ic_ref, w_ref, b_ref, ps_ref, *rest):
    *pending, o_ref = rest
    x = _read_x(x_ref, pending)
    h = _norm_mod(x, g_ref[...], sc_ref[...], sh_ref[...])
    outs = []
    for gi in range(len(POOL_WINDOWS)):
        hg = h[:, gi * POOL_G:(gi + 1) * POOL_G]
        hi, lo = _split(hg)
        m = m_ref[gi]
        mean = (_dot(m, hi) + _dot(m, lo)) * ic_ref[gi]
        outs.append(_dot((mean - hg).astype(BF16), w_ref[gi]))
    y = (jnp.concatenate(outs, axis=1) + b_ref[...]) * ps_ref[...]
    o_ref[...] = x + gate_ref[...] * y


def _pool(x, mod, layer, norm_g, mats, inv_cnt, w_bf16, bias, scale, L, n_batch, row_off, pend=None):
    rb = row_off // L
    lat = row_off > 0
    G = len(POOL_WINDOWS)

    def row_fn(b):
        return 1 + b if lat else 0

    p_specs, p_args = _pending_specs(pend, mod, L, lambda b: (rb + b, 0), row_fn)
    return pl.pallas_call(
        _pool_kernel,
        grid=(n_batch,),
        in_specs=[
            pl.BlockSpec((L, D), lambda b: (rb + b, 0)),
            _vec_spec(D),
            _mod_spec(layer, 1, row_fn),
            _mod_spec(layer, 0, row_fn),
            _mod_spec(layer, 2, row_fn),
            _full_spec((G, L, L)),
            _full_spec((G, L, 1)),
            _full_spec((G, POOL_G, POOL_G)),
            _vec_spec(D),
            _vec_spec(D),
        ] + p_specs,
        out_specs=pl.BlockSpec((L, D), lambda b: (rb + b, 0)),
        out_shape=jax.ShapeDtypeStruct((N_TOK, D), F32),
        input_output_aliases={0: 0},
        compiler_params=_cparams(1),
        name=f"pool_L{L}",
    )(x, norm_g.reshape(1, D), mod, mod, mod, mats, inv_cnt, w_bf16, bias.reshape(1, D), scale.reshape(1, D),
      *p_args)


ROUTER_PAD = 128
LANES = 128
D_EXT = D + LANES
MOE_TILE = 1024
MOE_MAX_TILES = N_TOK // MOE_TILE + MOE_GROUPS
MOE_ROWS = MOE_MAX_TILES * MOE_TILE
MOE_TILE_SHIFT = MOE_TILE.bit_length() - 1
assert 1 << MOE_TILE_SHIFT == MOE_TILE
MOE_DMA_CHUNK = 64
MOE_Y_ROWS = N_TOK + 2 * MOE_DMA_CHUNK


ROUTE_ROWS = 8


def _moe_route_kernel(x_ref, g_ref, sc_ref, sh_ref, wr_ref, tri_ref, h3_ref, route_ref, cnt_ref, carry):
    t = pl.program_id(0)
    refs = (x_ref, g_ref, sc_ref, sh_ref, wr_ref, tri_ref, h3_ref, route_ref, cnt_ref, carry)
    pl.when(t < N_TOK // TM)(functools.partial(_moe_route_tile, t, *refs))

    @pl.when(t == N_TOK // TM)
    def _():
        h3_ref[...] = jnp.zeros_like(h3_ref)


def _moe_route_tile(t, x_ref, g_ref, sc_ref, sh_ref, wr_ref, tri_ref, h3_ref, route_ref, cnt_ref, carry):
    @pl.when(t == 0)
    def _():
        carry[...] = jnp.zeros_like(carry)

    h = _norm_mod(x_ref[...], g_ref[...], sc_ref[...], sh_ref[...])
    w_hi, w_lo = _split(wr_ref[...])
    h_hi, h_lo = _split(h)
    nt = (((1,), (1,)), ((), ()))
    logits = (lax.dot_general(w_hi, h_hi, nt, preferred_element_type=F32)
              + (lax.dot_general(w_hi, h_lo, nt, preferred_element_type=F32)
                 + lax.dot_general(w_lo, h_hi, nt, preferred_element_type=F32)))
    neg = jnp.float32(-jnp.inf)
    r8 = lax.broadcasted_iota(jnp.int32, (ROUTE_ROWS, TM), 0)
    r16 = lax.broadcasted_iota(jnp.int32, (MOE_E, TM), 0)
    gl = jnp.where(r8 < MOE_GROUPS, logits[MOE_E:MOE_E + ROUTE_ROWS], neg)
    g_max = jnp.max(gl, axis=0, keepdims=True)
    g_idx = jnp.min(jnp.where(gl == g_max, r8, ROUTE_ROWS), axis=0, keepdims=True)
    p_grp = 1.0 / jnp.sum(jnp.exp(gl - g_max), axis=0, keepdims=True)
    in_grp = (r16 >> 2) == g_idx
    el = jnp.where(in_grp, logits[:MOE_E], neg)
    m1 = jnp.max(el, axis=0, keepdims=True)
    i1 = jnp.min(jnp.where(el == m1, r16, MOE_E), axis=0, keepdims=True)
    z = jnp.sum(jnp.exp(el - m1), axis=0, keepdims=True)
    el2 = jnp.where(r16 == i1, neg, el)
    m2 = jnp.max(el2, axis=0, keepdims=True)
    i2 = jnp.min(jnp.where(el2 == m2, r16, MOE_E), axis=0, keepdims=True)
    p1 = 1.0 / z
    p2 = jnp.exp(m2 - m1) / z
    tot = p1 + p2
    eid = r8 + MOE_PER_GROUP * g_idx
    in4 = r8 < MOE_PER_GROUP
    cw4 = (jnp.where(in4 & (eid == i1), p_grp * (p1 / tot), 0.0)
           + jnp.where(in4 & (eid == i2), p_grp * (p2 / tot), 0.0))
    member = jnp.where(r8 == g_idx, 1.0, 0.0)
    before = _dot(member.astype(BF16), tri_ref[...]) + carry[:, 0:1]
    rank = jnp.sum(jnp.where(r8 == g_idx, before, 0.0), axis=0, keepdims=True)
    carry[...] = carry[...] + jnp.sum(member, axis=1, keepdims=True)
    cnt_ref[...] = carry[...].astype(jnp.int32)
    route_ref[...] = jnp.where(r8 == 0, g_idx, jnp.where(r8 == 1, rank.astype(jnp.int32), 0))
    h3_ref[:, :D] = h
    cw_rows = jnp.concatenate([cw4, jnp.zeros((LANES - ROUTE_ROWS, TM), F32)], axis=0)
    h3_ref[:, D:] = cw_rows.T


def _moe_route(x, mod, layer, norm_g, w_router_t, tri):
    nt = N_TOK // TM

    def tok_tile(t):
        return jnp.minimum(t, nt - 1)

    return pl.pallas_call(
        _moe_route_kernel,
        grid=(nt + 1,),
        in_specs=[
            pl.BlockSpec((TM, D), lambda t: (tok_tile(t), 0)),
            _vec_spec(D),
            _mod_spec(layer, 4, lambda t: _row_tm(tok_tile(t))),
            _mod_spec(layer, 3, lambda t: _row_tm(tok_tile(t))),
            _full_spec((ROUTER_PAD, D)),
            _full_spec((TM, TM)),
        ],
        out_specs=[
            pl.BlockSpec((TM, D_EXT), lambda t: (t, 0)),
            pl.BlockSpec((ROUTE_ROWS, TM), lambda t: (0, tok_tile(t))),
            pl.BlockSpec((ROUTE_ROWS, LANES), lambda t: (0, 0)),
        ],
        out_shape=[
            jax.ShapeDtypeStruct((N_TOK + TM, D_EXT), F32),
            jax.ShapeDtypeStruct((ROUTE_ROWS, N_TOK), jnp.int32),
            jax.ShapeDtypeStruct((ROUTE_ROWS, LANES), jnp.int32),
        ],
        scratch_shapes=[pltpu.VMEM((ROUTE_ROWS, LANES), F32)],
        compiler_params=_cparams(1),
        name="moe_route",
    )(x, norm_g.reshape(1, D), mod, mod, w_router_t, tri)


def _moe_invert_kernel(pos_ref, src_ref):
    def mark(i, carry):
        parity = lax.shift_right_logical(i, jnp.int32(MOE_TILE_SHIFT)) & 1
        src_ref[i] = N_TOK + (i & (MOE_DMA_CHUNK - 1)) + MOE_DMA_CHUNK * parity
        return carry

    lax.fori_loop(0, MOE_ROWS, mark, 0, unroll=16)

    def place(n, carry):
        src_ref[pos_ref[n]] = n
        return carry

    lax.fori_loop(0, N_TOK, place, 0, unroll=16)


def _moe_invert(pos):
    smem = pl.BlockSpec(memory_space=pltpu.SMEM)
    return pl.pallas_call(
        _moe_invert_kernel,
        in_specs=[smem],
        out_specs=smem,
        out_shape=jax.ShapeDtypeStruct((MOE_ROWS,), jnp.int32),
        name="moe_invert",
    )(pos)


def _moe_expert_kernel(src_ref, grp_ref, nact_ref, nchunk_ref, h_hbm, w1_ref, w3_ref, w2_ref, y_hbm,
                       xb0, xb1, ab0, ab1, gsem, ssem):
    t = pl.program_id(0)
    k = pl.program_id(1)
    n_active = nact_ref[0]
    last = n_active - 1
    T = MOE_TILE
    CH = MOE_DMA_CHUNK
    xbufs, accs = (xb0, xb1), (ab0, ab1)

    def gather_row(tile, slot, base, rr):
        tok = src_ref[tile * T + base + rr]
        rows = xbufs[slot].at[pl.ds(base, CH), :]
        return pltpu.make_async_copy(h_hbm.at[pl.ds(tok, 1), :], rows.at[pl.ds(rr, 1), :], gsem.at[slot])

    def scatter_row(tile, slot, base, rr):
        dst = src_ref[tile * T + base + rr]
        rows = accs[slot].at[pl.ds(base, CH), :]
        return pltpu.make_async_copy(rows.at[pl.ds(rr, 1), :], y_hbm.at[pl.ds(dst, 1), :], ssem.at[slot])

    def start_rows(make, tile, slot):
        def chunk(c, carry):
            base = pl.multiple_of(c * CH, CH)
            for rr in range(CH):
                make(tile, slot, base, rr).start()
            return carry

        lax.fori_loop(0, nchunk_ref[tile], chunk, 0)

    def wait_rows(tile, slot, gather):
        def chunk(c, carry):
            if gather:
                pltpu.make_async_copy(h_hbm.at[pl.ds(0, CH), :], xbufs[slot].at[pl.ds(0, CH), :], gsem.at[slot]).wait()
            else:
                pltpu.make_async_copy(accs[slot].at[pl.ds(0, CH), :], y_hbm.at[pl.ds(0, CH), :], ssem.at[slot]).wait()
            return carry

        lax.fori_loop(0, nchunk_ref[tile], chunk, 0)

    def step(slot):
        other = 1 - slot
        xb, acc = xbufs[slot], accs[slot]

        @pl.when(k == 0)
        def _():
            if slot == 0:
                @pl.when(t == 0)
                def _():
                    xb0[...] = jnp.zeros_like(xb0)
                    xb1[...] = jnp.zeros_like(xb1)
                    ab0[...] = jnp.zeros_like(ab0)
                    ab1[...] = jnp.zeros_like(ab1)
                    dump = pltpu.make_async_copy(ab0.at[pl.ds(0, 2 * CH), :], y_hbm.at[pl.ds(N_TOK, 2 * CH), :],
                                                 ssem.at[0])
                    dump.start()
                    dump.wait()
                    start_rows(gather_row, 0, 0)

            wait_rows(t, slot, True)

            @pl.when(t < last)
            def _():
                start_rows(gather_row, t + 1, other)

            @pl.when(t >= 2)
            def _():
                wait_rows(t - 2, slot, False)

        x = xb[:, :D].astype(BF16)
        a = _dot(x, w1_ref[...].astype(BF16))
        b = _dot(x, w3_ref[...].astype(BF16))
        lane = lax.broadcasted_iota(jnp.int32, (T, LANES), 1)
        cwk = jnp.sum(jnp.where(lane == k, xb[:, D:], 0.0), axis=-1, keepdims=True)
        hid = (_silu(a) * b * cwk).astype(BF16)
        acc[...] = jnp.where(k > 0, acc[...], 0.0) + _dot(hid, w2_ref[...].astype(BF16))

        @pl.when(k == MOE_PER_GROUP - 1)
        def _():
            start_rows(scatter_row, t, slot)

            @pl.when(t == last)
            def _():
                wait_rows(t, slot, False)

                @pl.when(t >= 1)
                def _():
                    wait_rows(t - 1, other, False)

    for slot in (0, 1):
        pl.when((t < n_active) & (t % 2 == slot))(functools.partial(step, slot))


def _moe_experts(h_ext, src, tile_group, n_active, n_chunk, layer, w1, w3, w2):
    T = MOE_TILE

    def w_index(t, k, src_ref, grp_ref, nact_ref, nchunk_ref):
        last = nact_ref[0] - 1
        e = jnp.where(t <= last, grp_ref[t] * MOE_PER_GROUP + k, grp_ref[last] * MOE_PER_GROUP + MOE_PER_GROUP - 1)
        return (layer, e, 0, 0)

    grid_spec = pltpu.PrefetchScalarGridSpec(
        num_scalar_prefetch=4,
        grid=(MOE_MAX_TILES, MOE_PER_GROUP),
        in_specs=[
            pl.BlockSpec(memory_space=pl.ANY),
            pl.BlockSpec((None, None, D, MOE_HID), w_index),
            pl.BlockSpec((None, None, D, MOE_HID), w_index),
            pl.BlockSpec((None, None, MOE_HID, D), w_index),
        ],
        out_specs=pl.BlockSpec(memory_space=pl.ANY),
        scratch_shapes=[
            pltpu.VMEM((T, D_EXT), F32),
            pltpu.VMEM((T, D_EXT), F32),
            pltpu.VMEM((T, D), F32),
            pltpu.VMEM((T, D), F32),
            pltpu.SemaphoreType.DMA((2,)),
            pltpu.SemaphoreType.DMA((2,)),
        ],
    )
    return pl.pallas_call(
        _moe_expert_kernel,
        grid_spec=grid_spec,
        out_shape=jax.ShapeDtypeStruct((MOE_Y_ROWS, D), F32),
        compiler_params=_cparams(2),
        name="moe_experts",
    )(src, tile_group, n_active, n_chunk, h_ext, w1, w3, w2)


def _moe_combine_kernel(y_ref, gate_ref, x_ref, o_ref):
    o_ref[...] = x_ref[...] + gate_ref[...] * y_ref[...]


def _moe_combine(x, y3, mod, layer):
    return pl.pallas_call(
        _moe_combine_kernel,
        grid=(N_TOK // TM,),
        in_specs=[
            pl.BlockSpec((TM, D), lambda t: (t, 0)),
            _mod_spec(layer, 5, _row_tm),
            pl.BlockSpec((TM, D), lambda t: (t, 0)),
        ],
        out_specs=pl.BlockSpec((TM, D), lambda t: (t, 0)),
        out_shape=jax.ShapeDtypeStruct((N_TOK, D), F32),
        input_output_aliases={2: 0},
        compiler_params=_cparams(1),
        name="moe_combine",
    )(y3, mod, x)


def _moe(x, mod, layer, norm_g, w_rg, w_re, w1, w3, w2, tri, final_g=None):
    w_router_t = jnp.zeros((ROUTER_PAD, D), F32).at[:MOE_E].set(w_re.T).at[MOE_E:MOE_E + MOE_GROUPS].set(w_rg.T)
    h3, route, counts = _moe_route(x, mod, layer, norm_g, w_router_t, tri)
    cnt = counts[:MOE_GROUPS, 0]
    ntile = (cnt + MOE_TILE - 1) // MOE_TILE
    tile_end = jnp.cumsum(ntile)
    seg_start = (tile_end - ntile) * MOE_TILE
    g_idx, rank = route[0], route[1]
    pos = jnp.sum(jnp.where(g_idx[None, :] == jnp.arange(MOE_GROUPS)[:, None], seg_start[:, None], 0), axis=0) + rank
    tiles = jnp.arange(MOE_MAX_TILES, dtype=jnp.int32)
    tile_group = jnp.minimum(jnp.sum(tiles[:, None] >= tile_end[None, :], axis=1), MOE_GROUPS - 1).astype(jnp.int32)
    n_active = tile_end[-1:].astype(jnp.int32)
    src = _moe_invert(pos.astype(jnp.int32))
    first_tile = (tile_end - ntile)[tile_group]
    real_rows = jnp.clip(cnt[tile_group] - (tiles - first_tile) * MOE_TILE, 0, MOE_TILE)
    n_chunk = ((real_rows + MOE_DMA_CHUNK - 1) // MOE_DMA_CHUNK).astype(jnp.int32)
    y3 = _moe_experts(h3, src, tile_group, n_active, n_chunk, layer, w1, w3, w2)
    if final_g is None:
        return y3
    return tuple(_moe_combine_norm(x, y3, mod, layer, final_g, off, nb * L) for L, nb, off in
                 ((CTX_L, CTX_B, 0), (LAT_L, LAT_B, N_CTX)))


def _combine_norm_kernel(y_ref, gate_ref, x_ref, g_ref, o_ref):
    x = x_ref[...] + gate_ref[...] * y_ref[...]
    o_ref[...] = x * lax.rsqrt(jnp.mean(x * x, axis=-1, keepdims=True) + EPS) * g_ref[...]


def _moe_combine_norm(x, y3, mod, layer, final_g, row_off, n_rows):
    off = row_off // TM
    first_lat = N_CTX // TM

    def row_fn(t):
        g = t + off
        return jnp.where(g < first_lat, 0, 1 + (g - first_lat) // (LAT_L // TM))

    return pl.pallas_call(
        _combine_norm_kernel,
        grid=(n_rows // TM,),
        in_specs=[
            pl.BlockSpec((TM, D), lambda t: (t + off, 0)),
            _mod_spec(layer, 5, row_fn),
            pl.BlockSpec((TM, D), lambda t: (t + off, 0)),
            _vec_spec(D),
        ],
        out_specs=pl.BlockSpec((TM, D), lambda t: (t, 0)),
        out_shape=jax.ShapeDtypeStruct((n_rows, D), F32),
        compiler_params=_cparams(1),
        name="combine_final_norm",
    )(y3, mod, x, final_g.reshape(1, D))


def kernel(x_prompt, x_sample, state_gla, c, c_ctx, w_ada, b_ada, norm_g, hy_w_in, hy_b_in, hy_conv_w, hy_conv_b, hy_f_w1, hy_f_b1, hy_f_freq, hy_f_w2, hy_f_b2, hy_f_w3, hy_skip, hy_w_out, hy_b_out, gla_w_q, gla_w_k, gla_w_v, gla_w_g, gla_w_gk1, gla_w_gk2, gla_b_gk, gla_norm_g, gla_w_o, fn_w_out, fn_b_out, pool_w, pool_b, pool_scale, moe_w_rg, moe_w_re, moe_w1, moe_w3, moe_w2, final_g):
    groups = ((CTX_L, CTX_B, 0, None), (LAT_L, LAT_B, N_CTX, LAT_L // GRID_W))

    x = jnp.concatenate([x_prompt.reshape(N_CTX, D), x_sample.reshape(N_LAT, D)], axis=0)
    cond = jnp.zeros((MOD_ROWS, D), F32).at[0].set(c_ctx).at[1:1 + LAT_B].set(c)
    mod = _ada_table(cond, w_ada, b_ada).reshape(DEPTH * MOD_ROWS * 6, 1, D)

    tri_tm = jnp.asarray(np.triu(np.ones((TM, TM)), 1), BF16)

    new_states = []
    pend = None
    for i in range(DEPTH):
        kind, j = i % 4, i // 4
        if kind == 0:
            if pend is not None:
                x = _moe_combine(x, pend[0], mod, pend[1])
            u = _hyena_in(x, mod, i, norm_g[i, 0], hy_w_in[j].astype(BF16), hy_b_in[j], hy_conv_w[j], hy_conv_b[j])
            w_out = hy_w_out[j].astype(BF16)
            for L, nb, off, _ in groups:
                fwd, ff, inv = _dft_mats(L)
                khat = _hyena_filters(L, ff, hy_f_w1[j], hy_f_b1[j], hy_f_freq[j], hy_f_w2[j], hy_f_b2[j],
                                      hy_f_w3[j])
                z = _hyena_conv(u, khat, hy_skip[j], fwd, inv, L, nb, off, D if L == CTX_L else 512)
                x = _outproj(x, z, w_out, hy_b_out[j], mod, i, off, nb * L)
        elif kind == 1:
            w_cat = jnp.concatenate([gla_w_q[j], gla_w_k[j], gla_w_v[j], gla_w_g[j]], axis=1).astype(BF16)
            nk = GLA_H * GLA_DK
            wg1 = jnp.zeros((D, GK1_PAD), F32).at[:, :GLA_RANK].set(gla_w_gk1[j, 0])
            wg1 = wg1.at[:, GLA_RANK:2 * GLA_RANK].set(gla_w_gk1[j, 1]).astype(BF16)
            wg2 = jnp.zeros((GK1_PAD, 2 * nk), F32).at[:GLA_RANK, :nk].set(gla_w_gk2[j, 0])
            wg2 = wg2.at[GLA_RANK:2 * GLA_RANK, nk:].set(gla_w_gk2[j, 1]).astype(BF16)
            proj = _gla_proj(x, mod, i, norm_g[i, 0], w_cat, wg1, wg2, gla_b_gk[j].reshape(1, 2 * nk), pend)
            lower = np.tril(np.ones((GLA_CHUNK, GLA_CHUNK)))
            w_o = gla_w_o[j].astype(BF16)
            for L, nb, off, grid_rows in groups:
                eye = np.eye(L // GLA_CHUNK)
                tri = jnp.asarray(np.stack([np.kron(eye, lower), np.kron(eye, lower.T)]), BF16)
                s0 = None if grid_rows is None else state_gla[:, j]
                o, s_fin = _gla_core(proj, tri, gla_norm_g[j], s0, L, nb, off, GLA_H if L == CTX_L else 1)
                if grid_rows is None:
                    new_states.append(s_fin)
                x = _outproj(x, o, w_o, jnp.zeros((D,), F32), mod, i, off, nb * L, pend)
        elif kind == 2:
            w_out = fn_w_out[j].astype(BF16)
            for L, nb, off, _ in groups:
                chan, seq = _fnet_mats(L)
                x = _fnet(x, mod, i, norm_g[i, 0], chan, seq, w_out, fn_b_out[j], L, nb, off, pend)
        else:
            w_pool = pool_w[j].astype(BF16)
            for L, nb, off, grid_rows in groups:
                mats, inv_cnt = _pool_mats(L, grid_rows)
                x = _pool(x, mod, i, norm_g[i, 0], mats, inv_cnt, w_pool, pool_b[j], pool_scale[j], L, nb, off,
                          pend)

        out = _moe(x, mod, i, norm_g[i, 1], moe_w_rg[i], moe_w_re[i], moe_w1, moe_w3, moe_w2, tri_tm,
                   final_g if i == DEPTH - 1 else None)
        pend = (out, i)

    y_prompt, y_sample = out
    new_state_gla = jnp.stack(new_states, axis=1)
    return (y_prompt.reshape(CTX_B, CTX_L, D), y_sample.reshape(LAT_B, LAT_L, D), new_state_gla)
```

```python
import functools
import math

import jax
import jax.numpy as jnp
import numpy as np
from jax import lax
from jax.experimental import pallas as pl
from jax.experimental.pallas import tpu as pltpu

F32 = jnp.float32
BF16 = jnp.bfloat16

D = 1024
CTX_B, CTX_L = 32, 256
LAT_B, LAT_L = 2, 1024
N_CTX = CTX_B * CTX_L
N_LAT = LAT_B * LAT_L
N_TOK = N_CTX + N_LAT
DEPTH = 4
GRID_W = 64
EPS = 1e-6

HY_BANDS = 8
HY_EMB = 1 + 2 * HY_BANDS
HY_EMB_PAD = 32
HY_HID = 64
HY_FAST_DECAY = 0.3
HY_SLOW_DECAY = 1.5
HY_DECAY_TARGET = 1e-2

GLA_H = 4
GLA_DK = 128
GLA_DV = 256
GLA_RANK = 16
GLA_NORMALIZER = 16.0
GLA_CHUNK = 64

FNET_GROUPS = 4
FNET_C = D // FNET_GROUPS
POOL_WINDOWS = (2, 4, 8, 16)
POOL_G = D // len(POOL_WINDOWS)

MOE_GROUPS = 4
MOE_PER_GROUP = 4
MOE_E = MOE_GROUPS * MOE_PER_GROUP
MOE_HID = D // 2

MOD_ROWS = 8
TM = 512
TM_BIG = 1024
VMEM_LIMIT = 56 * 1024 * 1024


def _cparams(n_axes):
    return pltpu.CompilerParams(dimension_semantics=("arbitrary",) * n_axes, vmem_limit_bytes=VMEM_LIMIT)


def _norm_mod(x, g, sc, sh):
    ms = jnp.mean(x * x, axis=-1, keepdims=True)
    return (x * lax.rsqrt(ms + EPS) * g) * (1.0 + sc) + sh


def _split(a):
    hi = a.astype(BF16)
    lo = (a - hi.astype(F32)).astype(BF16)
    return hi, lo


def _dot(a, b):
    return jnp.dot(a, b, preferred_element_type=F32)


def _dot_precise(a, b):
    a_hi, a_lo = _split(a)
    b_hi, b_lo = _split(b)
    return _dot(a_hi, b_hi) + (_dot(a_hi, b_lo) + _dot(a_lo, b_hi))


def _silu(x):
    return x * (1.0 / (1.0 + jnp.exp(-x)))


def _log_sigmoid(x):
    return jnp.minimum(x, 0.0) - jnp.log(1.0 + jnp.exp(-jnp.abs(x)))


def _mod_spec(layer, chunk, row_fn):
    base = layer * MOD_ROWS * 6 + chunk

    def index_map(*ids):
        return (base + row_fn(*ids) * 6, 0, 0)

    return pl.BlockSpec((None, 1, D), index_map)


def _row_tm(t, *_):
    return jnp.where(t < N_CTX // TM, 0, 1 + (t - N_CTX // TM) // (LAT_L // TM))


def _row_big(t, *_):
    return jnp.where(t < N_CTX // TM_BIG, 0, 1 + (t - N_CTX // TM_BIG) // (LAT_L // TM_BIG))


def _vec_spec(n):
    return pl.BlockSpec((1, n), lambda *ids: (0, 0))


def _full_spec(shape):
    nd = len(shape)
    return pl.BlockSpec(shape, lambda *ids: (0,) * nd)


def _ada_kernel(cond_ref, w_ref, b_ref, o_ref):
    s = _silu(cond_ref[...]).astype(BF16)
    o_ref[...] = _dot(s, w_ref[...].astype(BF16)) + b_ref[...]


def _ada_table(cond, w_ada, b_ada):
    tn = 1536
    return pl.pallas_call(
        _ada_kernel,
        grid=(DEPTH, 6 * D // tn),
        in_specs=[
            pl.BlockSpec((MOD_ROWS, D), lambda i, j: (0, 0)),
            pl.BlockSpec((None, D, tn), lambda i, j: (i, 0, j)),
            pl.BlockSpec((None, 1, tn), lambda i, j: (i, 0, j)),
        ],
        out_specs=pl.BlockSpec((None, MOD_ROWS, tn), lambda i, j: (i, 0, j)),
        out_shape=jax.ShapeDtypeStruct((DEPTH, MOD_ROWS, 6 * D), F32),
        compiler_params=_cparams(2),
        name="ada_table",
    )(cond, w_ada, b_ada.reshape(DEPTH, 1, 6 * D))


def _pending_specs(pend, mod, block_rows, row_index, row_fn):
    if pend is None:
        return [], []
    y, prev_layer = pend
    return [pl.BlockSpec((block_rows, D), row_index), _mod_spec(prev_layer, 5, row_fn)], [y, mod]


def _read_x(x_ref, pending_refs):
    x = x_ref[...]
    if pending_refs:
        y_ref, gate_ref = pending_refs
        x = x + gate_ref[...] * y_ref[...]
    return x


def _outproj_kernel(z_ref, w_ref, b_ref, gate_ref, x_ref, *rest):
    *pending, o_ref = rest
    y = _dot(z_ref[...], w_ref[...]) + b_ref[...]
    o_ref[...] = _read_x(x_ref, pending) + gate_ref[...] * y


def _outproj(x, z, w_bf16, bias, mod, layer, row_off, n_rows, pend=None):
    k = z.shape[1]
    off = row_off // TM
    first_lat = N_CTX // TM

    def row_fn(t):
        g = t + off
        return jnp.where(g < first_lat, 0, 1 + (g - first_lat) // (LAT_L // TM))

    p_specs, p_args = _pending_specs(pend, mod, TM, lambda t: (t + off, 0), row_fn)
    return pl.pallas_call(
        _outproj_kernel,
        grid=(n_rows // TM,),
        in_specs=[
            pl.BlockSpec((TM, k), lambda t: (t, 0)),
            _full_spec((k, D)),
            _vec_spec(D),
            _mod_spec(layer, 2, row_fn),
            pl.BlockSpec((TM, D), lambda t: (t + off, 0)),
        ] + p_specs,
        out_specs=pl.BlockSpec((TM, D), lambda t: (t + off, 0)),
        out_shape=jax.ShapeDtypeStruct((N_TOK, D), F32),
        input_output_aliases={4: 0},
        compiler_params=_cparams(1),
        name="outproj_residual",
    )(z, w_bf16, bias.reshape(1, D), mod, x, *p_args)


def _dft_mats(L):
    n2 = 2 * L
    k = np.arange(L)[:, None].astype(np.float64)
    n = np.arange(n2)[None, :].astype(np.float64)
    ang = 2.0 * np.pi * k * n / n2
    full = np.concatenate([np.cos(ang), -np.sin(ang)], axis=0)
    full[L, :] = np.cos(np.pi * np.arange(n2))
    fwd = full[:, :L]
    bwd = np.zeros((n2, L))
    bwd[:, 1:] = full[:, n2 - np.arange(1, L)]
    t = np.arange(L)[:, None].astype(np.float64)
    kk = np.arange(L)[None, :].astype(np.float64)
    ang_i = 2.0 * np.pi * t * kk / n2
    inv_re = np.cos(ang_i) / L
    inv_re[:, 0] = 1.0 / n2
    inv_im = -np.sin(ang_i) / L
    inv_im[:, 0] = np.cos(np.pi * np.arange(L)) / n2
    inv = np.concatenate([inv_re, inv_im], axis=1)
    return tuple(jnp.asarray(m, F32).astype(BF16) for m in (fwd, np.concatenate([fwd, bwd], axis=1), inv))


def _hyena_pos_emb(L):
    pos = np.arange(L, dtype=np.float64)
    bands = np.linspace(1e-4, HY_BANDS - 1, HY_BANDS)
    ang = (2.0 * np.pi * pos / L)[:, None] * bands[None, :]
    z = np.concatenate([(pos / L)[:, None], np.cos(ang), -np.sin(ang)], axis=-1)
    zp = np.zeros((L, HY_EMB_PAD))
    zp[:, :HY_EMB] = z
    return jnp.asarray(zp, F32)


def _hyena_filter_kernel(z_ref, w1_ref, b1_ref, fr_ref, w2_ref, b2_ref, w3f_ref, w3b_ref, ff_ref, o_ref, *, L, tn):
    j = pl.program_id(1)
    fr = fr_ref[...]
    f = jnp.sin(fr * (_dot_precise(z_ref[...], w1_ref[...]) + b1_ref[...]))
    f = jnp.sin(fr * (_dot_precise(f, w2_ref[...]) + b2_ref[...]))
    t_lin = lax.broadcasted_iota(jnp.int32, (L, tn), 0).astype(F32) / float(L - 1)
    ch = (lax.broadcasted_iota(jnp.int32, (L, tn), 1) + j * tn).astype(F32)
    max_decay = math.log(HY_DECAY_TARGET) / HY_FAST_DECAY
    min_decay = math.log(HY_DECAY_TARGET) / HY_SLOW_DECAY
    deltas = min_decay + ch * ((max_decay - min_decay) / float(D - 1))
    window = jnp.exp(-t_lin * jnp.abs(deltas))
    kf = _dot_precise(f, w3f_ref[...]) * window
    kb = _dot_precise(f, w3b_ref[...]) * window
    taps = jnp.concatenate([kf, kb], axis=0).astype(BF16)
    o_ref[...] = _dot(ff_ref[...], taps)


def _hyena_filters(L, ff, f_w1, f_b1, f_freq, f_w2, f_b2, f_w3):
    tn = 512
    nj = D // tn
    w1p = jnp.zeros((HY_EMB_PAD, HY_HID), F32).at[:HY_EMB].set(f_w1)
    kern = functools.partial(_hyena_filter_kernel, L=L, tn=tn)
    return pl.pallas_call(
        kern,
        grid=(2, nj),
        in_specs=[
            _full_spec((L, HY_EMB_PAD)),
            _full_spec((HY_EMB_PAD, HY_HID)),
            _vec_spec(HY_HID),
            _vec_spec(HY_HID),
            _full_spec((HY_HID, HY_HID)),
            _vec_spec(HY_HID),
            pl.BlockSpec((HY_HID, tn), lambda o, j: (0, o * nj + j)),
            pl.BlockSpec((HY_HID, tn), lambda o, j: (0, (2 + o) * nj + j)),
            _full_spec((2 * L, 2 * L)),
        ],
        out_specs=pl.BlockSpec((None, 2 * L, tn), lambda o, j: (o, 0, j)),
        out_shape=jax.ShapeDtypeStruct((2, 2 * L, D), F32),
        compiler_params=_cparams(2),
        name=f"hyena_filters_L{L}",
    )(_hyena_pos_emb(L), w1p, f_b1.reshape(1, -1), f_freq.reshape(1, -1), f_w2, f_b2.reshape(1, -1),
      f_w3, f_w3, ff)


def _hyena_in_kernel(x_ref, g_ref, sc_ref, sh_ref, w_ref, b_ref, cw_ref, cb_ref, o_ref, h_scr):
    t = pl.program_id(0)

    @pl.when(pl.program_id(1) == 0)
    def _():
        h_scr[...] = _norm_mod(x_ref[...], g_ref[...], sc_ref[...], sh_ref[...]).astype(BF16)

    u = _dot(h_scr[...], w_ref[...]) + b_ref[...]
    cw = cw_ref[...]
    o_ref[...] = pltpu.roll(u, 1, 0) * cw[0:1] + u * cw[1:2] + pltpu.roll(u, TM_BIG - 1, 0) * cw[2:3] + cb_ref[...]
    is_ctx = (t < N_CTX // TM_BIG).astype(F32)
    for start in range(0, TM_BIG, CTX_L):
        f = 1.0 if start % LAT_L == 0 else is_ctx
        before = (start - 1) % TM_BIG
        o_ref[start:start + 1, :] = o_ref[start:start + 1, :] - f * (u[before:before + 1] * cw[0:1])
        end = start + CTX_L - 1
        g = 1.0 if (end + 1) % LAT_L == 0 else is_ctx
        after = (end + 1) % TM_BIG
        o_ref[end:end + 1, :] = o_ref[end:end + 1, :] - g * (u[after:after + 1] * cw[2:3])


def _hyena_in(x, mod, layer, norm_g, w_in_bf16, b_in, conv_w, conv_b):
    return pl.pallas_call(
        _hyena_in_kernel,
        grid=(N_TOK // TM_BIG, 3),
        in_specs=[
            pl.BlockSpec((TM_BIG, D), lambda t, p: (t, 0)),
            _vec_spec(D),
            _mod_spec(layer, 1, _row_big),
            _mod_spec(layer, 0, _row_big),
            pl.BlockSpec((D, D), lambda t, p: (0, p)),
            pl.BlockSpec((1, D), lambda t, p: (0, p)),
            pl.BlockSpec((3, D), lambda t, p: (0, p)),
            pl.BlockSpec((1, D), lambda t, p: (0, p)),
        ],
        out_specs=pl.BlockSpec((TM_BIG, D), lambda t, p: (t, p)),
        out_shape=jax.ShapeDtypeStruct((N_TOK, 3 * D), F32),
        scratch_shapes=[pltpu.VMEM((TM_BIG, D), BF16)],
        compiler_params=_cparams(2),
        name="hyena_in",
    )(x, norm_g.reshape(1, D), mod, mod, w_in_bf16, b_in.reshape(1, -1), conv_w, conv_b.reshape(1, -1))


def _hyena_conv_kernel(v_ref, x1_ref, x2_ref, kh_ref, skip_ref, fwd_ref, inv_ref, o_ref, *, L):
    fwd = fwd_ref[...]
    inv = inv_ref[...]
    row0 = lax.broadcasted_iota(jnp.int32, (L, v_ref.shape[1]), 0) == 0

    def long_conv(z, order):
        zh = _dot(fwd, z.astype(BF16))
        zr, zi = zh[:L], zh[L:]
        kr, ki = kh_ref[order, :L, :], kh_ref[order, L:, :]
        pr = jnp.where(row0, zr * kr, zr * kr - zi * ki)
        pi = jnp.where(row0, zi * ki, zr * ki + zi * kr)
        prod = jnp.concatenate([pr, pi], axis=0).astype(BF16)
        return _dot(inv, prod) + z * skip_ref[order:order + 1, :]

    z = x1_ref[...] * long_conv(v_ref[...], 0)
    z = x2_ref[...] * long_conv(z, 1)
    o_ref[...] = z.astype(BF16)


def _hyena_conv(u, khat, skip, fwd, inv, L, n_batch, row_off, tn):
    nj = D // tn
    rb = row_off // L
    kern = functools.partial(_hyena_conv_kernel, L=L)
    return pl.pallas_call(
        kern,
        grid=(nj, n_batch),
        in_specs=[
            pl.BlockSpec((L, tn), lambda j, b: (rb + b, j)),
            pl.BlockSpec((L, tn), lambda j, b: (rb + b, nj + j)),
            pl.BlockSpec((L, tn), lambda j, b: (rb + b, 2 * nj + j)),
            pl.BlockSpec((2, 2 * L, tn), lambda j, b: (0, 0, j)),
            pl.BlockSpec((2, tn), lambda j, b: (0, j)),
            _full_spec((2 * L, L)),
            _full_spec((L, 2 * L)),
        ],
        out_specs=pl.BlockSpec((L, tn), lambda j, b: (b, j)),
        out_shape=jax.ShapeDtypeStruct((n_batch * L, D), BF16),
        compiler_params=_cparams(2),
        name=f"hyena_conv_L{L}",
    )(u, u, u, khat, skip, fwd, inv)


GLA_PROJ = 2 * GLA_H * GLA_DK + 2 * GLA_H * GLA_DV
GLA_COLS = GLA_PROJ + 2 * GLA_H * GLA_DK
GK1_PAD = 128


def _gla_proj_kernel(x_ref, g_ref, sc_ref, sh_ref, w_ref, wg1_ref, wg2_ref, bg_ref, *rest):
    *pending, o_ref = rest
    h = _norm_mod(_read_x(x_ref, pending), g_ref[...], sc_ref[...], sh_ref[...]).astype(BF16)
    p = _dot(h, w_ref[...])
    nq = GLA_H * GLA_DK
    o_ref[:, 0:nq] = p[:, 0:nq] * (GLA_DK ** -0.5)
    o_ref[:, nq:nq + nq + GLA_H * GLA_DV] = p[:, nq:nq + nq + GLA_H * GLA_DV]
    o_ref[:, 2 * nq + GLA_H * GLA_DV:GLA_PROJ] = _silu(p[:, 2 * nq + GLA_H * GLA_DV:GLA_PROJ])
    low = _dot(h, wg1_ref[...]).astype(BF16)
    gk = _dot(low, wg2_ref[...]) + bg_ref[...]
    o_ref[:, GLA_PROJ:GLA_COLS] = _log_sigmoid(gk) / GLA_NORMALIZER


def _gla_proj(x, mod, layer, norm_g, w_cat, wg1, wg2, bg, pend=None):
    p_specs, p_args = _pending_specs(pend, mod, TM, lambda t: (t, 0), _row_tm)
    return pl.pallas_call(
        _gla_proj_kernel,
        grid=(N_TOK // TM,),
        in_specs=[
            pl.BlockSpec((TM, D), lambda t: (t, 0)),
            _vec_spec(D),
            _mod_spec(layer, 1, _row_tm),
            _mod_spec(layer, 0, _row_tm),
            _full_spec((D, GLA_PROJ)),
            _full_spec((D, GK1_PAD)),
            _full_spec((GK1_PAD, 2 * GLA_H * GLA_DK)),
            _vec_spec(2 * GLA_H * GLA_DK),
        ] + p_specs,
        out_specs=pl.BlockSpec((TM, GLA_COLS), lambda t: (t, 0)),
        out_shape=jax.ShapeDtypeStruct((N_TOK, GLA_COLS), F32),
        compiler_params=_cparams(1),
        name="gla_proj",
    )(x, norm_g.reshape(1, D), mod, mod, w_cat, wg1, wg2, bg, *p_args)


def _gla_core_kernel(*refs, L, has_s0, hps):
    if has_s0:
        q_ref, k_ref, v_ref, g_ref, gkf_ref, gkb_ref, tri_ref, ng_ref, s0_ref, o_ref, sf_ref, acc = refs
    else:
        q_ref, k_ref, v_ref, g_ref, gkf_ref, gkb_ref, tri_ref, ng_ref, o_ref, sf_ref, acc = refs
        s0_ref = None
    C = GLA_CHUNK
    n = L // C
    ri = lax.broadcasted_iota(jnp.int32, (C, C), 0)
    ci = lax.broadcasted_iota(jnp.int32, (C, C), 1)
    nt_dims = (((1,), (1,)), ((), ()))
    tn_dims = (((0,), (0,)), ((), ()))

    for hh in range(hps):
        kc = slice(hh * GLA_DK, (hh + 1) * GLA_DK)
        vc = slice(hh * GLA_DV, (hh + 1) * GLA_DV)
        for direction, gk_ref in enumerate((gkf_ref, gkb_ref)):
            keep = (ci <= ri) if direction == 0 else (ci >= ri)
            last = C - 1 if direction == 0 else 0
            gk_hi, gk_lo = _split(gk_ref[:, kc])
            b_all = _dot(tri_ref[direction], jnp.concatenate([gk_hi, gk_lo], axis=1))
            b_all = b_all[:, :GLA_DK] + b_all[:, GLA_DK:]
            st = s0_ref[direction, hh].T if has_s0 else jnp.zeros((GLA_DV, GLA_DK), F32)
            order = range(n) if direction == 0 else range(n - 1, -1, -1)
            for c in order:
                rows = slice(c * C, (c + 1) * C)
                b = b_all[rows]
                b_last = b[last:last + 1, :]
                q = q_ref[rows, kc]
                k = k_ref[rows, kc]
                v = v_ref[rows, vc].astype(BF16)
                qe = (q * jnp.exp(b)).astype(BF16)
                ke = (k * jnp.exp(-b)).astype(BF16)
                kd = (k * jnp.exp(b_last - b)).astype(BF16)
                scores = lax.dot_general(qe, ke, nt_dims, preferred_element_type=F32)
                scores = jnp.where(keep, scores, 0.0).astype(BF16)
                o = _dot(scores, v) + lax.dot_general(qe, st.astype(BF16), nt_dims, preferred_element_type=F32)
                if direction == 0:
                    acc[rows, vc] = o
                else:
                    acc[rows, vc] = acc[rows, vc] + o
                st = jnp.exp(b_last) * st + lax.dot_general(v, kd, tn_dims, preferred_element_type=F32)
            sf_ref[direction, hh] = st.T

        o = acc[:, vc]
        o = o * lax.rsqrt(jnp.mean(o * o, axis=-1, keepdims=True) + EPS) * ng_ref[...]
        o_ref[:, vc] = (o * g_ref[:, vc]).astype(BF16)


def _gla_core(proj, tri, norm_g, s0, L, n_batch, row_off, hps):
    rb = row_off // L
    H = GLA_H
    nh = H // hps
    has_s0 = s0 is not None
    kern = functools.partial(_gla_core_kernel, L=L, has_s0=has_s0, hps=hps)
    kb, vb = GLA_DK * hps, GLA_DV * hps
    in_specs = [
        pl.BlockSpec((L, kb), lambda b, h: (rb + b, h)),
        pl.BlockSpec((L, kb), lambda b, h: (rb + b, nh + h)),
        pl.BlockSpec((L, vb), lambda b, h: (rb + b, (2 * H * GLA_DK) // vb + h)),
        pl.BlockSpec((L, vb), lambda b, h: (rb + b, (2 * H * GLA_DK) // vb + nh + h)),
        pl.BlockSpec((L, kb), lambda b, h: (rb + b, GLA_PROJ // kb + h)),
        pl.BlockSpec((L, kb), lambda b, h: (rb + b, GLA_PROJ // kb + nh + h)),
        _full_spec((2, L, L)),
        _vec_spec(GLA_DV),
    ]
    args = [proj] * 6 + [tri, norm_g.reshape(1, GLA_DV)]
    state_spec = pl.BlockSpec((None, 2, hps, GLA_DK, GLA_DV), lambda b, h: (b, 0, h, 0, 0))
    if has_s0:
        in_specs.append(state_spec)
        args.append(s0)
    return pl.pallas_call(
        kern,
        grid=(n_batch, nh),
        in_specs=in_specs,
        out_specs=[pl.BlockSpec((L, vb), lambda b, h: (b, h)), state_spec],
        out_shape=[
            jax.ShapeDtypeStruct((n_batch * L, H * GLA_DV), BF16),
            jax.ShapeDtypeStruct((n_batch, 2, H, GLA_DK, GLA_DV), F32),
        ],
        scratch_shapes=[pltpu.VMEM((L, vb), F32)],
        compiler_params=_cparams(2),
        name=f"gla_core_L{L}",
    )(*args)


def _fnet_mats(L):
    c = np.arange(FNET_C)
    ang_c = 2.0 * np.pi * np.outer(c, c) / FNET_C
    chan = np.concatenate([np.cos(ang_c), np.sin(ang_c)], axis=1) / math.sqrt(FNET_C)
    t = np.arange(L)
    ang_l = 2.0 * np.pi * np.outer(t, t) / L
    seq = np.concatenate([np.cos(ang_l), -np.sin(ang_l)], axis=1) / math.sqrt(L)
    return jnp.asarray(chan, F32).astype(BF16), jnp.asarray(seq, F32).astype(BF16)


def _fnet_kernel(x_ref, g_ref, sc_ref, sh_ref, gate_ref, chan_ref, seq_ref, w_ref, b_ref, *rest):
    *pending, o_ref = rest
    x = _read_x(x_ref, pending)
    h = _norm_mod(x, g_ref[...], sc_ref[...], sh_ref[...]).astype(BF16)
    chan = chan_ref[...]
    cos_parts, sin_parts = [], []
    for gi in range(FNET_GROUPS):
        cs = _dot(h[:, gi * FNET_C:(gi + 1) * FNET_C], chan)
        cos_parts.append(cs[:, :FNET_C])
        sin_parts.append(cs[:, FNET_C:])
    stacked = jnp.concatenate([jnp.concatenate(cos_parts, axis=1), jnp.concatenate(sin_parts, axis=1)], axis=0)
    mixed = _dot(seq_ref[...], stacked.astype(BF16))
    y = _dot(mixed.astype(BF16), w_ref[...]) + b_ref[...]
    o_ref[...] = x + gate_ref[...] * y


def _fnet(x, mod, layer, norm_g, chan, seq, w_bf16, bias, L, n_batch, row_off, pend=None):
    rb = row_off // L
    lat = row_off > 0

    def row_fn(b):
        return 1 + b if lat else 0

    p_specs, p_args = _pending_specs(pend, mod, L, lambda b: (rb + b, 0), row_fn)
    return pl.pallas_call(
        _fnet_kernel,
        grid=(n_batch,),
        in_specs=[
            pl.BlockSpec((L, D), lambda b: (rb + b, 0)),
            _vec_spec(D),
            _mod_spec(layer, 1, row_fn),
            _mod_spec(layer, 0, row_fn),
            _mod_spec(layer, 2, row_fn),
            _full_spec((FNET_C, 2 * FNET_C)),
            _full_spec((L, 2 * L)),
            _full_spec((D, D)),
            _vec_spec(D),
        ] + p_specs,
        out_specs=pl.BlockSpec((L, D), lambda b: (rb + b, 0)),
        out_shape=jax.ShapeDtypeStruct((N_TOK, D), F32),
        input_output_aliases={0: 0},
        compiler_params=_cparams(1),
        name=f"fnet_L{L}",
    )(x, norm_g.reshape(1, D), mod, mod, mod, chan, seq, w_bf16, bias.reshape(1, D), *p_args)


def _window_bounds(n, k):
    t = np.arange(n)
    lo, hi = k // 2, k - k // 2 - 1
    return np.maximum(t - lo, 0), np.minimum(t + hi + 1, n)


def _pool_mats(L, grid_rows):
    mats, inv = [], []
    for k in POOL_WINDOWS:
        if grid_rows is None:
            s, e = _window_bounds(L, k)
            idx = np.arange(L)[None, :]
            m = ((idx >= s[:, None]) & (idx < e[:, None])).astype(np.float64)
            cnt = (e - s).astype(np.float64)
        else:
            sr, er = _window_bounds(grid_rows, k)
            sc, ec = _window_bounds(GRID_W, k)
            ir = np.arange(grid_rows)[None, :]
            ic = np.arange(GRID_W)[None, :]
            mr = ((ir >= sr[:, None]) & (ir < er[:, None])).astype(np.float64)
            mc = ((ic >= sc[:, None]) & (ic < ec[:, None])).astype(np.float64)
            m = np.kron(mr, mc)
            cnt = np.kron((er - sr).astype(np.float64), (ec - sc).astype(np.float64))
        mats.append(m)
        inv.append(1.0 / cnt)
    return jnp.asarray(np.stack(mats), BF16), jnp.asarray(np.stack(inv)[:, :, None], F32)


def _pool_kernel(x_ref, g_ref, sc_ref, sh_ref, gate_ref, m_ref, ic_ref, w_ref, b_ref, ps_ref, *rest):
    *pending, o_ref = rest
    x = _read_x(x_ref, pending)
    h = _norm_mod(x, g_ref[...], sc_ref[...], sh_ref[...])
    outs = []
    for gi in range(len(POOL_WINDOWS)):
        hg = h[:, gi * POOL_G:(gi + 1) * POOL_G]
        hi, lo = _split(hg)
        m = m_ref[gi]
        mean = (_dot(m, hi) + _dot(m, lo)) * ic_ref[gi]
        outs.append(_dot((mean - hg).astype(BF16), w_ref[gi]))
    y = (jnp.concatenate(outs, axis=1) + b_ref[...]) * ps_ref[...]
    o_ref[...] = x + gate_ref[...] * y


def _pool(x, mod, layer, norm_g, mats, inv_cnt, w_bf16, bias, scale, L, n_batch, row_off, pend=None):
    rb = row_off // L
    lat = row_off > 0
    G = len(POOL_WINDOWS)

    def row_fn(b):
        return 1 + b if lat else 0

    p_specs, p_args = _pending_specs(pend, mod, L, lambda b: (rb + b, 0), row_fn)
    return pl.pallas_call(
        _pool_kernel,
        grid=(n_batch,),
        in_specs=[
            pl.BlockSpec((L, D), lambda b: (rb + b, 0)),
            _vec_spec(D),
            _mod_spec(layer, 1, row_fn),
            _mod_spec(layer, 0, row_fn),
            _mod_spec(layer, 2, row_fn),
            _full_spec((G, L, L)),
            _full_spec((G, L, 1)),
            _full_spec((G, POOL_G, POOL_G)),
            _vec_spec(D),
            _vec_spec(D),
        ] + p_specs,
        out_specs=pl.BlockSpec((L, D), lambda b: (rb + b, 0)),
        out_shape=jax.ShapeDtypeStruct((N_TOK, D), F32),
        input_output_aliases={0: 0},
        compiler_params=_cparams(1),
        name=f"pool_L{L}",
    )(x, norm_g.reshape(1, D), mod, mod, mod, mats, inv_cnt, w_bf16, bias.reshape(1, D), scale.reshape(1, D),
      *p_args)


ROUTER_PAD = 128
LANES = 128
D_EXT = D + LANES
MOE_TILE = 1024
MOE_MAX_TILES = N_TOK // MOE_TILE + MOE_GROUPS
MOE_ROWS = MOE_MAX_TILES * MOE_TILE
MOE_TILE_SHIFT = MOE_TILE.bit_length() - 1
assert 1 << MOE_TILE_SHIFT == MOE_TILE
MOE_DMA_CHUNK = 64
MOE_CHUNK_SHIFT = MOE_DMA_CHUNK.bit_length() - 1
assert 1 << MOE_CHUNK_SHIFT == MOE_DMA_CHUNK and MOE_MAX_TILES % 2 == 0
MOE_Y_ROWS = N_TOK + 2 * MOE_DMA_CHUNK


ROUTE_ROWS = 8


def _moe_route_kernel(x_ref, g_ref, sc_ref, sh_ref, wr_ref, tri_ref, h3_ref, route_ref, cnt_ref, carry):
    t = pl.program_id(0)
    refs = (x_ref, g_ref, sc_ref, sh_ref, wr_ref, tri_ref, h3_ref, route_ref, cnt_ref, carry)
    pl.when(t < N_TOK // TM)(functools.partial(_moe_route_tile, t, *refs))

    @pl.when(t == N_TOK // TM)
    def _():
        h3_ref[...] = jnp.zeros_like(h3_ref)


def _moe_route_tile(t, x_ref, g_ref, sc_ref, sh_ref, wr_ref, tri_ref, h3_ref, route_ref, cnt_ref, carry):
    @pl.when(t == 0)
    def _():
        carry[...] = jnp.zeros_like(carry)

    h = _norm_mod(x_ref[...], g_ref[...], sc_ref[...], sh_ref[...])
    w_hi, w_lo = _split(wr_ref[...])
    h_hi, h_lo = _split(h)
    nt = (((1,), (1,)), ((), ()))
    logits = (lax.dot_general(w_hi, h_hi, nt, preferred_element_type=F32)
              + (lax.dot_general(w_hi, h_lo, nt, preferred_element_type=F32)
                 + lax.dot_general(w_lo, h_hi, nt, preferred_element_type=F32)))
    neg = jnp.float32(-jnp.inf)
    r8 = lax.broadcasted_iota(jnp.int32, (ROUTE_ROWS, TM), 0)
    r16 = lax.broadcasted_iota(jnp.int32, (MOE_E, TM), 0)
    gl = jnp.where(r8 < MOE_GROUPS, logits[MOE_E:MOE_E + ROUTE_ROWS], neg)
    g_max = jnp.max(gl, axis=0, keepdims=True)
    g_idx = jnp.min(jnp.where(gl == g_max, r8, ROUTE_ROWS), axis=0, keepdims=True)
    p_grp = 1.0 / jnp.sum(jnp.exp(gl - g_max), axis=0, keepdims=True)
    in_grp = (r16 >> 2) == g_idx
    el = jnp.where(in_grp, logits[:MOE_E], neg)
    m1 = jnp.max(el, axis=0, keepdims=True)
    i1 = jnp.min(jnp.where(el == m1, r16, MOE_E), axis=0, keepdims=True)
    z = jnp.sum(jnp.exp(el - m1), axis=0, keepdims=True)
    el2 = jnp.where(r16 == i1, neg, el)
    m2 = jnp.max(el2, axis=0, keepdims=True)
    i2 = jnp.min(jnp.where(el2 == m2, r16, MOE_E), axis=0, keepdims=True)
    p1 = 1.0 / z
    p2 = jnp.exp(m2 - m1) / z
    tot = p1 + p2
    eid = r8 + MOE_PER_GROUP * g_idx
    in4 = r8 < MOE_PER_GROUP
    cw4 = (jnp.where(in4 & (eid == i1), p_grp * (p1 / tot), 0.0)
           + jnp.where(in4 & (eid == i2), p_grp * (p2 / tot), 0.0))
    member = jnp.where(r8 == g_idx, 1.0, 0.0)
    before = _dot(member.astype(BF16), tri_ref[...]) + carry[:, 0:1]
    rank = jnp.sum(jnp.where(r8 == g_idx, before, 0.0), axis=0, keepdims=True)
    carry[...] = carry[...] + jnp.sum(member, axis=1, keepdims=True)
    cnt_ref[...] = carry[...].astype(jnp.int32)
    route_ref[...] = jnp.where(r8 == 0, g_idx, jnp.where(r8 == 1, rank.astype(jnp.int32), 0))
    h3_ref[:, :D] = h
    cw_rows = jnp.concatenate([cw4, jnp.zeros((LANES - ROUTE_ROWS, TM), F32)], axis=0)
    h3_ref[:, D:] = cw_rows.T


def _moe_route(x, mod, layer, norm_g, w_router_t, tri):
    nt = N_TOK // TM

    def tok_tile(t):
        return jnp.minimum(t, nt - 1)

    return pl.pallas_call(
        _moe_route_kernel,
        grid=(nt + 1,),
        in_specs=[
            pl.BlockSpec((TM, D), lambda t: (tok_tile(t), 0)),
            _vec_spec(D),
            _mod_spec(layer, 4, lambda t: _row_tm(tok_tile(t))),
            _mod_spec(layer, 3, lambda t: _row_tm(tok_tile(t))),
            _full_spec((ROUTER_PAD, D)),
            _full_spec((TM, TM)),
        ],
        out_specs=[
            pl.BlockSpec((TM, D_EXT), lambda t: (t, 0)),
            pl.BlockSpec((ROUTE_ROWS, TM), lambda t: (0, tok_tile(t))),
            pl.BlockSpec((ROUTE_ROWS, LANES), lambda t: (0, 0)),
        ],
        out_shape=[
            jax.ShapeDtypeStruct((N_TOK + TM, D_EXT), F32),
            jax.ShapeDtypeStruct((ROUTE_ROWS, N_TOK), jnp.int32),
            jax.ShapeDtypeStruct((ROUTE_ROWS, LANES), jnp.int32),
        ],
        scratch_shapes=[pltpu.VMEM((ROUTE_ROWS, LANES), F32)],
        compiler_params=_cparams(1),
        name="moe_route",
    )(x, norm_g.reshape(1, D), mod, mod, w_router_t, tri)


def _moe_invert_kernel(pos_ref, src_ref):
    for parity in (0, 1):
        def mark(j, carry, parity=parity):
            tile = 2 * lax.shift_right_logical(j, jnp.int32(MOE_TILE_SHIFT - MOE_CHUNK_SHIFT)) + parity
            base = tile * MOE_TILE + (j & (MOE_TILE // MOE_DMA_CHUNK - 1)) * MOE_DMA_CHUNK
            for rr in range(MOE_DMA_CHUNK):
                src_ref[base + rr] = jnp.int32(N_TOK + rr + MOE_DMA_CHUNK * parity)
            return carry

        lax.fori_loop(0, (MOE_MAX_TILES // 2) * (MOE_TILE // MOE_DMA_CHUNK), mark, 0)

    def place(n, carry):
        src_ref[pos_ref[n]] = n
        return carry

    lax.fori_loop(0, N_TOK, place, 0, unroll=16)


def _moe_invert(pos):
    smem = pl.BlockSpec(memory_space=pltpu.SMEM)
    return pl.pallas_call(
        _moe_invert_kernel,
        in_specs=[smem],
        out_specs=smem,
        out_shape=jax.ShapeDtypeStruct((MOE_ROWS,), jnp.int32),
        name="moe_invert",
    )(pos)


def _moe_expert_kernel(src_ref, grp_ref, nact_ref, nchunk_ref, h_hbm, w1_ref, w3_ref, w2_ref, y_hbm,
                       xb0, xb1, ab0, ab1, gsem, ssem):
    t = pl.program_id(0)
    k = pl.program_id(1)
    n_active = nact_ref[0]
    last = n_active - 1
    T = MOE_TILE
    CH = MOE_DMA_CHUNK
    xbufs, accs = (xb0, xb1), (ab0, ab1)

    def gather_row(tile, slot, base, rr):
        tok = src_ref[tile * T + base + rr]
        rows = xbufs[slot].at[pl.ds(base, CH), :]
        return pltpu.make_async_copy(h_hbm.at[pl.ds(tok, 1), :], rows.at[pl.ds(rr, 1), :], gsem.at[slot])

    def scatter_row(tile, slot, base, rr):
        dst = src_ref[tile * T + base + rr]
        rows = accs[slot].at[pl.ds(base, CH), :]
        return pltpu.make_async_copy(rows.at[pl.ds(rr, 1), :], y_hbm.at[pl.ds(dst, 1), :], ssem.at[slot])

    def start_rows(make, tile, slot):
        def chunk(c, carry):
            base = pl.multiple_of(c * CH, CH)
            for rr in range(CH):
                make(tile, slot, base, rr).start()
            return carry

        lax.fori_loop(0, nchunk_ref[tile], chunk, 0)

    def wait_rows(tile, slot, gather):
        def chunk(c, carry):
            if gather:
                pltpu.make_async_copy(h_hbm.at[pl.ds(0, CH), :], xbufs[slot].at[pl.ds(0, CH), :], gsem.at[slot]).wait()
            else:
                pltpu.make_async_copy(accs[slot].at[pl.ds(0, CH), :], y_hbm.at[pl.ds(0, CH), :], ssem.at[slot]).wait()
            return carry

        lax.fori_loop(0, nchunk_ref[tile], chunk, 0)

    def step(slot):
        other = 1 - slot
        xb, acc = xbufs[slot], accs[slot]

        @pl.when(k == 0)
        def _():
            if slot == 0:
                @pl.when(t == 0)
                def _():
                    xb0[...] = jnp.zeros_like(xb0)
                    xb1[...] = jnp.zeros_like(xb1)
                    ab0[...] = jnp.zeros_like(ab0)
                    ab1[...] = jnp.zeros_like(ab1)
                    dump = pltpu.make_async_copy(ab0.at[pl.ds(0, 2 * CH), :], y_hbm.at[pl.ds(N_TOK, 2 * CH), :],
                                                 ssem.at[0])
                    dump.start()
                    dump.wait()
                    start_rows(gather_row, 0, 0)

            wait_rows(t, slot, True)

            @pl.when(t < last)
            def _():
                start_rows(gather_row, t + 1, other)

            @pl.when(t >= 2)
            def _():
                wait_rows(t - 2, slot, False)

        x = xb[:, :D].astype(BF16)
        a = _dot(x, w1_ref[...].astype(BF16))
        b = _dot(x, w3_ref[...].astype(BF16))
        lane = lax.broadcasted_iota(jnp.int32, (T, LANES), 1)
        cwk = jnp.sum(jnp.where(lane == k, xb[:, D:], 0.0), axis=-1, keepdims=True)
        hid = (_silu(a) * b * cwk).astype(BF16)
        acc[...] = jnp.where(k > 0, acc[...], 0.0) + _dot(hid, w2_ref[...].astype(BF16))

        @pl.when(k == MOE_PER_GROUP - 1)
        def _():
            start_rows(scatter_row, t, slot)

            @pl.when(t == last)
            def _():
                wait_rows(t, slot, False)

                @pl.when(t >= 1)
                def _():
                    wait_rows(t - 1, other, False)

    for slot in (0, 1):
        pl.when((t < n_active) & (t % 2 == slot))(functools.partial(step, slot))


def _moe_experts(h_ext, src, tile_group, n_active, n_chunk, layer, w1, w3, w2):
    T = MOE_TILE

    def w_index(t, k, src_ref, grp_ref, nact_ref, nchunk_ref):
        last = nact_ref[0] - 1
        e = jnp.where(t <= last, grp_ref[t] * MOE_PER_GROUP + k, grp_ref[last] * MOE_PER_GROUP + MOE_PER_GROUP - 1)
        return (layer, e, 0, 0)

    grid_spec = pltpu.PrefetchScalarGridSpec(
        num_scalar_prefetch=4,
        grid=(MOE_MAX_TILES, MOE_PER_GROUP),
        in_specs=[
            pl.BlockSpec(memory_space=pl.ANY),
            pl.BlockSpec((None, None, D, MOE_HID), w_index),
            pl.BlockSpec((None, None, D, MOE_HID), w_index),
            pl.BlockSpec((None, None, MOE_HID, D), w_index),
        ],
        out_specs=pl.BlockSpec(memory_space=pl.ANY),
        scratch_shapes=[
            pltpu.VMEM((T, D_EXT), F32),
            pltpu.VMEM((T, D_EXT), F32),
            pltpu.VMEM((T, D), F32),
            pltpu.VMEM((T, D), F32),
            pltpu.SemaphoreType.DMA((2,)),
            pltpu.SemaphoreType.DMA((2,)),
        ],
    )
    return pl.pallas_call(
        _moe_expert_kernel,
        grid_spec=grid_spec,
        out_shape=jax.ShapeDtypeStruct((MOE_Y_ROWS, D), F32),
        compiler_params=_cparams(2),
        name="moe_experts",
    )(src, tile_group, n_active, n_chunk, h_ext, w1, w3, w2)


def _moe_combine_kernel(y_ref, gate_ref, x_ref, o_ref):
    o_ref[...] = x_ref[...] + gate_ref[...] * y_ref[...]


def _moe_combine(x, y3, mod, layer):
    return pl.pallas_call(
        _moe_combine_kernel,
        grid=(N_TOK // TM,),
        in_specs=[
            pl.BlockSpec((TM, D), lambda t: (t, 0)),
            _mod_spec(layer, 5, _row_tm),
            pl.BlockSpec((TM, D), lambda t: (t, 0)),
        ],
        out_specs=pl.BlockSpec((TM, D), lambda t: (t, 0)),
        out_shape=jax.ShapeDtypeStruct((N_TOK, D), F32),
        input_output_aliases={2: 0},
        compiler_params=_cparams(1),
        name="moe_combine",
    )(y3, mod, x)


def _moe(x, mod, layer, norm_g, w_rg, w_re, w1, w3, w2, tri, final_g=None):
    w_router_t = jnp.zeros((ROUTER_PAD, D), F32).at[:MOE_E].set(w_re.T).at[MOE_E:MOE_E + MOE_GROUPS].set(w_rg.T)
    h3, route, counts = _moe_route(x, mod, layer, norm_g, w_router_t, tri)
    cnt = counts[:MOE_GROUPS, 0]
    ntile = (cnt + MOE_TILE - 1) // MOE_TILE
    tile_end = jnp.cumsum(ntile)
    seg_start = (tile_end - ntile) * MOE_TILE
    g_idx, rank = route[0], route[1]
    pos = jnp.sum(jnp.where(g_idx[None, :] == jnp.arange(MOE_GROUPS)[:, None], seg_start[:, None], 0), axis=0) + rank
    tiles = jnp.arange(MOE_MAX_TILES, dtype=jnp.int32)
    tile_group = jnp.minimum(jnp.sum(tiles[:, None] >= tile_end[None, :], axis=1), MOE_GROUPS - 1).astype(jnp.int32)
    n_active = tile_end[-1:].astype(jnp.int32)
    src = _moe_invert(pos.astype(jnp.int32))
    first_tile = (tile_end - ntile)[tile_group]
    real_rows = jnp.clip(cnt[tile_group] - (tiles - first_tile) * MOE_TILE, 0, MOE_TILE)
    n_chunk = ((real_rows + MOE_DMA_CHUNK - 1) // MOE_DMA_CHUNK).astype(jnp.int32)
    y3 = _moe_experts(h3, src, tile_group, n_active, n_chunk, layer, w1, w3, w2)
    if final_g is None:
        return y3
    return tuple(_moe_combine_norm(x, y3, mod, layer, final_g, off, nb * L) for L, nb, off in
                 ((CTX_L, CTX_B, 0), (LAT_L, LAT_B, N_CTX)))


def _combine_norm_kernel(y_ref, gate_ref, x_ref, g_ref, o_ref):
    x = x_ref[...] + gate_ref[...] * y_ref[...]
    o_ref[...] = x * lax.rsqrt(jnp.mean(x * x, axis=-1, keepdims=True) + EPS) * g_ref[...]


def _moe_combine_norm(x, y3, mod, layer, final_g, row_off, n_rows):
    off = row_off // TM
    first_lat = N_CTX // TM

    def row_fn(t):
        g = t + off
        return jnp.where(g < first_lat, 0, 1 + (g - first_lat) // (LAT_L // TM))

    return pl.pallas_call(
        _combine_norm_kernel,
        grid=(n_rows // TM,),
        in_specs=[
            pl.BlockSpec((TM, D), lambda t: (t + off, 0)),
            _mod_spec(layer, 5, row_fn),
            pl.BlockSpec((TM, D), lambda t: (t + off, 0)),
            _vec_spec(D),
        ],
        out_specs=pl.BlockSpec((TM, D), lambda t: (t, 0)),
        out_shape=jax.ShapeDtypeStruct((n_rows, D), F32),
        compiler_params=_cparams(1),
        name="combine_final_norm",
    )(y3, mod, x, final_g.reshape(1, D))


def kernel(x_prompt, x_sample, state_gla, c, c_ctx, w_ada, b_ada, norm_g, hy_w_in, hy_b_in, hy_conv_w, hy_conv_b, hy_f_w1, hy_f_b1, hy_f_freq, hy_f_w2, hy_f_b2, hy_f_w3, hy_skip, hy_w_out, hy_b_out, gla_w_q, gla_w_k, gla_w_v, gla_w_g, gla_w_gk1, gla_w_gk2, gla_b_gk, gla_norm_g, gla_w_o, fn_w_out, fn_b_out, pool_w, pool_b, pool_scale, moe_w_rg, moe_w_re, moe_w1, moe_w3, moe_w2, final_g):
    groups = ((CTX_L, CTX_B, 0, None), (LAT_L, LAT_B, N_CTX, LAT_L // GRID_W))

    x = jnp.concatenate([x_prompt.reshape(N_CTX, D), x_sample.reshape(N_LAT, D)], axis=0)
    cond = jnp.zeros((MOD_ROWS, D), F32).at[0].set(c_ctx).at[1:1 + LAT_B].set(c)
    mod = _ada_table(cond, w_ada, b_ada).reshape(DEPTH * MOD_ROWS * 6, 1, D)

    tri_tm = jnp.asarray(np.triu(np.ones((TM, TM)), 1), BF16)

    new_states = []
    pend = None
    for i in range(DEPTH):
        kind, j = i % 4, i // 4
        if kind == 0:
            if pend is not None:
                x = _moe_combine(x, pend[0], mod, pend[1])
            u = _hyena_in(x, mod, i, norm_g[i, 0], hy_w_in[j].astype(BF16), hy_b_in[j], hy_conv_w[j], hy_conv_b[j])
            w_out = hy_w_out[j].astype(BF16)
            for L, nb, off, _ in groups:
                fwd, ff, inv = _dft_mats(L)
                khat = _hyena_filters(L, ff, hy_f_w1[j], hy_f_b1[j], hy_f_freq[j], hy_f_w2[j], hy_f_b2[j],
                                      hy_f_w3[j])
                z = _hyena_conv(u, khat, hy_skip[j], fwd, inv, L, nb, off, D if L == CTX_L else 512)
                x = _outproj(x, z, w_out, hy_b_out[j], mod, i, off, nb * L)
        elif kind == 1:
            w_cat = jnp.concatenate([gla_w_q[j], gla_w_k[j], gla_w_v[j], gla_w_g[j]], axis=1).astype(BF16)
            nk = GLA_H * GLA_DK
            wg1 = jnp.zeros((D, GK1_PAD), F32).at[:, :GLA_RANK].set(gla_w_gk1[j, 0])
            wg1 = wg1.at[:, GLA_RANK:2 * GLA_RANK].set(gla_w_gk1[j, 1]).astype(BF16)
            wg2 = jnp.zeros((GK1_PAD, 2 * nk), F32).at[:GLA_RANK, :nk].set(gla_w_gk2[j, 0])
            wg2 = wg2.at[GLA_RANK:2 * GLA_RANK, nk:].set(gla_w_gk2[j, 1]).astype(BF16)
            proj = _gla_proj(x, mod, i, norm_g[i, 0], w_cat, wg1, wg2, gla_b_gk[j].reshape(1, 2 * nk), pend)
            lower = np.tril(np.ones((GLA_CHUNK, GLA_CHUNK)))
            w_o = gla_w_o[j].astype(BF16)
            for L, nb, off, grid_rows in groups:
                eye = np.eye(L // GLA_CHUNK)
                tri = jnp.asarray(np.stack([np.kron(eye, lower), np.kron(eye, lower.T)]), BF16)
                s0 = None if grid_rows is None else state_gla[:, j]
                o, s_fin = _gla_core(proj, tri, gla_norm_g[j], s0, L, nb, off, GLA_H if L == CTX_L else 1)
                if grid_rows is None:
                    new_states.append(s_fin)
                x = _outproj(x, o, w_o, jnp.zeros((D,), F32), mod, i, off, nb * L, pend)
        elif kind == 2:
            w_out = fn_w_out[j].astype(BF16)
            for L, nb, off, _ in groups:
                chan, seq = _fnet_mats(L)
                x = _fnet(x, mod, i, norm_g[i, 0], chan, seq, w_out, fn_b_out[j], L, nb, off, pend)
        else:
            w_pool = pool_w[j].astype(BF16)
            for L, nb, off, grid_rows in groups:
                mats, inv_cnt = _pool_mats(L, grid_rows)
                x = _pool(x, mod, i, norm_g[i, 0], mats, inv_cnt, w_pool, pool_b[j], pool_scale[j], L, nb, off,
                          pend)

        out = _moe(x, mod, i, norm_g[i, 1], moe_w_rg[i], moe_w_re[i], moe_w1, moe_w3, moe_w2, tri_tm,
                   final_g if i == DEPTH - 1 else None)
        pend = (out, i)

    y_prompt, y_sample = out
    new_state_gla = jnp.stack(new_states, axis=1)
    return (y_prompt.reshape(CTX_B, CTX_L, D), y_sample.reshape(LAT_B, LAT_L, D), new_state_gla)
```

```python
import functools
import math

import jax
import jax.numpy as jnp
import numpy as np
from jax import lax
from jax.experimental import pallas as pl
from jax.experimental.pallas import tpu as pltpu

F32 = jnp.float32
BF16 = jnp.bfloat16

D = 1024
CTX_B, CTX_L = 32, 256
LAT_B, LAT_L = 2, 1024
N_CTX = CTX_B * CTX_L
N_LAT = LAT_B * LAT_L
N_TOK = N_CTX + N_LAT
DEPTH = 4
GRID_W = 64
EPS = 1e-6

HY_BANDS = 8
HY_EMB = 1 + 2 * HY_BANDS
HY_EMB_PAD = 32
HY_HID = 64
HY_FAST_DECAY = 0.3
HY_SLOW_DECAY = 1.5
HY_DECAY_TARGET = 1e-2

GLA_H = 4
GLA_DK = 128
GLA_DV = 256
GLA_RANK = 16
GLA_NORMALIZER = 16.0
GLA_CHUNK = 64

FNET_GROUPS = 4
FNET_C = D // FNET_GROUPS
POOL_WINDOWS = (2, 4, 8, 16)
POOL_G = D // len(POOL_WINDOWS)

MOE_GROUPS = 4
MOE_PER_GROUP = 4
MOE_E = MOE_GROUPS * MOE_PER_GROUP
MOE_HID = D // 2

MOD_ROWS = 8
TM = 512
TM_BIG = 1024
VMEM_LIMIT = 56 * 1024 * 1024


def _cparams(n_axes):
    return pltpu.CompilerParams(dimension_semantics=("arbitrary",) * n_axes, vmem_limit_bytes=VMEM_LIMIT)


def _norm_mod(x, g, sc, sh):
    ms = jnp.mean(x * x, axis=-1, keepdims=True)
    return (x * lax.rsqrt(ms + EPS) * g) * (1.0 + sc) + sh


def _split(a):
    hi = a.astype(BF16)
    lo = (a - hi.astype(F32)).astype(BF16)
    return hi, lo


def _dot(a, b):
    return jnp.dot(a, b, preferred_element_type=F32)


def _dot_precise(a, b):
    a_hi, a_lo = _split(a)
    b_hi, b_lo = _split(b)
    return _dot(a_hi, b_hi) + (_dot(a_hi, b_lo) + _dot(a_lo, b_hi))


def _silu(x):
    return x * (1.0 / (1.0 + jnp.exp(-x)))


def _log_sigmoid(x):
    return jnp.minimum(x, 0.0) - jnp.log(1.0 + jnp.exp(-jnp.abs(x)))


def _mod_spec(layer, chunk, row_fn):
    base = layer * MOD_ROWS * 6 + chunk

    def index_map(*ids):
        return (base + row_fn(*ids) * 6, 0, 0)

    return pl.BlockSpec((None, 1, D), index_map)


def _row_tm(t, *_):
    return jnp.where(t < N_CTX // TM, 0, 1 + (t - N_CTX // TM) // (LAT_L // TM))


def _row_big(t, *_):
    return jnp.where(t < N_CTX // TM_BIG, 0, 1 + (t - N_CTX // TM_BIG) // (LAT_L // TM_BIG))


def _vec_spec(n):
    return pl.BlockSpec((1, n), lambda *ids: (0, 0))


def _full_spec(shape):
    nd = len(shape)
    return pl.BlockSpec(shape, lambda *ids: (0,) * nd)


def _ada_kernel(cond_ref, w_ref, b_ref, o_ref):
    s = _silu(cond_ref[...]).astype(BF16)
    o_ref[...] = _dot(s, w_ref[...].astype(BF16)) + b_ref[...]


def _ada_table(cond, w_ada, b_ada):
    tn = 1536
    return pl.pallas_call(
        _ada_kernel,
        grid=(DEPTH, 6 * D // tn),
        in_specs=[
            pl.BlockSpec((MOD_ROWS, D), lambda i, j: (0, 0)),
            pl.BlockSpec((None, D, tn), lambda i, j: (i, 0, j)),
            pl.BlockSpec((None, 1, tn), lambda i, j: (i, 0, j)),
        ],
        out_specs=pl.BlockSpec((None, MOD_ROWS, tn), lambda i, j: (i, 0, j)),
        out_shape=jax.ShapeDtypeStruct((DEPTH, MOD_ROWS, 6 * D), F32),
        compiler_params=_cparams(2),
        name="ada_table",
    )(cond, w_ada, b_ada.reshape(DEPTH, 1, 6 * D))


def _pending_specs(pend, mod, block_rows, row_index, row_fn):
    if pend is None:
        return [], []
    y, prev_layer = pend
    return [pl.BlockSpec((block_rows, D), row_index), _mod_spec(prev_layer, 5, row_fn)], [y, mod]


def _read_x(x_ref, pending_refs):
    x = x_ref[...]
    if pending_refs:
        y_ref, gate_ref = pending_refs
        x = x + gate_ref[...] * y_ref[...]
    return x


def _outproj_kernel(z_ref, w_ref, b_ref, gate_ref, x_ref, *rest):
    *pending, o_ref = rest
    y = _dot(z_ref[...], w_ref[...]) + b_ref[...]
    o_ref[...] = _read_x(x_ref, pending) + gate_ref[...] * y


def _outproj(x, z, w_bf16, bias, mod, layer, row_off, n_rows, pend=None):
    k = z.shape[1]
    off = row_off // TM
    first_lat = N_CTX // TM

    def row_fn(t):
        g = t + off
        return jnp.where(g < first_lat, 0, 1 + (g - first_lat) // (LAT_L // TM))

    p_specs, p_args = _pending_specs(pend, mod, TM, lambda t: (t + off, 0), row_fn)
    return pl.pallas_call(
        _outproj_kernel,
        grid=(n_rows // TM,),
        in_specs=[
            pl.BlockSpec((TM, k), lambda t: (t, 0)),
            _full_spec((k, D)),
            _vec_spec(D),
            _mod_spec(layer, 2, row_fn),
            pl.BlockSpec((TM, D), lambda t: (t + off, 0)),
        ] + p_specs,
        out_specs=pl.BlockSpec((TM, D), lambda t: (t + off, 0)),
        out_shape=jax.ShapeDtypeStruct((N_TOK, D), F32),
        input_output_aliases={4: 0},
        compiler_params=_cparams(1),
        name="outproj_residual",
    )(z, w_bf16, bias.reshape(1, D), mod, x, *p_args)


def _dft_mats(L):
    n2 = 2 * L
    k = np.arange(L)[:, None].astype(np.float64)
    n = np.arange(n2)[None, :].astype(np.float64)
    ang = 2.0 * np.pi * k * n / n2
    full = np.concatenate([np.cos(ang), -np.sin(ang)], axis=0)
    full[L, :] = np.cos(np.pi * np.arange(n2))
    fwd = full[:, :L]
    bwd = np.zeros((n2, L))
    bwd[:, 1:] = full[:, n2 - np.arange(1, L)]
    t = np.arange(L)[:, None].astype(np.float64)
    kk = np.arange(L)[None, :].astype(np.float64)
    ang_i = 2.0 * np.pi * t * kk / n2
    inv_re = np.cos(ang_i) / L
    inv_re[:, 0] = 1.0 / n2
    inv_im = -np.sin(ang_i) / L
    inv_im[:, 0] = np.cos(np.pi * np.arange(L)) / n2
    inv = np.concatenate([inv_re, inv_im], axis=1)
    return tuple(jnp.asarray(m, F32).astype(BF16) for m in (fwd, np.concatenate([fwd, bwd], axis=1), inv))


def _hyena_pos_emb(L):
    pos = np.arange(L, dtype=np.float64)
    bands = np.linspace(1e-4, HY_BANDS - 1, HY_BANDS)
    ang = (2.0 * np.pi * pos / L)[:, None] * bands[None, :]
    z = np.concatenate([(pos / L)[:, None], np.cos(ang), -np.sin(ang)], axis=-1)
    zp = np.zeros((L, HY_EMB_PAD))
    zp[:, :HY_EMB] = z
    return jnp.asarray(zp, F32)


def _hyena_filter_kernel(z_ref, w1_ref, b1_ref, fr_ref, w2_ref, b2_ref, w3f_ref, w3b_ref, ff_ref, o_ref, *, L, tn):
    j = pl.program_id(1)
    fr = fr_ref[...]
    f = jnp.sin(fr * (_dot_precise(z_ref[...], w1_ref[...]) + b1_ref[...]))
    f = jnp.sin(fr * (_dot_precise(f, w2_ref[...]) + b2_ref[...]))
    t_lin = lax.broadcasted_iota(jnp.int32, (L, tn), 0).astype(F32) / float(L - 1)
    ch = (lax.broadcasted_iota(jnp.int32, (L, tn), 1) + j * tn).astype(F32)
    max_decay = math.log(HY_DECAY_TARGET) / HY_FAST_DECAY
    min_decay = math.log(HY_DECAY_TARGET) / HY_SLOW_DECAY
    deltas = min_decay + ch * ((max_decay - min_decay) / float(D - 1))
    window = jnp.exp(-t_lin * jnp.abs(deltas))
    kf = _dot_precise(f, w3f_ref[...]) * window
    kb = _dot_precise(f, w3b_ref[...]) * window
    taps = jnp.concatenate([kf, kb], axis=0).astype(BF16)
    o_ref[...] = _dot(ff_ref[...], taps)


def _hyena_filters(L, ff, f_w1, f_b1, f_freq, f_w2, f_b2, f_w3):
    tn = 512
    nj = D // tn
    w1p = jnp.zeros((HY_EMB_PAD, HY_HID), F32).at[:HY_EMB].set(f_w1)
    kern = functools.partial(_hyena_filter_kernel, L=L, tn=tn)
    return pl.pallas_call(
        kern,
        grid=(2, nj),
        in_specs=[
            _full_spec((L, HY_EMB_PAD)),
            _full_spec((HY_EMB_PAD, HY_HID)),
            _vec_spec(HY_HID),
            _vec_spec(HY_HID),
            _full_spec((HY_HID, HY_HID)),
            _vec_spec(HY_HID),
            pl.BlockSpec((HY_HID, tn), lambda o, j: (0, o * nj + j)),
            pl.BlockSpec((HY_HID, tn), lambda o, j: (0, (2 + o) * nj + j)),
            _full_spec((2 * L, 2 * L)),
        ],
        out_specs=pl.BlockSpec((None, 2 * L, tn), lambda o, j: (o, 0, j)),
        out_shape=jax.ShapeDtypeStruct((2, 2 * L, D), F32),
        compiler_params=_cparams(2),
        name=f"hyena_filters_L{L}",
    )(_hyena_pos_emb(L), w1p, f_b1.reshape(1, -1), f_freq.reshape(1, -1), f_w2, f_b2.reshape(1, -1),
      f_w3, f_w3, ff)


def _hyena_in_kernel(x_ref, g_ref, sc_ref, sh_ref, w_ref, b_ref, cw_ref, cb_ref, o_ref, h_scr):
    t = pl.program_id(0)

    @pl.when(pl.program_id(1) == 0)
    def _():
        h_scr[...] = _norm_mod(x_ref[...], g_ref[...], sc_ref[...], sh_ref[...]).astype(BF16)

    u = _dot(h_scr[...], w_ref[...]) + b_ref[...]
    cw = cw_ref[...]
    o_ref[...] = pltpu.roll(u, 1, 0) * cw[0:1] + u * cw[1:2] + pltpu.roll(u, TM_BIG - 1, 0) * cw[2:3] + cb_ref[...]
    is_ctx = (t < N_CTX // TM_BIG).astype(F32)
    for start in range(0, TM_BIG, CTX_L):
        f = 1.0 if start % LAT_L == 0 else is_ctx
        before = (start - 1) % TM_BIG
        o_ref[start:start + 1, :] = o_ref[start:start + 1, :] - f * (u[before:before + 1] * cw[0:1])
        end = start + CTX_L - 1
        g = 1.0 if (end + 1) % LAT_L == 0 else is_ctx
        after = (end + 1) % TM_BIG
        o_ref[end:end + 1, :] = o_ref[end:end + 1, :] - g * (u[after:after + 1] * cw[2:3])


def _hyena_in(x, mod, layer, norm_g, w_in_bf16, b_in, conv_w, conv_b):
    return pl.pallas_call(
        _hyena_in_kernel,
        grid=(N_TOK // TM_BIG, 3),
        in_specs=[
            pl.BlockSpec((TM_BIG, D), lambda t, p: (t, 0)),
            _vec_spec(D),
            _mod_spec(layer, 1, _row_big),
            _mod_spec(layer, 0, _row_big),
            pl.BlockSpec((D, D), lambda t, p: (0, p)),
            pl.BlockSpec((1, D), lambda t, p: (0, p)),
            pl.BlockSpec((3, D), lambda t, p: (0, p)),
            pl.BlockSpec((1, D), lambda t, p: (0, p)),
        ],
        out_specs=pl.BlockSpec((TM_BIG, D), lambda t, p: (t, p)),
        out_shape=jax.ShapeDtypeStruct((N_TOK, 3 * D), F32),
        scratch_shapes=[pltpu.VMEM((TM_BIG, D), BF16)],
        compiler_params=_cparams(2),
        name="hyena_in",
    )(x, norm_g.reshape(1, D), mod, mod, w_in_bf16, b_in.reshape(1, -1), conv_w, conv_b.reshape(1, -1))


def _hyena_conv_kernel(v_ref, x1_ref, x2_ref, kh_ref, skip_ref, fwd_ref, inv_ref, o_ref, *, L):
    fwd = fwd_ref[...]
    inv = inv_ref[...]
    row0 = lax.broadcasted_iota(jnp.int32, (L, v_ref.shape[1]), 0) == 0

    def long_conv(z, order):
        zh = _dot(fwd, z.astype(BF16))
        zr, zi = zh[:L], zh[L:]
        kr, ki = kh_ref[order, :L, :], kh_ref[order, L:, :]
        pr = jnp.where(row0, zr * kr, zr * kr - zi * ki)
        pi = jnp.where(row0, zi * ki, zr * ki + zi * kr)
        prod = jnp.concatenate([pr, pi], axis=0).astype(BF16)
        return _dot(inv, prod) + z * skip_ref[order:order + 1, :]

    z = x1_ref[...] * long_conv(v_ref[...], 0)
    z = x2_ref[...] * long_conv(z, 1)
    o_ref[...] = z.astype(BF16)


def _hyena_conv(u, khat, skip, fwd, inv, L, n_batch, row_off, tn):
    nj = D // tn
    rb = row_off // L
    kern = functools.partial(_hyena_conv_kernel, L=L)
    return pl.pallas_call(
        kern,
        grid=(nj, n_batch),
        in_specs=[
            pl.BlockSpec((L, tn), lambda j, b: (rb + b, j)),
            pl.BlockSpec((L, tn), lambda j, b: (rb + b, nj + j)),
            pl.BlockSpec((L, tn), lambda j, b: (rb + b, 2 * nj + j)),
            pl.BlockSpec((2, 2 * L, tn), lambda j, b: (0, 0, j)),
            pl.BlockSpec((2, tn), lambda j, b: (0, j)),
            _full_spec((2 * L, L)),
            _full_spec((L, 2 * L)),
        ],
        out_specs=pl.BlockSpec((L, tn), lambda j, b: (b, j)),
        out_shape=jax.ShapeDtypeStruct((n_batch * L, D), BF16),
        compiler_params=_cparams(2),
        name=f"hyena_conv_L{L}",
    )(u, u, u, khat, skip, fwd, inv)


GLA_PROJ = 2 * GLA_H * GLA_DK + 2 * GLA_H * GLA_DV
GLA_COLS = GLA_PROJ + 2 * GLA_H * GLA_DK
GK1_PAD = 128


def _gla_proj_kernel(x_ref, g_ref, sc_ref, sh_ref, w_ref, wg1_ref, wg2_ref, bg_ref, *rest):
    *pending, o_ref = rest
    h = _norm_mod(_read_x(x_ref, pending), g_ref[...], sc_ref[...], sh_ref[...]).astype(BF16)
    p = _dot(h, w_ref[...])
    nq = GLA_H * GLA_DK
    o_ref[:, 0:nq] = p[:, 0:nq] * (GLA_DK ** -0.5)
    o_ref[:, nq:nq + nq + GLA_H * GLA_DV] = p[:, nq:nq + nq + GLA_H * GLA_DV]
    o_ref[:, 2 * nq + GLA_H * GLA_DV:GLA_PROJ] = _silu(p[:, 2 * nq + GLA_H * GLA_DV:GLA_PROJ])
    low = _dot(h, wg1_ref[...]).astype(BF16)
    gk = _dot(low, wg2_ref[...]) + bg_ref[...]
    o_ref[:, GLA_PROJ:GLA_COLS] = _log_sigmoid(gk) / GLA_NORMALIZER


def _gla_proj(x, mod, layer, norm_g, w_cat, wg1, wg2, bg, pend=None):
    p_specs, p_args = _pending_specs(pend, mod, TM, lambda t: (t, 0), _row_tm)
    return pl.pallas_call(
        _gla_proj_kernel,
        grid=(N_TOK // TM,),
        in_specs=[
            pl.BlockSpec((TM, D), lambda t: (t, 0)),
            _vec_spec(D),
            _mod_spec(layer, 1, _row_tm),
            _mod_spec(layer, 0, _row_tm),
            _full_spec((D, GLA_PROJ)),
            _full_spec((D, GK1_PAD)),
            _full_spec((GK1_PAD, 2 * GLA_H * GLA_DK)),
            _vec_spec(2 * GLA_H * GLA_DK),
        ] + p_specs,
        out_specs=pl.BlockSpec((TM, GLA_COLS), lambda t: (t, 0)),
        out_shape=jax.ShapeDtypeStruct((N_TOK, GLA_COLS), F32),
        compiler_params=_cparams(1),
        name="gla_proj",
    )(x, norm_g.reshape(1, D), mod, mod, w_cat, wg1, wg2, bg, *p_args)


def _gla_core_kernel(*refs, L, has_s0, hps):
    if has_s0:
        q_ref, k_ref, v_ref, g_ref, gkf_ref, gkb_ref, tri_ref, ng_ref, s0_ref, o_ref, sf_ref, acc = refs
    else:
        q_ref, k_ref, v_ref, g_ref, gkf_ref, gkb_ref, tri_ref, ng_ref, o_ref, sf_ref, acc = refs
        s0_ref = None
    C = GLA_CHUNK
    n = L // C
    ri = lax.broadcasted_iota(jnp.int32, (C, C), 0)
    ci = lax.broadcasted_iota(jnp.int32, (C, C), 1)
    nt_dims = (((1,), (1,)), ((), ()))
    tn_dims = (((0,), (0,)), ((), ()))

    for hh in range(hps):
        kc = slice(hh * GLA_DK, (hh + 1) * GLA_DK)
        vc = slice(hh * GLA_DV, (hh + 1) * GLA_DV)
        for direction, gk_ref in enumerate((gkf_ref, gkb_ref)):
            keep = (ci <= ri) if direction == 0 else (ci >= ri)
            last = C - 1 if direction == 0 else 0
            gk_hi, gk_lo = _split(gk_ref[:, kc])
            b_all = _dot(tri_ref[direction], jnp.concatenate([gk_hi, gk_lo], axis=1))
            b_all = b_all[:, :GLA_DK] + b_all[:, GLA_DK:]
            st = s0_ref[direction, hh].T if has_s0 else jnp.zeros((GLA_DV, GLA_DK), F32)
            order = range(n) if direction == 0 else range(n - 1, -1, -1)
            for c in order:
                rows = slice(c * C, (c + 1) * C)
                b = b_all[rows]
                b_last = b[last:last + 1, :]
                q = q_ref[rows, kc]
                k = k_ref[rows, kc]
                v = v_ref[rows, vc].astype(BF16)
                qe = (q * jnp.exp(b)).astype(BF16)
                ke = (k * jnp.exp(-b)).astype(BF16)
                kd = (k * jnp.exp(b_last - b)).astype(BF16)
                scores = lax.dot_general(qe, ke, nt_dims, preferred_element_type=F32)
                scores = jnp.where(keep, scores, 0.0).astype(BF16)
                o = _dot(scores, v) + lax.dot_general(qe, st.astype(BF16), nt_dims, preferred_element_type=F32)
                if direction == 0:
                    acc[rows, vc] = o
                else:
                    acc[rows, vc] = acc[rows, vc] + o
                st = jnp.exp(b_last) * st + lax.dot_general(v, kd, tn_dims, preferred_element_type=F32)
            sf_ref[direction, hh] = st.T

        o = acc[:, vc]
        o = o * lax.rsqrt(jnp.mean(o * o, axis=-1, keepdims=True) + EPS) * ng_ref[...]
        o_ref[:, vc] = (o * g_ref[:, vc]).astype(BF16)


def _gla_core(proj, tri, norm_g, s0, L, n_batch, row_off, hps):
    rb = row_off // L
    H = GLA_H
    nh = H // hps
    has_s0 = s0 is not None
    kern = functools.partial(_gla_core_kernel, L=L, has_s0=has_s0, hps=hps)
    kb, vb = GLA_DK * hps, GLA_DV * hps
    in_specs = [
        pl.BlockSpec((L, kb), lambda b, h: (rb + b, h)),
        pl.BlockSpec((L, kb), lambda b, h: (rb + b, nh + h)),
        pl.BlockSpec((L, vb), lambda b, h: (rb + b, (2 * H * GLA_DK) // vb + h)),
        pl.BlockSpec((L, vb), lambda b, h: (rb + b, (2 * H * GLA_DK) // vb + nh + h)),
        pl.BlockSpec((L, kb), lambda b, h: (rb + b, GLA_PROJ // kb + h)),
        pl.BlockSpec((L, kb), lambda b, h: (rb + b, GLA_PROJ // kb + nh + h)),
        _full_spec((2, L, L)),
        _vec_spec(GLA_DV),
    ]
    args = [proj] * 6 + [tri, norm_g.reshape(1, GLA_DV)]
    state_spec = pl.BlockSpec((None, 2, hps, GLA_DK, GLA_DV), lambda b, h: (b, 0, h, 0, 0))
    if has_s0:
        in_specs.append(state_spec)
        args.append(s0)
    return pl.pallas_call(
        kern,
        grid=(n_batch, nh),
        in_specs=in_specs,
        out_specs=[pl.BlockSpec((L, vb), lambda b, h: (b, h)), state_spec],
        out_shape=[
            jax.ShapeDtypeStruct((n_batch * L, H * GLA_DV), BF16),
            jax.ShapeDtypeStruct((n_batch, 2, H, GLA_DK, GLA_DV), F32),
        ],
        scratch_shapes=[pltpu.VMEM((L, vb), F32)],
        compiler_params=_cparams(2),
        name=f"gla_core_L{L}",
    )(*args)


def _fnet_mats(L):
    c = np.arange(FNET_C)
    ang_c = 2.0 * np.pi * np.outer(c, c) / FNET_C
    chan = np.concatenate([np.cos(ang_c), np.sin(ang_c)], axis=1) / math.sqrt(FNET_C)
    t = np.arange(L)
    ang_l = 2.0 * np.pi * np.outer(t, t) / L
    seq = np.concatenate([np.cos(ang_l), -np.sin(ang_l)], axis=1) / math.sqrt(L)
    return jnp.asarray(chan, F32).astype(BF16), jnp.asarray(seq, F32).astype(BF16)


def _fnet_kernel(x_ref, g_ref, sc_ref, sh_ref, gate_ref, chan_ref, seq_ref, w_ref, b_ref, *rest):
    *pending, o_ref = rest
    x = _read_x(x_ref, pending)
    h = _norm_mod(x, g_ref[...], sc_ref[...], sh_ref[...]).astype(BF16)
    chan = chan_ref[...]
    cos_parts, sin_parts = [], []
    for gi in range(FNET_GROUPS):
        cs = _dot(h[:, gi * FNET_C:(gi + 1) * FNET_C], chan)
        cos_parts.append(cs[:, :FNET_C])
        sin_parts.append(cs[:, FNET_C:])
    stacked = jnp.concatenate([jnp.concatenate(cos_parts, axis=1), jnp.concatenate(sin_parts, axis=1)], axis=0)
    mixed = _dot(seq_ref[...], stacked.astype(BF16))
    y = _dot(mixed.astype(BF16), w_ref[...]) + b_ref[...]
    o_ref[...] = x + gate_ref[...] * y


def _fnet(x, mod, layer, norm_g, chan, seq, w_bf16, bias, L, n_batch, row_off, pend=None):
    rb = row_off // L
    lat = row_off > 0

    def row_fn(b):
        return 1 + b if lat else 0

    p_specs, p_args = _pending_specs(pend, mod, L, lambda b: (rb + b, 0), row_fn)
    return pl.pallas_call(
        _fnet_kernel,
        grid=(n_batch,),
        in_specs=[
            pl.BlockSpec((L, D), lambda b: (rb + b, 0)),
            _vec_spec(D),
            _mod_spec(layer, 1, row_fn),
            _mod_spec(layer, 0, row_fn),
            _mod_spec(layer, 2, row_fn),
            _full_spec((FNET_C, 2 * FNET_C)),
            _full_spec((L, 2 * L)),
            _full_spec((D, D)),
            _vec_spec(D),
        ] + p_specs,
        out_specs=pl.BlockSpec((L, D), lambda b: (rb + b, 0)),
        out_shape=jax.ShapeDtypeStruct((N_TOK, D), F32),
        input_output_aliases={0: 0},
        compiler_params=_cparams(1),
        name=f"fnet_L{L}",
    )(x, norm_g.reshape(1, D), mod, mod, mod, chan, seq, w_bf16, bias.reshape(1, D), *p_args)


def _window_bounds(n, k):
    t = np.arange(n)
    lo, hi = k // 2, k - k // 2 - 1
    return np.maximum(t - lo, 0), np.minimum(t + hi + 1, n)


def _pool_mats(L, grid_rows):
    mats, inv = [], []
    for k in POOL_WINDOWS:
        if grid_rows is None:
            s, e = _window_bounds(L, k)
            idx = np.arange(L)[None, :]
            m = ((idx >= s[:, None]) & (idx < e[:, None])).astype(np.float64)
            cnt = (e - s).astype(np.float64)
        else:
            sr, er = _window_bounds(grid_rows, k)
            sc, ec = _window_bounds(GRID_W, k)
            ir = np.arange(grid_rows)[None, :]
            ic = np.arange(GRID_W)[None, :]
            mr = ((ir >= sr[:, None]) & (ir < er[:, None])).astype(np.float64)
            mc = ((ic >= sc[:, None]) & (ic < ec[:, None])).astype(np.float64)
            m = np.kron(mr, mc)
            cnt = np.kron((er - sr).astype(np.float64), (ec - sc).astype(np.float64))
        mats.append(m)
        inv.append(1.0 / cnt)
    return jnp.asarray(np.stack(mats), BF16), jnp.asarray(np.stack(inv)[:, :, None], F32)


def _pool_kernel(x_ref, g_ref, sc_ref, sh_ref, gate_ref, m_ref, ic_ref, w_ref, b_ref, ps_ref, *rest):
    *pending, o_ref = rest
    x = _read_x(x_ref, pending)
    h = _norm_mod(x, g_ref[...], sc_ref[...], sh_ref[...])
    outs = []
    for gi in range(len(POOL_WINDOWS)):
        hg = h[:, gi * POOL_G:(gi + 1) * POOL_G]
        hi, lo = _split(hg)
        m = m_ref[gi]
        mean = (_dot(m, hi) + _dot(m, lo)) * ic_ref[gi]
        outs.append(_dot((mean - hg).astype(BF16), w_ref[gi]))
    y = (jnp.concatenate(outs, axis=1) + b_ref[...]) * ps_ref[...]
    o_ref[...] = x + gate_ref[...] * y


def _pool(x, mod, layer, norm_g, mats, inv_cnt, w_bf16, bias, scale, L, n_batch, row_off, pend=None):
    rb = row_off // L
    lat = row_off > 0
    G = len(POOL_WINDOWS)

    def row_fn(b):
        return 1 + b if lat else 0

    p_specs, p_args = _pending_specs(pend, mod, L, lambda b: (rb + b, 0), row_fn)
    return pl.pallas_call(
        _pool_kernel,
        grid=(n_batch,),
        in_specs=[
            pl.BlockSpec((L, D), lambda b: (rb + b, 0)),
            _vec_spec(D),
            _mod_spec(layer, 1, row_fn),
            _mod_spec(layer, 0, row_fn),
            _mod_spec(layer, 2, row_fn),
            _full_spec((G, L, L)),
            _full_spec((G, L, 1)),
            _full_spec((G, POOL_G, POOL_G)),
            _vec_spec(D),
            _vec_spec(D),
        ] + p_specs,
        out_specs=pl.BlockSpec((L, D), lambda b: (rb + b, 0)),
        out_shape=jax.ShapeDtypeStruct((N_TOK, D), F32),
        input_output_aliases={0: 0},
        compiler_params=_cparams(1),
        name=f"pool_L{L}",
    )(x, norm_g.reshape(1, D), mod, mod, mod, mats, inv_cnt, w_bf16, bias.reshape(1, D), scale.reshape(1, D),
      *p_args)


ROUTER_PAD = 128
LANES = 128
D_EXT = D + LANES
MOE_TILE = 1024
MOE_MAX_TILES = N_TOK // MOE_TILE + MOE_GROUPS
MOE_ROWS = MOE_MAX_TILES * MOE_TILE
MOE_TILE_SHIFT = MOE_TILE.bit_length() - 1
assert 1 << MOE_TILE_SHIFT == MOE_TILE
MOE_ROW_BLOCK = 256
MOE_DMA_CHUNK = 64
MOE_CHUNK_SHIFT = MOE_DMA_CHUNK.bit_length() - 1
assert 1 << MOE_CHUNK_SHIFT == MOE_DMA_CHUNK and MOE_MAX_TILES % 2 == 0
MOE_Y_ROWS = N_TOK + 2 * MOE_DMA_CHUNK


ROUTE_ROWS = 8


def _moe_route_kernel(x_ref, g_ref, sc_ref, sh_ref, wr_ref, tri_ref, h3_ref, route_ref, cnt_ref, carry):
    t = pl.program_id(0)
    refs = (x_ref, g_ref, sc_ref, sh_ref, wr_ref, tri_ref, h3_ref, route_ref, cnt_ref, carry)
    pl.when(t < N_TOK // TM)(functools.partial(_moe_route_tile, t, *refs))

    @pl.when(t == N_TOK // TM)
    def _():
        h3_ref[...] = jnp.zeros_like(h3_ref)


def _moe_route_tile(t, x_ref, g_ref, sc_ref, sh_ref, wr_ref, tri_ref, h3_ref, route_ref, cnt_ref, carry):
    @pl.when(t == 0)
    def _():
        carry[...] = jnp.zeros_like(carry)

    h = _norm_mod(x_ref[...], g_ref[...], sc_ref[...], sh_ref[...])
    w_hi, w_lo = _split(wr_ref[...])
    h_hi, h_lo = _split(h)
    nt = (((1,), (1,)), ((), ()))
    logits = (lax.dot_general(w_hi, h_hi, nt, preferred_element_type=F32)
              + (lax.dot_general(w_hi, h_lo, nt, preferred_element_type=F32)
                 + lax.dot_general(w_lo, h_hi, nt, preferred_element_type=F32)))
    neg = jnp.float32(-jnp.inf)
    r8 = lax.broadcasted_iota(jnp.int32, (ROUTE_ROWS, TM), 0)
    r16 = lax.broadcasted_iota(jnp.int32, (MOE_E, TM), 0)
    gl = jnp.where(r8 < MOE_GROUPS, logits[MOE_E:MOE_E + ROUTE_ROWS], neg)
    g_max = jnp.max(gl, axis=0, keepdims=True)
    g_idx = jnp.min(jnp.where(gl == g_max, r8, ROUTE_ROWS), axis=0, keepdims=True)
    p_grp = 1.0 / jnp.sum(jnp.exp(gl - g_max), axis=0, keepdims=True)
    in_grp = (r16 >> 2) == g_idx
    el = jnp.where(in_grp, logits[:MOE_E], neg)
    m1 = jnp.max(el, axis=0, keepdims=True)
    i1 = jnp.min(jnp.where(el == m1, r16, MOE_E), axis=0, keepdims=True)
    z = jnp.sum(jnp.exp(el - m1), axis=0, keepdims=True)
    el2 = jnp.where(r16 == i1, neg, el)
    m2 = jnp.max(el2, axis=0, keepdims=True)
    i2 = jnp.min(jnp.where(el2 == m2, r16, MOE_E), axis=0, keepdims=True)
    p1 = 1.0 / z
    p2 = jnp.exp(m2 - m1) / z
    tot = p1 + p2
    eid = r8 + MOE_PER_GROUP * g_idx
    in4 = r8 < MOE_PER_GROUP
    cw4 = (jnp.where(in4 & (eid == i1), p_grp * (p1 / tot), 0.0)
           + jnp.where(in4 & (eid == i2), p_grp * (p2 / tot), 0.0))
    member = jnp.where(r8 == g_idx, 1.0, 0.0)
    before = _dot(member.astype(BF16), tri_ref[...]) + carry[:, 0:1]
    rank = jnp.sum(jnp.where(r8 == g_idx, before, 0.0), axis=0, keepdims=True)
    carry[...] = carry[...] + jnp.sum(member, axis=1, keepdims=True)
    cnt_ref[...] = carry[...].astype(jnp.int32)
    route_ref[...] = jnp.where(r8 == 0, g_idx, jnp.where(r8 == 1, rank.astype(jnp.int32), 0))
    h3_ref[:, :D] = h
    cw_rows = jnp.concatenate([cw4, jnp.zeros((LANES - ROUTE_ROWS, TM), F32)], axis=0)
    h3_ref[:, D:] = cw_rows.T


def _moe_route(x, mod, layer, norm_g, w_router_t, tri):
    nt = N_TOK // TM

    def tok_tile(t):
        return jnp.minimum(t, nt - 1)

    return pl.pallas_call(
        _moe_route_kernel,
        grid=(nt + 1,),
        in_specs=[
            pl.BlockSpec((TM, D), lambda t: (tok_tile(t), 0)),
            _vec_spec(D),
            _mod_spec(layer, 4, lambda t: _row_tm(tok_tile(t))),
            _mod_spec(layer, 3, lambda t: _row_tm(tok_tile(t))),
            _full_spec((ROUTER_PAD, D)),
            _full_spec((TM, TM)),
        ],
        out_specs=[
            pl.BlockSpec((TM, D_EXT), lambda t: (t, 0)),
            pl.BlockSpec((ROUTE_ROWS, TM), lambda t: (0, tok_tile(t))),
            pl.BlockSpec((ROUTE_ROWS, LANES), lambda t: (0, 0)),
        ],
        out_shape=[
            jax.ShapeDtypeStruct((N_TOK + TM, D_EXT), F32),
            jax.ShapeDtypeStruct((ROUTE_ROWS, N_TOK), jnp.int32),
            jax.ShapeDtypeStruct((ROUTE_ROWS, LANES), jnp.int32),
        ],
        scratch_shapes=[pltpu.VMEM((ROUTE_ROWS, LANES), F32)],
        compiler_params=_cparams(1),
        name="moe_route",
    )(x, norm_g.reshape(1, D), mod, mod, w_router_t, tri)


def _moe_invert_kernel(pos_ref, src_ref):
    for parity in (0, 1):
        def mark(j, carry, parity=parity):
            tile = 2 * lax.shift_right_logical(j, jnp.int32(MOE_TILE_SHIFT - MOE_CHUNK_SHIFT)) + parity
            base = tile * MOE_TILE + (j & (MOE_TILE // MOE_DMA_CHUNK - 1)) * MOE_DMA_CHUNK
            for rr in range(MOE_DMA_CHUNK):
                src_ref[base + rr] = jnp.int32(N_TOK + rr + MOE_DMA_CHUNK * parity)
            return carry

        lax.fori_loop(0, (MOE_MAX_TILES // 2) * (MOE_TILE // MOE_DMA_CHUNK), mark, 0)

    def place(n, carry):
        src_ref[pos_ref[n]] = n
        return carry

    lax.fori_loop(0, N_TOK, place, 0, unroll=16)


def _moe_invert(pos):
    smem = pl.BlockSpec(memory_space=pltpu.SMEM)
    return pl.pallas_call(
        _moe_invert_kernel,
        in_specs=[smem],
        out_specs=smem,
        out_shape=jax.ShapeDtypeStruct((MOE_ROWS,), jnp.int32),
        name="moe_invert",
    )(pos)


def _moe_expert_kernel(src_ref, grp_ref, nact_ref, nchunk_ref, h_hbm, w1_ref, w3_ref, w2_ref, y_hbm,
                       xb0, xb1, ab0, ab1, gsem, ssem):
    t = pl.program_id(0)
    k = pl.program_id(1)
    n_active = nact_ref[0]
    last = n_active - 1
    T = MOE_TILE
    CH = MOE_DMA_CHUNK
    xbufs, accs = (xb0, xb1), (ab0, ab1)

    def gather_row(tile, slot, base, rr):
        tok = src_ref[tile * T + base + rr]
        rows = xbufs[slot].at[pl.ds(base, CH), :]
        return pltpu.make_async_copy(h_hbm.at[pl.ds(tok, 1), :], rows.at[pl.ds(rr, 1), :], gsem.at[slot])

    def scatter_row(tile, slot, base, rr):
        dst = src_ref[tile * T + base + rr]
        rows = accs[slot].at[pl.ds(base, CH), :]
        return pltpu.make_async_copy(rows.at[pl.ds(rr, 1), :], y_hbm.at[pl.ds(dst, 1), :], ssem.at[slot])

    def start_rows(make, tile, slot):
        def chunk(c, carry):
            base = pl.multiple_of(c * CH, CH)
            for rr in range(CH):
                make(tile, slot, base, rr).start()
            return carry

        lax.fori_loop(0, nchunk_ref[tile], chunk, 0)

    def wait_rows(tile, slot, gather):
        def chunk(c, carry):
            if gather:
                pltpu.make_async_copy(h_hbm.at[pl.ds(0, CH), :], xbufs[slot].at[pl.ds(0, CH), :], gsem.at[slot]).wait()
            else:
                pltpu.make_async_copy(accs[slot].at[pl.ds(0, CH), :], y_hbm.at[pl.ds(0, CH), :], ssem.at[slot]).wait()
            return carry

        lax.fori_loop(0, nchunk_ref[tile], chunk, 0)

    def step(slot):
        other = 1 - slot
        xb, acc = xbufs[slot], accs[slot]

        @pl.when(k == 0)
        def _():
            if slot == 0:
                @pl.when(t == 0)
                def _():
                    xb0[...] = jnp.zeros_like(xb0)
                    xb1[...] = jnp.zeros_like(xb1)
                    ab0[...] = jnp.zeros_like(ab0)
                    ab1[...] = jnp.zeros_like(ab1)
                    dump = pltpu.make_async_copy(ab0.at[pl.ds(0, 2 * CH), :], y_hbm.at[pl.ds(N_TOK, 2 * CH), :],
                                                 ssem.at[0])
                    dump.start()
                    dump.wait()
                    start_rows(gather_row, 0, 0)

            wait_rows(t, slot, True)

            @pl.when(t < last)
            def _():
                start_rows(gather_row, t + 1, other)

            @pl.when(t >= 2)
            def _():
                wait_rows(t - 2, slot, False)

        def experts(m):
            x = xb[:m, :D].astype(BF16)
            a = _dot(x, w1_ref[...].astype(BF16))
            b = _dot(x, w3_ref[...].astype(BF16))
            lane = lax.broadcasted_iota(jnp.int32, (m, LANES), 1)
            cwk = jnp.sum(jnp.where(lane == k, xb[:m, D:], 0.0), axis=-1, keepdims=True)
            hid = (_silu(a) * b * cwk).astype(BF16)
            acc[:m] = jnp.where(k > 0, acc[:m], 0.0) + _dot(hid, w2_ref[...].astype(BF16))

        blocks = (nchunk_ref[t] * CH + MOE_ROW_BLOCK - 1) // MOE_ROW_BLOCK
        for nb in range(1, T // MOE_ROW_BLOCK + 1):
            pl.when(blocks == nb)(functools.partial(experts, nb * MOE_ROW_BLOCK))

        @pl.when(k == MOE_PER_GROUP - 1)
        def _():
            start_rows(scatter_row, t, slot)

            @pl.when(t == last)
            def _():
                wait_rows(t, slot, False)

                @pl.when(t >= 1)
                def _():
                    wait_rows(t - 1, other, False)

    for slot in (0, 1):
        pl.when((t < n_active) & (t % 2 == slot))(functools.partial(step, slot))


def _moe_experts(h_ext, src, tile_group, n_active, n_chunk, layer, w1, w3, w2):
    T = MOE_TILE

    def w_index(t, k, src_ref, grp_ref, nact_ref, nchunk_ref):
        last = nact_ref[0] - 1
        e = jnp.where(t <= last, grp_ref[t] * MOE_PER_GROUP + k, grp_ref[last] * MOE_PER_GROUP + MOE_PER_GROUP - 1)
        return (layer, e, 0, 0)

    grid_spec = pltpu.PrefetchScalarGridSpec(
        num_scalar_prefetch=4,
        grid=(MOE_MAX_TILES, MOE_PER_GROUP),
        in_specs=[
            pl.BlockSpec(memory_space=pl.ANY),
            pl.BlockSpec((None, None, D, MOE_HID), w_index),
            pl.BlockSpec((None, None, D, MOE_HID), w_index),
            pl.BlockSpec((None, None, MOE_HID, D), w_index),
        ],
        out_specs=pl.BlockSpec(memory_space=pl.ANY),
        scratch_shapes=[
            pltpu.VMEM((T, D_EXT), F32),
            pltpu.VMEM((T, D_EXT), F32),
            pltpu.VMEM((T, D), F32),
            pltpu.VMEM((T, D), F32),
            pltpu.SemaphoreType.DMA((2,)),
            pltpu.SemaphoreType.DMA((2,)),
        ],
    )
    return pl.pallas_call(
        _moe_expert_kernel,
        grid_spec=grid_spec,
        out_shape=jax.ShapeDtypeStruct((MOE_Y_ROWS, D), F32),
        compiler_params=_cparams(2),
        name="moe_experts",
    )(src, tile_group, n_active, n_chunk, h_ext, w1, w3, w2)


def _moe_combine_kernel(y_ref, gate_ref, x_ref, o_ref):
    o_ref[...] = x_ref[...] + gate_ref[...] * y_ref[...]


def _moe_combine(x, y3, mod, layer):
    return pl.pallas_call(
        _moe_combine_kernel,
        grid=(N_TOK // TM,),
        in_specs=[
            pl.BlockSpec((TM, D), lambda t: (t, 0)),
            _mod_spec(layer, 5, _row_tm),
            pl.BlockSpec((TM, D), lambda t: (t, 0)),
        ],
        out_specs=pl.BlockSpec((TM, D), lambda t: (t, 0)),
        out_shape=jax.ShapeDtypeStruct((N_TOK, D), F32),
        input_output_aliases={2: 0},
        compiler_params=_cparams(1),
        name="moe_combine",
    )(y3, mod, x)


def _moe(x, mod, layer, norm_g, w_rg, w_re, w1, w3, w2, tri, final_g=None):
    w_router_t = jnp.zeros((ROUTER_PAD, D), F32).at[:MOE_E].set(w_re.T).at[MOE_E:MOE_E + MOE_GROUPS].set(w_rg.T)
    h3, route, counts = _moe_route(x, mod, layer, norm_g, w_router_t, tri)
    cnt = counts[:MOE_GROUPS, 0]
    ntile = (cnt + MOE_TILE - 1) // MOE_TILE
    tile_end = jnp.cumsum(ntile)
    seg_start = (tile_end - ntile) * MOE_TILE
    g_idx, rank = route[0], route[1]
    pos = jnp.sum(jnp.where(g_idx[None, :] == jnp.arange(MOE_GROUPS)[:, None], seg_start[:, None], 0), axis=0) + rank
    tiles = jnp.arange(MOE_MAX_TILES, dtype=jnp.int32)
    tile_group = jnp.minimum(jnp.sum(tiles[:, None] >= tile_end[None, :], axis=1), MOE_GROUPS - 1).astype(jnp.int32)
    n_active = tile_end[-1:].astype(jnp.int32)
    src = _moe_invert(pos.astype(jnp.int32))
    first_tile = (tile_end - ntile)[tile_group]
    real_rows = jnp.clip(cnt[tile_group] - (tiles - first_tile) * MOE_TILE, 0, MOE_TILE)
    n_chunk = ((real_rows + MOE_DMA_CHUNK - 1) // MOE_DMA_CHUNK).astype(jnp.int32)
    y3 = _moe_experts(h3, src, tile_group, n_active, n_chunk, layer, w1, w3, w2)
    if final_g is None:
        return y3
    return tuple(_moe_combine_norm(x, y3, mod, layer, final_g, off, nb * L) for L, nb, off in
                 ((CTX_L, CTX_B, 0), (LAT_L, LAT_B, N_CTX)))


def _combine_norm_kernel(y_ref, gate_ref, x_ref, g_ref, o_ref):
    x = x_ref[...] + gate_ref[...] * y_ref[...]
    o_ref[...] = x * lax.rsqrt(jnp.mean(x * x, axis=-1, keepdims=True) + EPS) * g_ref[...]


def _moe_combine_norm(x, y3, mod, layer, final_g, row_off, n_rows):
    off = row_off // TM
    first_lat = N_CTX // TM

    def row_fn(t):
        g = t + off
        return jnp.where(g < first_lat, 0, 1 + (g - first_lat) // (LAT_L // TM))

    return pl.pallas_call(
        _combine_norm_kernel,
        grid=(n_rows // TM,),
        in_specs=[
            pl.BlockSpec((TM, D), lambda t: (t + off, 0)),
            _mod_spec(layer, 5, row_fn),
            pl.BlockSpec((TM, D), lambda t: (t + off, 0)),
            _vec_spec(D),
        ],
        out_specs=pl.BlockSpec((TM, D), lambda t: (t, 0)),
        out_shape=jax.ShapeDtypeStruct((n_rows, D), F32),
        compiler_params=_cparams(1),
        name="combine_final_norm",
    )(y3, mod, x, final_g.reshape(1, D))


def kernel(x_prompt, x_sample, state_gla, c, c_ctx, w_ada, b_ada, norm_g, hy_w_in, hy_b_in, hy_conv_w, hy_conv_b, hy_f_w1, hy_f_b1, hy_f_freq, hy_f_w2, hy_f_b2, hy_f_w3, hy_skip, hy_w_out, hy_b_out, gla_w_q, gla_w_k, gla_w_v, gla_w_g, gla_w_gk1, gla_w_gk2, gla_b_gk, gla_norm_g, gla_w_o, fn_w_out, fn_b_out, pool_w, pool_b, pool_scale, moe_w_rg, moe_w_re, moe_w1, moe_w3, moe_w2, final_g):
    groups = ((CTX_L, CTX_B, 0, None), (LAT_L, LAT_B, N_CTX, LAT_L // GRID_W))

    x = jnp.concatenate([x_prompt.reshape(N_CTX, D), x_sample.reshape(N_LAT, D)], axis=0)
    cond = jnp.zeros((MOD_ROWS, D), F32).at[0].set(c_ctx).at[1:1 + LAT_B].set(c)
    mod = _ada_table(cond, w_ada, b_ada).reshape(DEPTH * MOD_ROWS * 6, 1, D)

    tri_tm = jnp.asarray(np.triu(np.ones((TM, TM)), 1), BF16)

    new_states = []
    pend = None
    for i in range(DEPTH):
        kind, j = i % 4, i // 4
        if kind == 0:
            if pend is not None:
                x = _moe_combine(x, pend[0], mod, pend[1])
            u = _hyena_in(x, mod, i, norm_g[i, 0], hy_w_in[j].astype(BF16), hy_b_in[j], hy_conv_w[j], hy_conv_b[j])
            w_out = hy_w_out[j].astype(BF16)
            for L, nb, off, _ in groups:
                fwd, ff, inv = _dft_mats(L)
                khat = _hyena_filters(L, ff, hy_f_w1[j], hy_f_b1[j], hy_f_freq[j], hy_f_w2[j], hy_f_b2[j],
                                      hy_f_w3[j])
                z = _hyena_conv(u, khat, hy_skip[j], fwd, inv, L, nb, off, D if L == CTX_L else 512)
                x = _outproj(x, z, w_out, hy_b_out[j], mod, i, off, nb * L)
        elif kind == 1:
            w_cat = jnp.concatenate([gla_w_q[j], gla_w_k[j], gla_w_v[j], gla_w_g[j]], axis=1).astype(BF16)
            nk = GLA_H * GLA_DK
            wg1 = jnp.zeros((D, GK1_PAD), F32).at[:, :GLA_RANK].set(gla_w_gk1[j, 0])
            wg1 = wg1.at[:, GLA_RANK:2 * GLA_RANK].set(gla_w_gk1[j, 1]).astype(BF16)
            wg2 = jnp.zeros((GK1_PAD, 2 * nk), F32).at[:GLA_RANK, :nk].set(gla_w_gk2[j, 0])
            wg2 = wg2.at[GLA_RANK:2 * GLA_RANK, nk:].set(gla_w_gk2[j, 1]).astype(BF16)
            proj = _gla_proj(x, mod, i, norm_g[i, 0], w_cat, wg1, wg2, gla_b_gk[j].reshape(1, 2 * nk), pend)
            lower = np.tril(np.ones((GLA_CHUNK, GLA_CHUNK)))
            w_o = gla_w_o[j].astype(BF16)
            for L, nb, off, grid_rows in groups:
                eye = np.eye(L // GLA_CHUNK)
                tri = jnp.asarray(np.stack([np.kron(eye, lower), np.kron(eye, lower.T)]), BF16)
                s0 = None if grid_rows is None else state_gla[:, j]
                o, s_fin = _gla_core(proj, tri, gla_norm_g[j], s0, L, nb, off, GLA_H if L == CTX_L else 1)
                if grid_rows is None:
                    new_states.append(s_fin)
                x = _outproj(x, o, w_o, jnp.zeros((D,), F32), mod, i, off, nb * L, pend)
        elif kind == 2:
            w_out = fn_w_out[j].astype(BF16)
            for L, nb, off, _ in groups:
                chan, seq = _fnet_mats(L)
                x = _fnet(x, mod, i, norm_g[i, 0], chan, seq, w_out, fn_b_out[j], L, nb, off, pend)
        else:
            w_pool = pool_w[j].astype(BF16)
            for L, nb, off, grid_rows in groups:
                mats, inv_cnt = _pool_mats(L, grid_rows)
                x = _pool(x, mod, i, norm_g[i, 0], mats, inv_cnt, w_pool, pool_b[j], pool_scale[j], L, nb, off,
                          pend)

        out = _moe(x, mod, i, norm_g[i, 1], moe_w_rg[i], moe_w_re[i], moe_w1, moe_w3, moe_w2, tri_tm,
                   final_g if i == DEPTH - 1 else None)
        pend = (out, i)

    y_prompt, y_sample = out
    new_state_gla = jnp.stack(new_states, axis=1)
    return (y_prompt.reshape(CTX_B, CTX_L, D), y_sample.reshape(LAT_B, LAT_L, D), new_state_gla)
```

```python
import functools
import math

import jax
import jax.numpy as jnp
import numpy as np
from jax import lax
from jax.experimental import pallas as pl
from jax.experimental.pallas import tpu as pltpu

F32 = jnp.float32
BF16 = jnp.bfloat16

D = 1024
CTX_B, CTX_L = 32, 256
LAT_B, LAT_L = 2, 1024
N_CTX = CTX_B * CTX_L
N_LAT = LAT_B * LAT_L
N_TOK = N_CTX + N_LAT
DEPTH = 4
GRID_W = 64
EPS = 1e-6

HY_BANDS = 8
HY_EMB = 1 + 2 * HY_BANDS
HY_EMB_PAD = 32
HY_HID = 64
HY_FAST_DECAY = 0.3
HY_SLOW_DECAY = 1.5
HY_DECAY_TARGET = 1e-2

GLA_H = 4
GLA_DK = 128
GLA_DV = 256
GLA_RANK = 16
GLA_NORMALIZER = 16.0
GLA_CHUNK = 64

FNET_GROUPS = 4
FNET_C = D // FNET_GROUPS
POOL_WINDOWS = (2, 4, 8, 16)
POOL_G = D // len(POOL_WINDOWS)

MOE_GROUPS = 4
MOE_PER_GROUP = 4
MOE_E = MOE_GROUPS * MOE_PER_GROUP
MOE_HID = D // 2

MOD_ROWS = 8
TM = 512
TM_BIG = 1024
VMEM_LIMIT = 56 * 1024 * 1024


def _cparams(n_axes):
    return pltpu.CompilerParams(dimension_semantics=("arbitrary",) * n_axes, vmem_limit_bytes=VMEM_LIMIT)


def _norm_mod(x, g, sc, sh):
    ms = jnp.mean(x * x, axis=-1, keepdims=True)
    return (x * lax.rsqrt(ms + EPS) * g) * (1.0 + sc) + sh


def _split(a):
    hi = a.astype(BF16)
    lo = (a - hi.astype(F32)).astype(BF16)
    return hi, lo


def _dot(a, b):
    return jnp.dot(a, b, preferred_element_type=F32)


def _dot_precise(a, b):
    a_hi, a_lo = _split(a)
    b_hi, b_lo = _split(b)
    return _dot(a_hi, b_hi) + (_dot(a_hi, b_lo) + _dot(a_lo, b_hi))


def _silu(x):
    return x * (1.0 / (1.0 + jnp.exp(-x)))


def _log_sigmoid(x):
    return jnp.minimum(x, 0.0) - jnp.log(1.0 + jnp.exp(-jnp.abs(x)))


def _mod_spec(layer, chunk, row_fn):
    base = layer * MOD_ROWS * 6 + chunk

    def index_map(*ids):
        return (base + row_fn(*ids) * 6, 0, 0)

    return pl.BlockSpec((None, 1, D), index_map)


def _row_tm(t, *_):
    return jnp.where(t < N_CTX // TM, 0, 1 + (t - N_CTX // TM) // (LAT_L // TM))


def _row_big(t, *_):
    return jnp.where(t < N_CTX // TM_BIG, 0, 1 + (t - N_CTX // TM_BIG) // (LAT_L // TM_BIG))


def _vec_spec(n):
    return pl.BlockSpec((1, n), lambda *ids: (0, 0))


def _full_spec(shape):
    nd = len(shape)
    return pl.BlockSpec(shape, lambda *ids: (0,) * nd)


def _ada_kernel(cond_ref, w_ref, b_ref, o_ref):
    s = _silu(cond_ref[...]).astype(BF16)
    o_ref[...] = _dot(s, w_ref[...].astype(BF16)) + b_ref[...]


def _ada_table(cond, w_ada, b_ada):
    tn = 1536
    return pl.pallas_call(
        _ada_kernel,
        grid=(DEPTH, 6 * D // tn),
        in_specs=[
            pl.BlockSpec((MOD_ROWS, D), lambda i, j: (0, 0)),
            pl.BlockSpec((None, D, tn), lambda i, j: (i, 0, j)),
            pl.BlockSpec((None, 1, tn), lambda i, j: (i, 0, j)),
        ],
        out_specs=pl.BlockSpec((None, MOD_ROWS, tn), lambda i, j: (i, 0, j)),
        out_shape=jax.ShapeDtypeStruct((DEPTH, MOD_ROWS, 6 * D), F32),
        compiler_params=_cparams(2),
        name="ada_table",
    )(cond, w_ada, b_ada.reshape(DEPTH, 1, 6 * D))


def _pending_specs(pend, mod, block_rows, row_index, row_fn):
    if pend is None:
        return [], []
    y, prev_layer = pend
    return [pl.BlockSpec((block_rows, D), row_index), _mod_spec(prev_layer, 5, row_fn)], [y, mod]


def _read_x(x_ref, pending_refs):
    x = x_ref[...]
    if pending_refs:
        y_ref, gate_ref = pending_refs
        x = x + gate_ref[...] * y_ref[...]
    return x


def _outproj_kernel(z_ref, w_ref, b_ref, gate_ref, x_ref, *rest):
    *pending, o_ref = rest
    y = _dot(z_ref[...], w_ref[...]) + b_ref[...]
    o_ref[...] = _read_x(x_ref, pending) + gate_ref[...] * y


def _outproj(x, z, w_bf16, bias, mod, layer, row_off, n_rows, pend=None):
    k = z.shape[1]
    off = row_off // TM
    first_lat = N_CTX // TM

    def row_fn(t):
        g = t + off
        return jnp.where(g < first_lat, 0, 1 + (g - first_lat) // (LAT_L // TM))

    p_specs, p_args = _pending_specs(pend, mod, TM, lambda t: (t + off, 0), row_fn)
    return pl.pallas_call(
        _outproj_kernel,
        grid=(n_rows // TM,),
        in_specs=[
            pl.BlockSpec((TM, k), lambda t: (t, 0)),
            _full_spec((k, D)),
            _vec_spec(D),
            _mod_spec(layer, 2, row_fn),
            pl.BlockSpec((TM, D), lambda t: (t + off, 0)),
        ] + p_specs,
        out_specs=pl.BlockSpec((TM, D), lambda t: (t + off, 0)),
        out_shape=jax.ShapeDtypeStruct((N_TOK, D), F32),
        input_output_aliases={4: 0},
        compiler_params=_cparams(1),
        name="outproj_residual",
    )(z, w_bf16, bias.reshape(1, D), mod, x, *p_args)


def _outproj_joint_kernel(zc_ref, zl_ref, w_ref, b_ref, gate_ref, xc_ref, xl_ref, o_ref):
    t = pl.program_id(0)
    for is_ctx, z_ref, x_ref in ((True, zc_ref, xc_ref), (False, zl_ref, xl_ref)):
        @pl.when((t < N_CTX // TM) == is_ctx)
        def _(z_ref=z_ref, x_ref=x_ref):
            y = _dot(z_ref[...], w_ref[...]) + b_ref[...]
            o_ref[...] = x_ref[...] + gate_ref[...] * y


def _outproj_joint(x_ctx, x_lat, z_ctx, z_lat, w_bf16, bias, mod, layer):
    k = z_ctx.shape[1]
    n_ctx_tiles = N_CTX // TM

    def ctx_block(t):
        return (jnp.minimum(t, n_ctx_tiles - 1), 0)

    def lat_block(t):
        return (jnp.maximum(t - n_ctx_tiles, 0), 0)

    return pl.pallas_call(
        _outproj_joint_kernel,
        grid=(N_TOK // TM,),
        in_specs=[
            pl.BlockSpec((TM, k), ctx_block),
            pl.BlockSpec((TM, k), lat_block),
            _full_spec((k, D)),
            _vec_spec(D),
            _mod_spec(layer, 2, _row_tm),
            pl.BlockSpec((TM, D), ctx_block),
            pl.BlockSpec((TM, D), lat_block),
        ],
        out_specs=pl.BlockSpec((TM, D), lambda t: (t, 0)),
        out_shape=jax.ShapeDtypeStruct((N_TOK, D), F32),
        compiler_params=_cparams(1),
        name="outproj_joint",
    )(z_ctx, z_lat, w_bf16, bias.reshape(1, D), mod, x_ctx, x_lat)


def _dft_mats(L):
    n2 = 2 * L
    k = np.arange(L)[:, None].astype(np.float64)
    n = np.arange(n2)[None, :].astype(np.float64)
    ang = 2.0 * np.pi * k * n / n2
    full = np.concatenate([np.cos(ang), -np.sin(ang)], axis=0)
    full[L, :] = np.cos(np.pi * np.arange(n2))
    fwd = full[:, :L]
    bwd = np.zeros((n2, L))
    bwd[:, 1:] = full[:, n2 - np.arange(1, L)]
    t = np.arange(L)[:, None].astype(np.float64)
    kk = np.arange(L)[None, :].astype(np.float64)
    ang_i = 2.0 * np.pi * t * kk / n2
    inv_re = np.cos(ang_i) / L
    inv_re[:, 0] = 1.0 / n2
    inv_im = -np.sin(ang_i) / L
    inv_im[:, 0] = np.cos(np.pi * np.arange(L)) / n2
    inv = np.concatenate([inv_re, inv_im], axis=1)
    return tuple(jnp.asarray(m, F32).astype(BF16) for m in (fwd, np.concatenate([fwd, bwd], axis=1), inv))


def _hyena_pos_emb(L):
    pos = np.arange(L, dtype=np.float64)
    bands = np.linspace(1e-4, HY_BANDS - 1, HY_BANDS)
    ang = (2.0 * np.pi * pos / L)[:, None] * bands[None, :]
    z = np.concatenate([(pos / L)[:, None], np.cos(ang), -np.sin(ang)], axis=-1)
    zp = np.zeros((L, HY_EMB_PAD))
    zp[:, :HY_EMB] = z
    return jnp.asarray(zp, F32)


def _hyena_filter_kernel(z_ref, w1_ref, b1_ref, fr_ref, w2_ref, b2_ref, w3f_ref, w3b_ref, ff_ref, o_ref, *, L, tn):
    j = pl.program_id(1)
    fr = fr_ref[...]
    f = jnp.sin(fr * (_dot_precise(z_ref[...], w1_ref[...]) + b1_ref[...]))
    f = jnp.sin(fr * (_dot_precise(f, w2_ref[...]) + b2_ref[...]))
    t_lin = lax.broadcasted_iota(jnp.int32, (L, tn), 0).astype(F32) / float(L - 1)
    ch = (lax.broadcasted_iota(jnp.int32, (L, tn), 1) + j * tn).astype(F32)
    max_decay = math.log(HY_DECAY_TARGET) / HY_FAST_DECAY
    min_decay = math.log(HY_DECAY_TARGET) / HY_SLOW_DECAY
    deltas = min_decay + ch * ((max_decay - min_decay) / float(D - 1))
    window = jnp.exp(-t_lin * jnp.abs(deltas))
    kf = _dot_precise(f, w3f_ref[...]) * window
    kb = _dot_precise(f, w3b_ref[...]) * window
    taps = jnp.concatenate([kf, kb], axis=0).astype(BF16)
    o_ref[...] = _dot(ff_ref[...], taps)


def _hyena_filters(L, ff, f_w1, f_b1, f_freq, f_w2, f_b2, f_w3):
    tn = 512
    nj = D // tn
    w1p = jnp.zeros((HY_EMB_PAD, HY_HID), F32).at[:HY_EMB].set(f_w1)
    kern = functools.partial(_hyena_filter_kernel, L=L, tn=tn)
    return pl.pallas_call(
        kern,
        grid=(2, nj),
        in_specs=[
            _full_spec((L, HY_EMB_PAD)),
            _full_spec((HY_EMB_PAD, HY_HID)),
            _vec_spec(HY_HID),
            _vec_spec(HY_HID),
            _full_spec((HY_HID, HY_HID)),
            _vec_spec(HY_HID),
            pl.BlockSpec((HY_HID, tn), lambda o, j: (0, o * nj + j)),
            pl.BlockSpec((HY_HID, tn), lambda o, j: (0, (2 + o) * nj + j)),
            _full_spec((2 * L, 2 * L)),
        ],
        out_specs=pl.BlockSpec((None, 2 * L, tn), lambda o, j: (o, 0, j)),
        out_shape=jax.ShapeDtypeStruct((2, 2 * L, D), F32),
        compiler_params=_cparams(2),
        name=f"hyena_filters_L{L}",
    )(_hyena_pos_emb(L), w1p, f_b1.reshape(1, -1), f_freq.reshape(1, -1), f_w2, f_b2.reshape(1, -1),
      f_w3, f_w3, ff)


def _hyena_in_kernel(xc_ref, xl_ref, g_ref, sc_ref, sh_ref, w_ref, b_ref, cw_ref, cb_ref, o_ref, h_scr):
    t = pl.program_id(0)
    first = pl.program_id(1) == 0

    for is_ctx, x_ref in ((True, xc_ref), (False, xl_ref)):
        @pl.when(first & ((t < N_CTX // TM_BIG) == is_ctx))
        def _(x_ref=x_ref):
            h_scr[...] = _norm_mod(x_ref[...], g_ref[...], sc_ref[...], sh_ref[...]).astype(BF16)

    u = _dot(h_scr[...], w_ref[...]) + b_ref[...]
    cw = cw_ref[...]
    o_ref[...] = pltpu.roll(u, 1, 0) * cw[0:1] + u * cw[1:2] + pltpu.roll(u, TM_BIG - 1, 0) * cw[2:3] + cb_ref[...]
    is_ctx = (t < N_CTX // TM_BIG).astype(F32)
    for start in range(0, TM_BIG, CTX_L):
        f = 1.0 if start % LAT_L == 0 else is_ctx
        before = (start - 1) % TM_BIG
        o_ref[start:start + 1, :] = o_ref[start:start + 1, :] - f * (u[before:before + 1] * cw[0:1])
        end = start + CTX_L - 1
        g = 1.0 if (end + 1) % LAT_L == 0 else is_ctx
        after = (end + 1) % TM_BIG
        o_ref[end:end + 1, :] = o_ref[end:end + 1, :] - g * (u[after:after + 1] * cw[2:3])


def _hyena_in(x_ctx, x_lat, mod, layer, norm_g, w_in_bf16, b_in, conv_w, conv_b):
    n_ctx_tiles = N_CTX // TM_BIG
    return pl.pallas_call(
        _hyena_in_kernel,
        grid=(N_TOK // TM_BIG, 3),
        in_specs=[
            pl.BlockSpec((TM_BIG, D), lambda t, p: (jnp.minimum(t, n_ctx_tiles - 1), 0)),
            pl.BlockSpec((TM_BIG, D), lambda t, p: (jnp.maximum(t - n_ctx_tiles, 0), 0)),
            _vec_spec(D),
            _mod_spec(layer, 1, _row_big),
            _mod_spec(layer, 0, _row_big),
            pl.BlockSpec((D, D), lambda t, p: (0, p)),
            pl.BlockSpec((1, D), lambda t, p: (0, p)),
            pl.BlockSpec((3, D), lambda t, p: (0, p)),
            pl.BlockSpec((1, D), lambda t, p: (0, p)),
        ],
        out_specs=pl.BlockSpec((TM_BIG, D), lambda t, p: (t, p)),
        out_shape=jax.ShapeDtypeStruct((N_TOK, 3 * D), F32),
        scratch_shapes=[pltpu.VMEM((TM_BIG, D), BF16)],
        compiler_params=_cparams(2),
        name="hyena_in",
    )(x_ctx, x_lat, norm_g.reshape(1, D), mod, mod, w_in_bf16, b_in.reshape(1, -1), conv_w, conv_b.reshape(1, -1))


def _hyena_conv_kernel(v_ref, x1_ref, x2_ref, kh_ref, skip_ref, fwd_ref, inv_ref, o_ref, *, L):
    fwd = fwd_ref[...]
    inv = inv_ref[...]
    row0 = lax.broadcasted_iota(jnp.int32, (L, v_ref.shape[1]), 0) == 0

    def long_conv(z, order):
        zh = _dot(fwd, z.astype(BF16))
        zr, zi = zh[:L], zh[L:]
        kr, ki = kh_ref[order, :L, :], kh_ref[order, L:, :]
        pr = jnp.where(row0, zr * kr, zr * kr - zi * ki)
        pi = jnp.where(row0, zi * ki, zr * ki + zi * kr)
        prod = jnp.concatenate([pr, pi], axis=0).astype(BF16)
        return _dot(inv, prod) + z * skip_ref[order:order + 1, :]

    z = x1_ref[...] * long_conv(v_ref[...], 0)
    z = x2_ref[...] * long_conv(z, 1)
    o_ref[...] = z.astype(BF16)


def _hyena_conv(u, khat, skip, fwd, inv, L, n_batch, row_off, tn):
    nj = D // tn
    rb = row_off // L
    kern = functools.partial(_hyena_conv_kernel, L=L)
    return pl.pallas_call(
        kern,
        grid=(nj, n_batch),
        in_specs=[
            pl.BlockSpec((L, tn), lambda j, b: (rb + b, j)),
            pl.BlockSpec((L, tn), lambda j, b: (rb + b, nj + j)),
            pl.BlockSpec((L, tn), lambda j, b: (rb + b, 2 * nj + j)),
            pl.BlockSpec((2, 2 * L, tn), lambda j, b: (0, 0, j)),
            pl.BlockSpec((2, tn), lambda j, b: (0, j)),
            _full_spec((2 * L, L)),
            _full_spec((L, 2 * L)),
        ],
        out_specs=pl.BlockSpec((L, tn), lambda j, b: (b, j)),
        out_shape=jax.ShapeDtypeStruct((n_batch * L, D), BF16),
        compiler_params=_cparams(2),
        name=f"hyena_conv_L{L}",
    )(u, u, u, khat, skip, fwd, inv)


GLA_PROJ = 2 * GLA_H * GLA_DK + 2 * GLA_H * GLA_DV
GLA_COLS = GLA_PROJ + 2 * GLA_H * GLA_DK
GK1_PAD = 128


def _gla_proj_kernel(x_ref, g_ref, sc_ref, sh_ref, w_ref, wg1_ref, wg2_ref, bg_ref, *rest):
    *pending, o_ref = rest
    h = _norm_mod(_read_x(x_ref, pending), g_ref[...], sc_ref[...], sh_ref[...]).astype(BF16)
    p = _dot(h, w_ref[...])
    nq = GLA_H * GLA_DK
    o_ref[:, 0:nq] = p[:, 0:nq] * (GLA_DK ** -0.5)
    o_ref[:, nq:nq + nq + GLA_H * GLA_DV] = p[:, nq:nq + nq + GLA_H * GLA_DV]
    o_ref[:, 2 * nq + GLA_H * GLA_DV:GLA_PROJ] = _silu(p[:, 2 * nq + GLA_H * GLA_DV:GLA_PROJ])
    low = _dot(h, wg1_ref[...]).astype(BF16)
    gk = _dot(low, wg2_ref[...]) + bg_ref[...]
    o_ref[:, GLA_PROJ:GLA_COLS] = _log_sigmoid(gk) / GLA_NORMALIZER


def _gla_proj(x, mod, layer, norm_g, w_cat, wg1, wg2, bg, pend=None):
    p_specs, p_args = _pending_specs(pend, mod, TM, lambda t: (t, 0), _row_tm)
    return pl.pallas_call(
        _gla_proj_kernel,
        grid=(N_TOK // TM,),
        in_specs=[
            pl.BlockSpec((TM, D), lambda t: (t, 0)),
            _vec_spec(D),
            _mod_spec(layer, 1, _row_tm),
            _mod_spec(layer, 0, _row_tm),
            _full_spec((D, GLA_PROJ)),
            _full_spec((D, GK1_PAD)),
            _full_spec((GK1_PAD, 2 * GLA_H * GLA_DK)),
            _vec_spec(2 * GLA_H * GLA_DK),
        ] + p_specs,
        out_specs=pl.BlockSpec((TM, GLA_COLS), lambda t: (t, 0)),
        out_shape=jax.ShapeDtypeStruct((N_TOK, GLA_COLS), F32),
        compiler_params=_cparams(1),
        name="gla_proj",
    )(x, norm_g.reshape(1, D), mod, mod, w_cat, wg1, wg2, bg, *p_args)


def _gla_core_kernel(*refs, L, has_s0, hps):
    if has_s0:
        q_ref, k_ref, v_ref, g_ref, gkf_ref, gkb_ref, tri_ref, ng_ref, s0_ref, o_ref, sf_ref, acc = refs
    else:
        q_ref, k_ref, v_ref, g_ref, gkf_ref, gkb_ref, tri_ref, ng_ref, o_ref, sf_ref, acc = refs
        s0_ref = None
    C = GLA_CHUNK
    n = L // C
    ri = lax.broadcasted_iota(jnp.int32, (C, C), 0)
    ci = lax.broadcasted_iota(jnp.int32, (C, C), 1)
    nt_dims = (((1,), (1,)), ((), ()))
    tn_dims = (((0,), (0,)), ((), ()))

    for hh in range(hps):
        kc = slice(hh * GLA_DK, (hh + 1) * GLA_DK)
        vc = slice(hh * GLA_DV, (hh + 1) * GLA_DV)
        for direction, gk_ref in enumerate((gkf_ref, gkb_ref)):
            keep = (ci <= ri) if direction == 0 else (ci >= ri)
            last = C - 1 if direction == 0 else 0
            gk_hi, gk_lo = _split(gk_ref[:, kc])
            b_all = _dot(tri_ref[direction], jnp.concatenate([gk_hi, gk_lo], axis=1))
            b_all = b_all[:, :GLA_DK] + b_all[:, GLA_DK:]
            st = s0_ref[direction, hh].T if has_s0 else jnp.zeros((GLA_DV, GLA_DK), F32)
            order = range(n) if direction == 0 else range(n - 1, -1, -1)
            for c in order:
                rows = slice(c * C, (c + 1) * C)
                b = b_all[rows]
                b_last = b[last:last + 1, :]
                q = q_ref[rows, kc]
                k = k_ref[rows, kc]
                v = v_ref[rows, vc].astype(BF16)
                qe = (q * jnp.exp(b)).astype(BF16)
                ke = (k * jnp.exp(-b)).astype(BF16)
                kd = (k * jnp.exp(b_last - b)).astype(BF16)
                scores = lax.dot_general(qe, ke, nt_dims, preferred_element_type=F32)
                scores = jnp.where(keep, scores, 0.0).astype(BF16)
                o = _dot(scores, v) + lax.dot_general(qe, st.astype(BF16), nt_dims, preferred_element_type=F32)
                if direction == 0:
                    acc[rows, vc] = o
                else:
                    acc[rows, vc] = acc[rows, vc] + o
                st = jnp.exp(b_last) * st + lax.dot_general(v, kd, tn_dims, preferred_element_type=F32)
            sf_ref[direction, hh] = st.T

        o = acc[:, vc]
        o = o * lax.rsqrt(jnp.mean(o * o, axis=-1, keepdims=True) + EPS) * ng_ref[...]
        o_ref[:, vc] = (o * g_ref[:, vc]).astype(BF16)


def _gla_core(proj, tri, norm_g, s0, L, n_batch, row_off, hps):
    rb = row_off // L
    H = GLA_H
    nh = H // hps
    has_s0 = s0 is not None
    kern = functools.partial(_gla_core_kernel, L=L, has_s0=has_s0, hps=hps)
    kb, vb = GLA_DK * hps, GLA_DV * hps
    in_specs = [
        pl.BlockSpec((L, kb), lambda b, h: (rb + b, h)),
        pl.BlockSpec((L, kb), lambda b, h: (rb + b, nh + h)),
        pl.BlockSpec((L, vb), lambda b, h: (rb + b, (2 * H * GLA_DK) // vb + h)),
        pl.BlockSpec((L, vb), lambda b, h: (rb + b, (2 * H * GLA_DK) // vb + nh + h)),
        pl.BlockSpec((L, kb), lambda b, h: (rb + b, GLA_PROJ // kb + h)),
        pl.BlockSpec((L, kb), lambda b, h: (rb + b, GLA_PROJ // kb + nh + h)),
        _full_spec((2, L, L)),
        _vec_spec(GLA_DV),
    ]
    args = [proj] * 6 + [tri, norm_g.reshape(1, GLA_DV)]
    state_spec = pl.BlockSpec((None, 2, hps, GLA_DK, GLA_DV), lambda b, h: (b, 0, h, 0, 0))
    if has_s0:
        in_specs.append(state_spec)
        args.append(s0)
    return pl.pallas_call(
        kern,
        grid=(n_batch, nh),
        in_specs=in_specs,
        out_specs=[pl.BlockSpec((L, vb), lambda b, h: (b, h)), state_spec],
        out_shape=[
            jax.ShapeDtypeStruct((n_batch * L, H * GLA_DV), BF16),
            jax.ShapeDtypeStruct((n_batch, 2, H, GLA_DK, GLA_DV), F32),
        ],
        scratch_shapes=[pltpu.VMEM((L, vb), F32)],
        compiler_params=_cparams(2),
        name=f"gla_core_L{L}",
    )(*args)


def _fnet_mats(L):
    c = np.arange(FNET_C)
    ang_c = 2.0 * np.pi * np.outer(c, c) / FNET_C
    chan = np.concatenate([np.cos(ang_c), np.sin(ang_c)], axis=1) / math.sqrt(FNET_C)
    t = np.arange(L)
    ang_l = 2.0 * np.pi * np.outer(t, t) / L
    seq = np.concatenate([np.cos(ang_l), -np.sin(ang_l)], axis=1) / math.sqrt(L)
    return jnp.asarray(chan, F32).astype(BF16), jnp.asarray(seq, F32).astype(BF16)


def _fnet_kernel(x_ref, g_ref, sc_ref, sh_ref, gate_ref, chan_ref, seq_ref, w_ref, b_ref, *rest):
    *pending, o_ref = rest
    x = _read_x(x_ref, pending)
    h = _norm_mod(x, g_ref[...], sc_ref[...], sh_ref[...]).astype(BF16)
    chan = chan_ref[...]
    cos_parts, sin_parts = [], []
    for gi in range(FNET_GROUPS):
        cs = _dot(h[:, gi * FNET_C:(gi + 1) * FNET_C], chan)
        cos_parts.append(cs[:, :FNET_C])
        sin_parts.append(cs[:, FNET_C:])
    stacked = jnp.concatenate([jnp.concatenate(cos_parts, axis=1), jnp.concatenate(sin_parts, axis=1)], axis=0)
    mixed = _dot(seq_ref[...], stacked.astype(BF16))
    y = _dot(mixed.astype(BF16), w_ref[...]) + b_ref[...]
    o_ref[...] = x + gate_ref[...] * y


def _fnet(x, mod, layer, norm_g, chan, seq, w_bf16, bias, L, n_batch, row_off, pend=None):
    rb = row_off // L
    lat = row_off > 0

    def row_fn(b):
        return 1 + b if lat else 0

    p_specs, p_args = _pending_specs(pend, mod, L, lambda b: (rb + b, 0), row_fn)
    return pl.pallas_call(
        _fnet_kernel,
        grid=(n_batch,),
        in_specs=[
            pl.BlockSpec((L, D), lambda b: (rb + b, 0)),
            _vec_spec(D),
            _mod_spec(layer, 1, row_fn),
            _mod_spec(layer, 0, row_fn),
            _mod_spec(layer, 2, row_fn),
            _full_spec((FNET_C, 2 * FNET_C)),
            _full_spec((L, 2 * L)),
            _full_spec((D, D)),
            _vec_spec(D),
        ] + p_specs,
        out_specs=pl.BlockSpec((L, D), lambda b: (rb + b, 0)),
        out_shape=jax.ShapeDtypeStruct((N_TOK, D), F32),
        input_output_aliases={0: 0},
        compiler_params=_cparams(1),
        name=f"fnet_L{L}",
    )(x, norm_g.reshape(1, D), mod, mod, mod, chan, seq, w_bf16, bias.reshape(1, D), *p_args)


def _window_bounds(n, k):
    t = np.arange(n)
    lo, hi = k // 2, k - k // 2 - 1
    return np.maximum(t - lo, 0), np.minimum(t + hi + 1, n)


def _pool_mats(L, grid_rows):
    mats, inv = [], []
    for k in POOL_WINDOWS:
        if grid_rows is None:
            s, e = _window_bounds(L, k)
            idx = np.arange(L)[None, :]
            m = ((idx >= s[:, None]) & (idx < e[:, None])).astype(np.float64)
            cnt = (e - s).astype(np.float64)
        else:
            sr, er = _window_bounds(grid_rows, k)
            sc, ec = _window_bounds(GRID_W, k)
            ir = np.arange(grid_rows)[None, :]
            ic = np.arange(GRID_W)[None, :]
            mr = ((ir >= sr[:, None]) & (ir < er[:, None])).astype(np.float64)
            mc = ((ic >= sc[:, None]) & (ic < ec[:, None])).astype(np.float64)
            m = np.kron(mr, mc)
            cnt = np.kron((er - sr).astype(np.float64), (ec - sc).astype(np.float64))
        mats.append(m)
        inv.append(1.0 / cnt)
    return jnp.asarray(np.stack(mats), BF16), jnp.asarray(np.stack(inv)[:, :, None], F32)


def _pool_kernel(x_ref, g_ref, sc_ref, sh_ref, gate_ref, m_ref, ic_ref, w_ref, b_ref, ps_ref, *rest):
    *pending, o_ref = rest
    x = _read_x(x_ref, pending)
    h = _norm_mod(x, g_ref[...], sc_ref[...], sh_ref[...])
    outs = []
    for gi in range(len(POOL_WINDOWS)):
        hg = h[:, gi * POOL_G:(gi + 1) * POOL_G]
        hi, lo = _split(hg)
        m = m_ref[gi]
        mean = (_dot(m, hi) + _dot(m, lo)) * ic_ref[gi]
        outs.append(_dot((mean - hg).astype(BF16), w_ref[gi]))
    y = (jnp.concatenate(outs, axis=1) + b_ref[...]) * ps_ref[...]
    o_ref[...] = x + gate_ref[...] * y


def _pool(x, mod, layer, norm_g, mats, inv_cnt, w_bf16, bias, scale, L, n_batch, row_off, pend=None):
    rb = row_off // L
    lat = row_off > 0
    G = len(POOL_WINDOWS)

    def row_fn(b):
        return 1 + b if lat else 0

    p_specs, p_args = _pending_specs(pend, mod, L, lambda b: (rb + b, 0), row_fn)
    return pl.pallas_call(
        _pool_kernel,
        grid=(n_batch,),
        in_specs=[
            pl.BlockSpec((L, D), lambda b: (rb + b, 0)),
            _vec_spec(D),
            _mod_spec(layer, 1, row_fn),
            _mod_spec(layer, 0, row_fn),
            _mod_spec(layer, 2, row_fn),
            _full_spec((G, L, L)),
            _full_spec((G, L, 1)),
            _full_spec((G, POOL_G, POOL_G)),
            _vec_spec(D),
            _vec_spec(D),
        ] + p_specs,
        out_specs=pl.BlockSpec((L, D), lambda b: (rb + b, 0)),
        out_shape=jax.ShapeDtypeStruct((N_TOK, D), F32),
        input_output_aliases={0: 0},
        compiler_params=_cparams(1),
        name=f"pool_L{L}",
    )(x, norm_g.reshape(1, D), mod, mod, mod, mats, inv_cnt, w_bf16, bias.reshape(1, D), scale.reshape(1, D),
      *p_args)


ROUTER_PAD = 128
LANES = 128
D_EXT = D + LANES
MOE_TILE = 1024
MOE_MAX_TILES = N_TOK // MOE_TILE + MOE_GROUPS
MOE_ROWS = MOE_MAX_TILES * MOE_TILE
MOE_TILE_SHIFT = MOE_TILE.bit_length() - 1
assert 1 << MOE_TILE_SHIFT == MOE_TILE
MOE_ROW_BLOCK = 256
MOE_DMA_CHUNK = 64
MOE_CHUNK_SHIFT = MOE_DMA_CHUNK.bit_length() - 1
assert 1 << MOE_CHUNK_SHIFT == MOE_DMA_CHUNK and MOE_MAX_TILES % 2 == 0
MOE_Y_ROWS = N_TOK + 2 * MOE_DMA_CHUNK


ROUTE_ROWS = 8


def _moe_route_kernel(x_ref, g_ref, sc_ref, sh_ref, wr_ref, tri_ref, h3_ref, route_ref, cnt_ref, carry):
    t = pl.program_id(0)
    refs = (x_ref, g_ref, sc_ref, sh_ref, wr_ref, tri_ref, h3_ref, route_ref, cnt_ref, carry)
    pl.when(t < N_TOK // TM)(functools.partial(_moe_route_tile, t, *refs))

    @pl.when(t == N_TOK // TM)
    def _():
        h3_ref[...] = jnp.zeros_like(h3_ref)


def _moe_route_tile(t, x_ref, g_ref, sc_ref, sh_ref, wr_ref, tri_ref, h3_ref, route_ref, cnt_ref, carry):
    @pl.when(t == 0)
    def _():
        carry[...] = jnp.zeros_like(carry)

    h = _norm_mod(x_ref[...], g_ref[...], sc_ref[...], sh_ref[...])
    w_hi, w_lo = _split(wr_ref[...])
    h_hi, h_lo = _split(h)
    nt = (((1,), (1,)), ((), ()))
    logits = (lax.dot_general(w_hi, h_hi, nt, preferred_element_type=F32)
              + (lax.dot_general(w_hi, h_lo, nt, preferred_element_type=F32)
                 + lax.dot_general(w_lo, h_hi, nt, preferred_element_type=F32)))
    neg = jnp.float32(-jnp.inf)
    r8 = lax.broadcasted_iota(jnp.int32, (ROUTE_ROWS, TM), 0)
    r16 = lax.broadcasted_iota(jnp.int32, (MOE_E, TM), 0)
    gl = jnp.where(r8 < MOE_GROUPS, logits[MOE_E:MOE_E + ROUTE_ROWS], neg)
    g_max = jnp.max(gl, axis=0, keepdims=True)
    g_idx = jnp.min(jnp.where(gl == g_max, r8, ROUTE_ROWS), axis=0, keepdims=True)
    p_grp = 1.0 / jnp.sum(jnp.exp(gl - g_max), axis=0, keepdims=True)
    in_grp = (r16 >> 2) == g_idx
    el = jnp.where(in_grp, logits[:MOE_E], neg)
    m1 = jnp.max(el, axis=0, keepdims=True)
    i1 = jnp.min(jnp.where(el == m1, r16, MOE_E), axis=0, keepdims=True)
    z = jnp.sum(jnp.exp(el - m1), axis=0, keepdims=True)
    el2 = jnp.where(r16 == i1, neg, el)
    m2 = jnp.max(el2, axis=0, keepdims=True)
    i2 = jnp.min(jnp.where(el2 == m2, r16, MOE_E), axis=0, keepdims=True)
    p1 = 1.0 / z
    p2 = jnp.exp(m2 - m1) / z
    tot = p1 + p2
    eid = r8 + MOE_PER_GROUP * g_idx
    in4 = r8 < MOE_PER_GROUP
    cw4 = (jnp.where(in4 & (eid == i1), p_grp * (p1 / tot), 0.0)
           + jnp.where(in4 & (eid == i2), p_grp * (p2 / tot), 0.0))
    member = jnp.where(r8 == g_idx, 1.0, 0.0)
    before = _dot(member.astype(BF16), tri_ref[...]) + carry[:, 0:1]
    rank = jnp.sum(jnp.where(r8 == g_idx, before, 0.0), axis=0, keepdims=True)
    carry[...] = carry[...] + jnp.sum(member, axis=1, keepdims=True)
    cnt_ref[...] = carry[...].astype(jnp.int32)
    route_ref[...] = jnp.where(r8 == 0, g_idx, jnp.where(r8 == 1, rank.astype(jnp.int32), 0))
    h3_ref[:, :D] = h
    cw_rows = jnp.concatenate([cw4, jnp.zeros((LANES - ROUTE_ROWS, TM), F32)], axis=0)
    h3_ref[:, D:] = cw_rows.T


def _moe_route(x, mod, layer, norm_g, w_router_t, tri):
    nt = N_TOK // TM

    def tok_tile(t):
        return jnp.minimum(t, nt - 1)

    return pl.pallas_call(
        _moe_route_kernel,
        grid=(nt + 1,),
        in_specs=[
            pl.BlockSpec((TM, D), lambda t: (tok_tile(t), 0)),
            _vec_spec(D),
            _mod_spec(layer, 4, lambda t: _row_tm(tok_tile(t))),
            _mod_spec(layer, 3, lambda t: _row_tm(tok_tile(t))),
            _full_spec((ROUTER_PAD, D)),
            _full_spec((TM, TM)),
        ],
        out_specs=[
            pl.BlockSpec((TM, D_EXT), lambda t: (t, 0)),
            pl.BlockSpec((ROUTE_ROWS, TM), lambda t: (0, tok_tile(t))),
            pl.BlockSpec((ROUTE_ROWS, LANES), lambda t: (0, 0)),
        ],
        out_shape=[
            jax.ShapeDtypeStruct((N_TOK + TM, D_EXT), F32),
            jax.ShapeDtypeStruct((ROUTE_ROWS, N_TOK), jnp.int32),
            jax.ShapeDtypeStruct((ROUTE_ROWS, LANES), jnp.int32),
        ],
        scratch_shapes=[pltpu.VMEM((ROUTE_ROWS, LANES), F32)],
        compiler_params=_cparams(1),
        name="moe_route",
    )(x, norm_g.reshape(1, D), mod, mod, w_router_t, tri)


def _moe_invert_kernel(pos_ref, lo_ref, hi_ref, src_ref):
    def mark(j, carry):
        parity = lax.shift_right_logical(j, jnp.int32(MOE_TILE_SHIFT - MOE_CHUNK_SHIFT)) & 1
        first = N_TOK + MOE_DMA_CHUNK * parity
        base = j * MOE_DMA_CHUNK
        for rr in range(MOE_DMA_CHUNK):
            src_ref[base + rr] = first + rr
        return carry

    for g in range(MOE_GROUPS + 1):
        lax.fori_loop(lo_ref[g], hi_ref[g], mark, 0)

    def place(n, carry):
        src_ref[pos_ref[n]] = n
        return carry

    lax.fori_loop(0, N_TOK, place, 0, unroll=16)


def _moe_invert(pos, mark_lo, mark_hi):
    smem = pl.BlockSpec(memory_space=pltpu.SMEM)
    return pl.pallas_call(
        _moe_invert_kernel,
        in_specs=[smem, smem, smem],
        out_specs=smem,
        out_shape=jax.ShapeDtypeStruct((MOE_ROWS,), jnp.int32),
        name="moe_invert",
    )(pos, mark_lo, mark_hi)


def _moe_expert_kernel(src_ref, grp_ref, nact_ref, nchunk_ref, h_hbm, w1_ref, w3_ref, w2_ref, y_hbm,
                       xb0, xb1, ab0, ab1, gsem, ssem):
    t = pl.program_id(0)
    k = pl.program_id(1)
    n_active = nact_ref[0]
    last = n_active - 1
    T = MOE_TILE
    CH = MOE_DMA_CHUNK
    xbufs, accs = (xb0, xb1), (ab0, ab1)

    def gather_row(tile, slot, base, rr):
        tok = src_ref[tile * T + base + rr]
        rows = xbufs[slot].at[pl.ds(base, CH), :]
        return pltpu.make_async_copy(h_hbm.at[pl.ds(tok, 1), :], rows.at[pl.ds(rr, 1), :], gsem.at[slot])

    def scatter_row(tile, slot, base, rr):
        dst = src_ref[tile * T + base + rr]
        rows = accs[slot].at[pl.ds(base, CH), :]
        return pltpu.make_async_copy(rows.at[pl.ds(rr, 1), :], y_hbm.at[pl.ds(dst, 1), :], ssem.at[slot])

    def start_rows(make, tile, slot):
        def chunk(c, carry):
            base = pl.multiple_of(c * CH, CH)
            for rr in range(CH):
                make(tile, slot, base, rr).start()
            return carry

        lax.fori_loop(0, nchunk_ref[tile], chunk, 0)

    def wait_rows(tile, slot, gather):
        def chunk(c, carry):
            if gather:
                pltpu.make_async_copy(h_hbm.at[pl.ds(0, CH), :], xbufs[slot].at[pl.ds(0, CH), :], gsem.at[slot]).wait()
            else:
                pltpu.make_async_copy(accs[slot].at[pl.ds(0, CH), :], y_hbm.at[pl.ds(0, CH), :], ssem.at[slot]).wait()
            return carry

        lax.fori_loop(0, nchunk_ref[tile], chunk, 0)

    def step(slot):
        other = 1 - slot
        xb, acc = xbufs[slot], accs[slot]

        @pl.when(k == 0)
        def _():
            if slot == 0:
                @pl.when(t == 0)
                def _():
                    xb0[...] = jnp.zeros_like(xb0)
                    xb1[...] = jnp.zeros_like(xb1)
                    ab0[...] = jnp.zeros_like(ab0)
                    ab1[...] = jnp.zeros_like(ab1)
                    dump = pltpu.make_async_copy(ab0.at[pl.ds(0, 2 * CH), :], y_hbm.at[pl.ds(N_TOK, 2 * CH), :],
                                                 ssem.at[0])
                    dump.start()
                    dump.wait()
                    start_rows(gather_row, 0, 0)

            wait_rows(t, slot, True)

            @pl.when(t < last)
            def _():
                start_rows(gather_row, t + 1, other)

            @pl.when(t >= 2)
            def _():
                wait_rows(t - 2, slot, False)

        def experts(m):
            x = xb[:m, :D].astype(BF16)
            a = _dot(x, w1_ref[...].astype(BF16))
            b = _dot(x, w3_ref[...].astype(BF16))
            lane = lax.broadcasted_iota(jnp.int32, (m, LANES), 1)
            cwk = jnp.sum(jnp.where(lane == k, xb[:m, D:], 0.0), axis=-1, keepdims=True)
            hid = (_silu(a) * b * cwk).astype(BF16)
            acc[:m] = jnp.where(k > 0, acc[:m], 0.0) + _dot(hid, w2_ref[...].astype(BF16))

        blocks = (nchunk_ref[t] * CH + MOE_ROW_BLOCK - 1) // MOE_ROW_BLOCK
        for nb in range(1, T // MOE_ROW_BLOCK + 1):
            pl.when(blocks == nb)(functools.partial(experts, nb * MOE_ROW_BLOCK))

        @pl.when(k == MOE_PER_GROUP - 1)
        def _():
            start_rows(scatter_row, t, slot)

            @pl.when(t == last)
            def _():
                wait_rows(t, slot, False)

                @pl.when(t >= 1)
                def _():
                    wait_rows(t - 1, other, False)

    for slot in (0, 1):
        pl.when((t < n_active) & (t % 2 == slot))(functools.partial(step, slot))


def _moe_experts(h_ext, src, tile_group, n_active, n_chunk, layer, w1, w3, w2):
    T = MOE_TILE

    def w_index(t, k, src_ref, grp_ref, nact_ref, nchunk_ref):
        last = nact_ref[0] - 1
        e = jnp.where(t <= last, grp_ref[t] * MOE_PER_GROUP + k, grp_ref[last] * MOE_PER_GROUP + MOE_PER_GROUP - 1)
        return (layer, e, 0, 0)

    grid_spec = pltpu.PrefetchScalarGridSpec(
        num_scalar_prefetch=4,
        grid=(MOE_MAX_TILES, MOE_PER_GROUP),
        in_specs=[
            pl.BlockSpec(memory_space=pl.ANY),
            pl.BlockSpec((None, None, D, MOE_HID), w_index),
            pl.BlockSpec((None, None, D, MOE_HID), w_index),
            pl.BlockSpec((None, None, MOE_HID, D), w_index),
        ],
        out_specs=pl.BlockSpec(memory_space=pl.ANY),
        scratch_shapes=[
            pltpu.VMEM((T, D_EXT), F32),
            pltpu.VMEM((T, D_EXT), F32),
            pltpu.VMEM((T, D), F32),
            pltpu.VMEM((T, D), F32),
            pltpu.SemaphoreType.DMA((2,)),
            pltpu.SemaphoreType.DMA((2,)),
        ],
    )
    return pl.pallas_call(
        _moe_expert_kernel,
        grid_spec=grid_spec,
        out_shape=jax.ShapeDtypeStruct((MOE_Y_ROWS, D), F32),
        compiler_params=_cparams(2),
        name="moe_experts",
    )(src, tile_group, n_active, n_chunk, h_ext, w1, w3, w2)


def _moe_combine_kernel(y_ref, gate_ref, x_ref, o_ref):
    o_ref[...] = x_ref[...] + gate_ref[...] * y_ref[...]


def _moe_combine(x, y3, mod, layer):
    return pl.pallas_call(
        _moe_combine_kernel,
        grid=(N_TOK // TM,),
        in_specs=[
            pl.BlockSpec((TM, D), lambda t: (t, 0)),
            _mod_spec(layer, 5, _row_tm),
            pl.BlockSpec((TM, D), lambda t: (t, 0)),
        ],
        out_specs=pl.BlockSpec((TM, D), lambda t: (t, 0)),
        out_shape=jax.ShapeDtypeStruct((N_TOK, D), F32),
        input_output_aliases={2: 0},
        compiler_params=_cparams(1),
        name="moe_combine",
    )(y3, mod, x)


def _moe(x, mod, layer, norm_g, w_rg, w_re, w1, w3, w2, tri, final_g=None):
    w_router_t = jnp.zeros((ROUTER_PAD, D), F32).at[:MOE_E].set(w_re.T).at[MOE_E:MOE_E + MOE_GROUPS].set(w_rg.T)
    h3, route, counts = _moe_route(x, mod, layer, norm_g, w_router_t, tri)
    cnt = counts[:MOE_GROUPS, 0]
    ntile = (cnt + MOE_TILE - 1) // MOE_TILE
    tile_end = jnp.cumsum(ntile)
    seg_start = (tile_end - ntile) * MOE_TILE
    g_idx, rank = route[0], route[1]
    pos = jnp.sum(jnp.where(g_idx[None, :] == jnp.arange(MOE_GROUPS)[:, None], seg_start[:, None], 0), axis=0) + rank
    tiles = jnp.arange(MOE_MAX_TILES, dtype=jnp.int32)
    tile_group = jnp.minimum(jnp.sum(tiles[:, None] >= tile_end[None, :], axis=1), MOE_GROUPS - 1).astype(jnp.int32)
    n_active = tile_end[-1:].astype(jnp.int32)
    seg_end = tile_end * MOE_TILE
    mark_lo = jnp.concatenate([(seg_start + cnt) // MOE_DMA_CHUNK, seg_end[-1:] // MOE_DMA_CHUNK])
    mark_hi = jnp.concatenate([seg_end // MOE_DMA_CHUNK, jnp.full((1,), MOE_ROWS // MOE_DMA_CHUNK)])
    src = _moe_invert(pos.astype(jnp.int32), mark_lo.astype(jnp.int32), mark_hi.astype(jnp.int32))
    first_tile = (tile_end - ntile)[tile_group]
    real_rows = jnp.clip(cnt[tile_group] - (tiles - first_tile) * MOE_TILE, 0, MOE_TILE)
    n_chunk = ((real_rows + MOE_DMA_CHUNK - 1) // MOE_DMA_CHUNK).astype(jnp.int32)
    y3 = _moe_experts(h3, src, tile_group, n_active, n_chunk, layer, w1, w3, w2)
    if final_g is None:
        return y3
    return tuple(_moe_combine_norm(x, y3, mod, layer, final_g, off, nb * L) for L, nb, off in
                 ((CTX_L, CTX_B, 0), (LAT_L, LAT_B, N_CTX)))


def _combine_norm_kernel(y_ref, gate_ref, x_ref, g_ref, o_ref):
    x = x_ref[...] + gate_ref[...] * y_ref[...]
    o_ref[...] = x * lax.rsqrt(jnp.mean(x * x, axis=-1, keepdims=True) + EPS) * g_ref[...]


def _moe_combine_norm(x, y3, mod, layer, final_g, row_off, n_rows):
    off = row_off // TM
    first_lat = N_CTX // TM

    def row_fn(t):
        g = t + off
        return jnp.where(g < first_lat, 0, 1 + (g - first_lat) // (LAT_L // TM))

    return pl.pallas_call(
        _combine_norm_kernel,
        grid=(n_rows // TM,),
        in_specs=[
            pl.BlockSpec((TM, D), lambda t: (t + off, 0)),
            _mod_spec(layer, 5, row_fn),
            pl.BlockSpec((TM, D), lambda t: (t + off, 0)),
            _vec_spec(D),
        ],
        out_specs=pl.BlockSpec((TM, D), lambda t: (t, 0)),
        out_shape=jax.ShapeDtypeStruct((n_rows, D), F32),
        compiler_params=_cparams(1),
        name="combine_final_norm",
    )(y3, mod, x, final_g.reshape(1, D))


def kernel(x_prompt, x_sample, state_gla, c, c_ctx, w_ada, b_ada, norm_g, hy_w_in, hy_b_in, hy_conv_w, hy_conv_b, hy_f_w1, hy_f_b1, hy_f_freq, hy_f_w2, hy_f_b2, hy_f_w3, hy_skip, hy_w_out, hy_b_out, gla_w_q, gla_w_k, gla_w_v, gla_w_g, gla_w_gk1, gla_w_gk2, gla_b_gk, gla_norm_g, gla_w_o, fn_w_out, fn_b_out, pool_w, pool_b, pool_scale, moe_w_rg, moe_w_re, moe_w1, moe_w3, moe_w2, final_g):
    groups = ((CTX_L, CTX_B, 0, None), (LAT_L, LAT_B, N_CTX, LAT_L // GRID_W))

    x_ctx, x_lat = x_prompt.reshape(N_CTX, D), x_sample.reshape(N_LAT, D)
    x = None
    cond =jnp.zeros((MOD_ROWS, D), F32).at[0].set(c_ctx).at[1:1 + LAT_B].set(c)
    mod = _ada_table(cond, w_ada, b_ada).reshape(DEPTH * MOD_ROWS * 6, 1, D)

    tri_tm = jnp.asarray(np.triu(np.ones((TM, TM)), 1), BF16)

    new_states = []
    pend = None
    for i in range(DEPTH):
        kind, j = i % 4, i // 4
        if kind == 0:
            if i > 0:
                x = _moe_combine(x, pend[0], mod, pend[1])
                x_ctx, x_lat = x[:N_CTX], x[N_CTX:]
            u = _hyena_in(x_ctx, x_lat, mod, i, norm_g[i, 0], hy_w_in[j].astype(BF16), hy_b_in[j], hy_conv_w[j],
                          hy_conv_b[j])
            zs = []
            for L, nb, off, _ in groups:
                fwd, ff, inv = _dft_mats(L)
                khat = _hyena_filters(L, ff, hy_f_w1[j], hy_f_b1[j], hy_f_freq[j], hy_f_w2[j], hy_f_b2[j],
                                      hy_f_w3[j])
                zs.append(_hyena_conv(u, khat, hy_skip[j], fwd, inv, L, nb, off, D if L == CTX_L else 512))
            x = _outproj_joint(x_ctx, x_lat, zs[0], zs[1], hy_w_out[j].astype(BF16), hy_b_out[j], mod, i)
        elif kind == 1:
            w_cat = jnp.concatenate([gla_w_q[j], gla_w_k[j], gla_w_v[j], gla_w_g[j]], axis=1).astype(BF16)
            nk = GLA_H * GLA_DK
            wg1 = jnp.zeros((D, GK1_PAD), F32).at[:, :GLA_RANK].set(gla_w_gk1[j, 0])
            wg1 = wg1.at[:, GLA_RANK:2 * GLA_RANK].set(gla_w_gk1[j, 1]).astype(BF16)
            wg2 = jnp.zeros((GK1_PAD, 2 * nk), F32).at[:GLA_RANK, :nk].set(gla_w_gk2[j, 0])
            wg2 = wg2.at[GLA_RANK:2 * GLA_RANK, nk:].set(gla_w_gk2[j, 1]).astype(BF16)
            proj = _gla_proj(x, mod, i, norm_g[i, 0], w_cat, wg1, wg2, gla_b_gk[j].reshape(1, 2 * nk), pend)
            lower = np.tril(np.ones((GLA_CHUNK, GLA_CHUNK)))
            w_o = gla_w_o[j].astype(BF16)
            for L, nb, off, grid_rows in groups:
                eye = np.eye(L // GLA_CHUNK)
                tri = jnp.asarray(np.stack([np.kron(eye, lower), np.kron(eye, lower.T)]), BF16)
                s0 = None if grid_rows is None else state_gla[:, j]
                o, s_fin = _gla_core(proj, tri, gla_norm_g[j], s0, L, nb, off, GLA_H if L == CTX_L else 1)
                if grid_rows is None:
                    new_states.append(s_fin)
                x = _outproj(x, o, w_o, jnp.zeros((D,), F32), mod, i, off, nb * L, pend)
        elif kind == 2:
            w_out = fn_w_out[j].astype(BF16)
            for L, nb, off, _ in groups:
                chan, seq = _fnet_mats(L)
                x = _fnet(x, mod, i, norm_g[i, 0], chan, seq, w_out, fn_b_out[j], L, nb, off, pend)
        else:
            w_pool = pool_w[j].astype(BF16)
            for L, nb, off, grid_rows in groups:
                mats, inv_cnt = _pool_mats(L, grid_rows)
                x = _pool(x, mod, i, norm_g[i, 0], mats, inv_cnt, w_pool, pool_b[j], pool_scale[j], L, nb, off,
                          pend)

        out = _moe(x, mod, i, norm_g[i, 1], moe_w_rg[i], moe_w_re[i], moe_w1, moe_w3, moe_w2, tri_tm,
                   final_g if i == DEPTH - 1 else None)
        pend = (out, i)

    y_prompt, y_sample = out
    new_state_gla = jnp.stack(new_states, axis=1)
    return (y_prompt.reshape(CTX_B, CTX_L, D), y_sample.reshape(LAT_B, LAT_L, D), new_state_gla)
```

```python
import functools
import math

import jax
import jax.numpy as jnp
import numpy as np
from jax import lax
from jax.experimental import pallas as pl
from jax.experimental.pallas import tpu as pltpu

F32 = jnp.float32
BF16 = jnp.bfloat16

D = 1024
CTX_B, CTX_L = 32, 256
LAT_B, LAT_L = 2, 1024
N_CTX = CTX_B * CTX_L
N_LAT = LAT_B * LAT_L
N_TOK = N_CTX + N_LAT
DEPTH = 4
GRID_W = 64
EPS = 1e-6

HY_BANDS = 8
HY_EMB = 1 + 2 * HY_BANDS
HY_EMB_PAD = 32
HY_HID = 64
HY_FAST_DECAY = 0.3
HY_SLOW_DECAY = 1.5
HY_DECAY_TARGET = 1e-2

GLA_H = 4
GLA_DK = 128
GLA_DV = 256
GLA_RANK = 16
GLA_NORMALIZER = 16.0
GLA_CHUNK = 64
GLA_WHOLE_SEQ_CHUNKS = 4

FNET_GROUPS = 4
FNET_C = D // FNET_GROUPS
POOL_WINDOWS = (2, 4, 8, 16)
POOL_G = D // len(POOL_WINDOWS)

MOE_GROUPS = 4
MOE_PER_GROUP = 4
MOE_E = MOE_GROUPS * MOE_PER_GROUP
MOE_HID = D // 2

MOD_ROWS = 8
TM = 512
TM_BIG = 1024
VMEM_LIMIT = 56 * 1024 * 1024


def _cparams(n_axes):
    return pltpu.CompilerParams(dimension_semantics=("arbitrary",) * n_axes, vmem_limit_bytes=VMEM_LIMIT)


def _norm_mod(x, g, sc, sh):
    ms = jnp.mean(x * x, axis=-1, keepdims=True)
    return (x * lax.rsqrt(ms + EPS) * g) * (1.0 + sc) + sh


def _split(a):
    hi = a.astype(BF16)
    lo = (a - hi.astype(F32)).astype(BF16)
    return hi, lo


def _dot(a, b):
    return jnp.dot(a, b, preferred_element_type=F32)


def _dot_precise(a, b):
    a_hi, a_lo = _split(a)
    b_hi, b_lo = _split(b)
    return _dot(a_hi, b_hi) + (_dot(a_hi, b_lo) + _dot(a_lo, b_hi))


def _silu(x):
    return x * (1.0 / (1.0 + jnp.exp(-x)))


def _log_sigmoid(x):
    return jnp.minimum(x, 0.0) - jnp.log(1.0 + jnp.exp(-jnp.abs(x)))


def _mod_spec(layer, chunk, row_fn):
    base = layer * MOD_ROWS * 6 + chunk

    def index_map(*ids):
        return (base + row_fn(*ids) * 6, 0, 0)

    return pl.BlockSpec((None, 1, D), index_map)


def _row_tm(t, *_):
    return jnp.where(t < N_CTX // TM, 0, 1 + (t - N_CTX // TM) // (LAT_L // TM))


def _row_big(t, *_):
    return jnp.where(t < N_CTX // TM_BIG, 0, 1 + (t - N_CTX // TM_BIG) // (LAT_L // TM_BIG))


def _vec_spec(n):
    return pl.BlockSpec((1, n), lambda *ids: (0, 0))


def _full_spec(shape):
    nd = len(shape)
    return pl.BlockSpec(shape, lambda *ids: (0,) * nd)


def _ada_kernel(cond_ref, w_ref, b_ref, o_ref):
    s = _silu(cond_ref[...]).astype(BF16)
    o_ref[...] = _dot(s, w_ref[...].astype(BF16)) + b_ref[...]


def _ada_table(cond, w_ada, b_ada):
    tn = 1536
    return pl.pallas_call(
        _ada_kernel,
        grid=(DEPTH, 6 * D // tn),
        in_specs=[
            pl.BlockSpec((MOD_ROWS, D), lambda i, j: (0, 0)),
            pl.BlockSpec((None, D, tn), lambda i, j: (i, 0, j)),
            pl.BlockSpec((None, 1, tn), lambda i, j: (i, 0, j)),
        ],
        out_specs=pl.BlockSpec((None, MOD_ROWS, tn), lambda i, j: (i, 0, j)),
        out_shape=jax.ShapeDtypeStruct((DEPTH, MOD_ROWS, 6 * D), F32),
        compiler_params=_cparams(2),
        name="ada_table",
    )(cond, w_ada, b_ada.reshape(DEPTH, 1, 6 * D))


def _pending_specs(pend, mod, block_rows, row_index, row_fn):
    if pend is None:
        return [], []
    y, prev_layer = pend
    return [pl.BlockSpec((block_rows, D), row_index), _mod_spec(prev_layer, 5, row_fn)], [y, mod]


def _read_x(x_ref, pending_refs):
    x = x_ref[...]
    if pending_refs:
        y_ref, gate_ref = pending_refs
        x = x + gate_ref[...] * y_ref[...]
    return x


def _outproj_kernel(z_ref, w_ref, b_ref, gate_ref, x_ref, *rest):
    *pending, o_ref = rest
    y = _dot(z_ref[...], w_ref[...]) + b_ref[...]
    o_ref[...] = _read_x(x_ref, pending) + gate_ref[...] * y


def _outproj(x, z, w_bf16, bias, mod, layer, row_off, n_rows, pend=None):
    k = z.shape[1]
    off = row_off // TM
    first_lat = N_CTX // TM

    def row_fn(t):
        g = t + off
        return jnp.where(g < first_lat, 0, 1 + (g - first_lat) // (LAT_L // TM))

    p_specs, p_args = _pending_specs(pend, mod, TM, lambda t: (t + off, 0), row_fn)
    return pl.pallas_call(
        _outproj_kernel,
        grid=(n_rows // TM,),
        in_specs=[
            pl.BlockSpec((TM, k), lambda t: (t, 0)),
            _full_spec((k, D)),
            _vec_spec(D),
            _mod_spec(layer, 2, row_fn),
            pl.BlockSpec((TM, D), lambda t: (t + off, 0)),
        ] + p_specs,
        out_specs=pl.BlockSpec((TM, D), lambda t: (t + off, 0)),
        out_shape=jax.ShapeDtypeStruct((N_TOK, D), F32),
        input_output_aliases={4: 0},
        compiler_params=_cparams(1),
        name="outproj_residual",
    )(z, w_bf16, bias.reshape(1, D), mod, x, *p_args)


def _outproj_joint_kernel(zc_ref, zl_ref, w_ref, b_ref, gate_ref, xc_ref, xl_ref, o_ref):
    t = pl.program_id(0)
    for is_ctx, z_ref, x_ref in ((True, zc_ref, xc_ref), (False, zl_ref, xl_ref)):
        @pl.when((t < N_CTX // TM) == is_ctx)
        def _(z_ref=z_ref, x_ref=x_ref):
            y = _dot(z_ref[...], w_ref[...]) + b_ref[...]
            o_ref[...] = x_ref[...] + gate_ref[...] * y


def _outproj_joint(x_ctx, x_lat, z_ctx, z_lat, w_bf16, bias, mod, layer):
    k = z_ctx.shape[1]
    n_ctx_tiles = N_CTX // TM

    def ctx_block(t):
        return (jnp.minimum(t, n_ctx_tiles - 1), 0)

    def lat_block(t):
        return (jnp.maximum(t - n_ctx_tiles, 0), 0)

    return pl.pallas_call(
        _outproj_joint_kernel,
        grid=(N_TOK // TM,),
        in_specs=[
            pl.BlockSpec((TM, k), ctx_block),
            pl.BlockSpec((TM, k), lat_block),
            _full_spec((k, D)),
            _vec_spec(D),
            _mod_spec(layer, 2, _row_tm),
            pl.BlockSpec((TM, D), ctx_block),
            pl.BlockSpec((TM, D), lat_block),
        ],
        out_specs=pl.BlockSpec((TM, D), lambda t: (t, 0)),
        out_shape=jax.ShapeDtypeStruct((N_TOK, D), F32),
        compiler_params=_cparams(1),
        name="outproj_joint",
    )(z_ctx, z_lat, w_bf16, bias.reshape(1, D), mod, x_ctx, x_lat)


def _dft_mats(L):
    n2 = 2 * L
    k = np.arange(L)[:, None].astype(np.float64)
    n = np.arange(n2)[None, :].astype(np.float64)
    ang = 2.0 * np.pi * k * n / n2
    full = np.concatenate([np.cos(ang), -np.sin(ang)], axis=0)
    full[L, :] = np.cos(np.pi * np.arange(n2))
    fwd = full[:, :L]
    bwd = np.zeros((n2, L))
    bwd[:, 1:] = full[:, n2 - np.arange(1, L)]
    t = np.arange(L)[:, None].astype(np.float64)
    kk = np.arange(L)[None, :].astype(np.float64)
    ang_i = 2.0 * np.pi * t * kk / n2
    inv_re = np.cos(ang_i) / L
    inv_re[:, 0] = 1.0 / n2
    inv_im = -np.sin(ang_i) / L
    inv_im[:, 0] = np.cos(np.pi * np.arange(L)) / n2
    inv = np.concatenate([inv_re, inv_im], axis=1)
    return tuple(jnp.asarray(m, F32).astype(BF16) for m in (fwd, np.concatenate([fwd, bwd], axis=1), inv))


def _hyena_pos_emb(L):
    pos = np.arange(L, dtype=np.float64)
    bands = np.linspace(1e-4, HY_BANDS - 1, HY_BANDS)
    ang = (2.0 * np.pi * pos / L)[:, None] * bands[None, :]
    z = np.concatenate([(pos / L)[:, None], np.cos(ang), -np.sin(ang)], axis=-1)
    zp = np.zeros((L, HY_EMB_PAD))
    zp[:, :HY_EMB] = z
    return jnp.asarray(zp, F32)


def _hyena_filter_kernel(z_ref, w1_ref, b1_ref, fr_ref, w2_ref, b2_ref, w3f_ref, w3b_ref, ff_ref, o_ref, *, L, tn):
    j = pl.program_id(1)
    fr = fr_ref[...]
    f = jnp.sin(fr * (_dot_precise(z_ref[...], w1_ref[...]) + b1_ref[...]))
    f = jnp.sin(fr * (_dot_precise(f, w2_ref[...]) + b2_ref[...]))
    t_lin = lax.broadcasted_iota(jnp.int32, (L, tn), 0).astype(F32) / float(L - 1)
    ch = (lax.broadcasted_iota(jnp.int32, (L, tn), 1) + j * tn).astype(F32)
    max_decay = math.log(HY_DECAY_TARGET) / HY_FAST_DECAY
    min_decay = math.log(HY_DECAY_TARGET) / HY_SLOW_DECAY
    deltas = min_decay + ch * ((max_decay - min_decay) / float(D - 1))
    window = jnp.exp(-t_lin * jnp.abs(deltas))
    kf = _dot_precise(f, w3f_ref[...]) * window
    kb = _dot_precise(f, w3b_ref[...]) * window
    taps = jnp.concatenate([kf, kb], axis=0).astype(BF16)
    o_ref[...] = _dot(ff_ref[...], taps)


def _hyena_filters(L, ff, f_w1, f_b1, f_freq, f_w2, f_b2, f_w3):
    tn = 512
    nj = D // tn
    w1p = jnp.zeros((HY_EMB_PAD, HY_HID), F32).at[:HY_EMB].set(f_w1)
    kern = functools.partial(_hyena_filter_kernel, L=L, tn=tn)
    return pl.pallas_call(
        kern,
        grid=(2, nj),
        in_specs=[
            _full_spec((L, HY_EMB_PAD)),
            _full_spec((HY_EMB_PAD, HY_HID)),
            _vec_spec(HY_HID),
            _vec_spec(HY_HID),
            _full_spec((HY_HID, HY_HID)),
            _vec_spec(HY_HID),
            pl.BlockSpec((HY_HID, tn), lambda o, j: (0, o * nj + j)),
            pl.BlockSpec((HY_HID, tn), lambda o, j: (0, (2 + o) * nj + j)),
            _full_spec((2 * L, 2 * L)),
        ],
        out_specs=pl.BlockSpec((None, 2 * L, tn), lambda o, j: (o, 0, j)),
        out_shape=jax.ShapeDtypeStruct((2, 2 * L, D), F32),
        compiler_params=_cparams(2),
        name=f"hyena_filters_L{L}",
    )(_hyena_pos_emb(L), w1p, f_b1.reshape(1, -1), f_freq.reshape(1, -1), f_w2, f_b2.reshape(1, -1),
      f_w3, f_w3, ff)


def _hyena_in_kernel(xc_ref, xl_ref, g_ref, sc_ref, sh_ref, w_ref, b_ref, cw_ref, cb_ref, o_ref, h_scr):
    t = pl.program_id(0)
    first = pl.program_id(1) == 0

    for is_ctx, x_ref in ((True, xc_ref), (False, xl_ref)):
        @pl.when(first & ((t < N_CTX // TM_BIG) == is_ctx))
        def _(x_ref=x_ref):
            h_scr[...] = _norm_mod(x_ref[...], g_ref[...], sc_ref[...], sh_ref[...]).astype(BF16)

    u = _dot(h_scr[...], w_ref[...]) + b_ref[...]
    cw = cw_ref[...]
    o_ref[...] = pltpu.roll(u, 1, 0) * cw[0:1] + u * cw[1:2] + pltpu.roll(u, TM_BIG - 1, 0) * cw[2:3] + cb_ref[...]
    is_ctx = (t < N_CTX // TM_BIG).astype(F32)
    for start in range(0, TM_BIG, CTX_L):
        f = 1.0 if start % LAT_L == 0 else is_ctx
        before = (start - 1) % TM_BIG
        o_ref[start:start + 1, :] = o_ref[start:start + 1, :] - f * (u[before:before + 1] * cw[0:1])
        end = start + CTX_L - 1
        g = 1.0 if (end + 1) % LAT_L == 0 else is_ctx
        after = (end + 1) % TM_BIG
        o_ref[end:end + 1, :] = o_ref[end:end + 1, :] - g * (u[after:after + 1] * cw[2:3])


def _hyena_in(x_ctx, x_lat, mod, layer, norm_g, w_in_bf16, b_in, conv_w, conv_b):
    n_ctx_tiles = N_CTX // TM_BIG
    return pl.pallas_call(
        _hyena_in_kernel,
        grid=(N_TOK // TM_BIG, 3),
        in_specs=[
            pl.BlockSpec((TM_BIG, D), lambda t, p: (jnp.minimum(t, n_ctx_tiles - 1), 0)),
            pl.BlockSpec((TM_BIG, D), lambda t, p: (jnp.maximum(t - n_ctx_tiles, 0), 0)),
            _vec_spec(D),
            _mod_spec(layer, 1, _row_big),
            _mod_spec(layer, 0, _row_big),
            pl.BlockSpec((D, D), lambda t, p: (0, p)),
            pl.BlockSpec((1, D), lambda t, p: (0, p)),
            pl.BlockSpec((3, D), lambda t, p: (0, p)),
            pl.BlockSpec((1, D), lambda t, p: (0, p)),
        ],
        out_specs=pl.BlockSpec((TM_BIG, D), lambda t, p: (t, p)),
        out_shape=jax.ShapeDtypeStruct((N_TOK, 3 * D), F32),
        scratch_shapes=[pltpu.VMEM((TM_BIG, D), BF16)],
        compiler_params=_cparams(2),
        name="hyena_in",
    )(x_ctx, x_lat, norm_g.reshape(1, D), mod, mod, w_in_bf16, b_in.reshape(1, -1), conv_w, conv_b.reshape(1, -1))


def _hyena_conv_kernel(v_ref, x1_ref, x2_ref, kh_ref, skip_ref, fwd_ref, inv_ref, o_ref, *, L):
    fwd = fwd_ref[...]
    inv = inv_ref[...]
    row0 = lax.broadcasted_iota(jnp.int32, (L, v_ref.shape[1]), 0) == 0

    def long_conv(z, order):
        zh = _dot(fwd, z.astype(BF16))
        zr, zi = zh[:L], zh[L:]
        kr, ki = kh_ref[order, :L, :], kh_ref[order, L:, :]
        pr = jnp.where(row0, zr * kr, zr * kr - zi * ki)
        pi = jnp.where(row0, zi * ki, zr * ki + zi * kr)
        prod = jnp.concatenate([pr, pi], axis=0).astype(BF16)
        return _dot(inv, prod) + z * skip_ref[order:order + 1, :]

    z = x1_ref[...] * long_conv(v_ref[...], 0)
    z = x2_ref[...] * long_conv(z, 1)
    o_ref[...] = z.astype(BF16)


def _hyena_conv(u, khat, skip, fwd, inv, L, n_batch, row_off, tn):
    nj = D // tn
    rb = row_off // L
    kern = functools.partial(_hyena_conv_kernel, L=L)
    return pl.pallas_call(
        kern,
        grid=(nj, n_batch),
        in_specs=[
            pl.BlockSpec((L, tn), lambda j, b: (rb + b, j)),
            pl.BlockSpec((L, tn), lambda j, b: (rb + b, nj + j)),
            pl.BlockSpec((L, tn), lambda j, b: (rb + b, 2 * nj + j)),
            pl.BlockSpec((2, 2 * L, tn), lambda j, b: (0, 0, j)),
            pl.BlockSpec((2, tn), lambda j, b: (0, j)),
            _full_spec((2 * L, L)),
            _full_spec((L, 2 * L)),
        ],
        out_specs=pl.BlockSpec((L, tn), lambda j, b: (b, j)),
        out_shape=jax.ShapeDtypeStruct((n_batch * L, D), BF16),
        compiler_params=_cparams(2),
        name=f"hyena_conv_L{L}",
    )(u, u, u, khat, skip, fwd, inv)


GLA_PROJ = 2 * GLA_H * GLA_DK + 2 * GLA_H * GLA_DV
GLA_COLS = GLA_PROJ + 2 * GLA_H * GLA_DK
GK1_PAD = 128


def _gla_proj_kernel(x_ref, g_ref, sc_ref, sh_ref, w_ref, wg1_ref, wg2_ref, bg_ref, *rest):
    *pending, o_ref = rest
    h = _norm_mod(_read_x(x_ref, pending), g_ref[...], sc_ref[...], sh_ref[...]).astype(BF16)
    p = _dot(h, w_ref[...])
    nq = GLA_H * GLA_DK
    o_ref[:, 0:nq] = p[:, 0:nq] * (GLA_DK ** -0.5)
    o_ref[:, nq:nq + nq + GLA_H * GLA_DV] = p[:, nq:nq + nq + GLA_H * GLA_DV]
    o_ref[:, 2 * nq + GLA_H * GLA_DV:GLA_PROJ] = _silu(p[:, 2 * nq + GLA_H * GLA_DV:GLA_PROJ])
    low = _dot(h, wg1_ref[...]).astype(BF16)
    gk = _dot(low, wg2_ref[...]) + bg_ref[...]
    o_ref[:, GLA_PROJ:GLA_COLS] = _log_sigmoid(gk) / GLA_NORMALIZER


def _gla_proj(x, mod, layer, norm_g, w_cat, wg1, wg2, bg, pend=None):
    p_specs, p_args = _pending_specs(pend, mod, TM, lambda t: (t, 0), _row_tm)
    return pl.pallas_call(
        _gla_proj_kernel,
        grid=(N_TOK // TM,),
        in_specs=[
            pl.BlockSpec((TM, D), lambda t: (t, 0)),
            _vec_spec(D),
            _mod_spec(layer, 1, _row_tm),
            _mod_spec(layer, 0, _row_tm),
            _full_spec((D, GLA_PROJ)),
            _full_spec((D, GK1_PAD)),
            _full_spec((GK1_PAD, 2 * GLA_H * GLA_DK)),
            _vec_spec(2 * GLA_H * GLA_DK),
        ] + p_specs,
        out_specs=pl.BlockSpec((TM, GLA_COLS), lambda t: (t, 0)),
        out_shape=jax.ShapeDtypeStruct((N_TOK, GLA_COLS), F32),
        compiler_params=_cparams(1),
        name="gla_proj",
    )(x, norm_g.reshape(1, D), mod, mod, w_cat, wg1, wg2, bg, *p_args)


def _gla_core_kernel(*refs, L, has_s0, hps):
    if has_s0:
        q_ref, k_ref, v_ref, g_ref, gkf_ref, gkb_ref, tri_ref, ng_ref, s0_ref, o_ref, sf_ref, acc = refs
    else:
        q_ref, k_ref, v_ref, g_ref, gkf_ref, gkb_ref, tri_ref, ng_ref, o_ref, sf_ref, acc = refs
        s0_ref = None
    C = GLA_CHUNK
    n = L // C
    ri = lax.broadcasted_iota(jnp.int32, (C, C), 0)
    ci = lax.broadcasted_iota(jnp.int32, (C, C), 1)
    nt_dims = (((1,), (1,)), ((), ()))
    tn_dims = (((0,), (0,)), ((), ()))
    whole = n <= GLA_WHOLE_SEQ_CHUNKS
    if whole:
        rl = lax.broadcasted_iota(jnp.int32, (L, L), 0)
        cl = lax.broadcasted_iota(jnp.int32, (L, L), 1)
        same_chunk = (rl >> (C.bit_length() - 1)) == (cl >> (C.bit_length() - 1))

    for hh in range(hps):
        kc = slice(hh * GLA_DK, (hh + 1) * GLA_DK)
        vc = slice(hh * GLA_DV, (hh + 1) * GLA_DV)
        for direction, gk_ref in enumerate((gkf_ref, gkb_ref)):
            keep = (ci <= ri) if direction == 0 else (ci >= ri)
            last = C - 1 if direction == 0 else 0
            gk_hi, gk_lo = _split(gk_ref[:, kc])
            b_all = _dot(tri_ref[direction], jnp.concatenate([gk_hi, gk_lo], axis=1))
            b_all = b_all[:, :GLA_DK] + b_all[:, GLA_DK:]
            if whole:
                qe_all = (q_ref[:, kc] * jnp.exp(b_all)).astype(BF16)
                ke_all = (k_ref[:, kc] * jnp.exp(-b_all)).astype(BF16)
                keep_all = same_chunk & ((cl <= rl) if direction == 0 else (cl >= rl))
                s_all = lax.dot_general(qe_all, ke_all, nt_dims, preferred_element_type=F32)
                s_all = jnp.where(keep_all, s_all, 0.0).astype(BF16)
                o_intra = _dot(s_all, v_ref[:, vc].astype(BF16))
            st = s0_ref[direction, hh].T if has_s0 else jnp.zeros((GLA_DV, GLA_DK), F32)
            order = range(n) if direction == 0 else range(n - 1, -1, -1)
            for c in order:
                rows = slice(c * C, (c + 1) * C)
                b = b_all[rows]
                b_last = b[last:last + 1, :]
                k = k_ref[rows, kc]
                v = v_ref[rows, vc].astype(BF16)
                kd = (k * jnp.exp(b_last - b)).astype(BF16)
                if whole:
                    qe = qe_all[rows]
                    o = o_intra[rows]
                else:
                    qe = (q_ref[rows, kc] * jnp.exp(b)).astype(BF16)
                    ke = (k * jnp.exp(-b)).astype(BF16)
                    scores = lax.dot_general(qe, ke, nt_dims, preferred_element_type=F32)
                    o = _dot(jnp.where(keep, scores, 0.0).astype(BF16), v)
                o = o + lax.dot_general(qe, st.astype(BF16), nt_dims, preferred_element_type=F32)
                if direction == 0:
                    acc[rows, vc] = o
                else:
                    acc[rows, vc] = acc[rows, vc] + o
                st = jnp.exp(b_last) * st + lax.dot_general(v, kd, tn_dims, preferred_element_type=F32)
            sf_ref[direction, hh] = st.T

        o = acc[:, vc]
        o = o * lax.rsqrt(jnp.mean(o * o, axis=-1, keepdims=True) + EPS) * ng_ref[...]
        o_ref[:, vc] = (o * g_ref[:, vc]).astype(BF16)


def _gla_core(proj, tri, norm_g, s0, L, n_batch, row_off, hps):
    rb = row_off // L
    H = GLA_H
    nh = H // hps
    has_s0 = s0 is not None
    kern = functools.partial(_gla_core_kernel, L=L, has_s0=has_s0, hps=hps)
    kb, vb = GLA_DK * hps, GLA_DV * hps
    in_specs = [
        pl.BlockSpec((L, kb), lambda b, h: (rb + b, h)),
        pl.BlockSpec((L, kb), lambda b, h: (rb + b, nh + h)),
        pl.BlockSpec((L, vb), lambda b, h: (rb + b, (2 * H * GLA_DK) // vb + h)),
        pl.BlockSpec((L, vb), lambda b, h: (rb + b, (2 * H * GLA_DK) // vb + nh + h)),
        pl.BlockSpec((L, kb), lambda b, h: (rb + b, GLA_PROJ // kb + h)),
        pl.BlockSpec((L, kb), lambda b, h: (rb + b, GLA_PROJ // kb + nh + h)),
        _full_spec((2, L, L)),
        _vec_spec(GLA_DV),
    ]
    args = [proj] * 6 + [tri, norm_g.reshape(1, GLA_DV)]
    state_spec = pl.BlockSpec((None, 2, hps, GLA_DK, GLA_DV), lambda b, h: (b, 0, h, 0, 0))
    if has_s0:
        in_specs.append(state_spec)
        args.append(s0)
    return pl.pallas_call(
        kern,
        grid=(n_batch, nh),
        in_specs=in_specs,
        out_specs=[pl.BlockSpec((L, vb), lambda b, h: (b, h)), state_spec],
        out_shape=[
            jax.ShapeDtypeStruct((n_batch * L, H * GLA_DV), BF16),
            jax.ShapeDtypeStruct((n_batch, 2, H, GLA_DK, GLA_DV), F32),
        ],
        scratch_shapes=[pltpu.VMEM((L, vb), F32)],
        compiler_params=_cparams(2),
        name=f"gla_core_L{L}",
    )(*args)


def _fnet_mats(L):
    c = np.arange(FNET_C)
    ang_c = 2.0 * np.pi * np.outer(c, c) / FNET_C
    chan = np.concatenate([np.cos(ang_c), np.sin(ang_c)], axis=1) / math.sqrt(FNET_C)
    t = np.arange(L)
    ang_l = 2.0 * np.pi * np.outer(t, t) / L
    seq = np.concatenate([np.cos(ang_l), -np.sin(ang_l)], axis=1) / math.sqrt(L)
    return jnp.asarray(chan, F32).astype(BF16), jnp.asarray(seq, F32).astype(BF16)


def _fnet_kernel(x_ref, g_ref, sc_ref, sh_ref, gate_ref, chan_ref, seq_ref, w_ref, b_ref, *rest):
    *pending, o_ref = rest
    x = _read_x(x_ref, pending)
    h = _norm_mod(x, g_ref[...], sc_ref[...], sh_ref[...]).astype(BF16)
    chan = chan_ref[...]
    cos_parts, sin_parts = [], []
    for gi in range(FNET_GROUPS):
        cs = _dot(h[:, gi * FNET_C:(gi + 1) * FNET_C], chan)
        cos_parts.append(cs[:, :FNET_C])
        sin_parts.append(cs[:, FNET_C:])
    stacked = jnp.concatenate([jnp.concatenate(cos_parts, axis=1), jnp.concatenate(sin_parts, axis=1)], axis=0)
    mixed = _dot(seq_ref[...], stacked.astype(BF16))
    y = _dot(mixed.astype(BF16), w_ref[...]) + b_ref[...]
    o_ref[...] = x + gate_ref[...] * y


def _fnet(x, mod, layer, norm_g, chan, seq, w_bf16, bias, L, n_batch, row_off, pend=None):
    rb = row_off // L
    lat = row_off > 0

    def row_fn(b):
        return 1 + b if lat else 0

    p_specs, p_args = _pending_specs(pend, mod, L, lambda b: (rb + b, 0), row_fn)
    return pl.pallas_call(
        _fnet_kernel,
        grid=(n_batch,),
        in_specs=[
            pl.BlockSpec((L, D), lambda b: (rb + b, 0)),
            _vec_spec(D),
            _mod_spec(layer, 1, row_fn),
            _mod_spec(layer, 0, row_fn),
            _mod_spec(layer, 2, row_fn),
            _full_spec((FNET_C, 2 * FNET_C)),
            _full_spec((L, 2 * L)),
            _full_spec((D, D)),
            _vec_spec(D),
        ] + p_specs,
        out_specs=pl.BlockSpec((L, D), lambda b: (rb + b, 0)),
        out_shape=jax.ShapeDtypeStruct((N_TOK, D), F32),
        input_output_aliases={0: 0},
        compiler_params=_cparams(1),
        name=f"fnet_L{L}",
    )(x, norm_g.reshape(1, D), mod, mod, mod, chan, seq, w_bf16, bias.reshape(1, D), *p_args)


def _window_bounds(n, k):
    t = np.arange(n)
    lo, hi = k // 2, k - k // 2 - 1
    return np.maximum(t - lo, 0), np.minimum(t + hi + 1, n)


def _pool_mats(L, grid_rows):
    mats, inv = [], []
    for k in POOL_WINDOWS:
        if grid_rows is None:
            s, e = _window_bounds(L, k)
            idx = np.arange(L)[None, :]
            m = ((idx >= s[:, None]) & (idx < e[:, None])).astype(np.float64)
            cnt = (e - s).astype(np.float64)
        else:
            sr, er = _window_bounds(grid_rows, k)
            sc, ec = _window_bounds(GRID_W, k)
            ir = np.arange(grid_rows)[None, :]
            ic = np.arange(GRID_W)[None, :]
            mr = ((ir >= sr[:, None]) & (ir < er[:, None])).astype(np.float64)
            mc = ((ic >= sc[:, None]) & (ic < ec[:, None])).astype(np.float64)
            m = np.kron(mr, mc)
            cnt = np.kron((er - sr).astype(np.float64), (ec - sc).astype(np.float64))
        mats.append(m)
        inv.append(1.0 / cnt)
    return jnp.asarray(np.stack(mats), BF16), jnp.asarray(np.stack(inv)[:, :, None], F32)


def _pool_kernel(x_ref, g_ref, sc_ref, sh_ref, gate_ref, m_ref, ic_ref, w_ref, b_ref, ps_ref, *rest):
    *pending, o_ref = rest
    x = _read_x(x_ref, pending)
    h = _norm_mod(x, g_ref[...], sc_ref[...], sh_ref[...])
    outs = []
    for gi in range(len(POOL_WINDOWS)):
        hg = h[:, gi * POOL_G:(gi + 1) * POOL_G]
        hi, lo = _split(hg)
        m = m_ref[gi]
        mean = (_dot(m, hi) + _dot(m, lo)) * ic_ref[gi]
        outs.append(_dot((mean - hg).astype(BF16), w_ref[gi]))
    y = (jnp.concatenate(outs, axis=1) + b_ref[...]) * ps_ref[...]
    o_ref[...] = x + gate_ref[...] * y


def _pool(x, mod, layer, norm_g, mats, inv_cnt, w_bf16, bias, scale, L, n_batch, row_off, pend=None):
    rb = row_off // L
    lat = row_off > 0
    G = len(POOL_WINDOWS)

    def row_fn(b):
        return 1 + b if lat else 0

    p_specs, p_args = _pending_specs(pend, mod, L, lambda b: (rb + b, 0), row_fn)
    return pl.pallas_call(
        _pool_kernel,
        grid=(n_batch,),
        in_specs=[
            pl.BlockSpec((L, D), lambda b: (rb + b, 0)),
            _vec_spec(D),
            _mod_spec(layer, 1, row_fn),
            _mod_spec(layer, 0, row_fn),
            _mod_spec(layer, 2, row_fn),
            _full_spec((G, L, L)),
            _full_spec((G, L, 1)),
            _full_spec((G, POOL_G, POOL_G)),
            _vec_spec(D),
            _vec_spec(D),
        ] + p_specs,
        out_specs=pl.BlockSpec((L, D), lambda b: (rb + b, 0)),
        out_shape=jax.ShapeDtypeStruct((N_TOK, D), F32),
        input_output_aliases={0: 0},
        compiler_params=_cparams(1),
        name=f"pool_L{L}",
    )(x, norm_g.reshape(1, D), mod, mod, mod, mats, inv_cnt, w_bf16, bias.reshape(1, D), scale.reshape(1, D),
      *p_args)


ROUTER_PAD = 128
LANES = 128
D_EXT = D + LANES
MOE_TILE = 1024
MOE_MAX_TILES = N_TOK // MOE_TILE + MOE_GROUPS
MOE_ROWS = MOE_MAX_TILES * MOE_TILE
MOE_TILE_SHIFT = MOE_TILE.bit_length() - 1
assert 1 << MOE_TILE_SHIFT == MOE_TILE
MOE_ROW_BLOCK = 256
MOE_DMA_CHUNK = 64
MOE_CHUNK_SHIFT = MOE_DMA_CHUNK.bit_length() - 1
assert 1 << MOE_CHUNK_SHIFT == MOE_DMA_CHUNK and MOE_MAX_TILES % 2 == 0
MOE_Y_ROWS = N_TOK + 2 * MOE_DMA_CHUNK


ROUTE_ROWS = 8


def _moe_route_kernel(x_ref, g_ref, sc_ref, sh_ref, wr_ref, tri_ref, h3_ref, route_ref, cnt_ref, carry):
    t = pl.program_id(0)
    refs = (x_ref, g_ref, sc_ref, sh_ref, wr_ref, tri_ref, h3_ref, route_ref, cnt_ref, carry)
    pl.when(t < N_TOK // TM)(functools.partial(_moe_route_tile, t, *refs))

    @pl.when(t == N_TOK // TM)
    def _():
        h3_ref[...] = jnp.zeros_like(h3_ref)


def _moe_route_tile(t, x_ref, g_ref, sc_ref, sh_ref, wr_ref, tri_ref, h3_ref, route_ref, cnt_ref, carry):
    @pl.when(t == 0)
    def _():
        carry[...] = jnp.zeros_like(carry)

    h = _norm_mod(x_ref[...], g_ref[...], sc_ref[...], sh_ref[...])
    w_hi, w_lo = _split(wr_ref[...])
    h_hi, h_lo = _split(h)
    nt = (((1,), (1,)), ((), ()))
    logits = (lax.dot_general(w_hi, h_hi, nt, preferred_element_type=F32)
              + (lax.dot_general(w_hi, h_lo, nt, preferred_element_type=F32)
                 + lax.dot_general(w_lo, h_hi, nt, preferred_element_type=F32)))
    neg = jnp.float32(-jnp.inf)
    r8 = lax.broadcasted_iota(jnp.int32, (ROUTE_ROWS, TM), 0)
    r16 = lax.broadcasted_iota(jnp.int32, (MOE_E, TM), 0)
    gl = jnp.where(r8 < MOE_GROUPS, logits[MOE_E:MOE_E + ROUTE_ROWS], neg)
    g_max = jnp.max(gl, axis=0, keepdims=True)
    g_idx = jnp.min(jnp.where(gl == g_max, r8, ROUTE_ROWS), axis=0, keepdims=True)
    p_grp = 1.0 / jnp.sum(jnp.exp(gl - g_max), axis=0, keepdims=True)
    in_grp = (r16 >> 2) == g_idx
    el = jnp.where(in_grp, logits[:MOE_E], neg)
    m1 = jnp.max(el, axis=0, keepdims=True)
    i1 = jnp.min(jnp.where(el == m1, r16, MOE_E), axis=0, keepdims=True)
    z = jnp.sum(jnp.exp(el - m1), axis=0, keepdims=True)
    el2 = jnp.where(r16 == i1, neg, el)
    m2 = jnp.max(el2, axis=0, keepdims=True)
    i2 = jnp.min(jnp.where(el2 == m2, r16, MOE_E), axis=0, keepdims=True)
    p1 = 1.0 / z
    p2 = jnp.exp(m2 - m1) / z
    tot = p1 + p2
    eid = r8 + MOE_PER_GROUP * g_idx
    in4 = r8 < MOE_PER_GROUP
    cw4 = (jnp.where(in4 & (eid == i1), p_grp * (p1 / tot), 0.0)
           + jnp.where(in4 & (eid == i2), p_grp * (p2 / tot), 0.0))
    member = jnp.where(r8 == g_idx, 1.0, 0.0)
    before = _dot(member.astype(BF16), tri_ref[...]) + carry[:, 0:1]
    rank = jnp.sum(jnp.where(r8 == g_idx, before, 0.0), axis=0, keepdims=True)
    carry[...] = carry[...] + jnp.sum(member, axis=1, keepdims=True)
    cnt_ref[...] = carry[...].astype(jnp.int32)
    route_ref[...] = jnp.where(r8 == 0, g_idx, jnp.where(r8 == 1, rank.astype(jnp.int32), 0))
    h3_ref[:, :D] = h
    cw_rows = jnp.concatenate([cw4, jnp.zeros((LANES - ROUTE_ROWS, TM), F32)], axis=0)
    h3_ref[:, D:] = cw_rows.T


def _moe_route(x, mod, layer, norm_g, w_router_t, tri):
    nt = N_TOK // TM

    def tok_tile(t):
        return jnp.minimum(t, nt - 1)

    return pl.pallas_call(
        _moe_route_kernel,
        grid=(nt + 1,),
        in_specs=[
            pl.BlockSpec((TM, D), lambda t: (tok_tile(t), 0)),
            _vec_spec(D),
            _mod_spec(layer, 4, lambda t: _row_tm(tok_tile(t))),
            _mod_spec(layer, 3, lambda t: _row_tm(tok_tile(t))),
            _full_spec((ROUTER_PAD, D)),
            _full_spec((TM, TM)),
        ],
        out_specs=[
            pl.BlockSpec((TM, D_EXT), lambda t: (t, 0)),
            pl.BlockSpec((ROUTE_ROWS, TM), lambda t: (0, tok_tile(t))),
            pl.BlockSpec((ROUTE_ROWS, LANES), lambda t: (0, 0)),
        ],
        out_shape=[
            jax.ShapeDtypeStruct((N_TOK + TM, D_EXT), F32),
            jax.ShapeDtypeStruct((ROUTE_ROWS, N_TOK), jnp.int32),
            jax.ShapeDtypeStruct((ROUTE_ROWS, LANES), jnp.int32),
        ],
        scratch_shapes=[pltpu.VMEM((ROUTE_ROWS, LANES), F32)],
        compiler_params=_cparams(1),
        name="moe_route",
    )(x, norm_g.reshape(1, D), mod, mod, w_router_t, tri)


def _moe_invert_kernel(pos_ref, lo_ref, hi_ref, src_ref):
    def mark(j, carry):
        parity = lax.shift_right_logical(j, jnp.int32(MOE_TILE_SHIFT - MOE_CHUNK_SHIFT)) & 1
        first = N_TOK + MOE_DMA_CHUNK * parity
        base = j * MOE_DMA_CHUNK
        for rr in range(MOE_DMA_CHUNK):
            src_ref[base + rr] = first + rr
        return carry

    for g in range(MOE_GROUPS + 1):
        lax.fori_loop(lo_ref[g], hi_ref[g], mark, 0)

    def place(n, carry):
        src_ref[pos_ref[n]] = n
        return carry

    lax.fori_loop(0, N_TOK, place, 0, unroll=16)


def _moe_invert(pos, mark_lo, mark_hi):
    smem = pl.BlockSpec(memory_space=pltpu.SMEM)
    return pl.pallas_call(
        _moe_invert_kernel,
        in_specs=[smem, smem, smem],
        out_specs=smem,
        out_shape=jax.ShapeDtypeStruct((MOE_ROWS,), jnp.int32),
        name="moe_invert",
    )(pos, mark_lo, mark_hi)


def _moe_expert_kernel(src_ref, grp_ref, nact_ref, nchunk_ref, h_hbm, w1_ref, w3_ref, w2_ref, y_hbm,
                       xb0, xb1, ab0, ab1, gsem, ssem):
    t = pl.program_id(0)
    k = pl.program_id(1)
    n_active = nact_ref[0]
    last = n_active - 1
    T = MOE_TILE
    CH = MOE_DMA_CHUNK
    xbufs, accs = (xb0, xb1), (ab0, ab1)

    def gather_row(tile, slot, base, rr):
        tok = src_ref[tile * T + base + rr]
        rows = xbufs[slot].at[pl.ds(base, CH), :]
        return pltpu.make_async_copy(h_hbm.at[pl.ds(tok, 1), :], rows.at[pl.ds(rr, 1), :], gsem.at[slot])

    def scatter_row(tile, slot, base, rr):
        dst = src_ref[tile * T + base + rr]
        rows = accs[slot].at[pl.ds(base, CH), :]
        return pltpu.make_async_copy(rows.at[pl.ds(rr, 1), :], y_hbm.at[pl.ds(dst, 1), :], ssem.at[slot])

    def start_rows(make, tile, slot):
        def chunk(c, carry):
            base = pl.multiple_of(c * CH, CH)
            for rr in range(CH):
                make(tile, slot, base, rr).start()
            return carry

        lax.fori_loop(0, nchunk_ref[tile], chunk, 0)

    def wait_rows(tile, slot, gather):
        def chunk(c, carry):
            if gather:
                pltpu.make_async_copy(h_hbm.at[pl.ds(0, CH), :], xbufs[slot].at[pl.ds(0, CH), :], gsem.at[slot]).wait()
            else:
                pltpu.make_async_copy(accs[slot].at[pl.ds(0, CH), :], y_hbm.at[pl.ds(0, CH), :], ssem.at[slot]).wait()
            return carry

        lax.fori_loop(0, nchunk_ref[tile], chunk, 0)

    def step(slot):
        other = 1 - slot
        xb, acc = xbufs[slot], accs[slot]

        @pl.when(k == 0)
        def _():
            if slot == 0:
                @pl.when(t == 0)
                def _():
                    xb0[...] = jnp.zeros_like(xb0)
                    xb1[...] = jnp.zeros_like(xb1)
                    ab0[...] = jnp.zeros_like(ab0)
                    ab1[...] = jnp.zeros_like(ab1)
                    dump = pltpu.make_async_copy(ab0.at[pl.ds(0, 2 * CH), :], y_hbm.at[pl.ds(N_TOK, 2 * CH), :],
                                                 ssem.at[0])
                    dump.start()
                    dump.wait()
                    start_rows(gather_row, 0, 0)

            wait_rows(t, slot, True)

            @pl.when(t < last)
            def _():
                start_rows(gather_row, t + 1, other)

            @pl.when(t >= 2)
            def _():
                wait_rows(t - 2, slot, False)

        def experts(m):
            x = xb[:m, :D].astype(BF16)
            a = _dot(x, w1_ref[...].astype(BF16))
            b = _dot(x, w3_ref[...].astype(BF16))
            lane = lax.broadcasted_iota(jnp.int32, (m, LANES), 1)
            cwk = jnp.sum(jnp.where(lane == k, xb[:m, D:], 0.0), axis=-1, keepdims=True)
            hid = (_silu(a) * b * cwk).astype(BF16)
            acc[:m] = jnp.where(k > 0, acc[:m], 0.0) + _dot(hid, w2_ref[...].astype(BF16))

        blocks = (nchunk_ref[t] * CH + MOE_ROW_BLOCK - 1) // MOE_ROW_BLOCK
        for nb in range(1, T // MOE_ROW_BLOCK + 1):
            pl.when(blocks == nb)(functools.partial(experts, nb * MOE_ROW_BLOCK))

        @pl.when(k == MOE_PER_GROUP - 1)
        def _():
            start_rows(scatter_row, t, slot)

            @pl.when(t == last)
            def _():
                wait_rows(t, slot, False)

                @pl.when(t >= 1)
                def _():
                    wait_rows(t - 1, other, False)

    for slot in (0, 1):
        pl.when((t < n_active) & (t % 2 == slot))(functools.partial(step, slot))


def _moe_experts(h_ext, src, tile_group, n_active, n_chunk, layer, w1, w3, w2):
    T = MOE_TILE

    def w_index(t, k, src_ref, grp_ref, nact_ref, nchunk_ref):
        last = nact_ref[0] - 1
        e = jnp.where(t <= last, grp_ref[t] * MOE_PER_GROUP + k, grp_ref[last] * MOE_PER_GROUP + MOE_PER_GROUP - 1)
        return (layer, e, 0, 0)

    grid_spec = pltpu.PrefetchScalarGridSpec(
        num_scalar_prefetch=4,
        grid=(MOE_MAX_TILES, MOE_PER_GROUP),
        in_specs=[
            pl.BlockSpec(memory_space=pl.ANY),
            pl.BlockSpec((None, None, D, MOE_HID), w_index),
            pl.BlockSpec((None, None, D, MOE_HID), w_index),
            pl.BlockSpec((None, None, MOE_HID, D), w_index),
        ],
        out_specs=pl.BlockSpec(memory_space=pl.ANY),
        scratch_shapes=[
            pltpu.VMEM((T, D_EXT), F32),
            pltpu.VMEM((T, D_EXT), F32),
            pltpu.VMEM((T, D), F32),
            pltpu.VMEM((T, D), F32),
            pltpu.SemaphoreType.DMA((2,)),
            pltpu.SemaphoreType.DMA((2,)),
        ],
    )
    return pl.pallas_call(
        _moe_expert_kernel,
        grid_spec=grid_spec,
        out_shape=jax.ShapeDtypeStruct((MOE_Y_ROWS, D), F32),
        compiler_params=_cparams(2),
        name="moe_experts",
    )(src, tile_group, n_active, n_chunk, h_ext, w1, w3, w2)


def _moe_combine_kernel(y_ref, gate_ref, x_ref, o_ref):
    o_ref[...] = x_ref[...] + gate_ref[...] * y_ref[...]


def _moe_combine(x, y3, mod, layer):
    return pl.pallas_call(
        _moe_combine_kernel,
        grid=(N_TOK // TM,),
        in_specs=[
            pl.BlockSpec((TM, D), lambda t: (t, 0)),
            _mod_spec(layer, 5, _row_tm),
            pl.BlockSpec((TM, D), lambda t: (t, 0)),
        ],
        out_specs=pl.BlockSpec((TM, D), lambda t: (t, 0)),
        out_shape=jax.ShapeDtypeStruct((N_TOK, D), F32),
        input_output_aliases={2: 0},
        compiler_params=_cparams(1),
        name="moe_combine",
    )(y3, mod, x)


def _moe(x, mod, layer, norm_g, w_rg, w_re, w1, w3, w2, tri, final_g=None):
    w_router_t = jnp.zeros((ROUTER_PAD, D), F32).at[:MOE_E].set(w_re.T).at[MOE_E:MOE_E + MOE_GROUPS].set(w_rg.T)
    h3, route, counts = _moe_route(x, mod, layer, norm_g, w_router_t, tri)
    cnt = counts[:MOE_GROUPS, 0]
    ntile = (cnt + MOE_TILE - 1) // MOE_TILE
    tile_end = jnp.cumsum(ntile)
    seg_start = (tile_end - ntile) * MOE_TILE
    g_idx, rank = route[0], route[1]
    pos = jnp.sum(jnp.where(g_idx[None, :] == jnp.arange(MOE_GROUPS)[:, None], seg_start[:, None], 0), axis=0) + rank
    tiles = jnp.arange(MOE_MAX_TILES, dtype=jnp.int32)
    tile_group = jnp.minimum(jnp.sum(tiles[:, None] >= tile_end[None, :], axis=1), MOE_GROUPS - 1).astype(jnp.int32)
    n_active = tile_end[-1:].astype(jnp.int32)
    seg_end = tile_end * MOE_TILE
    mark_lo = jnp.concatenate([(seg_start + cnt) // MOE_DMA_CHUNK, seg_end[-1:] // MOE_DMA_CHUNK])
    mark_hi = jnp.concatenate([seg_end // MOE_DMA_CHUNK, jnp.full((1,), MOE_ROWS // MOE_DMA_CHUNK)])
    src = _moe_invert(pos.astype(jnp.int32), mark_lo.astype(jnp.int32), mark_hi.astype(jnp.int32))
    first_tile = (tile_end - ntile)[tile_group]
    real_rows = jnp.clip(cnt[tile_group] - (tiles - first_tile) * MOE_TILE, 0, MOE_TILE)
    n_chunk = ((real_rows + MOE_DMA_CHUNK - 1) // MOE_DMA_CHUNK).astype(jnp.int32)
    y3 = _moe_experts(h3, src, tile_group, n_active, n_chunk, layer, w1, w3, w2)
    if final_g is None:
        return y3
    return tuple(_moe_combine_norm(x, y3, mod, layer, final_g, off, nb * L) for L, nb, off in
                 ((CTX_L, CTX_B, 0), (LAT_L, LAT_B, N_CTX)))


def _combine_norm_kernel(y_ref, gate_ref, x_ref, g_ref, o_ref):
    x = x_ref[...] + gate_ref[...] * y_ref[...]
    o_ref[...] = x * lax.rsqrt(jnp.mean(x * x, axis=-1, keepdims=True) + EPS) * g_ref[...]


def _moe_combine_norm(x, y3, mod, layer, final_g, row_off, n_rows):
    off = row_off // TM
    first_lat = N_CTX // TM

    def row_fn(t):
        g = t + off
        return jnp.where(g < first_lat, 0, 1 + (g - first_lat) // (LAT_L // TM))

    return pl.pallas_call(
        _combine_norm_kernel,
        grid=(n_rows // TM,),
        in_specs=[
            pl.BlockSpec((TM, D), lambda t: (t + off, 0)),
            _mod_spec(layer, 5, row_fn),
            pl.BlockSpec((TM, D), lambda t: (t + off, 0)),
            _vec_spec(D),
        ],
        out_specs=pl.BlockSpec((TM, D), lambda t: (t, 0)),
        out_shape=jax.ShapeDtypeStruct((n_rows, D), F32),
        compiler_params=_cparams(1),
        name="combine_final_norm",
    )(y3, mod, x, final_g.reshape(1, D))


def kernel(x_prompt, x_sample, state_gla, c, c_ctx, w_ada, b_ada, norm_g, hy_w_in, hy_b_in, hy_conv_w, hy_conv_b, hy_f_w1, hy_f_b1, hy_f_freq, hy_f_w2, hy_f_b2, hy_f_w3, hy_skip, hy_w_out, hy_b_out, gla_w_q, gla_w_k, gla_w_v, gla_w_g, gla_w_gk1, gla_w_gk2, gla_b_gk, gla_norm_g, gla_w_o, fn_w_out, fn_b_out, pool_w, pool_b, pool_scale, moe_w_rg, moe_w_re, moe_w1, moe_w3, moe_w2, final_g):
    groups = ((CTX_L, CTX_B, 0, None), (LAT_L, LAT_B, N_CTX, LAT_L // GRID_W))

    x_ctx, x_lat = x_prompt.reshape(N_CTX, D), x_sample.reshape(N_LAT, D)
    x = None
    cond =jnp.zeros((MOD_ROWS, D), F32).at[0].set(c_ctx).at[1:1 + LAT_B].set(c)
    mod = _ada_table(cond, w_ada, b_ada).reshape(DEPTH * MOD_ROWS * 6, 1, D)

    tri_tm = jnp.asarray(np.triu(np.ones((TM, TM)), 1), BF16)

    new_states = []
    pend = None
    for i in range(DEPTH):
        kind, j = i % 4, i // 4
        if kind == 0:
            if i > 0:
                x = _moe_combine(x, pend[0], mod, pend[1])
                x_ctx, x_lat = x[:N_CTX], x[N_CTX:]
            u = _hyena_in(x_ctx, x_lat, mod, i, norm_g[i, 0], hy_w_in[j].astype(BF16), hy_b_in[j], hy_conv_w[j],
                          hy_conv_b[j])
            zs = []
            for L, nb, off, _ in groups:
                fwd, ff, inv = _dft_mats(L)
                khat = _hyena_filters(L, ff, hy_f_w1[j], hy_f_b1[j], hy_f_freq[j], hy_f_w2[j], hy_f_b2[j],
                                      hy_f_w3[j])
                zs.append(_hyena_conv(u, khat, hy_skip[j], fwd, inv, L, nb, off, D if L == CTX_L else 512))
            x = _outproj_joint(x_ctx, x_lat, zs[0], zs[1], hy_w_out[j].astype(BF16), hy_b_out[j], mod, i)
        elif kind == 1:
            w_cat = jnp.concatenate([gla_w_q[j], gla_w_k[j], gla_w_v[j], gla_w_g[j]], axis=1).astype(BF16)
            nk = GLA_H * GLA_DK
            wg1 = jnp.zeros((D, GK1_PAD), F32).at[:, :GLA_RANK].set(gla_w_gk1[j, 0])
            wg1 = wg1.at[:, GLA_RANK:2 * GLA_RANK].set(gla_w_gk1[j, 1]).astype(BF16)
            wg2 = jnp.zeros((GK1_PAD, 2 * nk), F32).at[:GLA_RANK, :nk].set(gla_w_gk2[j, 0])
            wg2 = wg2.at[GLA_RANK:2 * GLA_RANK, nk:].set(gla_w_gk2[j, 1]).astype(BF16)
            proj = _gla_proj(x, mod, i, norm_g[i, 0], w_cat, wg1, wg2, gla_b_gk[j].reshape(1, 2 * nk), pend)
            lower = np.tril(np.ones((GLA_CHUNK, GLA_CHUNK)))
            w_o = gla_w_o[j].astype(BF16)
            for L, nb, off, grid_rows in groups:
                eye = np.eye(L // GLA_CHUNK)
                tri = jnp.asarray(np.stack([np.kron(eye, lower), np.kron(eye, lower.T)]), BF16)
                s0 = None if grid_rows is None else state_gla[:, j]
                o, s_fin = _gla_core(proj, tri, gla_norm_g[j], s0, L, nb, off, GLA_H if L == CTX_L else 1)
                if grid_rows is None:
                    new_states.append(s_fin)
                x = _outproj(x, o, w_o, jnp.zeros((D,), F32), mod, i, off, nb * L, pend)
        elif kind == 2:
            w_out = fn_w_out[j].astype(BF16)
            for L, nb, off, _ in groups:
                chan, seq = _fnet_mats(L)
                x = _fnet(x, mod, i, norm_g[i, 0], chan, seq, w_out, fn_b_out[j], L, nb, off, pend)
        else:
            w_pool = pool_w[j].astype(BF16)
            for L, nb, off, grid_rows in groups:
                mats, inv_cnt = _pool_mats(L, grid_rows)
                x = _pool(x, mod, i, norm_g[i, 0], mats, inv_cnt, w_pool, pool_b[j], pool_scale[j], L, nb, off,
                          pend)

        out = _moe(x, mod, i, norm_g[i, 1], moe_w_rg[i], moe_w_re[i], moe_w1, moe_w3, moe_w2, tri_tm,
                   final_g if i == DEPTH - 1 else None)
        pend = (out, i)

    y_prompt, y_sample = out
    new_state_gla = jnp.stack(new_states, axis=1)
    return (y_prompt.reshape(CTX_B, CTX_L, D), y_sample.reshape(LAT_B, LAT_L, D), new_state_gla)
```

```python
import functools
import math

import jax
import jax.numpy as jnp
import numpy as np
from jax import lax
from jax.experimental import pallas as pl
from jax.experimental.pallas import tpu as pltpu

F32 = jnp.float32
BF16 = jnp.bfloat16

D = 1024
CTX_B, CTX_L = 32, 256
LAT_B, LAT_L = 2, 1024
N_CTX = CTX_B * CTX_L
N_LAT = LAT_B * LAT_L
N_TOK = N_CTX + N_LAT
DEPTH = 4
GRID_W = 64
EPS = 1e-6

HY_BANDS = 8
HY_EMB = 1 + 2 * HY_BANDS
HY_EMB_PAD = 32
HY_HID = 64
HY_FAST_DECAY = 0.3
HY_SLOW_DECAY = 1.5
HY_DECAY_TARGET = 1e-2

GLA_H = 4
GLA_DK = 128
GLA_DV = 256
GLA_RANK = 16
GLA_NORMALIZER = 16.0
GLA_CHUNK = 64
GLA_WHOLE_SEQ_CHUNKS = 4

FNET_GROUPS = 4
FNET_C = D // FNET_GROUPS
POOL_WINDOWS = (2, 4, 8, 16)
POOL_G = D // len(POOL_WINDOWS)

MOE_GROUPS = 4
MOE_PER_GROUP = 4
MOE_E = MOE_GROUPS * MOE_PER_GROUP
MOE_HID = D // 2

MOD_ROWS = 8
TM = 512
TM_BIG = 1024
VMEM_LIMIT = 56 * 1024 * 1024


def _cparams(n_axes):
    return pltpu.CompilerParams(dimension_semantics=("arbitrary",) * n_axes, vmem_limit_bytes=VMEM_LIMIT)


def _norm_mod(x, g, sc, sh):
    ms = jnp.mean(x * x, axis=-1, keepdims=True)
    return (x * lax.rsqrt(ms + EPS) * g) * (1.0 + sc) + sh


def _split(a):
    hi = a.astype(BF16)
    lo = (a - hi.astype(F32)).astype(BF16)
    return hi, lo


def _dot(a, b):
    return jnp.dot(a, b, preferred_element_type=F32)


def _dot_precise(a, b):
    a_hi, a_lo = _split(a)
    b_hi, b_lo = _split(b)
    return _dot(a_hi, b_hi) + (_dot(a_hi, b_lo) + _dot(a_lo, b_hi))


def _silu(x):
    return x * (1.0 / (1.0 + jnp.exp(-x)))


def _log_sigmoid(x):
    return jnp.minimum(x, 0.0) - jnp.log(1.0 + jnp.exp(-jnp.abs(x)))


def _mod_spec(layer, chunk, row_fn):
    base = layer * MOD_ROWS * 6 + chunk

    def index_map(*ids):
        return (base + row_fn(*ids) * 6, 0, 0)

    return pl.BlockSpec((None, 1, D), index_map)


def _row_tm(t, *_):
    return jnp.where(t < N_CTX // TM, 0, 1 + (t - N_CTX // TM) // (LAT_L // TM))


def _row_big(t, *_):
    return jnp.where(t < N_CTX // TM_BIG, 0, 1 + (t - N_CTX // TM_BIG) // (LAT_L // TM_BIG))


def _vec_spec(n):
    return pl.BlockSpec((1, n), lambda *ids: (0, 0))


def _full_spec(shape):
    nd = len(shape)
    return pl.BlockSpec(shape, lambda *ids: (0,) * nd)


def _ada_kernel(cond_ref, w_ref, b_ref, o_ref):
    s = _silu(cond_ref[...]).astype(BF16)
    o_ref[...] = _dot(s, w_ref[...].astype(BF16)) + b_ref[...]


def _ada_table(cond, w_ada, b_ada):
    tn = 1536
    return pl.pallas_call(
        _ada_kernel,
        grid=(DEPTH, 6 * D // tn),
        in_specs=[
            pl.BlockSpec((MOD_ROWS, D), lambda i, j: (0, 0)),
            pl.BlockSpec((None, D, tn), lambda i, j: (i, 0, j)),
            pl.BlockSpec((None, 1, tn), lambda i, j: (i, 0, j)),
        ],
        out_specs=pl.BlockSpec((None, MOD_ROWS, tn), lambda i, j: (i, 0, j)),
        out_shape=jax.ShapeDtypeStruct((DEPTH, MOD_ROWS, 6 * D), F32),
        compiler_params=_cparams(2),
        name="ada_table",
    )(cond, w_ada, b_ada.reshape(DEPTH, 1, 6 * D))


def _pending_specs(pend, mod, block_rows, row_index, row_fn):
    if pend is None:
        return [], []
    y, prev_layer = pend
    return [pl.BlockSpec((block_rows, D), row_index), _mod_spec(prev_layer, 5, row_fn)], [y, mod]


def _read_x(x_ref, pending_refs):
    x = x_ref[...]
    if pending_refs:
        y_ref, gate_ref = pending_refs
        x = x + gate_ref[...] * y_ref[...]
    return x


def _outproj_kernel(z_ref, w_ref, b_ref, gate_ref, x_ref, *rest):
    *pending, o_ref = rest
    y = _dot(z_ref[...], w_ref[...]) + b_ref[...]
    o_ref[...] = _read_x(x_ref, pending) + gate_ref[...] * y


def _outproj(x, z, w_bf16, bias, mod, layer, row_off, n_rows, pend=None):
    k = z.shape[1]
    off = row_off // TM
    first_lat = N_CTX // TM

    def row_fn(t):
        g = t + off
        return jnp.where(g < first_lat, 0, 1 + (g - first_lat) // (LAT_L // TM))

    p_specs, p_args = _pending_specs(pend, mod, TM, lambda t: (t + off, 0), row_fn)
    return pl.pallas_call(
        _outproj_kernel,
        grid=(n_rows // TM,),
        in_specs=[
            pl.BlockSpec((TM, k), lambda t: (t, 0)),
            _full_spec((k, D)),
            _vec_spec(D),
            _mod_spec(layer, 2, row_fn),
            pl.BlockSpec((TM, D), lambda t: (t + off, 0)),
        ] + p_specs,
        out_specs=pl.BlockSpec((TM, D), lambda t: (t + off, 0)),
        out_shape=jax.ShapeDtypeStruct((N_TOK, D), F32),
        input_output_aliases={4: 0},
        compiler_params=_cparams(1),
        name="outproj_residual",
    )(z, w_bf16, bias.reshape(1, D), mod, x, *p_args)


def _outproj_joint_kernel(zc_ref, zl_ref, w_ref, b_ref, gate_ref, xc_ref, xl_ref, o_ref):
    t = pl.program_id(0)
    for is_ctx, z_ref, x_ref in ((True, zc_ref, xc_ref), (False, zl_ref, xl_ref)):
        @pl.when((t < N_CTX // TM) == is_ctx)
        def _(z_ref=z_ref, x_ref=x_ref):
            y = _dot(z_ref[...], w_ref[...]) + b_ref[...]
            o_ref[...] = x_ref[...] + gate_ref[...] * y


def _outproj_joint(x_ctx, x_lat, z_ctx, z_lat, w_bf16, bias, mod, layer):
    k = z_ctx.shape[1]
    n_ctx_tiles = N_CTX // TM

    def ctx_block(t):
        return (jnp.minimum(t, n_ctx_tiles - 1), 0)

    def lat_block(t):
        return (jnp.maximum(t - n_ctx_tiles, 0), 0)

    return pl.pallas_call(
        _outproj_joint_kernel,
        grid=(N_TOK // TM,),
        in_specs=[
            pl.BlockSpec((TM, k), ctx_block),
            pl.BlockSpec((TM, k), lat_block),
            _full_spec((k, D)),
            _vec_spec(D),
            _mod_spec(layer, 2, _row_tm),
            pl.BlockSpec((TM, D), ctx_block),
            pl.BlockSpec((TM, D), lat_block),
        ],
        out_specs=pl.BlockSpec((TM, D), lambda t: (t, 0)),
        out_shape=jax.ShapeDtypeStruct((N_TOK, D), F32),
        compiler_params=_cparams(1),
        name="outproj_joint",
    )(z_ctx, z_lat, w_bf16, bias.reshape(1, D), mod, x_ctx, x_lat)


def _dft_mats(L):
    n2 = 2 * L
    k = np.arange(L)[:, None].astype(np.float64)
    n = np.arange(n2)[None, :].astype(np.float64)
    ang = 2.0 * np.pi * k * n / n2
    full = np.concatenate([np.cos(ang), -np.sin(ang)], axis=0)
    full[L, :] = np.cos(np.pi * np.arange(n2))
    fwd = full[:, :L]
    bwd = np.zeros((n2, L))
    bwd[:, 1:] = full[:, n2 - np.arange(1, L)]
    t = np.arange(L)[:, None].astype(np.float64)
    kk = np.arange(L)[None, :].astype(np.float64)
    ang_i = 2.0 * np.pi * t * kk / n2
    inv_re = np.cos(ang_i) / L
    inv_re[:, 0] = 1.0 / n2
    inv_im = -np.sin(ang_i) / L
    inv_im[:, 0] = np.cos(np.pi * np.arange(L)) / n2
    inv = np.concatenate([inv_re, inv_im], axis=1)
    return tuple(jnp.asarray(m, F32).astype(BF16) for m in (fwd, np.concatenate([fwd, bwd], axis=1), inv))


def _hyena_pos_emb(L):
    pos = np.arange(L, dtype=np.float64)
    bands = np.linspace(1e-4, HY_BANDS - 1, HY_BANDS)
    ang = (2.0 * np.pi * pos / L)[:, None] * bands[None, :]
    z = np.concatenate([(pos / L)[:, None], np.cos(ang), -np.sin(ang)], axis=-1)
    zp = np.zeros((L, HY_EMB_PAD))
    zp[:, :HY_EMB] = z
    return jnp.asarray(zp, F32)


def _hyena_filter_kernel(z_ref, w1_ref, b1_ref, fr_ref, w2_ref, b2_ref, w3f_ref, w3b_ref, ff_ref, o_ref, *, L, tn):
    j = pl.program_id(1)
    fr = fr_ref[...]
    f = jnp.sin(fr * (_dot_precise(z_ref[...], w1_ref[...]) + b1_ref[...]))
    f = jnp.sin(fr * (_dot_precise(f, w2_ref[...]) + b2_ref[...]))
    t_lin = lax.broadcasted_iota(jnp.int32, (L, tn), 0).astype(F32) / float(L - 1)
    ch = (lax.broadcasted_iota(jnp.int32, (L, tn), 1) + j * tn).astype(F32)
    max_decay = math.log(HY_DECAY_TARGET) / HY_FAST_DECAY
    min_decay = math.log(HY_DECAY_TARGET) / HY_SLOW_DECAY
    deltas = min_decay + ch * ((max_decay - min_decay) / float(D - 1))
    window = jnp.exp(-t_lin * jnp.abs(deltas))
    kf = _dot_precise(f, w3f_ref[...]) * window
    kb = _dot_precise(f, w3b_ref[...]) * window
    taps = jnp.concatenate([kf, kb], axis=0).astype(BF16)
    o_ref[...] = _dot(ff_ref[...], taps)


def _hyena_filters(L, ff, f_w1, f_b1, f_freq, f_w2, f_b2, f_w3):
    tn = 512
    nj = D // tn
    w1p = jnp.zeros((HY_EMB_PAD, HY_HID), F32).at[:HY_EMB].set(f_w1)
    kern = functools.partial(_hyena_filter_kernel, L=L, tn=tn)
    return pl.pallas_call(
        kern,
        grid=(2, nj),
        in_specs=[
            _full_spec((L, HY_EMB_PAD)),
            _full_spec((HY_EMB_PAD, HY_HID)),
            _vec_spec(HY_HID),
            _vec_spec(HY_HID),
            _full_spec((HY_HID, HY_HID)),
            _vec_spec(HY_HID),
            pl.BlockSpec((HY_HID, tn), lambda o, j: (0, o * nj + j)),
            pl.BlockSpec((HY_HID, tn), lambda o, j: (0, (2 + o) * nj + j)),
            _full_spec((2 * L, 2 * L)),
        ],
        out_specs=pl.BlockSpec((None, 2 * L, tn), lambda o, j: (o, 0, j)),
        out_shape=jax.ShapeDtypeStruct((2, 2 * L, D), F32),
        compiler_params=_cparams(2),
        name=f"hyena_filters_L{L}",
    )(_hyena_pos_emb(L), w1p, f_b1.reshape(1, -1), f_freq.reshape(1, -1), f_w2, f_b2.reshape(1, -1),
      f_w3, f_w3, ff)


def _hyena_in_kernel(xc_ref, xl_ref, g_ref, sc_ref, sh_ref, w_ref, b_ref, cw_ref, cb_ref, o_ref, h_scr):
    t = pl.program_id(0)
    first = pl.program_id(1) == 0

    for is_ctx, x_ref in ((True, xc_ref), (False, xl_ref)):
        @pl.when(first & ((t < N_CTX // TM_BIG) == is_ctx))
        def _(x_ref=x_ref):
            h_scr[...] = _norm_mod(x_ref[...], g_ref[...], sc_ref[...], sh_ref[...]).astype(BF16)

    u = _dot(h_scr[...], w_ref[...]) + b_ref[...]
    cw = cw_ref[...]
    o_ref[...] = pltpu.roll(u, 1, 0) * cw[0:1] + u * cw[1:2] + pltpu.roll(u, TM_BIG - 1, 0) * cw[2:3] + cb_ref[...]
    is_ctx = (t < N_CTX // TM_BIG).astype(F32)
    for start in range(0, TM_BIG, CTX_L):
        f = 1.0 if start % LAT_L == 0 else is_ctx
        before = (start - 1) % TM_BIG
        o_ref[start:start + 1, :] = o_ref[start:start + 1, :] - f * (u[before:before + 1] * cw[0:1])
        end = start + CTX_L - 1
        g = 1.0 if (end + 1) % LAT_L == 0 else is_ctx
        after = (end + 1) % TM_BIG
        o_ref[end:end + 1, :] = o_ref[end:end + 1, :] - g * (u[after:after + 1] * cw[2:3])


def _hyena_in(x_ctx, x_lat, mod, layer, norm_g, w_in_bf16, b_in, conv_w, conv_b):
    n_ctx_tiles = N_CTX // TM_BIG
    return pl.pallas_call(
        _hyena_in_kernel,
        grid=(N_TOK // TM_BIG, 3),
        in_specs=[
            pl.BlockSpec((TM_BIG, D), lambda t, p: (jnp.minimum(t, n_ctx_tiles - 1), 0)),
            pl.BlockSpec((TM_BIG, D), lambda t, p: (jnp.maximum(t - n_ctx_tiles, 0), 0)),
            _vec_spec(D),
            _mod_spec(layer, 1, _row_big),
            _mod_spec(layer, 0, _row_big),
            pl.BlockSpec((D, D), lambda t, p: (0, p)),
            pl.BlockSpec((1, D), lambda t, p: (0, p)),
            pl.BlockSpec((3, D), lambda t, p: (0, p)),
            pl.BlockSpec((1, D), lambda t, p: (0, p)),
        ],
        out_specs=pl.BlockSpec((TM_BIG, D), lambda t, p: (t, p)),
        out_shape=jax.ShapeDtypeStruct((N_TOK, 3 * D), F32),
        scratch_shapes=[pltpu.VMEM((TM_BIG, D), BF16)],
        compiler_params=_cparams(2),
        name="hyena_in",
    )(x_ctx, x_lat, norm_g.reshape(1, D), mod, mod, w_in_bf16, b_in.reshape(1, -1), conv_w, conv_b.reshape(1, -1))


def _hyena_conv_kernel(v_ref, x1_ref, x2_ref, kh_ref, skip_ref, fwd_ref, inv_ref, o_ref, *, L):
    fwd = fwd_ref[...]
    inv = inv_ref[...]
    row0 = lax.broadcasted_iota(jnp.int32, (L, v_ref.shape[1]), 0) == 0

    def long_conv(z, order):
        zh = _dot(fwd, z.astype(BF16))
        zr, zi = zh[:L], zh[L:]
        kr, ki = kh_ref[order, :L, :], kh_ref[order, L:, :]
        pr = jnp.where(row0, zr * kr, zr * kr - zi * ki)
        pi = jnp.where(row0, zi * ki, zr * ki + zi * kr)
        prod = jnp.concatenate([pr, pi], axis=0).astype(BF16)
        return _dot(inv, prod) + z * skip_ref[order:order + 1, :]

    z = x1_ref[...] * long_conv(v_ref[...], 0)
    z = x2_ref[...] * long_conv(z, 1)
    o_ref[...] = z.astype(BF16)


def _hyena_conv(u, khat, skip, fwd, inv, L, n_batch, row_off, tn):
    nj = D // tn
    rb = row_off // L
    kern = functools.partial(_hyena_conv_kernel, L=L)
    return pl.pallas_call(
        kern,
        grid=(nj, n_batch),
        in_specs=[
            pl.BlockSpec((L, tn), lambda j, b: (rb + b, j)),
            pl.BlockSpec((L, tn), lambda j, b: (rb + b, nj + j)),
            pl.BlockSpec((L, tn), lambda j, b: (rb + b, 2 * nj + j)),
            pl.BlockSpec((2, 2 * L, tn), lambda j, b: (0, 0, j)),
            pl.BlockSpec((2, tn), lambda j, b: (0, j)),
            _full_spec((2 * L, L)),
            _full_spec((L, 2 * L)),
        ],
        out_specs=pl.BlockSpec((L, tn), lambda j, b: (b, j)),
        out_shape=jax.ShapeDtypeStruct((n_batch * L, D), BF16),
        compiler_params=_cparams(2),
        name=f"hyena_conv_L{L}",
    )(u, u, u, khat, skip, fwd, inv)


GLA_PROJ = 2 * GLA_H * GLA_DK + 2 * GLA_H * GLA_DV
GLA_COLS = GLA_PROJ + 2 * GLA_H * GLA_DK
GK1_PAD = 128


def _gla_proj_kernel(x_ref, g_ref, sc_ref, sh_ref, w_ref, wg1_ref, wg2_ref, bg_ref, *rest):
    *pending, o_ref = rest
    h = _norm_mod(_read_x(x_ref, pending), g_ref[...], sc_ref[...], sh_ref[...]).astype(BF16)
    p = _dot(h, w_ref[...])
    nq = GLA_H * GLA_DK
    o_ref[:, 0:nq] = p[:, 0:nq] * (GLA_DK ** -0.5)
    o_ref[:, nq:nq + nq + GLA_H * GLA_DV] = p[:, nq:nq + nq + GLA_H * GLA_DV]
    o_ref[:, 2 * nq + GLA_H * GLA_DV:GLA_PROJ] = _silu(p[:, 2 * nq + GLA_H * GLA_DV:GLA_PROJ])
    low = _dot(h, wg1_ref[...]).astype(BF16)
    gk = _dot(low, wg2_ref[...]) + bg_ref[...]
    o_ref[:, GLA_PROJ:GLA_COLS] = _log_sigmoid(gk) / GLA_NORMALIZER


def _gla_proj(x, mod, layer, norm_g, w_cat, wg1, wg2, bg, pend=None):
    p_specs, p_args = _pending_specs(pend, mod, TM, lambda t: (t, 0), _row_tm)
    return pl.pallas_call(
        _gla_proj_kernel,
        grid=(N_TOK // TM,),
        in_specs=[
            pl.BlockSpec((TM, D), lambda t: (t, 0)),
            _vec_spec(D),
            _mod_spec(layer, 1, _row_tm),
            _mod_spec(layer, 0, _row_tm),
            _full_spec((D, GLA_PROJ)),
            _full_spec((D, GK1_PAD)),
            _full_spec((GK1_PAD, 2 * GLA_H * GLA_DK)),
            _vec_spec(2 * GLA_H * GLA_DK),
        ] + p_specs,
        out_specs=pl.BlockSpec((TM, GLA_COLS), lambda t: (t, 0)),
        out_shape=jax.ShapeDtypeStruct((N_TOK, GLA_COLS), F32),
        compiler_params=_cparams(1),
        name="gla_proj",
    )(x, norm_g.reshape(1, D), mod, mod, w_cat, wg1, wg2, bg, *p_args)


def _gla_core_kernel(*refs, L, has_s0, hps, n_proj):
    q_ref, k_ref, v_ref, g_ref, gkf_ref, gkb_ref, tri_ref, ng_ref = refs[:8]
    rest = list(refs[8:])
    s0_ref = rest.pop(0) if has_s0 else None
    proj_refs = [rest.pop(0) for _ in range(n_proj)]
    o_ref, sf_ref, acc = rest[:3]
    o_gated = rest[3] if n_proj else o_ref
    C = GLA_CHUNK
    n = L // C
    ri = lax.broadcasted_iota(jnp.int32, (C, C), 0)
    ci = lax.broadcasted_iota(jnp.int32, (C, C), 1)
    nt_dims = (((1,), (1,)), ((), ()))
    tn_dims = (((0,), (0,)), ((), ()))
    whole = n <= GLA_WHOLE_SEQ_CHUNKS
    if whole:
        rl = lax.broadcasted_iota(jnp.int32, (L, L), 0)
        cl = lax.broadcasted_iota(jnp.int32, (L, L), 1)
        same_chunk = (rl >> (C.bit_length() - 1)) == (cl >> (C.bit_length() - 1))

    for hh in range(hps):
        kc = slice(hh * GLA_DK, (hh + 1) * GLA_DK)
        vc = slice(hh * GLA_DV, (hh + 1) * GLA_DV)
        for direction, gk_ref in enumerate((gkf_ref, gkb_ref)):
            keep = (ci <= ri) if direction == 0 else (ci >= ri)
            last = C - 1 if direction == 0 else 0
            gk_hi, gk_lo = _split(gk_ref[:, kc])
            gk_parts = jnp.concatenate([gk_hi, gk_lo], axis=1)
            if whole:
                b_all = _dot(tri_ref[direction], gk_parts)
                b_all = b_all[:, :GLA_DK] + b_all[:, GLA_DK:]
                qe_all = (q_ref[:, kc] * jnp.exp(b_all)).astype(BF16)
                ke_all = (k_ref[:, kc] * jnp.exp(-b_all)).astype(BF16)
                keep_all = same_chunk & ((cl <= rl) if direction == 0 else (cl >= rl))
                s_all = lax.dot_general(qe_all, ke_all, nt_dims, preferred_element_type=F32)
                s_all = jnp.where(keep_all, s_all, 0.0).astype(BF16)
                o_intra = _dot(s_all, v_ref[:, vc].astype(BF16))
            st = s0_ref[direction, hh].T if has_s0 else jnp.zeros((GLA_DV, GLA_DK), F32)
            order = range(n) if direction == 0 else range(n - 1, -1, -1)
            for c in order:
                rows = slice(c * C, (c + 1) * C)
                if whole:
                    b = b_all[rows]
                else:
                    b = _dot(tri_ref[direction], gk_parts[rows])
                    b = b[:, :GLA_DK] + b[:, GLA_DK:]
                b_last = b[last:last + 1, :]
                k = k_ref[rows, kc]
                v = v_ref[rows, vc].astype(BF16)
                kd = (k * jnp.exp(b_last - b)).astype(BF16)
                if whole:
                    qe = qe_all[rows]
                    o = o_intra[rows]
                else:
                    qe = (q_ref[rows, kc] * jnp.exp(b)).astype(BF16)
                    ke = (k * jnp.exp(-b)).astype(BF16)
                    scores = lax.dot_general(qe, ke, nt_dims, preferred_element_type=F32)
                    o = _dot(jnp.where(keep, scores, 0.0).astype(BF16), v)
                o = o + lax.dot_general(qe, st.astype(BF16), nt_dims, preferred_element_type=F32)
                if direction == 0:
                    acc[rows, vc] = o
                else:
                    acc[rows, vc] = acc[rows, vc] + o
                st = jnp.exp(b_last) * st + lax.dot_general(v, kd, tn_dims, preferred_element_type=F32)
            sf_ref[direction, hh] = st.T

        o = acc[:, vc]
        o = o * lax.rsqrt(jnp.mean(o * o, axis=-1, keepdims=True) + EPS) * ng_ref[...]
        o_gated[:, vc] = (o * g_ref[:, vc]).astype(BF16)

    if n_proj:
        w_ref, gate_ref, x_ref, *pending = proj_refs
        o_ref[...] = _read_x(x_ref, pending) + gate_ref[...] * _dot(o_gated[...], w_ref[...])


def _gla_core(proj, tri, norm_g, s0, L, n_batch, row_off, hps, out_proj=None):
    rb = row_off // L
    H = GLA_H
    nh = H // hps
    has_s0 = s0 is not None
    kb, vb = GLA_DK * hps, GLA_DV * hps
    in_specs = [
        pl.BlockSpec((L, kb), lambda b, h: (rb + b, h)),
        pl.BlockSpec((L, kb), lambda b, h: (rb + b, nh + h)),
        pl.BlockSpec((L, vb), lambda b, h: (rb + b, (2 * H * GLA_DK) // vb + h)),
        pl.BlockSpec((L, vb), lambda b, h: (rb + b, (2 * H * GLA_DK) // vb + nh + h)),
        pl.BlockSpec((L, kb), lambda b, h: (rb + b, GLA_PROJ // kb + h)),
        pl.BlockSpec((L, kb), lambda b, h: (rb + b, GLA_PROJ // kb + nh + h)),
        _full_spec(tri.shape),
        _vec_spec(GLA_DV),
    ]
    args = [proj] * 6 + [tri, norm_g.reshape(1, GLA_DV)]
    state_spec = pl.BlockSpec((None, 2, hps, GLA_DK, GLA_DV), lambda b, h: (b, 0, h, 0, 0))
    if has_s0:
        in_specs.append(state_spec)
        args.append(s0)
    first_spec = pl.BlockSpec((L, vb), lambda b, h: (b, h))
    first_shape = jax.ShapeDtypeStruct((n_batch * L, H * GLA_DV), BF16)
    scratch = [pltpu.VMEM((L, vb), F32)]
    aliases, n_proj = {}, 0
    if out_proj is not None:
        assert nh == 1, "the fused output projection needs every head in the step"
        x, w_o, mod, layer, pend = out_proj

        def row_fn(b, h):
            return 1 + b if row_off > 0 else 0

        x_spec = pl.BlockSpec((L, D), lambda b, h: (rb + b, 0))
        p_specs, p_args = _pending_specs(pend, mod, L, lambda b, h: (rb + b, 0), row_fn)
        aliases = {len(args) + 2: 0}
        in_specs += [_full_spec((H * GLA_DV, D)), _mod_spec(layer, 2, row_fn), x_spec] + p_specs
        args += [w_o, mod, x] + p_args
        n_proj = 3 + len(p_args)
        first_spec, first_shape = x_spec, jax.ShapeDtypeStruct((N_TOK, D), F32)
        scratch.append(pltpu.VMEM((L, vb), BF16))
    kern = functools.partial(_gla_core_kernel, L=L, has_s0=has_s0, hps=hps, n_proj=n_proj)
    return pl.pallas_call(
        kern,
        grid=(n_batch, nh),
        in_specs=in_specs,
        out_specs=[first_spec, state_spec],
        out_shape=[first_shape, jax.ShapeDtypeStruct((n_batch, 2, H, GLA_DK, GLA_DV), F32)],
        scratch_shapes=scratch,
        input_output_aliases=aliases,
        compiler_params=_cparams(2),
        name=f"gla_core_L{L}",
    )(*args)


def _fnet_mats(L):
    c = np.arange(FNET_C)
    ang_c = 2.0 * np.pi * np.outer(c, c) / FNET_C
    chan = np.concatenate([np.cos(ang_c), np.sin(ang_c)], axis=1) / math.sqrt(FNET_C)
    t = np.arange(L)
    ang_l = 2.0 * np.pi * np.outer(t, t) / L
    seq = np.concatenate([np.cos(ang_l), -np.sin(ang_l)], axis=1) / math.sqrt(L)
    return jnp.asarray(chan, F32).astype(BF16), jnp.asarray(seq, F32).astype(BF16)


def _fnet_kernel(x_ref, g_ref, sc_ref, sh_ref, gate_ref, chan_ref, seq_ref, w_ref, b_ref, *rest):
    *pending, o_ref = rest
    x = _read_x(x_ref, pending)
    h = _norm_mod(x, g_ref[...], sc_ref[...], sh_ref[...]).astype(BF16)
    chan = chan_ref[...]
    cos_parts, sin_parts = [], []
    for gi in range(FNET_GROUPS):
        cs = _dot(h[:, gi * FNET_C:(gi + 1) * FNET_C], chan)
        cos_parts.append(cs[:, :FNET_C])
        sin_parts.append(cs[:, FNET_C:])
    stacked = jnp.concatenate([jnp.concatenate(cos_parts, axis=1), jnp.concatenate(sin_parts, axis=1)], axis=0)
    mixed = _dot(seq_ref[...], stacked.astype(BF16))
    y = _dot(mixed.astype(BF16), w_ref[...]) + b_ref[...]
    o_ref[...] = x + gate_ref[...] * y


def _fnet(x, mod, layer, norm_g, chan, seq, w_bf16, bias, L, n_batch, row_off, pend=None):
    rb = row_off // L
    lat = row_off > 0

    def row_fn(b):
        return 1 + b if lat else 0

    p_specs, p_args = _pending_specs(pend, mod, L, lambda b: (rb + b, 0), row_fn)
    return pl.pallas_call(
        _fnet_kernel,
        grid=(n_batch,),
        in_specs=[
            pl.BlockSpec((L, D), lambda b: (rb + b, 0)),
            _vec_spec(D),
            _mod_spec(layer, 1, row_fn),
            _mod_spec(layer, 0, row_fn),
            _mod_spec(layer, 2, row_fn),
            _full_spec((FNET_C, 2 * FNET_C)),
            _full_spec((L, 2 * L)),
            _full_spec((D, D)),
            _vec_spec(D),
        ] + p_specs,
        out_specs=pl.BlockSpec((L, D), lambda b: (rb + b, 0)),
        out_shape=jax.ShapeDtypeStruct((N_TOK, D), F32),
        input_output_aliases={0: 0},
        compiler_params=_cparams(1),
        name=f"fnet_L{L}",
    )(x, norm_g.reshape(1, D), mod, mod, mod, chan, seq, w_bf16, bias.reshape(1, D), *p_args)


def _window_bounds(n, k):
    t = np.arange(n)
    lo, hi = k // 2, k - k // 2 - 1
    return np.maximum(t - lo, 0), np.minimum(t + hi + 1, n)


def _pool_mats(L, grid_rows):
    mats, inv = [], []
    for k in POOL_WINDOWS:
        if grid_rows is None:
            s, e = _window_bounds(L, k)
            idx = np.arange(L)[None, :]
            m = ((idx >= s[:, None]) & (idx < e[:, None])).astype(np.float64)
            cnt = (e - s).astype(np.float64)
        else:
            sr, er = _window_bounds(grid_rows, k)
            sc, ec = _window_bounds(GRID_W, k)
            ir = np.arange(grid_rows)[None, :]
            ic = np.arange(GRID_W)[None, :]
            mr = ((ir >= sr[:, None]) & (ir < er[:, None])).astype(np.float64)
            mc = ((ic >= sc[:, None]) & (ic < ec[:, None])).astype(np.float64)
            m = np.kron(mr, mc)
            cnt = np.kron((er - sr).astype(np.float64), (ec - sc).astype(np.float64))
        mats.append(m)
        inv.append(1.0 / cnt)
    return jnp.asarray(np.stack(mats), BF16), jnp.asarray(np.stack(inv)[:, :, None], F32)


def _pool_kernel(x_ref, g_ref, sc_ref, sh_ref, gate_ref, m_ref, ic_ref, w_ref, b_ref, ps_ref, *rest):
    *pending, o_ref = rest
    x = _read_x(x_ref, pending)
    h = _norm_mod(x, g_ref[...], sc_ref[...], sh_ref[...])
    outs = []
    for gi in range(len(POOL_WINDOWS)):
        hg = h[:, gi * POOL_G:(gi + 1) * POOL_G]
        hi, lo = _split(hg)
        m = m_ref[gi]
        mean = (_dot(m, hi) + _dot(m, lo)) * ic_ref[gi]
        outs.append(_dot((mean - hg).astype(BF16), w_ref[gi]))
    y = (jnp.concatenate(outs, axis=1) + b_ref[...]) * ps_ref[...]
    o_ref[...] = x + gate_ref[...] * y


def _pool(x, mod, layer, norm_g, mats, inv_cnt, w_bf16, bias, scale, L, n_batch, row_off, pend=None):
    rb = row_off // L
    lat = row_off > 0
    G = len(POOL_WINDOWS)

    def row_fn(b):
        return 1 + b if lat else 0

    p_specs, p_args = _pending_specs(pend, mod, L, lambda b: (rb + b, 0), row_fn)
    return pl.pallas_call(
        _pool_kernel,
        grid=(n_batch,),
        in_specs=[
            pl.BlockSpec((L, D), lambda b: (rb + b, 0)),
            _vec_spec(D),
            _mod_spec(layer, 1, row_fn),
            _mod_spec(layer, 0, row_fn),
            _mod_spec(layer, 2, row_fn),
            _full_spec((G, L, L)),
            _full_spec((G, L, 1)),
            _full_spec((G, POOL_G, POOL_G)),
            _vec_spec(D),
            _vec_spec(D),
        ] + p_specs,
        out_specs=pl.BlockSpec((L, D), lambda b: (rb + b, 0)),
        out_shape=jax.ShapeDtypeStruct((N_TOK, D), F32),
        input_output_aliases={0: 0},
        compiler_params=_cparams(1),
        name=f"pool_L{L}",
    )(x, norm_g.reshape(1, D), mod, mod, mod, mats, inv_cnt, w_bf16, bias.reshape(1, D), scale.reshape(1, D),
      *p_args)


ROUTER_PAD = 128
LANES = 128
D_EXT = D + LANES
MOE_TILE = 1024
MOE_MAX_TILES = N_TOK // MOE_TILE + MOE_GROUPS
MOE_ROWS = MOE_MAX_TILES * MOE_TILE
MOE_TILE_SHIFT = MOE_TILE.bit_length() - 1
assert 1 << MOE_TILE_SHIFT == MOE_TILE
MOE_ROW_BLOCK = 256
MOE_DMA_CHUNK = 64
MOE_CHUNK_SHIFT = MOE_DMA_CHUNK.bit_length() - 1
assert 1 << MOE_CHUNK_SHIFT == MOE_DMA_CHUNK and MOE_MAX_TILES % 2 == 0
MOE_Y_ROWS = N_TOK + 2 * MOE_DMA_CHUNK


ROUTE_ROWS = 8


def _moe_route_kernel(x_ref, g_ref, sc_ref, sh_ref, wr_ref, tri_ref, h3_ref, route_ref, cnt_ref, carry):
    t = pl.program_id(0)
    refs = (x_ref, g_ref, sc_ref, sh_ref, wr_ref, tri_ref, h3_ref, route_ref, cnt_ref, carry)
    pl.when(t < N_TOK // TM)(functools.partial(_moe_route_tile, t, *refs))

    @pl.when(t == N_TOK // TM)
    def _():
        h3_ref[...] = jnp.zeros_like(h3_ref)


def _moe_route_tile(t, x_ref, g_ref, sc_ref, sh_ref, wr_ref, tri_ref, h3_ref, route_ref, cnt_ref, carry):
    @pl.when(t == 0)
    def _():
        carry[...] = jnp.zeros_like(carry)

    h = _norm_mod(x_ref[...], g_ref[...], sc_ref[...], sh_ref[...])
    w_hi, w_lo = _split(wr_ref[...])
    h_hi, h_lo = _split(h)
    nt = (((1,), (1,)), ((), ()))
    logits = (lax.dot_general(w_hi, h_hi, nt, preferred_element_type=F32)
              + (lax.dot_general(w_hi, h_lo, nt, preferred_element_type=F32)
                 + lax.dot_general(w_lo, h_hi, nt, preferred_element_type=F32)))
    neg = jnp.float32(-jnp.inf)
    r8 = lax.broadcasted_iota(jnp.int32, (ROUTE_ROWS, TM), 0)
    r16 = lax.broadcasted_iota(jnp.int32, (MOE_E, TM), 0)
    gl = jnp.where(r8 < MOE_GROUPS, logits[MOE_E:MOE_E + ROUTE_ROWS], neg)
    g_max = jnp.max(gl, axis=0, keepdims=True)
    g_idx = jnp.min(jnp.where(gl == g_max, r8, ROUTE_ROWS), axis=0, keepdims=True)
    p_grp = 1.0 / jnp.sum(jnp.exp(gl - g_max), axis=0, keepdims=True)
    in_grp = (r16 >> 2) == g_idx
    el = jnp.where(in_grp, logits[:MOE_E], neg)
    m1 = jnp.max(el, axis=0, keepdims=True)
    i1 = jnp.min(jnp.where(el == m1, r16, MOE_E), axis=0, keepdims=True)
    z = jnp.sum(jnp.exp(el - m1), axis=0, keepdims=True)
    el2 = jnp.where(r16 == i1, neg, el)
    m2 = jnp.max(el2, axis=0, keepdims=True)
    i2 = jnp.min(jnp.where(el2 == m2, r16, MOE_E), axis=0, keepdims=True)
    p1 = 1.0 / z
    p2 = jnp.exp(m2 - m1) / z
    tot = p1 + p2
    eid = r8 + MOE_PER_GROUP * g_idx
    in4 = r8 < MOE_PER_GROUP
    cw4 = (jnp.where(in4 & (eid == i1), p_grp * (p1 / tot), 0.0)
           + jnp.where(in4 & (eid == i2), p_grp * (p2 / tot), 0.0))
    member = jnp.where(r8 == g_idx, 1.0, 0.0)
    before = _dot(member.astype(BF16), tri_ref[...]) + carry[:, 0:1]
    rank = jnp.sum(jnp.where(r8 == g_idx, before, 0.0), axis=0, keepdims=True)
    carry[...] = carry[...] + jnp.sum(member, axis=1, keepdims=True)
    cnt_ref[...] = carry[...].astype(jnp.int32)
    route_ref[...] = jnp.where(r8 == 0, g_idx, jnp.where(r8 == 1, rank.astype(jnp.int32), 0))
    h3_ref[:, :D] = h
    cw_rows = jnp.concatenate([cw4, jnp.zeros((LANES - ROUTE_ROWS, TM), F32)], axis=0)
    h3_ref[:, D:] = cw_rows.T


def _moe_route(x, mod, layer, norm_g, w_router_t, tri):
    nt = N_TOK // TM

    def tok_tile(t):
        return jnp.minimum(t, nt - 1)

    return pl.pallas_call(
        _moe_route_kernel,
        grid=(nt + 1,),
        in_specs=[
            pl.BlockSpec((TM, D), lambda t: (tok_tile(t), 0)),
            _vec_spec(D),
            _mod_spec(layer, 4, lambda t: _row_tm(tok_tile(t))),
            _mod_spec(layer, 3, lambda t: _row_tm(tok_tile(t))),
            _full_spec((ROUTER_PAD, D)),
            _full_spec((TM, TM)),
        ],
        out_specs=[
            pl.BlockSpec((TM, D_EXT), lambda t: (t, 0)),
            pl.BlockSpec((ROUTE_ROWS, TM), lambda t: (0, tok_tile(t))),
            pl.BlockSpec((ROUTE_ROWS, LANES), lambda t: (0, 0)),
        ],
        out_shape=[
            jax.ShapeDtypeStruct((N_TOK + TM, D_EXT), F32),
            jax.ShapeDtypeStruct((ROUTE_ROWS, N_TOK), jnp.int32),
            jax.ShapeDtypeStruct((ROUTE_ROWS, LANES), jnp.int32),
        ],
        scratch_shapes=[pltpu.VMEM((ROUTE_ROWS, LANES), F32)],
        compiler_params=_cparams(1),
        name="moe_route",
    )(x, norm_g.reshape(1, D), mod, mod, w_router_t, tri)


def _moe_invert_kernel(pos_ref, lo_ref, hi_ref, src_ref):
    def mark(j, carry):
        parity = lax.shift_right_logical(j, jnp.int32(MOE_TILE_SHIFT - MOE_CHUNK_SHIFT)) & 1
        first = N_TOK + MOE_DMA_CHUNK * parity
        base = j * MOE_DMA_CHUNK
        for rr in range(MOE_DMA_CHUNK):
            src_ref[base + rr] = first + rr
        return carry

    for g in range(MOE_GROUPS + 1):
        lax.fori_loop(lo_ref[g], hi_ref[g], mark, 0)

    def place(n, carry):
        src_ref[pos_ref[n]] = n
        return carry

    lax.fori_loop(0, N_TOK, place, 0, unroll=16)


def _moe_invert(pos, mark_lo, mark_hi):
    smem = pl.BlockSpec(memory_space=pltpu.SMEM)
    return pl.pallas_call(
        _moe_invert_kernel,
        in_specs=[smem, smem, smem],
        out_specs=smem,
        out_shape=jax.ShapeDtypeStruct((MOE_ROWS,), jnp.int32),
        name="moe_invert",
    )(pos, mark_lo, mark_hi)


def _moe_expert_kernel(src_ref, grp_ref, nact_ref, nchunk_ref, h_hbm, w1_ref, w3_ref, w2_ref, y_hbm,
                       xb0, xb1, ab0, ab1, gsem, ssem):
    t = pl.program_id(0)
    k = pl.program_id(1)
    n_active = nact_ref[0]
    last = n_active - 1
    T = MOE_TILE
    CH = MOE_DMA_CHUNK
    xbufs, accs = (xb0, xb1), (ab0, ab1)

    def gather_row(tile, slot, base, rr):
        tok = src_ref[tile * T + base + rr]
        rows = xbufs[slot].at[pl.ds(base, CH), :]
        return pltpu.make_async_copy(h_hbm.at[pl.ds(tok, 1), :], rows.at[pl.ds(rr, 1), :], gsem.at[slot])

    def scatter_row(tile, slot, base, rr):
        dst = src_ref[tile * T + base + rr]
        rows = accs[slot].at[pl.ds(base, CH), :]
        return pltpu.make_async_copy(rows.at[pl.ds(rr, 1), :], y_hbm.at[pl.ds(dst, 1), :], ssem.at[slot])

    def start_rows(make, tile, slot):
        def chunk(c, carry):
            base = pl.multiple_of(c * CH, CH)
            for rr in range(CH):
                make(tile, slot, base, rr).start()
            return carry

        lax.fori_loop(0, nchunk_ref[tile], chunk, 0)

    def wait_rows(tile, slot, gather):
        def chunk(c, carry):
            if gather:
                pltpu.make_async_copy(h_hbm.at[pl.ds(0, CH), :], xbufs[slot].at[pl.ds(0, CH), :], gsem.at[slot]).wait()
            else:
                pltpu.make_async_copy(accs[slot].at[pl.ds(0, CH), :], y_hbm.at[pl.ds(0, CH), :], ssem.at[slot]).wait()
            return carry

        lax.fori_loop(0, nchunk_ref[tile], chunk, 0)

    def step(slot):
        other = 1 - slot
        xb, acc = xbufs[slot], accs[slot]

        @pl.when(k == 0)
        def _():
            if slot == 0:
                @pl.when(t == 0)
                def _():
                    xb0[...] = jnp.zeros_like(xb0)
                    xb1[...] = jnp.zeros_like(xb1)
                    ab0[...] = jnp.zeros_like(ab0)
                    ab1[...] = jnp.zeros_like(ab1)
                    dump = pltpu.make_async_copy(ab0.at[pl.ds(0, 2 * CH), :], y_hbm.at[pl.ds(N_TOK, 2 * CH), :],
                                                 ssem.at[0])
                    dump.start()
                    dump.wait()
                    start_rows(gather_row, 0, 0)

            wait_rows(t, slot, True)

            @pl.when(t < last)
            def _():
                start_rows(gather_row, t + 1, other)

            @pl.when(t >= 2)
            def _():
                wait_rows(t - 2, slot, False)

        def experts(m):
            x = xb[:m, :D].astype(BF16)
            a = _dot(x, w1_ref[...].astype(BF16))
            b = _dot(x, w3_ref[...].astype(BF16))
            lane = lax.broadcasted_iota(jnp.int32, (m, LANES), 1)
            cwk = jnp.sum(jnp.where(lane == k, xb[:m, D:], 0.0), axis=-1, keepdims=True)
            hid = (_silu(a) * b * cwk).astype(BF16)
            acc[:m] = jnp.where(k > 0, acc[:m], 0.0) + _dot(hid, w2_ref[...].astype(BF16))

        blocks = (nchunk_ref[t] * CH + MOE_ROW_BLOCK - 1) // MOE_ROW_BLOCK
        for nb in range(1, T // MOE_ROW_BLOCK + 1):
            pl.when(blocks == nb)(functools.partial(experts, nb * MOE_ROW_BLOCK))

        @pl.when(k == MOE_PER_GROUP - 1)
        def _():
            start_rows(scatter_row, t, slot)

            @pl.when(t == last)
            def _():
                wait_rows(t, slot, False)

                @pl.when(t >= 1)
                def _():
                    wait_rows(t - 1, other, False)

    for slot in (0, 1):
        pl.when((t < n_active) & (t % 2 == slot))(functools.partial(step, slot))


def _moe_experts(h_ext, src, tile_group, n_active, n_chunk, layer, w1, w3, w2):
    T = MOE_TILE

    def w_index(t, k, src_ref, grp_ref, nact_ref, nchunk_ref):
        last = nact_ref[0] - 1
        e = jnp.where(t <= last, grp_ref[t] * MOE_PER_GROUP + k, grp_ref[last] * MOE_PER_GROUP + MOE_PER_GROUP - 1)
        return (layer, e, 0, 0)

    grid_spec = pltpu.PrefetchScalarGridSpec(
        num_scalar_prefetch=4,
        grid=(MOE_MAX_TILES, MOE_PER_GROUP),
        in_specs=[
            pl.BlockSpec(memory_space=pl.ANY),
            pl.BlockSpec((None, None, D, MOE_HID), w_index),
            pl.BlockSpec((None, None, D, MOE_HID), w_index),
            pl.BlockSpec((None, None, MOE_HID, D), w_index),
        ],
        out_specs=pl.BlockSpec(memory_space=pl.ANY),
        scratch_shapes=[
            pltpu.VMEM((T, D_EXT), F32),
            pltpu.VMEM((T, D_EXT), F32),
            pltpu.VMEM((T, D), F32),
            pltpu.VMEM((T, D), F32),
            pltpu.SemaphoreType.DMA((2,)),
            pltpu.SemaphoreType.DMA((2,)),
        ],
    )
    return pl.pallas_call(
        _moe_expert_kernel,
        grid_spec=grid_spec,
        out_shape=jax.ShapeDtypeStruct((MOE_Y_ROWS, D), F32),
        compiler_params=_cparams(2),
        name="moe_experts",
    )(src, tile_group, n_active, n_chunk, h_ext, w1, w3, w2)


def _moe_combine_kernel(y_ref, gate_ref, x_ref, o_ref):
    o_ref[...] = x_ref[...] + gate_ref[...] * y_ref[...]


def _moe_combine(x, y3, mod, layer):
    return pl.pallas_call(
        _moe_combine_kernel,
        grid=(N_TOK // TM,),
        in_specs=[
            pl.BlockSpec((TM, D), lambda t: (t, 0)),
            _mod_spec(layer, 5, _row_tm),
            pl.BlockSpec((TM, D), lambda t: (t, 0)),
        ],
        out_specs=pl.BlockSpec((TM, D), lambda t: (t, 0)),
        out_shape=jax.ShapeDtypeStruct((N_TOK, D), F32),
        input_output_aliases={2: 0},
        compiler_params=_cparams(1),
        name="moe_combine",
    )(y3, mod, x)


def _moe(x, mod, layer, norm_g, w_rg, w_re, w1, w3, w2, tri, final_g=None):
    w_router_t = jnp.zeros((ROUTER_PAD, D), F32).at[:MOE_E].set(w_re.T).at[MOE_E:MOE_E + MOE_GROUPS].set(w_rg.T)
    h3, route, counts = _moe_route(x, mod, layer, norm_g, w_router_t, tri)
    cnt = counts[:MOE_GROUPS, 0]
    ntile = (cnt + MOE_TILE - 1) // MOE_TILE
    tile_end = jnp.cumsum(ntile)
    seg_start = (tile_end - ntile) * MOE_TILE
    g_idx, rank = route[0], route[1]
    pos = jnp.sum(jnp.where(g_idx[None, :] == jnp.arange(MOE_GROUPS)[:, None], seg_start[:, None], 0), axis=0) + rank
    tiles = jnp.arange(MOE_MAX_TILES, dtype=jnp.int32)
    tile_group = jnp.minimum(jnp.sum(tiles[:, None] >= tile_end[None, :], axis=1), MOE_GROUPS - 1).astype(jnp.int32)
    n_active = tile_end[-1:].astype(jnp.int32)
    seg_end = tile_end * MOE_TILE
    mark_lo = jnp.concatenate([(seg_start + cnt) // MOE_DMA_CHUNK, seg_end[-1:] // MOE_DMA_CHUNK])
    mark_hi = jnp.concatenate([seg_end // MOE_DMA_CHUNK, jnp.full((1,), MOE_ROWS // MOE_DMA_CHUNK)])
    src = _moe_invert(pos.astype(jnp.int32), mark_lo.astype(jnp.int32), mark_hi.astype(jnp.int32))
    first_tile = (tile_end - ntile)[tile_group]
    real_rows = jnp.clip(cnt[tile_group] - (tiles - first_tile) * MOE_TILE, 0, MOE_TILE)
    n_chunk = ((real_rows + MOE_DMA_CHUNK - 1) // MOE_DMA_CHUNK).astype(jnp.int32)
    y3 = _moe_experts(h3, src, tile_group, n_active, n_chunk, layer, w1, w3, w2)
    if final_g is None:
        return y3
    return tuple(_moe_combine_norm(x, y3, mod, layer, final_g, off, nb * L) for L, nb, off in
                 ((CTX_L, CTX_B, 0), (LAT_L, LAT_B, N_CTX)))


def _combine_norm_kernel(y_ref, gate_ref, x_ref, g_ref, o_ref):
    x = x_ref[...] + gate_ref[...] * y_ref[...]
    o_ref[...] = x * lax.rsqrt(jnp.mean(x * x, axis=-1, keepdims=True) + EPS) * g_ref[...]


def _moe_combine_norm(x, y3, mod, layer, final_g, row_off, n_rows):
    off = row_off // TM
    first_lat = N_CTX // TM

    def row_fn(t):
        g = t + off
        return jnp.where(g < first_lat, 0, 1 + (g - first_lat) // (LAT_L // TM))

    return pl.pallas_call(
        _combine_norm_kernel,
        grid=(n_rows // TM,),
        in_specs=[
            pl.BlockSpec((TM, D), lambda t: (t + off, 0)),
            _mod_spec(layer, 5, row_fn),
            pl.BlockSpec((TM, D), lambda t: (t + off, 0)),
            _vec_spec(D),
        ],
        out_specs=pl.BlockSpec((TM, D), lambda t: (t, 0)),
        out_shape=jax.ShapeDtypeStruct((n_rows, D), F32),
        compiler_params=_cparams(1),
        name="combine_final_norm",
    )(y3, mod, x, final_g.reshape(1, D))


def kernel(x_prompt, x_sample, state_gla, c, c_ctx, w_ada, b_ada, norm_g, hy_w_in, hy_b_in, hy_conv_w, hy_conv_b, hy_f_w1, hy_f_b1, hy_f_freq, hy_f_w2, hy_f_b2, hy_f_w3, hy_skip, hy_w_out, hy_b_out, gla_w_q, gla_w_k, gla_w_v, gla_w_g, gla_w_gk1, gla_w_gk2, gla_b_gk, gla_norm_g, gla_w_o, fn_w_out, fn_b_out, pool_w, pool_b, pool_scale, moe_w_rg, moe_w_re, moe_w1, moe_w3, moe_w2, final_g):
    groups = ((CTX_L, CTX_B, 0, None), (LAT_L, LAT_B, N_CTX, LAT_L // GRID_W))

    x_ctx, x_lat = x_prompt.reshape(N_CTX, D), x_sample.reshape(N_LAT, D)
    x = None
    cond =jnp.zeros((MOD_ROWS, D), F32).at[0].set(c_ctx).at[1:1 + LAT_B].set(c)
    mod = _ada_table(cond, w_ada, b_ada).reshape(DEPTH * MOD_ROWS * 6, 1, D)

    tri_tm = jnp.asarray(np.triu(np.ones((TM, TM)), 1), BF16)

    new_states = []
    pend = None
    for i in range(DEPTH):
        kind, j = i % 4, i // 4
        if kind == 0:
            if i > 0:
                x = _moe_combine(x, pend[0], mod, pend[1])
                x_ctx, x_lat = x[:N_CTX], x[N_CTX:]
            u = _hyena_in(x_ctx, x_lat, mod, i, norm_g[i, 0], hy_w_in[j].astype(BF16), hy_b_in[j], hy_conv_w[j],
                          hy_conv_b[j])
            zs = []
            for L, nb, off, _ in groups:
                fwd, ff, inv = _dft_mats(L)
                khat = _hyena_filters(L, ff, hy_f_w1[j], hy_f_b1[j], hy_f_freq[j], hy_f_w2[j], hy_f_b2[j],
                                      hy_f_w3[j])
                zs.append(_hyena_conv(u, khat, hy_skip[j], fwd, inv, L, nb, off, D if L == CTX_L else 512))
            x = _outproj_joint(x_ctx, x_lat, zs[0], zs[1], hy_w_out[j].astype(BF16), hy_b_out[j], mod, i)
        elif kind == 1:
            w_cat = jnp.concatenate([gla_w_q[j], gla_w_k[j], gla_w_v[j], gla_w_g[j]], axis=1).astype(BF16)
            nk = GLA_H * GLA_DK
            wg1 = jnp.zeros((D, GK1_PAD), F32).at[:, :GLA_RANK].set(gla_w_gk1[j, 0])
            wg1 = wg1.at[:, GLA_RANK:2 * GLA_RANK].set(gla_w_gk1[j, 1]).astype(BF16)
            wg2 = jnp.zeros((GK1_PAD, 2 * nk), F32).at[:GLA_RANK, :nk].set(gla_w_gk2[j, 0])
            wg2 = wg2.at[GLA_RANK:2 * GLA_RANK, nk:].set(gla_w_gk2[j, 1]).astype(BF16)
            proj = _gla_proj(x, mod, i, norm_g[i, 0], w_cat, wg1, wg2, gla_b_gk[j].reshape(1, 2 * nk), pend)
            lower = np.tril(np.ones((GLA_CHUNK, GLA_CHUNK)))
            w_o = gla_w_o[j].astype(BF16)
            for L, nb, off, grid_rows in groups:
                eye = np.eye(L // GLA_CHUNK if L // GLA_CHUNK <= GLA_WHOLE_SEQ_CHUNKS else 1)
                tri = jnp.asarray(np.stack([np.kron(eye, lower), np.kron(eye, lower.T)]), BF16)
                s0 = None if grid_rows is None else state_gla[:, j]
                if L == CTX_L:
                    x, s_fin = _gla_core(proj, tri, gla_norm_g[j], s0, L, nb, off, GLA_H, (x, w_o, mod, i, pend))
                else:
                    o, s_fin = _gla_core(proj, tri, gla_norm_g[j], s0, L, nb, off, 1)
                    x = _outproj(x, o, w_o, jnp.zeros((D,), F32), mod, i, off, nb * L, pend)
                if grid_rows is None:
                    new_states.append(s_fin)
        elif kind == 2:
            w_out = fn_w_out[j].astype(BF16)
            for L, nb, off, _ in groups:
                chan, seq = _fnet_mats(L)
                x = _fnet(x, mod, i, norm_g[i, 0], chan, seq, w_out, fn_b_out[j], L, nb, off, pend)
        else:
            w_pool = pool_w[j].astype(BF16)
            for L, nb, off, grid_rows in groups:
                mats, inv_cnt = _pool_mats(L, grid_rows)
                x = _pool(x, mod, i, norm_g[i, 0], mats, inv_cnt, w_pool, pool_b[j], pool_scale[j], L, nb, off,
                          pend)

        out = _moe(x, mod, i, norm_g[i, 1], moe_w_rg[i], moe_w_re[i], moe_w1, moe_w3, moe_w2, tri_tm,
                   final_g if i == DEPTH - 1 else None)
        pend = (out, i)

    y_prompt, y_sample = out
    new_state_gla = jnp.stack(new_states, axis=1)
    return (y_prompt.reshape(CTX_B, CTX_L, D), y_sample.reshape(LAT_B, LAT_L, D), new_state_gla)
```

```python
import functools
import math

import jax
import jax.numpy as jnp
import numpy as np
from jax import lax
from jax.experimental import pallas as pl
from jax.experimental.pallas import tpu as pltpu

F32 = jnp.float32
BF16 = jnp.bfloat16

D = 1024
CTX_B, CTX_L = 32, 256
LAT_B, LAT_L = 2, 1024
N_CTX = CTX_B * CTX_L
N_LAT = LAT_B * LAT_L
N_TOK = N_CTX + N_LAT
DEPTH = 4
GRID_W = 64
EPS = 1e-6

HY_BANDS = 8
HY_EMB = 1 + 2 * HY_BANDS
HY_EMB_PAD = 32
HY_HID = 64
HY_FAST_DECAY = 0.3
HY_SLOW_DECAY = 1.5
HY_DECAY_TARGET = 1e-2

GLA_H = 4
GLA_DK = 128
GLA_DV = 256
GLA_RANK = 16
GLA_NORMALIZER = 16.0
GLA_CHUNK = 64
GLA_WHOLE_SEQ_CHUNKS = 4

FNET_GROUPS = 4
FNET_C = D // FNET_GROUPS
POOL_WINDOWS = (2, 4, 8, 16)
POOL_G = D // len(POOL_WINDOWS)

MOE_GROUPS = 4
MOE_PER_GROUP = 4
MOE_E = MOE_GROUPS * MOE_PER_GROUP
MOE_HID = D // 2

MOD_ROWS = 8
TM = 512
TM_BIG = 1024
VMEM_LIMIT = 56 * 1024 * 1024


def _cparams(n_axes):
    return pltpu.CompilerParams(dimension_semantics=("arbitrary",) * n_axes, vmem_limit_bytes=VMEM_LIMIT)


def _norm_mod(x, g, sc, sh):
    ms = jnp.mean(x * x, axis=-1, keepdims=True)
    return (x * lax.rsqrt(ms + EPS) * g) * (1.0 + sc) + sh


def _split(a):
    hi = a.astype(BF16)
    lo = (a - hi.astype(F32)).astype(BF16)
    return hi, lo


def _dot(a, b):
    return jnp.dot(a, b, preferred_element_type=F32)


def _dot_precise(a, b):
    a_hi, a_lo = _split(a)
    b_hi, b_lo = _split(b)
    return _dot(a_hi, b_hi) + (_dot(a_hi, b_lo) + _dot(a_lo, b_hi))


def _silu(x):
    return x * (1.0 / (1.0 + jnp.exp(-x)))


def _log_sigmoid(x):
    return jnp.minimum(x, 0.0) - jnp.log(1.0 + jnp.exp(-jnp.abs(x)))


def _mod_spec(layer, chunk, row_fn):
    base = layer * MOD_ROWS * 6 + chunk

    def index_map(*ids):
        return (base + row_fn(*ids) * 6, 0, 0)

    return pl.BlockSpec((None, 1, D), index_map)


def _row_tm(t, *_):
    return jnp.where(t < N_CTX // TM, 0, 1 + (t - N_CTX // TM) // (LAT_L // TM))


def _row_big(t, *_):
    return jnp.where(t < N_CTX // TM_BIG, 0, 1 + (t - N_CTX // TM_BIG) // (LAT_L // TM_BIG))


def _vec_spec(n):
    return pl.BlockSpec((1, n), lambda *ids: (0, 0))


def _full_spec(shape):
    nd = len(shape)
    return pl.BlockSpec(shape, lambda *ids: (0,) * nd)


def _ada_kernel(cond_ref, w_ref, b_ref, o_ref):
    s = _silu(cond_ref[...]).astype(BF16)
    o_ref[...] = _dot(s, w_ref[...].astype(BF16)) + b_ref[...]


def _ada_table(cond, w_ada, b_ada):
    tn = 1536
    return pl.pallas_call(
        _ada_kernel,
        grid=(DEPTH, 6 * D // tn),
        in_specs=[
            pl.BlockSpec((MOD_ROWS, D), lambda i, j: (0, 0)),
            pl.BlockSpec((None, D, tn), lambda i, j: (i, 0, j)),
            pl.BlockSpec((None, 1, tn), lambda i, j: (i, 0, j)),
        ],
        out_specs=pl.BlockSpec((None, MOD_ROWS, tn), lambda i, j: (i, 0, j)),
        out_shape=jax.ShapeDtypeStruct((DEPTH, MOD_ROWS, 6 * D), F32),
        compiler_params=_cparams(2),
        name="ada_table",
    )(cond, w_ada, b_ada.reshape(DEPTH, 1, 6 * D))


def _pending_specs(pend, mod, block_rows, row_index, row_fn):
    if pend is None:
        return [], []
    y, prev_layer = pend
    return [pl.BlockSpec((block_rows, D), row_index), _mod_spec(prev_layer, 5, row_fn)], [y, mod]


def _read_x(x_ref, pending_refs):
    x = x_ref[...]
    if pending_refs:
        y_ref, gate_ref = pending_refs
        x = x + gate_ref[...] * y_ref[...]
    return x


def _outproj_kernel(z_ref, w_ref, b_ref, gate_ref, x_ref, *rest):
    *pending, o_ref = rest
    y = _dot(z_ref[...], w_ref[...]) + b_ref[...]
    o_ref[...] = _read_x(x_ref, pending) + gate_ref[...] * y


def _outproj(x, z, w_bf16, bias, mod, layer, row_off, n_rows, pend=None):
    k = z.shape[1]
    off = row_off // TM
    first_lat = N_CTX // TM

    def row_fn(t):
        g = t + off
        return jnp.where(g < first_lat, 0, 1 + (g - first_lat) // (LAT_L // TM))

    p_specs, p_args = _pending_specs(pend, mod, TM, lambda t: (t + off, 0), row_fn)
    return pl.pallas_call(
        _outproj_kernel,
        grid=(n_rows // TM,),
        in_specs=[
            pl.BlockSpec((TM, k), lambda t: (t, 0)),
            _full_spec((k, D)),
            _vec_spec(D),
            _mod_spec(layer, 2, row_fn),
            pl.BlockSpec((TM, D), lambda t: (t + off, 0)),
        ] + p_specs,
        out_specs=pl.BlockSpec((TM, D), lambda t: (t + off, 0)),
        out_shape=jax.ShapeDtypeStruct((N_TOK, D), F32),
        input_output_aliases={4: 0},
        compiler_params=_cparams(1),
        name="outproj_residual",
    )(z, w_bf16, bias.reshape(1, D), mod, x, *p_args)


def _outproj_joint_kernel(zc_ref, zl_ref, w_ref, b_ref, gate_ref, xc_ref, xl_ref, o_ref):
    t = pl.program_id(0)
    for is_ctx, z_ref, x_ref in ((True, zc_ref, xc_ref), (False, zl_ref, xl_ref)):
        @pl.when((t < N_CTX // TM) == is_ctx)
        def _(z_ref=z_ref, x_ref=x_ref):
            y = _dot(z_ref[...], w_ref[...]) + b_ref[...]
            o_ref[...] = x_ref[...] + gate_ref[...] * y


def _outproj_joint(x_ctx, x_lat, z_ctx, z_lat, w_bf16, bias, mod, layer):
    k = z_ctx.shape[1]
    n_ctx_tiles = N_CTX // TM

    def ctx_block(t):
        return (jnp.minimum(t, n_ctx_tiles - 1), 0)

    def lat_block(t):
        return (jnp.maximum(t - n_ctx_tiles, 0), 0)

    return pl.pallas_call(
        _outproj_joint_kernel,
        grid=(N_TOK // TM,),
        in_specs=[
            pl.BlockSpec((TM, k), ctx_block),
            pl.BlockSpec((TM, k), lat_block),
            _full_spec((k, D)),
            _vec_spec(D),
            _mod_spec(layer, 2, _row_tm),
            pl.BlockSpec((TM, D), ctx_block),
            pl.BlockSpec((TM, D), lat_block),
        ],
        out_specs=pl.BlockSpec((TM, D), lambda t: (t, 0)),
        out_shape=jax.ShapeDtypeStruct((N_TOK, D), F32),
        compiler_params=_cparams(1),
        name="outproj_joint",
    )(z_ctx, z_lat, w_bf16, bias.reshape(1, D), mod, x_ctx, x_lat)


def _dft_mats(L):
    n2 = 2 * L
    k = np.arange(L)[:, None].astype(np.float64)
    n = np.arange(n2)[None, :].astype(np.float64)
    ang = 2.0 * np.pi * k * n / n2
    full = np.concatenate([np.cos(ang), -np.sin(ang)], axis=0)
    full[L, :] = np.cos(np.pi * np.arange(n2))
    fwd = full[:, :L]
    bwd = np.zeros((n2, L))
    bwd[:, 1:] = full[:, n2 - np.arange(1, L)]
    t = np.arange(L)[:, None].astype(np.float64)
    kk = np.arange(L)[None, :].astype(np.float64)
    ang_i = 2.0 * np.pi * t * kk / n2
    inv_re = np.cos(ang_i) / L
    inv_re[:, 0] = 1.0 / n2
    inv_im = -np.sin(ang_i) / L
    inv_im[:, 0] = np.cos(np.pi * np.arange(L)) / n2
    inv = np.concatenate([inv_re, inv_im], axis=1)
    return tuple(jnp.asarray(m, F32).astype(BF16) for m in (fwd, np.concatenate([fwd, bwd], axis=1), inv))


def _hyena_pos_emb(L):
    pos = np.arange(L, dtype=np.float64)
    bands = np.linspace(1e-4, HY_BANDS - 1, HY_BANDS)
    ang = (2.0 * np.pi * pos / L)[:, None] * bands[None, :]
    z = np.concatenate([(pos / L)[:, None], np.cos(ang), -np.sin(ang)], axis=-1)
    zp = np.zeros((L, HY_EMB_PAD))
    zp[:, :HY_EMB] = z
    return jnp.asarray(zp, F32)


def _hyena_filter_kernel(z_ref, w1_ref, b1_ref, fr_ref, w2_ref, b2_ref, w3f_ref, w3b_ref, ff_ref, o_ref, *, L, tn):
    j = pl.program_id(1)
    fr = fr_ref[...]
    f = jnp.sin(fr * (_dot_precise(z_ref[...], w1_ref[...]) + b1_ref[...]))
    f = jnp.sin(fr * (_dot_precise(f, w2_ref[...]) + b2_ref[...]))
    t_lin = lax.broadcasted_iota(jnp.int32, (L, tn), 0).astype(F32) / float(L - 1)
    ch = (lax.broadcasted_iota(jnp.int32, (L, tn), 1) + j * tn).astype(F32)
    max_decay = math.log(HY_DECAY_TARGET) / HY_FAST_DECAY
    min_decay = math.log(HY_DECAY_TARGET) / HY_SLOW_DECAY
    deltas = min_decay + ch * ((max_decay - min_decay) / float(D - 1))
    window = jnp.exp(-t_lin * jnp.abs(deltas))
    kf = _dot_precise(f, w3f_ref[...]) * window
    kb = _dot_precise(f, w3b_ref[...]) * window
    taps = jnp.concatenate([kf, kb], axis=0).astype(BF16)
    o_ref[...] = _dot(ff_ref[...], taps)


def _hyena_filters(L, ff, f_w1, f_b1, f_freq, f_w2, f_b2, f_w3):
    tn = 512
    nj = D // tn
    w1p = jnp.zeros((HY_EMB_PAD, HY_HID), F32).at[:HY_EMB].set(f_w1)
    kern = functools.partial(_hyena_filter_kernel, L=L, tn=tn)
    return pl.pallas_call(
        kern,
        grid=(2, nj),
        in_specs=[
            _full_spec((L, HY_EMB_PAD)),
            _full_spec((HY_EMB_PAD, HY_HID)),
            _vec_spec(HY_HID),
            _vec_spec(HY_HID),
            _full_spec((HY_HID, HY_HID)),
            _vec_spec(HY_HID),
            pl.BlockSpec((HY_HID, tn), lambda o, j: (0, o * nj + j)),
            pl.BlockSpec((HY_HID, tn), lambda o, j: (0, (2 + o) * nj + j)),
            _full_spec((2 * L, 2 * L)),
        ],
        out_specs=pl.BlockSpec((None, 2 * L, tn), lambda o, j: (o, 0, j)),
        out_shape=jax.ShapeDtypeStruct((2, 2 * L, D), F32),
        compiler_params=_cparams(2),
        name=f"hyena_filters_L{L}",
    )(_hyena_pos_emb(L), w1p, f_b1.reshape(1, -1), f_freq.reshape(1, -1), f_w2, f_b2.reshape(1, -1),
      f_w3, f_w3, ff)


def _hyena_in_kernel(xc_ref, xl_ref, g_ref, sc_ref, sh_ref, w_ref, b_ref, cw_ref, cb_ref, o_ref, h_scr):
    t = pl.program_id(0)
    first = pl.program_id(1) == 0

    for is_ctx, x_ref in ((True, xc_ref), (False, xl_ref)):
        @pl.when(first & ((t < N_CTX // TM_BIG) == is_ctx))
        def _(x_ref=x_ref):
            h_scr[...] = _norm_mod(x_ref[...], g_ref[...], sc_ref[...], sh_ref[...]).astype(BF16)

    u = _dot(h_scr[...], w_ref[...]) + b_ref[...]
    cw = cw_ref[...]
    o_ref[...] = pltpu.roll(u, 1, 0) * cw[0:1] + u * cw[1:2] + pltpu.roll(u, TM_BIG - 1, 0) * cw[2:3] + cb_ref[...]
    is_ctx = (t < N_CTX // TM_BIG).astype(F32)
    for start in range(0, TM_BIG, CTX_L):
        f = 1.0 if start % LAT_L == 0 else is_ctx
        before = (start - 1) % TM_BIG
        o_ref[start:start + 1, :] = o_ref[start:start + 1, :] - f * (u[before:before + 1] * cw[0:1])
        end = start + CTX_L - 1
        g = 1.0 if (end + 1) % LAT_L == 0 else is_ctx
        after = (end + 1) % TM_BIG
        o_ref[end:end + 1, :] = o_ref[end:end + 1, :] - g * (u[after:after + 1] * cw[2:3])


def _hyena_in(x_ctx, x_lat, mod, layer, norm_g, w_in_bf16, b_in, conv_w, conv_b):
    n_ctx_tiles = N_CTX // TM_BIG
    return pl.pallas_call(
        _hyena_in_kernel,
        grid=(N_TOK // TM_BIG, 3),
        in_specs=[
            pl.BlockSpec((TM_BIG, D), lambda t, p: (jnp.minimum(t, n_ctx_tiles - 1), 0)),
            pl.BlockSpec((TM_BIG, D), lambda t, p: (jnp.maximum(t - n_ctx_tiles, 0), 0)),
            _vec_spec(D),
            _mod_spec(layer, 1, _row_big),
            _mod_spec(layer, 0, _row_big),
            pl.BlockSpec((D, D), lambda t, p: (0, p)),
            pl.BlockSpec((1, D), lambda t, p: (0, p)),
            pl.BlockSpec((3, D), lambda t, p: (0, p)),
            pl.BlockSpec((1, D), lambda t, p: (0, p)),
        ],
        out_specs=pl.BlockSpec((TM_BIG, D), lambda t, p: (t, p)),
        out_shape=jax.ShapeDtypeStruct((N_TOK, 3 * D), F32),
        scratch_shapes=[pltpu.VMEM((TM_BIG, D), BF16)],
        compiler_params=_cparams(2),
        name="hyena_in",
    )(x_ctx, x_lat, norm_g.reshape(1, D), mod, mod, w_in_bf16, b_in.reshape(1, -1), conv_w, conv_b.reshape(1, -1))


def _hyena_conv_kernel(v_ref, x1_ref, x2_ref, kh_ref, skip_ref, fwd_ref, inv_ref, o_ref, *, L):
    fwd = fwd_ref[...]
    inv = inv_ref[...]
    row0 = lax.broadcasted_iota(jnp.int32, (L, v_ref.shape[1]), 0) == 0

    def long_conv(z, order):
        zh = _dot(fwd, z.astype(BF16))
        zr, zi = zh[:L], zh[L:]
        kr, ki = kh_ref[order, :L, :], kh_ref[order, L:, :]
        pr = jnp.where(row0, zr * kr, zr * kr - zi * ki)
        pi = jnp.where(row0, zi * ki, zr * ki + zi * kr)
        prod = jnp.concatenate([pr, pi], axis=0).astype(BF16)
        return _dot(inv, prod) + z * skip_ref[order:order + 1, :]

    z = x1_ref[...] * long_conv(v_ref[...], 0)
    z = x2_ref[...] * long_conv(z, 1)
    o_ref[...] = z.astype(BF16)


def _hyena_conv(u, khat, skip, fwd, inv, L, n_batch, row_off, tn):
    nj = D // tn
    rb = row_off // L
    kern = functools.partial(_hyena_conv_kernel, L=L)
    return pl.pallas_call(
        kern,
        grid=(nj, n_batch),
        in_specs=[
            pl.BlockSpec((L, tn), lambda j, b: (rb + b, j)),
            pl.BlockSpec((L, tn), lambda j, b: (rb + b, nj + j)),
            pl.BlockSpec((L, tn), lambda j, b: (rb + b, 2 * nj + j)),
            pl.BlockSpec((2, 2 * L, tn), lambda j, b: (0, 0, j)),
            pl.BlockSpec((2, tn), lambda j, b: (0, j)),
            _full_spec((2 * L, L)),
            _full_spec((L, 2 * L)),
        ],
        out_specs=pl.BlockSpec((L, tn), lambda j, b: (b, j)),
        out_shape=jax.ShapeDtypeStruct((n_batch * L, D), BF16),
        compiler_params=_cparams(2),
        name=f"hyena_conv_L{L}",
    )(u, u, u, khat, skip, fwd, inv)


GLA_PROJ = 2 * GLA_H * GLA_DK + 2 * GLA_H * GLA_DV
GLA_COLS = GLA_PROJ + 2 * GLA_H * GLA_DK
GK1_PAD = 128


def _gla_proj_kernel(x_ref, g_ref, sc_ref, sh_ref, w_ref, wg1_ref, wg2_ref, bg_ref, *rest):
    *pending, o_ref = rest
    h = _norm_mod(_read_x(x_ref, pending), g_ref[...], sc_ref[...], sh_ref[...]).astype(BF16)
    p = _dot(h, w_ref[...])
    nq = GLA_H * GLA_DK
    o_ref[:, 0:nq] = p[:, 0:nq] * (GLA_DK ** -0.5)
    o_ref[:, nq:nq + nq + GLA_H * GLA_DV] = p[:, nq:nq + nq + GLA_H * GLA_DV]
    o_ref[:, 2 * nq + GLA_H * GLA_DV:GLA_PROJ] = _silu(p[:, 2 * nq + GLA_H * GLA_DV:GLA_PROJ])
    low = _dot(h, wg1_ref[...]).astype(BF16)
    gk = _dot(low, wg2_ref[...]) + bg_ref[...]
    o_ref[:, GLA_PROJ:GLA_COLS] = _log_sigmoid(gk) / GLA_NORMALIZER


def _gla_proj(x, mod, layer, norm_g, w_cat, wg1, wg2, bg, pend=None):
    p_specs, p_args = _pending_specs(pend, mod, TM, lambda t: (t, 0), _row_tm)
    return pl.pallas_call(
        _gla_proj_kernel,
        grid=(N_TOK // TM,),
        in_specs=[
            pl.BlockSpec((TM, D), lambda t: (t, 0)),
            _vec_spec(D),
            _mod_spec(layer, 1, _row_tm),
            _mod_spec(layer, 0, _row_tm),
            _full_spec((D, GLA_PROJ)),
            _full_spec((D, GK1_PAD)),
            _full_spec((GK1_PAD, 2 * GLA_H * GLA_DK)),
            _vec_spec(2 * GLA_H * GLA_DK),
        ] + p_specs,
        out_specs=pl.BlockSpec((TM, GLA_COLS), lambda t: (t, 0)),
        out_shape=jax.ShapeDtypeStruct((N_TOK, GLA_COLS), F32),
        compiler_params=_cparams(1),
        name="gla_proj",
    )(x, norm_g.reshape(1, D), mod, mod, w_cat, wg1, wg2, bg, *p_args)


def _gla_core_kernel(*refs, L, has_s0, hps, n_proj):
    q_ref, k_ref, v_ref, g_ref, gkf_ref, gkb_ref, tri_ref, ng_ref = refs[:8]
    rest = list(refs[8:])
    s0_ref = rest.pop(0) if has_s0 else None
    proj_refs = [rest.pop(0) for _ in range(n_proj)]
    o_ref, sf_ref, acc = rest[:3]
    o_gated = rest[3] if n_proj else o_ref
    C = GLA_CHUNK
    n = L // C
    ri = lax.broadcasted_iota(jnp.int32, (C, C), 0)
    ci = lax.broadcasted_iota(jnp.int32, (C, C), 1)
    nt_dims = (((1,), (1,)), ((), ()))
    tn_dims = (((0,), (0,)), ((), ()))
    whole = n <= GLA_WHOLE_SEQ_CHUNKS
    if whole:
        rl = lax.broadcasted_iota(jnp.int32, (L, L), 0)
        cl = lax.broadcasted_iota(jnp.int32, (L, L), 1)
        same_chunk = (rl >> (C.bit_length() - 1)) == (cl >> (C.bit_length() - 1))

    for hh in range(hps):
        kc = slice(hh * GLA_DK, (hh + 1) * GLA_DK)
        vc = slice(hh * GLA_DV, (hh + 1) * GLA_DV)
        for direction, gk_ref in enumerate((gkf_ref, gkb_ref)):
            keep = (ci <= ri) if direction == 0 else (ci >= ri)
            last = C - 1 if direction == 0 else 0
            gk_hi, gk_lo = _split(gk_ref[:, kc])
            gk_parts = jnp.concatenate([gk_hi, gk_lo], axis=1)
            if whole:
                b_all = _dot(tri_ref[direction], gk_parts)
                b_all = b_all[:, :GLA_DK] + b_all[:, GLA_DK:]
                qe_all = (q_ref[:, kc] * jnp.exp(b_all)).astype(BF16)
                ke_all = (k_ref[:, kc] * jnp.exp(-b_all)).astype(BF16)
                keep_all = same_chunk & ((cl <= rl) if direction == 0 else (cl >= rl))
                s_all = lax.dot_general(qe_all, ke_all, nt_dims, preferred_element_type=F32)
                s_all = jnp.where(keep_all, s_all, 0.0).astype(BF16)
                o_intra = _dot(s_all, v_ref[:, vc].astype(BF16))
            st = s0_ref[direction, hh].T if has_s0 else jnp.zeros((GLA_DV, GLA_DK), F32)
            order = range(n) if direction == 0 else range(n - 1, -1, -1)
            for c in order:
                rows = slice(c * C, (c + 1) * C)
                if whole:
                    b = b_all[rows]
                else:
                    b = _dot(tri_ref[direction], gk_parts[rows])
                    b = b[:, :GLA_DK] + b[:, GLA_DK:]
                b_last = b[last:last + 1, :]
                k = k_ref[rows, kc]
                v = v_ref[rows, vc].astype(BF16)
                kd = (k * jnp.exp(b_last - b)).astype(BF16)
                if whole:
                    qe = qe_all[rows]
                    o = o_intra[rows]
                else:
                    qe = (q_ref[rows, kc] * jnp.exp(b)).astype(BF16)
                    ke = (k * jnp.exp(-b)).astype(BF16)
                    scores = lax.dot_general(qe, ke, nt_dims, preferred_element_type=F32)
                    o = _dot(jnp.where(keep, scores, 0.0).astype(BF16), v)
                o = o + lax.dot_general(qe, st.astype(BF16), nt_dims, preferred_element_type=F32)
                if direction == 0:
                    acc[rows, vc] = o
                else:
                    acc[rows, vc] = acc[rows, vc] + o
                st = jnp.exp(b_last) * st + lax.dot_general(v, kd, tn_dims, preferred_element_type=F32)
            sf_ref[direction, hh] = st.T

        o = acc[:, vc]
        o = o * lax.rsqrt(jnp.mean(o * o, axis=-1, keepdims=True) + EPS) * ng_ref[...]
        o_gated[:, vc] = (o * g_ref[:, vc]).astype(BF16)

    if n_proj:
        w_ref, gate_ref, x_ref, *pending = proj_refs
        o_ref[...] = _read_x(x_ref, pending) + gate_ref[...] * _dot(o_gated[...], w_ref[...])


def _gla_core(proj, tri, norm_g, s0, L, n_batch, row_off, hps, out_proj=None):
    rb = row_off // L
    H = GLA_H
    nh = H // hps
    has_s0 = s0 is not None
    kb, vb = GLA_DK * hps, GLA_DV * hps
    in_specs = [
        pl.BlockSpec((L, kb), lambda b, h: (rb + b, h)),
        pl.BlockSpec((L, kb), lambda b, h: (rb + b, nh + h)),
        pl.BlockSpec((L, vb), lambda b, h: (rb + b, (2 * H * GLA_DK) // vb + h)),
        pl.BlockSpec((L, vb), lambda b, h: (rb + b, (2 * H * GLA_DK) // vb + nh + h)),
        pl.BlockSpec((L, kb), lambda b, h: (rb + b, GLA_PROJ // kb + h)),
        pl.BlockSpec((L, kb), lambda b, h: (rb + b, GLA_PROJ // kb + nh + h)),
        _full_spec(tri.shape),
        _vec_spec(GLA_DV),
    ]
    args = [proj] * 6 + [tri, norm_g.reshape(1, GLA_DV)]
    state_spec = pl.BlockSpec((None, 2, hps, GLA_DK, GLA_DV), lambda b, h: (b, 0, h, 0, 0))
    if has_s0:
        in_specs.append(state_spec)
        args.append(s0)
    first_spec = pl.BlockSpec((L, vb), lambda b, h: (b, h))
    first_shape = jax.ShapeDtypeStruct((n_batch * L, H * GLA_DV), BF16)
    scratch = [pltpu.VMEM((L, vb), F32)]
    aliases, n_proj = {}, 0
    if out_proj is not None:
        assert nh == 1, "the fused output projection needs every head in the step"
        x, w_o, mod, layer, pend = out_proj

        def row_fn(b, h):
            return 1 + b if row_off > 0 else 0

        x_spec = pl.BlockSpec((L, D), lambda b, h: (rb + b, 0))
        p_specs, p_args = _pending_specs(pend, mod, L, lambda b, h: (rb + b, 0), row_fn)
        aliases = {len(args) + 2: 0}
        in_specs += [_full_spec((H * GLA_DV, D)), _mod_spec(layer, 2, row_fn), x_spec] + p_specs
        args += [w_o, mod, x] + p_args
        n_proj = 3 + len(p_args)
        first_spec, first_shape = x_spec, jax.ShapeDtypeStruct((N_TOK, D), F32)
        scratch.append(pltpu.VMEM((L, vb), BF16))
    kern = functools.partial(_gla_core_kernel, L=L, has_s0=has_s0, hps=hps, n_proj=n_proj)
    return pl.pallas_call(
        kern,
        grid=(n_batch, nh),
        in_specs=in_specs,
        out_specs=[first_spec, state_spec],
        out_shape=[first_shape, jax.ShapeDtypeStruct((n_batch, 2, H, GLA_DK, GLA_DV), F32)],
        scratch_shapes=scratch,
        input_output_aliases=aliases,
        compiler_params=_cparams(2),
        name=f"gla_core_L{L}",
    )(*args)


def _fnet_mats(L):
    c = np.arange(FNET_C)
    ang_c = 2.0 * np.pi * np.outer(c, c) / FNET_C
    chan = np.concatenate([np.cos(ang_c), np.sin(ang_c)], axis=1) / math.sqrt(FNET_C)
    t = np.arange(L)
    ang_l = 2.0 * np.pi * np.outer(t, t) / L
    seq = np.concatenate([np.cos(ang_l), -np.sin(ang_l)], axis=1) / math.sqrt(L)
    return jnp.asarray(chan, F32).astype(BF16), jnp.asarray(seq, F32).astype(BF16)


def _fnet_kernel(x_ref, g_ref, sc_ref, sh_ref, gate_ref, chan_ref, seq_ref, w_ref, b_ref, *rest):
    *pending, o_ref = rest
    x = _read_x(x_ref, pending)
    h = _norm_mod(x, g_ref[...], sc_ref[...], sh_ref[...]).astype(BF16)
    chan = chan_ref[...]
    cos_parts, sin_parts = [], []
    for gi in range(FNET_GROUPS):
        cs = _dot(h[:, gi * FNET_C:(gi + 1) * FNET_C], chan)
        cos_parts.append(cs[:, :FNET_C])
        sin_parts.append(cs[:, FNET_C:])
    stacked = jnp.concatenate([jnp.concatenate(cos_parts, axis=1), jnp.concatenate(sin_parts, axis=1)], axis=0)
    mixed = _dot(seq_ref[...], stacked.astype(BF16))
    y = _dot(mixed.astype(BF16), w_ref[...]) + b_ref[...]
    o_ref[...] = x + gate_ref[...] * y


def _fnet(x, mod, layer, norm_g, chan, seq, w_bf16, bias, L, n_batch, row_off, pend=None):
    rb = row_off // L
    lat = row_off > 0

    def row_fn(b):
        return 1 + b if lat else 0

    p_specs, p_args = _pending_specs(pend, mod, L, lambda b: (rb + b, 0), row_fn)
    return pl.pallas_call(
        _fnet_kernel,
        grid=(n_batch,),
        in_specs=[
            pl.BlockSpec((L, D), lambda b: (rb + b, 0)),
            _vec_spec(D),
            _mod_spec(layer, 1, row_fn),
            _mod_spec(layer, 0, row_fn),
            _mod_spec(layer, 2, row_fn),
            _full_spec((FNET_C, 2 * FNET_C)),
            _full_spec((L, 2 * L)),
            _full_spec((D, D)),
            _vec_spec(D),
        ] + p_specs,
        out_specs=pl.BlockSpec((L, D), lambda b: (rb + b, 0)),
        out_shape=jax.ShapeDtypeStruct((N_TOK, D), F32),
        input_output_aliases={0: 0},
        compiler_params=_cparams(1),
        name=f"fnet_L{L}",
    )(x, norm_g.reshape(1, D), mod, mod, mod, chan, seq, w_bf16, bias.reshape(1, D), *p_args)


def _window_bounds(n, k):
    t = np.arange(n)
    lo, hi = k // 2, k - k // 2 - 1
    return np.maximum(t - lo, 0), np.minimum(t + hi + 1, n)


def _pool_mats(L, grid_rows):
    mats, inv = [], []
    for k in POOL_WINDOWS:
        if grid_rows is None:
            s, e = _window_bounds(L, k)
            idx = np.arange(L)[None, :]
            m = ((idx >= s[:, None]) & (idx < e[:, None])).astype(np.float64)
            cnt = (e - s).astype(np.float64)
        else:
            sr, er = _window_bounds(grid_rows, k)
            sc, ec = _window_bounds(GRID_W, k)
            ir = np.arange(grid_rows)[None, :]
            ic = np.arange(GRID_W)[None, :]
            mr = ((ir >= sr[:, None]) & (ir < er[:, None])).astype(np.float64)
            mc = ((ic >= sc[:, None]) & (ic < ec[:, None])).astype(np.float64)
            m = np.kron(mr, mc)
            cnt = np.kron((er - sr).astype(np.float64), (ec - sc).astype(np.float64))
        mats.append(m)
        inv.append(1.0 / cnt)
    return jnp.asarray(np.stack(mats), BF16), jnp.asarray(np.stack(inv)[:, :, None], F32)


def _pool_kernel(x_ref, g_ref, sc_ref, sh_ref, gate_ref, m_ref, ic_ref, w_ref, b_ref, ps_ref, *rest):
    *pending, o_ref = rest
    x = _read_x(x_ref, pending)
    h = _norm_mod(x, g_ref[...], sc_ref[...], sh_ref[...])
    outs = []
    for gi in range(len(POOL_WINDOWS)):
        hg = h[:, gi * POOL_G:(gi + 1) * POOL_G]
        hi, lo = _split(hg)
        m = m_ref[gi]
        mean = (_dot(m, hi) + _dot(m, lo)) * ic_ref[gi]
        outs.append(_dot((mean - hg).astype(BF16), w_ref[gi]))
    y = (jnp.concatenate(outs, axis=1) + b_ref[...]) * ps_ref[...]
    o_ref[...] = x + gate_ref[...] * y


def _pool(x, mod, layer, norm_g, mats, inv_cnt, w_bf16, bias, scale, L, n_batch, row_off, pend=None):
    rb = row_off // L
    lat = row_off > 0
    G = len(POOL_WINDOWS)

    def row_fn(b):
        return 1 + b if lat else 0

    p_specs, p_args = _pending_specs(pend, mod, L, lambda b: (rb + b, 0), row_fn)
    return pl.pallas_call(
        _pool_kernel,
        grid=(n_batch,),
        in_specs=[
            pl.BlockSpec((L, D), lambda b: (rb + b, 0)),
            _vec_spec(D),
            _mod_spec(layer, 1, row_fn),
            _mod_spec(layer, 0, row_fn),
            _mod_spec(layer, 2, row_fn),
            _full_spec((G, L, L)),
            _full_spec((G, L, 1)),
            _full_spec((G, POOL_G, POOL_G)),
            _vec_spec(D),
            _vec_spec(D),
        ] + p_specs,
        out_specs=pl.BlockSpec((L, D), lambda b: (rb + b, 0)),
        out_shape=jax.ShapeDtypeStruct((N_TOK, D), F32),
        input_output_aliases={0: 0},
        compiler_params=_cparams(1),
        name=f"pool_L{L}",
    )(x, norm_g.reshape(1, D), mod, mod, mod, mats, inv_cnt, w_bf16, bias.reshape(1, D), scale.reshape(1, D),
      *p_args)


ROUTER_PAD = 128
LANES = 128
D_EXT = D + LANES
MOE_TILE = 1024
MOE_MAX_TILES = N_TOK // MOE_TILE + MOE_GROUPS
MOE_ROWS = MOE_MAX_TILES * MOE_TILE
MOE_TILE_SHIFT = MOE_TILE.bit_length() - 1
assert 1 << MOE_TILE_SHIFT == MOE_TILE
MOE_STEP_EXPERTS = 2
MOE_STEPS = MOE_PER_GROUP // MOE_STEP_EXPERTS
MOE_ROW_BLOCK = 256
MOE_DMA_CHUNK = 64
MOE_CHUNK_SHIFT = MOE_DMA_CHUNK.bit_length() - 1
assert 1 << MOE_CHUNK_SHIFT == MOE_DMA_CHUNK and MOE_MAX_TILES % 2 == 0
MOE_Y_ROWS = N_TOK + 2 * MOE_DMA_CHUNK


ROUTE_ROWS = 8
TM_ROUTE = TM_BIG


def _moe_route_kernel(x_ref, g_ref, sc_ref, sh_ref, wr_ref, tri_ref, h3_ref, route_ref, cnt_ref, carry):
    t = pl.program_id(0)
    refs = (x_ref, g_ref, sc_ref, sh_ref, wr_ref, tri_ref, h3_ref, route_ref, cnt_ref, carry)
    pl.when(t < N_TOK // TM_ROUTE)(functools.partial(_moe_route_tile, t, *refs))

    @pl.when(t == N_TOK // TM_ROUTE)
    def _():
        h3_ref[...] = jnp.zeros_like(h3_ref)


def _moe_route_tile(t, x_ref, g_ref, sc_ref, sh_ref, wr_ref, tri_ref, h3_ref, route_ref, cnt_ref, carry):
    @pl.when(t == 0)
    def _():
        carry[...] = jnp.zeros_like(carry)

    h = _norm_mod(x_ref[...], g_ref[...], sc_ref[...], sh_ref[...])
    w_hi, w_lo = _split(wr_ref[...])
    h_hi, h_lo = _split(h)
    nt = (((1,), (1,)), ((), ()))
    logits = (lax.dot_general(w_hi, h_hi, nt, preferred_element_type=F32)
              + (lax.dot_general(w_hi, h_lo, nt, preferred_element_type=F32)
                 + lax.dot_general(w_lo, h_hi, nt, preferred_element_type=F32)))
    neg = jnp.float32(-jnp.inf)
    r8 = lax.broadcasted_iota(jnp.int32, (ROUTE_ROWS, TM_ROUTE), 0)
    r16 = lax.broadcasted_iota(jnp.int32, (MOE_E, TM_ROUTE), 0)
    gl = jnp.where(r8 < MOE_GROUPS, logits[MOE_E:MOE_E + ROUTE_ROWS], neg)
    g_max = jnp.max(gl, axis=0, keepdims=True)
    g_idx = jnp.min(jnp.where(gl == g_max, r8, ROUTE_ROWS), axis=0, keepdims=True)
    p_grp = 1.0 / jnp.sum(jnp.exp(gl - g_max), axis=0, keepdims=True)
    in_grp = (r16 >> 2) == g_idx
    el = jnp.where(in_grp, logits[:MOE_E], neg)
    m1 = jnp.max(el, axis=0, keepdims=True)
    i1 = jnp.min(jnp.where(el == m1, r16, MOE_E), axis=0, keepdims=True)
    z = jnp.sum(jnp.exp(el - m1), axis=0, keepdims=True)
    el2 = jnp.where(r16 == i1, neg, el)
    m2 = jnp.max(el2, axis=0, keepdims=True)
    i2 = jnp.min(jnp.where(el2 == m2, r16, MOE_E), axis=0, keepdims=True)
    p1 = 1.0 / z
    p2 = jnp.exp(m2 - m1) / z
    tot = p1 + p2
    eid = r8 + MOE_PER_GROUP * g_idx
    in4 = r8 < MOE_PER_GROUP
    cw4 = (jnp.where(in4 & (eid == i1), p_grp * (p1 / tot), 0.0)
           + jnp.where(in4 & (eid == i2), p_grp * (p2 / tot), 0.0))
    member = jnp.where(r8 == g_idx, 1.0, 0.0)
    before = _dot(member.astype(BF16), tri_ref[...]) + carry[:, 0:1]
    rank = jnp.sum(jnp.where(r8 == g_idx, before, 0.0), axis=0, keepdims=True)
    carry[...] = carry[...] + jnp.sum(member, axis=1, keepdims=True)
    cnt_ref[...] = carry[...].astype(jnp.int32)
    route_ref[...] = jnp.where(r8 == 0, g_idx, jnp.where(r8 == 1, rank.astype(jnp.int32), 0))
    h3_ref[:, :D] = h
    cw_rows = jnp.concatenate([cw4, jnp.zeros((LANES - ROUTE_ROWS, TM_ROUTE), F32)], axis=0)
    h3_ref[:, D:] = cw_rows.T


def _moe_route(x, mod, layer, norm_g, w_router_t, tri):
    nt = N_TOK // TM_ROUTE

    def tok_tile(t):
        return jnp.minimum(t, nt - 1)

    return pl.pallas_call(
        _moe_route_kernel,
        grid=(nt + 1,),
        in_specs=[
            pl.BlockSpec((TM_ROUTE, D), lambda t: (tok_tile(t), 0)),
            _vec_spec(D),
            _mod_spec(layer, 4, lambda t: _row_big(tok_tile(t))),
            _mod_spec(layer, 3, lambda t: _row_big(tok_tile(t))),
            _full_spec((ROUTER_PAD, D)),
            _full_spec((TM_ROUTE, TM_ROUTE)),
        ],
        out_specs=[
            pl.BlockSpec((TM_ROUTE, D_EXT), lambda t: (t, 0)),
            pl.BlockSpec((ROUTE_ROWS, TM_ROUTE), lambda t: (0, tok_tile(t))),
            pl.BlockSpec((ROUTE_ROWS, LANES), lambda t: (0, 0)),
        ],
        out_shape=[
            jax.ShapeDtypeStruct((N_TOK + TM_ROUTE, D_EXT), F32),
            jax.ShapeDtypeStruct((ROUTE_ROWS, N_TOK), jnp.int32),
            jax.ShapeDtypeStruct((ROUTE_ROWS, LANES), jnp.int32),
        ],
        scratch_shapes=[pltpu.VMEM((ROUTE_ROWS, LANES), F32)],
        compiler_params=_cparams(1),
        name="moe_route",
    )(x, norm_g.reshape(1, D), mod, mod, w_router_t, tri)


def _moe_invert_kernel(pos_ref, lo_ref, hi_ref, src_ref):
    def mark(j, carry):
        parity = lax.shift_right_logical(j, jnp.int32(MOE_TILE_SHIFT - MOE_CHUNK_SHIFT)) & 1
        first = N_TOK + MOE_DMA_CHUNK * parity
        base = j * MOE_DMA_CHUNK
        for rr in range(MOE_DMA_CHUNK):
            src_ref[base + rr] = first + rr
        return carry

    for g in range(MOE_GROUPS + 1):
        lax.fori_loop(lo_ref[g], hi_ref[g], mark, 0)

    def place(n, carry):
        src_ref[pos_ref[n]] = n
        return carry

    lax.fori_loop(0, N_TOK, place, 0, unroll=16)


def _moe_invert(pos, mark_lo, mark_hi):
    smem = pl.BlockSpec(memory_space=pltpu.SMEM)
    return pl.pallas_call(
        _moe_invert_kernel,
        in_specs=[smem, smem, smem],
        out_specs=smem,
        out_shape=jax.ShapeDtypeStruct((MOE_ROWS,), jnp.int32),
        name="moe_invert",
    )(pos, mark_lo, mark_hi)


def _moe_expert_kernel(src_ref, grp_ref, nact_ref, nchunk_ref, h_hbm, w1_ref, w3_ref, w2_ref, y_hbm,
                       xb0, xb1, ab0, ab1, gsem, ssem):
    t = pl.program_id(0)
    k = pl.program_id(1)
    n_active = nact_ref[0]
    last = n_active - 1
    T = MOE_TILE
    CH = MOE_DMA_CHUNK
    xbufs, accs = (xb0, xb1), (ab0, ab1)

    def gather_row(tile, slot, base, rr):
        tok = src_ref[tile * T + base + rr]
        rows = xbufs[slot].at[pl.ds(base, CH), :]
        return pltpu.make_async_copy(h_hbm.at[pl.ds(tok, 1), :], rows.at[pl.ds(rr, 1), :], gsem.at[slot])

    def scatter_row(tile, slot, base, rr):
        dst = src_ref[tile * T + base + rr]
        rows = accs[slot].at[pl.ds(base, CH), :]
        return pltpu.make_async_copy(rows.at[pl.ds(rr, 1), :], y_hbm.at[pl.ds(dst, 1), :], ssem.at[slot])

    def start_rows(make, tile, slot):
        def chunk(c, carry):
            base = pl.multiple_of(c * CH, CH)
            for rr in range(CH):
                make(tile, slot, base, rr).start()
            return carry

        lax.fori_loop(0, nchunk_ref[tile], chunk, 0)

    def wait_rows(tile, slot, gather):
        def chunk(c, carry):
            if gather:
                pltpu.make_async_copy(h_hbm.at[pl.ds(0, CH), :], xbufs[slot].at[pl.ds(0, CH), :], gsem.at[slot]).wait()
            else:
                pltpu.make_async_copy(accs[slot].at[pl.ds(0, CH), :], y_hbm.at[pl.ds(0, CH), :], ssem.at[slot]).wait()
            return carry

        lax.fori_loop(0, nchunk_ref[tile], chunk, 0)

    def step(slot):
        other = 1 - slot
        xb, acc = xbufs[slot], accs[slot]

        @pl.when(k == 0)
        def _():
            if slot == 0:
                @pl.when(t == 0)
                def _():
                    xb0[...] = jnp.zeros_like(xb0)
                    xb1[...] = jnp.zeros_like(xb1)
                    ab0[...] = jnp.zeros_like(ab0)
                    ab1[...] = jnp.zeros_like(ab1)
                    dump = pltpu.make_async_copy(ab0.at[pl.ds(0, 2 * CH), :], y_hbm.at[pl.ds(N_TOK, 2 * CH), :],
                                                 ssem.at[0])
                    dump.start()
                    dump.wait()
                    start_rows(gather_row, 0, 0)

            wait_rows(t, slot, True)

            @pl.when(t < last)
            def _():
                start_rows(gather_row, t + 1, other)

            @pl.when(t >= 2)
            def _():
                wait_rows(t - 2, slot, False)

        def experts(m):
            x = xb[:m, :D].astype(BF16)
            lane = lax.broadcasted_iota(jnp.int32, (m, LANES), 1)
            total = jnp.where(k > 0, acc[:m], 0.0)
            for j in range(MOE_STEP_EXPERTS):
                a = _dot(x, w1_ref[j].astype(BF16))
                b = _dot(x, w3_ref[j].astype(BF16))
                cwk = jnp.sum(jnp.where(lane == k * MOE_STEP_EXPERTS + j, xb[:m, D:], 0.0), axis=-1, keepdims=True)
                hid = (_silu(a) * b * cwk).astype(BF16)
                total = total + _dot(hid, w2_ref[j].astype(BF16))
            acc[:m] = total

        blocks = (nchunk_ref[t] * CH + MOE_ROW_BLOCK - 1) // MOE_ROW_BLOCK
        for nb in range(1, T // MOE_ROW_BLOCK + 1):
            pl.when(blocks == nb)(functools.partial(experts, nb * MOE_ROW_BLOCK))

        @pl.when(k == MOE_STEPS - 1)
        def _():
            start_rows(scatter_row, t, slot)

            @pl.when(t == last)
            def _():
                wait_rows(t, slot, False)

                @pl.when(t >= 1)
                def _():
                    wait_rows(t - 1, other, False)

    for slot in (0, 1):
        pl.when((t < n_active) & (t % 2 == slot))(functools.partial(step, slot))


def _moe_experts(h_ext, src, tile_group, n_active, n_chunk, layer, w1, w3, w2):
    T = MOE_TILE

    def w_index(t, k, src_ref, grp_ref, nact_ref, nchunk_ref):
        last = nact_ref[0] - 1
        blk = jnp.where(t <= last, grp_ref[t] * MOE_STEPS + k, grp_ref[last] * MOE_STEPS + MOE_STEPS - 1)
        return (layer, blk, 0, 0)

    grid_spec = pltpu.PrefetchScalarGridSpec(
        num_scalar_prefetch=4,
        grid=(MOE_MAX_TILES, MOE_STEPS),
        in_specs=[
            pl.BlockSpec(memory_space=pl.ANY),
            pl.BlockSpec((None, MOE_STEP_EXPERTS, D, MOE_HID), w_index),
            pl.BlockSpec((None, MOE_STEP_EXPERTS, D, MOE_HID), w_index),
            pl.BlockSpec((None, MOE_STEP_EXPERTS, MOE_HID, D), w_index),
        ],
        out_specs=pl.BlockSpec(memory_space=pl.ANY),
        scratch_shapes=[
            pltpu.VMEM((T, D_EXT), F32),
            pltpu.VMEM((T, D_EXT), F32),
            pltpu.VMEM((T, D), F32),
            pltpu.VMEM((T, D), F32),
            pltpu.SemaphoreType.DMA((2,)),
            pltpu.SemaphoreType.DMA((2,)),
        ],
    )
    return pl.pallas_call(
        _moe_expert_kernel,
        grid_spec=grid_spec,
        out_shape=jax.ShapeDtypeStruct((MOE_Y_ROWS, D), F32),
        compiler_params=_cparams(2),
        name="moe_experts",
    )(src, tile_group, n_active, n_chunk, h_ext, w1, w3, w2)


def _moe_combine_kernel(y_ref, gate_ref, x_ref, o_ref):
    o_ref[...] = x_ref[...] + gate_ref[...] * y_ref[...]


def _moe_combine(x, y3, mod, layer):
    return pl.pallas_call(
        _moe_combine_kernel,
        grid=(N_TOK // TM,),
        in_specs=[
            pl.BlockSpec((TM, D), lambda t: (t, 0)),
            _mod_spec(layer, 5, _row_tm),
            pl.BlockSpec((TM, D), lambda t: (t, 0)),
        ],
        out_specs=pl.BlockSpec((TM, D), lambda t: (t, 0)),
        out_shape=jax.ShapeDtypeStruct((N_TOK, D), F32),
        input_output_aliases={2: 0},
        compiler_params=_cparams(1),
        name="moe_combine",
    )(y3, mod, x)


def _moe(x, mod, layer, norm_g, w_rg, w_re, w1, w3, w2, tri, final_g=None):
    w_router_t = jnp.zeros((ROUTER_PAD, D), F32).at[:MOE_E].set(w_re.T).at[MOE_E:MOE_E + MOE_GROUPS].set(w_rg.T)
    h3, route, counts = _moe_route(x, mod, layer, norm_g, w_router_t, tri)
    cnt = counts[:MOE_GROUPS, 0]
    ntile = (cnt + MOE_TILE - 1) // MOE_TILE
    tile_end = jnp.cumsum(ntile)
    seg_start = (tile_end - ntile) * MOE_TILE
    g_idx, rank = route[0], route[1]
    pos = jnp.sum(jnp.where(g_idx[None, :] == jnp.arange(MOE_GROUPS)[:, None], seg_start[:, None], 0), axis=0) + rank
    tiles = jnp.arange(MOE_MAX_TILES, dtype=jnp.int32)
    tile_group = jnp.minimum(jnp.sum(tiles[:, None] >= tile_end[None, :], axis=1), MOE_GROUPS - 1).astype(jnp.int32)
    n_active = tile_end[-1:].astype(jnp.int32)
    seg_end = tile_end * MOE_TILE
    mark_lo = jnp.concatenate([(seg_start + cnt) // MOE_DMA_CHUNK, seg_end[-1:] // MOE_DMA_CHUNK])
    mark_hi = jnp.concatenate([seg_end // MOE_DMA_CHUNK, jnp.full((1,), MOE_ROWS // MOE_DMA_CHUNK)])
    src = _moe_invert(pos.astype(jnp.int32), mark_lo.astype(jnp.int32), mark_hi.astype(jnp.int32))
    first_tile = (tile_end - ntile)[tile_group]
    real_rows = jnp.clip(cnt[tile_group] - (tiles - first_tile) * MOE_TILE, 0, MOE_TILE)
    n_chunk = ((real_rows + MOE_DMA_CHUNK - 1) // MOE_DMA_CHUNK).astype(jnp.int32)
    y3 = _moe_experts(h3, src, tile_group, n_active, n_chunk, layer, w1, w3, w2)
    if final_g is None:
        return y3
    return tuple(_moe_combine_norm(x, y3, mod, layer, final_g, off, nb * L) for L, nb, off in
                 ((CTX_L, CTX_B, 0), (LAT_L, LAT_B, N_CTX)))


def _combine_norm_kernel(y_ref, gate_ref, x_ref, g_ref, o_ref):
    x = x_ref[...] + gate_ref[...] * y_ref[...]
    o_ref[...] = x * lax.rsqrt(jnp.mean(x * x, axis=-1, keepdims=True) + EPS) * g_ref[...]


def _moe_combine_norm(x, y3, mod, layer, final_g, row_off, n_rows):
    off = row_off // TM
    first_lat = N_CTX // TM

    def row_fn(t):
        g = t + off
        return jnp.where(g < first_lat, 0, 1 + (g - first_lat) // (LAT_L // TM))

    return pl.pallas_call(
        _combine_norm_kernel,
        grid=(n_rows // TM,),
        in_specs=[
            pl.BlockSpec((TM, D), lambda t: (t + off, 0)),
            _mod_spec(layer, 5, row_fn),
            pl.BlockSpec((TM, D), lambda t: (t + off, 0)),
            _vec_spec(D),
        ],
        out_specs=pl.BlockSpec((TM, D), lambda t: (t, 0)),
        out_shape=jax.ShapeDtypeStruct((n_rows, D), F32),
        compiler_params=_cparams(1),
        name="combine_final_norm",
    )(y3, mod, x, final_g.reshape(1, D))


def kernel(x_prompt, x_sample, state_gla, c, c_ctx, w_ada, b_ada, norm_g, hy_w_in, hy_b_in, hy_conv_w, hy_conv_b, hy_f_w1, hy_f_b1, hy_f_freq, hy_f_w2, hy_f_b2, hy_f_w3, hy_skip, hy_w_out, hy_b_out, gla_w_q, gla_w_k, gla_w_v, gla_w_g, gla_w_gk1, gla_w_gk2, gla_b_gk, gla_norm_g, gla_w_o, fn_w_out, fn_b_out, pool_w, pool_b, pool_scale, moe_w_rg, moe_w_re, moe_w1, moe_w3, moe_w2, final_g):
    groups = ((CTX_L, CTX_B, 0, None), (LAT_L, LAT_B, N_CTX, LAT_L // GRID_W))

    x_ctx, x_lat = x_prompt.reshape(N_CTX, D), x_sample.reshape(N_LAT, D)
    x = None
    cond =jnp.zeros((MOD_ROWS, D), F32).at[0].set(c_ctx).at[1:1 + LAT_B].set(c)
    mod = _ada_table(cond, w_ada, b_ada).reshape(DEPTH * MOD_ROWS * 6, 1, D)

    tri_tm = jnp.asarray(np.triu(np.ones((TM_ROUTE, TM_ROUTE)), 1), BF16)

    new_states = []
    pend = None
    for i in range(DEPTH):
        kind, j = i % 4, i // 4
        if kind == 0:
            if i > 0:
                x = _moe_combine(x, pend[0], mod, pend[1])
                x_ctx, x_lat = x[:N_CTX], x[N_CTX:]
            u = _hyena_in(x_ctx, x_lat, mod, i, norm_g[i, 0], hy_w_in[j].astype(BF16), hy_b_in[j], hy_conv_w[j],
                          hy_conv_b[j])
            zs = []
            for L, nb, off, _ in groups:
                fwd, ff, inv = _dft_mats(L)
                khat = _hyena_filters(L, ff, hy_f_w1[j], hy_f_b1[j], hy_f_freq[j], hy_f_w2[j], hy_f_b2[j],
                                      hy_f_w3[j])
                zs.append(_hyena_conv(u, khat, hy_skip[j], fwd, inv, L, nb, off, D if L == CTX_L else 512))
            x = _outproj_joint(x_ctx, x_lat, zs[0], zs[1], hy_w_out[j].astype(BF16), hy_b_out[j], mod, i)
        elif kind == 1:
            w_cat = jnp.concatenate([gla_w_q[j], gla_w_k[j], gla_w_v[j], gla_w_g[j]], axis=1).astype(BF16)
            nk = GLA_H * GLA_DK
            wg1 = jnp.zeros((D, GK1_PAD), F32).at[:, :GLA_RANK].set(gla_w_gk1[j, 0])
            wg1 = wg1.at[:, GLA_RANK:2 * GLA_RANK].set(gla_w_gk1[j, 1]).astype(BF16)
            wg2 = jnp.zeros((GK1_PAD, 2 * nk), F32).at[:GLA_RANK, :nk].set(gla_w_gk2[j, 0])
            wg2 = wg2.at[GLA_RANK:2 * GLA_RANK, nk:].set(gla_w_gk2[j, 1]).astype(BF16)
            proj = _gla_proj(x, mod, i, norm_g[i, 0], w_cat, wg1, wg2, gla_b_gk[j].reshape(1, 2 * nk), pend)
            lower = np.tril(np.ones((GLA_CHUNK, GLA_CHUNK)))
            w_o = gla_w_o[j].astype(BF16)
            for L, nb, off, grid_rows in groups:
                eye = np.eye(L // GLA_CHUNK if L // GLA_CHUNK <= GLA_WHOLE_SEQ_CHUNKS else 1)
                tri = jnp.asarray(np.stack([np.kron(eye, lower), np.kron(eye, lower.T)]), BF16)
                s0 = None if grid_rows is None else state_gla[:, j]
                if L == CTX_L:
                    x, s_fin = _gla_core(proj, tri, gla_norm_g[j], s0, L, nb, off, GLA_H, (x, w_o, mod, i, pend))
                else:
                    o, s_fin = _gla_core(proj, tri, gla_norm_g[j], s0, L, nb, off, 1)
                    x = _outproj(x, o, w_o, jnp.zeros((D,), F32), mod, i, off, nb * L, pend)
                if grid_rows is None:
                    new_states.append(s_fin)
        elif kind == 2:
            w_out = fn_w_out[j].astype(BF16)
            for L, nb, off, _ in groups:
                chan, seq = _fnet_mats(L)
                x = _fnet(x, mod, i, norm_g[i, 0], chan, seq, w_out, fn_b_out[j], L, nb, off, pend)
        else:
            w_pool = pool_w[j].astype(BF16)
            for L, nb, off, grid_rows in groups:
                mats, inv_cnt = _pool_mats(L, grid_rows)
                x = _pool(x, mod, i, norm_g[i, 0], mats, inv_cnt, w_pool, pool_b[j], pool_scale[j], L, nb, off,
                          pend)

        out = _moe(x, mod, i, norm_g[i, 1], moe_w_rg[i], moe_w_re[i], moe_w1, moe_w3, moe_w2, tri_tm,
                   final_g if i == DEPTH - 1 else None)
        pend = (out, i)

    y_prompt, y_sample = out
    new_state_gla = jnp.stack(new_states, axis=1)
    return (y_prompt.reshape(CTX_B, CTX_L, D), y_sample.reshape(LAT_B, LAT_L, D), new_state_gla)
```

```python
import functools
import math

import jax
import jax.numpy as jnp
import numpy as np
from jax import lax
from jax.experimental import pallas as pl
from jax.experimental.pallas import tpu as pltpu

F32 = jnp.float32
BF16 = jnp.bfloat16

D = 1024
CTX_B, CTX_L = 32, 256
LAT_B, LAT_L = 2, 1024
N_CTX = CTX_B * CTX_L
N_LAT = LAT_B * LAT_L
N_TOK = N_CTX + N_LAT
DEPTH = 4
GRID_W = 64
EPS = 1e-6

HY_BANDS = 8
HY_EMB = 1 + 2 * HY_BANDS
HY_EMB_PAD = 32
HY_HID = 64
HY_FAST_DECAY = 0.3
HY_SLOW_DECAY = 1.5
HY_DECAY_TARGET = 1e-2

GLA_H = 4
GLA_DK = 128
GLA_DV = 256
GLA_RANK = 16
GLA_NORMALIZER = 16.0
GLA_CHUNK = 64
GLA_WHOLE_SEQ_CHUNKS = 4

FNET_GROUPS = 4
FNET_C = D // FNET_GROUPS
POOL_WINDOWS = (2, 4, 8, 16)
POOL_G = D // len(POOL_WINDOWS)

MOE_GROUPS = 4
MOE_PER_GROUP = 4
MOE_E = MOE_GROUPS * MOE_PER_GROUP
MOE_HID = D // 2

MOD_ROWS = 8
TM = 512
TM_BIG = 1024
VMEM_LIMIT = 56 * 1024 * 1024


def _cparams(n_axes):
    return pltpu.CompilerParams(dimension_semantics=("arbitrary",) * n_axes, vmem_limit_bytes=VMEM_LIMIT)


def _norm_mod(x, g, sc, sh):
    ms = jnp.mean(x * x, axis=-1, keepdims=True)
    return (x * lax.rsqrt(ms + EPS) * g) * (1.0 + sc) + sh


def _split(a):
    hi = a.astype(BF16)
    lo = (a - hi.astype(F32)).astype(BF16)
    return hi, lo


def _dot(a, b):
    return jnp.dot(a, b, preferred_element_type=F32)


def _dot_precise(a, b):
    a_hi, a_lo = _split(a)
    b_hi, b_lo = _split(b)
    return _dot(a_hi, b_hi) + (_dot(a_hi, b_lo) + _dot(a_lo, b_hi))


def _silu(x):
    return x * (1.0 / (1.0 + jnp.exp(-x)))


def _log_sigmoid(x):
    return jnp.minimum(x, 0.0) - jnp.log(1.0 + jnp.exp(-jnp.abs(x)))


def _mod_spec(layer, chunk, row_fn):
    base = layer * MOD_ROWS * 6 + chunk

    def index_map(*ids):
        return (base + row_fn(*ids) * 6, 0, 0)

    return pl.BlockSpec((None, 1, D), index_map)


def _row_tm(t, *_):
    return jnp.where(t < N_CTX // TM, 0, 1 + (t - N_CTX // TM) // (LAT_L // TM))


def _row_big(t, *_):
    return jnp.where(t < N_CTX // TM_BIG, 0, 1 + (t - N_CTX // TM_BIG) // (LAT_L // TM_BIG))


def _vec_spec(n):
    return pl.BlockSpec((1, n), lambda *ids: (0, 0))


def _full_spec(shape):
    nd = len(shape)
    return pl.BlockSpec(shape, lambda *ids: (0,) * nd)


def _ada_kernel(cond_ref, w_ref, b_ref, o_ref):
    s = _silu(cond_ref[...]).astype(BF16)
    o_ref[...] = _dot(s, w_ref[...].astype(BF16)) + b_ref[...]


def _ada_table(cond, w_ada, b_ada):
    tn = 1536
    return pl.pallas_call(
        _ada_kernel,
        grid=(DEPTH, 6 * D // tn),
        in_specs=[
            pl.BlockSpec((MOD_ROWS, D), lambda i, j: (0, 0)),
            pl.BlockSpec((None, D, tn), lambda i, j: (i, 0, j)),
            pl.BlockSpec((None, 1, tn), lambda i, j: (i, 0, j)),
        ],
        out_specs=pl.BlockSpec((None, MOD_ROWS, tn), lambda i, j: (i, 0, j)),
        out_shape=jax.ShapeDtypeStruct((DEPTH, MOD_ROWS, 6 * D), F32),
        compiler_params=_cparams(2),
        name="ada_table",
    )(cond, w_ada, b_ada.reshape(DEPTH, 1, 6 * D))


def _pending_specs(pend, mod, block_rows, row_index, row_fn):
    if pend is None:
        return [], []
    y, prev_layer = pend
    return [pl.BlockSpec((block_rows, D), row_index), _mod_spec(prev_layer, 5, row_fn)], [y, mod]


def _read_x(x_ref, pending_refs):
    x = x_ref[...]
    if pending_refs:
        y_ref, gate_ref = pending_refs
        x = x + gate_ref[...] * y_ref[...]
    return x


def _outproj_kernel(z_ref, w_ref, b_ref, gate_ref, x_ref, *rest):
    *pending, o_ref = rest
    y = _dot(z_ref[...], w_ref[...]) + b_ref[...]
    o_ref[...] = _read_x(x_ref, pending) + gate_ref[...] * y


def _outproj(x, z, w_bf16, bias, mod, layer, row_off, n_rows, pend=None):
    k = z.shape[1]
    off = row_off // TM
    first_lat = N_CTX // TM

    def row_fn(t):
        g = t + off
        return jnp.where(g < first_lat, 0, 1 + (g - first_lat) // (LAT_L // TM))

    p_specs, p_args = _pending_specs(pend, mod, TM, lambda t: (t + off, 0), row_fn)
    return pl.pallas_call(
        _outproj_kernel,
        grid=(n_rows // TM,),
        in_specs=[
            pl.BlockSpec((TM, k), lambda t: (t, 0)),
            _full_spec((k, D)),
            _vec_spec(D),
            _mod_spec(layer, 2, row_fn),
            pl.BlockSpec((TM, D), lambda t: (t + off, 0)),
        ] + p_specs,
        out_specs=pl.BlockSpec((TM, D), lambda t: (t + off, 0)),
        out_shape=jax.ShapeDtypeStruct((N_TOK, D), F32),
        input_output_aliases={4: 0},
        compiler_params=_cparams(1),
        name="outproj_residual",
    )(z, w_bf16, bias.reshape(1, D), mod, x, *p_args)


def _outproj_joint_kernel(zc_ref, zl_ref, w_ref, b_ref, gate_ref, xc_ref, xl_ref, o_ref):
    t = pl.program_id(0)
    for is_ctx, z_ref, x_ref in ((True, zc_ref, xc_ref), (False, zl_ref, xl_ref)):
        @pl.when((t < N_CTX // TM) == is_ctx)
        def _(z_ref=z_ref, x_ref=x_ref):
            y = _dot(z_ref[...], w_ref[...]) + b_ref[...]
            o_ref[...] = x_ref[...] + gate_ref[...] * y


def _outproj_joint(x_ctx, x_lat, z_ctx, z_lat, w_bf16, bias, mod, layer):
    k = z_ctx.shape[1]
    n_ctx_tiles = N_CTX // TM

    def ctx_block(t):
        return (jnp.minimum(t, n_ctx_tiles - 1), 0)

    def lat_block(t):
        return (jnp.maximum(t - n_ctx_tiles, 0), 0)

    return pl.pallas_call(
        _outproj_joint_kernel,
        grid=(N_TOK // TM,),
        in_specs=[
            pl.BlockSpec((TM, k), ctx_block),
            pl.BlockSpec((TM, k), lat_block),
            _full_spec((k, D)),
            _vec_spec(D),
            _mod_spec(layer, 2, _row_tm),
            pl.BlockSpec((TM, D), ctx_block),
            pl.BlockSpec((TM, D), lat_block),
        ],
        out_specs=pl.BlockSpec((TM, D), lambda t: (t, 0)),
        out_shape=jax.ShapeDtypeStruct((N_TOK, D), F32),
        compiler_params=_cparams(1),
        name="outproj_joint",
    )(z_ctx, z_lat, w_bf16, bias.reshape(1, D), mod, x_ctx, x_lat)


def _dft_mats(L):
    n2 = 2 * L
    k = np.arange(L)[:, None].astype(np.float64)
    n = np.arange(n2)[None, :].astype(np.float64)
    ang = 2.0 * np.pi * k * n / n2
    full = np.concatenate([np.cos(ang), -np.sin(ang)], axis=0)
    full[L, :] = np.cos(np.pi * np.arange(n2))
    fwd = full[:, :L]
    bwd = np.zeros((n2, L))
    bwd[:, 1:] = full[:, n2 - np.arange(1, L)]
    t = np.arange(L)[:, None].astype(np.float64)
    kk = np.arange(L)[None, :].astype(np.float64)
    ang_i = 2.0 * np.pi * t * kk / n2
    inv_re = np.cos(ang_i) / L
    inv_re[:, 0] = 1.0 / n2
    inv_im = -np.sin(ang_i) / L
    inv_im[:, 0] = np.cos(np.pi * np.arange(L)) / n2
    inv = np.concatenate([inv_re, inv_im], axis=1)
    return tuple(jnp.asarray(m, F32).astype(BF16) for m in (fwd, np.concatenate([fwd, bwd], axis=1), inv))


def _hyena_pos_emb(L):
    pos = np.arange(L, dtype=np.float64)
    bands = np.linspace(1e-4, HY_BANDS - 1, HY_BANDS)
    ang = (2.0 * np.pi * pos / L)[:, None] * bands[None, :]
    z = np.concatenate([(pos / L)[:, None], np.cos(ang), -np.sin(ang)], axis=-1)
    zp = np.zeros((L, HY_EMB_PAD))
    zp[:, :HY_EMB] = z
    return jnp.asarray(zp, F32)


def _hyena_filter_kernel(z_ref, w1_ref, b1_ref, fr_ref, w2_ref, b2_ref, w3f_ref, w3b_ref, ff_ref, o_ref, *, L, tn):
    j = pl.program_id(1)
    fr = fr_ref[...]
    f = jnp.sin(fr * (_dot_precise(z_ref[...], w1_ref[...]) + b1_ref[...]))
    f = jnp.sin(fr * (_dot_precise(f, w2_ref[...]) + b2_ref[...]))
    t_lin = lax.broadcasted_iota(jnp.int32, (L, tn), 0).astype(F32) / float(L - 1)
    ch = (lax.broadcasted_iota(jnp.int32, (L, tn), 1) + j * tn).astype(F32)
    max_decay = math.log(HY_DECAY_TARGET) / HY_FAST_DECAY
    min_decay = math.log(HY_DECAY_TARGET) / HY_SLOW_DECAY
    deltas = min_decay + ch * ((max_decay - min_decay) / float(D - 1))
    window = jnp.exp(-t_lin * jnp.abs(deltas))
    kf = _dot_precise(f, w3f_ref[...]) * window
    kb = _dot_precise(f, w3b_ref[...]) * window
    taps = jnp.concatenate([kf, kb], axis=0).astype(BF16)
    o_ref[...] = _dot(ff_ref[...], taps)


def _hyena_filters(L, ff, f_w1, f_b1, f_freq, f_w2, f_b2, f_w3):
    tn = 512
    nj = D // tn
    w1p = jnp.zeros((HY_EMB_PAD, HY_HID), F32).at[:HY_EMB].set(f_w1)
    kern = functools.partial(_hyena_filter_kernel, L=L, tn=tn)
    return pl.pallas_call(
        kern,
        grid=(2, nj),
        in_specs=[
            _full_spec((L, HY_EMB_PAD)),
            _full_spec((HY_EMB_PAD, HY_HID)),
            _vec_spec(HY_HID),
            _vec_spec(HY_HID),
            _full_spec((HY_HID, HY_HID)),
            _vec_spec(HY_HID),
            pl.BlockSpec((HY_HID, tn), lambda o, j: (0, o * nj + j)),
            pl.BlockSpec((HY_HID, tn), lambda o, j: (0, (2 + o) * nj + j)),
            _full_spec((2 * L, 2 * L)),
        ],
        out_specs=pl.BlockSpec((None, 2 * L, tn), lambda o, j: (o, 0, j)),
        out_shape=jax.ShapeDtypeStruct((2, 2 * L, D), F32),
        compiler_params=_cparams(2),
        name=f"hyena_filters_L{L}",
    )(_hyena_pos_emb(L), w1p, f_b1.reshape(1, -1), f_freq.reshape(1, -1), f_w2, f_b2.reshape(1, -1),
      f_w3, f_w3, ff)


def _hyena_in_kernel(xc_ref, xl_ref, g_ref, sc_ref, sh_ref, w_ref, b_ref, cw_ref, cb_ref, o_ref, h_scr):
    t = pl.program_id(0)
    first = pl.program_id(1) == 0

    for is_ctx, x_ref in ((True, xc_ref), (False, xl_ref)):
        @pl.when(first & ((t < N_CTX // TM_BIG) == is_ctx))
        def _(x_ref=x_ref):
            h_scr[...] = _norm_mod(x_ref[...], g_ref[...], sc_ref[...], sh_ref[...]).astype(BF16)

    u = _dot(h_scr[...], w_ref[...]) + b_ref[...]
    cw = cw_ref[...]
    o_ref[...] = pltpu.roll(u, 1, 0) * cw[0:1] + u * cw[1:2] + pltpu.roll(u, TM_BIG - 1, 0) * cw[2:3] + cb_ref[...]
    is_ctx = (t < N_CTX // TM_BIG).astype(F32)
    for start in range(0, TM_BIG, CTX_L):
        f = 1.0 if start % LAT_L == 0 else is_ctx
        before = (start - 1) % TM_BIG
        o_ref[start:start + 1, :] = o_ref[start:start + 1, :] - f * (u[before:before + 1] * cw[0:1])
        end = start + CTX_L - 1
        g = 1.0 if (end + 1) % LAT_L == 0 else is_ctx
        after = (end + 1) % TM_BIG
        o_ref[end:end + 1, :] = o_ref[end:end + 1, :] - g * (u[after:after + 1] * cw[2:3])


def _hyena_in(x_ctx, x_lat, mod, layer, norm_g, w_in_bf16, b_in, conv_w, conv_b):
    n_ctx_tiles = N_CTX // TM_BIG
    return pl.pallas_call(
        _hyena_in_kernel,
        grid=(N_TOK // TM_BIG, 3),
        in_specs=[
            pl.BlockSpec((TM_BIG, D), lambda t, p: (jnp.minimum(t, n_ctx_tiles - 1), 0)),
            pl.BlockSpec((TM_BIG, D), lambda t, p: (jnp.maximum(t - n_ctx_tiles, 0), 0)),
            _vec_spec(D),
            _mod_spec(layer, 1, _row_big),
            _mod_spec(layer, 0, _row_big),
            pl.BlockSpec((D, D), lambda t, p: (0, p)),
            pl.BlockSpec((1, D), lambda t, p: (0, p)),
            pl.BlockSpec((3, D), lambda t, p: (0, p)),
            pl.BlockSpec((1, D), lambda t, p: (0, p)),
        ],
        out_specs=pl.BlockSpec((TM_BIG, D), lambda t, p: (t, p)),
        out_shape=jax.ShapeDtypeStruct((N_TOK, 3 * D), F32),
        scratch_shapes=[pltpu.VMEM((TM_BIG, D), BF16)],
        compiler_params=_cparams(2),
        name="hyena_in",
    )(x_ctx, x_lat, norm_g.reshape(1, D), mod, mod, w_in_bf16, b_in.reshape(1, -1), conv_w, conv_b.reshape(1, -1))


def _hyena_conv_kernel(v_ref, x1_ref, x2_ref, kh_ref, skip_ref, fwd_ref, inv_ref, o_ref, *, L):
    fwd = fwd_ref[...]
    inv = inv_ref[...]
    row0 = lax.broadcasted_iota(jnp.int32, (L, v_ref.shape[1]), 0) == 0

    def long_conv(z, order):
        zh = _dot(fwd, z.astype(BF16))
        zr, zi = zh[:L], zh[L:]
        kr, ki = kh_ref[order, :L, :], kh_ref[order, L:, :]
        pr = jnp.where(row0, zr * kr, zr * kr - zi * ki)
        pi = jnp.where(row0, zi * ki, zr * ki + zi * kr)
        prod = jnp.concatenate([pr, pi], axis=0).astype(BF16)
        return _dot(inv, prod) + z * skip_ref[order:order + 1, :]

    z = x1_ref[...] * long_conv(v_ref[...], 0)
    z = x2_ref[...] * long_conv(z, 1)
    o_ref[...] = z.astype(BF16)


def _hyena_conv(u, khat, skip, fwd, inv, L, n_batch, row_off, tn):
    nj = D // tn
    rb = row_off // L
    kern = functools.partial(_hyena_conv_kernel, L=L)
    return pl.pallas_call(
        kern,
        grid=(nj, n_batch),
        in_specs=[
            pl.BlockSpec((L, tn), lambda j, b: (rb + b, j)),
            pl.BlockSpec((L, tn), lambda j, b: (rb + b, nj + j)),
            pl.BlockSpec((L, tn), lambda j, b: (rb + b, 2 * nj + j)),
            pl.BlockSpec((2, 2 * L, tn), lambda j, b: (0, 0, j)),
            pl.BlockSpec((2, tn), lambda j, b: (0, j)),
            _full_spec((2 * L, L)),
            _full_spec((L, 2 * L)),
        ],
        out_specs=pl.BlockSpec((L, tn), lambda j, b: (b, j)),
        out_shape=jax.ShapeDtypeStruct((n_batch * L, D), BF16),
        compiler_params=_cparams(2),
        name=f"hyena_conv_L{L}",
    )(u, u, u, khat, skip, fwd, inv)


GLA_PROJ = 2 * GLA_H * GLA_DK + 2 * GLA_H * GLA_DV
GLA_COLS = GLA_PROJ + 2 * GLA_H * GLA_DK
GK1_PAD = 128


def _gla_proj_kernel(x_ref, g_ref, sc_ref, sh_ref, w_ref, wg1_ref, wg2_ref, bg_ref, *rest):
    *pending, o_ref = rest
    h = _norm_mod(_read_x(x_ref, pending), g_ref[...], sc_ref[...], sh_ref[...]).astype(BF16)
    p = _dot(h, w_ref[...])
    nq = GLA_H * GLA_DK
    o_ref[:, 0:nq] = p[:, 0:nq] * (GLA_DK ** -0.5)
    o_ref[:, nq:nq + nq + GLA_H * GLA_DV] = p[:, nq:nq + nq + GLA_H * GLA_DV]
    o_ref[:, 2 * nq + GLA_H * GLA_DV:GLA_PROJ] = _silu(p[:, 2 * nq + GLA_H * GLA_DV:GLA_PROJ])
    low = _dot(h, wg1_ref[...]).astype(BF16)
    gk = _dot(low, wg2_ref[...]) + bg_ref[...]
    o_ref[:, GLA_PROJ:GLA_COLS] = _log_sigmoid(gk) / GLA_NORMALIZER


def _gla_proj(x, mod, layer, norm_g, w_cat, wg1, wg2, bg, pend=None):
    p_specs, p_args = _pending_specs(pend, mod, TM, lambda t: (t, 0), _row_tm)
    return pl.pallas_call(
        _gla_proj_kernel,
        grid=(N_TOK // TM,),
        in_specs=[
            pl.BlockSpec((TM, D), lambda t: (t, 0)),
            _vec_spec(D),
            _mod_spec(layer, 1, _row_tm),
            _mod_spec(layer, 0, _row_tm),
            _full_spec((D, GLA_PROJ)),
            _full_spec((D, GK1_PAD)),
            _full_spec((GK1_PAD, 2 * GLA_H * GLA_DK)),
            _vec_spec(2 * GLA_H * GLA_DK),
        ] + p_specs,
        out_specs=pl.BlockSpec((TM, GLA_COLS), lambda t: (t, 0)),
        out_shape=jax.ShapeDtypeStruct((N_TOK, GLA_COLS), F32),
        compiler_params=_cparams(1),
        name="gla_proj",
    )(x, norm_g.reshape(1, D), mod, mod, w_cat, wg1, wg2, bg, *p_args)


def _gla_core_kernel(*refs, L, has_s0, hps, n_proj):
    q_ref, k_ref, v_ref, g_ref, gkf_ref, gkb_ref, tri_ref, ng_ref = refs[:8]
    rest = list(refs[8:])
    s0_ref = rest.pop(0) if has_s0 else None
    proj_refs = [rest.pop(0) for _ in range(n_proj)]
    o_ref, sf_ref, acc = rest[:3]
    o_gated = rest[3] if n_proj else o_ref
    C = GLA_CHUNK
    n = L // C
    ri = lax.broadcasted_iota(jnp.int32, (C, C), 0)
    ci = lax.broadcasted_iota(jnp.int32, (C, C), 1)
    nt_dims = (((1,), (1,)), ((), ()))
    tn_dims = (((0,), (0,)), ((), ()))
    whole = n <= GLA_WHOLE_SEQ_CHUNKS
    if whole:
        rl = lax.broadcasted_iota(jnp.int32, (L, L), 0)
        cl = lax.broadcasted_iota(jnp.int32, (L, L), 1)
        same_chunk = (rl >> (C.bit_length() - 1)) == (cl >> (C.bit_length() - 1))

    for hh in range(hps):
        kc = slice(hh * GLA_DK, (hh + 1) * GLA_DK)
        vc = slice(hh * GLA_DV, (hh + 1) * GLA_DV)
        for direction, gk_ref in enumerate((gkf_ref, gkb_ref)):
            keep = (ci <= ri) if direction == 0 else (ci >= ri)
            last = C - 1 if direction == 0 else 0
            gk_hi, gk_lo = _split(gk_ref[:, kc])
            gk_parts = jnp.concatenate([gk_hi, gk_lo], axis=1)
            if whole:
                b_all = _dot(tri_ref[direction], gk_parts)
                b_all = b_all[:, :GLA_DK] + b_all[:, GLA_DK:]
                qe_all = (q_ref[:, kc] * jnp.exp(b_all)).astype(BF16)
                ke_all = (k_ref[:, kc] * jnp.exp(-b_all)).astype(BF16)
                keep_all = same_chunk & ((cl <= rl) if direction == 0 else (cl >= rl))
                s_all = lax.dot_general(qe_all, ke_all, nt_dims, preferred_element_type=F32)
                s_all = jnp.where(keep_all, s_all, 0.0).astype(BF16)
                o_intra = _dot(s_all, v_ref[:, vc].astype(BF16))
            st = s0_ref[direction, hh].T if has_s0 else jnp.zeros((GLA_DV, GLA_DK), F32)
            order = range(n) if direction == 0 else range(n - 1, -1, -1)
            for c in order:
                rows = slice(c * C, (c + 1) * C)
                if whole:
                    b = b_all[rows]
                else:
                    b = _dot(tri_ref[direction], gk_parts[rows])
                    b = b[:, :GLA_DK] + b[:, GLA_DK:]
                b_last = b[last:last + 1, :]
                k = k_ref[rows, kc]
                v = v_ref[rows, vc].astype(BF16)
                kd = (k * jnp.exp(b_last - b)).astype(BF16)
                if whole:
                    qe = qe_all[rows]
                    o = o_intra[rows]
                else:
                    qe = (q_ref[rows, kc] * jnp.exp(b)).astype(BF16)
                    ke = (k * jnp.exp(-b)).astype(BF16)
                    scores = lax.dot_general(qe, ke, nt_dims, preferred_element_type=F32)
                    o = _dot(jnp.where(keep, scores, 0.0).astype(BF16), v)
                o = o + lax.dot_general(qe, st.astype(BF16), nt_dims, preferred_element_type=F32)
                if direction == 0:
                    acc[rows, vc] = o
                else:
                    acc[rows, vc] = acc[rows, vc] + o
                st = jnp.exp(b_last) * st + lax.dot_general(v, kd, tn_dims, preferred_element_type=F32)
            sf_ref[direction, hh] = st.T

        o = acc[:, vc]
        o = o * lax.rsqrt(jnp.mean(o * o, axis=-1, keepdims=True) + EPS) * ng_ref[...]
        o_gated[:, vc] = (o * g_ref[:, vc]).astype(BF16)

    if n_proj:
        w_ref, gate_ref, x_ref, *pending = proj_refs
        o_ref[...] = _read_x(x_ref, pending) + gate_ref[...] * _dot(o_gated[...], w_ref[...])


def _gla_core(proj, tri, norm_g, s0, L, n_batch, row_off, hps, out_proj=None):
    rb = row_off // L
    H = GLA_H
    nh = H // hps
    has_s0 = s0 is not None
    kb, vb = GLA_DK * hps, GLA_DV * hps
    in_specs = [
        pl.BlockSpec((L, kb), lambda b, h: (rb + b, h)),
        pl.BlockSpec((L, kb), lambda b, h: (rb + b, nh + h)),
        pl.BlockSpec((L, vb), lambda b, h: (rb + b, (2 * H * GLA_DK) // vb + h)),
        pl.BlockSpec((L, vb), lambda b, h: (rb + b, (2 * H * GLA_DK) // vb + nh + h)),
        pl.BlockSpec((L, kb), lambda b, h: (rb + b, GLA_PROJ // kb + h)),
        pl.BlockSpec((L, kb), lambda b, h: (rb + b, GLA_PROJ // kb + nh + h)),
        _full_spec(tri.shape),
        _vec_spec(GLA_DV),
    ]
    args = [proj] * 6 + [tri, norm_g.reshape(1, GLA_DV)]
    state_spec = pl.BlockSpec((None, 2, hps, GLA_DK, GLA_DV), lambda b, h: (b, 0, h, 0, 0))
    if has_s0:
        in_specs.append(state_spec)
        args.append(s0)
    first_spec = pl.BlockSpec((L, vb), lambda b, h: (b, h))
    first_shape = jax.ShapeDtypeStruct((n_batch * L, H * GLA_DV), BF16)
    scratch = [pltpu.VMEM((L, vb), F32)]
    aliases, n_proj = {}, 0
    if out_proj is not None:
        assert nh == 1, "the fused output projection needs every head in the step"
        x, w_o, mod, layer, pend = out_proj

        def row_fn(b, h):
            return 1 + b if row_off > 0 else 0

        x_spec = pl.BlockSpec((L, D), lambda b, h: (rb + b, 0))
        p_specs, p_args = _pending_specs(pend, mod, L, lambda b, h: (rb + b, 0), row_fn)
        aliases = {len(args) + 2: 0}
        in_specs += [_full_spec((H * GLA_DV, D)), _mod_spec(layer, 2, row_fn), x_spec] + p_specs
        args += [w_o, mod, x] + p_args
        n_proj = 3 + len(p_args)
        first_spec, first_shape = x_spec, jax.ShapeDtypeStruct((N_TOK, D), F32)
        scratch.append(pltpu.VMEM((L, vb), BF16))
    kern = functools.partial(_gla_core_kernel, L=L, has_s0=has_s0, hps=hps, n_proj=n_proj)
    return pl.pallas_call(
        kern,
        grid=(n_batch, nh),
        in_specs=in_specs,
        out_specs=[first_spec, state_spec],
        out_shape=[first_shape, jax.ShapeDtypeStruct((n_batch, 2, H, GLA_DK, GLA_DV), F32)],
        scratch_shapes=scratch,
        input_output_aliases=aliases,
        compiler_params=_cparams(2),
        name=f"gla_core_L{L}",
    )(*args)


def _fnet_mats(L):
    c = np.arange(FNET_C)
    ang_c = 2.0 * np.pi * np.outer(c, c) / FNET_C
    chan = np.concatenate([np.cos(ang_c), np.sin(ang_c)], axis=1) / math.sqrt(FNET_C)
    t = np.arange(L)
    ang_l = 2.0 * np.pi * np.outer(t, t) / L
    seq = np.concatenate([np.cos(ang_l), -np.sin(ang_l)], axis=1) / math.sqrt(L)
    return jnp.asarray(chan, F32).astype(BF16), jnp.asarray(seq, F32).astype(BF16)


def _fnet_kernel(x_ref, g_ref, sc_ref, sh_ref, gate_ref, chan_ref, seq_ref, w_ref, b_ref, *rest):
    *pending, o_ref = rest
    x = _read_x(x_ref, pending)
    h = _norm_mod(x, g_ref[...], sc_ref[...], sh_ref[...]).astype(BF16)
    chan = chan_ref[...]
    cos_parts, sin_parts = [], []
    for gi in range(FNET_GROUPS):
        cs = _dot(h[:, gi * FNET_C:(gi + 1) * FNET_C], chan)
        cos_parts.append(cs[:, :FNET_C])
        sin_parts.append(cs[:, FNET_C:])
    stacked = jnp.concatenate([jnp.concatenate(cos_parts, axis=1), jnp.concatenate(sin_parts, axis=1)], axis=0)
    mixed = _dot(seq_ref[...], stacked.astype(BF16))
    y = _dot(mixed.astype(BF16), w_ref[...]) + b_ref[...]
    o_ref[...] = x + gate_ref[...] * y


def _fnet(x, mod, layer, norm_g, chan, seq, w_bf16, bias, L, n_batch, row_off, pend=None):
    rb = row_off // L
    lat = row_off > 0

    def row_fn(b):
        return 1 + b if lat else 0

    p_specs, p_args = _pending_specs(pend, mod, L, lambda b: (rb + b, 0), row_fn)
    return pl.pallas_call(
        _fnet_kernel,
        grid=(n_batch,),
        in_specs=[
            pl.BlockSpec((L, D), lambda b: (rb + b, 0)),
            _vec_spec(D),
            _mod_spec(layer, 1, row_fn),
            _mod_spec(layer, 0, row_fn),
            _mod_spec(layer, 2, row_fn),
            _full_spec((FNET_C, 2 * FNET_C)),
            _full_spec((L, 2 * L)),
            _full_spec((D, D)),
            _vec_spec(D),
        ] + p_specs,
        out_specs=pl.BlockSpec((L, D), lambda b: (rb + b, 0)),
        out_shape=jax.ShapeDtypeStruct((N_TOK, D), F32),
        input_output_aliases={0: 0},
        compiler_params=_cparams(1),
        name=f"fnet_L{L}",
    )(x, norm_g.reshape(1, D), mod, mod, mod, chan, seq, w_bf16, bias.reshape(1, D), *p_args)


def _window_bounds(n, k):
    t = np.arange(n)
    lo, hi = k // 2, k - k // 2 - 1
    return np.maximum(t - lo, 0), np.minimum(t + hi + 1, n)


def _pool_mats(L, grid_rows):
    mats, inv = [], []
    for k in POOL_WINDOWS:
        if grid_rows is None:
            s, e = _window_bounds(L, k)
            idx = np.arange(L)[None, :]
            m = ((idx >= s[:, None]) & (idx < e[:, None])).astype(np.float64)
            cnt = (e - s).astype(np.float64)
        else:
            sr, er = _window_bounds(grid_rows, k)
            sc, ec = _window_bounds(GRID_W, k)
            ir = np.arange(grid_rows)[None, :]
            ic = np.arange(GRID_W)[None, :]
            mr = ((ir >= sr[:, None]) & (ir < er[:, None])).astype(np.float64)
            mc = ((ic >= sc[:, None]) & (ic < ec[:, None])).astype(np.float64)
            m = np.kron(mr, mc)
            cnt = np.kron((er - sr).astype(np.float64), (ec - sc).astype(np.float64))
        mats.append(m)
        inv.append(1.0 / cnt)
    return jnp.asarray(np.stack(mats), BF16), jnp.asarray(np.stack(inv)[:, :, None], F32)


def _pool_kernel(x_ref, g_ref, sc_ref, sh_ref, gate_ref, m_ref, ic_ref, w_ref, b_ref, ps_ref, *rest):
    *pending, o_ref = rest
    x = _read_x(x_ref, pending)
    h = _norm_mod(x, g_ref[...], sc_ref[...], sh_ref[...])
    outs = []
    for gi in range(len(POOL_WINDOWS)):
        hg = h[:, gi * POOL_G:(gi + 1) * POOL_G]
        hi, lo = _split(hg)
        m = m_ref[gi]
        mean = (_dot(m, hi) + _dot(m, lo)) * ic_ref[gi]
        outs.append(_dot((mean - hg).astype(BF16), w_ref[gi]))
    y = (jnp.concatenate(outs, axis=1) + b_ref[...]) * ps_ref[...]
    o_ref[...] = x + gate_ref[...] * y


def _pool(x, mod, layer, norm_g, mats, inv_cnt, w_bf16, bias, scale, L, n_batch, row_off, pend=None):
    rb = row_off // L
    lat = row_off > 0
    G = len(POOL_WINDOWS)

    def row_fn(b):
        return 1 + b if lat else 0

    p_specs, p_args = _pending_specs(pend, mod, L, lambda b: (rb + b, 0), row_fn)
    return pl.pallas_call(
        _pool_kernel,
        grid=(n_batch,),
        in_specs=[
            pl.BlockSpec((L, D), lambda b: (rb + b, 0)),
            _vec_spec(D),
            _mod_spec(layer, 1, row_fn),
            _mod_spec(layer, 0, row_fn),
            _mod_spec(layer, 2, row_fn),
            _full_spec((G, L, L)),
            _full_spec((G, L, 1)),
            _full_spec((G, POOL_G, POOL_G)),
            _vec_spec(D),
            _vec_spec(D),
        ] + p_specs,
        out_specs=pl.BlockSpec((L, D), lambda b: (rb + b, 0)),
        out_shape=jax.ShapeDtypeStruct((N_TOK, D), F32),
        input_output_aliases={0: 0},
        compiler_params=_cparams(1),
        name=f"pool_L{L}",
    )(x, norm_g.reshape(1, D), mod, mod, mod, mats, inv_cnt, w_bf16, bias.reshape(1, D), scale.reshape(1, D),
      *p_args)


ROUTER_PAD = 128
LANES = 128
D_EXT = D + LANES
MOE_TILE = 1024
MOE_MAX_TILES = N_TOK // MOE_TILE + MOE_GROUPS
MOE_ROWS = MOE_MAX_TILES * MOE_TILE
MOE_TILE_SHIFT = MOE_TILE.bit_length() - 1
assert 1 << MOE_TILE_SHIFT == MOE_TILE
MOE_ROW_BLOCK = 256
MOE_DMA_CHUNK = 64
MOE_CHUNK_SHIFT = MOE_DMA_CHUNK.bit_length() - 1
assert 1 << MOE_CHUNK_SHIFT == MOE_DMA_CHUNK and MOE_MAX_TILES % 2 == 0
MOE_Y_ROWS = N_TOK + 2 * MOE_DMA_CHUNK


ROUTE_ROWS = 8
TM_ROUTE = TM_BIG


def _moe_route_kernel(x_ref, g_ref, sc_ref, sh_ref, wr_ref, tri_ref, h3_ref, route_ref, cnt_ref, carry):
    t = pl.program_id(0)
    refs = (x_ref, g_ref, sc_ref, sh_ref, wr_ref, tri_ref, h3_ref, route_ref, cnt_ref, carry)
    pl.when(t < N_TOK // TM_ROUTE)(functools.partial(_moe_route_tile, t, *refs))

    @pl.when(t == N_TOK // TM_ROUTE)
    def _():
        h3_ref[...] = jnp.zeros_like(h3_ref)


def _moe_route_tile(t, x_ref, g_ref, sc_ref, sh_ref, wr_ref, tri_ref, h3_ref, route_ref, cnt_ref, carry):
    @pl.when(t == 0)
    def _():
        carry[...] = jnp.zeros_like(carry)

    h = _norm_mod(x_ref[...], g_ref[...], sc_ref[...], sh_ref[...])
    w_hi, w_lo = _split(wr_ref[...])
    h_hi, h_lo = _split(h)
    nt = (((1,), (1,)), ((), ()))
    logits = (lax.dot_general(w_hi, h_hi, nt, preferred_element_type=F32)
              + (lax.dot_general(w_hi, h_lo, nt, preferred_element_type=F32)
                 + lax.dot_general(w_lo, h_hi, nt, preferred_element_type=F32)))
    neg = jnp.float32(-jnp.inf)
    r8 = lax.broadcasted_iota(jnp.int32, (ROUTE_ROWS, TM_ROUTE), 0)
    r16 = lax.broadcasted_iota(jnp.int32, (MOE_E, TM_ROUTE), 0)
    gl = jnp.where(r8 < MOE_GROUPS, logits[MOE_E:MOE_E + ROUTE_ROWS], neg)
    g_max = jnp.max(gl, axis=0, keepdims=True)
    g_idx = jnp.min(jnp.where(gl == g_max, r8, ROUTE_ROWS), axis=0, keepdims=True)
    p_grp = 1.0 / jnp.sum(jnp.exp(gl - g_max), axis=0, keepdims=True)
    in_grp = (r16 >> 2) == g_idx
    el = jnp.where(in_grp, logits[:MOE_E], neg)
    m1 = jnp.max(el, axis=0, keepdims=True)
    i1 = jnp.min(jnp.where(el == m1, r16, MOE_E), axis=0, keepdims=True)
    z = jnp.sum(jnp.exp(el - m1), axis=0, keepdims=True)
    el2 = jnp.where(r16 == i1, neg, el)
    m2 = jnp.max(el2, axis=0, keepdims=True)
    i2 = jnp.min(jnp.where(el2 == m2, r16, MOE_E), axis=0, keepdims=True)
    p1 = 1.0 / z
    p2 = jnp.exp(m2 - m1) / z
    tot = p1 + p2
    eid = r8 + MOE_PER_GROUP * g_idx
    in4 = r8 < MOE_PER_GROUP
    cw4 = (jnp.where(in4 & (eid == i1), p_grp * (p1 / tot), 0.0)
           + jnp.where(in4 & (eid == i2), p_grp * (p2 / tot), 0.0))
    member = jnp.where(r8 == g_idx, 1.0, 0.0)
    before = _dot(member.astype(BF16), tri_ref[...]) + carry[:, 0:1]
    rank = jnp.sum(jnp.where(r8 == g_idx, before, 0.0), axis=0, keepdims=True)
    carry[...] = carry[...] + jnp.sum(member, axis=1, keepdims=True)
    cnt_ref[...] = carry[...].astype(jnp.int32)
    route_ref[...] = jnp.where(r8 == 0, g_idx, jnp.where(r8 == 1, rank.astype(jnp.int32), 0))
    h3_ref[:, :D] = h
    cw_rows = jnp.concatenate([cw4, jnp.zeros((LANES - ROUTE_ROWS, TM_ROUTE), F32)], axis=0)
    h3_ref[:, D:] = cw_rows.T


def _moe_route(x, mod, layer, norm_g, w_router_t, tri):
    nt = N_TOK // TM_ROUTE

    def tok_tile(t):
        return jnp.minimum(t, nt - 1)

    return pl.pallas_call(
        _moe_route_kernel,
        grid=(nt + 1,),
        in_specs=[
            pl.BlockSpec((TM_ROUTE, D), lambda t: (tok_tile(t), 0)),
            _vec_spec(D),
            _mod_spec(layer, 4, lambda t: _row_big(tok_tile(t))),
            _mod_spec(layer, 3, lambda t: _row_big(tok_tile(t))),
            _full_spec((ROUTER_PAD, D)),
            _full_spec((TM_ROUTE, TM_ROUTE)),
        ],
        out_specs=[
            pl.BlockSpec((TM_ROUTE, D_EXT), lambda t: (t, 0)),
            pl.BlockSpec((ROUTE_ROWS, TM_ROUTE), lambda t: (0, tok_tile(t))),
            pl.BlockSpec((ROUTE_ROWS, LANES), lambda t: (0, 0)),
        ],
        out_shape=[
            jax.ShapeDtypeStruct((N_TOK + TM_ROUTE, D_EXT), F32),
            jax.ShapeDtypeStruct((ROUTE_ROWS, N_TOK), jnp.int32),
            jax.ShapeDtypeStruct((ROUTE_ROWS, LANES), jnp.int32),
        ],
        scratch_shapes=[pltpu.VMEM((ROUTE_ROWS, LANES), F32)],
        compiler_params=_cparams(1),
        name="moe_route",
    )(x, norm_g.reshape(1, D), mod, mod, w_router_t, tri)


def _moe_invert_kernel(pos_ref, lo_ref, hi_ref, src_ref):
    def mark(j, carry):
        parity = lax.shift_right_logical(j, jnp.int32(MOE_TILE_SHIFT - MOE_CHUNK_SHIFT)) & 1
        first = N_TOK + MOE_DMA_CHUNK * parity
        base = j * MOE_DMA_CHUNK
        for rr in range(MOE_DMA_CHUNK):
            src_ref[base + rr] = first + rr
        return carry

    for g in range(MOE_GROUPS + 1):
        lax.fori_loop(lo_ref[g], hi_ref[g], mark, 0)

    def place(n, carry):
        src_ref[pos_ref[n]] = n
        return carry

    lax.fori_loop(0, N_TOK, place, 0, unroll=16)


def _moe_invert(pos, mark_lo, mark_hi):
    smem = pl.BlockSpec(memory_space=pltpu.SMEM)
    return pl.pallas_call(
        _moe_invert_kernel,
        in_specs=[smem, smem, smem],
        out_specs=smem,
        out_shape=jax.ShapeDtypeStruct((MOE_ROWS,), jnp.int32),
        name="moe_invert",
    )(pos, mark_lo, mark_hi)


def _moe_expert_kernel(src_ref, grp_ref, nact_ref, nchunk_ref, h_hbm, w1_ref, w3_ref, w2_ref, y_hbm,
                       xb0, xb1, ab0, ab1, gsem, ssem):
    t = pl.program_id(0)
    k = pl.program_id(1)
    n_active = nact_ref[0]
    last = n_active - 1
    T = MOE_TILE
    CH = MOE_DMA_CHUNK
    xbufs, accs = (xb0, xb1), (ab0, ab1)

    def gather_row(tile, slot, base, rr):
        tok = src_ref[tile * T + base + rr]
        rows = xbufs[slot].at[pl.ds(base, CH), :]
        return pltpu.make_async_copy(h_hbm.at[pl.ds(tok, 1), :], rows.at[pl.ds(rr, 1), :], gsem.at[slot])

    def scatter_row(tile, slot, base, rr):
        dst = src_ref[tile * T + base + rr]
        rows = accs[slot].at[pl.ds(base, CH), :]
        return pltpu.make_async_copy(rows.at[pl.ds(rr, 1), :], y_hbm.at[pl.ds(dst, 1), :], ssem.at[slot])

    def start_rows(make, tile, slot):
        def chunk(c, carry):
            base = pl.multiple_of(c * CH, CH)
            for rr in range(CH):
                make(tile, slot, base, rr).start()
            return carry

        lax.fori_loop(0, nchunk_ref[tile], chunk, 0)

    def wait_rows(tile, slot, gather):
        def chunk(c, carry):
            if gather:
                pltpu.make_async_copy(h_hbm.at[pl.ds(0, CH), :], xbufs[slot].at[pl.ds(0, CH), :], gsem.at[slot]).wait()
            else:
                pltpu.make_async_copy(accs[slot].at[pl.ds(0, CH), :], y_hbm.at[pl.ds(0, CH), :], ssem.at[slot]).wait()
            return carry

        lax.fori_loop(0, nchunk_ref[tile], chunk, 0)

    def step(slot):
        other = 1 - slot
        xb, acc = xbufs[slot], accs[slot]

        @pl.when(k == 0)
        def _():
            if slot == 0:
                @pl.when(t == 0)
                def _():
                    xb0[...] = jnp.zeros_like(xb0)
                    xb1[...] = jnp.zeros_like(xb1)
                    ab0[...] = jnp.zeros_like(ab0)
                    ab1[...] = jnp.zeros_like(ab1)
                    dump = pltpu.make_async_copy(ab0.at[pl.ds(0, 2 * CH), :], y_hbm.at[pl.ds(N_TOK, 2 * CH), :],
                                                 ssem.at[0])
                    dump.start()
                    dump.wait()
                    start_rows(gather_row, 0, 0)

            wait_rows(t, slot, True)

            @pl.when(t < last)
            def _():
                start_rows(gather_row, t + 1, other)

            @pl.when(t >= 2)
            def _():
                wait_rows(t - 2, slot, False)

        def experts(m):
            x = xb[:m, :D].astype(BF16)
            a = _dot(x, w1_ref[...].astype(BF16))
            b = _dot(x, w3_ref[...].astype(BF16))
            lane = lax.broadcasted_iota(jnp.int32, (m, LANES), 1)
            cwk = jnp.sum(jnp.where(lane == k, xb[:m, D:], 0.0), axis=-1, keepdims=True)
            hid = (_silu(a) * b * cwk).astype(BF16)
            acc[:m] = jnp.where(k > 0, acc[:m], 0.0) + _dot(hid, w2_ref[...].astype(BF16))

        blocks = (nchunk_ref[t] * CH + MOE_ROW_BLOCK - 1) // MOE_ROW_BLOCK
        for nb in range(1, T // MOE_ROW_BLOCK + 1):
            pl.when(blocks == nb)(functools.partial(experts, nb * MOE_ROW_BLOCK))

        @pl.when(k == MOE_PER_GROUP - 1)
        def _():
            start_rows(scatter_row, t, slot)

            @pl.when(t == last)
            def _():
                wait_rows(t, slot, False)

                @pl.when(t >= 1)
                def _():
                    wait_rows(t - 1, other, False)

    for slot in (0, 1):
        pl.when((t < n_active) & (t % 2 == slot))(functools.partial(step, slot))


def _moe_experts(h_ext, src, tile_group, n_active, n_chunk, layer, w1, w3, w2):
    T = MOE_TILE

    def w_index(t, k, src_ref, grp_ref, nact_ref, nchunk_ref):
        last = nact_ref[0] - 1
        e = jnp.where(t <= last, grp_ref[t] * MOE_PER_GROUP + k, grp_ref[last] * MOE_PER_GROUP + MOE_PER_GROUP - 1)
        return (layer, e, 0, 0)

    grid_spec = pltpu.PrefetchScalarGridSpec(
        num_scalar_prefetch=4,
        grid=(MOE_MAX_TILES, MOE_PER_GROUP),
        in_specs=[
            pl.BlockSpec(memory_space=pl.ANY),
            pl.BlockSpec((None, None, D, MOE_HID), w_index),
            pl.BlockSpec((None, None, D, MOE_HID), w_index),
            pl.BlockSpec((None, None, MOE_HID, D), w_index),
        ],
        out_specs=pl.BlockSpec(memory_space=pl.ANY),
        scratch_shapes=[
            pltpu.VMEM((T, D_EXT), F32),
            pltpu.VMEM((T, D_EXT), F32),
            pltpu.VMEM((T, D), F32),
            pltpu.VMEM((T, D), F32),
            pltpu.SemaphoreType.DMA((2,)),
            pltpu.SemaphoreType.DMA((2,)),
        ],
    )
    return pl.pallas_call(
        _moe_expert_kernel,
        grid_spec=grid_spec,
        out_shape=jax.ShapeDtypeStruct((MOE_Y_ROWS, D), F32),
        compiler_params=_cparams(2),
        name="moe_experts",
    )(src, tile_group, n_active, n_chunk, h_ext, w1, w3, w2)


def _moe_combine_kernel(y_ref, gate_ref, x_ref, o_ref):
    o_ref[...] = x_ref[...] + gate_ref[...] * y_ref[...]


def _moe_combine(x, y3, mod, layer):
    return pl.pallas_call(
        _moe_combine_kernel,
        grid=(N_TOK // TM,),
        in_specs=[
            pl.BlockSpec((TM, D), lambda t: (t, 0)),
            _mod_spec(layer, 5, _row_tm),
            pl.BlockSpec((TM, D), lambda t: (t, 0)),
        ],
        out_specs=pl.BlockSpec((TM, D), lambda t: (t, 0)),
        out_shape=jax.ShapeDtypeStruct((N_TOK, D), F32),
        input_output_aliases={2: 0},
        compiler_params=_cparams(1),
        name="moe_combine",
    )(y3, mod, x)


def _moe(x, mod, layer, norm_g, w_rg, w_re, w1, w3, w2, tri, final_g=None):
    w_router_t = jnp.zeros((ROUTER_PAD, D), F32).at[:MOE_E].set(w_re.T).at[MOE_E:MOE_E + MOE_GROUPS].set(w_rg.T)
    h3, route, counts = _moe_route(x, mod, layer, norm_g, w_router_t, tri)
    cnt = counts[:MOE_GROUPS, 0]
    ntile = (cnt + MOE_TILE - 1) // MOE_TILE
    tile_end = jnp.cumsum(ntile)
    seg_start = (tile_end - ntile) * MOE_TILE
    g_idx, rank = route[0], route[1]
    pos = jnp.sum(jnp.where(g_idx[None, :] == jnp.arange(MOE_GROUPS)[:, None], seg_start[:, None], 0), axis=0) + rank
    tiles = jnp.arange(MOE_MAX_TILES, dtype=jnp.int32)
    tile_group = jnp.minimum(jnp.sum(tiles[:, None] >= tile_end[None, :], axis=1), MOE_GROUPS - 1).astype(jnp.int32)
    n_active = tile_end[-1:].astype(jnp.int32)
    seg_end = tile_end * MOE_TILE
    mark_lo = jnp.concatenate([(seg_start + cnt) // MOE_DMA_CHUNK, seg_end[-1:] // MOE_DMA_CHUNK])
    mark_hi = jnp.concatenate([seg_end // MOE_DMA_CHUNK, jnp.full((1,), MOE_ROWS // MOE_DMA_CHUNK)])
    src = _moe_invert(pos.astype(jnp.int32), mark_lo.astype(jnp.int32), mark_hi.astype(jnp.int32))
    first_tile = (tile_end - ntile)[tile_group]
    real_rows = jnp.clip(cnt[tile_group] - (tiles - first_tile) * MOE_TILE, 0, MOE_TILE)
    n_chunk = ((real_rows + MOE_DMA_CHUNK - 1) // MOE_DMA_CHUNK).astype(jnp.int32)
    y3 = _moe_experts(h3, src, tile_group, n_active, n_chunk, layer, w1, w3, w2)
    if final_g is None:
        return y3
    return tuple(_moe_combine_norm(x, y3, mod, layer, final_g, off, nb * L) for L, nb, off in
                 ((CTX_L, CTX_B, 0), (LAT_L, LAT_B, N_CTX)))


def _combine_norm_kernel(y_ref, gate_ref, x_ref, g_ref, o_ref):
    x = x_ref[...] + gate_ref[...] * y_ref[...]
    o_ref[...] = x * lax.rsqrt(jnp.mean(x * x, axis=-1, keepdims=True) + EPS) * g_ref[...]


def _moe_combine_norm(x, y3, mod, layer, final_g, row_off, n_rows):
    off = row_off // TM
    first_lat = N_CTX // TM

    def row_fn(t):
        g = t + off
        return jnp.where(g < first_lat, 0, 1 + (g - first_lat) // (LAT_L // TM))

    return pl.pallas_call(
        _combine_norm_kernel,
        grid=(n_rows // TM,),
        in_specs=[
            pl.BlockSpec((TM, D), lambda t: (t + off, 0)),
            _mod_spec(layer, 5, row_fn),
            pl.BlockSpec((TM, D), lambda t: (t + off, 0)),
            _vec_spec(D),
        ],
        out_specs=pl.BlockSpec((TM, D), lambda t: (t, 0)),
        out_shape=jax.ShapeDtypeStruct((n_rows, D), F32),
        compiler_params=_cparams(1),
        name="combine_final_norm",
    )(y3, mod, x, final_g.reshape(1, D))


def kernel(x_prompt, x_sample, state_gla, c, c_ctx, w_ada, b_ada, norm_g, hy_w_in, hy_b_in, hy_conv_w, hy_conv_b, hy_f_w1, hy_f_b1, hy_f_freq, hy_f_w2, hy_f_b2, hy_f_w3, hy_skip, hy_w_out, hy_b_out, gla_w_q, gla_w_k, gla_w_v, gla_w_g, gla_w_gk1, gla_w_gk2, gla_b_gk, gla_norm_g, gla_w_o, fn_w_out, fn_b_out, pool_w, pool_b, pool_scale, moe_w_rg, moe_w_re, moe_w1, moe_w3, moe_w2, final_g):
    groups = ((CTX_L, CTX_B, 0, None), (LAT_L, LAT_B, N_CTX, LAT_L // GRID_W))

    x_ctx, x_lat = x_prompt.reshape(N_CTX, D), x_sample.reshape(N_LAT, D)
    x = None
    cond =jnp.zeros((MOD_ROWS, D), F32).at[0].set(c_ctx).at[1:1 + LAT_B].set(c)
    mod = _ada_table(cond, w_ada, b_ada).reshape(DEPTH * MOD_ROWS * 6, 1, D)

    tri_tm = jnp.asarray(np.triu(np.ones((TM_ROUTE, TM_ROUTE)), 1), BF16)

    new_states = []
    pend = None
    for i in range(DEPTH):
        kind, j = i % 4, i // 4
        if kind == 0:
            if i > 0:
                x = _moe_combine(x, pend[0], mod, pend[1])
                x_ctx, x_lat = x[:N_CTX], x[N_CTX:]
            u = _hyena_in(x_ctx, x_lat, mod, i, norm_g[i, 0], hy_w_in[j].astype(BF16), hy_b_in[j], hy_conv_w[j],
                          hy_conv_b[j])
            zs = []
            for L, nb, off, _ in groups:
                fwd, ff, inv = _dft_mats(L)
                khat = _hyena_filters(L, ff, hy_f_w1[j], hy_f_b1[j], hy_f_freq[j], hy_f_w2[j], hy_f_b2[j],
                                      hy_f_w3[j])
                zs.append(_hyena_conv(u, khat, hy_skip[j], fwd, inv, L, nb, off, D if L == CTX_L else 512))
            x = _outproj_joint(x_ctx, x_lat, zs[0], zs[1], hy_w_out[j].astype(BF16), hy_b_out[j], mod, i)
        elif kind == 1:
            w_cat = jnp.concatenate([gla_w_q[j], gla_w_k[j], gla_w_v[j], gla_w_g[j]], axis=1).astype(BF16)
            nk = GLA_H * GLA_DK
            wg1 = jnp.zeros((D, GK1_PAD), F32).at[:, :GLA_RANK].set(gla_w_gk1[j, 0])
            wg1 = wg1.at[:, GLA_RANK:2 * GLA_RANK].set(gla_w_gk1[j, 1]).astype(BF16)
            wg2 = jnp.zeros((GK1_PAD, 2 * nk), F32).at[:GLA_RANK, :nk].set(gla_w_gk2[j, 0])
            wg2 = wg2.at[GLA_RANK:2 * GLA_RANK, nk:].set(gla_w_gk2[j, 1]).astype(BF16)
            proj = _gla_proj(x, mod, i, norm_g[i, 0], w_cat, wg1, wg2, gla_b_gk[j].reshape(1, 2 * nk), pend)
            lower = np.tril(np.ones((GLA_CHUNK, GLA_CHUNK)))
            w_o = gla_w_o[j].astype(BF16)
            for L, nb, off, grid_rows in groups:
                eye = np.eye(L // GLA_CHUNK if L // GLA_CHUNK <= GLA_WHOLE_SEQ_CHUNKS else 1)
                tri = jnp.asarray(np.stack([np.kron(eye, lower), np.kron(eye, lower.T)]), BF16)
                s0 = None if grid_rows is None else state_gla[:, j]
                if L == CTX_L:
                    x, s_fin = _gla_core(proj, tri, gla_norm_g[j], s0, L, nb, off, GLA_H, (x, w_o, mod, i, pend))
                else:
                    o, s_fin = _gla_core(proj, tri, gla_norm_g[j], s0, L, nb, off, 1)
                    x = _outproj(x, o, w_o, jnp.zeros((D,), F32), mod, i, off, nb * L, pend)
                if grid_rows is None:
                    new_states.append(s_fin)
        elif kind == 2:
            w_out = fn_w_out[j].astype(BF16)
            for L, nb, off, _ in groups:
                chan, seq = _fnet_mats(L)
                x = _fnet(x, mod, i, norm_g[i, 0], chan, seq, w_out, fn_b_out[j], L, nb, off, pend)
        else:
            w_pool = pool_w[j].astype(BF16)
            for L, nb, off, grid_rows in groups:
                mats, inv_cnt = _pool_mats(L, grid_rows)
                x = _pool(x, mod, i, norm_g[i, 0], mats, inv_cnt, w_pool, pool_b[j], pool_scale[j], L, nb, off,
                          pend)

        out = _moe(x, mod, i, norm_g[i, 1], moe_w_rg[i], moe_w_re[i], moe_w1, moe_w3, moe_w2, tri_tm,
                   final_g if i == DEPTH - 1 else None)
        pend = (out, i)

    y_prompt, y_sample = out
    new_state_gla = jnp.stack(new_states, axis=1)
    return (y_prompt.reshape(CTX_B, CTX_L, D), y_sample.reshape(LAT_B, LAT_L, D), new_state_gla)
```

```python
import functools
import math

import jax
import jax.numpy as jnp
import numpy as np
from jax import lax
from jax.experimental import pallas as pl
from jax.experimental.pallas import tpu as pltpu

F32 = jnp.float32
BF16 = jnp.bfloat16

D = 1024
CTX_B, CTX_L = 32, 256
LAT_B, LAT_L = 2, 1024
N_CTX = CTX_B * CTX_L
N_LAT = LAT_B * LAT_L
N_TOK = N_CTX + N_LAT
DEPTH = 4
GRID_W = 64
EPS = 1e-6

HY_BANDS = 8
HY_EMB = 1 + 2 * HY_BANDS
HY_EMB_PAD = 32
HY_HID = 64
HY_FAST_DECAY = 0.3
HY_SLOW_DECAY = 1.5
HY_DECAY_TARGET = 1e-2

GLA_H = 4
GLA_DK = 128
GLA_DV = 256
GLA_RANK = 16
GLA_NORMALIZER = 16.0
GLA_CHUNK = 64
GLA_WHOLE_SEQ_CHUNKS = 4

FNET_GROUPS = 4
FNET_C = D // FNET_GROUPS
POOL_WINDOWS = (2, 4, 8, 16)
POOL_G = D // len(POOL_WINDOWS)

MOE_GROUPS = 4
MOE_PER_GROUP = 4
MOE_E = MOE_GROUPS * MOE_PER_GROUP
MOE_HID = D // 2

MOD_ROWS = 8
TM = 512
TM_BIG = 1024
V7X_VMEM_BYTES = 64 * 1024 * 1024
VMEM_LIMIT = V7X_VMEM_BYTES - 8 * 1024 * 1024


def _cparams(n_axes):
    return pltpu.CompilerParams(dimension_semantics=("arbitrary",) * n_axes, vmem_limit_bytes=VMEM_LIMIT)


def _norm_mod(x, g, sc, sh):
    ms = jnp.mean(x * x, axis=-1, keepdims=True)
    return (x * lax.rsqrt(ms + EPS) * g) * (1.0 + sc) + sh


def _split(a):
    hi = a.astype(BF16)
    lo = (a - hi.astype(F32)).astype(BF16)
    return hi, lo


def _dot(a, b):
    return jnp.dot(a, b, preferred_element_type=F32)


def _dot_precise(a, b):
    a_hi, a_lo = _split(a)
    b_hi, b_lo = _split(b)
    return _dot(a_hi, b_hi) + (_dot(a_hi, b_lo) + _dot(a_lo, b_hi))


def _silu(x):
    return x * (1.0 / (1.0 + jnp.exp(-x)))


def _log_sigmoid(x):
    return jnp.minimum(x, 0.0) - jnp.log(1.0 + jnp.exp(-jnp.abs(x)))


def _mod_spec(layer, chunk, row_fn):
    base = layer * MOD_ROWS * 6 + chunk

    def index_map(*ids):
        return (base + row_fn(*ids) * 6, 0, 0)

    return pl.BlockSpec((None, 1, D), index_map)


def _row_tm(t, *_):
    return jnp.where(t < N_CTX // TM, 0, 1 + (t - N_CTX // TM) // (LAT_L // TM))


def _row_big(t, *_):
    return jnp.where(t < N_CTX // TM_BIG, 0, 1 + (t - N_CTX // TM_BIG) // (LAT_L // TM_BIG))


def _vec_spec(n):
    return pl.BlockSpec((1, n), lambda *ids: (0, 0))


def _full_spec(shape):
    nd = len(shape)
    return pl.BlockSpec(shape, lambda *ids: (0,) * nd)


def _ada_kernel(cond_ref, w_ref, b_ref, o_ref):
    s = _silu(cond_ref[...]).astype(BF16)
    o_ref[...] = _dot(s, w_ref[...].astype(BF16)) + b_ref[...]


def _ada_table(cond, w_ada, b_ada):
    tn = 1536
    return pl.pallas_call(
        _ada_kernel,
        grid=(DEPTH, 6 * D // tn),
        in_specs=[
            pl.BlockSpec((MOD_ROWS, D), lambda i, j: (0, 0)),
            pl.BlockSpec((None, D, tn), lambda i, j: (i, 0, j)),
            pl.BlockSpec((None, 1, tn), lambda i, j: (i, 0, j)),
        ],
        out_specs=pl.BlockSpec((None, MOD_ROWS, tn), lambda i, j: (i, 0, j)),
        out_shape=jax.ShapeDtypeStruct((DEPTH, MOD_ROWS, 6 * D), F32),
        compiler_params=_cparams(2),
        name="ada_table",
    )(cond, w_ada, b_ada.reshape(DEPTH, 1, 6 * D))


def _pending_specs(pend, mod, block_rows, row_index, row_fn):
    if pend is None:
        return [], []
    y, prev_layer = pend
    return [pl.BlockSpec((block_rows, D), row_index), _mod_spec(prev_layer, 5, row_fn)], [y, mod]


def _read_x(x_ref, pending_refs):
    x = x_ref[...]
    if pending_refs:
        y_ref, gate_ref = pending_refs
        x = x + gate_ref[...] * y_ref[...]
    return x


def _outproj_kernel(z_ref, w_ref, b_ref, gate_ref, x_ref, *rest):
    *pending, o_ref = rest
    y = _dot(z_ref[...], w_ref[...]) + b_ref[...]
    o_ref[...] = _read_x(x_ref, pending) + gate_ref[...] * y


def _outproj(x, z, w_bf16, bias, mod, layer, row_off, n_rows, pend=None):
    k = z.shape[1]
    off = row_off // TM
    first_lat = N_CTX // TM

    def row_fn(t):
        g = t + off
        return jnp.where(g < first_lat, 0, 1 + (g - first_lat) // (LAT_L // TM))

    p_specs, p_args = _pending_specs(pend, mod, TM, lambda t: (t + off, 0), row_fn)
    return pl.pallas_call(
        _outproj_kernel,
        grid=(n_rows // TM,),
        in_specs=[
            pl.BlockSpec((TM, k), lambda t: (t, 0)),
            _full_spec((k, D)),
            _vec_spec(D),
            _mod_spec(layer, 2, row_fn),
            pl.BlockSpec((TM, D), lambda t: (t + off, 0)),
        ] + p_specs,
        out_specs=pl.BlockSpec((TM, D), lambda t: (t + off, 0)),
        out_shape=jax.ShapeDtypeStruct((N_TOK, D), F32),
        input_output_aliases={4: 0},
        compiler_params=_cparams(1),
        name="outproj_residual",
    )(z, w_bf16, bias.reshape(1, D), mod, x, *p_args)


def _outproj_joint_kernel(zc_ref, zl_ref, w_ref, b_ref, gate_ref, xc_ref, xl_ref, o_ref):
    t = pl.program_id(0)
    for is_ctx, z_ref, x_ref in ((True, zc_ref, xc_ref), (False, zl_ref, xl_ref)):
        @pl.when((t < N_CTX // TM_BIG) == is_ctx)
        def _(z_ref=z_ref, x_ref=x_ref):
            y = _dot(z_ref[...], w_ref[...]) + b_ref[...]
            o_ref[...] = x_ref[...] + gate_ref[...] * y


def _outproj_joint(x_ctx, x_lat, z_ctx, z_lat, w_bf16, bias, mod, layer):
    k = z_ctx.shape[1]
    n_ctx_tiles = N_CTX // TM_BIG

    def ctx_block(t):
        return (jnp.minimum(t, n_ctx_tiles - 1), 0)

    def lat_block(t):
        return (jnp.maximum(t - n_ctx_tiles, 0), 0)

    return pl.pallas_call(
        _outproj_joint_kernel,
        grid=(N_TOK // TM_BIG,),
        in_specs=[
            pl.BlockSpec((TM_BIG, k), ctx_block),
            pl.BlockSpec((TM_BIG, k), lat_block),
            _full_spec((k, D)),
            _vec_spec(D),
            _mod_spec(layer, 2, _row_big),
            pl.BlockSpec((TM_BIG, D), ctx_block),
            pl.BlockSpec((TM_BIG, D), lat_block),
        ],
        out_specs=pl.BlockSpec((TM_BIG, D), lambda t: (t, 0)),
        out_shape=jax.ShapeDtypeStruct((N_TOK, D), F32),
        compiler_params=_cparams(1),
        name="outproj_joint",
    )(z_ctx, z_lat, w_bf16, bias.reshape(1, D), mod, x_ctx, x_lat)


def _dft_mats(L):
    n2 = 2 * L
    k = np.arange(L)[:, None].astype(np.float64)
    n = np.arange(n2)[None, :].astype(np.float64)
    ang = 2.0 * np.pi * k * n / n2
    full = np.concatenate([np.cos(ang), -np.sin(ang)], axis=0)
    full[L, :] = np.cos(np.pi * np.arange(n2))
    fwd = full[:, :L]
    bwd = np.zeros((n2, L))
    bwd[:, 1:] = full[:, n2 - np.arange(1, L)]
    t = np.arange(L)[:, None].astype(np.float64)
    kk = np.arange(L)[None, :].astype(np.float64)
    ang_i = 2.0 * np.pi * t * kk / n2
    inv_re = np.cos(ang_i) / L
    inv_re[:, 0] = 1.0 / n2
    inv_im = -np.sin(ang_i) / L
    inv_im[:, 0] = np.cos(np.pi * np.arange(L)) / n2
    inv = np.concatenate([inv_re, inv_im], axis=1)
    return tuple(jnp.asarray(m, F32).astype(BF16) for m in (fwd, np.concatenate([fwd, bwd], axis=1), inv))


def _hyena_pos_emb(L):
    pos = np.arange(L, dtype=np.float64)
    bands = np.linspace(1e-4, HY_BANDS - 1, HY_BANDS)
    ang = (2.0 * np.pi * pos / L)[:, None] * bands[None, :]
    z = np.concatenate([(pos / L)[:, None], np.cos(ang), -np.sin(ang)], axis=-1)
    zp = np.zeros((L, HY_EMB_PAD))
    zp[:, :HY_EMB] = z
    return jnp.asarray(zp, F32)


def _hyena_filter_kernel(z_ref, w1_ref, b1_ref, fr_ref, w2_ref, b2_ref, w3f_ref, w3b_ref, ff_ref, o_ref, *, L, tn):
    j = pl.program_id(1)
    fr = fr_ref[...]
    f = jnp.sin(fr * (_dot_precise(z_ref[...], w1_ref[...]) + b1_ref[...]))
    f = jnp.sin(fr * (_dot_precise(f, w2_ref[...]) + b2_ref[...]))
    t_lin = lax.broadcasted_iota(jnp.int32, (L, tn), 0).astype(F32) / float(L - 1)
    ch = (lax.broadcasted_iota(jnp.int32, (L, tn), 1) + j * tn).astype(F32)
    max_decay = math.log(HY_DECAY_TARGET) / HY_FAST_DECAY
    min_decay = math.log(HY_DECAY_TARGET) / HY_SLOW_DECAY
    deltas = min_decay + ch * ((max_decay - min_decay) / float(D - 1))
    window = jnp.exp(-t_lin * jnp.abs(deltas))
    kf = _dot_precise(f, w3f_ref[...]) * window
    kb = _dot_precise(f, w3b_ref[...]) * window
    taps = jnp.concatenate([kf, kb], axis=0).astype(BF16)
    o_ref[...] = _dot(ff_ref[...], taps)


def _hyena_filters(L, ff, f_w1, f_b1, f_freq, f_w2, f_b2, f_w3):
    tn = 512
    nj = D // tn
    w1p = jnp.zeros((HY_EMB_PAD, HY_HID), F32).at[:HY_EMB].set(f_w1)
    kern = functools.partial(_hyena_filter_kernel, L=L, tn=tn)
    return pl.pallas_call(
        kern,
        grid=(2, nj),
        in_specs=[
            _full_spec((L, HY_EMB_PAD)),
            _full_spec((HY_EMB_PAD, HY_HID)),
            _vec_spec(HY_HID),
            _vec_spec(HY_HID),
            _full_spec((HY_HID, HY_HID)),
            _vec_spec(HY_HID),
            pl.BlockSpec((HY_HID, tn), lambda o, j: (0, o * nj + j)),
            pl.BlockSpec((HY_HID, tn), lambda o, j: (0, (2 + o) * nj + j)),
            _full_spec((2 * L, 2 * L)),
        ],
        out_specs=pl.BlockSpec((None, 2 * L, tn), lambda o, j: (o, 0, j)),
        out_shape=jax.ShapeDtypeStruct((2, 2 * L, D), F32),
        compiler_params=_cparams(2),
        name=f"hyena_filters_L{L}",
    )(_hyena_pos_emb(L), w1p, f_b1.reshape(1, -1), f_freq.reshape(1, -1), f_w2, f_b2.reshape(1, -1),
      f_w3, f_w3, ff)


def _hyena_in_kernel(xc_ref, xl_ref, g_ref, sc_ref, sh_ref, w_ref, b_ref, cw_ref, cb_ref, o_ref, h_scr):
    t = pl.program_id(0)
    first = pl.program_id(1) == 0

    for is_ctx, x_ref in ((True, xc_ref), (False, xl_ref)):
        @pl.when(first & ((t < N_CTX // TM_BIG) == is_ctx))
        def _(x_ref=x_ref):
            h_scr[...] = _norm_mod(x_ref[...], g_ref[...], sc_ref[...], sh_ref[...]).astype(BF16)

    u = _dot(h_scr[...], w_ref[...]) + b_ref[...]
    cw = cw_ref[...]
    o_ref[...] = pltpu.roll(u, 1, 0) * cw[0:1] + u * cw[1:2] + pltpu.roll(u, TM_BIG - 1, 0) * cw[2:3] + cb_ref[...]
    is_ctx = (t < N_CTX // TM_BIG).astype(F32)
    for start in range(0, TM_BIG, CTX_L):
        f = 1.0 if start % LAT_L == 0 else is_ctx
        before = (start - 1) % TM_BIG
        o_ref[start:start + 1, :] = o_ref[start:start + 1, :] - f * (u[before:before + 1] * cw[0:1])
        end = start + CTX_L - 1
        g = 1.0 if (end + 1) % LAT_L == 0 else is_ctx
        after = (end + 1) % TM_BIG
        o_ref[end:end + 1, :] = o_ref[end:end + 1, :] - g * (u[after:after + 1] * cw[2:3])


def _hyena_in(x_ctx, x_lat, mod, layer, norm_g, w_in_bf16, b_in, conv_w, conv_b):
    n_ctx_tiles = N_CTX // TM_BIG
    return pl.pallas_call(
        _hyena_in_kernel,
        grid=(N_TOK // TM_BIG, 3),
        in_specs=[
            pl.BlockSpec((TM_BIG, D), lambda t, p: (jnp.minimum(t, n_ctx_tiles - 1), 0)),
            pl.BlockSpec((TM_BIG, D), lambda t, p: (jnp.maximum(t - n_ctx_tiles, 0), 0)),
            _vec_spec(D),
            _mod_spec(layer, 1, _row_big),
            _mod_spec(layer, 0, _row_big),
            pl.BlockSpec((D, D), lambda t, p: (0, p)),
            pl.BlockSpec((1, D), lambda t, p: (0, p)),
            pl.BlockSpec((3, D), lambda t, p: (0, p)),
            pl.BlockSpec((1, D), lambda t, p: (0, p)),
        ],
        out_specs=pl.BlockSpec((TM_BIG, D), lambda t, p: (t, p)),
        out_shape=jax.ShapeDtypeStruct((N_TOK, 3 * D), F32),
        scratch_shapes=[pltpu.VMEM((TM_BIG, D), BF16)],
        compiler_params=_cparams(2),
        name="hyena_in",
    )(x_ctx, x_lat, norm_g.reshape(1, D), mod, mod, w_in_bf16, b_in.reshape(1, -1), conv_w, conv_b.reshape(1, -1))


def _hyena_conv_kernel(v_ref, x1_ref, x2_ref, kh_ref, skip_ref, fwd_ref, inv_ref, o_ref, *, L):
    fwd = fwd_ref[...]
    inv = inv_ref[...]
    row0 = lax.broadcasted_iota(jnp.int32, (L, v_ref.shape[1]), 0) == 0

    def long_conv(z, order):
        zh = _dot(fwd, z.astype(BF16))
        zr, zi = zh[:L], zh[L:]
        kr, ki = kh_ref[order, :L, :], kh_ref[order, L:, :]
        pr = jnp.where(row0, zr * kr, zr * kr - zi * ki)
        pi = jnp.where(row0, zi * ki, zr * ki + zi * kr)
        prod = jnp.concatenate([pr, pi], axis=0).astype(BF16)
        return _dot(inv, prod) + z * skip_ref[order:order + 1, :]

    z = x1_ref[...] * long_conv(v_ref[...], 0)
    z = x2_ref[...] * long_conv(z, 1)
    o_ref[...] = z.astype(BF16)


def _hyena_conv(u, khat, skip, fwd, inv, L, n_batch, row_off, tn):
    nj = D // tn
    rb = row_off // L
    kern = functools.partial(_hyena_conv_kernel, L=L)
    return pl.pallas_call(
        kern,
        grid=(nj, n_batch),
        in_specs=[
            pl.BlockSpec((L, tn), lambda j, b: (rb + b, j)),
            pl.BlockSpec((L, tn), lambda j, b: (rb + b, nj + j)),
            pl.BlockSpec((L, tn), lambda j, b: (rb + b, 2 * nj + j)),
            pl.BlockSpec((2, 2 * L, tn), lambda j, b: (0, 0, j)),
            pl.BlockSpec((2, tn), lambda j, b: (0, j)),
            _full_spec((2 * L, L)),
            _full_spec((L, 2 * L)),
        ],
        out_specs=pl.BlockSpec((L, tn), lambda j, b: (b, j)),
        out_shape=jax.ShapeDtypeStruct((n_batch * L, D), BF16),
        compiler_params=_cparams(2),
        name=f"hyena_conv_L{L}",
    )(u, u, u, khat, skip, fwd, inv)


GLA_PROJ = 2 * GLA_H * GLA_DK + 2 * GLA_H * GLA_DV
GLA_COLS = GLA_PROJ + 2 * GLA_H * GLA_DK
GK1_PAD = 128


def _gla_proj_kernel(x_ref, g_ref, sc_ref, sh_ref, w_ref, wg1_ref, wg2_ref, bg_ref, *rest):
    *pending, o_ref = rest
    h = _norm_mod(_read_x(x_ref, pending), g_ref[...], sc_ref[...], sh_ref[...]).astype(BF16)
    p = _dot(h, w_ref[...])
    nq = GLA_H * GLA_DK
    o_ref[:, 0:nq] = p[:, 0:nq] * (GLA_DK ** -0.5)
    o_ref[:, nq:nq + nq + GLA_H * GLA_DV] = p[:, nq:nq + nq + GLA_H * GLA_DV]
    o_ref[:, 2 * nq + GLA_H * GLA_DV:GLA_PROJ] = _silu(p[:, 2 * nq + GLA_H * GLA_DV:GLA_PROJ])
    low = _dot(h, wg1_ref[...]).astype(BF16)
    gk = _dot(low, wg2_ref[...]) + bg_ref[...]
    o_ref[:, GLA_PROJ:GLA_COLS] = _log_sigmoid(gk) / GLA_NORMALIZER


def _gla_proj(x, mod, layer, norm_g, w_cat, wg1, wg2, bg, pend=None):
    p_specs, p_args = _pending_specs(pend, mod, TM, lambda t: (t, 0), _row_tm)
    return pl.pallas_call(
        _gla_proj_kernel,
        grid=(N_TOK // TM,),
        in_specs=[
            pl.BlockSpec((TM, D), lambda t: (t, 0)),
            _vec_spec(D),
            _mod_spec(layer, 1, _row_tm),
            _mod_spec(layer, 0, _row_tm),
            _full_spec((D, GLA_PROJ)),
            _full_spec((D, GK1_PAD)),
            _full_spec((GK1_PAD, 2 * GLA_H * GLA_DK)),
            _vec_spec(2 * GLA_H * GLA_DK),
        ] + p_specs,
        out_specs=pl.BlockSpec((TM, GLA_COLS), lambda t: (t, 0)),
        out_shape=jax.ShapeDtypeStruct((N_TOK, GLA_COLS), F32),
        compiler_params=_cparams(1),
        name="gla_proj",
    )(x, norm_g.reshape(1, D), mod, mod, w_cat, wg1, wg2, bg, *p_args)


def _gla_core_kernel(*refs, L, has_s0, hps, n_proj):
    q_ref, k_ref, v_ref, g_ref, gkf_ref, gkb_ref, tri_ref, ng_ref = refs[:8]
    rest = list(refs[8:])
    s0_ref = rest.pop(0) if has_s0 else None
    proj_refs = [rest.pop(0) for _ in range(n_proj)]
    o_ref, sf_ref, acc = rest[:3]
    o_gated = rest[3] if n_proj else o_ref
    C = GLA_CHUNK
    n = L // C
    ri = lax.broadcasted_iota(jnp.int32, (C, C), 0)
    ci = lax.broadcasted_iota(jnp.int32, (C, C), 1)
    nt_dims = (((1,), (1,)), ((), ()))
    tn_dims = (((0,), (0,)), ((), ()))
    whole = n <= GLA_WHOLE_SEQ_CHUNKS
    if whole:
        rl = lax.broadcasted_iota(jnp.int32, (L, L), 0)
        cl = lax.broadcasted_iota(jnp.int32, (L, L), 1)
        same_chunk = (rl >> (C.bit_length() - 1)) == (cl >> (C.bit_length() - 1))

    for hh in range(hps):
        kc = slice(hh * GLA_DK, (hh + 1) * GLA_DK)
        vc = slice(hh * GLA_DV, (hh + 1) * GLA_DV)
        for direction, gk_ref in enumerate((gkf_ref, gkb_ref)):
            keep = (ci <= ri) if direction == 0 else (ci >= ri)
            last = C - 1 if direction == 0 else 0
            gk_hi, gk_lo = _split(gk_ref[:, kc])
            gk_parts = jnp.concatenate([gk_hi, gk_lo], axis=1)
            if whole:
                b_all = _dot(tri_ref[direction], gk_parts)
                b_all = b_all[:, :GLA_DK] + b_all[:, GLA_DK:]
                qe_all = (q_ref[:, kc] * jnp.exp(b_all)).astype(BF16)
                ke_all = (k_ref[:, kc] * jnp.exp(-b_all)).astype(BF16)
                keep_all = same_chunk & ((cl <= rl) if direction == 0 else (cl >= rl))
                s_all = lax.dot_general(qe_all, ke_all, nt_dims, preferred_element_type=F32)
                s_all = jnp.where(keep_all, s_all, 0.0).astype(BF16)
                o_intra = _dot(s_all, v_ref[:, vc].astype(BF16))
            st = s0_ref[direction, hh].T if has_s0 else jnp.zeros((GLA_DV, GLA_DK), F32)
            order = range(n) if direction == 0 else range(n - 1, -1, -1)
            for c in order:
                rows = slice(c * C, (c + 1) * C)
                if whole:
                    b = b_all[rows]
                else:
                    b = _dot(tri_ref[direction], gk_parts[rows])
                    b = b[:, :GLA_DK] + b[:, GLA_DK:]
                b_last = b[last:last + 1, :]
                k = k_ref[rows, kc]
                v = v_ref[rows, vc].astype(BF16)
                kd = (k * jnp.exp(b_last - b)).astype(BF16)
                if whole:
                    qe = qe_all[rows]
                    o = o_intra[rows]
                else:
                    qe = (q_ref[rows, kc] * jnp.exp(b)).astype(BF16)
                    ke = (k * jnp.exp(-b)).astype(BF16)
                    scores = lax.dot_general(qe, ke, nt_dims, preferred_element_type=F32)
                    o = _dot(jnp.where(keep, scores, 0.0).astype(BF16), v)
                o = o + lax.dot_general(qe, st.astype(BF16), nt_dims, preferred_element_type=F32)
                if direction == 0:
                    acc[rows, vc] = o
                else:
                    acc[rows, vc] = acc[rows, vc] + o
                st = jnp.exp(b_last) * st + lax.dot_general(v, kd, tn_dims, preferred_element_type=F32)
            sf_ref[direction, hh] = st.T

        o = acc[:, vc]
        o = o * lax.rsqrt(jnp.mean(o * o, axis=-1, keepdims=True) + EPS) * ng_ref[...]
        o_gated[:, vc] = (o * g_ref[:, vc]).astype(BF16)

    if n_proj:
        w_ref, gate_ref, x_ref, *pending = proj_refs
        o_ref[...] = _read_x(x_ref, pending) + gate_ref[...] * _dot(o_gated[...], w_ref[...])


def _gla_core(proj, tri, norm_g, s0, L, n_batch, row_off, hps, out_proj=None):
    rb = row_off // L
    H = GLA_H
    nh = H // hps
    has_s0 = s0 is not None
    kb, vb = GLA_DK * hps, GLA_DV * hps
    in_specs = [
        pl.BlockSpec((L, kb), lambda b, h: (rb + b, h)),
        pl.BlockSpec((L, kb), lambda b, h: (rb + b, nh + h)),
        pl.BlockSpec((L, vb), lambda b, h: (rb + b, (2 * H * GLA_DK) // vb + h)),
        pl.BlockSpec((L, vb), lambda b, h: (rb + b, (2 * H * GLA_DK) // vb + nh + h)),
        pl.BlockSpec((L, kb), lambda b, h: (rb + b, GLA_PROJ // kb + h)),
        pl.BlockSpec((L, kb), lambda b, h: (rb + b, GLA_PROJ // kb + nh + h)),
        _full_spec(tri.shape),
        _vec_spec(GLA_DV),
    ]
    args = [proj] * 6 + [tri, norm_g.reshape(1, GLA_DV)]
    state_spec = pl.BlockSpec((None, 2, hps, GLA_DK, GLA_DV), lambda b, h: (b, 0, h, 0, 0))
    if has_s0:
        in_specs.append(state_spec)
        args.append(s0)
    first_spec = pl.BlockSpec((L, vb), lambda b, h: (b, h))
    first_shape = jax.ShapeDtypeStruct((n_batch * L, H * GLA_DV), BF16)
    scratch = [pltpu.VMEM((L, vb), F32)]
    aliases, n_proj = {}, 0
    if out_proj is not None:
        assert nh == 1, "the fused output projection needs every head in the step"
        x, w_o, mod, layer, pend = out_proj

        def row_fn(b, h):
            return 1 + b if row_off > 0 else 0

        x_spec = pl.BlockSpec((L, D), lambda b, h: (rb + b, 0))
        p_specs, p_args = _pending_specs(pend, mod, L, lambda b, h: (rb + b, 0), row_fn)
        aliases = {len(args) + 2: 0}
        in_specs += [_full_spec((H * GLA_DV, D)), _mod_spec(layer, 2, row_fn), x_spec] + p_specs
        args += [w_o, mod, x] + p_args
        n_proj = 3 + len(p_args)
        first_spec, first_shape = x_spec, jax.ShapeDtypeStruct((N_TOK, D), F32)
        scratch.append(pltpu.VMEM((L, vb), BF16))
    kern = functools.partial(_gla_core_kernel, L=L, has_s0=has_s0, hps=hps, n_proj=n_proj)
    return pl.pallas_call(
        kern,
        grid=(n_batch, nh),
        in_specs=in_specs,
        out_specs=[first_spec, state_spec],
        out_shape=[first_shape, jax.ShapeDtypeStruct((n_batch, 2, H, GLA_DK, GLA_DV), F32)],
        scratch_shapes=scratch,
        input_output_aliases=aliases,
        compiler_params=_cparams(2),
        name=f"gla_core_L{L}",
    )(*args)


def _fnet_mats(L):
    c = np.arange(FNET_C)
    ang_c = 2.0 * np.pi * np.outer(c, c) / FNET_C
    chan = np.concatenate([np.cos(ang_c), np.sin(ang_c)], axis=1) / math.sqrt(FNET_C)
    t = np.arange(L)
    ang_l = 2.0 * np.pi * np.outer(t, t) / L
    seq = np.concatenate([np.cos(ang_l), -np.sin(ang_l)], axis=1) / math.sqrt(L)
    return jnp.asarray(chan, F32).astype(BF16), jnp.asarray(seq, F32).astype(BF16)


def _fnet_kernel(x_ref, g_ref, sc_ref, sh_ref, gate_ref, chan_ref, seq_ref, w_ref, b_ref, *rest):
    *pending, o_ref = rest
    x = _read_x(x_ref, pending)
    h = _norm_mod(x, g_ref[...], sc_ref[...], sh_ref[...]).astype(BF16)
    chan = chan_ref[...]
    cos_parts, sin_parts = [], []
    for gi in range(FNET_GROUPS):
        cs = _dot(h[:, gi * FNET_C:(gi + 1) * FNET_C], chan)
        cos_parts.append(cs[:, :FNET_C])
        sin_parts.append(cs[:, FNET_C:])
    stacked = jnp.concatenate([jnp.concatenate(cos_parts, axis=1), jnp.concatenate(sin_parts, axis=1)], axis=0)
    mixed = _dot(seq_ref[...], stacked.astype(BF16))
    y = _dot(mixed.astype(BF16), w_ref[...]) + b_ref[...]
    o_ref[...] = x + gate_ref[...] * y


def _fnet(x, mod, layer, norm_g, chan, seq, w_bf16, bias, L, n_batch, row_off, pend=None):
    rb = row_off // L
    lat = row_off > 0

    def row_fn(b):
        return 1 + b if lat else 0

    p_specs, p_args = _pending_specs(pend, mod, L, lambda b: (rb + b, 0), row_fn)
    return pl.pallas_call(
        _fnet_kernel,
        grid=(n_batch,),
        in_specs=[
            pl.BlockSpec((L, D), lambda b: (rb + b, 0)),
            _vec_spec(D),
            _mod_spec(layer, 1, row_fn),
            _mod_spec(layer, 0, row_fn),
            _mod_spec(layer, 2, row_fn),
            _full_spec((FNET_C, 2 * FNET_C)),
            _full_spec((L, 2 * L)),
            _full_spec((D, D)),
            _vec_spec(D),
        ] + p_specs,
        out_specs=pl.BlockSpec((L, D), lambda b: (rb + b, 0)),
        out_shape=jax.ShapeDtypeStruct((N_TOK, D), F32),
        input_output_aliases={0: 0},
        compiler_params=_cparams(1),
        name=f"fnet_L{L}",
    )(x, norm_g.reshape(1, D), mod, mod, mod, chan, seq, w_bf16, bias.reshape(1, D), *p_args)


def _window_bounds(n, k):
    t = np.arange(n)
    lo, hi = k // 2, k - k // 2 - 1
    return np.maximum(t - lo, 0), np.minimum(t + hi + 1, n)


def _pool_mats(L, grid_rows):
    mats, inv = [], []
    for k in POOL_WINDOWS:
        if grid_rows is None:
            s, e = _window_bounds(L, k)
            idx = np.arange(L)[None, :]
            m = ((idx >= s[:, None]) & (idx < e[:, None])).astype(np.float64)
            cnt = (e - s).astype(np.float64)
        else:
            sr, er = _window_bounds(grid_rows, k)
            sc, ec = _window_bounds(GRID_W, k)
            ir = np.arange(grid_rows)[None, :]
            ic = np.arange(GRID_W)[None, :]
            mr = ((ir >= sr[:, None]) & (ir < er[:, None])).astype(np.float64)
            mc = ((ic >= sc[:, None]) & (ic < ec[:, None])).astype(np.float64)
            m = np.kron(mr, mc)
            cnt = np.kron((er - sr).astype(np.float64), (ec - sc).astype(np.float64))
        mats.append(m)
        inv.append(1.0 / cnt)
    return jnp.asarray(np.stack(mats), BF16), jnp.asarray(np.stack(inv)[:, :, None], F32)


def _pool_kernel(x_ref, g_ref, sc_ref, sh_ref, gate_ref, m_ref, ic_ref, w_ref, b_ref, ps_ref, *rest):
    *pending, o_ref = rest
    x = _read_x(x_ref, pending)
    h = _norm_mod(x, g_ref[...], sc_ref[...], sh_ref[...])
    outs = []
    for gi in range(len(POOL_WINDOWS)):
        hg = h[:, gi * POOL_G:(gi + 1) * POOL_G]
        hi, lo = _split(hg)
        m = m_ref[gi]
        mean = (_dot(m, hi) + _dot(m, lo)) * ic_ref[gi]
        outs.append(_dot((mean - hg).astype(BF16), w_ref[gi]))
    y = (jnp.concatenate(outs, axis=1) + b_ref[...]) * ps_ref[...]
    o_ref[...] = x + gate_ref[...] * y


def _pool(x, mod, layer, norm_g, mats, inv_cnt, w_bf16, bias, scale, L, n_batch, row_off, pend=None):
    rb = row_off // L
    lat = row_off > 0
    G = len(POOL_WINDOWS)

    def row_fn(b):
        return 1 + b if lat else 0

    p_specs, p_args = _pending_specs(pend, mod, L, lambda b: (rb + b, 0), row_fn)
    return pl.pallas_call(
        _pool_kernel,
        grid=(n_batch,),
        in_specs=[
            pl.BlockSpec((L, D), lambda b: (rb + b, 0)),
            _vec_spec(D),
            _mod_spec(layer, 1, row_fn),
            _mod_spec(layer, 0, row_fn),
            _mod_spec(layer, 2, row_fn),
            _full_spec((G, L, L)),
            _full_spec((G, L, 1)),
            _full_spec((G, POOL_G, POOL_G)),
            _vec_spec(D),
            _vec_spec(D),
        ] + p_specs,
        out_specs=pl.BlockSpec((L, D), lambda b: (rb + b, 0)),
        out_shape=jax.ShapeDtypeStruct((N_TOK, D), F32),
        input_output_aliases={0: 0},
        compiler_params=_cparams(1),
        name=f"pool_L{L}",
    )(x, norm_g.reshape(1, D), mod, mod, mod, mats, inv_cnt, w_bf16, bias.reshape(1, D), scale.reshape(1, D),
      *p_args)


ROUTER_PAD = 128
LANES = 128
D_EXT = D + LANES
MOE_TILE = 1024
MOE_MAX_TILES = N_TOK // MOE_TILE + MOE_GROUPS
MOE_ROWS = MOE_MAX_TILES * MOE_TILE
MOE_TILE_SHIFT = MOE_TILE.bit_length() - 1
assert 1 << MOE_TILE_SHIFT == MOE_TILE
MOE_ROW_BLOCK = 256
MOE_DMA_CHUNK = 64
MOE_CHUNK_SHIFT = MOE_DMA_CHUNK.bit_length() - 1
assert 1 << MOE_CHUNK_SHIFT == MOE_DMA_CHUNK and MOE_MAX_TILES % 2 == 0
MOE_Y_ROWS = N_TOK + 2 * MOE_DMA_CHUNK


ROUTE_ROWS = 8
TM_ROUTE = TM_BIG


def _moe_route_kernel(x_ref, g_ref, sc_ref, sh_ref, wr_ref, tri_ref, h3_ref, route_ref, cnt_ref, carry):
    t = pl.program_id(0)
    refs = (x_ref, g_ref, sc_ref, sh_ref, wr_ref, tri_ref, h3_ref, route_ref, cnt_ref, carry)
    pl.when(t < N_TOK // TM_ROUTE)(functools.partial(_moe_route_tile, t, *refs))

    @pl.when(t == N_TOK // TM_ROUTE)
    def _():
        h3_ref[...] = jnp.zeros_like(h3_ref)


def _moe_route_tile(t, x_ref, g_ref, sc_ref, sh_ref, wr_ref, tri_ref, h3_ref, route_ref, cnt_ref, carry):
    @pl.when(t == 0)
    def _():
        carry[...] = jnp.zeros_like(carry)

    h = _norm_mod(x_ref[...], g_ref[...], sc_ref[...], sh_ref[...])
    w_hi, w_lo = _split(wr_ref[...])
    h_hi, h_lo = _split(h)
    nt = (((1,), (1,)), ((), ()))
    logits = (lax.dot_general(w_hi, h_hi, nt, preferred_element_type=F32)
              + (lax.dot_general(w_hi, h_lo, nt, preferred_element_type=F32)
                 + lax.dot_general(w_lo, h_hi, nt, preferred_element_type=F32)))
    neg = jnp.float32(-jnp.inf)
    r8 = lax.broadcasted_iota(jnp.int32, (ROUTE_ROWS, TM_ROUTE), 0)
    r16 = lax.broadcasted_iota(jnp.int32, (MOE_E, TM_ROUTE), 0)
    gl = jnp.where(r8 < MOE_GROUPS, logits[MOE_E:MOE_E + ROUTE_ROWS], neg)
    g_max = jnp.max(gl, axis=0, keepdims=True)
    g_idx = jnp.min(jnp.where(gl == g_max, r8, ROUTE_ROWS), axis=0, keepdims=True)
    p_grp = 1.0 / jnp.sum(jnp.exp(gl - g_max), axis=0, keepdims=True)
    in_grp = (r16 >> 2) == g_idx
    el = jnp.where(in_grp, logits[:MOE_E], neg)
    m1 = jnp.max(el, axis=0, keepdims=True)
    i1 = jnp.min(jnp.where(el == m1, r16, MOE_E), axis=0, keepdims=True)
    z = jnp.sum(jnp.exp(el - m1), axis=0, keepdims=True)
    el2 = jnp.where(r16 == i1, neg, el)
    m2 = jnp.max(el2, axis=0, keepdims=True)
    i2 = jnp.min(jnp.where(el2 == m2, r16, MOE_E), axis=0, keepdims=True)
    p1 = 1.0 / z
    p2 = jnp.exp(m2 - m1) / z
    tot = p1 + p2
    eid = r8 + MOE_PER_GROUP * g_idx
    in4 = r8 < MOE_PER_GROUP
    cw4 = (jnp.where(in4 & (eid == i1), p_grp * (p1 / tot), 0.0)
           + jnp.where(in4 & (eid == i2), p_grp * (p2 / tot), 0.0))
    member = jnp.where(r8 == g_idx, 1.0, 0.0)
    before = _dot(member.astype(BF16), tri_ref[...]) + carry[:, 0:1]
    rank = jnp.sum(jnp.where(r8 == g_idx, before, 0.0), axis=0, keepdims=True)
    carry[...] = carry[...] + jnp.sum(member, axis=1, keepdims=True)
    cnt_ref[...] = carry[...].astype(jnp.int32)
    route_ref[...] = jnp.where(r8 == 0, g_idx, jnp.where(r8 == 1, rank.astype(jnp.int32), 0))
    h3_ref[:, :D] = h
    cw_rows = jnp.concatenate([cw4, jnp.zeros((LANES - ROUTE_ROWS, TM_ROUTE), F32)], axis=0)
    h3_ref[:, D:] = cw_rows.T


def _moe_route(x, mod, layer, norm_g, w_router_t, tri):
    nt = N_TOK // TM_ROUTE

    def tok_tile(t):
        return jnp.minimum(t, nt - 1)

    return pl.pallas_call(
        _moe_route_kernel,
        grid=(nt + 1,),
        in_specs=[
            pl.BlockSpec((TM_ROUTE, D), lambda t: (tok_tile(t), 0)),
            _vec_spec(D),
            _mod_spec(layer, 4, lambda t: _row_big(tok_tile(t))),
            _mod_spec(layer, 3, lambda t: _row_big(tok_tile(t))),
            _full_spec((ROUTER_PAD, D)),
            _full_spec((TM_ROUTE, TM_ROUTE)),
        ],
        out_specs=[
            pl.BlockSpec((TM_ROUTE, D_EXT), lambda t: (t, 0)),
            pl.BlockSpec((ROUTE_ROWS, TM_ROUTE), lambda t: (0, tok_tile(t))),
            pl.BlockSpec((ROUTE_ROWS, LANES), lambda t: (0, 0)),
        ],
        out_shape=[
            jax.ShapeDtypeStruct((N_TOK + TM_ROUTE, D_EXT), F32),
            jax.ShapeDtypeStruct((ROUTE_ROWS, N_TOK), jnp.int32),
            jax.ShapeDtypeStruct((ROUTE_ROWS, LANES), jnp.int32),
        ],
        scratch_shapes=[pltpu.VMEM((ROUTE_ROWS, LANES), F32)],
        compiler_params=_cparams(1),
        name="moe_route",
    )(x, norm_g.reshape(1, D), mod, mod, w_router_t, tri)


def _moe_invert_kernel(pos_ref, lo_ref, hi_ref, src_ref):
    def mark(j, carry):
        parity = lax.shift_right_logical(j, jnp.int32(MOE_TILE_SHIFT - MOE_CHUNK_SHIFT)) & 1
        first = N_TOK + MOE_DMA_CHUNK * parity
        base = j * MOE_DMA_CHUNK
        for rr in range(MOE_DMA_CHUNK):
            src_ref[base + rr] = first + rr
        return carry

    for g in range(MOE_GROUPS + 1):
        lax.fori_loop(lo_ref[g], hi_ref[g], mark, 0)

    def place(n, carry):
        src_ref[pos_ref[n]] = n
        return carry

    lax.fori_loop(0, N_TOK, place, 0, unroll=16)


def _moe_invert(pos, mark_lo, mark_hi):
    smem = pl.BlockSpec(memory_space=pltpu.SMEM)
    return pl.pallas_call(
        _moe_invert_kernel,
        in_specs=[smem, smem, smem],
        out_specs=smem,
        out_shape=jax.ShapeDtypeStruct((MOE_ROWS,), jnp.int32),
        name="moe_invert",
    )(pos, mark_lo, mark_hi)


def _moe_expert_kernel(src_ref, grp_ref, nact_ref, nchunk_ref, h_hbm, w1_ref, w3_ref, w2_ref, y_hbm,
                       xb0, xb1, ab0, ab1, gsem, ssem):
    t = pl.program_id(0)
    k = pl.program_id(1)
    n_active = nact_ref[0]
    last = n_active - 1
    T = MOE_TILE
    CH = MOE_DMA_CHUNK
    xbufs, accs = (xb0, xb1), (ab0, ab1)

    def gather_row(tile, slot, base, rr):
        tok = src_ref[tile * T + base + rr]
        rows = xbufs[slot].at[pl.ds(base, CH), :]
        return pltpu.make_async_copy(h_hbm.at[pl.ds(tok, 1), :], rows.at[pl.ds(rr, 1), :], gsem.at[slot])

    def scatter_row(tile, slot, base, rr):
        dst = src_ref[tile * T + base + rr]
        rows = accs[slot].at[pl.ds(base, CH), :]
        return pltpu.make_async_copy(rows.at[pl.ds(rr, 1), :], y_hbm.at[pl.ds(dst, 1), :], ssem.at[slot])

    def start_rows(make, tile, slot):
        def chunk(c, carry):
            base = pl.multiple_of(c * CH, CH)
            for rr in range(CH):
                make(tile, slot, base, rr).start()
            return carry

        lax.fori_loop(0, nchunk_ref[tile], chunk, 0)

    def wait_rows(tile, slot, gather):
        def chunk(c, carry):
            if gather:
                pltpu.make_async_copy(h_hbm.at[pl.ds(0, CH), :], xbufs[slot].at[pl.ds(0, CH), :], gsem.at[slot]).wait()
            else:
                pltpu.make_async_copy(accs[slot].at[pl.ds(0, CH), :], y_hbm.at[pl.ds(0, CH), :], ssem.at[slot]).wait()
            return carry

        lax.fori_loop(0, nchunk_ref[tile], chunk, 0)

    def step(slot):
        other = 1 - slot
        xb, acc = xbufs[slot], accs[slot]

        @pl.when(k == 0)
        def _():
            if slot == 0:
                @pl.when(t == 0)
                def _():
                    xb0[...] = jnp.zeros_like(xb0)
                    xb1[...] = jnp.zeros_like(xb1)
                    ab0[...] = jnp.zeros_like(ab0)
                    ab1[...] = jnp.zeros_like(ab1)
                    dump = pltpu.make_async_copy(ab0.at[pl.ds(0, 2 * CH), :], y_hbm.at[pl.ds(N_TOK, 2 * CH), :],
                                                 ssem.at[0])
                    dump.start()
                    dump.wait()
                    start_rows(gather_row, 0, 0)

            wait_rows(t, slot, True)

            @pl.when(t < last)
            def _():
                start_rows(gather_row, t + 1, other)

            @pl.when(t >= 2)
            def _():
                wait_rows(t - 2, slot, False)

        def experts(m):
            x = xb[:m, :D].astype(BF16)
            a = _dot(x, w1_ref[...].astype(BF16))
            b = _dot(x, w3_ref[...].astype(BF16))
            lane = lax.broadcasted_iota(jnp.int32, (m, LANES), 1)
            cwk = jnp.sum(jnp.where(lane == k, xb[:m, D:], 0.0), axis=-1, keepdims=True)
            hid = (_silu(a) * b * cwk).astype(BF16)
            acc[:m] = jnp.where(k > 0, acc[:m], 0.0) + _dot(hid, w2_ref[...].astype(BF16))

        blocks = (nchunk_ref[t] * CH + MOE_ROW_BLOCK - 1) // MOE_ROW_BLOCK
        for nb in range(1, T // MOE_ROW_BLOCK + 1):
            pl.when(blocks == nb)(functools.partial(experts, nb * MOE_ROW_BLOCK))

        @pl.when(k == MOE_PER_GROUP - 1)
        def _():
            start_rows(scatter_row, t, slot)

            @pl.when(t == last)
            def _():
                wait_rows(t, slot, False)

                @pl.when(t >= 1)
                def _():
                    wait_rows(t - 1, other, False)

    for slot in (0, 1):
        pl.when((t < n_active) & (t % 2 == slot))(functools.partial(step, slot))


def _moe_experts(h_ext, src, tile_group, n_active, n_chunk, layer, w1, w3, w2):
    T = MOE_TILE

    def w_index(t, k, src_ref, grp_ref, nact_ref, nchunk_ref):
        last = nact_ref[0] - 1
        e = jnp.where(t <= last, grp_ref[t] * MOE_PER_GROUP + k, grp_ref[last] * MOE_PER_GROUP + MOE_PER_GROUP - 1)
        return (layer, e, 0, 0)

    grid_spec = pltpu.PrefetchScalarGridSpec(
        num_scalar_prefetch=4,
        grid=(MOE_MAX_TILES, MOE_PER_GROUP),
        in_specs=[
            pl.BlockSpec(memory_space=pl.ANY),
            pl.BlockSpec((None, None, D, MOE_HID), w_index),
            pl.BlockSpec((None, None, D, MOE_HID), w_index),
            pl.BlockSpec((None, None, MOE_HID, D), w_index),
        ],
        out_specs=pl.BlockSpec(memory_space=pl.ANY),
        scratch_shapes=[
            pltpu.VMEM((T, D_EXT), F32),
            pltpu.VMEM((T, D_EXT), F32),
            pltpu.VMEM((T, D), F32),
            pltpu.VMEM((T, D), F32),
            pltpu.SemaphoreType.DMA((2,)),
            pltpu.SemaphoreType.DMA((2,)),
        ],
    )
    return pl.pallas_call(
        _moe_expert_kernel,
        grid_spec=grid_spec,
        out_shape=jax.ShapeDtypeStruct((MOE_Y_ROWS, D), F32),
        compiler_params=_cparams(2),
        name="moe_experts",
    )(src, tile_group, n_active, n_chunk, h_ext, w1, w3, w2)


def _moe_combine_kernel(y_ref, gate_ref, x_ref, o_ref):
    o_ref[...] = x_ref[...] + gate_ref[...] * y_ref[...]


def _moe_combine(x, y3, mod, layer):
    return pl.pallas_call(
        _moe_combine_kernel,
        grid=(N_TOK // TM,),
        in_specs=[
            pl.BlockSpec((TM, D), lambda t: (t, 0)),
            _mod_spec(layer, 5, _row_tm),
            pl.BlockSpec((TM, D), lambda t: (t, 0)),
        ],
        out_specs=pl.BlockSpec((TM, D), lambda t: (t, 0)),
        out_shape=jax.ShapeDtypeStruct((N_TOK, D), F32),
        input_output_aliases={2: 0},
        compiler_params=_cparams(1),
        name="moe_combine",
    )(y3, mod, x)


def _moe(x, mod, layer, norm_g, w_rg, w_re, w1, w3, w2, tri, final_g=None):
    w_router_t = jnp.zeros((ROUTER_PAD, D), F32).at[:MOE_E].set(w_re.T).at[MOE_E:MOE_E + MOE_GROUPS].set(w_rg.T)
    h3, route, counts = _moe_route(x, mod, layer, norm_g, w_router_t, tri)
    cnt = counts[:MOE_GROUPS, 0]
    ntile = (cnt + MOE_TILE - 1) // MOE_TILE
    tile_end = jnp.cumsum(ntile)
    seg_start = (tile_end - ntile) * MOE_TILE
    g_idx, rank = route[0], route[1]
    pos = jnp.sum(jnp.where(g_idx[None, :] == jnp.arange(MOE_GROUPS)[:, None], seg_start[:, None], 0), axis=0) + rank
    tiles = jnp.arange(MOE_MAX_TILES, dtype=jnp.int32)
    tile_group = jnp.minimum(jnp.sum(tiles[:, None] >= tile_end[None, :], axis=1), MOE_GROUPS - 1).astype(jnp.int32)
    n_active = tile_end[-1:].astype(jnp.int32)
    seg_end = tile_end * MOE_TILE
    mark_lo = jnp.concatenate([(seg_start + cnt) // MOE_DMA_CHUNK, seg_end[-1:] // MOE_DMA_CHUNK])
    mark_hi = jnp.concatenate([seg_end // MOE_DMA_CHUNK, jnp.full((1,), MOE_ROWS // MOE_DMA_CHUNK)])
    src = _moe_invert(pos.astype(jnp.int32), mark_lo.astype(jnp.int32), mark_hi.astype(jnp.int32))
    first_tile = (tile_end - ntile)[tile_group]
    real_rows = jnp.clip(cnt[tile_group] - (tiles - first_tile) * MOE_TILE, 0, MOE_TILE)
    n_chunk = ((real_rows + MOE_DMA_CHUNK - 1) // MOE_DMA_CHUNK).astype(jnp.int32)
    y3 = _moe_experts(h3, src, tile_group, n_active, n_chunk, layer, w1, w3, w2)
    if final_g is None:
        return y3
    return tuple(_moe_combine_norm(x, y3, mod, layer, final_g, off, nb * L) for L, nb, off in
                 ((CTX_L, CTX_B, 0), (LAT_L, LAT_B, N_CTX)))


def _combine_norm_kernel(y_ref, gate_ref, x_ref, g_ref, o_ref):
    x = x_ref[...] + gate_ref[...] * y_ref[...]
    o_ref[...] = x * lax.rsqrt(jnp.mean(x * x, axis=-1, keepdims=True) + EPS) * g_ref[...]


def _moe_combine_norm(x, y3, mod, layer, final_g, row_off, n_rows):
    off = row_off // TM_BIG
    first_lat = N_CTX // TM_BIG

    def row_fn(t):
        g = t + off
        return jnp.where(g < first_lat, 0, 1 + (g - first_lat) // (LAT_L // TM_BIG))

    return pl.pallas_call(
        _combine_norm_kernel,
        grid=(n_rows // TM_BIG,),
        in_specs=[
            pl.BlockSpec((TM_BIG, D), lambda t: (t + off, 0)),
            _mod_spec(layer, 5, row_fn),
            pl.BlockSpec((TM_BIG, D), lambda t: (t + off, 0)),
            _vec_spec(D),
        ],
        out_specs=pl.BlockSpec((TM_BIG, D), lambda t: (t, 0)),
        out_shape=jax.ShapeDtypeStruct((n_rows, D), F32),
        compiler_params=_cparams(1),
        name="combine_final_norm",
    )(y3, mod, x, final_g.reshape(1, D))


def kernel(x_prompt, x_sample, state_gla, c, c_ctx, w_ada, b_ada, norm_g, hy_w_in, hy_b_in, hy_conv_w, hy_conv_b, hy_f_w1, hy_f_b1, hy_f_freq, hy_f_w2, hy_f_b2, hy_f_w3, hy_skip, hy_w_out, hy_b_out, gla_w_q, gla_w_k, gla_w_v, gla_w_g, gla_w_gk1, gla_w_gk2, gla_b_gk, gla_norm_g, gla_w_o, fn_w_out, fn_b_out, pool_w, pool_b, pool_scale, moe_w_rg, moe_w_re, moe_w1, moe_w3, moe_w2, final_g):
    groups = ((CTX_L, CTX_B, 0, None), (LAT_L, LAT_B, N_CTX, LAT_L // GRID_W))

    x_ctx, x_lat = x_prompt.reshape(N_CTX, D), x_sample.reshape(N_LAT, D)
    x = None
    cond =jnp.zeros((MOD_ROWS, D), F32).at[0].set(c_ctx).at[1:1 + LAT_B].set(c)
    mod = _ada_table(cond, w_ada, b_ada).reshape(DEPTH * MOD_ROWS * 6, 1, D)

    tri_tm = jnp.asarray(np.triu(np.ones((TM_ROUTE, TM_ROUTE)), 1), BF16)

    new_states = []
    pend = None
    for i in range(DEPTH):
        kind, j = i % 4, i // 4
        if kind == 0:
            if i > 0:
                x = _moe_combine(x, pend[0], mod, pend[1])
                x_ctx, x_lat = x[:N_CTX], x[N_CTX:]
            u = _hyena_in(x_ctx, x_lat, mod, i, norm_g[i, 0], hy_w_in[j].astype(BF16), hy_b_in[j], hy_conv_w[j],
                          hy_conv_b[j])
            zs = []
            for L, nb, off, _ in groups:
                fwd, ff, inv = _dft_mats(L)
                khat = _hyena_filters(L, ff, hy_f_w1[j], hy_f_b1[j], hy_f_freq[j], hy_f_w2[j], hy_f_b2[j],
                                      hy_f_w3[j])
                zs.append(_hyena_conv(u, khat, hy_skip[j], fwd, inv, L, nb, off, D if L == CTX_L else 512))
            x = _outproj_joint(x_ctx, x_lat, zs[0], zs[1], hy_w_out[j].astype(BF16), hy_b_out[j], mod, i)
        elif kind == 1:
            w_cat = jnp.concatenate([gla_w_q[j], gla_w_k[j], gla_w_v[j], gla_w_g[j]], axis=1).astype(BF16)
            nk = GLA_H * GLA_DK
            wg1 = jnp.zeros((D, GK1_PAD), F32).at[:, :GLA_RANK].set(gla_w_gk1[j, 0])
            wg1 = wg1.at[:, GLA_RANK:2 * GLA_RANK].set(gla_w_gk1[j, 1]).astype(BF16)
            wg2 = jnp.zeros((GK1_PAD, 2 * nk), F32).at[:GLA_RANK, :nk].set(gla_w_gk2[j, 0])
            wg2 = wg2.at[GLA_RANK:2 * GLA_RANK, nk:].set(gla_w_gk2[j, 1]).astype(BF16)
            proj = _gla_proj(x, mod, i, norm_g[i, 0], w_cat, wg1, wg2, gla_b_gk[j].reshape(1, 2 * nk), pend)
            lower = np.tril(np.ones((GLA_CHUNK, GLA_CHUNK)))
            w_o = gla_w_o[j].astype(BF16)
            for L, nb, off, grid_rows in groups:
                eye = np.eye(L // GLA_CHUNK if L // GLA_CHUNK <= GLA_WHOLE_SEQ_CHUNKS else 1)
                tri = jnp.asarray(np.stack([np.kron(eye, lower), np.kron(eye, lower.T)]), BF16)
                s0 = None if grid_rows is None else state_gla[:, j]
                if L == CTX_L:
                    x, s_fin = _gla_core(proj, tri, gla_norm_g[j], s0, L, nb, off, GLA_H, (x, w_o, mod, i, pend))
                else:
                    o, s_fin = _gla_core(proj, tri, gla_norm_g[j], s0, L, nb, off, 1)
                    x = _outproj(x, o, w_o, jnp.zeros((D,), F32), mod, i, off, nb * L, pend)
                if grid_rows is None:
                    new_states.append(s_fin)
        elif kind == 2:
            w_out = fn_w_out[j].astype(BF16)
            for L, nb, off, _ in groups:
                chan, seq = _fnet_mats(L)
                x = _fnet(x, mod, i, norm_g[i, 0], chan, seq, w_out, fn_b_out[j], L, nb, off, pend)
        else:
            w_pool = pool_w[j].astype(BF16)
            for L, nb, off, grid_rows in groups:
                mats, inv_cnt = _pool_mats(L, grid_rows)
                x = _pool(x, mod, i, norm_g[i, 0], mats, inv_cnt, w_pool, pool_b[j], pool_scale[j], L, nb, off,
                          pend)

        out = _moe(x, mod, i, norm_g[i, 1], moe_w_rg[i], moe_w_re[i], moe_w1, moe_w3, moe_w2, tri_tm,
                   final_g if i == DEPTH - 1 else None)
        pend = (out, i)

    y_prompt, y_sample = out
    new_state_gla = jnp.stack(new_states, axis=1)
    return (y_prompt.reshape(CTX_B, CTX_L, D), y_sample.reshape(LAT_B, LAT_L, D), new_state_gla)
```

```python
import functools
import math

import jax
import jax.numpy as jnp
import numpy as np
from jax import lax
from jax.experimental import pallas as pl
from jax.experimental.pallas import tpu as pltpu

F32 = jnp.float32
BF16 = jnp.bfloat16

D = 1024
CTX_B, CTX_L = 32, 256
LAT_B, LAT_L = 2, 1024
N_CTX = CTX_B * CTX_L
N_LAT = LAT_B * LAT_L
N_TOK = N_CTX + N_LAT
DEPTH = 4
GRID_W = 64
EPS = 1e-6

HY_BANDS = 8
HY_EMB = 1 + 2 * HY_BANDS
HY_EMB_PAD = 32
HY_HID = 64
HY_FAST_DECAY = 0.3
HY_SLOW_DECAY = 1.5
HY_DECAY_TARGET = 1e-2

GLA_H = 4
GLA_DK = 128
GLA_DV = 256
GLA_RANK = 16
GLA_NORMALIZER = 16.0
GLA_CHUNK = 64
GLA_WHOLE_SEQ_CHUNKS = 4

FNET_GROUPS = 4
FNET_C = D // FNET_GROUPS
POOL_WINDOWS = (2, 4, 8, 16)
POOL_G = D // len(POOL_WINDOWS)

MOE_GROUPS = 4
MOE_PER_GROUP = 4
MOE_E = MOE_GROUPS * MOE_PER_GROUP
MOE_HID = D // 2

MOD_ROWS = 8
TM = 512
TM_BIG = 1024
V7X_VMEM_BYTES = 64 * 1024 * 1024
VMEM_LIMIT = V7X_VMEM_BYTES - 8 * 1024 * 1024


def _cparams(n_axes):
    return pltpu.CompilerParams(dimension_semantics=("arbitrary",) * n_axes, vmem_limit_bytes=VMEM_LIMIT)


def _norm_mod(x, g, sc, sh):
    ms = jnp.mean(x * x, axis=-1, keepdims=True)
    return (x * lax.rsqrt(ms + EPS) * g) * (1.0 + sc) + sh


def _split(a):
    hi = a.astype(BF16)
    lo = (a - hi.astype(F32)).astype(BF16)
    return hi, lo


def _dot(a, b):
    return jnp.dot(a, b, preferred_element_type=F32)


def _dot_precise(a, b):
    a_hi, a_lo = _split(a)
    b_hi, b_lo = _split(b)
    return _dot(a_hi, b_hi) + (_dot(a_hi, b_lo) + _dot(a_lo, b_hi))


def _silu(x):
    return x * (1.0 / (1.0 + jnp.exp(-x)))


def _log_sigmoid(x):
    return jnp.minimum(x, 0.0) - jnp.log(1.0 + jnp.exp(-jnp.abs(x)))


def _mod_spec(layer, chunk, row_fn):
    base = layer * MOD_ROWS * 6 + chunk

    def index_map(*ids):
        return (base + row_fn(*ids) * 6, 0, 0)

    return pl.BlockSpec((None, 1, D), index_map)


def _row_tm(t, *_):
    return jnp.where(t < N_CTX // TM, 0, 1 + (t - N_CTX // TM) // (LAT_L // TM))


def _row_big(t, *_):
    return jnp.where(t < N_CTX // TM_BIG, 0, 1 + (t - N_CTX // TM_BIG) // (LAT_L // TM_BIG))


def _vec_spec(n):
    return pl.BlockSpec((1, n), lambda *ids: (0, 0))


def _full_spec(shape):
    nd = len(shape)
    return pl.BlockSpec(shape, lambda *ids: (0,) * nd)


def _ada_kernel(cond_ref, w_ref, b_ref, o_ref):
    s = _silu(cond_ref[...]).astype(BF16)
    o_ref[...] = _dot(s, w_ref[...].astype(BF16)) + b_ref[...]


def _ada_table(cond, w_ada, b_ada):
    tn = 3072
    return pl.pallas_call(
        _ada_kernel,
        grid=(DEPTH, 6 * D // tn),
        in_specs=[
            pl.BlockSpec((MOD_ROWS, D), lambda i, j: (0, 0)),
            pl.BlockSpec((None, D, tn), lambda i, j: (i, 0, j)),
            pl.BlockSpec((None, 1, tn), lambda i, j: (i, 0, j)),
        ],
        out_specs=pl.BlockSpec((None, MOD_ROWS, tn), lambda i, j: (i, 0, j)),
        out_shape=jax.ShapeDtypeStruct((DEPTH, MOD_ROWS, 6 * D), F32),
        compiler_params=_cparams(2),
        name="ada_table",
    )(cond, w_ada, b_ada.reshape(DEPTH, 1, 6 * D))


def _pending_specs(pend, mod, block_rows, row_index, row_fn):
    if pend is None:
        return [], []
    y, prev_layer = pend
    return [pl.BlockSpec((block_rows, D), row_index), _mod_spec(prev_layer, 5, row_fn)], [y, mod]


def _read_x(x_ref, pending_refs):
    x = x_ref[...]
    if pending_refs:
        y_ref, gate_ref = pending_refs
        x = x + gate_ref[...] * y_ref[...]
    return x


def _outproj_kernel(z_ref, w_ref, b_ref, gate_ref, x_ref, *rest):
    *pending, o_ref = rest
    y = _dot(z_ref[...], w_ref[...]) + b_ref[...]
    o_ref[...] = _read_x(x_ref, pending) + gate_ref[...] * y


def _outproj(x, z, w_bf16, bias, mod, layer, row_off, n_rows, pend=None):
    k = z.shape[1]
    off = row_off // TM
    first_lat = N_CTX // TM

    def row_fn(t):
        g = t + off
        return jnp.where(g < first_lat, 0, 1 + (g - first_lat) // (LAT_L // TM))

    p_specs, p_args = _pending_specs(pend, mod, TM, lambda t: (t + off, 0), row_fn)
    return pl.pallas_call(
        _outproj_kernel,
        grid=(n_rows // TM,),
        in_specs=[
            pl.BlockSpec((TM, k), lambda t: (t, 0)),
            _full_spec((k, D)),
            _vec_spec(D),
            _mod_spec(layer, 2, row_fn),
            pl.BlockSpec((TM, D), lambda t: (t + off, 0)),
        ] + p_specs,
        out_specs=pl.BlockSpec((TM, D), lambda t: (t + off, 0)),
        out_shape=jax.ShapeDtypeStruct((N_TOK, D), F32),
        input_output_aliases={4: 0},
        compiler_params=_cparams(1),
        name="outproj_residual",
    )(z, w_bf16, bias.reshape(1, D), mod, x, *p_args)


def _outproj_joint_kernel(zc_ref, zl_ref, w_ref, b_ref, gate_ref, xc_ref, xl_ref, o_ref):
    t = pl.program_id(0)
    for is_ctx, z_ref, x_ref in ((True, zc_ref, xc_ref), (False, zl_ref, xl_ref)):
        @pl.when((t < N_CTX // TM_BIG) == is_ctx)
        def _(z_ref=z_ref, x_ref=x_ref):
            y = _dot(z_ref[...], w_ref[...]) + b_ref[...]
            o_ref[...] = x_ref[...] + gate_ref[...] * y


def _outproj_joint(x_ctx, x_lat, z_ctx, z_lat, w_bf16, bias, mod, layer):
    k = z_ctx.shape[1]
    n_ctx_tiles = N_CTX // TM_BIG

    def ctx_block(t):
        return (jnp.minimum(t, n_ctx_tiles - 1), 0)

    def lat_block(t):
        return (jnp.maximum(t - n_ctx_tiles, 0), 0)

    return pl.pallas_call(
        _outproj_joint_kernel,
        grid=(N_TOK // TM_BIG,),
        in_specs=[
            pl.BlockSpec((TM_BIG, k), ctx_block),
            pl.BlockSpec((TM_BIG, k), lat_block),
            _full_spec((k, D)),
            _vec_spec(D),
            _mod_spec(layer, 2, _row_big),
            pl.BlockSpec((TM_BIG, D), ctx_block),
            pl.BlockSpec((TM_BIG, D), lat_block),
        ],
        out_specs=pl.BlockSpec((TM_BIG, D), lambda t: (t, 0)),
        out_shape=jax.ShapeDtypeStruct((N_TOK, D), F32),
        compiler_params=_cparams(1),
        name="outproj_joint",
    )(z_ctx, z_lat, w_bf16, bias.reshape(1, D), mod, x_ctx, x_lat)


def _dft_mats(L):
    n2 = 2 * L
    k = np.arange(L)[:, None].astype(np.float64)
    n = np.arange(n2)[None, :].astype(np.float64)
    ang = 2.0 * np.pi * k * n / n2
    full = np.concatenate([np.cos(ang), -np.sin(ang)], axis=0)
    full[L, :] = np.cos(np.pi * np.arange(n2))
    fwd = full[:, :L]
    bwd = np.zeros((n2, L))
    bwd[:, 1:] = full[:, n2 - np.arange(1, L)]
    t = np.arange(L)[:, None].astype(np.float64)
    kk = np.arange(L)[None, :].astype(np.float64)
    ang_i = 2.0 * np.pi * t * kk / n2
    inv_re = np.cos(ang_i) / L
    inv_re[:, 0] = 1.0 / n2
    inv_im = -np.sin(ang_i) / L
    inv_im[:, 0] = np.cos(np.pi * np.arange(L)) / n2
    inv = np.concatenate([inv_re, inv_im], axis=1)
    return tuple(jnp.asarray(m, F32).astype(BF16) for m in (fwd, np.concatenate([fwd, bwd], axis=1), inv))


def _hyena_pos_emb(L):
    pos = np.arange(L, dtype=np.float64)
    bands = np.linspace(1e-4, HY_BANDS - 1, HY_BANDS)
    ang = (2.0 * np.pi * pos / L)[:, None] * bands[None, :]
    z = np.concatenate([(pos / L)[:, None], np.cos(ang), -np.sin(ang)], axis=-1)
    zp = np.zeros((L, HY_EMB_PAD))
    zp[:, :HY_EMB] = z
    return jnp.asarray(zp, F32)


def _hyena_filter_kernel(z_ref, w1_ref, b1_ref, fr_ref, w2_ref, b2_ref, w3f_ref, w3b_ref, ff_ref, o_ref, *, L, tn):
    j = pl.program_id(1)
    fr = fr_ref[...]
    f = jnp.sin(fr * (_dot_precise(z_ref[...], w1_ref[...]) + b1_ref[...]))
    f = jnp.sin(fr * (_dot_precise(f, w2_ref[...]) + b2_ref[...]))
    t_lin = lax.broadcasted_iota(jnp.int32, (L, tn), 0).astype(F32) / float(L - 1)
    ch = (lax.broadcasted_iota(jnp.int32, (L, tn), 1) + j * tn).astype(F32)
    max_decay = math.log(HY_DECAY_TARGET) / HY_FAST_DECAY
    min_decay = math.log(HY_DECAY_TARGET) / HY_SLOW_DECAY
    deltas = min_decay + ch * ((max_decay - min_decay) / float(D - 1))
    window = jnp.exp(-t_lin * jnp.abs(deltas))
    kf = _dot_precise(f, w3f_ref[...]) * window
    kb = _dot_precise(f, w3b_ref[...]) * window
    taps = jnp.concatenate([kf, kb], axis=0).astype(BF16)
    o_ref[...] = _dot(ff_ref[...], taps)


def _hyena_filters(L, ff, f_w1, f_b1, f_freq, f_w2, f_b2, f_w3):
    tn = 512
    nj = D // tn
    w1p = jnp.zeros((HY_EMB_PAD, HY_HID), F32).at[:HY_EMB].set(f_w1)
    kern = functools.partial(_hyena_filter_kernel, L=L, tn=tn)
    return pl.pallas_call(
        kern,
        grid=(2, nj),
        in_specs=[
            _full_spec((L, HY_EMB_PAD)),
            _full_spec((HY_EMB_PAD, HY_HID)),
            _vec_spec(HY_HID),
            _vec_spec(HY_HID),
            _full_spec((HY_HID, HY_HID)),
            _vec_spec(HY_HID),
            pl.BlockSpec((HY_HID, tn), lambda o, j: (0, o * nj + j)),
            pl.BlockSpec((HY_HID, tn), lambda o, j: (0, (2 + o) * nj + j)),
            _full_spec((2 * L, 2 * L)),
        ],
        out_specs=pl.BlockSpec((None, 2 * L, tn), lambda o, j: (o, 0, j)),
        out_shape=jax.ShapeDtypeStruct((2, 2 * L, D), F32),
        compiler_params=_cparams(2),
        name=f"hyena_filters_L{L}",
    )(_hyena_pos_emb(L), w1p, f_b1.reshape(1, -1), f_freq.reshape(1, -1), f_w2, f_b2.reshape(1, -1),
      f_w3, f_w3, ff)


def _hyena_in_kernel(xc_ref, xl_ref, g_ref, sc_ref, sh_ref, w_ref, b_ref, cw_ref, cb_ref, o_ref, h_scr):
    t = pl.program_id(0)
    first = pl.program_id(1) == 0

    for is_ctx, x_ref in ((True, xc_ref), (False, xl_ref)):
        @pl.when(first & ((t < N_CTX // TM_BIG) == is_ctx))
        def _(x_ref=x_ref):
            h_scr[...] = _norm_mod(x_ref[...], g_ref[...], sc_ref[...], sh_ref[...]).astype(BF16)

    u = _dot(h_scr[...], w_ref[...]) + b_ref[...]
    cw = cw_ref[...]
    o_ref[...] = pltpu.roll(u, 1, 0) * cw[0:1] + u * cw[1:2] + pltpu.roll(u, TM_BIG - 1, 0) * cw[2:3] + cb_ref[...]
    is_ctx = (t < N_CTX // TM_BIG).astype(F32)
    for start in range(0, TM_BIG, CTX_L):
        f = 1.0 if start % LAT_L == 0 else is_ctx
        before = (start - 1) % TM_BIG
        o_ref[start:start + 1, :] = o_ref[start:start + 1, :] - f * (u[before:before + 1] * cw[0:1])
        end = start + CTX_L - 1
        g = 1.0 if (end + 1) % LAT_L == 0 else is_ctx
        after = (end + 1) % TM_BIG
        o_ref[end:end + 1, :] = o_ref[end:end + 1, :] - g * (u[after:after + 1] * cw[2:3])


def _hyena_in(x_ctx, x_lat, mod, layer, norm_g, w_in_bf16, b_in, conv_w, conv_b):
    n_ctx_tiles = N_CTX // TM_BIG
    return pl.pallas_call(
        _hyena_in_kernel,
        grid=(N_TOK // TM_BIG, 3),
        in_specs=[
            pl.BlockSpec((TM_BIG, D), lambda t, p: (jnp.minimum(t, n_ctx_tiles - 1), 0)),
            pl.BlockSpec((TM_BIG, D), lambda t, p: (jnp.maximum(t - n_ctx_tiles, 0), 0)),
            _vec_spec(D),
            _mod_spec(layer, 1, _row_big),
            _mod_spec(layer, 0, _row_big),
            pl.BlockSpec((D, D), lambda t, p: (0, p)),
            pl.BlockSpec((1, D), lambda t, p: (0, p)),
            pl.BlockSpec((3, D), lambda t, p: (0, p)),
            pl.BlockSpec((1, D), lambda t, p: (0, p)),
        ],
        out_specs=pl.BlockSpec((TM_BIG, D), lambda t, p: (t, p)),
        out_shape=jax.ShapeDtypeStruct((N_TOK, 3 * D), F32),
        scratch_shapes=[pltpu.VMEM((TM_BIG, D), BF16)],
        compiler_params=_cparams(2),
        name="hyena_in",
    )(x_ctx, x_lat, norm_g.reshape(1, D), mod, mod, w_in_bf16, b_in.reshape(1, -1), conv_w, conv_b.reshape(1, -1))


def _hyena_conv_kernel(v_ref, x1_ref, x2_ref, kh_ref, skip_ref, fwd_ref, inv_ref, o_ref, *, L):
    fwd = fwd_ref[...]
    inv = inv_ref[...]
    row0 = lax.broadcasted_iota(jnp.int32, (L, v_ref.shape[1]), 0) == 0

    def long_conv(z, order):
        zh = _dot(fwd, z.astype(BF16))
        zr, zi = zh[:L], zh[L:]
        kr, ki = kh_ref[order, :L, :], kh_ref[order, L:, :]
        pr = jnp.where(row0, zr * kr, zr * kr - zi * ki)
        pi = jnp.where(row0, zi * ki, zr * ki + zi * kr)
        prod = jnp.concatenate([pr, pi], axis=0).astype(BF16)
        return _dot(inv, prod) + z * skip_ref[order:order + 1, :]

    z = x1_ref[...] * long_conv(v_ref[...], 0)
    z = x2_ref[...] * long_conv(z, 1)
    o_ref[...] = z.astype(BF16)


def _hyena_conv(u, khat, skip, fwd, inv, L, n_batch, row_off, tn):
    nj = D // tn
    rb = row_off // L
    kern = functools.partial(_hyena_conv_kernel, L=L)
    return pl.pallas_call(
        kern,
        grid=(nj, n_batch),
        in_specs=[
            pl.BlockSpec((L, tn), lambda j, b: (rb + b, j)),
            pl.BlockSpec((L, tn), lambda j, b: (rb + b, nj + j)),
            pl.BlockSpec((L, tn), lambda j, b: (rb + b, 2 * nj + j)),
            pl.BlockSpec((2, 2 * L, tn), lambda j, b: (0, 0, j)),
            pl.BlockSpec((2, tn), lambda j, b: (0, j)),
            _full_spec((2 * L, L)),
            _full_spec((L, 2 * L)),
        ],
        out_specs=pl.BlockSpec((L, tn), lambda j, b: (b, j)),
        out_shape=jax.ShapeDtypeStruct((n_batch * L, D), BF16),
        compiler_params=_cparams(2),
        name=f"hyena_conv_L{L}",
    )(u, u, u, khat, skip, fwd, inv)


GLA_PROJ = 2 * GLA_H * GLA_DK + 2 * GLA_H * GLA_DV
GLA_COLS = GLA_PROJ + 2 * GLA_H * GLA_DK
GK1_PAD = 128


def _gla_proj_kernel(x_ref, g_ref, sc_ref, sh_ref, w_ref, wg1_ref, wg2_ref, bg_ref, *rest):
    *pending, o_ref = rest
    h = _norm_mod(_read_x(x_ref, pending), g_ref[...], sc_ref[...], sh_ref[...]).astype(BF16)
    p = _dot(h, w_ref[...])
    nq = GLA_H * GLA_DK
    o_ref[:, 0:nq] = p[:, 0:nq] * (GLA_DK ** -0.5)
    o_ref[:, nq:nq + nq + GLA_H * GLA_DV] = p[:, nq:nq + nq + GLA_H * GLA_DV]
    o_ref[:, 2 * nq + GLA_H * GLA_DV:GLA_PROJ] = _silu(p[:, 2 * nq + GLA_H * GLA_DV:GLA_PROJ])
    low = _dot(h, wg1_ref[...]).astype(BF16)
    gk = _dot(low, wg2_ref[...]) + bg_ref[...]
    o_ref[:, GLA_PROJ:GLA_COLS] = _log_sigmoid(gk) / GLA_NORMALIZER


def _gla_proj(x, mod, layer, norm_g, w_cat, wg1, wg2, bg, pend=None):
    p_specs, p_args = _pending_specs(pend, mod, TM, lambda t: (t, 0), _row_tm)
    return pl.pallas_call(
        _gla_proj_kernel,
        grid=(N_TOK // TM,),
        in_specs=[
            pl.BlockSpec((TM, D), lambda t: (t, 0)),
            _vec_spec(D),
            _mod_spec(layer, 1, _row_tm),
            _mod_spec(layer, 0, _row_tm),
            _full_spec((D, GLA_PROJ)),
            _full_spec((D, GK1_PAD)),
            _full_spec((GK1_PAD, 2 * GLA_H * GLA_DK)),
            _vec_spec(2 * GLA_H * GLA_DK),
        ] + p_specs,
        out_specs=pl.BlockSpec((TM, GLA_COLS), lambda t: (t, 0)),
        out_shape=jax.ShapeDtypeStruct((N_TOK, GLA_COLS), F32),
        compiler_params=_cparams(1),
        name="gla_proj",
    )(x, norm_g.reshape(1, D), mod, mod, w_cat, wg1, wg2, bg, *p_args)


def _gla_core_kernel(*refs, L, has_s0, hps, n_proj):
    q_ref, k_ref, v_ref, g_ref, gkf_ref, gkb_ref, tri_ref, ng_ref = refs[:8]
    rest = list(refs[8:])
    s0_ref = rest.pop(0) if has_s0 else None
    proj_refs = [rest.pop(0) for _ in range(n_proj)]
    o_ref, sf_ref, acc = rest[:3]
    o_gated = rest[3] if n_proj else o_ref
    C = GLA_CHUNK
    n = L // C
    ri = lax.broadcasted_iota(jnp.int32, (C, C), 0)
    ci = lax.broadcasted_iota(jnp.int32, (C, C), 1)
    nt_dims = (((1,), (1,)), ((), ()))
    tn_dims = (((0,), (0,)), ((), ()))
    whole = n <= GLA_WHOLE_SEQ_CHUNKS
    if whole:
        rl = lax.broadcasted_iota(jnp.int32, (L, L), 0)
        cl = lax.broadcasted_iota(jnp.int32, (L, L), 1)
        same_chunk = (rl >> (C.bit_length() - 1)) == (cl >> (C.bit_length() - 1))

    for hh in range(hps):
        kc = slice(hh * GLA_DK, (hh + 1) * GLA_DK)
        vc = slice(hh * GLA_DV, (hh + 1) * GLA_DV)
        for direction, gk_ref in enumerate((gkf_ref, gkb_ref)):
            keep = (ci <= ri) if direction == 0 else (ci >= ri)
            last = C - 1 if direction == 0 else 0
            gk_hi, gk_lo = _split(gk_ref[:, kc])
            gk_parts = jnp.concatenate([gk_hi, gk_lo], axis=1)
            if whole:
                b_all = _dot(tri_ref[direction], gk_parts)
                b_all = b_all[:, :GLA_DK] + b_all[:, GLA_DK:]
                qe_all = (q_ref[:, kc] * jnp.exp(b_all)).astype(BF16)
                ke_all = (k_ref[:, kc] * jnp.exp(-b_all)).astype(BF16)
                keep_all = same_chunk & ((cl <= rl) if direction == 0 else (cl >= rl))
                s_all = lax.dot_general(qe_all, ke_all, nt_dims, preferred_element_type=F32)
                s_all = jnp.where(keep_all, s_all, 0.0).astype(BF16)
                o_intra = _dot(s_all, v_ref[:, vc].astype(BF16))
            st = s0_ref[direction, hh].T if has_s0 else jnp.zeros((GLA_DV, GLA_DK), F32)
            order = range(n) if direction == 0 else range(n - 1, -1, -1)
            for c in order:
                rows = slice(c * C, (c + 1) * C)
                if whole:
                    b = b_all[rows]
                else:
                    b = _dot(tri_ref[direction], gk_parts[rows])
                    b = b[:, :GLA_DK] + b[:, GLA_DK:]
                b_last = b[last:last + 1, :]
                k = k_ref[rows, kc]
                v = v_ref[rows, vc].astype(BF16)
                kd = (k * jnp.exp(b_last - b)).astype(BF16)
                if whole:
                    qe = qe_all[rows]
                    o = o_intra[rows]
                else:
                    qe = (q_ref[rows, kc] * jnp.exp(b)).astype(BF16)
                    ke = (k * jnp.exp(-b)).astype(BF16)
                    scores = lax.dot_general(qe, ke, nt_dims, preferred_element_type=F32)
                    o = _dot(jnp.where(keep, scores, 0.0).astype(BF16), v)
                o = o + lax.dot_general(qe, st.astype(BF16), nt_dims, preferred_element_type=F32)
                if direction == 0:
                    acc[rows, vc] = o
                else:
                    acc[rows, vc] = acc[rows, vc] + o
                st = jnp.exp(b_last) * st + lax.dot_general(v, kd, tn_dims, preferred_element_type=F32)
            sf_ref[direction, hh] = st.T

        o = acc[:, vc]
        o = o * lax.rsqrt(jnp.mean(o * o, axis=-1, keepdims=True) + EPS) * ng_ref[...]
        o_gated[:, vc] = (o * g_ref[:, vc]).astype(BF16)

    if n_proj:
        w_ref, gate_ref, x_ref, *pending = proj_refs
        o_ref[...] = _read_x(x_ref, pending) + gate_ref[...] * _dot(o_gated[...], w_ref[...])


def _gla_core(proj, tri, norm_g, s0, L, n_batch, row_off, hps, out_proj=None):
    rb = row_off // L
    H = GLA_H
    nh = H // hps
    has_s0 = s0 is not None
    kb, vb = GLA_DK * hps, GLA_DV * hps
    in_specs = [
        pl.BlockSpec((L, kb), lambda b, h: (rb + b, h)),
        pl.BlockSpec((L, kb), lambda b, h: (rb + b, nh + h)),
        pl.BlockSpec((L, vb), lambda b, h: (rb + b, (2 * H * GLA_DK) // vb + h)),
        pl.BlockSpec((L, vb), lambda b, h: (rb + b, (2 * H * GLA_DK) // vb + nh + h)),
        pl.BlockSpec((L, kb), lambda b, h: (rb + b, GLA_PROJ // kb + h)),
        pl.BlockSpec((L, kb), lambda b, h: (rb + b, GLA_PROJ // kb + nh + h)),
        _full_spec(tri.shape),
        _vec_spec(GLA_DV),
    ]
    args = [proj] * 6 + [tri, norm_g.reshape(1, GLA_DV)]
    state_spec = pl.BlockSpec((None, 2, hps, GLA_DK, GLA_DV), lambda b, h: (b, 0, h, 0, 0))
    if has_s0:
        in_specs.append(state_spec)
        args.append(s0)
    first_spec = pl.BlockSpec((L, vb), lambda b, h: (b, h))
    first_shape = jax.ShapeDtypeStruct((n_batch * L, H * GLA_DV), BF16)
    scratch = [pltpu.VMEM((L, vb), F32)]
    aliases, n_proj = {}, 0
    if out_proj is not None:
        assert nh == 1, "the fused output projection needs every head in the step"
        x, w_o, mod, layer, pend = out_proj

        def row_fn(b, h):
            return 1 + b if row_off > 0 else 0

        x_spec = pl.BlockSpec((L, D), lambda b, h: (rb + b, 0))
        p_specs, p_args = _pending_specs(pend, mod, L, lambda b, h: (rb + b, 0), row_fn)
        aliases = {len(args) + 2: 0}
        in_specs += [_full_spec((H * GLA_DV, D)), _mod_spec(layer, 2, row_fn), x_spec] + p_specs
        args += [w_o, mod, x] + p_args
        n_proj = 3 + len(p_args)
        first_spec, first_shape = x_spec, jax.ShapeDtypeStruct((N_TOK, D), F32)
        scratch.append(pltpu.VMEM((L, vb), BF16))
    kern = functools.partial(_gla_core_kernel, L=L, has_s0=has_s0, hps=hps, n_proj=n_proj)
    return pl.pallas_call(
        kern,
        grid=(n_batch, nh),
        in_specs=in_specs,
        out_specs=[first_spec, state_spec],
        out_shape=[first_shape, jax.ShapeDtypeStruct((n_batch, 2, H, GLA_DK, GLA_DV), F32)],
        scratch_shapes=scratch,
        input_output_aliases=aliases,
        compiler_params=_cparams(2),
        name=f"gla_core_L{L}",
    )(*args)


def _fnet_mats(L):
    c = np.arange(FNET_C)
    ang_c = 2.0 * np.pi * np.outer(c, c) / FNET_C
    chan = np.concatenate([np.cos(ang_c), np.sin(ang_c)], axis=1) / math.sqrt(FNET_C)
    t = np.arange(L)
    ang_l = 2.0 * np.pi * np.outer(t, t) / L
    seq = np.concatenate([np.cos(ang_l), -np.sin(ang_l)], axis=1) / math.sqrt(L)
    return jnp.asarray(chan, F32).astype(BF16), jnp.asarray(seq, F32).astype(BF16)


def _fnet_kernel(x_ref, g_ref, sc_ref, sh_ref, gate_ref, chan_ref, seq_ref, w_ref, b_ref, *rest):
    *pending, o_ref = rest
    x = _read_x(x_ref, pending)
    h = _norm_mod(x, g_ref[...], sc_ref[...], sh_ref[...]).astype(BF16)
    chan = chan_ref[...]
    cos_parts, sin_parts = [], []
    for gi in range(FNET_GROUPS):
        cs = _dot(h[:, gi * FNET_C:(gi + 1) * FNET_C], chan)
        cos_parts.append(cs[:, :FNET_C])
        sin_parts.append(cs[:, FNET_C:])
    stacked = jnp.concatenate([jnp.concatenate(cos_parts, axis=1), jnp.concatenate(sin_parts, axis=1)], axis=0)
    mixed = _dot(seq_ref[...], stacked.astype(BF16))
    y = _dot(mixed.astype(BF16), w_ref[...]) + b_ref[...]
    o_ref[...] = x + gate_ref[...] * y


def _fnet(x, mod, layer, norm_g, chan, seq, w_bf16, bias, L, n_batch, row_off, pend=None):
    rb = row_off // L
    lat = row_off > 0

    def row_fn(b):
        return 1 + b if lat else 0

    p_specs, p_args = _pending_specs(pend, mod, L, lambda b: (rb + b, 0), row_fn)
    return pl.pallas_call(
        _fnet_kernel,
        grid=(n_batch,),
        in_specs=[
            pl.BlockSpec((L, D), lambda b: (rb + b, 0)),
            _vec_spec(D),
            _mod_spec(layer, 1, row_fn),
            _mod_spec(layer, 0, row_fn),
            _mod_spec(layer, 2, row_fn),
            _full_spec((FNET_C, 2 * FNET_C)),
            _full_spec((L, 2 * L)),
            _full_spec((D, D)),
            _vec_spec(D),
        ] + p_specs,
        out_specs=pl.BlockSpec((L, D), lambda b: (rb + b, 0)),
        out_shape=jax.ShapeDtypeStruct((N_TOK, D), F32),
        input_output_aliases={0: 0},
        compiler_params=_cparams(1),
        name=f"fnet_L{L}",
    )(x, norm_g.reshape(1, D), mod, mod, mod, chan, seq, w_bf16, bias.reshape(1, D), *p_args)


def _window_bounds(n, k):
    t = np.arange(n)
    lo, hi = k // 2, k - k // 2 - 1
    return np.maximum(t - lo, 0), np.minimum(t + hi + 1, n)


def _pool_mats(L, grid_rows):
    mats, inv = [], []
    for k in POOL_WINDOWS:
        if grid_rows is None:
            s, e = _window_bounds(L, k)
            idx = np.arange(L)[None, :]
            m = ((idx >= s[:, None]) & (idx < e[:, None])).astype(np.float64)
            cnt = (e - s).astype(np.float64)
        else:
            sr, er = _window_bounds(grid_rows, k)
            sc, ec = _window_bounds(GRID_W, k)
            ir = np.arange(grid_rows)[None, :]
            ic = np.arange(GRID_W)[None, :]
            mr = ((ir >= sr[:, None]) & (ir < er[:, None])).astype(np.float64)
            mc = ((ic >= sc[:, None]) & (ic < ec[:, None])).astype(np.float64)
            m = np.kron(mr, mc)
            cnt = np.kron((er - sr).astype(np.float64), (ec - sc).astype(np.float64))
        mats.append(m)
        inv.append(1.0 / cnt)
    return jnp.asarray(np.stack(mats), BF16), jnp.asarray(np.stack(inv)[:, :, None], F32)


def _pool_kernel(x_ref, g_ref, sc_ref, sh_ref, gate_ref, m_ref, ic_ref, w_ref, b_ref, ps_ref, *rest):
    *pending, o_ref = rest
    x = _read_x(x_ref, pending)
    h = _norm_mod(x, g_ref[...], sc_ref[...], sh_ref[...])
    outs = []
    for gi in range(len(POOL_WINDOWS)):
        hg = h[:, gi * POOL_G:(gi + 1) * POOL_G]
        hi, lo = _split(hg)
        m = m_ref[gi]
        mean = (_dot(m, hi) + _dot(m, lo)) * ic_ref[gi]
        outs.append(_dot((mean - hg).astype(BF16), w_ref[gi]))
    y = (jnp.concatenate(outs, axis=1) + b_ref[...]) * ps_ref[...]
    o_ref[...] = x + gate_ref[...] * y


def _pool(x, mod, layer, norm_g, mats, inv_cnt, w_bf16, bias, scale, L, n_batch, row_off, pend=None):
    rb = row_off // L
    lat = row_off > 0
    G = len(POOL_WINDOWS)

    def row_fn(b):
        return 1 + b if lat else 0

    p_specs, p_args = _pending_specs(pend, mod, L, lambda b: (rb + b, 0), row_fn)
    return pl.pallas_call(
        _pool_kernel,
        grid=(n_batch,),
        in_specs=[
            pl.BlockSpec((L, D), lambda b: (rb + b, 0)),
            _vec_spec(D),
            _mod_spec(layer, 1, row_fn),
            _mod_spec(layer, 0, row_fn),
            _mod_spec(layer, 2, row_fn),
            _full_spec((G, L, L)),
            _full_spec((G, L, 1)),
            _full_spec((G, POOL_G, POOL_G)),
            _vec_spec(D),
            _vec_spec(D),
        ] + p_specs,
        out_specs=pl.BlockSpec((L, D), lambda b: (rb + b, 0)),
        out_shape=jax.ShapeDtypeStruct((N_TOK, D), F32),
        input_output_aliases={0: 0},
        compiler_params=_cparams(1),
        name=f"pool_L{L}",
    )(x, norm_g.reshape(1, D), mod, mod, mod, mats, inv_cnt, w_bf16, bias.reshape(1, D), scale.reshape(1, D),
      *p_args)


ROUTER_PAD = 128
LANES = 128
D_EXT = D + LANES
MOE_TILE = 1024
MOE_MAX_TILES = N_TOK // MOE_TILE + MOE_GROUPS
MOE_ROWS = MOE_MAX_TILES * MOE_TILE
MOE_TILE_SHIFT = MOE_TILE.bit_length() - 1
assert 1 << MOE_TILE_SHIFT == MOE_TILE
MOE_ROW_BLOCK = 128
MOE_DMA_CHUNK = 64
MOE_CHUNK_SHIFT = MOE_DMA_CHUNK.bit_length() - 1
assert 1 << MOE_CHUNK_SHIFT == MOE_DMA_CHUNK and MOE_MAX_TILES % 2 == 0
MOE_Y_ROWS = N_TOK + 2 * MOE_DMA_CHUNK


ROUTE_ROWS = 8
TM_ROUTE = TM_BIG


def _moe_route_kernel(x_ref, g_ref, sc_ref, sh_ref, wr_ref, tri_ref, h3_ref, route_ref, cnt_ref, carry):
    t = pl.program_id(0)
    refs = (x_ref, g_ref, sc_ref, sh_ref, wr_ref, tri_ref, h3_ref, route_ref, cnt_ref, carry)
    pl.when(t < N_TOK // TM_ROUTE)(functools.partial(_moe_route_tile, t, *refs))

    @pl.when(t == N_TOK // TM_ROUTE)
    def _():
        h3_ref[...] = jnp.zeros_like(h3_ref)


def _moe_route_tile(t, x_ref, g_ref, sc_ref, sh_ref, wr_ref, tri_ref, h3_ref, route_ref, cnt_ref, carry):
    @pl.when(t == 0)
    def _():
        carry[...] = jnp.zeros_like(carry)

    h = _norm_mod(x_ref[...], g_ref[...], sc_ref[...], sh_ref[...])
    w_hi, w_lo = _split(wr_ref[...])
    h_hi, h_lo = _split(h)
    nt = (((1,), (1,)), ((), ()))
    logits = (lax.dot_general(w_hi, h_hi, nt, preferred_element_type=F32)
              + (lax.dot_general(w_hi, h_lo, nt, preferred_element_type=F32)
                 + lax.dot_general(w_lo, h_hi, nt, preferred_element_type=F32)))
    neg = jnp.float32(-jnp.inf)
    r8 = lax.broadcasted_iota(jnp.int32, (ROUTE_ROWS, TM_ROUTE), 0)
    r16 = lax.broadcasted_iota(jnp.int32, (MOE_E, TM_ROUTE), 0)
    gl = jnp.where(r8 < MOE_GROUPS, logits[MOE_E:MOE_E + ROUTE_ROWS], neg)
    g_max = jnp.max(gl, axis=0, keepdims=True)
    g_idx = jnp.min(jnp.where(gl == g_max, r8, ROUTE_ROWS), axis=0, keepdims=True)
    p_grp = 1.0 / jnp.sum(jnp.exp(gl - g_max), axis=0, keepdims=True)
    in_grp = (r16 >> 2) == g_idx
    el = jnp.where(in_grp, logits[:MOE_E], neg)
    m1 = jnp.max(el, axis=0, keepdims=True)
    i1 = jnp.min(jnp.where(el == m1, r16, MOE_E), axis=0, keepdims=True)
    z = jnp.sum(jnp.exp(el - m1), axis=0, keepdims=True)
    el2 = jnp.where(r16 == i1, neg, el)
    m2 = jnp.max(el2, axis=0, keepdims=True)
    i2 = jnp.min(jnp.where(el2 == m2, r16, MOE_E), axis=0, keepdims=True)
    p1 = 1.0 / z
    p2 = jnp.exp(m2 - m1) / z
    tot = p1 + p2
    eid = r8 + MOE_PER_GROUP * g_idx
    in4 = r8 < MOE_PER_GROUP
    cw4 = (jnp.where(in4 & (eid == i1), p_grp * (p1 / tot), 0.0)
           + jnp.where(in4 & (eid == i2), p_grp * (p2 / tot), 0.0))
    member = jnp.where(r8 == g_idx, 1.0, 0.0)
    before = _dot(member.astype(BF16), tri_ref[...]) + carry[:, 0:1]
    rank = jnp.sum(jnp.where(r8 == g_idx, before, 0.0), axis=0, keepdims=True)
    carry[...] = carry[...] + jnp.sum(member, axis=1, keepdims=True)
    cnt_ref[...] = carry[...].astype(jnp.int32)
    route_ref[...] = jnp.where(r8 == 0, g_idx, jnp.where(r8 == 1, rank.astype(jnp.int32), 0))
    h3_ref[:, :D] = h
    cw_rows = jnp.concatenate([cw4, jnp.zeros((LANES - ROUTE_ROWS, TM_ROUTE), F32)], axis=0)
    h3_ref[:, D:] = cw_rows.T


def _moe_route(x, mod, layer, norm_g, w_router_t, tri):
    nt = N_TOK // TM_ROUTE

    def tok_tile(t):
        return jnp.minimum(t, nt - 1)

    return pl.pallas_call(
        _moe_route_kernel,
        grid=(nt + 1,),
        in_specs=[
            pl.BlockSpec((TM_ROUTE, D), lambda t: (tok_tile(t), 0)),
            _vec_spec(D),
            _mod_spec(layer, 4, lambda t: _row_big(tok_tile(t))),
            _mod_spec(layer, 3, lambda t: _row_big(tok_tile(t))),
            _full_spec((ROUTER_PAD, D)),
            _full_spec((TM_ROUTE, TM_ROUTE)),
        ],
        out_specs=[
            pl.BlockSpec((TM_ROUTE, D_EXT), lambda t: (t, 0)),
            pl.BlockSpec((ROUTE_ROWS, TM_ROUTE), lambda t: (0, tok_tile(t))),
            pl.BlockSpec((ROUTE_ROWS, LANES), lambda t: (0, 0)),
        ],
        out_shape=[
            jax.ShapeDtypeStruct((N_TOK + TM_ROUTE, D_EXT), F32),
            jax.ShapeDtypeStruct((ROUTE_ROWS, N_TOK), jnp.int32),
            jax.ShapeDtypeStruct((ROUTE_ROWS, LANES), jnp.int32),
        ],
        scratch_shapes=[pltpu.VMEM((ROUTE_ROWS, LANES), F32)],
        compiler_params=_cparams(1),
        name="moe_route",
    )(x, norm_g.reshape(1, D), mod, mod, w_router_t, tri)


def _moe_invert_kernel(pos_ref, lo_ref, hi_ref, src_ref):
    def mark(j, carry):
        parity = lax.shift_right_logical(j, jnp.int32(MOE_TILE_SHIFT - MOE_CHUNK_SHIFT)) & 1
        first = N_TOK + MOE_DMA_CHUNK * parity
        base = j * MOE_DMA_CHUNK
        for rr in range(MOE_DMA_CHUNK):
            src_ref[base + rr] = first + rr
        return carry

    for g in range(MOE_GROUPS + 1):
        lax.fori_loop(lo_ref[g], hi_ref[g], mark, 0)

    def place(n, carry):
        src_ref[pos_ref[n]] = n
        return carry

    lax.fori_loop(0, N_TOK, place, 0, unroll=16)


def _moe_invert(pos, mark_lo, mark_hi):
    smem = pl.BlockSpec(memory_space=pltpu.SMEM)
    return pl.pallas_call(
        _moe_invert_kernel,
        in_specs=[smem, smem, smem],
        out_specs=smem,
        out_shape=jax.ShapeDtypeStruct((MOE_ROWS,), jnp.int32),
        name="moe_invert",
    )(pos, mark_lo, mark_hi)


def _moe_expert_kernel(src_ref, grp_ref, nact_ref, nchunk_ref, h_hbm, w1_ref, w3_ref, w2_ref, y_hbm,
                       xb0, xb1, ab0, ab1, gsem, ssem):
    t = pl.program_id(0)
    k = pl.program_id(1)
    n_active = nact_ref[0]
    last = n_active - 1
    T = MOE_TILE
    CH = MOE_DMA_CHUNK
    xbufs, accs = (xb0, xb1), (ab0, ab1)

    def gather_row(tile, slot, base, rr):
        tok = src_ref[tile * T + base + rr]
        rows = xbufs[slot].at[pl.ds(base, CH), :]
        return pltpu.make_async_copy(h_hbm.at[pl.ds(tok, 1), :], rows.at[pl.ds(rr, 1), :], gsem.at[slot])

    def scatter_row(tile, slot, base, rr):
        dst = src_ref[tile * T + base + rr]
        rows = accs[slot].at[pl.ds(base, CH), :]
        return pltpu.make_async_copy(rows.at[pl.ds(rr, 1), :], y_hbm.at[pl.ds(dst, 1), :], ssem.at[slot])

    def start_rows(make, tile, slot):
        def chunk(c, carry):
            base = pl.multiple_of(c * CH, CH)
            for rr in range(CH):
                make(tile, slot, base, rr).start()
            return carry

        lax.fori_loop(0, nchunk_ref[tile], chunk, 0)

    def wait_rows(tile, slot, gather):
        def chunk(c, carry):
            if gather:
                pltpu.make_async_copy(h_hbm.at[pl.ds(0, CH), :], xbufs[slot].at[pl.ds(0, CH), :], gsem.at[slot]).wait()
            else:
                pltpu.make_async_copy(accs[slot].at[pl.ds(0, CH), :], y_hbm.at[pl.ds(0, CH), :], ssem.at[slot]).wait()
            return carry

        lax.fori_loop(0, nchunk_ref[tile], chunk, 0)

    def step(slot):
        other = 1 - slot
        xb, acc = xbufs[slot], accs[slot]

        @pl.when(k == 0)
        def _():
            if slot == 0:
                @pl.when(t == 0)
                def _():
                    xb0[...] = jnp.zeros_like(xb0)
                    xb1[...] = jnp.zeros_like(xb1)
                    ab0[...] = jnp.zeros_like(ab0)
                    ab1[...] = jnp.zeros_like(ab1)
                    dump = pltpu.make_async_copy(ab0.at[pl.ds(0, 2 * CH), :], y_hbm.at[pl.ds(N_TOK, 2 * CH), :],
                                                 ssem.at[0])
                    dump.start()
                    dump.wait()
                    start_rows(gather_row, 0, 0)

            wait_rows(t, slot, True)

            @pl.when(t < last)
            def _():
                start_rows(gather_row, t + 1, other)

            @pl.when(t >= 2)
            def _():
                wait_rows(t - 2, slot, False)

        def experts(m):
            x = xb[:m, :D].astype(BF16)
            a = _dot(x, w1_ref[...].astype(BF16))
            b = _dot(x, w3_ref[...].astype(BF16))
            lane = lax.broadcasted_iota(jnp.int32, (m, LANES), 1)
            cwk = jnp.sum(jnp.where(lane == k, xb[:m, D:], 0.0), axis=-1, keepdims=True)
            hid = (_silu(a) * b * cwk).astype(BF16)
            acc[:m] = jnp.where(k > 0, acc[:m], 0.0) + _dot(hid, w2_ref[...].astype(BF16))

        blocks = (nchunk_ref[t] * CH + MOE_ROW_BLOCK - 1) // MOE_ROW_BLOCK
        for nb in range(1, T // MOE_ROW_BLOCK + 1):
            pl.when(blocks == nb)(functools.partial(experts, nb * MOE_ROW_BLOCK))

        @pl.when(k == MOE_PER_GROUP - 1)
        def _():
            start_rows(scatter_row, t, slot)

            @pl.when(t == last)
            def _():
                wait_rows(t, slot, False)

                @pl.when(t >= 1)
                def _():
                    wait_rows(t - 1, other, False)

    for slot in (0, 1):
        pl.when((t < n_active) & (t % 2 == slot))(functools.partial(step, slot))


def _moe_experts(h_ext, src, tile_group, n_active, n_chunk, layer, w1, w3, w2):
    T = MOE_TILE

    def w_index(t, k, src_ref, grp_ref, nact_ref, nchunk_ref):
        last = nact_ref[0] - 1
        e = jnp.where(t <= last, grp_ref[t] * MOE_PER_GROUP + k, grp_ref[last] * MOE_PER_GROUP + MOE_PER_GROUP - 1)
        return (layer, e, 0, 0)

    grid_spec = pltpu.PrefetchScalarGridSpec(
        num_scalar_prefetch=4,
        grid=(MOE_MAX_TILES, MOE_PER_GROUP),
        in_specs=[
            pl.BlockSpec(memory_space=pl.ANY),
            pl.BlockSpec((None, None, D, MOE_HID), w_index),
            pl.BlockSpec((None, None, D, MOE_HID), w_index),
            pl.BlockSpec((None, None, MOE_HID, D), w_index),
        ],
        out_specs=pl.BlockSpec(memory_space=pl.ANY),
        scratch_shapes=[
            pltpu.VMEM((T, D_EXT), F32),
            pltpu.VMEM((T, D_EXT), F32),
            pltpu.VMEM((T, D), F32),
            pltpu.VMEM((T, D), F32),
            pltpu.SemaphoreType.DMA((2,)),
            pltpu.SemaphoreType.DMA((2,)),
        ],
    )
    return pl.pallas_call(
        _moe_expert_kernel,
        grid_spec=grid_spec,
        out_shape=jax.ShapeDtypeStruct((MOE_Y_ROWS, D), F32),
        compiler_params=_cparams(2),
        name="moe_experts",
    )(src, tile_group, n_active, n_chunk, h_ext, w1, w3, w2)


def _moe_combine_kernel(y_ref, gate_ref, x_ref, o_ref):
    o_ref[...] = x_ref[...] + gate_ref[...] * y_ref[...]


def _moe_combine(x, y3, mod, layer):
    return pl.pallas_call(
        _moe_combine_kernel,
        grid=(N_TOK // TM,),
        in_specs=[
            pl.BlockSpec((TM, D), lambda t: (t, 0)),
            _mod_spec(layer, 5, _row_tm),
            pl.BlockSpec((TM, D), lambda t: (t, 0)),
        ],
        out_specs=pl.BlockSpec((TM, D), lambda t: (t, 0)),
        out_shape=jax.ShapeDtypeStruct((N_TOK, D), F32),
        input_output_aliases={2: 0},
        compiler_params=_cparams(1),
        name="moe_combine",
    )(y3, mod, x)


def _moe(x, mod, layer, norm_g, w_rg, w_re, w1, w3, w2, tri, final_g=None):
    w_router_t = jnp.zeros((ROUTER_PAD, D), F32).at[:MOE_E].set(w_re.T).at[MOE_E:MOE_E + MOE_GROUPS].set(w_rg.T)
    h3, route, counts = _moe_route(x, mod, layer, norm_g, w_router_t, tri)
    cnt = counts[:MOE_GROUPS, 0]
    ntile = (cnt + MOE_TILE - 1) // MOE_TILE
    tile_end = jnp.cumsum(ntile)
    seg_start = (tile_end - ntile) * MOE_TILE
    g_idx, rank = route[0], route[1]
    pos = jnp.sum(jnp.where(g_idx[None, :] == jnp.arange(MOE_GROUPS)[:, None], seg_start[:, None], 0), axis=0) + rank
    tiles = jnp.arange(MOE_MAX_TILES, dtype=jnp.int32)
    tile_group = jnp.minimum(jnp.sum(tiles[:, None] >= tile_end[None, :], axis=1), MOE_GROUPS - 1).astype(jnp.int32)
    n_active = tile_end[-1:].astype(jnp.int32)
    seg_end = tile_end * MOE_TILE
    mark_lo = jnp.concatenate([(seg_start + cnt) // MOE_DMA_CHUNK, seg_end[-1:] // MOE_DMA_CHUNK])
    mark_hi = jnp.concatenate([seg_end // MOE_DMA_CHUNK, jnp.full((1,), MOE_ROWS // MOE_DMA_CHUNK)])
    src = _moe_invert(pos.astype(jnp.int32), mark_lo.astype(jnp.int32), mark_hi.astype(jnp.int32))
    first_tile = (tile_end - ntile)[tile_group]
    real_rows = jnp.clip(cnt[tile_group] - (tiles - first_tile) * MOE_TILE, 0, MOE_TILE)
    n_chunk = ((real_rows + MOE_DMA_CHUNK - 1) // MOE_DMA_CHUNK).astype(jnp.int32)
    y3 = _moe_experts(h3, src, tile_group, n_active, n_chunk, layer, w1, w3, w2)
    if final_g is None:
        return y3
    return tuple(_moe_combine_norm(x, y3, mod, layer, final_g, off, nb * L) for L, nb, off in
                 ((CTX_L, CTX_B, 0), (LAT_L, LAT_B, N_CTX)))


def _combine_norm_kernel(y_ref, gate_ref, x_ref, g_ref, o_ref):
    x = x_ref[...] + gate_ref[...] * y_ref[...]
    o_ref[...] = x * lax.rsqrt(jnp.mean(x * x, axis=-1, keepdims=True) + EPS) * g_ref[...]


def _moe_combine_norm(x, y3, mod, layer, final_g, row_off, n_rows):
    off = row_off // TM_BIG
    first_lat = N_CTX // TM_BIG

    def row_fn(t):
        g = t + off
        return jnp.where(g < first_lat, 0, 1 + (g - first_lat) // (LAT_L // TM_BIG))

    return pl.pallas_call(
        _combine_norm_kernel,
        grid=(n_rows // TM_BIG,),
        in_specs=[
            pl.BlockSpec((TM_BIG, D), lambda t: (t + off, 0)),
            _mod_spec(layer, 5, row_fn),
            pl.BlockSpec((TM_BIG, D), lambda t: (t + off, 0)),
            _vec_spec(D),
        ],
        out_specs=pl.BlockSpec((TM_BIG, D), lambda t: (t, 0)),
        out_shape=jax.ShapeDtypeStruct((n_rows, D), F32),
        compiler_params=_cparams(1),
        name="combine_final_norm",
    )(y3, mod, x, final_g.reshape(1, D))


def kernel(x_prompt, x_sample, state_gla, c, c_ctx, w_ada, b_ada, norm_g, hy_w_in, hy_b_in, hy_conv_w, hy_conv_b, hy_f_w1, hy_f_b1, hy_f_freq, hy_f_w2, hy_f_b2, hy_f_w3, hy_skip, hy_w_out, hy_b_out, gla_w_q, gla_w_k, gla_w_v, gla_w_g, gla_w_gk1, gla_w_gk2, gla_b_gk, gla_norm_g, gla_w_o, fn_w_out, fn_b_out, pool_w, pool_b, pool_scale, moe_w_rg, moe_w_re, moe_w1, moe_w3, moe_w2, final_g):
    groups = ((CTX_L, CTX_B, 0, None), (LAT_L, LAT_B, N_CTX, LAT_L // GRID_W))

    x_ctx, x_lat = x_prompt.reshape(N_CTX, D), x_sample.reshape(N_LAT, D)
    x = None
    cond =jnp.zeros((MOD_ROWS, D), F32).at[0].set(c_ctx).at[1:1 + LAT_B].set(c)
    mod = _ada_table(cond, w_ada, b_ada).reshape(DEPTH * MOD_ROWS * 6, 1, D)

    tri_tm = jnp.asarray(np.triu(np.ones((TM_ROUTE, TM_ROUTE)), 1), BF16)

    new_states = []
    pend = None
    for i in range(DEPTH):
        kind, j = i % 4, i // 4
        if kind == 0:
            if i > 0:
                x = _moe_combine(x, pend[0], mod, pend[1])
                x_ctx, x_lat = x[:N_CTX], x[N_CTX:]
            u = _hyena_in(x_ctx, x_lat, mod, i, norm_g[i, 0], hy_w_in[j].astype(BF16), hy_b_in[j], hy_conv_w[j],
                          hy_conv_b[j])
            zs = []
            for L, nb, off, _ in groups:
                fwd, ff, inv = _dft_mats(L)
                khat = _hyena_filters(L, ff, hy_f_w1[j], hy_f_b1[j], hy_f_freq[j], hy_f_w2[j], hy_f_b2[j],
                                      hy_f_w3[j])
                zs.append(_hyena_conv(u, khat, hy_skip[j], fwd, inv, L, nb, off, D if L == CTX_L else 512))
            x = _outproj_joint(x_ctx, x_lat, zs[0], zs[1], hy_w_out[j].astype(BF16), hy_b_out[j], mod, i)
        elif kind == 1:
            w_cat = jnp.concatenate([gla_w_q[j], gla_w_k[j], gla_w_v[j], gla_w_g[j]], axis=1).astype(BF16)
            nk = GLA_H * GLA_DK
            wg1 = jnp.zeros((D, GK1_PAD), F32).at[:, :GLA_RANK].set(gla_w_gk1[j, 0])
            wg1 = wg1.at[:, GLA_RANK:2 * GLA_RANK].set(gla_w_gk1[j, 1]).astype(BF16)
            wg2 = jnp.zeros((GK1_PAD, 2 * nk), F32).at[:GLA_RANK, :nk].set(gla_w_gk2[j, 0])
            wg2 = wg2.at[GLA_RANK:2 * GLA_RANK, nk:].set(gla_w_gk2[j, 1]).astype(BF16)
            proj = _gla_proj(x, mod, i, norm_g[i, 0], w_cat, wg1, wg2, gla_b_gk[j].reshape(1, 2 * nk), pend)
            lower = np.tril(np.ones((GLA_CHUNK, GLA_CHUNK)))
            w_o = gla_w_o[j].astype(BF16)
            for L, nb, off, grid_rows in groups:
                eye = np.eye(L // GLA_CHUNK if L // GLA_CHUNK <= GLA_WHOLE_SEQ_CHUNKS else 1)
                tri = jnp.asarray(np.stack([np.kron(eye, lower), np.kron(eye, lower.T)]), BF16)
                s0 = None if grid_rows is None else state_gla[:, j]
                if L == CTX_L:
                    x, s_fin = _gla_core(proj, tri, gla_norm_g[j], s0, L, nb, off, GLA_H, (x, w_o, mod, i, pend))
                else:
                    o, s_fin = _gla_core(proj, tri, gla_norm_g[j], s0, L, nb, off, 1)
                    x = _outproj(x, o, w_o, jnp.zeros((D,), F32), mod, i, off, nb * L, pend)
                if grid_rows is None:
                    new_states.append(s_fin)
        elif kind == 2:
            w_out = fn_w_out[j].astype(BF16)
            for L, nb, off, _ in groups:
                chan, seq = _fnet_mats(L)
                x = _fnet(x, mod, i, norm_g[i, 0], chan, seq, w_out, fn_b_out[j], L, nb, off, pend)
        else:
            w_pool = pool_w[j].astype(BF16)
            for L, nb, off, grid_rows in groups:
                mats, inv_cnt = _pool_mats(L, grid_rows)
                x = _pool(x, mod, i, norm_g[i, 0], mats, inv_cnt, w_pool, pool_b[j], pool_scale[j], L, nb, off,
                          pend)

        out = _moe(x, mod, i, norm_g[i, 1], moe_w_rg[i], moe_w_re[i], moe_w1, moe_w3, moe_w2, tri_tm,
                   final_g if i == DEPTH - 1 else None)
        pend = (out, i)

    y_prompt, y_sample = out
    new_state_gla = jnp.stack(new_states, axis=1)
    return (y_prompt.reshape(CTX_B, CTX_L, D), y_sample.reshape(LAT_B, LAT_L, D), new_state_gla)
```

```python
import functools
import math

import jax
import jax.numpy as jnp
import numpy as np
from jax import lax
from jax.experimental import pallas as pl
from jax.experimental.pallas import tpu as pltpu

F32 = jnp.float32
BF16 = jnp.bfloat16

D = 1024
CTX_B, CTX_L = 32, 256
LAT_B, LAT_L = 2, 1024
N_CTX = CTX_B * CTX_L
N_LAT = LAT_B * LAT_L
N_TOK = N_CTX + N_LAT
DEPTH = 4
GRID_W = 64
EPS = 1e-6

HY_BANDS = 8
HY_EMB = 1 + 2 * HY_BANDS
HY_EMB_PAD = 32
HY_HID = 64
HY_FAST_DECAY = 0.3
HY_SLOW_DECAY = 1.5
HY_DECAY_TARGET = 1e-2

GLA_H = 4
GLA_DK = 128
GLA_DV = 256
GLA_RANK = 16
GLA_NORMALIZER = 16.0
GLA_CHUNK = 64
GLA_WHOLE_SEQ_CHUNKS = 4

FNET_GROUPS = 4
FNET_C = D // FNET_GROUPS
POOL_WINDOWS = (2, 4, 8, 16)
POOL_G = D // len(POOL_WINDOWS)

MOE_GROUPS = 4
MOE_PER_GROUP = 4
MOE_E = MOE_GROUPS * MOE_PER_GROUP
MOE_HID = D // 2

MOD_ROWS = 8
TM = 512
TM_BIG = 1024
V7X_VMEM_BYTES = 64 * 1024 * 1024
VMEM_LIMIT = V7X_VMEM_BYTES - 8 * 1024 * 1024


def _cparams(n_axes):
    return pltpu.CompilerParams(dimension_semantics=("arbitrary",) * n_axes, vmem_limit_bytes=VMEM_LIMIT)


def _norm_mod(x, g, sc, sh):
    ms = jnp.mean(x * x, axis=-1, keepdims=True)
    return (x * lax.rsqrt(ms + EPS) * g) * (1.0 + sc) + sh


def _split(a):
    hi = a.astype(BF16)
    lo = (a - hi.astype(F32)).astype(BF16)
    return hi, lo


def _dot(a, b):
    return jnp.dot(a, b, preferred_element_type=F32)


def _dot_precise(a, b):
    a_hi, a_lo = _split(a)
    b_hi, b_lo = _split(b)
    return _dot(a_hi, b_hi) + (_dot(a_hi, b_lo) + _dot(a_lo, b_hi))


def _silu(x):
    return x * (1.0 / (1.0 + jnp.exp(-x)))


def _log_sigmoid(x):
    return jnp.minimum(x, 0.0) - jnp.log(1.0 + jnp.exp(-jnp.abs(x)))


def _mod_spec(layer, chunk, row_fn):
    base = layer * MOD_ROWS * 6 + chunk

    def index_map(*ids):
        return (base + row_fn(*ids) * 6, 0, 0)

    return pl.BlockSpec((None, 1, D), index_map)


def _row_tm(t, *_):
    return jnp.where(t < N_CTX // TM, 0, 1 + (t - N_CTX // TM) // (LAT_L // TM))


def _row_big(t, *_):
    return jnp.where(t < N_CTX // TM_BIG, 0, 1 + (t - N_CTX // TM_BIG) // (LAT_L // TM_BIG))


def _vec_spec(n):
    return pl.BlockSpec((1, n), lambda *ids: (0, 0))


def _full_spec(shape):
    nd = len(shape)
    return pl.BlockSpec(shape, lambda *ids: (0,) * nd)


def _ada_kernel(cond_ref, w_ref, b_ref, o_ref):
    s = _silu(cond_ref[...]).astype(BF16)
    o_ref[...] = _dot(s, w_ref[...].astype(BF16)) + b_ref[...]


def _ada_table(cond, w_ada, b_ada):
    tn = 1536
    return pl.pallas_call(
        _ada_kernel,
        grid=(DEPTH, 6 * D // tn),
        in_specs=[
            pl.BlockSpec((MOD_ROWS, D), lambda i, j: (0, 0)),
            pl.BlockSpec((None, D, tn), lambda i, j: (i, 0, j)),
            pl.BlockSpec((None, 1, tn), lambda i, j: (i, 0, j)),
        ],
        out_specs=pl.BlockSpec((None, MOD_ROWS, tn), lambda i, j: (i, 0, j)),
        out_shape=jax.ShapeDtypeStruct((DEPTH, MOD_ROWS, 6 * D), F32),
        compiler_params=_cparams(2),
        name="ada_table",
    )(cond, w_ada, b_ada.reshape(DEPTH, 1, 6 * D))


def _pending_specs(pend, mod, block_rows, row_index, row_fn):
    if pend is None:
        return [], []
    y, prev_layer = pend
    return [pl.BlockSpec((block_rows, D), row_index), _mod_spec(prev_layer, 5, row_fn)], [y, mod]


def _read_x(x_ref, pending_refs):
    x = x_ref[...]
    if pending_refs:
        y_ref, gate_ref = pending_refs
        x = x + gate_ref[...] * y_ref[...]
    return x


def _outproj_kernel(z_ref, w_ref, b_ref, gate_ref, x_ref, *rest):
    *pending, o_ref = rest
    y = _dot(z_ref[...], w_ref[...]) + b_ref[...]
    o_ref[...] = _read_x(x_ref, pending) + gate_ref[...] * y


def _outproj(x, z, w_bf16, bias, mod, layer, row_off, n_rows, pend=None):
    k = z.shape[1]
    off = row_off // TM
    first_lat = N_CTX // TM

    def row_fn(t):
        g = t + off
        return jnp.where(g < first_lat, 0, 1 + (g - first_lat) // (LAT_L // TM))

    p_specs, p_args = _pending_specs(pend, mod, TM, lambda t: (t + off, 0), row_fn)
    return pl.pallas_call(
        _outproj_kernel,
        grid=(n_rows // TM,),
        in_specs=[
            pl.BlockSpec((TM, k), lambda t: (t, 0)),
            _full_spec((k, D)),
            _vec_spec(D),
            _mod_spec(layer, 2, row_fn),
            pl.BlockSpec((TM, D), lambda t: (t + off, 0)),
        ] + p_specs,
        out_specs=pl.BlockSpec((TM, D), lambda t: (t + off, 0)),
        out_shape=jax.ShapeDtypeStruct((N_TOK, D), F32),
        input_output_aliases={4: 0},
        compiler_params=_cparams(1),
        name="outproj_residual",
    )(z, w_bf16, bias.reshape(1, D), mod, x, *p_args)


def _outproj_joint_kernel(zc_ref, zl_ref, w_ref, b_ref, gate_ref, xc_ref, xl_ref, o_ref):
    t = pl.program_id(0)
    for is_ctx, z_ref, x_ref in ((True, zc_ref, xc_ref), (False, zl_ref, xl_ref)):
        @pl.when((t < N_CTX // TM_BIG) == is_ctx)
        def _(z_ref=z_ref, x_ref=x_ref):
            y = _dot(z_ref[...], w_ref[...]) + b_ref[...]
            o_ref[...] = x_ref[...] + gate_ref[...] * y


def _outproj_joint(x_ctx, x_lat, z_ctx, z_lat, w_bf16, bias, mod, layer):
    k = z_ctx.shape[1]
    n_ctx_tiles = N_CTX // TM_BIG

    def ctx_block(t):
        return (jnp.minimum(t, n_ctx_tiles - 1), 0)

    def lat_block(t):
        return (jnp.maximum(t - n_ctx_tiles, 0), 0)

    return pl.pallas_call(
        _outproj_joint_kernel,
        grid=(N_TOK // TM_BIG,),
        in_specs=[
            pl.BlockSpec((TM_BIG, k), ctx_block),
            pl.BlockSpec((TM_BIG, k), lat_block),
            _full_spec((k, D)),
            _vec_spec(D),
            _mod_spec(layer, 2, _row_big),
            pl.BlockSpec((TM_BIG, D), ctx_block),
            pl.BlockSpec((TM_BIG, D), lat_block),
        ],
        out_specs=pl.BlockSpec((TM_BIG, D), lambda t: (t, 0)),
        out_shape=jax.ShapeDtypeStruct((N_TOK, D), F32),
        compiler_params=_cparams(1),
        name="outproj_joint",
    )(z_ctx, z_lat, w_bf16, bias.reshape(1, D), mod, x_ctx, x_lat)


def _dft_mats(L):
    n2 = 2 * L
    k = np.arange(L)[:, None].astype(np.float64)
    n = np.arange(n2)[None, :].astype(np.float64)
    ang = 2.0 * np.pi * k * n / n2
    full = np.concatenate([np.cos(ang), -np.sin(ang)], axis=0)
    full[L, :] = np.cos(np.pi * np.arange(n2))
    fwd = full[:, :L]
    bwd = np.zeros((n2, L))
    bwd[:, 1:] = full[:, n2 - np.arange(1, L)]
    t = np.arange(L)[:, None].astype(np.float64)
    kk = np.arange(L)[None, :].astype(np.float64)
    ang_i = 2.0 * np.pi * t * kk / n2
    inv_re = np.cos(ang_i) / L
    inv_re[:, 0] = 1.0 / n2
    inv_im = -np.sin(ang_i) / L
    inv_im[:, 0] = np.cos(np.pi * np.arange(L)) / n2
    inv = np.concatenate([inv_re, inv_im], axis=1)
    return tuple(jnp.asarray(m, F32).astype(BF16) for m in (fwd, np.concatenate([fwd, bwd], axis=1), inv))


def _hyena_pos_emb(L):
    pos = np.arange(L, dtype=np.float64)
    bands = np.linspace(1e-4, HY_BANDS - 1, HY_BANDS)
    ang = (2.0 * np.pi * pos / L)[:, None] * bands[None, :]
    z = np.concatenate([(pos / L)[:, None], np.cos(ang), -np.sin(ang)], axis=-1)
    zp = np.zeros((L, HY_EMB_PAD))
    zp[:, :HY_EMB] = z
    return jnp.asarray(zp, F32)


def _hyena_filter_kernel(z_ref, w1_ref, b1_ref, fr_ref, w2_ref, b2_ref, w3f_ref, w3b_ref, ff_ref, o_ref, *, L, tn):
    j = pl.program_id(1)
    fr = fr_ref[...]
    f = jnp.sin(fr * (_dot_precise(z_ref[...], w1_ref[...]) + b1_ref[...]))
    f = jnp.sin(fr * (_dot_precise(f, w2_ref[...]) + b2_ref[...]))
    t_lin = lax.broadcasted_iota(jnp.int32, (L, tn), 0).astype(F32) / float(L - 1)
    ch = (lax.broadcasted_iota(jnp.int32, (L, tn), 1) + j * tn).astype(F32)
    max_decay = math.log(HY_DECAY_TARGET) / HY_FAST_DECAY
    min_decay = math.log(HY_DECAY_TARGET) / HY_SLOW_DECAY
    deltas = min_decay + ch * ((max_decay - min_decay) / float(D - 1))
    window = jnp.exp(-t_lin * jnp.abs(deltas))
    kf = _dot_precise(f, w3f_ref[...]) * window
    kb = _dot_precise(f, w3b_ref[...]) * window
    taps = jnp.concatenate([kf, kb], axis=0).astype(BF16)
    o_ref[...] = _dot(ff_ref[...], taps)


def _hyena_filters(L, ff, f_w1, f_b1, f_freq, f_w2, f_b2, f_w3):
    tn = 512
    nj = D // tn
    w1p = jnp.zeros((HY_EMB_PAD, HY_HID), F32).at[:HY_EMB].set(f_w1)
    kern = functools.partial(_hyena_filter_kernel, L=L, tn=tn)
    return pl.pallas_call(
        kern,
        grid=(2, nj),
        in_specs=[
            _full_spec((L, HY_EMB_PAD)),
            _full_spec((HY_EMB_PAD, HY_HID)),
            _vec_spec(HY_HID),
            _vec_spec(HY_HID),
            _full_spec((HY_HID, HY_HID)),
            _vec_spec(HY_HID),
            pl.BlockSpec((HY_HID, tn), lambda o, j: (0, o * nj + j)),
            pl.BlockSpec((HY_HID, tn), lambda o, j: (0, (2 + o) * nj + j)),
            _full_spec((2 * L, 2 * L)),
        ],
        out_specs=pl.BlockSpec((None, 2 * L, tn), lambda o, j: (o, 0, j)),
        out_shape=jax.ShapeDtypeStruct((2, 2 * L, D), F32),
        compiler_params=_cparams(2),
        name=f"hyena_filters_L{L}",
    )(_hyena_pos_emb(L), w1p, f_b1.reshape(1, -1), f_freq.reshape(1, -1), f_w2, f_b2.reshape(1, -1),
      f_w3, f_w3, ff)


def _hyena_in_kernel(xc_ref, xl_ref, g_ref, sc_ref, sh_ref, w_ref, b_ref, cw_ref, cb_ref, o_ref, h_scr):
    t = pl.program_id(0)
    first = pl.program_id(1) == 0

    for is_ctx, x_ref in ((True, xc_ref), (False, xl_ref)):
        @pl.when(first & ((t < N_CTX // TM_BIG) == is_ctx))
        def _(x_ref=x_ref):
            h_scr[...] = _norm_mod(x_ref[...], g_ref[...], sc_ref[...], sh_ref[...]).astype(BF16)

    u = _dot(h_scr[...], w_ref[...]) + b_ref[...]
    cw = cw_ref[...]
    o_ref[...] = pltpu.roll(u, 1, 0) * cw[0:1] + u * cw[1:2] + pltpu.roll(u, TM_BIG - 1, 0) * cw[2:3] + cb_ref[...]
    is_ctx = (t < N_CTX // TM_BIG).astype(F32)
    for start in range(0, TM_BIG, CTX_L):
        f = 1.0 if start % LAT_L == 0 else is_ctx
        before = (start - 1) % TM_BIG
        o_ref[start:start + 1, :] = o_ref[start:start + 1, :] - f * (u[before:before + 1] * cw[0:1])
        end = start + CTX_L - 1
        g = 1.0 if (end + 1) % LAT_L == 0 else is_ctx
        after = (end + 1) % TM_BIG
        o_ref[end:end + 1, :] = o_ref[end:end + 1, :] - g * (u[after:after + 1] * cw[2:3])


def _hyena_in(x_ctx, x_lat, mod, layer, norm_g, w_in_bf16, b_in, conv_w, conv_b):
    n_ctx_tiles = N_CTX // TM_BIG
    return pl.pallas_call(
        _hyena_in_kernel,
        grid=(N_TOK // TM_BIG, 3),
        in_specs=[
            pl.BlockSpec((TM_BIG, D), lambda t, p: (jnp.minimum(t, n_ctx_tiles - 1), 0)),
            pl.BlockSpec((TM_BIG, D), lambda t, p: (jnp.maximum(t - n_ctx_tiles, 0), 0)),
            _vec_spec(D),
            _mod_spec(layer, 1, _row_big),
            _mod_spec(layer, 0, _row_big),
            pl.BlockSpec((D, D), lambda t, p: (0, p)),
            pl.BlockSpec((1, D), lambda t, p: (0, p)),
            pl.BlockSpec((3, D), lambda t, p: (0, p)),
            pl.BlockSpec((1, D), lambda t, p: (0, p)),
        ],
        out_specs=pl.BlockSpec((TM_BIG, D), lambda t, p: (t, p)),
        out_shape=jax.ShapeDtypeStruct((N_TOK, 3 * D), F32),
        scratch_shapes=[pltpu.VMEM((TM_BIG, D), BF16)],
        compiler_params=_cparams(2),
        name="hyena_in",
    )(x_ctx, x_lat, norm_g.reshape(1, D), mod, mod, w_in_bf16, b_in.reshape(1, -1), conv_w, conv_b.reshape(1, -1))


def _hyena_conv_kernel(v_ref, x1_ref, x2_ref, kh_ref, skip_ref, fwd_ref, inv_ref, o_ref, *, L):
    fwd = fwd_ref[...]
    inv = inv_ref[...]
    row0 = lax.broadcasted_iota(jnp.int32, (L, v_ref.shape[1]), 0) == 0

    def long_conv(z, order):
        zh = _dot(fwd, z.astype(BF16))
        zr, zi = zh[:L], zh[L:]
        kr, ki = kh_ref[order, :L, :], kh_ref[order, L:, :]
        pr = jnp.where(row0, zr * kr, zr * kr - zi * ki)
        pi = jnp.where(row0, zi * ki, zr * ki + zi * kr)
        prod = jnp.concatenate([pr, pi], axis=0).astype(BF16)
        return _dot(inv, prod) + z * skip_ref[order:order + 1, :]

    z = x1_ref[...] * long_conv(v_ref[...], 0)
    z = x2_ref[...] * long_conv(z, 1)
    o_ref[...] = z.astype(BF16)


def _hyena_conv(u, khat, skip, fwd, inv, L, n_batch, row_off, tn):
    nj = D // tn
    rb = row_off // L
    kern = functools.partial(_hyena_conv_kernel, L=L)
    return pl.pallas_call(
        kern,
        grid=(nj, n_batch),
        in_specs=[
            pl.BlockSpec((L, tn), lambda j, b: (rb + b, j)),
            pl.BlockSpec((L, tn), lambda j, b: (rb + b, nj + j)),
            pl.BlockSpec((L, tn), lambda j, b: (rb + b, 2 * nj + j)),
            pl.BlockSpec((2, 2 * L, tn), lambda j, b: (0, 0, j)),
            pl.BlockSpec((2, tn), lambda j, b: (0, j)),
            _full_spec((2 * L, L)),
            _full_spec((L, 2 * L)),
        ],
        out_specs=pl.BlockSpec((L, tn), lambda j, b: (b, j)),
        out_shape=jax.ShapeDtypeStruct((n_batch * L, D), BF16),
        compiler_params=_cparams(2),
        name=f"hyena_conv_L{L}",
    )(u, u, u, khat, skip, fwd, inv)


GLA_PROJ = 2 * GLA_H * GLA_DK + 2 * GLA_H * GLA_DV
GLA_COLS = GLA_PROJ + 2 * GLA_H * GLA_DK
GK1_PAD = 128


def _gla_proj_kernel(x_ref, g_ref, sc_ref, sh_ref, w_ref, wg1_ref, wg2_ref, bg_ref, *rest):
    *pending, o_ref = rest
    h = _norm_mod(_read_x(x_ref, pending), g_ref[...], sc_ref[...], sh_ref[...]).astype(BF16)
    p = _dot(h, w_ref[...])
    nq = GLA_H * GLA_DK
    o_ref[:, 0:nq] = p[:, 0:nq] * (GLA_DK ** -0.5)
    o_ref[:, nq:nq + nq + GLA_H * GLA_DV] = p[:, nq:nq + nq + GLA_H * GLA_DV]
    o_ref[:, 2 * nq + GLA_H * GLA_DV:GLA_PROJ] = _silu(p[:, 2 * nq + GLA_H * GLA_DV:GLA_PROJ])
    low = _dot(h, wg1_ref[...]).astype(BF16)
    gk = _dot(low, wg2_ref[...]) + bg_ref[...]
    o_ref[:, GLA_PROJ:GLA_COLS] = _log_sigmoid(gk) / GLA_NORMALIZER


def _gla_proj(x, mod, layer, norm_g, w_cat, wg1, wg2, bg, pend=None):
    p_specs, p_args = _pending_specs(pend, mod, TM, lambda t: (t, 0), _row_tm)
    return pl.pallas_call(
        _gla_proj_kernel,
        grid=(N_TOK // TM,),
        in_specs=[
            pl.BlockSpec((TM, D), lambda t: (t, 0)),
            _vec_spec(D),
            _mod_spec(layer, 1, _row_tm),
            _mod_spec(layer, 0, _row_tm),
            _full_spec((D, GLA_PROJ)),
            _full_spec((D, GK1_PAD)),
            _full_spec((GK1_PAD, 2 * GLA_H * GLA_DK)),
            _vec_spec(2 * GLA_H * GLA_DK),
        ] + p_specs,
        out_specs=pl.BlockSpec((TM, GLA_COLS), lambda t: (t, 0)),
        out_shape=jax.ShapeDtypeStruct((N_TOK, GLA_COLS), F32),
        compiler_params=_cparams(1),
        name="gla_proj",
    )(x, norm_g.reshape(1, D), mod, mod, w_cat, wg1, wg2, bg, *p_args)


def _gla_core_kernel(*refs, L, has_s0, hps, n_proj):
    q_ref, k_ref, v_ref, g_ref, gkf_ref, gkb_ref, tri_ref, ng_ref = refs[:8]
    rest = list(refs[8:])
    s0_ref = rest.pop(0) if has_s0 else None
    proj_refs = [rest.pop(0) for _ in range(n_proj)]
    o_ref, sf_ref, acc = rest[:3]
    o_gated = rest[3] if n_proj else o_ref
    C = GLA_CHUNK
    n = L // C
    ri = lax.broadcasted_iota(jnp.int32, (C, C), 0)
    ci = lax.broadcasted_iota(jnp.int32, (C, C), 1)
    nt_dims = (((1,), (1,)), ((), ()))
    tn_dims = (((0,), (0,)), ((), ()))
    whole = n <= GLA_WHOLE_SEQ_CHUNKS
    if whole:
        rl = lax.broadcasted_iota(jnp.int32, (L, L), 0)
        cl = lax.broadcasted_iota(jnp.int32, (L, L), 1)
        same_chunk = (rl >> (C.bit_length() - 1)) == (cl >> (C.bit_length() - 1))

    for hh in range(hps):
        kc = slice(hh * GLA_DK, (hh + 1) * GLA_DK)
        vc = slice(hh * GLA_DV, (hh + 1) * GLA_DV)
        for direction, gk_ref in enumerate((gkf_ref, gkb_ref)):
            keep = (ci <= ri) if direction == 0 else (ci >= ri)
            last = C - 1 if direction == 0 else 0
            gk_hi, gk_lo = _split(gk_ref[:, kc])
            gk_parts = jnp.concatenate([gk_hi, gk_lo], axis=1)
            if whole:
                b_all = _dot(tri_ref[direction], gk_parts)
                b_all = b_all[:, :GLA_DK] + b_all[:, GLA_DK:]
                qe_all = (q_ref[:, kc] * jnp.exp(b_all)).astype(BF16)
                ke_all = (k_ref[:, kc] * jnp.exp(-b_all)).astype(BF16)
                keep_all = same_chunk & ((cl <= rl) if direction == 0 else (cl >= rl))
                s_all = lax.dot_general(qe_all, ke_all, nt_dims, preferred_element_type=F32)
                s_all = jnp.where(keep_all, s_all, 0.0).astype(BF16)
                o_intra = _dot(s_all, v_ref[:, vc].astype(BF16))
            st = s0_ref[direction, hh].T if has_s0 else jnp.zeros((GLA_DV, GLA_DK), F32)
            order = range(n) if direction == 0 else range(n - 1, -1, -1)
            for c in order:
                rows = slice(c * C, (c + 1) * C)
                if whole:
                    b = b_all[rows]
                else:
                    b = _dot(tri_ref[direction], gk_parts[rows])
                    b = b[:, :GLA_DK] + b[:, GLA_DK:]
                b_last = b[last:last + 1, :]
                k = k_ref[rows, kc]
                v = v_ref[rows, vc].astype(BF16)
                kd = (k * jnp.exp(b_last - b)).astype(BF16)
                if whole:
                    qe = qe_all[rows]
                    o = o_intra[rows]
                else:
                    qe = (q_ref[rows, kc] * jnp.exp(b)).astype(BF16)
                    ke = (k * jnp.exp(-b)).astype(BF16)
                    scores = lax.dot_general(qe, ke, nt_dims, preferred_element_type=F32)
                    o = _dot(jnp.where(keep, scores, 0.0).astype(BF16), v)
                o = o + lax.dot_general(qe, st.astype(BF16), nt_dims, preferred_element_type=F32)
                if direction == 0:
                    acc[rows, vc] = o
                else:
                    acc[rows, vc] = acc[rows, vc] + o
                st = jnp.exp(b_last) * st + lax.dot_general(v, kd, tn_dims, preferred_element_type=F32)
            sf_ref[direction, hh] = st.T

        o = acc[:, vc]
        o = o * lax.rsqrt(jnp.mean(o * o, axis=-1, keepdims=True) + EPS) * ng_ref[...]
        o_gated[:, vc] = (o * g_ref[:, vc]).astype(BF16)

    if n_proj:
        w_ref, gate_ref, x_ref, *pending = proj_refs
        o_ref[...] = _read_x(x_ref, pending) + gate_ref[...] * _dot(o_gated[...], w_ref[...])


def _gla_core(proj, tri, norm_g, s0, L, n_batch, row_off, hps, out_proj=None):
    rb = row_off // L
    H = GLA_H
    nh = H // hps
    has_s0 = s0 is not None
    kb, vb = GLA_DK * hps, GLA_DV * hps
    in_specs = [
        pl.BlockSpec((L, kb), lambda b, h: (rb + b, h)),
        pl.BlockSpec((L, kb), lambda b, h: (rb + b, nh + h)),
        pl.BlockSpec((L, vb), lambda b, h: (rb + b, (2 * H * GLA_DK) // vb + h)),
        pl.BlockSpec((L, vb), lambda b, h: (rb + b, (2 * H * GLA_DK) // vb + nh + h)),
        pl.BlockSpec((L, kb), lambda b, h: (rb + b, GLA_PROJ // kb + h)),
        pl.BlockSpec((L, kb), lambda b, h: (rb + b, GLA_PROJ // kb + nh + h)),
        _full_spec(tri.shape),
        _vec_spec(GLA_DV),
    ]
    args = [proj] * 6 + [tri, norm_g.reshape(1, GLA_DV)]
    state_spec = pl.BlockSpec((None, 2, hps, GLA_DK, GLA_DV), lambda b, h: (b, 0, h, 0, 0))
    if has_s0:
        in_specs.append(state_spec)
        args.append(s0)
    first_spec = pl.BlockSpec((L, vb), lambda b, h: (b, h))
    first_shape = jax.ShapeDtypeStruct((n_batch * L, H * GLA_DV), BF16)
    scratch = [pltpu.VMEM((L, vb), F32)]
    aliases, n_proj = {}, 0
    if out_proj is not None:
        assert nh == 1, "the fused output projection needs every head in the step"
        x, w_o, mod, layer, pend = out_proj

        def row_fn(b, h):
            return 1 + b if row_off > 0 else 0

        x_spec = pl.BlockSpec((L, D), lambda b, h: (rb + b, 0))
        p_specs, p_args = _pending_specs(pend, mod, L, lambda b, h: (rb + b, 0), row_fn)
        aliases = {len(args) + 2: 0}
        in_specs += [_full_spec((H * GLA_DV, D)), _mod_spec(layer, 2, row_fn), x_spec] + p_specs
        args += [w_o, mod, x] + p_args
        n_proj = 3 + len(p_args)
        first_spec, first_shape = x_spec, jax.ShapeDtypeStruct((N_TOK, D), F32)
        scratch.append(pltpu.VMEM((L, vb), BF16))
    kern = functools.partial(_gla_core_kernel, L=L, has_s0=has_s0, hps=hps, n_proj=n_proj)
    return pl.pallas_call(
        kern,
        grid=(n_batch, nh),
        in_specs=in_specs,
        out_specs=[first_spec, state_spec],
        out_shape=[first_shape, jax.ShapeDtypeStruct((n_batch, 2, H, GLA_DK, GLA_DV), F32)],
        scratch_shapes=scratch,
        input_output_aliases=aliases,
        compiler_params=_cparams(2),
        name=f"gla_core_L{L}",
    )(*args)


def _fnet_mats(L):
    c = np.arange(FNET_C)
    ang_c = 2.0 * np.pi * np.outer(c, c) / FNET_C
    chan = np.concatenate([np.cos(ang_c), np.sin(ang_c)], axis=1) / math.sqrt(FNET_C)
    t = np.arange(L)
    ang_l = 2.0 * np.pi * np.outer(t, t) / L
    seq = np.concatenate([np.cos(ang_l), -np.sin(ang_l)], axis=1) / math.sqrt(L)
    return jnp.asarray(chan, F32).astype(BF16), jnp.asarray(seq, F32).astype(BF16)


def _fnet_kernel(x_ref, g_ref, sc_ref, sh_ref, gate_ref, chan_ref, seq_ref, w_ref, b_ref, *rest):
    *pending, o_ref = rest
    x = _read_x(x_ref, pending)
    h = _norm_mod(x, g_ref[...], sc_ref[...], sh_ref[...]).astype(BF16)
    chan = chan_ref[...]
    cos_parts, sin_parts = [], []
    for gi in range(FNET_GROUPS):
        cs = _dot(h[:, gi * FNET_C:(gi + 1) * FNET_C], chan)
        cos_parts.append(cs[:, :FNET_C])
        sin_parts.append(cs[:, FNET_C:])
    stacked = jnp.concatenate([jnp.concatenate(cos_parts, axis=1), jnp.concatenate(sin_parts, axis=1)], axis=0)
    mixed = _dot(seq_ref[...], stacked.astype(BF16))
    y = _dot(mixed.astype(BF16), w_ref[...]) + b_ref[...]
    o_ref[...] = x + gate_ref[...] * y


def _fnet(x, mod, layer, norm_g, chan, seq, w_bf16, bias, L, n_batch, row_off, pend=None):
    rb = row_off // L
    lat = row_off > 0

    def row_fn(b):
        return 1 + b if lat else 0

    p_specs, p_args = _pending_specs(pend, mod, L, lambda b: (rb + b, 0), row_fn)
    return pl.pallas_call(
        _fnet_kernel,
        grid=(n_batch,),
        in_specs=[
            pl.BlockSpec((L, D), lambda b: (rb + b, 0)),
            _vec_spec(D),
            _mod_spec(layer, 1, row_fn),
            _mod_spec(layer, 0, row_fn),
            _mod_spec(layer, 2, row_fn),
            _full_spec((FNET_C, 2 * FNET_C)),
            _full_spec((L, 2 * L)),
            _full_spec((D, D)),
            _vec_spec(D),
        ] + p_specs,
        out_specs=pl.BlockSpec((L, D), lambda b: (rb + b, 0)),
        out_shape=jax.ShapeDtypeStruct((N_TOK, D), F32),
        input_output_aliases={0: 0},
        compiler_params=_cparams(1),
        name=f"fnet_L{L}",
    )(x, norm_g.reshape(1, D), mod, mod, mod, chan, seq, w_bf16, bias.reshape(1, D), *p_args)


def _window_bounds(n, k):
    t = np.arange(n)
    lo, hi = k // 2, k - k // 2 - 1
    return np.maximum(t - lo, 0), np.minimum(t + hi + 1, n)


def _pool_mats(L, grid_rows):
    mats, inv = [], []
    for k in POOL_WINDOWS:
        if grid_rows is None:
            s, e = _window_bounds(L, k)
            idx = np.arange(L)[None, :]
            m = ((idx >= s[:, None]) & (idx < e[:, None])).astype(np.float64)
            cnt = (e - s).astype(np.float64)
        else:
            sr, er = _window_bounds(grid_rows, k)
            sc, ec = _window_bounds(GRID_W, k)
            ir = np.arange(grid_rows)[None, :]
            ic = np.arange(GRID_W)[None, :]
            mr = ((ir >= sr[:, None]) & (ir < er[:, None])).astype(np.float64)
            mc = ((ic >= sc[:, None]) & (ic < ec[:, None])).astype(np.float64)
            m = np.kron(mr, mc)
            cnt = np.kron((er - sr).astype(np.float64), (ec - sc).astype(np.float64))
        mats.append(m)
        inv.append(1.0 / cnt)
    return jnp.asarray(np.stack(mats), BF16), jnp.asarray(np.stack(inv)[:, :, None], F32)


def _pool_kernel(x_ref, g_ref, sc_ref, sh_ref, gate_ref, m_ref, ic_ref, w_ref, b_ref, ps_ref, *rest):
    *pending, o_ref = rest
    x = _read_x(x_ref, pending)
    h = _norm_mod(x, g_ref[...], sc_ref[...], sh_ref[...])
    outs = []
    for gi in range(len(POOL_WINDOWS)):
        hg = h[:, gi * POOL_G:(gi + 1) * POOL_G]
        hi, lo = _split(hg)
        m = m_ref[gi]
        mean = (_dot(m, hi) + _dot(m, lo)) * ic_ref[gi]
        outs.append(_dot((mean - hg).astype(BF16), w_ref[gi]))
    y = (jnp.concatenate(outs, axis=1) + b_ref[...]) * ps_ref[...]
    o_ref[...] = x + gate_ref[...] * y


def _pool(x, mod, layer, norm_g, mats, inv_cnt, w_bf16, bias, scale, L, n_batch, row_off, pend=None):
    rb = row_off // L
    lat = row_off > 0
    G = len(POOL_WINDOWS)

    def row_fn(b):
        return 1 + b if lat else 0

    p_specs, p_args = _pending_specs(pend, mod, L, lambda b: (rb + b, 0), row_fn)
    return pl.pallas_call(
        _pool_kernel,
        grid=(n_batch,),
        in_specs=[
            pl.BlockSpec((L, D), lambda b: (rb + b, 0)),
            _vec_spec(D),
            _mod_spec(layer, 1, row_fn),
            _mod_spec(layer, 0, row_fn),
            _mod_spec(layer, 2, row_fn),
            _full_spec((G, L, L)),
            _full_spec((G, L, 1)),
            _full_spec((G, POOL_G, POOL_G)),
            _vec_spec(D),
            _vec_spec(D),
        ] + p_specs,
        out_specs=pl.BlockSpec((L, D), lambda b: (rb + b, 0)),
        out_shape=jax.ShapeDtypeStruct((N_TOK, D), F32),
        input_output_aliases={0: 0},
        compiler_params=_cparams(1),
        name=f"pool_L{L}",
    )(x, norm_g.reshape(1, D), mod, mod, mod, mats, inv_cnt, w_bf16, bias.reshape(1, D), scale.reshape(1, D),
      *p_args)


ROUTER_PAD = 128
LANES = 128
D_EXT = D + LANES
MOE_TILE = 1024
MOE_MAX_TILES = N_TOK // MOE_TILE + MOE_GROUPS
MOE_ROWS = MOE_MAX_TILES * MOE_TILE
MOE_TILE_SHIFT = MOE_TILE.bit_length() - 1
assert 1 << MOE_TILE_SHIFT == MOE_TILE
MOE_ROW_BLOCK = 256
MOE_DMA_CHUNK = 64
MOE_CHUNK_SHIFT = MOE_DMA_CHUNK.bit_length() - 1
assert 1 << MOE_CHUNK_SHIFT == MOE_DMA_CHUNK and MOE_MAX_TILES % 2 == 0
MOE_Y_ROWS = N_TOK + 2 * MOE_DMA_CHUNK


ROUTE_ROWS = 8
TM_ROUTE = TM_BIG


def _moe_route_kernel(x_ref, g_ref, sc_ref, sh_ref, wr_ref, tri_ref, h3_ref, route_ref, cnt_ref, carry):
    t = pl.program_id(0)
    refs = (x_ref, g_ref, sc_ref, sh_ref, wr_ref, tri_ref, h3_ref, route_ref, cnt_ref, carry)
    pl.when(t < N_TOK // TM_ROUTE)(functools.partial(_moe_route_tile, t, *refs))

    @pl.when(t == N_TOK // TM_ROUTE)
    def _():
        h3_ref[...] = jnp.zeros_like(h3_ref)


def _moe_route_tile(t, x_ref, g_ref, sc_ref, sh_ref, wr_ref, tri_ref, h3_ref, route_ref, cnt_ref, carry):
    @pl.when(t == 0)
    def _():
        carry[...] = jnp.zeros_like(carry)

    h = _norm_mod(x_ref[...], g_ref[...], sc_ref[...], sh_ref[...])
    w_hi, w_lo = _split(wr_ref[...])
    h_hi, h_lo = _split(h)
    nt = (((1,), (1,)), ((), ()))
    logits = (lax.dot_general(w_hi, h_hi, nt, preferred_element_type=F32)
              + (lax.dot_general(w_hi, h_lo, nt, preferred_element_type=F32)
                 + lax.dot_general(w_lo, h_hi, nt, preferred_element_type=F32)))
    neg = jnp.float32(-jnp.inf)
    r8 = lax.broadcasted_iota(jnp.int32, (ROUTE_ROWS, TM_ROUTE), 0)
    r16 = lax.broadcasted_iota(jnp.int32, (MOE_E, TM_ROUTE), 0)
    gl = jnp.where(r8 < MOE_GROUPS, logits[MOE_E:MOE_E + ROUTE_ROWS], neg)
    g_max = jnp.max(gl, axis=0, keepdims=True)
    g_idx = jnp.min(jnp.where(gl == g_max, r8, ROUTE_ROWS), axis=0, keepdims=True)
    p_grp = 1.0 / jnp.sum(jnp.exp(gl - g_max), axis=0, keepdims=True)
    in_grp = (r16 >> 2) == g_idx
    el = jnp.where(in_grp, logits[:MOE_E], neg)
    m1 = jnp.max(el, axis=0, keepdims=True)
    i1 = jnp.min(jnp.where(el == m1, r16, MOE_E), axis=0, keepdims=True)
    z = jnp.sum(jnp.exp(el - m1), axis=0, keepdims=True)
    el2 = jnp.where(r16 == i1, neg, el)
    m2 = jnp.max(el2, axis=0, keepdims=True)
    i2 = jnp.min(jnp.where(el2 == m2, r16, MOE_E), axis=0, keepdims=True)
    p1 = 1.0 / z
    p2 = jnp.exp(m2 - m1) / z
    tot = p1 + p2
    eid = r8 + MOE_PER_GROUP * g_idx
    in4 = r8 < MOE_PER_GROUP
    cw4 = (jnp.where(in4 & (eid == i1), p_grp * (p1 / tot), 0.0)
           + jnp.where(in4 & (eid == i2), p_grp * (p2 / tot), 0.0))
    member = jnp.where(r8 == g_idx, 1.0, 0.0)
    before = _dot(member.astype(BF16), tri_ref[...]) + carry[:, 0:1]
    rank = jnp.sum(jnp.where(r8 == g_idx, before, 0.0), axis=0, keepdims=True)
    carry[...] = carry[...] + jnp.sum(member, axis=1, keepdims=True)
    cnt_ref[...] = carry[...].astype(jnp.int32)
    route_ref[...] = jnp.where(r8 == 0, g_idx, jnp.where(r8 == 1, rank.astype(jnp.int32), 0))
    h3_ref[:, :D] = h
    cw_rows = jnp.concatenate([cw4, jnp.zeros((LANES - ROUTE_ROWS, TM_ROUTE), F32)], axis=0)
    h3_ref[:, D:] = cw_rows.T


def _moe_route(x, mod, layer, norm_g, w_router_t, tri):
    nt = N_TOK // TM_ROUTE

    def tok_tile(t):
        return jnp.minimum(t, nt - 1)

    return pl.pallas_call(
        _moe_route_kernel,
        grid=(nt + 1,),
        in_specs=[
            pl.BlockSpec((TM_ROUTE, D), lambda t: (tok_tile(t), 0)),
            _vec_spec(D),
            _mod_spec(layer, 4, lambda t: _row_big(tok_tile(t))),
            _mod_spec(layer, 3, lambda t: _row_big(tok_tile(t))),
            _full_spec((ROUTER_PAD, D)),
            _full_spec((TM_ROUTE, TM_ROUTE)),
        ],
        out_specs=[
            pl.BlockSpec((TM_ROUTE, D_EXT), lambda t: (t, 0)),
            pl.BlockSpec((ROUTE_ROWS, TM_ROUTE), lambda t: (0, tok_tile(t))),
            pl.BlockSpec((ROUTE_ROWS, LANES), lambda t: (0, 0)),
        ],
        out_shape=[
            jax.ShapeDtypeStruct((N_TOK + TM_ROUTE, D_EXT), F32),
            jax.ShapeDtypeStruct((ROUTE_ROWS, N_TOK), jnp.int32),
            jax.ShapeDtypeStruct((ROUTE_ROWS, LANES), jnp.int32),
        ],
        scratch_shapes=[pltpu.VMEM((ROUTE_ROWS, LANES), F32)],
        compiler_params=_cparams(1),
        name="moe_route",
    )(x, norm_g.reshape(1, D), mod, mod, w_router_t, tri)


def _moe_invert_rows(pos_ref, lo_ref, hi_ref, src_ref):
    def mark(j, carry):
        parity = lax.shift_right_logical(j, jnp.int32(MOE_TILE_SHIFT - MOE_CHUNK_SHIFT)) & 1
        first = N_TOK + MOE_DMA_CHUNK * parity
        base = j * MOE_DMA_CHUNK
        for rr in range(MOE_DMA_CHUNK):
            src_ref[base + rr] = first + rr
        return carry

    for g in range(MOE_GROUPS + 1):
        lax.fori_loop(lo_ref[g], hi_ref[g], mark, 0)

    def place(n, carry):
        src_ref[pos_ref[n]] = n
        return carry

    lax.fori_loop(0, N_TOK, place, 0, unroll=16)


def _moe_expert_kernel(pos_ref, lo_ref, hi_ref, grp_ref, nact_ref, nchunk_ref, h_hbm, w1_ref, w3_ref, w2_ref, y_hbm,
                       xb0, xb1, ab0, ab1, src_ref, gsem, ssem):
    t = pl.program_id(0)
    k = pl.program_id(1)
    n_active = nact_ref[0]
    last = n_active - 1
    T = MOE_TILE
    CH = MOE_DMA_CHUNK
    xbufs, accs = (xb0, xb1), (ab0, ab1)

    def gather_row(tile, slot, base, rr):
        tok = src_ref[tile * T + base + rr]
        rows = xbufs[slot].at[pl.ds(base, CH), :]
        return pltpu.make_async_copy(h_hbm.at[pl.ds(tok, 1), :], rows.at[pl.ds(rr, 1), :], gsem.at[slot])

    def scatter_row(tile, slot, base, rr):
        dst = src_ref[tile * T + base + rr]
        rows = accs[slot].at[pl.ds(base, CH), :]
        return pltpu.make_async_copy(rows.at[pl.ds(rr, 1), :], y_hbm.at[pl.ds(dst, 1), :], ssem.at[slot])

    def start_rows(make, tile, slot):
        def chunk(c, carry):
            base = pl.multiple_of(c * CH, CH)
            for rr in range(CH):
                make(tile, slot, base, rr).start()
            return carry

        lax.fori_loop(0, nchunk_ref[tile], chunk, 0)

    def wait_rows(tile, slot, gather):
        def chunk(c, carry):
            if gather:
                pltpu.make_async_copy(h_hbm.at[pl.ds(0, CH), :], xbufs[slot].at[pl.ds(0, CH), :], gsem.at[slot]).wait()
            else:
                pltpu.make_async_copy(accs[slot].at[pl.ds(0, CH), :], y_hbm.at[pl.ds(0, CH), :], ssem.at[slot]).wait()
            return carry

        lax.fori_loop(0, nchunk_ref[tile], chunk, 0)

    def step(slot):
        other = 1 - slot
        xb, acc = xbufs[slot], accs[slot]

        @pl.when(k == 0)
        def _():
            if slot == 0:
                @pl.when(t == 0)
                def _():
                    xb0[...] = jnp.zeros_like(xb0)
                    xb1[...] = jnp.zeros_like(xb1)
                    ab0[...] = jnp.zeros_like(ab0)
                    ab1[...] = jnp.zeros_like(ab1)
                    dump = pltpu.make_async_copy(ab0.at[pl.ds(0, 2 * CH), :], y_hbm.at[pl.ds(N_TOK, 2 * CH), :],
                                                 ssem.at[0])
                    dump.start()
                    dump.wait()
                    _moe_invert_rows(pos_ref, lo_ref, hi_ref, src_ref)
                    start_rows(gather_row, 0, 0)

            wait_rows(t, slot, True)

            @pl.when(t < last)
            def _():
                start_rows(gather_row, t + 1, other)

            @pl.when(t >= 2)
            def _():
                wait_rows(t - 2, slot, False)

        def experts(m):
            x = xb[:m, :D].astype(BF16)
            a = _dot(x, w1_ref[...].astype(BF16))
            b = _dot(x, w3_ref[...].astype(BF16))
            lane = lax.broadcasted_iota(jnp.int32, (m, LANES), 1)
            cwk = jnp.sum(jnp.where(lane == k, xb[:m, D:], 0.0), axis=-1, keepdims=True)
            hid = (_silu(a) * b * cwk).astype(BF16)
            acc[:m] = jnp.where(k > 0, acc[:m], 0.0) + _dot(hid, w2_ref[...].astype(BF16))

        blocks = (nchunk_ref[t] * CH + MOE_ROW_BLOCK - 1) // MOE_ROW_BLOCK
        for nb in range(1, T // MOE_ROW_BLOCK + 1):
            pl.when(blocks == nb)(functools.partial(experts, nb * MOE_ROW_BLOCK))

        @pl.when(k == MOE_PER_GROUP - 1)
        def _():
            start_rows(scatter_row, t, slot)

            @pl.when(t == last)
            def _():
                wait_rows(t, slot, False)

                @pl.when(t >= 1)
                def _():
                    wait_rows(t - 1, other, False)

    for slot in (0, 1):
        pl.when((t < n_active) & (t % 2 == slot))(functools.partial(step, slot))


def _moe_experts(h_ext, pos, mark_lo, mark_hi, tile_group, n_active, n_chunk, layer, w1, w3, w2):
    T = MOE_TILE

    def w_index(t, k, pos_ref, lo_ref, hi_ref, grp_ref, nact_ref, nchunk_ref):
        last = nact_ref[0] - 1
        e = jnp.where(t <= last, grp_ref[t] * MOE_PER_GROUP + k, grp_ref[last] * MOE_PER_GROUP + MOE_PER_GROUP - 1)
        return (layer, e, 0, 0)

    grid_spec = pltpu.PrefetchScalarGridSpec(
        num_scalar_prefetch=6,
        grid=(MOE_MAX_TILES, MOE_PER_GROUP),
        in_specs=[
            pl.BlockSpec(memory_space=pl.ANY),
            pl.BlockSpec((None, None, D, MOE_HID), w_index),
            pl.BlockSpec((None, None, D, MOE_HID), w_index),
            pl.BlockSpec((None, None, MOE_HID, D), w_index),
        ],
        out_specs=pl.BlockSpec(memory_space=pl.ANY),
        scratch_shapes=[
            pltpu.VMEM((T, D_EXT), F32),
            pltpu.VMEM((T, D_EXT), F32),
            pltpu.VMEM((T, D), F32),
            pltpu.VMEM((T, D), F32),
            pltpu.SMEM((MOE_ROWS,), jnp.int32),
            pltpu.SemaphoreType.DMA((2,)),
            pltpu.SemaphoreType.DMA((2,)),
        ],
    )
    return pl.pallas_call(
        _moe_expert_kernel,
        grid_spec=grid_spec,
        out_shape=jax.ShapeDtypeStruct((MOE_Y_ROWS, D), F32),
        compiler_params=_cparams(2),
        name="moe_experts",
    )(pos, mark_lo, mark_hi, tile_group, n_active, n_chunk, h_ext, w1, w3, w2)


def _moe_combine_kernel(y_ref, gate_ref, x_ref, o_ref):
    o_ref[...] = x_ref[...] + gate_ref[...] * y_ref[...]


def _moe_combine(x, y3, mod, layer):
    return pl.pallas_call(
        _moe_combine_kernel,
        grid=(N_TOK // TM,),
        in_specs=[
            pl.BlockSpec((TM, D), lambda t: (t, 0)),
            _mod_spec(layer, 5, _row_tm),
            pl.BlockSpec((TM, D), lambda t: (t, 0)),
        ],
        out_specs=pl.BlockSpec((TM, D), lambda t: (t, 0)),
        out_shape=jax.ShapeDtypeStruct((N_TOK, D), F32),
        input_output_aliases={2: 0},
        compiler_params=_cparams(1),
        name="moe_combine",
    )(y3, mod, x)


def _moe(x, mod, layer, norm_g, w_rg, w_re, w1, w3, w2, tri, final_g=None):
    w_router_t = jnp.zeros((ROUTER_PAD, D), F32).at[:MOE_E].set(w_re.T).at[MOE_E:MOE_E + MOE_GROUPS].set(w_rg.T)
    h3, route, counts = _moe_route(x, mod, layer, norm_g, w_router_t, tri)
    cnt = counts[:MOE_GROUPS, 0]
    ntile = (cnt + MOE_TILE - 1) // MOE_TILE
    tile_end = jnp.cumsum(ntile)
    seg_start = (tile_end - ntile) * MOE_TILE
    g_idx, rank = route[0], route[1]
    pos = jnp.sum(jnp.where(g_idx[None, :] == jnp.arange(MOE_GROUPS)[:, None], seg_start[:, None], 0), axis=0) + rank
    tiles = jnp.arange(MOE_MAX_TILES, dtype=jnp.int32)
    tile_group = jnp.minimum(jnp.sum(tiles[:, None] >= tile_end[None, :], axis=1), MOE_GROUPS - 1).astype(jnp.int32)
    n_active = tile_end[-1:].astype(jnp.int32)
    seg_end = tile_end * MOE_TILE
    mark_lo = jnp.concatenate([(seg_start + cnt) // MOE_DMA_CHUNK, seg_end[-1:] // MOE_DMA_CHUNK])
    mark_hi = jnp.concatenate([seg_end // MOE_DMA_CHUNK, jnp.full((1,), MOE_ROWS // MOE_DMA_CHUNK)])
    first_tile = (tile_end - ntile)[tile_group]
    real_rows = jnp.clip(cnt[tile_group] - (tiles - first_tile) * MOE_TILE, 0, MOE_TILE)
    n_chunk = ((real_rows + MOE_DMA_CHUNK - 1) // MOE_DMA_CHUNK).astype(jnp.int32)
    y3 = _moe_experts(h3, pos.astype(jnp.int32), mark_lo.astype(jnp.int32), mark_hi.astype(jnp.int32), tile_group,
                      n_active, n_chunk, layer, w1, w3, w2)
    if final_g is None:
        return y3
    return tuple(_moe_combine_norm(x, y3, mod, layer, final_g, off, nb * L) for L, nb, off in
                 ((CTX_L, CTX_B, 0), (LAT_L, LAT_B, N_CTX)))


def _combine_norm_kernel(y_ref, gate_ref, x_ref, g_ref, o_ref):
    x = x_ref[...] + gate_ref[...] * y_ref[...]
    o_ref[...] = x * lax.rsqrt(jnp.mean(x * x, axis=-1, keepdims=True) + EPS) * g_ref[...]


def _moe_combine_norm(x, y3, mod, layer, final_g, row_off, n_rows):
    off = row_off // TM_BIG
    first_lat = N_CTX // TM_BIG

    def row_fn(t):
        g = t + off
        return jnp.where(g < first_lat, 0, 1 + (g - first_lat) // (LAT_L // TM_BIG))

    return pl.pallas_call(
        _combine_norm_kernel,
        grid=(n_rows // TM_BIG,),
        in_specs=[
            pl.BlockSpec((TM_BIG, D), lambda t: (t + off, 0)),
            _mod_spec(layer, 5, row_fn),
            pl.BlockSpec((TM_BIG, D), lambda t: (t + off, 0)),
            _vec_spec(D),
        ],
        out_specs=pl.BlockSpec((TM_BIG, D), lambda t: (t, 0)),
        out_shape=jax.ShapeDtypeStruct((n_rows, D), F32),
        compiler_params=_cparams(1),
        name="combine_final_norm",
    )(y3, mod, x, final_g.reshape(1, D))


def kernel(x_prompt, x_sample, state_gla, c, c_ctx, w_ada, b_ada, norm_g, hy_w_in, hy_b_in, hy_conv_w, hy_conv_b, hy_f_w1, hy_f_b1, hy_f_freq, hy_f_w2, hy_f_b2, hy_f_w3, hy_skip, hy_w_out, hy_b_out, gla_w_q, gla_w_k, gla_w_v, gla_w_g, gla_w_gk1, gla_w_gk2, gla_b_gk, gla_norm_g, gla_w_o, fn_w_out, fn_b_out, pool_w, pool_b, pool_scale, moe_w_rg, moe_w_re, moe_w1, moe_w3, moe_w2, final_g):
    groups = ((CTX_L, CTX_B, 0, None), (LAT_L, LAT_B, N_CTX, LAT_L // GRID_W))

    x_ctx, x_lat = x_prompt.reshape(N_CTX, D), x_sample.reshape(N_LAT, D)
    x = None
    cond =jnp.zeros((MOD_ROWS, D), F32).at[0].set(c_ctx).at[1:1 + LAT_B].set(c)
    mod = _ada_table(cond, w_ada, b_ada).reshape(DEPTH * MOD_ROWS * 6, 1, D)

    tri_tm = jnp.asarray(np.triu(np.ones((TM_ROUTE, TM_ROUTE)), 1), BF16)

    new_states = []
    pend = None
    for i in range(DEPTH):
        kind, j = i % 4, i // 4
        if kind == 0:
            if i > 0:
                x = _moe_combine(x, pend[0], mod, pend[1])
                x_ctx, x_lat = x[:N_CTX], x[N_CTX:]
            u = _hyena_in(x_ctx, x_lat, mod, i, norm_g[i, 0], hy_w_in[j].astype(BF16), hy_b_in[j], hy_conv_w[j],
                          hy_conv_b[j])
            zs = []
            for L, nb, off, _ in groups:
                fwd, ff, inv = _dft_mats(L)
                khat = _hyena_filters(L, ff, hy_f_w1[j], hy_f_b1[j], hy_f_freq[j], hy_f_w2[j], hy_f_b2[j],
                                      hy_f_w3[j])
                zs.append(_hyena_conv(u, khat, hy_skip[j], fwd, inv, L, nb, off, D if L == CTX_L else 512))
            x = _outproj_joint(x_ctx, x_lat, zs[0], zs[1], hy_w_out[j].astype(BF16), hy_b_out[j], mod, i)
        elif kind == 1:
            w_cat = jnp.concatenate([gla_w_q[j], gla_w_k[j], gla_w_v[j], gla_w_g[j]], axis=1).astype(BF16)
            nk = GLA_H * GLA_DK
            wg1 = jnp.zeros((D, GK1_PAD), F32).at[:, :GLA_RANK].set(gla_w_gk1[j, 0])
            wg1 = wg1.at[:, GLA_RANK:2 * GLA_RANK].set(gla_w_gk1[j, 1]).astype(BF16)
            wg2 = jnp.zeros((GK1_PAD, 2 * nk), F32).at[:GLA_RANK, :nk].set(gla_w_gk2[j, 0])
            wg2 = wg2.at[GLA_RANK:2 * GLA_RANK, nk:].set(gla_w_gk2[j, 1]).astype(BF16)
            proj = _gla_proj(x, mod, i, norm_g[i, 0], w_cat, wg1, wg2, gla_b_gk[j].reshape(1, 2 * nk), pend)
            lower = np.tril(np.ones((GLA_CHUNK, GLA_CHUNK)))
            w_o = gla_w_o[j].astype(BF16)
            for L, nb, off, grid_rows in groups:
                eye = np.eye(L // GLA_CHUNK if L // GLA_CHUNK <= GLA_WHOLE_SEQ_CHUNKS else 1)
                tri = jnp.asarray(np.stack([np.kron(eye, lower), np.kron(eye, lower.T)]), BF16)
                s0 = None if grid_rows is None else state_gla[:, j]
                if L == CTX_L:
                    x, s_fin = _gla_core(proj, tri, gla_norm_g[j], s0, L, nb, off, GLA_H, (x, w_o, mod, i, pend))
                else:
                    o, s_fin = _gla_core(proj, tri, gla_norm_g[j], s0, L, nb, off, 1)
                    x = _outproj(x, o, w_o, jnp.zeros((D,), F32), mod, i, off, nb * L, pend)
                if grid_rows is None:
                    new_states.append(s_fin)
        elif kind == 2:
            w_out = fn_w_out[j].astype(BF16)
            for L, nb, off, _ in groups:
                chan, seq = _fnet_mats(L)
                x = _fnet(x, mod, i, norm_g[i, 0], chan, seq, w_out, fn_b_out[j], L, nb, off, pend)
        else:
            w_pool = pool_w[j].astype(BF16)
            for L, nb, off, grid_rows in groups:
                mats, inv_cnt = _pool_mats(L, grid_rows)
                x = _pool(x, mod, i, norm_g[i, 0], mats, inv_cnt, w_pool, pool_b[j], pool_scale[j], L, nb, off,
                          pend)

        out = _moe(x, mod, i, norm_g[i, 1], moe_w_rg[i], moe_w_re[i], moe_w1, moe_w3, moe_w2, tri_tm,
                   final_g if i == DEPTH - 1 else None)
        pend = (out, i)

    y_prompt, y_sample = out
    new_state_gla = jnp.stack(new_states, axis=1)
    return (y_prompt.reshape(CTX_B, CTX_L, D), y_sample.reshape(LAT_B, LAT_L, D), new_state_gla)
```

```python
import functools
import math

import jax
import jax.numpy as jnp
import numpy as np
from jax import lax
from jax.experimental import pallas as pl
from jax.experimental.pallas import tpu as pltpu

F32 = jnp.float32
BF16 = jnp.bfloat16

D = 1024
CTX_B, CTX_L = 32, 256
LAT_B, LAT_L = 2, 1024
N_CTX = CTX_B * CTX_L
N_LAT = LAT_B * LAT_L
N_TOK = N_CTX + N_LAT
DEPTH = 4
GRID_W = 64
EPS = 1e-6

HY_BANDS = 8
HY_EMB = 1 + 2 * HY_BANDS
HY_EMB_PAD = 32
HY_HID = 64
HY_FAST_DECAY = 0.3
HY_SLOW_DECAY = 1.5
HY_DECAY_TARGET = 1e-2

GLA_H = 4
GLA_DK = 128
GLA_DV = 256
GLA_RANK = 16
GLA_NORMALIZER = 16.0
GLA_CHUNK = 64
GLA_WHOLE_SEQ_CHUNKS = 4

FNET_GROUPS = 4
FNET_C = D // FNET_GROUPS
POOL_WINDOWS = (2, 4, 8, 16)
POOL_G = D // len(POOL_WINDOWS)

MOE_GROUPS = 4
MOE_PER_GROUP = 4
MOE_E = MOE_GROUPS * MOE_PER_GROUP
MOE_HID = D // 2

MOD_ROWS = 8
TM = 512
TM_BIG = 1024
V7X_VMEM_BYTES = 64 * 1024 * 1024
VMEM_LIMIT = V7X_VMEM_BYTES - 8 * 1024 * 1024


def _cparams(n_axes):
    return pltpu.CompilerParams(dimension_semantics=("arbitrary",) * n_axes, vmem_limit_bytes=VMEM_LIMIT)


def _norm_mod(x, g, sc, sh):
    ms = jnp.mean(x * x, axis=-1, keepdims=True)
    return (x * lax.rsqrt(ms + EPS) * g) * (1.0 + sc) + sh


def _split(a):
    hi = a.astype(BF16)
    lo = (a - hi.astype(F32)).astype(BF16)
    return hi, lo


def _dot(a, b):
    return jnp.dot(a, b, preferred_element_type=F32)


def _dot_precise(a, b):
    a_hi, a_lo = _split(a)
    b_hi, b_lo = _split(b)
    return _dot(a_hi, b_hi) + (_dot(a_hi, b_lo) + _dot(a_lo, b_hi))


def _silu(x):
    return x * (1.0 / (1.0 + jnp.exp(-x)))


def _log_sigmoid(x):
    return jnp.minimum(x, 0.0) - jnp.log(1.0 + jnp.exp(-jnp.abs(x)))


def _mod_spec(layer, chunk, row_fn):
    base = layer * MOD_ROWS * 6 + chunk

    def index_map(*ids):
        return (base + row_fn(*ids) * 6, 0, 0)

    return pl.BlockSpec((None, 1, D), index_map)


def _row_tm(t, *_):
    return jnp.where(t < N_CTX // TM, 0, 1 + (t - N_CTX // TM) // (LAT_L // TM))


def _row_big(t, *_):
    return jnp.where(t < N_CTX // TM_BIG, 0, 1 + (t - N_CTX // TM_BIG) // (LAT_L // TM_BIG))


def _vec_spec(n):
    return pl.BlockSpec((1, n), lambda *ids: (0, 0))


def _full_spec(shape):
    nd = len(shape)
    return pl.BlockSpec(shape, lambda *ids: (0,) * nd)


def _ada_kernel(cond_ref, w_ref, b_ref, o_ref):
    s = _silu(cond_ref[...]).astype(BF16)
    o_ref[...] = _dot(s, w_ref[...].astype(BF16)) + b_ref[...]


def _ada_table(cond, w_ada, b_ada):
    tn = 1536
    return pl.pallas_call(
        _ada_kernel,
        grid=(DEPTH, 6 * D // tn),
        in_specs=[
            pl.BlockSpec((MOD_ROWS, D), lambda i, j: (0, 0)),
            pl.BlockSpec((None, D, tn), lambda i, j: (i, 0, j)),
            pl.BlockSpec((None, 1, tn), lambda i, j: (i, 0, j)),
        ],
        out_specs=pl.BlockSpec((None, MOD_ROWS, tn), lambda i, j: (i, 0, j)),
        out_shape=jax.ShapeDtypeStruct((DEPTH, MOD_ROWS, 6 * D), F32),
        compiler_params=_cparams(2),
        name="ada_table",
    )(cond, w_ada, b_ada.reshape(DEPTH, 1, 6 * D))


def _pending_specs(pend, mod, block_rows, row_index, row_fn):
    if pend is None:
        return [], []
    y, prev_layer = pend
    return [pl.BlockSpec((block_rows, D), row_index), _mod_spec(prev_layer, 5, row_fn)], [y, mod]


def _read_x(x_ref, pending_refs):
    x = x_ref[...]
    if pending_refs:
        y_ref, gate_ref = pending_refs
        x = x + gate_ref[...] * y_ref[...]
    return x


def _outproj_kernel(z_ref, w_ref, b_ref, gate_ref, x_ref, *rest):
    *pending, o_ref = rest
    y = _dot(z_ref[...], w_ref[...]) + b_ref[...]
    o_ref[...] = _read_x(x_ref, pending) + gate_ref[...] * y


def _outproj(x, z, w_bf16, bias, mod, layer, row_off, n_rows, pend=None):
    k = z.shape[1]
    off = row_off // TM
    first_lat = N_CTX // TM

    def row_fn(t):
        g = t + off
        return jnp.where(g < first_lat, 0, 1 + (g - first_lat) // (LAT_L // TM))

    p_specs, p_args = _pending_specs(pend, mod, TM, lambda t: (t + off, 0), row_fn)
    return pl.pallas_call(
        _outproj_kernel,
        grid=(n_rows // TM,),
        in_specs=[
            pl.BlockSpec((TM, k), lambda t: (t, 0)),
            _full_spec((k, D)),
            _vec_spec(D),
            _mod_spec(layer, 2, row_fn),
            pl.BlockSpec((TM, D), lambda t: (t + off, 0)),
        ] + p_specs,
        out_specs=pl.BlockSpec((TM, D), lambda t: (t + off, 0)),
        out_shape=jax.ShapeDtypeStruct((N_TOK, D), F32),
        input_output_aliases={4: 0},
        compiler_params=_cparams(1),
        name="outproj_residual",
    )(z, w_bf16, bias.reshape(1, D), mod, x, *p_args)


def _outproj_joint_kernel(zc_ref, zl_ref, w_ref, b_ref, gate_ref, xc_ref, xl_ref, o_ref):
    t = pl.program_id(0)
    for is_ctx, z_ref, x_ref in ((True, zc_ref, xc_ref), (False, zl_ref, xl_ref)):
        @pl.when((t < N_CTX // TM_BIG) == is_ctx)
        def _(z_ref=z_ref, x_ref=x_ref):
            y = _dot(z_ref[...], w_ref[...]) + b_ref[...]
            o_ref[...] = x_ref[...] + gate_ref[...] * y


def _outproj_joint(x_ctx, x_lat, z_ctx, z_lat, w_bf16, bias, mod, layer):
    k = z_ctx.shape[1]
    n_ctx_tiles = N_CTX // TM_BIG

    def ctx_block(t):
        return (jnp.minimum(t, n_ctx_tiles - 1), 0)

    def lat_block(t):
        return (jnp.maximum(t - n_ctx_tiles, 0), 0)

    return pl.pallas_call(
        _outproj_joint_kernel,
        grid=(N_TOK // TM_BIG,),
        in_specs=[
            pl.BlockSpec((TM_BIG, k), ctx_block),
            pl.BlockSpec((TM_BIG, k), lat_block),
            _full_spec((k, D)),
            _vec_spec(D),
            _mod_spec(layer, 2, _row_big),
            pl.BlockSpec((TM_BIG, D), ctx_block),
            pl.BlockSpec((TM_BIG, D), lat_block),
        ],
        out_specs=pl.BlockSpec((TM_BIG, D), lambda t: (t, 0)),
        out_shape=jax.ShapeDtypeStruct((N_TOK, D), F32),
        compiler_params=_cparams(1),
        name="outproj_joint",
    )(z_ctx, z_lat, w_bf16, bias.reshape(1, D), mod, x_ctx, x_lat)


def _dft_mats(L):
    n2 = 2 * L
    k = np.arange(L)[:, None].astype(np.float64)
    n = np.arange(n2)[None, :].astype(np.float64)
    ang = 2.0 * np.pi * k * n / n2
    full = np.concatenate([np.cos(ang), -np.sin(ang)], axis=0)
    full[L, :] = np.cos(np.pi * np.arange(n2))
    fwd = full[:, :L]
    bwd = np.zeros((n2, L))
    bwd[:, 1:] = full[:, n2 - np.arange(1, L)]
    t = np.arange(L)[:, None].astype(np.float64)
    kk = np.arange(L)[None, :].astype(np.float64)
    ang_i = 2.0 * np.pi * t * kk / n2
    inv_re = np.cos(ang_i) / L
    inv_re[:, 0] = 1.0 / n2
    inv_im = -np.sin(ang_i) / L
    inv_im[:, 0] = np.cos(np.pi * np.arange(L)) / n2
    inv = np.concatenate([inv_re, inv_im], axis=1)
    return tuple(jnp.asarray(m, F32).astype(BF16) for m in (fwd, np.concatenate([fwd, bwd], axis=1), inv))


def _hyena_pos_emb(L):
    pos = np.arange(L, dtype=np.float64)
    bands = np.linspace(1e-4, HY_BANDS - 1, HY_BANDS)
    ang = (2.0 * np.pi * pos / L)[:, None] * bands[None, :]
    z = np.concatenate([(pos / L)[:, None], np.cos(ang), -np.sin(ang)], axis=-1)
    zp = np.zeros((L, HY_EMB_PAD))
    zp[:, :HY_EMB] = z
    return jnp.asarray(zp, F32)


def _hyena_filter_kernel(z_ref, w1_ref, b1_ref, fr_ref, w2_ref, b2_ref, w3f_ref, w3b_ref, ff_ref, o_ref, *, L, tn):
    j = pl.program_id(1)
    fr = fr_ref[...]
    f = jnp.sin(fr * (_dot_precise(z_ref[...], w1_ref[...]) + b1_ref[...]))
    f = jnp.sin(fr * (_dot_precise(f, w2_ref[...]) + b2_ref[...]))
    t_lin = lax.broadcasted_iota(jnp.int32, (L, tn), 0).astype(F32) / float(L - 1)
    ch = (lax.broadcasted_iota(jnp.int32, (L, tn), 1) + j * tn).astype(F32)
    max_decay = math.log(HY_DECAY_TARGET) / HY_FAST_DECAY
    min_decay = math.log(HY_DECAY_TARGET) / HY_SLOW_DECAY
    deltas = min_decay + ch * ((max_decay - min_decay) / float(D - 1))
    window = jnp.exp(-t_lin * jnp.abs(deltas))
    kf = _dot_precise(f, w3f_ref[...]) * window
    kb = _dot_precise(f, w3b_ref[...]) * window
    taps = jnp.concatenate([kf, kb], axis=0).astype(BF16)
    o_ref[...] = _dot(ff_ref[...], taps)


def _hyena_filters(L, ff, f_w1, f_b1, f_freq, f_w2, f_b2, f_w3):
    tn = 512
    nj = D // tn
    w1p = jnp.zeros((HY_EMB_PAD, HY_HID), F32).at[:HY_EMB].set(f_w1)
    kern = functools.partial(_hyena_filter_kernel, L=L, tn=tn)
    return pl.pallas_call(
        kern,
        grid=(2, nj),
        in_specs=[
            _full_spec((L, HY_EMB_PAD)),
            _full_spec((HY_EMB_PAD, HY_HID)),
            _vec_spec(HY_HID),
            _vec_spec(HY_HID),
            _full_spec((HY_HID, HY_HID)),
            _vec_spec(HY_HID),
            pl.BlockSpec((HY_HID, tn), lambda o, j: (0, o * nj + j)),
            pl.BlockSpec((HY_HID, tn), lambda o, j: (0, (2 + o) * nj + j)),
            _full_spec((2 * L, 2 * L)),
        ],
        out_specs=pl.BlockSpec((None, 2 * L, tn), lambda o, j: (o, 0, j)),
        out_shape=jax.ShapeDtypeStruct((2, 2 * L, D), F32),
        compiler_params=_cparams(2),
        name=f"hyena_filters_L{L}",
    )(_hyena_pos_emb(L), w1p, f_b1.reshape(1, -1), f_freq.reshape(1, -1), f_w2, f_b2.reshape(1, -1),
      f_w3, f_w3, ff)


def _hyena_in_kernel(xc_ref, xl_ref, g_ref, sc_ref, sh_ref, w_ref, b_ref, cw_ref, cb_ref, o_ref, h_scr):
    t = pl.program_id(0)
    first = pl.program_id(1) == 0

    for is_ctx, x_ref in ((True, xc_ref), (False, xl_ref)):
        @pl.when(first & ((t < N_CTX // TM_BIG) == is_ctx))
        def _(x_ref=x_ref):
            h_scr[...] = _norm_mod(x_ref[...], g_ref[...], sc_ref[...], sh_ref[...]).astype(BF16)

    u = _dot(h_scr[...], w_ref[...]) + b_ref[...]
    cw = cw_ref[...]
    o_ref[...] = pltpu.roll(u, 1, 0) * cw[0:1] + u * cw[1:2] + pltpu.roll(u, TM_BIG - 1, 0) * cw[2:3] + cb_ref[...]
    is_ctx = (t < N_CTX // TM_BIG).astype(F32)
    for start in range(0, TM_BIG, CTX_L):
        f = 1.0 if start % LAT_L == 0 else is_ctx
        before = (start - 1) % TM_BIG
        o_ref[start:start + 1, :] = o_ref[start:start + 1, :] - f * (u[before:before + 1] * cw[0:1])
        end = start + CTX_L - 1
        g = 1.0 if (end + 1) % LAT_L == 0 else is_ctx
        after = (end + 1) % TM_BIG
        o_ref[end:end + 1, :] = o_ref[end:end + 1, :] - g * (u[after:after + 1] * cw[2:3])


def _hyena_in(x_ctx, x_lat, mod, layer, norm_g, w_in_bf16, b_in, conv_w, conv_b):
    n_ctx_tiles = N_CTX // TM_BIG
    return pl.pallas_call(
        _hyena_in_kernel,
        grid=(N_TOK // TM_BIG, 3),
        in_specs=[
            pl.BlockSpec((TM_BIG, D), lambda t, p: (jnp.minimum(t, n_ctx_tiles - 1), 0)),
            pl.BlockSpec((TM_BIG, D), lambda t, p: (jnp.maximum(t - n_ctx_tiles, 0), 0)),
            _vec_spec(D),
            _mod_spec(layer, 1, _row_big),
            _mod_spec(layer, 0, _row_big),
            pl.BlockSpec((D, D), lambda t, p: (0, p)),
            pl.BlockSpec((1, D), lambda t, p: (0, p)),
            pl.BlockSpec((3, D), lambda t, p: (0, p)),
            pl.BlockSpec((1, D), lambda t, p: (0, p)),
        ],
        out_specs=pl.BlockSpec((TM_BIG, D), lambda t, p: (t, p)),
        out_shape=jax.ShapeDtypeStruct((N_TOK, 3 * D), F32),
        scratch_shapes=[pltpu.VMEM((TM_BIG, D), BF16)],
        compiler_params=_cparams(2),
        name="hyena_in",
    )(x_ctx, x_lat, norm_g.reshape(1, D), mod, mod, w_in_bf16, b_in.reshape(1, -1), conv_w, conv_b.reshape(1, -1))


def _hyena_conv_kernel(v_ref, x1_ref, x2_ref, kh_ref, skip_ref, fwd_ref, inv_ref, o_ref, *, L):
    fwd = fwd_ref[...]
    inv = inv_ref[...]
    row0 = lax.broadcasted_iota(jnp.int32, (L, v_ref.shape[1]), 0) == 0

    def long_conv(z, order):
        zh = _dot(fwd, z.astype(BF16))
        zr, zi = zh[:L], zh[L:]
        kr, ki = kh_ref[order, :L, :], kh_ref[order, L:, :]
        pr = jnp.where(row0, zr * kr, zr * kr - zi * ki)
        pi = jnp.where(row0, zi * ki, zr * ki + zi * kr)
        prod = jnp.concatenate([pr, pi], axis=0).astype(BF16)
        return _dot(inv, prod) + z * skip_ref[order:order + 1, :]

    z = x1_ref[...] * long_conv(v_ref[...], 0)
    z = x2_ref[...] * long_conv(z, 1)
    o_ref[...] = z.astype(BF16)


def _hyena_conv(u, khat, skip, fwd, inv, L, n_batch, row_off, tn):
    nj = D // tn
    rb = row_off // L
    kern = functools.partial(_hyena_conv_kernel, L=L)
    return pl.pallas_call(
        kern,
        grid=(nj, n_batch),
        in_specs=[
            pl.BlockSpec((L, tn), lambda j, b: (rb + b, j)),
            pl.BlockSpec((L, tn), lambda j, b: (rb + b, nj + j)),
            pl.BlockSpec((L, tn), lambda j, b: (rb + b, 2 * nj + j)),
            pl.BlockSpec((2, 2 * L, tn), lambda j, b: (0, 0, j)),
            pl.BlockSpec((2, tn), lambda j, b: (0, j)),
            _full_spec((2 * L, L)),
            _full_spec((L, 2 * L)),
        ],
        out_specs=pl.BlockSpec((L, tn), lambda j, b: (b, j)),
        out_shape=jax.ShapeDtypeStruct((n_batch * L, D), BF16),
        compiler_params=_cparams(2),
        name=f"hyena_conv_L{L}",
    )(u, u, u, khat, skip, fwd, inv)


GLA_PROJ = 2 * GLA_H * GLA_DK + 2 * GLA_H * GLA_DV
GLA_COLS = GLA_PROJ + 2 * GLA_H * GLA_DK
GK1_PAD = 128


def _gla_proj_kernel(x_ref, g_ref, sc_ref, sh_ref, w_ref, wg1_ref, wg2_ref, bg_ref, *rest):
    *pending, o_ref = rest
    h = _norm_mod(_read_x(x_ref, pending), g_ref[...], sc_ref[...], sh_ref[...]).astype(BF16)
    p = _dot(h, w_ref[...])
    nq = GLA_H * GLA_DK
    o_ref[:, 0:nq] = p[:, 0:nq] * (GLA_DK ** -0.5)
    o_ref[:, nq:nq + nq + GLA_H * GLA_DV] = p[:, nq:nq + nq + GLA_H * GLA_DV]
    o_ref[:, 2 * nq + GLA_H * GLA_DV:GLA_PROJ] = _silu(p[:, 2 * nq + GLA_H * GLA_DV:GLA_PROJ])
    low = _dot(h, wg1_ref[...]).astype(BF16)
    gk = _dot(low, wg2_ref[...]) + bg_ref[...]
    o_ref[:, GLA_PROJ:GLA_COLS] = _log_sigmoid(gk) / GLA_NORMALIZER


def _gla_proj(x, mod, layer, norm_g, w_cat, wg1, wg2, bg, pend=None):
    p_specs, p_args = _pending_specs(pend, mod, TM, lambda t: (t, 0), _row_tm)
    return pl.pallas_call(
        _gla_proj_kernel,
        grid=(N_TOK // TM,),
        in_specs=[
            pl.BlockSpec((TM, D), lambda t: (t, 0)),
            _vec_spec(D),
            _mod_spec(layer, 1, _row_tm),
            _mod_spec(layer, 0, _row_tm),
            _full_spec((D, GLA_PROJ)),
            _full_spec((D, GK1_PAD)),
            _full_spec((GK1_PAD, 2 * GLA_H * GLA_DK)),
            _vec_spec(2 * GLA_H * GLA_DK),
        ] + p_specs,
        out_specs=pl.BlockSpec((TM, GLA_COLS), lambda t: (t, 0)),
        out_shape=jax.ShapeDtypeStruct((N_TOK, GLA_COLS), F32),
        compiler_params=_cparams(1),
        name="gla_proj",
    )(x, norm_g.reshape(1, D), mod, mod, w_cat, wg1, wg2, bg, *p_args)


def _gla_core_kernel(*refs, L, has_s0, hps, n_proj, spb):
    q_ref, k_ref, v_ref, g_ref, gkf_ref, gkb_ref, tri_ref, ng_ref = refs[:8]
    rest = list(refs[8:])
    s0_ref = rest.pop(0) if has_s0 else None
    proj_refs = [rest.pop(0) for _ in range(n_proj)]
    o_ref, sf_all, acc_all = rest[:3]
    og_all = rest[3] if n_proj else o_ref
    row_refs = (q_ref, k_ref, v_ref, g_ref, gkf_ref, gkb_ref, acc_all, og_all)
    s0_all = s0_ref
    C = GLA_CHUNK
    n = L // C
    ri = lax.broadcasted_iota(jnp.int32, (C, C), 0)
    ci = lax.broadcasted_iota(jnp.int32, (C, C), 1)
    nt_dims = (((1,), (1,)), ((), ()))
    tn_dims = (((0,), (0,)), ((), ()))
    whole = n <= GLA_WHOLE_SEQ_CHUNKS
    if whole:
        rl = lax.broadcasted_iota(jnp.int32, (L, L), 0)
        cl = lax.broadcasted_iota(jnp.int32, (L, L), 1)
        same_chunk = (rl >> (C.bit_length() - 1)) == (cl >> (C.bit_length() - 1))

    for bb, hh in ((bb, hh) for bb in range(spb) for hh in range(hps)):
        q_ref, k_ref, v_ref, g_ref, gkf_ref, gkb_ref, acc, o_gated = (r.at[pl.ds(bb * L, L), :] for r in row_refs)
        s0_ref = s0_all.at[bb] if has_s0 else None
        sf_ref = sf_all.at[bb]
        kc = slice(hh * GLA_DK, (hh + 1) * GLA_DK)
        vc = slice(hh * GLA_DV, (hh + 1) * GLA_DV)
        for direction, gk_ref in enumerate((gkf_ref, gkb_ref)):
            keep = (ci <= ri) if direction == 0 else (ci >= ri)
            last = C - 1 if direction == 0 else 0
            gk_hi, gk_lo = _split(gk_ref[:, kc])
            gk_parts = jnp.concatenate([gk_hi, gk_lo], axis=1)
            if whole:
                b_all = _dot(tri_ref[direction], gk_parts)
                b_all = b_all[:, :GLA_DK] + b_all[:, GLA_DK:]
                qe_all = (q_ref[:, kc] * jnp.exp(b_all)).astype(BF16)
                ke_all = (k_ref[:, kc] * jnp.exp(-b_all)).astype(BF16)
                keep_all = same_chunk & ((cl <= rl) if direction == 0 else (cl >= rl))
                s_all = lax.dot_general(qe_all, ke_all, nt_dims, preferred_element_type=F32)
                s_all = jnp.where(keep_all, s_all, 0.0).astype(BF16)
                o_intra = _dot(s_all, v_ref[:, vc].astype(BF16))
            st = s0_ref[direction, hh].T if has_s0 else jnp.zeros((GLA_DV, GLA_DK), F32)
            order = range(n) if direction == 0 else range(n - 1, -1, -1)
            for c in order:
                rows = slice(c * C, (c + 1) * C)
                if whole:
                    b = b_all[rows]
                else:
                    b = _dot(tri_ref[direction], gk_parts[rows])
                    b = b[:, :GLA_DK] + b[:, GLA_DK:]
                b_last = b[last:last + 1, :]
                k = k_ref[rows, kc]
                v = v_ref[rows, vc].astype(BF16)
                kd = (k * jnp.exp(b_last - b)).astype(BF16)
                if whole:
                    qe = qe_all[rows]
                    o = o_intra[rows]
                else:
                    qe = (q_ref[rows, kc] * jnp.exp(b)).astype(BF16)
                    ke = (k * jnp.exp(-b)).astype(BF16)
                    scores = lax.dot_general(qe, ke, nt_dims, preferred_element_type=F32)
                    o = _dot(jnp.where(keep, scores, 0.0).astype(BF16), v)
                o = o + lax.dot_general(qe, st.astype(BF16), nt_dims, preferred_element_type=F32)
                if direction == 0:
                    acc[rows, vc] = o
                else:
                    acc[rows, vc] = acc[rows, vc] + o
                st = jnp.exp(b_last) * st + lax.dot_general(v, kd, tn_dims, preferred_element_type=F32)
            sf_ref[direction, hh] = st.T

        o = acc[:, vc]
        o = o * lax.rsqrt(jnp.mean(o * o, axis=-1, keepdims=True) + EPS) * ng_ref[...]
        o_gated[:, vc] = (o * g_ref[:, vc]).astype(BF16)

    if n_proj:
        w_ref, gate_ref, x_ref, *pending = proj_refs
        o_ref[...] = _read_x(x_ref, pending) + gate_ref[...] * _dot(og_all[...], w_ref[...])


def _gla_core(proj, tri, norm_g, s0, L, n_batch, row_off, hps, out_proj=None, spb=1):
    lb = L * spb
    assert spb == 1 or row_off == 0, "several sequences per step share one modulation row"
    rb = row_off // lb
    H = GLA_H
    nh = H // hps
    has_s0 = s0 is not None
    kb, vb = GLA_DK * hps, GLA_DV * hps
    in_specs = [
        pl.BlockSpec((lb, kb), lambda b, h: (rb + b, h)),
        pl.BlockSpec((lb, kb), lambda b, h: (rb + b, nh + h)),
        pl.BlockSpec((lb, vb), lambda b, h: (rb + b, (2 * H * GLA_DK) // vb + h)),
        pl.BlockSpec((lb, vb), lambda b, h: (rb + b, (2 * H * GLA_DK) // vb + nh + h)),
        pl.BlockSpec((lb, kb), lambda b, h: (rb + b, GLA_PROJ // kb + h)),
        pl.BlockSpec((lb, kb), lambda b, h: (rb + b, GLA_PROJ // kb + nh + h)),
        _full_spec(tri.shape),
        _vec_spec(GLA_DV),
    ]
    args = [proj] * 6 + [tri, norm_g.reshape(1, GLA_DV)]
    state_spec = pl.BlockSpec((spb, 2, hps, GLA_DK, GLA_DV), lambda b, h: (b, 0, h, 0, 0))
    if has_s0:
        in_specs.append(state_spec)
        args.append(s0)
    first_spec = pl.BlockSpec((lb, vb), lambda b, h: (b, h))
    first_shape = jax.ShapeDtypeStruct((n_batch * L, H * GLA_DV), BF16)
    scratch = [pltpu.VMEM((lb, vb), F32)]
    aliases, n_proj = {}, 0
    if out_proj is not None:
        assert nh == 1, "the fused output projection needs every head in the step"
        x, w_o, mod, layer, pend = out_proj

        def row_fn(b, h):
            return 1 + b if row_off > 0 else 0

        x_spec = pl.BlockSpec((lb, D), lambda b, h: (rb + b, 0))
        p_specs, p_args = _pending_specs(pend, mod, lb, lambda b, h: (rb + b, 0), row_fn)
        aliases = {len(args) + 2: 0}
        in_specs += [_full_spec((H * GLA_DV, D)), _mod_spec(layer, 2, row_fn), x_spec] + p_specs
        args += [w_o, mod, x] + p_args
        n_proj = 3 + len(p_args)
        first_spec, first_shape = x_spec, jax.ShapeDtypeStruct((N_TOK, D), F32)
        scratch.append(pltpu.VMEM((lb, vb), BF16))
    kern = functools.partial(_gla_core_kernel, L=L, has_s0=has_s0, hps=hps, n_proj=n_proj, spb=spb)
    return pl.pallas_call(
        kern,
        grid=(n_batch // spb, nh),
        in_specs=in_specs,
        out_specs=[first_spec, state_spec],
        out_shape=[first_shape, jax.ShapeDtypeStruct((n_batch, 2, H, GLA_DK, GLA_DV), F32)],
        scratch_shapes=scratch,
        input_output_aliases=aliases,
        compiler_params=_cparams(2),
        name=f"gla_core_L{L}",
    )(*args)


def _fnet_mats(L):
    c = np.arange(FNET_C)
    ang_c = 2.0 * np.pi * np.outer(c, c) / FNET_C
    chan = np.concatenate([np.cos(ang_c), np.sin(ang_c)], axis=1) / math.sqrt(FNET_C)
    t = np.arange(L)
    ang_l = 2.0 * np.pi * np.outer(t, t) / L
    seq = np.concatenate([np.cos(ang_l), -np.sin(ang_l)], axis=1) / math.sqrt(L)
    return jnp.asarray(chan, F32).astype(BF16), jnp.asarray(seq, F32).astype(BF16)


def _fnet_kernel(x_ref, g_ref, sc_ref, sh_ref, gate_ref, chan_ref, seq_ref, w_ref, b_ref, *rest):
    *pending, o_ref = rest
    x = _read_x(x_ref, pending)
    h = _norm_mod(x, g_ref[...], sc_ref[...], sh_ref[...]).astype(BF16)
    chan = chan_ref[...]
    cos_parts, sin_parts = [], []
    for gi in range(FNET_GROUPS):
        cs = _dot(h[:, gi * FNET_C:(gi + 1) * FNET_C], chan)
        cos_parts.append(cs[:, :FNET_C])
        sin_parts.append(cs[:, FNET_C:])
    stacked = jnp.concatenate([jnp.concatenate(cos_parts, axis=1), jnp.concatenate(sin_parts, axis=1)], axis=0)
    mixed = _dot(seq_ref[...], stacked.astype(BF16))
    y = _dot(mixed.astype(BF16), w_ref[...]) + b_ref[...]
    o_ref[...] = x + gate_ref[...] * y


def _fnet(x, mod, layer, norm_g, chan, seq, w_bf16, bias, L, n_batch, row_off, pend=None):
    rb = row_off // L
    lat = row_off > 0

    def row_fn(b):
        return 1 + b if lat else 0

    p_specs, p_args = _pending_specs(pend, mod, L, lambda b: (rb + b, 0), row_fn)
    return pl.pallas_call(
        _fnet_kernel,
        grid=(n_batch,),
        in_specs=[
            pl.BlockSpec((L, D), lambda b: (rb + b, 0)),
            _vec_spec(D),
            _mod_spec(layer, 1, row_fn),
            _mod_spec(layer, 0, row_fn),
            _mod_spec(layer, 2, row_fn),
            _full_spec((FNET_C, 2 * FNET_C)),
            _full_spec((L, 2 * L)),
            _full_spec((D, D)),
            _vec_spec(D),
        ] + p_specs,
        out_specs=pl.BlockSpec((L, D), lambda b: (rb + b, 0)),
        out_shape=jax.ShapeDtypeStruct((N_TOK, D), F32),
        input_output_aliases={0: 0},
        compiler_params=_cparams(1),
        name=f"fnet_L{L}",
    )(x, norm_g.reshape(1, D), mod, mod, mod, chan, seq, w_bf16, bias.reshape(1, D), *p_args)


def _window_bounds(n, k):
    t = np.arange(n)
    lo, hi = k // 2, k - k // 2 - 1
    return np.maximum(t - lo, 0), np.minimum(t + hi + 1, n)


def _pool_mats(L, grid_rows):
    mats, inv = [], []
    for k in POOL_WINDOWS:
        if grid_rows is None:
            s, e = _window_bounds(L, k)
            idx = np.arange(L)[None, :]
            m = ((idx >= s[:, None]) & (idx < e[:, None])).astype(np.float64)
            cnt = (e - s).astype(np.float64)
        else:
            sr, er = _window_bounds(grid_rows, k)
            sc, ec = _window_bounds(GRID_W, k)
            ir = np.arange(grid_rows)[None, :]
            ic = np.arange(GRID_W)[None, :]
            mr = ((ir >= sr[:, None]) & (ir < er[:, None])).astype(np.float64)
            mc = ((ic >= sc[:, None]) & (ic < ec[:, None])).astype(np.float64)
            m = np.kron(mr, mc)
            cnt = np.kron((er - sr).astype(np.float64), (ec - sc).astype(np.float64))
        mats.append(m)
        inv.append(1.0 / cnt)
    return jnp.asarray(np.stack(mats), BF16), jnp.asarray(np.stack(inv)[:, :, None], F32)


def _pool_kernel(x_ref, g_ref, sc_ref, sh_ref, gate_ref, m_ref, ic_ref, w_ref, b_ref, ps_ref, *rest):
    *pending, o_ref = rest
    x = _read_x(x_ref, pending)
    h = _norm_mod(x, g_ref[...], sc_ref[...], sh_ref[...])
    outs = []
    for gi in range(len(POOL_WINDOWS)):
        hg = h[:, gi * POOL_G:(gi + 1) * POOL_G]
        hi, lo = _split(hg)
        m = m_ref[gi]
        mean = (_dot(m, hi) + _dot(m, lo)) * ic_ref[gi]
        outs.append(_dot((mean - hg).astype(BF16), w_ref[gi]))
    y = (jnp.concatenate(outs, axis=1) + b_ref[...]) * ps_ref[...]
    o_ref[...] = x + gate_ref[...] * y


def _pool(x, mod, layer, norm_g, mats, inv_cnt, w_bf16, bias, scale, L, n_batch, row_off, pend=None):
    rb = row_off // L
    lat = row_off > 0
    G = len(POOL_WINDOWS)

    def row_fn(b):
        return 1 + b if lat else 0

    p_specs, p_args = _pending_specs(pend, mod, L, lambda b: (rb + b, 0), row_fn)
    return pl.pallas_call(
        _pool_kernel,
        grid=(n_batch,),
        in_specs=[
            pl.BlockSpec((L, D), lambda b: (rb + b, 0)),
            _vec_spec(D),
            _mod_spec(layer, 1, row_fn),
            _mod_spec(layer, 0, row_fn),
            _mod_spec(layer, 2, row_fn),
            _full_spec((G, L, L)),
            _full_spec((G, L, 1)),
            _full_spec((G, POOL_G, POOL_G)),
            _vec_spec(D),
            _vec_spec(D),
        ] + p_specs,
        out_specs=pl.BlockSpec((L, D), lambda b: (rb + b, 0)),
        out_shape=jax.ShapeDtypeStruct((N_TOK, D), F32),
        input_output_aliases={0: 0},
        compiler_params=_cparams(1),
        name=f"pool_L{L}",
    )(x, norm_g.reshape(1, D), mod, mod, mod, mats, inv_cnt, w_bf16, bias.reshape(1, D), scale.reshape(1, D),
      *p_args)


ROUTER_PAD = 128
LANES = 128
D_EXT = D + LANES
MOE_TILE = 1024
MOE_MAX_TILES = N_TOK // MOE_TILE + MOE_GROUPS
MOE_ROWS = MOE_MAX_TILES * MOE_TILE
MOE_TILE_SHIFT = MOE_TILE.bit_length() - 1
assert 1 << MOE_TILE_SHIFT == MOE_TILE
MOE_ROW_BLOCK = 256
MOE_DMA_CHUNK = 64
MOE_CHUNK_SHIFT = MOE_DMA_CHUNK.bit_length() - 1
assert 1 << MOE_CHUNK_SHIFT == MOE_DMA_CHUNK and MOE_MAX_TILES % 2 == 0
MOE_Y_ROWS = N_TOK + 2 * MOE_DMA_CHUNK


ROUTE_ROWS = 8
TM_ROUTE = TM_BIG


def _moe_route_kernel(x_ref, g_ref, sc_ref, sh_ref, wr_ref, tri_ref, h3_ref, route_ref, cnt_ref, carry):
    t = pl.program_id(0)
    refs = (x_ref, g_ref, sc_ref, sh_ref, wr_ref, tri_ref, h3_ref, route_ref, cnt_ref, carry)
    pl.when(t < N_TOK // TM_ROUTE)(functools.partial(_moe_route_tile, t, *refs))

    @pl.when(t == N_TOK // TM_ROUTE)
    def _():
        h3_ref[...] = jnp.zeros_like(h3_ref)


def _moe_route_tile(t, x_ref, g_ref, sc_ref, sh_ref, wr_ref, tri_ref, h3_ref, route_ref, cnt_ref, carry):
    @pl.when(t == 0)
    def _():
        carry[...] = jnp.zeros_like(carry)

    h = _norm_mod(x_ref[...], g_ref[...], sc_ref[...], sh_ref[...])
    w_hi, w_lo = _split(wr_ref[...])
    h_hi, h_lo = _split(h)
    nt = (((1,), (1,)), ((), ()))
    logits = (lax.dot_general(w_hi, h_hi, nt, preferred_element_type=F32)
              + (lax.dot_general(w_hi, h_lo, nt, preferred_element_type=F32)
                 + lax.dot_general(w_lo, h_hi, nt, preferred_element_type=F32)))
    neg = jnp.float32(-jnp.inf)
    r8 = lax.broadcasted_iota(jnp.int32, (ROUTE_ROWS, TM_ROUTE), 0)
    r16 = lax.broadcasted_iota(jnp.int32, (MOE_E, TM_ROUTE), 0)
    gl = jnp.where(r8 < MOE_GROUPS, logits[MOE_E:MOE_E + ROUTE_ROWS], neg)
    g_max = jnp.max(gl, axis=0, keepdims=True)
    g_idx = jnp.min(jnp.where(gl == g_max, r8, ROUTE_ROWS), axis=0, keepdims=True)
    p_grp = 1.0 / jnp.sum(jnp.exp(gl - g_max), axis=0, keepdims=True)
    in_grp = (r16 >> 2) == g_idx
    el = jnp.where(in_grp, logits[:MOE_E], neg)
    m1 = jnp.max(el, axis=0, keepdims=True)
    i1 = jnp.min(jnp.where(el == m1, r16, MOE_E), axis=0, keepdims=True)
    z = jnp.sum(jnp.exp(el - m1), axis=0, keepdims=True)
    el2 = jnp.where(r16 == i1, neg, el)
    m2 = jnp.max(el2, axis=0, keepdims=True)
    i2 = jnp.min(jnp.where(el2 == m2, r16, MOE_E), axis=0, keepdims=True)
    p1 = 1.0 / z
    p2 = jnp.exp(m2 - m1) / z
    tot = p1 + p2
    eid = r8 + MOE_PER_GROUP * g_idx
    in4 = r8 < MOE_PER_GROUP
    cw4 = (jnp.where(in4 & (eid == i1), p_grp * (p1 / tot), 0.0)
           + jnp.where(in4 & (eid == i2), p_grp * (p2 / tot), 0.0))
    member = jnp.where(r8 == g_idx, 1.0, 0.0)
    before = _dot(member.astype(BF16), tri_ref[...]) + carry[:, 0:1]
    rank = jnp.sum(jnp.where(r8 == g_idx, before, 0.0), axis=0, keepdims=True)
    carry[...] = carry[...] + jnp.sum(member, axis=1, keepdims=True)
    cnt_ref[...] = carry[...].astype(jnp.int32)
    route_ref[...] = jnp.where(r8 == 0, g_idx, jnp.where(r8 == 1, rank.astype(jnp.int32), 0))
    h3_ref[:, :D] = h
    cw_rows = jnp.concatenate([cw4, jnp.zeros((LANES - ROUTE_ROWS, TM_ROUTE), F32)], axis=0)
    h3_ref[:, D:] = cw_rows.T


def _moe_route(x, mod, layer, norm_g, w_router_t, tri):
    nt = N_TOK // TM_ROUTE

    def tok_tile(t):
        return jnp.minimum(t, nt - 1)

    return pl.pallas_call(
        _moe_route_kernel,
        grid=(nt + 1,),
        in_specs=[
            pl.BlockSpec((TM_ROUTE, D), lambda t: (tok_tile(t), 0)),
            _vec_spec(D),
            _mod_spec(layer, 4, lambda t: _row_big(tok_tile(t))),
            _mod_spec(layer, 3, lambda t: _row_big(tok_tile(t))),
            _full_spec((ROUTER_PAD, D)),
            _full_spec((TM_ROUTE, TM_ROUTE)),
        ],
        out_specs=[
            pl.BlockSpec((TM_ROUTE, D_EXT), lambda t: (t, 0)),
            pl.BlockSpec((ROUTE_ROWS, TM_ROUTE), lambda t: (0, tok_tile(t))),
            pl.BlockSpec((ROUTE_ROWS, LANES), lambda t: (0, 0)),
        ],
        out_shape=[
            jax.ShapeDtypeStruct((N_TOK + TM_ROUTE, D_EXT), F32),
            jax.ShapeDtypeStruct((ROUTE_ROWS, N_TOK), jnp.int32),
            jax.ShapeDtypeStruct((ROUTE_ROWS, LANES), jnp.int32),
        ],
        scratch_shapes=[pltpu.VMEM((ROUTE_ROWS, LANES), F32)],
        compiler_params=_cparams(1),
        name="moe_route",
    )(x, norm_g.reshape(1, D), mod, mod, w_router_t, tri)


def _moe_invert_rows(pos_ref, lo_ref, hi_ref, src_ref):
    def mark(j, carry):
        parity = lax.shift_right_logical(j, jnp.int32(MOE_TILE_SHIFT - MOE_CHUNK_SHIFT)) & 1
        first = N_TOK + MOE_DMA_CHUNK * parity
        base = j * MOE_DMA_CHUNK
        for rr in range(MOE_DMA_CHUNK):
            src_ref[base + rr] = first + rr
        return carry

    for g in range(MOE_GROUPS + 1):
        lax.fori_loop(lo_ref[g], hi_ref[g], mark, 0)

    def place(n, carry):
        src_ref[pos_ref[n]] = n
        return carry

    lax.fori_loop(0, N_TOK, place, 0, unroll=16)


def _moe_expert_kernel(pos_ref, lo_ref, hi_ref, grp_ref, nact_ref, nchunk_ref, h_hbm, w1_ref, w3_ref, w2_ref, y_hbm,
                       xb0, xb1, ab0, ab1, src_ref, gsem, ssem):
    t = pl.program_id(0)
    k = pl.program_id(1)
    n_active = nact_ref[0]
    last = n_active - 1
    T = MOE_TILE
    CH = MOE_DMA_CHUNK
    xbufs, accs = (xb0, xb1), (ab0, ab1)

    def gather_row(tile, slot, base, rr):
        tok = src_ref[tile * T + base + rr]
        rows = xbufs[slot].at[pl.ds(base, CH), :]
        return pltpu.make_async_copy(h_hbm.at[pl.ds(tok, 1), :], rows.at[pl.ds(rr, 1), :], gsem.at[slot])

    def scatter_row(tile, slot, base, rr):
        dst = src_ref[tile * T + base + rr]
        rows = accs[slot].at[pl.ds(base, CH), :]
        return pltpu.make_async_copy(rows.at[pl.ds(rr, 1), :], y_hbm.at[pl.ds(dst, 1), :], ssem.at[slot])

    def start_rows(make, tile, slot):
        def chunk(c, carry):
            base = pl.multiple_of(c * CH, CH)
            for rr in range(CH):
                make(tile, slot, base, rr).start()
            return carry

        lax.fori_loop(0, nchunk_ref[tile], chunk, 0)

    def wait_rows(tile, slot, gather):
        def chunk(c, carry):
            if gather:
                pltpu.make_async_copy(h_hbm.at[pl.ds(0, CH), :], xbufs[slot].at[pl.ds(0, CH), :], gsem.at[slot]).wait()
            else:
                pltpu.make_async_copy(accs[slot].at[pl.ds(0, CH), :], y_hbm.at[pl.ds(0, CH), :], ssem.at[slot]).wait()
            return carry

        lax.fori_loop(0, nchunk_ref[tile], chunk, 0)

    def step(slot):
        other = 1 - slot
        xb, acc = xbufs[slot], accs[slot]

        @pl.when(k == 0)
        def _():
            if slot == 0:
                @pl.when(t == 0)
                def _():
                    xb0[...] = jnp.zeros_like(xb0)
                    xb1[...] = jnp.zeros_like(xb1)
                    ab0[...] = jnp.zeros_like(ab0)
                    ab1[...] = jnp.zeros_like(ab1)
                    dump = pltpu.make_async_copy(ab0.at[pl.ds(0, 2 * CH), :], y_hbm.at[pl.ds(N_TOK, 2 * CH), :],
                                                 ssem.at[0])
                    dump.start()
                    dump.wait()
                    _moe_invert_rows(pos_ref, lo_ref, hi_ref, src_ref)
                    start_rows(gather_row, 0, 0)

            wait_rows(t, slot, True)

            @pl.when(t < last)
            def _():
                start_rows(gather_row, t + 1, other)

            @pl.when(t >= 2)
            def _():
                wait_rows(t - 2, slot, False)

        def experts(m):
            x = xb[:m, :D].astype(BF16)
            a = _dot(x, w1_ref[...].astype(BF16))
            b = _dot(x, w3_ref[...].astype(BF16))
            lane = lax.broadcasted_iota(jnp.int32, (m, LANES), 1)
            cwk = jnp.sum(jnp.where(lane == k, xb[:m, D:], 0.0), axis=-1, keepdims=True)
            hid = (_silu(a) * b * cwk).astype(BF16)
            acc[:m] = jnp.where(k > 0, acc[:m], 0.0) + _dot(hid, w2_ref[...].astype(BF16))

        blocks = (nchunk_ref[t] * CH + MOE_ROW_BLOCK - 1) // MOE_ROW_BLOCK
        for nb in range(1, T // MOE_ROW_BLOCK + 1):
            pl.when(blocks == nb)(functools.partial(experts, nb * MOE_ROW_BLOCK))

        @pl.when(k == MOE_PER_GROUP - 1)
        def _():
            start_rows(scatter_row, t, slot)

            @pl.when(t == last)
            def _():
                wait_rows(t, slot, False)

                @pl.when(t >= 1)
                def _():
                    wait_rows(t - 1, other, False)

    for slot in (0, 1):
        pl.when((t < n_active) & (t % 2 == slot))(functools.partial(step, slot))


def _moe_experts(h_ext, pos, mark_lo, mark_hi, tile_group, n_active, n_chunk, layer, w1, w3, w2):
    T = MOE_TILE

    def w_index(t, k, pos_ref, lo_ref, hi_ref, grp_ref, nact_ref, nchunk_ref):
        last = nact_ref[0] - 1
        e = jnp.where(t <= last, grp_ref[t] * MOE_PER_GROUP + k, grp_ref[last] * MOE_PER_GROUP + MOE_PER_GROUP - 1)
        return (layer, e, 0, 0)

    grid_spec = pltpu.PrefetchScalarGridSpec(
        num_scalar_prefetch=6,
        grid=(MOE_MAX_TILES, MOE_PER_GROUP),
        in_specs=[
            pl.BlockSpec(memory_space=pl.ANY),
            pl.BlockSpec((None, None, D, MOE_HID), w_index),
            pl.BlockSpec((None, None, D, MOE_HID), w_index),
            pl.BlockSpec((None, None, MOE_HID, D), w_index),
        ],
        out_specs=pl.BlockSpec(memory_space=pl.ANY),
        scratch_shapes=[
            pltpu.VMEM((T, D_EXT), F32),
            pltpu.VMEM((T, D_EXT), F32),
            pltpu.VMEM((T, D), F32),
            pltpu.VMEM((T, D), F32),
            pltpu.SMEM((MOE_ROWS,), jnp.int32),
            pltpu.SemaphoreType.DMA((2,)),
            pltpu.SemaphoreType.DMA((2,)),
        ],
    )
    return pl.pallas_call(
        _moe_expert_kernel,
        grid_spec=grid_spec,
        out_shape=jax.ShapeDtypeStruct((MOE_Y_ROWS, D), F32),
        compiler_params=_cparams(2),
        name="moe_experts",
    )(pos, mark_lo, mark_hi, tile_group, n_active, n_chunk, h_ext, w1, w3, w2)


def _moe_combine_kernel(y_ref, gate_ref, x_ref, o_ref):
    o_ref[...] = x_ref[...] + gate_ref[...] * y_ref[...]


def _moe_combine(x, y3, mod, layer):
    return pl.pallas_call(
        _moe_combine_kernel,
        grid=(N_TOK // TM,),
        in_specs=[
            pl.BlockSpec((TM, D), lambda t: (t, 0)),
            _mod_spec(layer, 5, _row_tm),
            pl.BlockSpec((TM, D), lambda t: (t, 0)),
        ],
        out_specs=pl.BlockSpec((TM, D), lambda t: (t, 0)),
        out_shape=jax.ShapeDtypeStruct((N_TOK, D), F32),
        input_output_aliases={2: 0},
        compiler_params=_cparams(1),
        name="moe_combine",
    )(y3, mod, x)


def _moe(x, mod, layer, norm_g, w_rg, w_re, w1, w3, w2, tri, final_g=None):
    w_router_t = jnp.zeros((ROUTER_PAD, D), F32).at[:MOE_E].set(w_re.T).at[MOE_E:MOE_E + MOE_GROUPS].set(w_rg.T)
    h3, route, counts = _moe_route(x, mod, layer, norm_g, w_router_t, tri)
    cnt = counts[:MOE_GROUPS, 0]
    ntile = (cnt + MOE_TILE - 1) // MOE_TILE
    tile_end = jnp.cumsum(ntile)
    seg_start = (tile_end - ntile) * MOE_TILE
    g_idx, rank = route[0], route[1]
    pos = jnp.sum(jnp.where(g_idx[None, :] == jnp.arange(MOE_GROUPS)[:, None], seg_start[:, None], 0), axis=0) + rank
    tiles = jnp.arange(MOE_MAX_TILES, dtype=jnp.int32)
    tile_group = jnp.minimum(jnp.sum(tiles[:, None] >= tile_end[None, :], axis=1), MOE_GROUPS - 1).astype(jnp.int32)
    n_active = tile_end[-1:].astype(jnp.int32)
    seg_end = tile_end * MOE_TILE
    mark_lo = jnp.concatenate([(seg_start + cnt) // MOE_DMA_CHUNK, seg_end[-1:] // MOE_DMA_CHUNK])
    mark_hi = jnp.concatenate([seg_end // MOE_DMA_CHUNK, jnp.full((1,), MOE_ROWS // MOE_DMA_CHUNK)])
    first_tile = (tile_end - ntile)[tile_group]
    real_rows = jnp.clip(cnt[tile_group] - (tiles - first_tile) * MOE_TILE, 0, MOE_TILE)
    n_chunk = ((real_rows + MOE_DMA_CHUNK - 1) // MOE_DMA_CHUNK).astype(jnp.int32)
    y3 = _moe_experts(h3, pos.astype(jnp.int32), mark_lo.astype(jnp.int32), mark_hi.astype(jnp.int32), tile_group,
                      n_active, n_chunk, layer, w1, w3, w2)
    if final_g is None:
        return y3
    return tuple(_moe_combine_norm(x, y3, mod, layer, final_g, off, nb * L) for L, nb, off in
                 ((CTX_L, CTX_B, 0), (LAT_L, LAT_B, N_CTX)))


def _combine_norm_kernel(y_ref, gate_ref, x_ref, g_ref, o_ref):
    x = x_ref[...] + gate_ref[...] * y_ref[...]
    o_ref[...] = x * lax.rsqrt(jnp.mean(x * x, axis=-1, keepdims=True) + EPS) * g_ref[...]


def _moe_combine_norm(x, y3, mod, layer, final_g, row_off, n_rows):
    off = row_off // TM_BIG
    first_lat = N_CTX // TM_BIG

    def row_fn(t):
        g = t + off
        return jnp.where(g < first_lat, 0, 1 + (g - first_lat) // (LAT_L // TM_BIG))

    return pl.pallas_call(
        _combine_norm_kernel,
        grid=(n_rows // TM_BIG,),
        in_specs=[
            pl.BlockSpec((TM_BIG, D), lambda t: (t + off, 0)),
            _mod_spec(layer, 5, row_fn),
            pl.BlockSpec((TM_BIG, D), lambda t: (t + off, 0)),
            _vec_spec(D),
        ],
        out_specs=pl.BlockSpec((TM_BIG, D), lambda t: (t, 0)),
        out_shape=jax.ShapeDtypeStruct((n_rows, D), F32),
        compiler_params=_cparams(1),
        name="combine_final_norm",
    )(y3, mod, x, final_g.reshape(1, D))


def kernel(x_prompt, x_sample, state_gla, c, c_ctx, w_ada, b_ada, norm_g, hy_w_in, hy_b_in, hy_conv_w, hy_conv_b, hy_f_w1, hy_f_b1, hy_f_freq, hy_f_w2, hy_f_b2, hy_f_w3, hy_skip, hy_w_out, hy_b_out, gla_w_q, gla_w_k, gla_w_v, gla_w_g, gla_w_gk1, gla_w_gk2, gla_b_gk, gla_norm_g, gla_w_o, fn_w_out, fn_b_out, pool_w, pool_b, pool_scale, moe_w_rg, moe_w_re, moe_w1, moe_w3, moe_w2, final_g):
    groups = ((CTX_L, CTX_B, 0, None), (LAT_L, LAT_B, N_CTX, LAT_L // GRID_W))

    x_ctx, x_lat = x_prompt.reshape(N_CTX, D), x_sample.reshape(N_LAT, D)
    x = None
    cond =jnp.zeros((MOD_ROWS, D), F32).at[0].set(c_ctx).at[1:1 + LAT_B].set(c)
    mod = _ada_table(cond, w_ada, b_ada).reshape(DEPTH * MOD_ROWS * 6, 1, D)

    tri_tm = jnp.asarray(np.triu(np.ones((TM_ROUTE, TM_ROUTE)), 1), BF16)

    new_states = []
    pend = None
    for i in range(DEPTH):
        kind, j = i % 4, i // 4
        if kind == 0:
            if i > 0:
                x = _moe_combine(x, pend[0], mod, pend[1])
                x_ctx, x_lat = x[:N_CTX], x[N_CTX:]
            u = _hyena_in(x_ctx, x_lat, mod, i, norm_g[i, 0], hy_w_in[j].astype(BF16), hy_b_in[j], hy_conv_w[j],
                          hy_conv_b[j])
            zs = []
            for L, nb, off, _ in groups:
                fwd, ff, inv = _dft_mats(L)
                khat = _hyena_filters(L, ff, hy_f_w1[j], hy_f_b1[j], hy_f_freq[j], hy_f_w2[j], hy_f_b2[j],
                                      hy_f_w3[j])
                zs.append(_hyena_conv(u, khat, hy_skip[j], fwd, inv, L, nb, off, D if L == CTX_L else 512))
            x = _outproj_joint(x_ctx, x_lat, zs[0], zs[1], hy_w_out[j].astype(BF16), hy_b_out[j], mod, i)
        elif kind == 1:
            w_cat = jnp.concatenate([gla_w_q[j], gla_w_k[j], gla_w_v[j], gla_w_g[j]], axis=1).astype(BF16)
            nk = GLA_H * GLA_DK
            wg1 = jnp.zeros((D, GK1_PAD), F32).at[:, :GLA_RANK].set(gla_w_gk1[j, 0])
            wg1 = wg1.at[:, GLA_RANK:2 * GLA_RANK].set(gla_w_gk1[j, 1]).astype(BF16)
            wg2 = jnp.zeros((GK1_PAD, 2 * nk), F32).at[:GLA_RANK, :nk].set(gla_w_gk2[j, 0])
            wg2 = wg2.at[GLA_RANK:2 * GLA_RANK, nk:].set(gla_w_gk2[j, 1]).astype(BF16)
            proj = _gla_proj(x, mod, i, norm_g[i, 0], w_cat, wg1, wg2, gla_b_gk[j].reshape(1, 2 * nk), pend)
            lower = np.tril(np.ones((GLA_CHUNK, GLA_CHUNK)))
            w_o = gla_w_o[j].astype(BF16)
            for L, nb, off, grid_rows in groups:
                eye = np.eye(L // GLA_CHUNK if L // GLA_CHUNK <= GLA_WHOLE_SEQ_CHUNKS else 1)
                tri = jnp.asarray(np.stack([np.kron(eye, lower), np.kron(eye, lower.T)]), BF16)
                s0 = None if grid_rows is None else state_gla[:, j]
                if L == CTX_L:
                    x, s_fin = _gla_core(proj, tri, gla_norm_g[j], s0, L, nb, off, GLA_H, (x, w_o, mod, i, pend),
                                         spb=2)
                else:
                    o, s_fin = _gla_core(proj, tri, gla_norm_g[j], s0, L, nb, off, 1)
                    x = _outproj(x, o, w_o, jnp.zeros((D,), F32), mod, i, off, nb * L, pend)
                if grid_rows is None:
                    new_states.append(s_fin)
        elif kind == 2:
            w_out = fn_w_out[j].astype(BF16)
            for L, nb, off, _ in groups:
                chan, seq = _fnet_mats(L)
                x = _fnet(x, mod, i, norm_g[i, 0], chan, seq, w_out, fn_b_out[j], L, nb, off, pend)
        else:
            w_pool = pool_w[j].astype(BF16)
            for L, nb, off, grid_rows in groups:
                mats, inv_cnt = _pool_mats(L, grid_rows)
                x = _pool(x, mod, i, norm_g[i, 0], mats, inv_cnt, w_pool, pool_b[j], pool_scale[j], L, nb, off,
                          pend)

        out = _moe(x, mod, i, norm_g[i, 1], moe_w_rg[i], moe_w_re[i], moe_w1, moe_w3, moe_w2, tri_tm,
                   final_g if i == DEPTH - 1 else None)
        pend = (out, i)

    y_prompt, y_sample = out
    new_state_gla = jnp.stack(new_states, axis=1)
    return (y_prompt.reshape(CTX_B, CTX_L, D), y_sample.reshape(LAT_B, LAT_L, D), new_state_gla)
```
